```python
import jax
import jax.numpy as jnp
from jax import lax
import numpy as np

D_MODEL = 1024
BATCH = 32
SEQ = 2048
DEPTH = 2

CHUNK = 64
D_PLE = 256
N_EVEN = (DEPTH + 1) // 2
N_ODD = DEPTH // 2
D_FF = 2816
A_HEADS = 8
A_KV_HEADS = 2
A_GROUP = A_HEADS // A_KV_HEADS
A_HEAD_DIM = 64
A_WIDTH = A_HEADS * A_HEAD_DIM
A_KV_WIDTH = A_KV_HEADS * A_HEAD_DIM
A_WINDOW = 128
A_PREV_CHUNKS = A_WINDOW // CHUNK
B_WIDTH = 512
B_BLOCKS = 8
B_BLOCK = B_WIDTH // B_BLOCKS
B_CONV = 4
RG_C = 8.0
AB_PROJ = A_WIDTH + 2 * A_KV_WIDTH + 2 * B_WIDTH
C_HEADS = 8
C_HEAD_DIM = 128
C_WIDTH = C_HEADS * C_HEAD_DIM
C_CONV = 4
C_PROJ = 4 * C_WIDTH + 2 * C_HEADS
DN_ALPHA = (2.0 * DEPTH) ** 0.25
DN_BETA = (8.0 * DEPTH) ** -0.25
LN_EPS = 1e-5
NORM_EPS = 1e-6
NEG = -1e30

kernel_name = 'hybrid_swa_rglru_gdn_deepnorm_macaron'


def layer_norm(x, g, b):
    xf = x.astype(jnp.float32)
    mu = jnp.mean(xf, -1, keepdims=True)
    var = jnp.mean(jnp.square(xf - mu), -1, keepdims=True)
    return ((xf - mu) * lax.rsqrt(var + LN_EPS) * g + b).astype(x.dtype)


def swiglu(x, wg, wu, wd):
    return (jax.nn.silu(x @ wg) * (x @ wu)) @ wd


def causal_dwconv(x, w):
    k, s = w.shape[0], x.shape[1]
    xp = jnp.pad(x, ((0, 0), (k - 1, 0), (0, 0)))
    y = xp[:, 0:s] * w[0]
    for j in range(1, k):
        y = y + xp[:, j:j + s] * w[j]
    return y


def chunk_band(t, n_prev):
    b, s = t.shape[:2]
    nc = s // CHUNK
    pad = n_prev * CHUNK
    tp = jnp.pad(t, ((0, 0), (pad, 0), (0, 0), (0, 0)))
    return jnp.concatenate(
        [tp[:, j * CHUNK:j * CHUNK + s].reshape(b, nc, CHUNK, *t.shape[2:]) for j in range(n_prev + 1)],
        axis=2)


def alibi_slopes(n):
    return 2.0 ** (-8.0 * jnp.arange(1, n + 1, dtype=jnp.float32) / n)


def sliding_window_sink_attention(q, k, v, sinks):
    b, s = q.shape[:2]
    nc = s // CHUNK
    pad = A_PREV_CHUNKS * CHUNK
    nk = pad + CHUNK
    qb = q.reshape(b, nc, CHUNK, A_KV_HEADS, A_GROUP, A_HEAD_DIM)
    kb = chunk_band(k, A_PREV_CHUNKS)
    vb = chunk_band(v, A_PREV_CHUNKS)
    sc = jnp.einsum('bnckgd,bnskd->bnkgcs', qb, kb).astype(jnp.float32) * (A_HEAD_DIM ** -0.5)
    dist = jnp.abs(jnp.arange(CHUNK)[:, None] + pad - jnp.arange(nk)[None, :]).astype(jnp.float32)
    slopes = alibi_slopes(A_HEADS).reshape(A_KV_HEADS, A_GROUP)
    valid = (jnp.arange(nc)[:, None] * CHUNK + jnp.arange(nk)[None, :] - pad) >= 0
    sc = sc - slopes[:, :, None, None] * dist
    sc = jnp.where(valid[None, :, None, None, None, :], sc, NEG)
    sink = sinks.astype(jnp.float32).reshape(A_KV_HEADS, A_GROUP)[:, :, None]
    m = jnp.maximum(sc.max(-1), sink)
    pr = jnp.exp(sc - m[..., None])
    den = pr.sum(-1) + jnp.exp(sink - m)
    o = jnp.einsum('bnkgcs,bnskd->bnckgd', pr / den[..., None], vb.astype(jnp.float32))
    return o.reshape(b, s, A_WIDTH).astype(q.dtype)


def rg_lru(x, w_a, b_a, w_x, b_x, lam):
    xb = x.reshape(*x.shape[:2], B_BLOCKS, B_BLOCK)
    r = jax.nn.sigmoid(jnp.einsum('bshi,hij->bshj', xb, w_a).reshape(x.shape) + b_a)
    i = jax.nn.sigmoid(jnp.einsum('bshi,hij->bshj', xb, w_x).reshape(x.shape) + b_x)
    log_a = (-RG_C * r * jax.nn.softplus(-lam)).astype(jnp.float32)
    a = jnp.exp(log_a)
    u = jnp.sqrt(-jnp.expm1(2.0 * log_a)) * (i * x).astype(jnp.float32)

    def combine(c1, c2):
        a1, b1 = c1
        a2, b2 = c2
        return a1 * a2, a2 * b1 + b2

    _, h = lax.associative_scan(combine, (a, u), axis=1)
    return h.astype(x.dtype)


def mixer_ab(x, w_in, sinks, conv_w, conv_b, w_a, b_a, w_x, b_x, lam, w_out):
    b, s = x.shape[:2]
    proj = x @ w_in
    o1 = A_WIDTH
    o2 = o1 + A_KV_WIDTH
    o3 = o2 + A_KV_WIDTH
    o4 = o3 + B_WIDTH
    q = proj[..., :o1].reshape(b, s, A_HEADS, A_HEAD_DIM)
    k = proj[..., o1:o2].reshape(b, s, A_KV_HEADS, A_HEAD_DIM)
    v = proj[..., o2:o3].reshape(b, s, A_KV_HEADS, A_HEAD_DIM)
    bx = proj[..., o3:o4]
    bg = proj[..., o4:]
    ya = sliding_window_sink_attention(q, k, v, sinks)
    bx = causal_dwconv(bx, conv_w) + conv_b
    yb = rg_lru(bx, w_a, b_a, w_x, b_x, lam) * jax.nn.gelu(bg)
    return jnp.concatenate([ya, yb], axis=-1) @ w_out


def gated_delta_rule(q, k, v, g, beta):
    f32 = jnp.float32
    b, s, h, dk = q.shape
    dv = v.shape[-1]
    nc = s // CHUNK

    def to_chunks(t):
        return t.astype(f32).reshape(b, nc, CHUNK, h, -1).transpose(1, 0, 3, 2, 4)

    q = to_chunks(q) * (dk ** -0.5)
    k = to_chunks(k)
    v = to_chunks(v)
    g = g.astype(f32).reshape(b, nc, CHUNK, h).transpose(1, 0, 3, 2)
    beta = beta.astype(f32).reshape(b, nc, CHUNK, h).transpose(1, 0, 3, 2)
    gc = jnp.cumsum(g, axis=-1)
    tril = jnp.tril(jnp.ones((CHUNK, CHUNK), bool))
    strict = jnp.tril(jnp.ones((CHUNK, CHUNK), bool), -1)
    diff = gc[..., :, None] - gc[..., None, :]
    decay = jnp.where(tril, jnp.exp(jnp.where(tril, diff, 0.0)), 0.0)
    kb = k * beta[..., None]
    lmat = jnp.where(strict, jnp.einsum('nbhid,nbhjd->nbhij', kb, k) * decay, 0.0)
    amat = lmat + jnp.eye(CHUNK, dtype=f32)
    u = lax.linalg.triangular_solve(amat, v * beta[..., None], left_side=True, lower=True, unit_diagonal=True)
    w = lax.linalg.triangular_solve(amat, kb * jnp.exp(gc)[..., None], left_side=True, lower=True, unit_diagonal=True)
    attn = jnp.einsum('nbhid,nbhjd->nbhij', q, k) * decay
    qg = q * jnp.exp(gc)[..., None]
    kdec = k * jnp.exp(gc[..., -1:] - gc)[..., None]
    glast = jnp.exp(gc[..., -1])

    def step(state, xs):
        qg_n, kdec_n, w_n, u_n, attn_n, gl_n = xs
        v_new = u_n - jnp.einsum('bhcd,bhde->bhce', w_n, state)
        o = jnp.einsum('bhcd,bhde->bhce', qg_n, state) + jnp.einsum('bhij,bhje->bhie', attn_n, v_new)
        state = state * gl_n[..., None, None] + jnp.einsum('bhcd,bhce->bhde', kdec_n, v_new)
        return state, o

    s0 = jnp.zeros((b, h, dk, dv), f32)
    _, o = lax.scan(step, s0, (qg, kdec, w, u, attn, glast))
    return o.transpose(1, 0, 3, 2, 4).reshape(b, s, h, dv)


def mixer_c(x, w_in, conv_w, a_log, dt_bias, norm_g, w_out):
    b, s = x.shape[:2]
    proj = x @ w_in
    qkv = jax.nn.silu(causal_dwconv(proj[..., :3 * C_WIDTH], conv_w))
    z = proj[..., 3 * C_WIDTH:4 * C_WIDTH].reshape(b, s, C_HEADS, C_HEAD_DIM)
    b_logit = proj[..., 4 * C_WIDTH:4 * C_WIDTH + C_HEADS]
    a_in = proj[..., 4 * C_WIDTH + C_HEADS:]
    q = qkv[..., :C_WIDTH].reshape(b, s, C_HEADS, C_HEAD_DIM).astype(jnp.float32)
    k = qkv[..., C_WIDTH:2 * C_WIDTH].reshape(b, s, C_HEADS, C_HEAD_DIM).astype(jnp.float32)
    v = qkv[..., 2 * C_WIDTH:].reshape(b, s, C_HEADS, C_HEAD_DIM)
    q = q * lax.rsqrt(jnp.sum(q * q, -1, keepdims=True) + NORM_EPS)
    k = k * lax.rsqrt(jnp.sum(k * k, -1, keepdims=True) + NORM_EPS)
    beta = jax.nn.sigmoid(b_logit.astype(jnp.float32))
    g = -jnp.exp(a_log.astype(jnp.float32)) * jax.nn.softplus((a_in + dt_bias).astype(jnp.float32))
    o = gated_delta_rule(q, k, v, g, beta)
    o = o * lax.rsqrt(jnp.mean(o * o, -1, keepdims=True) + NORM_EPS) * norm_g
    o = (o * jax.nn.silu(z.astype(jnp.float32))).astype(x.dtype)
    return o.reshape(b, s, C_WIDTH) @ w_out


def _fwd_setup_inputs(seed: int = 0) -> dict:
    key = jax.random.key(seed)
    ks = iter(jax.random.split(key, 48))
    f32 = jnp.float32

    def nrm(shape, scale):
        return jax.random.normal(next(ks), shape, f32) * scale

    d = D_MODEL
    x = nrm((BATCH, SEQ, d), 1.0)
    p = nrm((DEPTH, BATCH, SEQ, D_PLE), 1.0)
    ffn1_wg = nrm((DEPTH, d, D_FF), d ** -0.5)
    ffn1_wu = nrm((DEPTH, d, D_FF), d ** -0.5)
    ffn1_wd = nrm((DEPTH, D_FF, d), DN_BETA * D_FF ** -0.5)
    ffn2_wg = nrm((DEPTH, d, D_FF), d ** -0.5)
    ffn2_wu = nrm((DEPTH, d, D_FF), d ** -0.5)
    ffn2_wd = nrm((DEPTH, D_FF, d), DN_BETA * D_FF ** -0.5)
    ln_g = 1.0 + nrm((DEPTH, 3, d), 0.02)
    ln_b = nrm((DEPTH, 3, d), 0.02)
    ple_wg = nrm((DEPTH, d, d), d ** -0.5)
    ple_bg = nrm((DEPTH, d), 0.02)
    ple_wp = nrm((DEPTH, D_PLE, d), D_PLE ** -0.5)
    ab_w_in = nrm((N_EVEN, d, AB_PROJ), d ** -0.5)
    a_sinks = nrm((N_EVEN, A_HEADS), 0.5)
    b_conv_w = nrm((N_EVEN, B_CONV, B_WIDTH), B_CONV ** -0.5)
    b_conv_b = nrm((N_EVEN, B_WIDTH), 0.02)
    b_wa = nrm((N_EVEN, B_BLOCKS, B_BLOCK, B_BLOCK), B_BLOCK ** -0.5)
    b_ba = nrm((N_EVEN, B_WIDTH), 0.02)
    b_wx = nrm((N_EVEN, B_BLOCKS, B_BLOCK, B_BLOCK), B_BLOCK ** -0.5)
    b_bx = nrm((N_EVEN, B_WIDTH), 0.02)
    a_c = jax.random.uniform(next(ks), (N_EVEN, B_WIDTH), f32, 0.9, 0.999)
    a0 = a_c ** (1.0 / RG_C)
    b_lam = jnp.log(a0) - jnp.log1p(-a0)
    ab_w_out = nrm((N_EVEN, A_WIDTH + B_WIDTH, d), DN_BETA * (A_WIDTH + B_WIDTH) ** -0.5)
    c_w_in = nrm((N_ODD, d, C_PROJ), d ** -0.5)
    c_conv_w = nrm((N_ODD, C_CONV, 3 * C_WIDTH), C_CONV ** -0.5)
    c_a_log = jnp.log(jax.random.uniform(next(ks), (N_ODD, C_HEADS), f32, 1.0, 16.0))
    dt = jnp.exp(jax.random.uniform(next(ks), (N_ODD, C_HEADS), f32, np.log(1e-3), np.log(1e-1)))
    c_dt_bias = dt + jnp.log(-jnp.expm1(-dt))
    c_norm_g = 1.0 + nrm((N_ODD, C_HEAD_DIM), 0.02)
    c_w_out = nrm((N_ODD, C_WIDTH, d), DN_BETA * C_WIDTH ** -0.5)
    return {'x': x, 'p': p,
            'ffn1_wg': ffn1_wg, 'ffn1_wu': ffn1_wu, 'ffn1_wd': ffn1_wd,
            'ffn2_wg': ffn2_wg, 'ffn2_wu': ffn2_wu, 'ffn2_wd': ffn2_wd,
            'ln_g': ln_g, 'ln_b': ln_b,
            'ple_wg': ple_wg, 'ple_bg': ple_bg, 'ple_wp': ple_wp,
            'ab_w_in': ab_w_in, 'a_sinks': a_sinks,
            'b_conv_w': b_conv_w, 'b_conv_b': b_conv_b,
            'b_wa': b_wa, 'b_ba': b_ba, 'b_wx': b_wx, 'b_bx': b_bx, 'b_lam': b_lam,
            'ab_w_out': ab_w_out,
            'c_w_in': c_w_in, 'c_conv_w': c_conv_w, 'c_a_log': c_a_log, 'c_dt_bias': c_dt_bias,
            'c_norm_g': c_norm_g, 'c_w_out': c_w_out}


def _fwd_reference(x, p, ffn1_wg, ffn1_wu, ffn1_wd, ffn2_wg, ffn2_wu, ffn2_wd, ln_g, ln_b,
              ple_wg, ple_bg, ple_wp, ab_w_in, a_sinks, b_conv_w, b_conv_b,
              b_wa, b_ba, b_wx, b_bx, b_lam, ab_w_out,
              c_w_in, c_conv_w, c_a_log, c_dt_bias, c_norm_g, c_w_out):
    for i in range(DEPTH):
        j = i // 2
        x = layer_norm(DN_ALPHA * x + 0.5 * swiglu(x, ffn1_wg[i], ffn1_wu[i], ffn1_wd[i]), ln_g[i, 0], ln_b[i, 0])
        if i % 2 == 0:
            y = mixer_ab(x, ab_w_in[j], a_sinks[j], b_conv_w[j], b_conv_b[j],
                         b_wa[j], b_ba[j], b_wx[j], b_bx[j], b_lam[j], ab_w_out[j])
        else:
            y = mixer_c(x, c_w_in[j], c_conv_w[j], c_a_log[j], c_dt_bias[j], c_norm_g[j], c_w_out[j])
        x = layer_norm(DN_ALPHA * x + y, ln_g[i, 1], ln_b[i, 1])
        x = layer_norm(DN_ALPHA * x + 0.5 * swiglu(x, ffn2_wg[i], ffn2_wu[i], ffn2_wd[i]), ln_g[i, 2], ln_b[i, 2])
        x = x + jax.nn.sigmoid(x @ ple_wg[i] + ple_bg[i]) * (p[i] @ ple_wp[i])
    return x


import jax as _jax
import jax.numpy as _jnp

TWIN_FORMAT = 'train_step'
FWD_PARAMS = ['x', 'p', 'ffn1_wg', 'ffn1_wu', 'ffn1_wd', 'ffn2_wg', 'ffn2_wu', 'ffn2_wd', 'ln_g', 'ln_b', 'ple_wg', 'ple_bg', 'ple_wp', 'ab_w_in', 'a_sinks', 'b_conv_w', 'b_conv_b', 'b_wa', 'b_ba', 'b_wx', 'b_bx', 'b_lam', 'ab_w_out', 'c_w_in', 'c_conv_w', 'c_a_log', 'c_dt_bias', 'c_norm_g', 'c_w_out']
TWIN_WEIGHTS = ['ffn1_wg', 'ffn1_wu', 'ffn1_wd', 'ffn2_wg', 'ffn2_wu', 'ffn2_wd', 'ln_g', 'ln_b', 'ple_wg', 'ple_bg', 'ple_wp', 'ab_w_in', 'a_sinks', 'b_conv_w', 'b_conv_b', 'b_wa', 'b_ba', 'b_wx', 'b_bx', 'b_lam', 'ab_w_out', 'c_w_in', 'c_conv_w', 'c_a_log', 'c_dt_bias', 'c_norm_g', 'c_w_out']
TWIN_DIFF_INPUT = 'x'
TWIN_INPUTS = ['x', 'p', 'ffn1_wg', 'ffn1_wu', 'ffn1_wd', 'ffn2_wg', 'ffn2_wu', 'ffn2_wd', 'ln_g', 'ln_b', 'ple_wg', 'ple_bg', 'ple_wp', 'ab_w_in', 'a_sinks', 'b_conv_w', 'b_conv_b', 'b_wa', 'b_ba', 'b_wx', 'b_bx', 'b_lam', 'ab_w_out', 'c_w_in', 'c_conv_w', 'c_a_log', 'c_dt_bias', 'c_norm_g', 'c_w_out', 'loss_target', 'm_ffn1_wg', 'm_ffn1_wu', 'm_ffn1_wd', 'm_ffn2_wg', 'm_ffn2_wu', 'm_ffn2_wd', 'm_ln_g', 'm_ln_b', 'm_ple_wg', 'm_ple_bg', 'm_ple_wp', 'm_ab_w_in', 'm_a_sinks', 'm_b_conv_w', 'm_b_conv_b', 'm_b_wa', 'm_b_ba', 'm_b_wx', 'm_b_bx', 'm_b_lam', 'm_ab_w_out', 'm_c_w_in', 'm_c_conv_w', 'm_c_a_log', 'm_c_dt_bias', 'm_c_norm_g', 'm_c_w_out', 'v_ffn1_wg', 'v_ffn1_wu', 'v_ffn1_wd', 'v_ffn2_wg', 'v_ffn2_wu', 'v_ffn2_wd', 'v_ln_g', 'v_ln_b', 'v_ple_wg', 'v_ple_bg', 'v_ple_wp', 'v_ab_w_in', 'v_a_sinks', 'v_b_conv_w', 'v_b_conv_b', 'v_b_wa', 'v_b_ba', 'v_b_wx', 'v_b_bx', 'v_b_lam', 'v_ab_w_out', 'v_c_w_in', 'v_c_conv_w', 'v_c_a_log', 'v_c_dt_bias', 'v_c_norm_g', 'v_c_w_out']
TWIN_OUTPUTS = ['loss', 'grad_x', 'grad_ffn1_wg', 'grad_ffn1_wu', 'grad_ffn1_wd', 'grad_ffn2_wg', 'grad_ffn2_wu', 'grad_ffn2_wd', 'grad_ln_g', 'grad_ln_b', 'grad_ple_wg', 'grad_ple_bg', 'grad_ple_wp', 'grad_ab_w_in', 'grad_a_sinks', 'grad_b_conv_w', 'grad_b_conv_b', 'grad_b_wa', 'grad_b_ba', 'grad_b_wx', 'grad_b_bx', 'grad_b_lam', 'grad_ab_w_out', 'grad_c_w_in', 'grad_c_conv_w', 'grad_c_a_log', 'grad_c_dt_bias', 'grad_c_norm_g', 'grad_c_w_out', 'delta_ffn1_wg', 'delta_ffn1_wu', 'delta_ffn1_wd', 'delta_ffn2_wg', 'delta_ffn2_wu', 'delta_ffn2_wd', 'delta_ln_g', 'delta_ln_b', 'delta_ple_wg', 'delta_ple_bg', 'delta_ple_wp', 'delta_ab_w_in', 'delta_a_sinks', 'delta_b_conv_w', 'delta_b_conv_b', 'delta_b_wa', 'delta_b_ba', 'delta_b_wx', 'delta_b_bx', 'delta_b_lam', 'delta_ab_w_out', 'delta_c_w_in', 'delta_c_conv_w', 'delta_c_a_log', 'delta_c_dt_bias', 'delta_c_norm_g', 'delta_c_w_out', 'new_m_ffn1_wg', 'new_m_ffn1_wu', 'new_m_ffn1_wd', 'new_m_ffn2_wg', 'new_m_ffn2_wu', 'new_m_ffn2_wd', 'new_m_ln_g', 'new_m_ln_b', 'new_m_ple_wg', 'new_m_ple_bg', 'new_m_ple_wp', 'new_m_ab_w_in', 'new_m_a_sinks', 'new_m_b_conv_w', 'new_m_b_conv_b', 'new_m_b_wa', 'new_m_b_ba', 'new_m_b_wx', 'new_m_b_bx', 'new_m_b_lam', 'new_m_ab_w_out', 'new_m_c_w_in', 'new_m_c_conv_w', 'new_m_c_a_log', 'new_m_c_dt_bias', 'new_m_c_norm_g', 'new_m_c_w_out', 'new_v_ffn1_wg', 'new_v_ffn1_wu', 'new_v_ffn1_wd', 'new_v_ffn2_wg', 'new_v_ffn2_wu', 'new_v_ffn2_wd', 'new_v_ln_g', 'new_v_ln_b', 'new_v_ple_wg', 'new_v_ple_bg', 'new_v_ple_wp', 'new_v_ab_w_in', 'new_v_a_sinks', 'new_v_b_conv_w', 'new_v_b_conv_b', 'new_v_b_wa', 'new_v_b_ba', 'new_v_b_wx', 'new_v_b_bx', 'new_v_b_lam', 'new_v_ab_w_out', 'new_v_c_w_in', 'new_v_c_conv_w', 'new_v_c_a_log', 'new_v_c_dt_bias', 'new_v_c_norm_g', 'new_v_c_w_out']
TWIN_LEAF_KINDS = {'loss': 'loss', 'grad_x': 'grad_x', 'grad_ffn1_wg': 'grad_w', 'grad_ffn1_wu': 'grad_w', 'grad_ffn1_wd': 'grad_w', 'grad_ffn2_wg': 'grad_w', 'grad_ffn2_wu': 'grad_w', 'grad_ffn2_wd': 'grad_w', 'grad_ln_g': 'grad_w', 'grad_ln_b': 'grad_w', 'grad_ple_wg': 'grad_w', 'grad_ple_bg': 'grad_w', 'grad_ple_wp': 'grad_w', 'grad_ab_w_in': 'grad_w', 'grad_a_sinks': 'grad_w', 'grad_b_conv_w': 'grad_w', 'grad_b_conv_b': 'grad_w', 'grad_b_wa': 'grad_w', 'grad_b_ba': 'grad_w', 'grad_b_wx': 'grad_w', 'grad_b_bx': 'grad_w', 'grad_b_lam': 'grad_w', 'grad_ab_w_out': 'grad_w', 'grad_c_w_in': 'grad_w', 'grad_c_conv_w': 'grad_w', 'grad_c_a_log': 'grad_w', 'grad_c_dt_bias': 'grad_w', 'grad_c_norm_g': 'grad_w', 'grad_c_w_out': 'grad_w', 'delta_ffn1_wg': 'delta_w', 'delta_ffn1_wu': 'delta_w', 'delta_ffn1_wd': 'delta_w', 'delta_ffn2_wg': 'delta_w', 'delta_ffn2_wu': 'delta_w', 'delta_ffn2_wd': 'delta_w', 'delta_ln_g': 'delta_w', 'delta_ln_b': 'delta_w', 'delta_ple_wg': 'delta_w', 'delta_ple_bg': 'delta_w', 'delta_ple_wp': 'delta_w', 'delta_ab_w_in': 'delta_w', 'delta_a_sinks': 'delta_w', 'delta_b_conv_w': 'delta_w', 'delta_b_conv_b': 'delta_w', 'delta_b_wa': 'delta_w', 'delta_b_ba': 'delta_w', 'delta_b_wx': 'delta_w', 'delta_b_bx': 'delta_w', 'delta_b_lam': 'delta_w', 'delta_ab_w_out': 'delta_w', 'delta_c_w_in': 'delta_w', 'delta_c_conv_w': 'delta_w', 'delta_c_a_log': 'delta_w', 'delta_c_dt_bias': 'delta_w', 'delta_c_norm_g': 'delta_w', 'delta_c_w_out': 'delta_w', 'new_m_ffn1_wg': 'new_m', 'new_m_ffn1_wu': 'new_m', 'new_m_ffn1_wd': 'new_m', 'new_m_ffn2_wg': 'new_m', 'new_m_ffn2_wu': 'new_m', 'new_m_ffn2_wd': 'new_m', 'new_m_ln_g': 'new_m', 'new_m_ln_b': 'new_m', 'new_m_ple_wg': 'new_m', 'new_m_ple_bg': 'new_m', 'new_m_ple_wp': 'new_m', 'new_m_ab_w_in': 'new_m', 'new_m_a_sinks': 'new_m', 'new_m_b_conv_w': 'new_m', 'new_m_b_conv_b': 'new_m', 'new_m_b_wa': 'new_m', 'new_m_b_ba': 'new_m', 'new_m_b_wx': 'new_m', 'new_m_b_bx': 'new_m', 'new_m_b_lam': 'new_m', 'new_m_ab_w_out': 'new_m', 'new_m_c_w_in': 'new_m', 'new_m_c_conv_w': 'new_m', 'new_m_c_a_log': 'new_m', 'new_m_c_dt_bias': 'new_m', 'new_m_c_norm_g': 'new_m', 'new_m_c_w_out': 'new_m', 'new_v_ffn1_wg': 'new_v', 'new_v_ffn1_wu': 'new_v', 'new_v_ffn1_wd': 'new_v', 'new_v_ffn2_wg': 'new_v', 'new_v_ffn2_wu': 'new_v', 'new_v_ffn2_wd': 'new_v', 'new_v_ln_g': 'new_v', 'new_v_ln_b': 'new_v', 'new_v_ple_wg': 'new_v', 'new_v_ple_bg': 'new_v', 'new_v_ple_wp': 'new_v', 'new_v_ab_w_in': 'new_v', 'new_v_a_sinks': 'new_v', 'new_v_b_conv_w': 'new_v', 'new_v_b_conv_b': 'new_v', 'new_v_b_wa': 'new_v', 'new_v_b_ba': 'new_v', 'new_v_b_wx': 'new_v', 'new_v_b_bx': 'new_v', 'new_v_b_lam': 'new_v', 'new_v_ab_w_out': 'new_v', 'new_v_c_w_in': 'new_v', 'new_v_c_conv_w': 'new_v', 'new_v_c_a_log': 'new_v', 'new_v_c_dt_bias': 'new_v', 'new_v_c_norm_g': 'new_v', 'new_v_c_w_out': 'new_v'}


def _forward(args):
    return _fwd_reference(*[args[k] for k in FWD_PARAMS])


def _output_shape():
    out = _jax.eval_shape(lambda: _forward(_fwd_setup_inputs(0)))
    return out.shape, out.dtype

N_MICROBATCH = 1
ADAM_LR = 0.001
ADAM_B1 = 0.9
ADAM_B2 = 0.999
ADAM_EPS = 1e-08
ADAM_WD = 0.01
ADAM_STEP = 10
PER_EXAMPLE_BATCH_AXIS = {'x': 0, 'p': 1, 'loss_target': 0}
SHARED_INPUTS = []
_WEIGHT_DTYPES = {'ffn1_wg': _jnp.float32, 'ffn1_wu': _jnp.float32, 'ffn1_wd': _jnp.float32, 'ffn2_wg': _jnp.float32, 'ffn2_wu': _jnp.float32, 'ffn2_wd': _jnp.float32, 'ln_g': _jnp.float32, 'ln_b': _jnp.float32, 'ple_wg': _jnp.float32, 'ple_bg': _jnp.float32, 'ple_wp': _jnp.float32, 'ab_w_in': _jnp.float32, 'a_sinks': _jnp.float32, 'b_conv_w': _jnp.float32, 'b_conv_b': _jnp.float32, 'b_wa': _jnp.float32, 'b_ba': _jnp.float32, 'b_wx': _jnp.float32, 'b_bx': _jnp.float32, 'b_lam': _jnp.float32, 'ab_w_out': _jnp.float32, 'c_w_in': _jnp.float32, 'c_conv_w': _jnp.float32, 'c_a_log': _jnp.float32, 'c_dt_bias': _jnp.float32, 'c_norm_g': _jnp.float32, 'c_w_out': _jnp.float32}
MOMENT_SCALE = {'ffn1_wg': 2.161558e-02, 'ffn1_wu': 2.113348e-02, 'ffn1_wd': 7.008263e-02, 'ffn2_wg': 1.930700e-02, 'ffn2_wu': 1.895169e-02, 'ffn2_wd': 6.301907e-02, 'ln_g': 2.692305e+01, 'ln_b': 5.909878e+00, 'ple_wg': 1.787981e-01, 'ple_bg': 4.695113e+00, 'ple_wp': 4.795444e-01, 'ab_w_in': 4.778349e-02, 'a_sinks': 4.695004e-03, 'b_conv_w': 1.472183e-01, 'b_conv_b': 2.743576e+00, 'b_wa': 1.202910e-01, 'b_ba': 5.887605e-02, 'b_wx': 2.172445e-01, 'b_bx': 4.208542e-02, 'b_lam': 6.040696e-02, 'ab_w_out': 2.106016e-01, 'c_w_in': 5.148041e-02, 'c_conv_w': 6.882398e-02, 'c_a_log': 3.255042e-01, 'c_dt_bias': 2.987564e-01, 'c_norm_g': 4.463945e-01, 'c_w_out': 3.389311e-01}


def _to_microbatches(a, axis):
    t = _jnp.moveaxis(a, axis, 0)
    t = t.reshape((N_MICROBATCH, t.shape[0] // N_MICROBATCH) + t.shape[1:])
    return _jnp.moveaxis(t, 1, axis + 1)


def setup_inputs(seed: int = 0) -> dict:
    inp = _fwd_setup_inputs(seed)
    key = _jax.random.fold_in(_jax.random.key(seed), 7919)
    shape, _ = _output_shape()
    out = dict(inp)
    out["loss_target"] = _jax.random.normal(_jax.random.fold_in(key, 0), shape, _jnp.float32)
    for i, name in enumerate(TWIN_WEIGHTS):
        w = inp[name].astype(_jnp.float32)
        if MOMENT_SCALE is None:
            s = _jnp.sqrt(_jnp.mean(_jnp.square(w)) + 1e-30)
        else:
            s = MOMENT_SCALE[name]
        km, kv = _jax.random.split(_jax.random.fold_in(key, i + 1))
        out[name] = w
        out["m_" + name] = s * _jax.random.normal(km, w.shape, _jnp.float32)
        out["v_" + name] = (s * s) * _jax.random.uniform(kv, w.shape, _jnp.float32, 0.5, 1.5)
    if N_MICROBATCH > 1:
        for name, axis in PER_EXAMPLE_BATCH_AXIS.items():
            out[name] = _to_microbatches(out[name], axis)
    return {'x': out['x'], 'p': out['p'], 'ffn1_wg': out['ffn1_wg'], 'ffn1_wu': out['ffn1_wu'], 'ffn1_wd': out['ffn1_wd'], 'ffn2_wg': out['ffn2_wg'], 'ffn2_wu': out['ffn2_wu'], 'ffn2_wd': out['ffn2_wd'], 'ln_g': out['ln_g'], 'ln_b': out['ln_b'], 'ple_wg': out['ple_wg'], 'ple_bg': out['ple_bg'], 'ple_wp': out['ple_wp'], 'ab_w_in': out['ab_w_in'], 'a_sinks': out['a_sinks'], 'b_conv_w': out['b_conv_w'], 'b_conv_b': out['b_conv_b'], 'b_wa': out['b_wa'], 'b_ba': out['b_ba'], 'b_wx': out['b_wx'], 'b_bx': out['b_bx'], 'b_lam': out['b_lam'], 'ab_w_out': out['ab_w_out'], 'c_w_in': out['c_w_in'], 'c_conv_w': out['c_conv_w'], 'c_a_log': out['c_a_log'], 'c_dt_bias': out['c_dt_bias'], 'c_norm_g': out['c_norm_g'], 'c_w_out': out['c_w_out'], 'loss_target': out['loss_target'], 'm_ffn1_wg': out['m_ffn1_wg'], 'm_ffn1_wu': out['m_ffn1_wu'], 'm_ffn1_wd': out['m_ffn1_wd'], 'm_ffn2_wg': out['m_ffn2_wg'], 'm_ffn2_wu': out['m_ffn2_wu'], 'm_ffn2_wd': out['m_ffn2_wd'], 'm_ln_g': out['m_ln_g'], 'm_ln_b': out['m_ln_b'], 'm_ple_wg': out['m_ple_wg'], 'm_ple_bg': out['m_ple_bg'], 'm_ple_wp': out['m_ple_wp'], 'm_ab_w_in': out['m_ab_w_in'], 'm_a_sinks': out['m_a_sinks'], 'm_b_conv_w': out['m_b_conv_w'], 'm_b_conv_b': out['m_b_conv_b'], 'm_b_wa': out['m_b_wa'], 'm_b_ba': out['m_b_ba'], 'm_b_wx': out['m_b_wx'], 'm_b_bx': out['m_b_bx'], 'm_b_lam': out['m_b_lam'], 'm_ab_w_out': out['m_ab_w_out'], 'm_c_w_in': out['m_c_w_in'], 'm_c_conv_w': out['m_c_conv_w'], 'm_c_a_log': out['m_c_a_log'], 'm_c_dt_bias': out['m_c_dt_bias'], 'm_c_norm_g': out['m_c_norm_g'], 'm_c_w_out': out['m_c_w_out'], 'v_ffn1_wg': out['v_ffn1_wg'], 'v_ffn1_wu': out['v_ffn1_wu'], 'v_ffn1_wd': out['v_ffn1_wd'], 'v_ffn2_wg': out['v_ffn2_wg'], 'v_ffn2_wu': out['v_ffn2_wu'], 'v_ffn2_wd': out['v_ffn2_wd'], 'v_ln_g': out['v_ln_g'], 'v_ln_b': out['v_ln_b'], 'v_ple_wg': out['v_ple_wg'], 'v_ple_bg': out['v_ple_bg'], 'v_ple_wp': out['v_ple_wp'], 'v_ab_w_in': out['v_ab_w_in'], 'v_a_sinks': out['v_a_sinks'], 'v_b_conv_w': out['v_b_conv_w'], 'v_b_conv_b': out['v_b_conv_b'], 'v_b_wa': out['v_b_wa'], 'v_b_ba': out['v_b_ba'], 'v_b_wx': out['v_b_wx'], 'v_b_bx': out['v_b_bx'], 'v_b_lam': out['v_b_lam'], 'v_ab_w_out': out['v_ab_w_out'], 'v_c_w_in': out['v_c_w_in'], 'v_c_conv_w': out['v_c_conv_w'], 'v_c_a_log': out['v_c_a_log'], 'v_c_dt_bias': out['v_c_dt_bias'], 'v_c_norm_g': out['v_c_norm_g'], 'v_c_w_out': out['v_c_w_out']}


def _loss(weights, diff, rest, loss_target):
    with _jax.named_scope("forward"):
        args = {**rest, TWIN_DIFF_INPUT: diff, **{k: w.astype(_WEIGHT_DTYPES[k]) for k, w in weights.items()}}
        y = _forward(args)
    with _jax.named_scope("loss_head"):
        err = _jnp.square(y.astype(_jnp.float32) - loss_target)
        return 0.5 * _jnp.sum(_jnp.mean(err, axis=-1)) if err.ndim else 0.5 * err


def _adamw(w, g, m, v):
    m = ADAM_B1 * m + (1.0 - ADAM_B1) * g
    v = ADAM_B2 * v + (1.0 - ADAM_B2) * _jnp.square(g)
    m_hat = m / (1.0 - ADAM_B1 ** ADAM_STEP)
    v_hat = v / (1.0 - ADAM_B2 ** ADAM_STEP)
    delta = -ADAM_LR * (m_hat / (_jnp.sqrt(v_hat) + ADAM_EPS) + ADAM_WD * w)
    return delta, m, v


def reference(x, p, ffn1_wg, ffn1_wu, ffn1_wd, ffn2_wg, ffn2_wu, ffn2_wd, ln_g, ln_b, ple_wg, ple_bg, ple_wp, ab_w_in, a_sinks, b_conv_w, b_conv_b, b_wa, b_ba, b_wx, b_bx, b_lam, ab_w_out, c_w_in, c_conv_w, c_a_log, c_dt_bias, c_norm_g, c_w_out, loss_target, m_ffn1_wg, m_ffn1_wu, m_ffn1_wd, m_ffn2_wg, m_ffn2_wu, m_ffn2_wd, m_ln_g, m_ln_b, m_ple_wg, m_ple_bg, m_ple_wp, m_ab_w_in, m_a_sinks, m_b_conv_w, m_b_conv_b, m_b_wa, m_b_ba, m_b_wx, m_b_bx, m_b_lam, m_ab_w_out, m_c_w_in, m_c_conv_w, m_c_a_log, m_c_dt_bias, m_c_norm_g, m_c_w_out, v_ffn1_wg, v_ffn1_wu, v_ffn1_wd, v_ffn2_wg, v_ffn2_wu, v_ffn2_wd, v_ln_g, v_ln_b, v_ple_wg, v_ple_bg, v_ple_wp, v_ab_w_in, v_a_sinks, v_b_conv_w, v_b_conv_b, v_b_wa, v_b_ba, v_b_wx, v_b_bx, v_b_lam, v_ab_w_out, v_c_w_in, v_c_conv_w, v_c_a_log, v_c_dt_bias, v_c_norm_g, v_c_w_out):
    given = dict(x=x, p=p, ffn1_wg=ffn1_wg, ffn1_wu=ffn1_wu, ffn1_wd=ffn1_wd, ffn2_wg=ffn2_wg, ffn2_wu=ffn2_wu, ffn2_wd=ffn2_wd, ln_g=ln_g, ln_b=ln_b, ple_wg=ple_wg, ple_bg=ple_bg, ple_wp=ple_wp, ab_w_in=ab_w_in, a_sinks=a_sinks, b_conv_w=b_conv_w, b_conv_b=b_conv_b, b_wa=b_wa, b_ba=b_ba, b_wx=b_wx, b_bx=b_bx, b_lam=b_lam, ab_w_out=ab_w_out, c_w_in=c_w_in, c_conv_w=c_conv_w, c_a_log=c_a_log, c_dt_bias=c_dt_bias, c_norm_g=c_norm_g, c_w_out=c_w_out, loss_target=loss_target, m_ffn1_wg=m_ffn1_wg, m_ffn1_wu=m_ffn1_wu, m_ffn1_wd=m_ffn1_wd, m_ffn2_wg=m_ffn2_wg, m_ffn2_wu=m_ffn2_wu, m_ffn2_wd=m_ffn2_wd, m_ln_g=m_ln_g, m_ln_b=m_ln_b, m_ple_wg=m_ple_wg, m_ple_bg=m_ple_bg, m_ple_wp=m_ple_wp, m_ab_w_in=m_ab_w_in, m_a_sinks=m_a_sinks, m_b_conv_w=m_b_conv_w, m_b_conv_b=m_b_conv_b, m_b_wa=m_b_wa, m_b_ba=m_b_ba, m_b_wx=m_b_wx, m_b_bx=m_b_bx, m_b_lam=m_b_lam, m_ab_w_out=m_ab_w_out, m_c_w_in=m_c_w_in, m_c_conv_w=m_c_conv_w, m_c_a_log=m_c_a_log, m_c_dt_bias=m_c_dt_bias, m_c_norm_g=m_c_norm_g, m_c_w_out=m_c_w_out, v_ffn1_wg=v_ffn1_wg, v_ffn1_wu=v_ffn1_wu, v_ffn1_wd=v_ffn1_wd, v_ffn2_wg=v_ffn2_wg, v_ffn2_wu=v_ffn2_wu, v_ffn2_wd=v_ffn2_wd, v_ln_g=v_ln_g, v_ln_b=v_ln_b, v_ple_wg=v_ple_wg, v_ple_bg=v_ple_bg, v_ple_wp=v_ple_wp, v_ab_w_in=v_ab_w_in, v_a_sinks=v_a_sinks, v_b_conv_w=v_b_conv_w, v_b_conv_b=v_b_conv_b, v_b_wa=v_b_wa, v_b_ba=v_b_ba, v_b_wx=v_b_wx, v_b_bx=v_b_bx, v_b_lam=v_b_lam, v_ab_w_out=v_ab_w_out, v_c_w_in=v_c_w_in, v_c_conv_w=v_c_conv_w, v_c_a_log=v_c_a_log, v_c_dt_bias=v_c_dt_bias, v_c_norm_g=v_c_norm_g, v_c_w_out=v_c_w_out)
    weights = {n: given[n] for n in TWIN_WEIGHTS}
    shared = {n: given[n] for n in SHARED_INPUTS}
    per_example = {n: given[n] for n in ['x', 'p']}
    grad_fn = _jax.value_and_grad(_loss, argnums=(0, 1))

    def one_microbatch(ex, loss_target):
        ex = dict(ex)
        diff = ex.pop(TWIN_DIFF_INPUT)
        return grad_fn(weights, diff, {**shared, **ex}, loss_target)

    if N_MICROBATCH == 1:
        loss, (grad_w, grad_x) = one_microbatch(per_example, given["loss_target"])
    else:
        def body(carry, xs):
            loss_sum, grad_sum = carry
            l_k, (gw_k, gx_k) = one_microbatch(xs[0], xs[1])
            with _jax.named_scope("update"):
                return (loss_sum + l_k, _jax.tree.map(_jnp.add, grad_sum, gw_k)), gx_k

        init = (_jnp.zeros((), _jnp.float32), _jax.tree.map(_jnp.zeros_like, weights))
        (loss, grad_w), grad_x = _jax.lax.scan(body, init, (per_example, given["loss_target"]))
    with _jax.named_scope("update"):
        delta_w, new_m, new_v = {}, {}, {}
        for n in TWIN_WEIGHTS:
            delta_w[n], new_m[n], new_v[n] = _adamw(weights[n], grad_w[n], given["m_" + n], given["v_" + n])
    return (loss, grad_x, *[grad_w[n] for n in TWIN_WEIGHTS], *[delta_w[n] for n in TWIN_WEIGHTS],
            *[new_m[n] for n in TWIN_WEIGHTS], *[new_v[n] for n in TWIN_WEIGHTS])
```

```python
import functools
import math

import numpy as np
import jax
import jax.numpy as jnp
from jax import lax
from jax.experimental import pallas as pl
from jax.experimental.pallas import tpu as pltpu

F32 = jnp.float32
MM = jnp.bfloat16
HI = lax.Precision.HIGHEST

D_MODEL = 1024
D_FF = 2816
D_PLE = 256
DEPTH = 2
CHUNK = 64
A_HEADS = 8
A_KV_HEADS = 2
A_GROUP = 4
A_HEAD_DIM = 64
A_WIDTH = 512
A_KV_WIDTH = 128
B_WIDTH = 512
B_BLOCK = 64
RG_C = 8.0
AB_PROJ = 1792
C_HEADS = 8
C_HEAD_DIM = 128
C_WIDTH = 1024
DN_ALPHA = (2.0 * DEPTH) ** 0.25
LN_EPS = 1e-5
NORM_EPS = 1e-6
NEG = -1e30
ADAM_LR = 0.001
ADAM_B1 = 0.9
ADAM_B2 = 0.999
ADAM_EPS = 1e-08
ADAM_WD = 0.01
ADAM_STEP = 10
N_DEV = 8
VMEM_LIMIT = 56 * 1024 * 1024

NN = ((1,), (0,))
NT = ((1,), (1,))
TN = ((0,), (0,))


def _pcall(body, **kw):
    return pl.pallas_call(body, **kw)


def _cp(*sem):
    return pltpu.CompilerParams(dimension_semantics=sem, vmem_limit_bytes=VMEM_LIMIT)


def _dot(a, b, dims=NN, precision=None):
    return lax.dot_general(a, b, (dims, ((), ())), preferred_element_type=F32, precision=precision)


def _mdot(a, b, dims=NN):
    return _dot(a.astype(MM), b.astype(MM), dims)


def _tile(n, pref):
    if n <= pref:
        return n
    for c in range(pref - pref % 128, 0, -128):
        if n % c == 0:
            return c
    return n


def _sigmoid(x):
    return 1.0 / (1.0 + jnp.exp(-x))


def _softplus(x):
    return jnp.maximum(x, 0.0) + jnp.log(1.0 + jnp.exp(-jnp.abs(x)))


def _ln_stats(z):
    mu = jnp.mean(z, axis=-1, keepdims=True)
    zc = z - mu
    var = jnp.mean(zc * zc, axis=-1, keepdims=True)
    return zc, lax.rsqrt(var + LN_EPS)


def matmul(a, b, *, mode, name, tm=512, tn=512, tk=512, out_dtype=F32, scale=None, add=None, add_scale=1.0):
    if mode == "nn":
        (m, kk), (_, n) = a.shape, b.shape
        dims = NN
    elif mode == "nt":
        (m, kk), (n, _) = a.shape, b.shape
        dims = NT
    else:
        (kk, m), (_, n) = a.shape, b.shape
        dims = TN
    tm, tn, tk = _tile(m, tm), _tile(n, tn), _tile(kk, tk)
    if mode == "nn":
        a_spec = pl.BlockSpec((tm, tk), lambda i, j, k: (i, k))
        b_spec = pl.BlockSpec((tk, tn), lambda i, j, k: (k, j))
    elif mode == "nt":
        a_spec = pl.BlockSpec((tm, tk), lambda i, j, k: (i, k))
        b_spec = pl.BlockSpec((tn, tk), lambda i, j, k: (j, k))
    else:
        a_spec = pl.BlockSpec((tk, tm), lambda i, j, k: (k, i))
        b_spec = pl.BlockSpec((tk, tn), lambda i, j, k: (k, j))
    nk = kk // tk
    o_spec = pl.BlockSpec((tm, tn), lambda i, j, k: (i, j))
    has_add = add is not None

    def body(*refs):
        if has_add:
            a_ref, b_ref, add_ref, o_ref, acc_ref = refs
        else:
            a_ref, b_ref, o_ref, acc_ref = refs
        k = pl.program_id(2)

        @pl.when(k == 0)
        def _():
            acc_ref[...] = jnp.zeros_like(acc_ref)

        acc_ref[...] += _mdot(a_ref[...], b_ref[...], dims)

        @pl.when(k == nk - 1)
        def _():
            r = acc_ref[...]
            if scale is not None:
                r = r * scale
            if has_add:
                r = r + add_scale * add_ref[...].astype(F32)
            o_ref[...] = r.astype(out_dtype)

    ins = [a, b] + ([add] if has_add else [])
    in_specs = [a_spec, b_spec] + ([o_spec] if has_add else [])
    return _pcall(
        body, name=name, grid=(m // tm, n // tn, nk), in_specs=in_specs, out_specs=o_spec,
        out_shape=jax.ShapeDtypeStruct((m, n), out_dtype), scratch_shapes=[pltpu.VMEM((tm, tn), F32)],
        compiler_params=_cp("parallel", "parallel", "arbitrary"),
    )(*ins)


def ffn_fwd(x, wg, wu, wd, g, b, *, name, tm=512, tf=256):
    t, d = x.shape
    f = wg.shape[1]
    tm = min(tm, t)
    nj = f // tf

    def body(x_ref, wg_ref, wu_ref, wd_ref, g_ref, b_ref, y_ref, z_ref, xb_ref, acc_ref):
        j = pl.program_id(1)

        @pl.when(j == 0)
        def _():
            xb_ref[...] = x_ref[...].astype(MM)
            acc_ref[...] = jnp.zeros_like(acc_ref)

        xb = xb_ref[...]
        hg = _dot(xb, wg_ref[...])
        hu = _dot(xb, wu_ref[...])
        act = (hg * _sigmoid(hg) * hu).astype(MM)
        acc_ref[...] += _dot(act, wd_ref[...])

        @pl.when(j == nj - 1)
        def _():
            z = DN_ALPHA * x_ref[...] + 0.5 * acc_ref[...]
            z_ref[...] = z
            zc, rstd = _ln_stats(z)
            y_ref[...] = zc * rstd * g_ref[...] + b_ref[...]

    row = pl.BlockSpec((tm, d), lambda i, j: (i, 0))
    vec = pl.BlockSpec((1, d), lambda i, j: (0, 0))
    return _pcall(
        body, name=name, grid=(t // tm, nj),
        in_specs=[row, pl.BlockSpec((d, tf), lambda i, j: (0, j)), pl.BlockSpec((d, tf), lambda i, j: (0, j)),
                  pl.BlockSpec((tf, d), lambda i, j: (j, 0)), vec, vec],
        out_specs=[row, row],
        out_shape=[jax.ShapeDtypeStruct((t, d), F32), jax.ShapeDtypeStruct((t, d), F32)],
        scratch_shapes=[pltpu.VMEM((tm, d), MM), pltpu.VMEM((tm, d), F32)],
        compiler_params=_cp("parallel", "arbitrary"),
    )(x, wg, wu, wd, g.reshape(1, d), b.reshape(1, d))


def ffn_bwd(x, dz, wg, wu, wd, *, name, tm=512, tf=256):
    t, d = x.shape
    f = wg.shape[1]
    tm = min(tm, t)
    nj = f // tf

    def body(x_ref, dz_ref, wg_ref, wu_ref, wd_ref, dx_ref, act_ref, dhg_ref, dhu_ref, xb_ref, dfb_ref, acc_ref):
        j = pl.program_id(1)

        @pl.when(j == 0)
        def _():
            xb_ref[...] = x_ref[...].astype(MM)
            dfb_ref[...] = (0.5 * dz_ref[...]).astype(MM)
            acc_ref[...] = jnp.zeros_like(acc_ref)

        xb = xb_ref[...]
        hg = _dot(xb, wg_ref[...])
        hu = _dot(xb, wu_ref[...])
        s = _sigmoid(hg)
        dact = _dot(dfb_ref[...], wd_ref[...], NT)
        sg = hg * s
        act_ref[...] = (sg * hu).astype(MM)
        dhu = (dact * sg).astype(MM)
        dhg = (dact * hu * (s + sg * (1.0 - s))).astype(MM)
        dhu_ref[...] = dhu
        dhg_ref[...] = dhg
        acc_ref[...] += _dot(dhg, wg_ref[...], NT) + _dot(dhu, wu_ref[...], NT)

        @pl.when(j == nj - 1)
        def _():
            dx_ref[...] = DN_ALPHA * dz_ref[...] + acc_ref[...]

    row = pl.BlockSpec((tm, d), lambda i, j: (i, 0))
    hid = pl.BlockSpec((tm, tf), lambda i, j: (i, j))
    return _pcall(
        body, name=name, grid=(t // tm, nj),
        in_specs=[row, row, pl.BlockSpec((d, tf), lambda i, j: (0, j)), pl.BlockSpec((d, tf), lambda i, j: (0, j)),
                  pl.BlockSpec((tf, d), lambda i, j: (j, 0))],
        out_specs=[row, hid, hid, hid],
        out_shape=[jax.ShapeDtypeStruct((t, d), F32)] + [jax.ShapeDtypeStruct((t, f), MM)] * 3,
        scratch_shapes=[pltpu.VMEM((tm, d), MM), pltpu.VMEM((tm, d), MM), pltpu.VMEM((tm, d), F32)],
        compiler_params=_cp("parallel", "arbitrary"),
    )(x, dz, wg, wu, wd)


def ln_bwd(dy, z, g, *, name, tm=512):
    t, d = z.shape
    tm = min(tm, t)

    def body(dy_ref, z_ref, g_ref, dz_ref, dg_ref, db_ref):
        i = pl.program_id(0)

        @pl.when(i == 0)
        def _():
            dg_ref[...] = jnp.zeros_like(dg_ref)
            db_ref[...] = jnp.zeros_like(db_ref)

        dy = dy_ref[...]
        zc, rstd = _ln_stats(z_ref[...])
        xh = zc * rstd
        dg_ref[...] += jnp.sum(dy * xh, axis=0, keepdims=True)
        db_ref[...] += jnp.sum(dy, axis=0, keepdims=True)
        dxh = dy * g_ref[...]
        m1 = jnp.mean(dxh, axis=-1, keepdims=True)
        m2 = jnp.mean(dxh * xh, axis=-1, keepdims=True)
        dz_ref[...] = rstd * (dxh - m1 - xh * m2)

    row = pl.BlockSpec((tm, d), lambda i: (i, 0))
    vec = pl.BlockSpec((1, d), lambda i: (0, 0))
    return _pcall(
        body, name=name, grid=(t // tm,), in_specs=[row, row, vec], out_specs=[row, vec, vec],
        out_shape=[jax.ShapeDtypeStruct((t, d), F32), jax.ShapeDtypeStruct((1, d), F32), jax.ShapeDtypeStruct((1, d), F32)],
        compiler_params=_cp("arbitrary"),
    )(dy, z, g.reshape(1, d))


def mm_ln_fwd(a, w, res, g, b, *, name, tm=512):
    t, kk = a.shape
    d = w.shape[1]
    tm = min(tm, t)

    def body(a_ref, w_ref, res_ref, g_ref, b_ref, y_ref, z_ref):
        z = DN_ALPHA * res_ref[...] + _mdot(a_ref[...], w_ref[...])
        z_ref[...] = z
        zc, rstd = _ln_stats(z)
        y_ref[...] = zc * rstd * g_ref[...] + b_ref[...]

    row = pl.BlockSpec((tm, d), lambda i: (i, 0))
    vec = pl.BlockSpec((1, d), lambda i: (0, 0))
    return _pcall(
        body, name=name, grid=(t // tm,),
        in_specs=[pl.BlockSpec((tm, kk), lambda i: (i, 0)), pl.BlockSpec((kk, d), lambda i: (0, 0)), row, vec, vec],
        out_specs=[row, row],
        out_shape=[jax.ShapeDtypeStruct((t, d), F32), jax.ShapeDtypeStruct((t, d), F32)],
        compiler_params=_cp("parallel"),
    )(a, w, res, g.reshape(1, d), b.reshape(1, d))


def ple_fwd(y, p, wg, bg, wp, *, name, tm=512):
    t, d = y.shape
    dp = p.shape[1]
    tm = min(tm, t)

    def body(y_ref, p_ref, wg_ref, bg_ref, wp_ref, o_ref):
        yv = y_ref[...]
        gate = _sigmoid(_mdot(yv, wg_ref[...]) + bg_ref[...])
        o_ref[...] = yv + gate * _mdot(p_ref[...], wp_ref[...])

    row = pl.BlockSpec((tm, d), lambda i: (i, 0))
    return _pcall(
        body, name=name, grid=(t // tm,),
        in_specs=[row, pl.BlockSpec((tm, dp), lambda i: (i, 0)), pl.BlockSpec((d, d), lambda i: (0, 0)),
                  pl.BlockSpec((1, d), lambda i: (0, 0)), pl.BlockSpec((dp, d), lambda i: (0, 0))],
        out_specs=row, out_shape=jax.ShapeDtypeStruct((t, d), F32), compiler_params=_cp("parallel"),
    )(y, p, wg, bg.reshape(1, d), wp)


def ple_bwd(do, y, p, wg, bg, wp, *, name, tm=512):
    t, d = y.shape
    dp = p.shape[1]
    tm = min(tm, t)

    def body(do_ref, y_ref, p_ref, wg_ref, bg_ref, wp_ref, dy_ref, dt_ref, de_ref, dbg_ref):
        i = pl.program_id(0)

        @pl.when(i == 0)
        def _():
            dbg_ref[...] = jnp.zeros_like(dbg_ref)

        dov = do_ref[...]
        gate = _sigmoid(_mdot(y_ref[...], wg_ref[...]) + bg_ref[...])
        emb = _mdot(p_ref[...], wp_ref[...])
        dt = dov * emb * gate * (1.0 - gate)
        dbg_ref[...] += jnp.sum(dt, axis=0, keepdims=True)
        dtb = dt.astype(MM)
        dt_ref[...] = dtb
        de_ref[...] = (dov * gate).astype(MM)
        dy_ref[...] = dov + _dot(dtb, wg_ref[...], NT)

    row = pl.BlockSpec((tm, d), lambda i: (i, 0))
    vec = pl.BlockSpec((1, d), lambda i: (0, 0))
    return _pcall(
        body, name=name, grid=(t // tm,),
        in_specs=[row, row, pl.BlockSpec((tm, dp), lambda i: (i, 0)), pl.BlockSpec((d, d), lambda i: (0, 0)),
                  vec, pl.BlockSpec((dp, d), lambda i: (0, 0))],
        out_specs=[row, row, row, vec],
        out_shape=[jax.ShapeDtypeStruct((t, d), F32), jax.ShapeDtypeStruct((t, d), MM),
                   jax.ShapeDtypeStruct((t, d), MM), jax.ShapeDtypeStruct((1, d), F32)],
        compiler_params=_cp("arbitrary"),
    )(do, y, p, wg, bg.reshape(1, d), wp)


def loss_fwd_bwd(y, tgt, *, name, tm=512):
    t, d = y.shape
    tm = min(tm, t)

    def body(y_ref, t_ref, l_ref, dy_ref):
        i = pl.program_id(0)

        @pl.when(i == 0)
        def _():
            l_ref[...] = jnp.zeros_like(l_ref)

        err = y_ref[...] - t_ref[...]
        dy_ref[...] = err * (1.0 / d)
        l_ref[...] += (0.5 / d) * jnp.sum(jnp.sum(err * err, axis=1, keepdims=True), axis=0, keepdims=True)

    row = pl.BlockSpec((tm, d), lambda i: (i, 0))
    return _pcall(
        body, name=name, grid=(t // tm,), in_specs=[row, row],
        out_specs=[pl.BlockSpec((1, 128), lambda i: (0, 0)), row],
        out_shape=[jax.ShapeDtypeStruct((1, 128), F32), jax.ShapeDtypeStruct((t, d), F32)],
        compiler_params=_cp("arbitrary"),
    )(y, tgt)


def _shift_dn(x, s, row):
    return x if s == 0 else jnp.where(row >= s, pltpu.roll(x, s, 0), 0.0)


def _shift_up(x, s, row):
    n = x.shape[0]
    return x if s == 0 else jnp.where(row < n - s, pltpu.roll(x, n - s, 0), 0.0)


def _conv_fwd(x, w, row):
    kk = w.shape[0]
    y = w[kk - 1:kk, :] * x
    for j in range(kk - 1):
        y = y + w[j:j + 1, :] * _shift_dn(x, kk - 1 - j, row)
    return y


def _conv_bwd(x, w, dy, row):
    kk = w.shape[0]
    dx = w[kk - 1:kk, :] * dy
    dws = []
    for j in range(kk - 1):
        dx = dx + w[j:j + 1, :] * _shift_up(dy, kk - 1 - j, row)
        dws.append(jnp.sum(dy * _shift_dn(x, kk - 1 - j, row), axis=0, keepdims=True))
    dws.append(jnp.sum(dy * x, axis=0, keepdims=True))
    return dx, jnp.concatenate(dws, axis=0)


def _gelu(x):
    c = math.sqrt(2.0 / math.pi)
    th = jnp.tanh(c * (x + 0.044715 * x * x * x))
    return 0.5 * x * (1.0 + th), th


def _gelu_grad(x, th):
    c = math.sqrt(2.0 / math.pi)
    return 0.5 * (1.0 + th) + 0.5 * x * (1.0 - th * th) * c * (1.0 + 3.0 * 0.044715 * x * x)


def _neg_expm1(y):
    ser = -(y * (1.0 + y * (0.5 + y * (1.0 / 6.0 + y * (1.0 / 24.0 + y * (1.0 / 120.0))))))
    return jnp.where(y > -0.05, ser, 1.0 - jnp.exp(y))


def _attn_head(qh, kk, vv, bias, valid, sink):
    s = _mdot(qh, kk, NT) * (A_HEAD_DIM ** -0.5) - bias
    s = jnp.where(valid, s, NEG)
    m = jnp.maximum(jnp.max(s, axis=-1, keepdims=True), sink)
    pr = jnp.exp(s - m)
    den = jnp.sum(pr, axis=-1, keepdims=True) + jnp.exp(sink - m)
    return pr / den, jnp.exp(sink - m) / den


def _attn_masks(n):
    ci = lax.broadcasted_iota(jnp.int32, (CHUNK, 3 * CHUNK), 0)
    ji = lax.broadcasted_iota(jnp.int32, (CHUNK, 3 * CHUNK), 1)
    dist = jnp.abs(ci + 2 * CHUNK - ji).astype(F32)
    valid = (n * CHUNK + ji - 2 * CHUNK) >= 0
    return dist, valid


def attn_fwd(proj, sinks, bsz, *, name):
    t = proj.shape[0]
    s_len = t // bsz
    nc = s_len // CHUNK
    pad = 2 * CHUNK

    def body(q_ref, k_ref, v_ref, sk_ref, o_ref, kp_ref, vp_ref):
        kp_ref[0:pad, :] = jnp.zeros((pad, A_KV_WIDTH), F32)
        vp_ref[0:pad, :] = jnp.zeros((pad, A_KV_WIDTH), F32)
        kp_ref[pad:, :] = k_ref[...]
        vp_ref[pad:, :] = v_ref[...]

        def chunk(n, carry):
            st = pl.multiple_of(n * CHUNK, CHUNK)
            q = q_ref[pl.ds(st, CHUNK), :]
            kb = kp_ref[pl.ds(st, 3 * CHUNK), :]
            vb = vp_ref[pl.ds(st, 3 * CHUNK), :]
            dist, valid = _attn_masks(n)
            outs = []
            for h in range(A_HEADS):
                kh = h // A_GROUP
                pn, _ = _attn_head(q[:, h * 64:(h + 1) * 64], kb[:, kh * 64:(kh + 1) * 64], None,
                                   (2.0 ** -(h + 1)) * dist, valid, sk_ref[h])
                outs.append(_mdot(pn, vb[:, kh * 64:(kh + 1) * 64]))
            o_ref[pl.ds(st, CHUNK), :] = jnp.concatenate(outs, axis=-1)
            return carry

        lax.fori_loop(0, nc, chunk, 0)

    return _pcall(
        body, name=name, grid=(bsz,),
        in_specs=[pl.BlockSpec((s_len, A_WIDTH), lambda b: (b, 0)), pl.BlockSpec((s_len, 128), lambda b: (b, 4)),
                  pl.BlockSpec((s_len, 128), lambda b: (b, 5)), pl.BlockSpec(memory_space=pltpu.SMEM)],
        out_specs=pl.BlockSpec((s_len, A_WIDTH), lambda b: (b, 0)),
        out_shape=jax.ShapeDtypeStruct((t, A_WIDTH), F32),
        scratch_shapes=[pltpu.VMEM((s_len + pad, A_KV_WIDTH), F32), pltpu.VMEM((s_len + pad, A_KV_WIDTH), F32)],
        compiler_params=_cp("parallel"),
    )(proj, proj, proj, sinks)


def attn_bwd(proj, sinks, dcat, bsz, *, name):
    t = proj.shape[0]
    s_len = t // bsz
    nc = s_len // CHUNK
    pad = 2 * CHUNK

    def body(q_ref, k_ref, v_ref, do_ref, sk_ref, dq_ref, dk_ref, dv_ref, dsk_ref, kp_ref, vp_ref, dkp_ref, dvp_ref):
        kp_ref[0:pad, :] = jnp.zeros((pad, A_KV_WIDTH), F32)
        vp_ref[0:pad, :] = jnp.zeros((pad, A_KV_WIDTH), F32)
        kp_ref[pad:, :] = k_ref[...]
        vp_ref[pad:, :] = v_ref[...]
        dkp_ref[...] = jnp.zeros_like(dkp_ref)
        dvp_ref[...] = jnp.zeros_like(dvp_ref)
        lane = lax.broadcasted_iota(jnp.int32, (1, 128), 1)

        def chunk(n, dsk):
            st = pl.multiple_of(n * CHUNK, CHUNK)
            q = q_ref[pl.ds(st, CHUNK), :]
            do = do_ref[pl.ds(st, CHUNK), :]
            kb = kp_ref[pl.ds(st, 3 * CHUNK), :]
            vb = vp_ref[pl.ds(st, 3 * CHUNK), :]
            dist, valid = _attn_masks(n)
            dqs, dks, dvs = [], [], []
            for kh in range(A_KV_HEADS):
                kk = kb[:, kh * 64:(kh + 1) * 64]
                vv = vb[:, kh * 64:(kh + 1) * 64]
                dk_acc = jnp.zeros((3 * CHUNK, 64), F32)
                dv_acc = jnp.zeros((3 * CHUNK, 64), F32)
                for gi in range(A_GROUP):
                    h = kh * A_GROUP + gi
                    qh = q[:, h * 64:(h + 1) * 64]
                    doh = do[:, h * 64:(h + 1) * 64]
                    pn, psink = _attn_head(qh, kk, None, (2.0 ** -(h + 1)) * dist, valid, sk_ref[h])
                    dp = _mdot(doh, vv, NT)
                    rowdot = jnp.sum(pn * dp, axis=-1, keepdims=True)
                    ds = pn * (dp - rowdot)
                    dsk = dsk + jnp.where(lane == h, -jnp.sum(psink * rowdot, axis=0, keepdims=True), 0.0)
                    dqs.append(_mdot(ds, kk) * (A_HEAD_DIM ** -0.5))
                    dk_acc = dk_acc + _mdot(ds, qh, TN) * (A_HEAD_DIM ** -0.5)
                    dv_acc = dv_acc + _mdot(pn, doh, TN)
                dks.append(dk_acc)
                dvs.append(dv_acc)
            dq_ref[pl.ds(st, CHUNK), :] = jnp.concatenate(dqs, axis=-1)
            dkp_ref[pl.ds(st, 3 * CHUNK), :] += jnp.concatenate(dks, axis=-1)
            dvp_ref[pl.ds(st, 3 * CHUNK), :] += jnp.concatenate(dvs, axis=-1)
            return dsk

        dsk = lax.fori_loop(0, nc, chunk, jnp.zeros((1, 128), F32))
        dsk_ref[0] = dsk
        dk_ref[...] = dkp_ref[pad:, :]
        dv_ref[...] = dvp_ref[pad:, :]

    kv = jax.ShapeDtypeStruct((t, A_KV_WIDTH), F32)
    return _pcall(
        body, name=name, grid=(bsz,),
        in_specs=[pl.BlockSpec((s_len, A_WIDTH), lambda b: (b, 0)), pl.BlockSpec((s_len, 128), lambda b: (b, 4)),
                  pl.BlockSpec((s_len, 128), lambda b: (b, 5)), pl.BlockSpec((s_len, A_WIDTH), lambda b: (b, 0)),
                  pl.BlockSpec(memory_space=pltpu.SMEM)],
        out_specs=[pl.BlockSpec((s_len, A_WIDTH), lambda b: (b, 0)), pl.BlockSpec((s_len, 128), lambda b: (b, 0)),
                   pl.BlockSpec((s_len, 128), lambda b: (b, 0)), pl.BlockSpec((1, 1, 128), lambda b: (b, 0, 0))],
        out_shape=[jax.ShapeDtypeStruct((t, A_WIDTH), F32), kv, kv, jax.ShapeDtypeStruct((bsz, 1, 128), F32)],
        scratch_shapes=[pltpu.VMEM((s_len + pad, A_KV_WIDTH), F32)] * 4,
        compiler_params=_cp("parallel"),
    )(proj, proj, proj, dcat, sinks)


def _lru_gates(x, cw, cb, wa, ba, wx, bx, lam, row):
    xc = _conv_fwd(x, cw, row) + cb
    r = _sigmoid(_mdot(xc, wa) + ba)
    i = _sigmoid(_mdot(xc, wx) + bx)
    sp = _softplus(-lam)
    log_a = -RG_C * r * sp
    a = jnp.exp(log_a)
    mult = jnp.sqrt(_neg_expm1(2.0 * log_a))
    return xc, r, i, sp, a, mult


def _lru_scan(a, u, row):
    n = a.shape[0]
    d = 1
    while d < n:
        a_sh = jnp.where(row >= d, pltpu.roll(a, d, 0), 1.0)
        u_sh = jnp.where(row >= d, pltpu.roll(u, d, 0), 0.0)
        u = a * u_sh + u
        a = a * a_sh
        d *= 2
    return u


def _lru_scan_rev(a, u, row):
    n = a.shape[0]
    d = 1
    while d < n:
        a_sh = jnp.where(row < n - d, pltpu.roll(a, n - d, 0), 1.0)
        u_sh = jnp.where(row < n - d, pltpu.roll(u, n - d, 0), 0.0)
        u = a * u_sh + u
        a = a * a_sh
        d *= 2
    return u


def _lru_specs(s_len, order):
    def at(f):
        return lambda *g: f(*order(*g))
    return [pl.BlockSpec((s_len, 128), at(lambda b, cb: (b, 6 + cb))), pl.BlockSpec((s_len, 128), at(lambda b, cb: (b, 10 + cb))),
            pl.BlockSpec((4, 128), at(lambda b, cb: (0, cb))), pl.BlockSpec((1, 128), at(lambda b, cb: (0, cb))),
            pl.BlockSpec((1, 128, 128), at(lambda b, cb: (cb, 0, 0))), pl.BlockSpec((1, 128), at(lambda b, cb: (0, cb))),
            pl.BlockSpec((1, 128, 128), at(lambda b, cb: (cb, 0, 0))), pl.BlockSpec((1, 128), at(lambda b, cb: (0, cb))),
            pl.BlockSpec((1, 128), at(lambda b, cb: (0, cb)))]


def lru_fwd(proj, cw, cb, wa, ba, wx, bxb, lam, bsz, *, name):
    t = proj.shape[0]
    s_len = t // bsz

    def body(x_ref, g_ref, cw_ref, cb_ref, wa_ref, ba_ref, wx_ref, bx_ref, lam_ref, y_ref):
        row = lax.broadcasted_iota(jnp.int32, (s_len, 128), 0)
        xc, r, i, sp, a, mult = _lru_gates(x_ref[...], cw_ref[...], cb_ref[...], wa_ref[0], ba_ref[...], wx_ref[0],
                                           bx_ref[...], lam_ref[...], row)
        h = _lru_scan(a, mult * (i * xc), row)
        y_ref[...] = h * _gelu(g_ref[...])[0]

    return _pcall(
        body, name=name, grid=(bsz, 4), in_specs=_lru_specs(s_len, lambda b, cb: (b, cb)),
        out_specs=pl.BlockSpec((s_len, 128), lambda b, cb: (b, cb)),
        out_shape=jax.ShapeDtypeStruct((t, B_WIDTH), F32), compiler_params=_cp("parallel", "parallel"),
    )(proj, proj, cw, cb.reshape(1, -1), wa, ba.reshape(1, -1), wx, bxb.reshape(1, -1), lam.reshape(1, -1))


def lru_bwd(proj, cw, cb, wa, ba, wx, bxb, lam, dcat, bsz, *, name):
    t = proj.shape[0]
    s_len = t // bsz

    def body(x_ref, g_ref, cw_ref, cb_ref, wa_ref, ba_ref, wx_ref, bx_ref, lam_ref, dy_ref,
             dx_ref, dg_ref, dcw_ref, dcb_ref, dwa_ref, dba_ref, dwx_ref, dbx_ref, dlam_ref):
        b = pl.program_id(1)
        row = lax.broadcasted_iota(jnp.int32, (s_len, 128), 0)
        x = x_ref[...]
        lam = lam_ref[...]
        xc, r, i, sp, a, mult = _lru_gates(x, cw_ref[...], cb_ref[...], wa_ref[0], ba_ref[...], wx_ref[0], bx_ref[...],
                                           lam, row)
        ixc = i * xc
        h = _lru_scan(a, mult * ixc, row)
        gv = g_ref[...]
        gl, th = _gelu(gv)
        dy = dy_ref[...]
        dg_ref[...] = dy * h * _gelu_grad(gv, th)
        gr = _lru_scan_rev(_shift_up(a, 1, row), dy * gl, row)
        da = gr * _shift_dn(h, 1, row)
        dmult = gr * ixc
        di = gr * mult * xc
        dxc = gr * mult * i
        dlog_a = da * a - dmult * (a * a) / mult
        dr = dlog_a * (-RG_C * sp)
        dlam = jnp.sum(dlog_a * r, axis=0, keepdims=True) * (RG_C * _sigmoid(-lam))
        dpa = dr * r * (1.0 - r)
        dpx = di * i * (1.0 - i)
        dxc = dxc + _mdot(dpa, wa_ref[0], NT) + _mdot(dpx, wx_ref[0], NT)
        dx, dcw = _conv_bwd(x, cw_ref[...], dxc, row)
        dx_ref[...] = dx

        @pl.when(b == 0)
        def _():
            for ref in (dcw_ref, dcb_ref, dwa_ref, dba_ref, dwx_ref, dbx_ref, dlam_ref):
                ref[...] = jnp.zeros_like(ref)

        dcw_ref[...] += dcw
        dcb_ref[...] += jnp.sum(dxc, axis=0, keepdims=True)
        dwa_ref[0] += _mdot(xc, dpa, TN)
        dwx_ref[0] += _mdot(xc, dpx, TN)
        dba_ref[...] += jnp.sum(dpa, axis=0, keepdims=True)
        dbx_ref[...] += jnp.sum(dpx, axis=0, keepdims=True)
        dlam_ref[...] += dlam

    order = lambda cb, b: (b, cb)
    act = pl.BlockSpec((s_len, 128), lambda cb, b: (b, cb))
    vec = pl.BlockSpec((1, 128), lambda cb, b: (0, cb))
    mat = pl.BlockSpec((1, 128, 128), lambda cb, b: (cb, 0, 0))
    vshape = jax.ShapeDtypeStruct((1, B_WIDTH), F32)
    mshape = jax.ShapeDtypeStruct((4, 128, 128), F32)
    return _pcall(
        body, name=name, grid=(4, bsz),
        in_specs=_lru_specs(s_len, order) + [pl.BlockSpec((s_len, 128), lambda cb, b: (b, 4 + cb))],
        out_specs=[act, act, pl.BlockSpec((4, 128), lambda cb, b: (0, cb)), vec, mat, vec, mat, vec, vec],
        out_shape=[jax.ShapeDtypeStruct((t, B_WIDTH), F32), jax.ShapeDtypeStruct((t, B_WIDTH), F32),
                   jax.ShapeDtypeStruct((4, B_WIDTH), F32), vshape, mshape, vshape, mshape, vshape, vshape],
        compiler_params=_cp("parallel", "arbitrary"),
    )(proj, proj, cw, cb.reshape(1, -1), wa, ba.reshape(1, -1), wx, bxb.reshape(1, -1), lam.reshape(1, -1), dcat)


_BDIMS = {"nn": ((2,), (1,)), "nt": ((2,), (2,)), "tn": ((1,), (1,))}
C_QSCALE = C_HEAD_DIM ** -0.5


def _bmm(a, b, mode, exact=False):
    dims = (_BDIMS[mode], ((0,), (0,)))
    if exact:
        return lax.dot_general(a, b, dims, preferred_element_type=F32, precision=lax.Precision.HIGH)
    return lax.dot_general(a.astype(MM), b.astype(MM), dims, preferred_element_type=F32)


def _col(x, idx, lane):
    return jnp.broadcast_to(jnp.sum(jnp.where(lane == idx, x, 0.0), axis=-1, keepdims=True), x.shape)


def _seg_cumsum(g, row):
    pos = row & (CHUNK - 1)
    d = 1
    while d < CHUNK:
        g = g + jnp.where(pos >= d, pltpu.roll(g, d, 0), 0.0)
        d *= 2
    return g


def _seg_cumsum_rev(g, row):
    pos = row & (CHUNK - 1)
    n = g.shape[0]
    d = 1
    while d < CHUNK:
        g = g + jnp.where(pos < CHUNK - d, pltpu.roll(g, n - d, 0), 0.0)
        d *= 2
    return g


def _gdn_prep(qr, kr, vr, gates, cwq, cwk, cwv, a_log, dtb, h):
    s_len = qr.shape[0]
    nc = s_len // CHUNK
    row = lax.broadcasted_iota(jnp.int32, (s_len, 128), 0)
    lane = lax.broadcasted_iota(jnp.int32, (s_len, 128), 1)
    r = {"row": row, "lane": lane}
    for nm, x, w in (("q", qr, cwq), ("k", kr, cwk), ("v", vr, cwv)):
        c = _conv_fwd(x, w, row)
        sg = _sigmoid(c)
        r["c" + nm], r["s" + nm], r[nm + "c"] = c, sg, c * sg
    r["rq"] = lax.rsqrt(jnp.sum(r["qc"] * r["qc"], axis=-1, keepdims=True) + NORM_EPS)
    r["rk"] = lax.rsqrt(jnp.sum(r["kc"] * r["kc"], axis=-1, keepdims=True) + NORM_EPS)
    r["qn"] = r["qc"] * r["rq"]
    r["kn"] = r["kc"] * r["rk"]
    r["beta"] = _sigmoid(_col(gates, h, lane))
    r["A"] = jnp.exp(a_log)
    r["pre"] = _col(gates, 8 + h, lane) + dtb
    r["sp"] = _softplus(r["pre"])
    gc = _seg_cumsum(-r["A"] * r["sp"], row)
    sh = (nc, CHUNK, 128)
    q3 = (r["qn"] * C_QSCALE).reshape(sh)
    k3 = r["kn"].reshape(sh)
    v3 = r["vc"].reshape(sh)
    beta3 = r["beta"].reshape(sh)
    gc3 = gc.reshape(sh)
    gcl3 = gc3[:, CHUNK - 1:CHUNK, :]
    eg = jnp.exp(gc3)
    ekd = jnp.exp(gcl3 - gc3)
    col64 = gc3[:, :, :CHUNK]
    row64 = jnp.swapaxes(gc3, 1, 2)[:, :CHUNK, :]
    ii = lax.broadcasted_iota(jnp.int32, (nc, CHUNK, CHUNK), 1)
    jj = lax.broadcasted_iota(jnp.int32, (nc, CHUNK, CHUNK), 2)
    tril = ii >= jj
    strict = ii > jj
    dm = jnp.where(tril, jnp.exp(jnp.where(tril, col64 - row64, 0.0)), 0.0)
    kb = k3 * beta3
    lmat = jnp.where(strict, _bmm(kb, k3, "nt") * dm, 0.0)
    attn = _bmm(q3, k3, "nt") * dm
    r.update(q3=q3, k3=k3, v3=v3, beta3=beta3, gc3=gc3, eg=eg, ekd=ekd, gl=jnp.exp(gcl3), dm=dm, kb=kb, lmat=lmat,
             attn=attn, strict=strict, tril=tril, qg=q3 * eg, kdec=k3 * ekd)
    return r


def _neumann_inverse(lmat):
    ii = lax.broadcasted_iota(jnp.int32, lmat.shape, 1)
    jj = lax.broadcasted_iota(jnp.int32, lmat.shape, 2)
    x = -lmat
    tm = jnp.where(ii == jj, 1.0, 0.0) + x
    pw = x
    for _ in range(5):
        pw = _bmm(pw, pw, "nn", exact=True)
        tm = tm + _bmm(tm, pw, "nn", exact=True)
    return tm


def _gdn_specs(s_len):
    act = lambda off: pl.BlockSpec((s_len, 128), lambda b, h: (b, off + h))
    cw = lambda off: pl.BlockSpec((4, 128), lambda b, h: (0, off + h))
    smem = pl.BlockSpec(memory_space=pltpu.SMEM)
    return [act(0), act(8), act(16), act(24), pl.BlockSpec((s_len, 128), lambda b, h: (b, 0)), cw(0), cw(8), cw(16),
            smem, smem, pl.BlockSpec((1, 128), lambda b, h: (0, 0))]


def gdn_fwd(proj, gates, cw, a_log, dtb, ng, bsz, *, name):
    t = proj.shape[0]
    s_len = t // bsz
    nc = s_len // CHUNK

    def body(q_ref, k_ref, v_ref, z_ref, gt_ref, cwq_ref, cwk_ref, cwv_ref, al_ref, dt_ref, ng_ref,
             out_ref, o_ref, vn_ref, tm_ref, st_ref, u_s, w_s, qg_s, kd_s, at_s, gl_s):
        h = pl.program_id(1)
        r = _gdn_prep(q_ref[...], k_ref[...], v_ref[...], gt_ref[...], cwq_ref[...], cwk_ref[...], cwv_ref[...],
                      al_ref[h], dt_ref[h], h)
        tm = _neumann_inverse(r["lmat"])
        tm_ref[0, 0] = tm
        u_s[...] = _bmm(tm, r["v3"] * r["beta3"], "nn", exact=True)
        w_s[...] = _bmm(tm, r["kb"] * r["eg"], "nn", exact=True)
        qg_s[...] = r["qg"]
        kd_s[...] = r["kdec"]
        at_s[...] = r["attn"]
        gl_s[...] = r["gl"]

        def chunk(n, state):
            st = pl.multiple_of(n * CHUNK, CHUNK)
            st_ref[0, 0, n] = state
            v_new = u_s[n] - _mdot(w_s[n], state)
            o_ref[pl.ds(st, CHUNK), :] = _mdot(qg_s[n], state) + _mdot(at_s[n], v_new)
            vn_ref[pl.ds(st, CHUNK), :] = v_new
            return state * gl_s[n] + _mdot(kd_s[n], v_new, TN)

        lax.fori_loop(0, nc, chunk, jnp.zeros((128, 128), F32))
        o = o_ref[...]
        rms = lax.rsqrt(jnp.mean(o * o, axis=-1, keepdims=True) + NORM_EPS)
        z = z_ref[...]
        out_ref[...] = o * rms * ng_ref[...] * (z * _sigmoid(z))

    blk = pl.BlockSpec((s_len, 128), lambda b, h: (b, h))
    full = jax.ShapeDtypeStruct((t, C_WIDTH), F32)
    return _pcall(
        body, name=name, grid=(bsz, C_HEADS), in_specs=_gdn_specs(s_len),
        out_specs=[blk, blk, blk, pl.BlockSpec((1, 1, nc, CHUNK, CHUNK), lambda b, h: (b, h, 0, 0, 0)),
                   pl.BlockSpec((1, 1, nc, 128, 128), lambda b, h: (b, h, 0, 0, 0))],
        out_shape=[full, full, full, jax.ShapeDtypeStruct((bsz, C_HEADS, nc, CHUNK, CHUNK), F32),
                   jax.ShapeDtypeStruct((bsz, C_HEADS, nc, 128, 128), F32)],
        scratch_shapes=[pltpu.VMEM((nc, CHUNK, 128), F32)] * 4 + [pltpu.VMEM((nc, CHUNK, CHUNK), F32),
                                                                   pltpu.VMEM((nc, 1, 128), F32)],
        compiler_params=_cp("parallel", "parallel"),
    )(proj, proj, proj, proj, gates, cw, cw, cw, a_log, dtb, ng.reshape(1, 128))


def gdn_bwd(proj, gates, cw, a_log, dtb, ng, o_pre, vnew, tmat, states, dout, bsz, *, name):
    t = proj.shape[0]
    s_len = t // bsz
    nc = s_len // CHUNK

    def body(q_ref, k_ref, v_ref, z_ref, gt_ref, cwq_ref, cwk_ref, cwv_ref, al_ref, dt_ref, ng_ref,
             o_ref, vn_ref, tm_ref, st_ref, do_ref,
             dq_ref, dk_ref, dv_ref, dz_ref, dgt_ref, dcq_ref, dck_ref, dcv_ref, dsm_ref,
             w_s, qg_s, kd_s, at_s, gl_s, dop_s, du_s, dw_s, dat_s, dqg_s, dkd_s, dgl_s):
        h = pl.program_id(1)
        qr, kr, vr = q_ref[...], k_ref[...], v_ref[...]
        r = _gdn_prep(qr, kr, vr, gt_ref[...], cwq_ref[...], cwk_ref[...], cwv_ref[...], al_ref[h], dt_ref[h], h)
        row, lane = r["row"], r["lane"]
        tm = tm_ref[0, 0]
        q3, k3, v3, beta3, eg, kb, dm = r["q3"], r["k3"], r["v3"], r["beta3"], r["eg"], r["kb"], r["dm"]
        u3 = _bmm(tm, v3 * beta3, "nn", exact=True)
        w3 = _bmm(tm, kb * eg, "nn", exact=True)
        w_s[...] = w3
        qg_s[...] = r["qg"]
        kd_s[...] = r["kdec"]
        at_s[...] = r["attn"]
        gl_s[...] = r["gl"]

        z = z_ref[...]
        sz = _sigmoid(z)
        o = o_ref[...]
        rms = lax.rsqrt(jnp.mean(o * o, axis=-1, keepdims=True) + NORM_EPS)
        on = o * rms
        dout_v = do_ref[...]
        ngv = ng_ref[...]
        dz_ref[...] = dout_v * on * ngv * (sz * (1.0 + z * (1.0 - sz)))
        dos = dout_v * (z * sz)
        dng = jnp.sum(dos * on, axis=0, keepdims=True)
        don = dos * ngv
        dop_s[...] = (rms * (don - on * jnp.mean(don * on, axis=-1, keepdims=True))).reshape(nc, CHUNK, 128)

        def chunk(i, dstate):
            n = nc - 1 - i
            st = pl.multiple_of(n * CHUNK, CHUNK)
            state = st_ref[0, 0, n]
            vn = vn_ref[pl.ds(st, CHUNK), :]
            do_n = dop_s[n]
            dvn = _mdot(at_s[n], do_n, TN) + _mdot(kd_s[n], dstate)
            du_s[n] = dvn
            dat_s[n] = _mdot(do_n, vn, NT)
            dqg_s[n] = _mdot(do_n, state, NT)
            dkd_s[n] = _mdot(vn, dstate, NT)
            dgl_s[n] = jnp.broadcast_to(jnp.sum(jnp.sum(state * dstate, axis=1, keepdims=True), axis=0, keepdims=True), (1, 128))
            dw_s[n] = -_mdot(dvn, state, NT)
            return dstate * gl_s[n] + _mdot(qg_s[n], do_n, TN) - _mdot(w_s[n], dvn, TN)

        lax.fori_loop(0, nc, chunk, jnp.zeros((128, 128), F32))

        du, dw, dqg, dkd = du_s[...], dw_s[...], dqg_s[...], dkd_s[...]
        dat = jnp.where(r["tril"], dat_s[...], 0.0)
        dvb = _bmm(tm, du, "tn", exact=True)
        dkbg = _bmm(tm, dw, "tn", exact=True)
        dl = -jnp.where(r["strict"], _bmm(dvb, u3, "nt") + _bmm(dkbg, w3, "nt"), 0.0)
        dml = dl * dm
        dn = dat * dm
        dkb = _bmm(dml, k3, "nn") + dkbg * eg
        dk3 = _bmm(dml, kb, "tn") + _bmm(dn, q3, "tn") + dkd * r["ekd"] + dkb * beta3
        dq3 = dqg * eg + _bmm(dn, k3, "nn")
        e = dl * r["lmat"] + dat * r["attn"]
        ones = jnp.ones((nc, CHUNK, 128), F32)
        colsum = lax.dot_general(e, ones, (_BDIMS["tn"], ((0,), (0,))), preferred_element_type=F32, precision=HI)
        dgc = jnp.sum(e, axis=-1, keepdims=True) - colsum
        dgc = dgc + eg * (jnp.sum(dqg * q3, axis=-1, keepdims=True) + jnp.sum(dkbg * kb, axis=-1, keepdims=True))
        skd = jnp.sum(dkd * r["kdec"], axis=-1, keepdims=True)
        dgcl = jnp.sum(skd, axis=1, keepdims=True) + dgl_s[...] * r["gl"]
        pos3 = lax.broadcasted_iota(jnp.int32, (nc, CHUNK, 128), 1)
        dgc = dgc - skd + jnp.where(pos3 == CHUNK - 1, dgcl, 0.0)
        dbeta = jnp.sum(dkb * k3, axis=-1, keepdims=True) + jnp.sum(dvb * v3, axis=-1, keepdims=True)
        dv3 = dvb * beta3

        dg = _seg_cumsum_rev(dgc.reshape(s_len, 128), row)
        beta = r["beta"]
        dbl = jnp.broadcast_to(dbeta, (nc, CHUNK, 128)).reshape(s_len, 128) * beta * (1.0 - beta)
        dai = dg * (-r["A"]) * _sigmoid(r["pre"])
        d_dtb = jnp.sum(dai, axis=0, keepdims=True)
        d_alog = jnp.sum(dg * (-r["sp"]), axis=0, keepdims=True) * r["A"]

        @pl.when(h == 0)
        def _():
            dgt_ref[...] = jnp.zeros_like(dgt_ref)
            dsm_ref[...] = jnp.zeros_like(dsm_ref)

        dgt_ref[...] += jnp.where(lane == h, dbl, 0.0) + jnp.where(lane == 8 + h, dai, 0.0)
        r16 = lax.broadcasted_iota(jnp.int32, (16, 128), 0)
        l16 = lax.broadcasted_iota(jnp.int32, (16, 128), 1)
        small = jnp.where((r16 == h) & (l16 == 0), d_alog, 0.0) + jnp.where((r16 == h) & (l16 == 1), d_dtb, 0.0)
        dsm_ref[0] += small + jnp.where(r16 == 8 + h, dng, 0.0)

        dqn = dq3.reshape(s_len, 128) * C_QSCALE
        dkn = dk3.reshape(s_len, 128)
        dqc = r["rq"] * (dqn - r["qn"] * jnp.sum(dqn * r["qn"], axis=-1, keepdims=True))
        dkc = r["rk"] * (dkn - r["kn"] * jnp.sum(dkn * r["kn"], axis=-1, keepdims=True))
        dvc = dv3.reshape(s_len, 128)
        for nm, x, w_ref, dxc, dx_ref, dc_ref in (("q", qr, cwq_ref, dqc, dq_ref, dcq_ref), ("k", kr, cwk_ref, dkc, dk_ref, dck_ref),
                                                 ("v", vr, cwv_ref, dvc, dv_ref, dcv_ref)):
            c, sg = r["c" + nm], r["s" + nm]
            dc = dxc * (sg * (1.0 + c * (1.0 - sg)))
            dx, dwc = _conv_bwd(x, w_ref[...], dc, row)
            dx_ref[...] = dx
            dc_ref[0] = dwc

    blk = pl.BlockSpec((s_len, 128), lambda b, h: (b, h))
    full = jax.ShapeDtypeStruct((t, C_WIDTH), F32)
    cwo = pl.BlockSpec((1, 4, 128), lambda b, h: (b, 0, h))
    cws = jax.ShapeDtypeStruct((bsz, 4, C_WIDTH), F32)
    c128 = pltpu.VMEM((nc, CHUNK, 128), F32)
    outs = _pcall(
        body, name=name, grid=(bsz, C_HEADS),
        in_specs=_gdn_specs(s_len) + [blk, blk, pl.BlockSpec((1, 1, nc, CHUNK, CHUNK), lambda b, h: (b, h, 0, 0, 0)),
                                      pl.BlockSpec((1, 1, nc, 128, 128), lambda b, h: (b, h, 0, 0, 0)), blk],
        out_specs=[blk, blk, blk, blk, pl.BlockSpec((s_len, 128), lambda b, h: (b, 0)), cwo, cwo, cwo,
                   pl.BlockSpec((1, 16, 128), lambda b, h: (b, 0, 0))],
        out_shape=[full, full, full, full, jax.ShapeDtypeStruct((t, 128), F32), cws, cws, cws,
                   jax.ShapeDtypeStruct((bsz, 16, 128), F32)],
        scratch_shapes=[c128, c128, c128, pltpu.VMEM((nc, CHUNK, CHUNK), F32), pltpu.VMEM((nc, 1, 128), F32), c128,
                        c128, c128, pltpu.VMEM((nc, CHUNK, CHUNK), F32), c128, c128, pltpu.VMEM((nc, 1, 128), F32)],
        compiler_params=_cp("parallel", "arbitrary"),
    )(proj, proj, proj, proj, gates, cw, cw, cw, a_log, dtb, ng.reshape(1, 128), o_pre, vnew, tmat, states, dout)
    dq, dk, dv, dz, dgates, dcq, dck, dcv, dsm = outs
    return dq, dk, dv, dz, dgates, jnp.concatenate([dcq, dck, dcv], axis=-1), dsm


def gdc_pre_fwd(proj, cw, bsz, *, name):
    t = proj.shape[0]
    s_len = t // bsz

    def body(x_ref, w_ref, y_ref):
        row = lax.broadcasted_iota(jnp.int32, (s_len, 128), 0)
        c = _conv_fwd(x_ref[...], w_ref[...], row)
        xc = c * _sigmoid(c)
        rn = lax.rsqrt(jnp.sum(xc * xc, axis=-1, keepdims=True) + NORM_EPS)
        y_ref[...] = jnp.where(pl.program_id(1) < 2 * C_HEADS, xc * rn, xc)

    blk = pl.BlockSpec((s_len, 128), lambda b, j: (b, j))
    return _pcall(
        body, name=name, grid=(bsz, 3 * C_HEADS), in_specs=[blk, pl.BlockSpec((4, 128), lambda b, j: (0, j))],
        out_specs=blk, out_shape=jax.ShapeDtypeStruct((t, 3 * C_WIDTH), F32), compiler_params=_cp("parallel", "parallel"),
    )(proj, cw)


def gdc_pre_bwd(proj, cw, dy, bsz, *, name):
    t = proj.shape[0]
    s_len = t // bsz

    def body(x_ref, w_ref, dy_ref, dx_ref, dw_ref):
        row = lax.broadcasted_iota(jnp.int32, (s_len, 128), 0)
        x = x_ref[...]
        c = _conv_fwd(x, w_ref[...], row)
        sg = _sigmoid(c)
        xc = c * sg
        rn = lax.rsqrt(jnp.sum(xc * xc, axis=-1, keepdims=True) + NORM_EPS)
        dyv = dy_ref[...]
        xn = xc * rn
        dxc = jnp.where(pl.program_id(1) < 2 * C_HEADS, rn * (dyv - xn * jnp.sum(dyv * xn, axis=-1, keepdims=True)), dyv)
        dc = dxc * (sg * (1.0 + c * (1.0 - sg)))
        dx, dw = _conv_bwd(x, w_ref[...], dc, row)
        dx_ref[...] = dx
        dw_ref[0] = dw

    blk = pl.BlockSpec((s_len, 128), lambda b, j: (b, j))
    return _pcall(
        body, name=name, grid=(bsz, 3 * C_HEADS), in_specs=[blk, pl.BlockSpec((4, 128), lambda b, j: (0, j)), blk],
        out_specs=[blk, pl.BlockSpec((1, 4, 128), lambda b, j: (b, 0, j))],
        out_shape=[jax.ShapeDtypeStruct((t, 3 * C_WIDTH), F32), jax.ShapeDtypeStruct((bsz, 4, 3 * C_WIDTH), F32)],
        compiler_params=_cp("parallel", "parallel"),
    )(proj, cw, dy)


GDC_GROUP = 8


def _gdc_local(qn, kn, vc, gates, a_log, dtb, h):
    rows = qn.shape[0]
    nc = rows // CHUNK
    row = lax.broadcasted_iota(jnp.int32, (rows, 128), 0)
    lane = lax.broadcasted_iota(jnp.int32, (rows, 128), 1)
    r = {"row": row, "lane": lane}
    r["beta"] = _sigmoid(_col(gates, h, lane))
    r["A"] = jnp.exp(a_log)
    r["pre"] = _col(gates, 8 + h, lane) + dtb
    r["sp"] = _softplus(r["pre"])
    gc = _seg_cumsum(-r["A"] * r["sp"], row)
    sh = (nc, CHUNK, 128)
    q3 = (qn * C_QSCALE).reshape(sh)
    k3 = kn.reshape(sh)
    v3 = vc.reshape(sh)
    beta3 = r["beta"].reshape(sh)
    gc3 = gc.reshape(sh)
    gcl3 = gc3[:, CHUNK - 1:CHUNK, :]
    eg = jnp.exp(gc3)
    ekd = jnp.exp(gcl3 - gc3)
    col64 = gc3[:, :, :CHUNK]
    row64 = jnp.swapaxes(gc3, 1, 2)[:, :CHUNK, :]
    ii = lax.broadcasted_iota(jnp.int32, (nc, CHUNK, CHUNK), 1)
    jj = lax.broadcasted_iota(jnp.int32, (nc, CHUNK, CHUNK), 2)
    tril = ii >= jj
    strict = ii > jj
    dm = jnp.where(tril, jnp.exp(jnp.where(tril, col64 - row64, 0.0)), 0.0)
    kb = k3 * beta3
    lmat = jnp.where(strict, _bmm(kb, k3, "nt") * dm, 0.0)
    attn = _bmm(q3, k3, "nt") * dm
    r.update(q3=q3, k3=k3, v3=v3, beta3=beta3, eg=eg, ekd=ekd, gl=jnp.exp(gcl3), dm=dm, kb=kb, lmat=lmat,
             attn=attn, strict=strict, tril=tril, qg=q3 * eg, kdec=k3 * ekd)
    return r


def _gdc_specs(s_len):
    act = lambda off: pl.BlockSpec((s_len, 128), lambda b, h: (b, off + h))
    smem = pl.BlockSpec(memory_space=pltpu.SMEM)
    return [act(0), act(8), act(16), act(24), pl.BlockSpec((s_len, 128), lambda b, h: (b, 0)), smem, smem,
            pl.BlockSpec((1, 128), lambda b, h: (0, 0))]


def gdc_fwd(qkv, proj, gates, a_log, dtb, ng, bsz, *, name):
    t = proj.shape[0]
    s_len = t // bsz
    nc = s_len // CHUNK
    grp = min(GDC_GROUP, nc)
    gr = grp * CHUNK

    def body(q_ref, k_ref, v_ref, z_ref, gt_ref, al_ref, dt_ref, ng_ref,
             out_ref, o_ref, vn_ref, tm_ref, st_ref, u_s, w_s, qg_s, kd_s, at_s, gl_s):
        h = pl.program_id(1)

        def local(gi, carry):
            rs = pl.ds(pl.multiple_of(gi * gr, gr), gr)
            cs = pl.ds(gi * grp, grp)
            r = _gdc_local(q_ref[rs, :], k_ref[rs, :], v_ref[rs, :], gt_ref[rs, :], al_ref[h], dt_ref[h], h)
            tm = _neumann_inverse(r["lmat"])
            tm_ref[0, 0, cs] = tm
            u_s[cs] = _bmm(tm, r["v3"] * r["beta3"], "nn", exact=True)
            w_s[cs] = _bmm(tm, r["kb"] * r["eg"], "nn", exact=True)
            qg_s[cs] = r["qg"]
            kd_s[cs] = r["kdec"]
            at_s[cs] = r["attn"]
            gl_s[cs] = r["gl"]
            return carry

        lax.fori_loop(0, nc // grp, local, 0)

        def chunk(n, state):
            st = pl.multiple_of(n * CHUNK, CHUNK)
            st_ref[0, 0, n] = state
            v_new = u_s[n] - _mdot(w_s[n], state)
            o_ref[pl.ds(st, CHUNK), :] = _mdot(qg_s[n], state) + _mdot(at_s[n], v_new)
            vn_ref[pl.ds(st, CHUNK), :] = v_new
            return state * gl_s[n] + _mdot(kd_s[n], v_new, TN)

        lax.fori_loop(0, nc, chunk, jnp.zeros((128, 128), F32))
        o = o_ref[...]
        rms = lax.rsqrt(jnp.mean(o * o, axis=-1, keepdims=True) + NORM_EPS)
        z = z_ref[...]
        out_ref[...] = o * rms * ng_ref[...] * (z * _sigmoid(z))

    blk = pl.BlockSpec((s_len, 128), lambda b, h: (b, h))
    full = jax.ShapeDtypeStruct((t, C_WIDTH), F32)
    return _pcall(
        body, name=name, grid=(bsz, C_HEADS), in_specs=_gdc_specs(s_len),
        out_specs=[blk, blk, blk, pl.BlockSpec((1, 1, nc, CHUNK, CHUNK), lambda b, h: (b, h, 0, 0, 0)),
                   pl.BlockSpec((1, 1, nc, 128, 128), lambda b, h: (b, h, 0, 0, 0))],
        out_shape=[full, full, full, jax.ShapeDtypeStruct((bsz, C_HEADS, nc, CHUNK, CHUNK), F32),
                   jax.ShapeDtypeStruct((bsz, C_HEADS, nc, 128, 128), F32)],
        scratch_shapes=[pltpu.VMEM((nc, CHUNK, 128), F32)] * 4 + [pltpu.VMEM((nc, CHUNK, CHUNK), F32),
                                                                   pltpu.VMEM((nc, 1, 128), F32)],
        compiler_params=_cp("parallel", "parallel"),
    )(qkv, qkv, qkv, proj, gates, a_log, dtb, ng.reshape(1, 128))


def gdc_bwd(qkv, proj, gates, a_log, dtb, ng, o_pre, vnew, tmat, states, dout, bsz, *, name):
    t = proj.shape[0]
    s_len = t // bsz
    nc = s_len // CHUNK
    grp = min(GDC_GROUP, nc)
    gr = grp * CHUNK

    def body(q_ref, k_ref, v_ref, z_ref, gt_ref, al_ref, dt_ref, ng_ref, o_ref, vn_ref, tm_ref, st_ref, do_ref,
             dq_ref, dk_ref, dv_ref, dz_ref, dgt_ref, dsm_ref,
             w_s, qg_s, kd_s, at_s, gl_s, dop_s, du_s, dw_s, dat_s, dqg_s, dkd_s, dgl_s):
        h = pl.program_id(1)
        a_log_h, dtb_h = al_ref[h], dt_ref[h]

        z = z_ref[...]
        sz = _sigmoid(z)
        o = o_ref[...]
        rms = lax.rsqrt(jnp.mean(o * o, axis=-1, keepdims=True) + NORM_EPS)
        on = o * rms
        dout_v = do_ref[...]
        ngv = ng_ref[...]
        dz_ref[...] = dout_v * on * ngv * (sz * (1.0 + z * (1.0 - sz)))
        dos = dout_v * (z * sz)
        dng = jnp.sum(dos * on, axis=0, keepdims=True)
        don = dos * ngv
        dop_s[...] = (rms * (don - on * jnp.mean(don * on, axis=-1, keepdims=True))).reshape(nc, CHUNK, 128)

        def local(gi, carry):
            rs = pl.ds(pl.multiple_of(gi * gr, gr), gr)
            cs = pl.ds(gi * grp, grp)
            r = _gdc_local(q_ref[rs, :], k_ref[rs, :], v_ref[rs, :], gt_ref[rs, :], a_log_h, dtb_h, h)
            w_s[cs] = _bmm(tm_ref[0, 0, cs], r["kb"] * r["eg"], "nn", exact=True)
            qg_s[cs] = r["qg"]
            kd_s[cs] = r["kdec"]
            at_s[cs] = r["attn"]
            gl_s[cs] = r["gl"]
            return carry

        lax.fori_loop(0, nc // grp, local, 0)

        def chunk(i, dstate):
            n = nc - 1 - i
            st = pl.multiple_of(n * CHUNK, CHUNK)
            state = st_ref[0, 0, n]
            vn = vn_ref[pl.ds(st, CHUNK), :]
            do_n = dop_s[n]
            dvn = _mdot(at_s[n], do_n, TN) + _mdot(kd_s[n], dstate)
            du_s[n] = dvn
            dat_s[n] = _mdot(do_n, vn, NT)
            dqg_s[n] = _mdot(do_n, state, NT)
            dkd_s[n] = _mdot(vn, dstate, NT)
            dgl_s[n] = jnp.broadcast_to(jnp.sum(jnp.sum(state * dstate, axis=1, keepdims=True), axis=0, keepdims=True), (1, 128))
            dw_s[n] = -_mdot(dvn, state, NT)
            return dstate * gl_s[n] + _mdot(qg_s[n], do_n, TN) - _mdot(w_s[n], dvn, TN)

        lax.fori_loop(0, nc, chunk, jnp.zeros((128, 128), F32))

        @pl.when(h == 0)
        def _():
            dgt_ref[...] = jnp.zeros_like(dgt_ref)
            dsm_ref[...] = jnp.zeros_like(dsm_ref)

        def local_bwd(gi, carry):
            d_alog, d_dtb = carry
            rs = pl.ds(pl.multiple_of(gi * gr, gr), gr)
            cs = pl.ds(gi * grp, grp)
            r = _gdc_local(q_ref[rs, :], k_ref[rs, :], v_ref[rs, :], gt_ref[rs, :], a_log_h, dtb_h, h)
            row, lane = r["row"], r["lane"]
            q3, k3, v3, beta3, eg, kb, dm = r["q3"], r["k3"], r["v3"], r["beta3"], r["eg"], r["kb"], r["dm"]
            tm = tm_ref[0, 0, cs]
            u3 = _bmm(tm, v3 * beta3, "nn", exact=True)
            w3 = w_s[cs]
            du, dw, dqg, dkd = du_s[cs], dw_s[cs], dqg_s[cs], dkd_s[cs]
            dat = jnp.where(r["tril"], dat_s[cs], 0.0)
            dvb = _bmm(tm, du, "tn", exact=True)
            dkbg = _bmm(tm, dw, "tn", exact=True)
            dl = -jnp.where(r["strict"], _bmm(dvb, u3, "nt") + _bmm(dkbg, w3, "nt"), 0.0)
            dml = dl * dm
            dn = dat * dm
            dkb = _bmm(dml, k3, "nn") + dkbg * eg
            dk3 = _bmm(dml, kb, "tn") + _bmm(dn, q3, "tn") + dkd * r["ekd"] + dkb * beta3
            dq3 = dqg * eg + _bmm(dn, k3, "nn")
            e = dl * r["lmat"] + dat * r["attn"]
            ones = jnp.ones((grp, CHUNK, 128), F32)
            colsum = lax.dot_general(e, ones, (_BDIMS["tn"], ((0,), (0,))), preferred_element_type=F32, precision=HI)
            dgc = jnp.sum(e, axis=-1, keepdims=True) - colsum
            dgc = dgc + eg * (jnp.sum(dqg * q3, axis=-1, keepdims=True) + jnp.sum(dkbg * kb, axis=-1, keepdims=True))
            skd = jnp.sum(dkd * r["kdec"], axis=-1, keepdims=True)
            dgcl = jnp.sum(skd, axis=1, keepdims=True) + dgl_s[cs] * r["gl"]
            pos3 = lax.broadcasted_iota(jnp.int32, (grp, CHUNK, 128), 1)
            dgc = dgc - skd + jnp.where(pos3 == CHUNK - 1, dgcl, 0.0)
            dbeta = jnp.sum(dkb * k3, axis=-1, keepdims=True) + jnp.sum(dvb * v3, axis=-1, keepdims=True)
            dg = _seg_cumsum_rev(dgc.reshape(gr, 128), row)
            beta = r["beta"]
            dbl = jnp.broadcast_to(dbeta, (grp, CHUNK, 128)).reshape(gr, 128) * beta * (1.0 - beta)
            dai = dg * (-r["A"]) * _sigmoid(r["pre"])
            dgt_ref[rs, :] += jnp.where(lane == h, dbl, 0.0) + jnp.where(lane == 8 + h, dai, 0.0)
            dq_ref[rs, :] = dq3.reshape(gr, 128) * C_QSCALE
            dk_ref[rs, :] = dk3.reshape(gr, 128)
            dv_ref[rs, :] = (dvb * beta3).reshape(gr, 128)
            return (d_alog + jnp.sum(dg * (-r["sp"]), axis=0, keepdims=True) * r["A"],
                    d_dtb + jnp.sum(dai, axis=0, keepdims=True))

        zero = jnp.zeros((1, 128), F32)
        d_alog, d_dtb = lax.fori_loop(0, nc // grp, local_bwd, (zero, zero))
        r16 = lax.broadcasted_iota(jnp.int32, (16, 128), 0)
        l16 = lax.broadcasted_iota(jnp.int32, (16, 128), 1)
        small = jnp.where((r16 == h) & (l16 == 0), d_alog, 0.0) + jnp.where((r16 == h) & (l16 == 1), d_dtb, 0.0)
        dsm_ref[0] += small + jnp.where(r16 == 8 + h, dng, 0.0)

    blk = pl.BlockSpec((s_len, 128), lambda b, h: (b, h))
    blk3 = lambda off: pl.BlockSpec((s_len, 128), lambda b, h: (b, off + h))
    full = jax.ShapeDtypeStruct((t, C_WIDTH), F32)
    c128 = pltpu.VMEM((nc, CHUNK, 128), F32)
    dq, dk, dv, dz, dgates, dsm = _pcall(
        body, name=name, grid=(bsz, C_HEADS),
        in_specs=_gdc_specs(s_len) + [blk, blk, pl.BlockSpec((1, 1, nc, CHUNK, CHUNK), lambda b, h: (b, h, 0, 0, 0)),
                                      pl.BlockSpec((1, 1, nc, 128, 128), lambda b, h: (b, h, 0, 0, 0)), blk],
        out_specs=[blk, blk, blk, blk, pl.BlockSpec((s_len, 128), lambda b, h: (b, 0)),
                   pl.BlockSpec((1, 16, 128), lambda b, h: (b, 0, 0))],
        out_shape=[full, full, full, full, jax.ShapeDtypeStruct((t, 128), F32), jax.ShapeDtypeStruct((bsz, 16, 128), F32)],
        scratch_shapes=[c128, c128, c128, pltpu.VMEM((nc, CHUNK, CHUNK), F32), pltpu.VMEM((nc, 1, 128), F32), c128,
                        c128, c128, pltpu.VMEM((nc, CHUNK, CHUNK), F32), c128, c128, pltpu.VMEM((nc, 1, 128), F32)],
        compiler_params=_cp("parallel", "arbitrary"),
    )(qkv, qkv, qkv, proj, gates, a_log, dtb, ng.reshape(1, 128), o_pre, vnew, tmat, states, dout)
    return jnp.concatenate([dq, dk, dv], axis=-1), dz, dgates, dsm


MESH_ID = pl.DeviceIdType.MESH
_FLIPS = [(0, 0, 1), (1, 0, 0), (0, 1, 0), (1, 1, 0), (1, 0, 1), (0, 1, 1), (1, 1, 1)]


def _me():
    return lax.axis_index("x"), lax.axis_index("y"), lax.axis_index("c")


def _flip(coord, d):
    return 1 - coord if d else coord


def all_gather(shard, *, name):
    def body(x_ref, o_ref, send_sems, recv_sems, local_sem):
        x, y, c = _me()
        mine = 4 * x + 2 * y + c
        own = pltpu.make_async_copy(x_ref, o_ref.at[mine], local_sem)
        own.start()
        copies = []
        for k, (dx, dy, dc) in enumerate(_FLIPS):
            cp = pltpu.make_async_remote_copy(
                src_ref=x_ref, dst_ref=o_ref.at[mine], send_sem=send_sems.at[k], recv_sem=recv_sems.at[k],
                device_id=(_flip(x, dx), _flip(y, dy), _flip(c, dc)), device_id_type=MESH_ID)
            cp.start()
            copies.append(cp)
        for cp in copies:
            cp.wait()
        own.wait()

    hbm = pl.BlockSpec(memory_space=pl.ANY)
    return _pcall(
        body, name=name, in_specs=[hbm], out_specs=hbm,
        out_shape=jax.ShapeDtypeStruct((N_DEV,) + shard.shape, shard.dtype),
        scratch_shapes=[pltpu.SemaphoreType.DMA((7,)), pltpu.SemaphoreType.DMA((7,)), pltpu.SemaphoreType.DMA(())],
    )(shard)


def all_to_all(parts, *, name):
    def body(x_ref, o_ref, send_sems, recv_sems, local_sem):
        x, y, c = _me()
        mine = 4 * x + 2 * y + c
        own = pltpu.make_async_copy(x_ref.at[mine], o_ref.at[mine], local_sem)
        own.start()
        copies = []
        for k, (dx, dy, dc) in enumerate(_FLIPS):
            px, py, pc = _flip(x, dx), _flip(y, dy), _flip(c, dc)
            cp = pltpu.make_async_remote_copy(
                src_ref=x_ref.at[4 * px + 2 * py + pc], dst_ref=o_ref.at[mine], send_sem=send_sems.at[k],
                recv_sem=recv_sems.at[k], device_id=(px, py, pc), device_id_type=MESH_ID)
            cp.start()
            copies.append(cp)
        for cp in copies:
            cp.wait()
        own.wait()

    hbm = pl.BlockSpec(memory_space=pl.ANY)
    return _pcall(
        body, name=name, in_specs=[hbm], out_specs=hbm, out_shape=jax.ShapeDtypeStruct(parts.shape, parts.dtype),
        scratch_shapes=[pltpu.SemaphoreType.DMA((7,)), pltpu.SemaphoreType.DMA((7,)), pltpu.SemaphoreType.DMA(())],
    )(parts)


def adamw_sum(parts, w, m, v, *, name, tr=256):
    r, cdim = w.shape
    tr = _tile8(r, tr)

    def body(p_ref, w_ref, m_ref, v_ref, g_ref, d_ref, mo_ref, vo_ref):
        g = p_ref[0].astype(F32)
        for j in range(1, N_DEV):
            g = g + p_ref[j].astype(F32)
        g_ref[...] = g
        mn = ADAM_B1 * m_ref[...] + (1.0 - ADAM_B1) * g
        vn = ADAM_B2 * v_ref[...] + (1.0 - ADAM_B2) * (g * g)
        mo_ref[...] = mn
        vo_ref[...] = vn
        m_hat = mn / (1.0 - ADAM_B1 ** ADAM_STEP)
        v_hat = vn / (1.0 - ADAM_B2 ** ADAM_STEP)
        d_ref[...] = -ADAM_LR * (m_hat / (jnp.sqrt(v_hat) + ADAM_EPS) + ADAM_WD * w_ref[...])

    blk = pl.BlockSpec((tr, cdim), lambda i: (i, 0))
    shp = jax.ShapeDtypeStruct((r, cdim), F32)
    return _pcall(
        body, name=name, grid=(r // tr,), in_specs=[pl.BlockSpec((N_DEV, tr, cdim), lambda i: (0, i, 0)), blk, blk, blk],
        out_specs=[blk, blk, blk, blk], out_shape=[shp, shp, shp, shp], compiler_params=_cp("parallel"),
    )(parts, w, m, v)


def _tile8(n, pref):
    for c in range(min(pref, n) - min(pref, n) % 16, 0, -16):
        if n % c == 0:
            return c
    return n


BIG = [("ffn1_wg", 2), ("ffn1_wu", 2), ("ffn1_wd", 1), ("ffn2_wg", 2), ("ffn2_wu", 2), ("ffn2_wd", 1), ("ple_wg", 1),
       ("ple_wp", 2), ("ab_w_in", 2), ("ab_w_out", 1), ("c_w_in", 2), ("c_w_out", 1)]
SMALL = [("ln_g", 2), ("ln_b", 2), ("b_conv_w", 2), ("c_conv_w", 2)]
REPL = ["ple_bg", "a_sinks", "b_conv_b", "b_wa", "b_ba", "b_wx", "b_bx", "b_lam", "c_a_log", "c_dt_bias", "c_norm_g"]
WEIGHTS = ["ffn1_wg", "ffn1_wu", "ffn1_wd", "ffn2_wg", "ffn2_wu", "ffn2_wd", "ln_g", "ln_b", "ple_wg", "ple_bg", "ple_wp",
           "ab_w_in", "a_sinks", "b_conv_w", "b_conv_b", "b_wa", "b_ba", "b_wx", "b_bx", "b_lam", "ab_w_out", "c_w_in",
           "c_conv_w", "c_a_log", "c_dt_bias", "c_norm_g", "c_w_out"]
PACK_COLS = 1024
PACK_ALIGN = 16 * PACK_COLS


def _as_bf16_bits(a):
    return lax.bitcast_convert_type(a, jnp.bfloat16).reshape(a.shape[:-1] + (2 * a.shape[-1],))


def _from_bf16_bits(a):
    return lax.bitcast_convert_type(a.reshape(a.shape[:-1] + (a.shape[-1] // 2, 2)), F32)


def _pad_rows(flat, align=PACK_ALIGN):
    n = flat.shape[-1]
    total = -(-n // align) * align
    flat = jnp.pad(flat, [(0, 0)] * (flat.ndim - 1) + [(0, total - n)])
    return flat.reshape(flat.shape[:-1] + (total // PACK_COLS, PACK_COLS))


def _join(blocks, axis):
    moved = jnp.moveaxis(blocks, 0, axis)
    shp = list(moved.shape)
    return moved.reshape(shp[:axis] + [shp[axis] * shp[axis + 1]] + shp[axis + 2:])


def _split(full, axis):
    shp = list(full.shape)
    return jnp.moveaxis(full.reshape(shp[:axis] + [N_DEV, shp[axis] // N_DEV] + shp[axis + 1:]), axis, 0)


def _dense_blocks(w):
    z = jnp.zeros((4, 2, 64, 2, 64), w.dtype)
    w4 = w.reshape(4, 2, 64, 64)
    z = z.at[:, 0, :, 0, :].set(w4[:, 0]).at[:, 1, :, 1, :].set(w4[:, 1])
    return z.reshape(4, 128, 128)


def _diag_blocks(d):
    d5 = d.reshape(4, 2, 64, 2, 64)
    return jnp.stack([d5[:, 0, :, 0, :], d5[:, 1, :, 1, :]], axis=1).reshape(8, 64, 64)


def kernel(x, p, ffn1_wg, ffn1_wu, ffn1_wd, ffn2_wg, ffn2_wu, ffn2_wd, ln_g, ln_b, ple_wg, ple_bg, ple_wp, ab_w_in, a_sinks, b_conv_w, b_conv_b, b_wa, b_ba, b_wx, b_bx, b_lam, ab_w_out, c_w_in, c_conv_w, c_a_log, c_dt_bias, c_norm_g, c_w_out, loss_target, m_ffn1_wg, m_ffn1_wu, m_ffn1_wd, m_ffn2_wg, m_ffn2_wu, m_ffn2_wd, m_ln_g, m_ln_b, m_ple_wg, m_ple_bg, m_ple_wp, m_ab_w_in, m_a_sinks, m_b_conv_w, m_b_conv_b, m_b_wa, m_b_ba, m_b_wx, m_b_bx, m_b_lam, m_ab_w_out, m_c_w_in, m_c_conv_w, m_c_a_log, m_c_dt_bias, m_c_norm_g, m_c_w_out, v_ffn1_wg, v_ffn1_wu, v_ffn1_wd, v_ffn2_wg, v_ffn2_wu, v_ffn2_wd, v_ln_g, v_ln_b, v_ple_wg, v_ple_bg, v_ple_wp, v_ab_w_in, v_a_sinks, v_b_conv_w, v_b_conv_b, v_b_wa, v_b_ba, v_b_wx, v_b_bx, v_b_lam, v_ab_w_out, v_c_w_in, v_c_conv_w, v_c_a_log, v_c_dt_bias, v_c_norm_g, v_c_w_out):
    a = dict(locals())
    return _step(a)


BIG_ROWS = 5632
SMALL_F32 = 73728


def _flat_pad(arrs, dtype, total):
    flat = jnp.concatenate([z.astype(dtype).reshape(-1) for z in arrs])
    return jnp.pad(flat, (0, total - flat.shape[0]))


def _flat8_pad(arrs, dtype, total):
    flat = jnp.concatenate([z.astype(dtype).reshape(N_DEV, -1) for z in arrs], axis=1)
    return jnp.pad(flat, ((0, 0), (0, total - flat.shape[1])))


def _bits(z):
    return z if MM == F32 else _as_bf16_bits(z)


def _unbits(z):
    return z if MM == F32 else _from_bf16_bits(z)


def _take(flat, names, shapes):
    out, off = {}, 0
    for n in names:
        sz = int(np.prod(shapes[n]))
        out[n] = flat[..., off:off + sz].reshape(flat.shape[:-1] + tuple(shapes[n]))
        off += sz
    return out


def _step(a):
    x, p = a["x"], a["p"]
    bsz, s_len, d = x.shape
    t = bsz * s_len
    x2 = x.reshape(t, d)
    tgt = a["loss_target"].reshape(t, d)
    p2 = p.reshape(DEPTH, t, D_PLE)
    shapes = {n: a[n].shape for n in WEIGHTS}
    big_names = [n for n, _ in BIG]
    small_names = [n for n, _ in SMALL]
    n_small = sum(int(np.prod(shapes[n])) for n in small_names)
    bits_per = 1 if MM == F32 else 2
    small_rows = -(-(n_small * bits_per) // PACK_ALIGN) * (PACK_ALIGN // PACK_COLS)

    send = jnp.concatenate([
        _flat_pad([a[n] for n in big_names], MM, BIG_ROWS * PACK_COLS).reshape(BIG_ROWS, PACK_COLS),
        _bits(_flat_pad([a[n] for n in small_names], F32, small_rows * PACK_COLS // bits_per)).reshape(small_rows, PACK_COLS),
    ], axis=0)
    gathered = all_gather(send, name="gather_weights")
    wb = _take(gathered[:, :BIG_ROWS].reshape(N_DEV, -1), big_names, shapes)
    ws = _take(_unbits(gathered[:, BIG_ROWS:].reshape(N_DEV, -1)), small_names, shapes)
    w = {n: _join(wb[n], ax) for n, ax in BIG}
    w.update({n: _join(ws[n], ax) for n, ax in SMALL})
    ln_g, ln_b = w["ln_g"], w["ln_b"]
    c_in_main = w["c_w_in"][0][:, :4 * C_WIDTH]
    c_in_gate = jnp.pad(w["c_w_in"][0][:, 4 * C_WIDTH:], ((0, 0), (0, 128 - 2 * C_HEADS)))
    wa_d, wx_d = _dense_blocks(a["b_wa"][0]), _dense_blocks(a["b_wx"][0])
    lru_w = (w["b_conv_w"][0], a["b_conv_b"][0], wa_d, a["b_ba"][0], wx_d, a["b_bx"][0], a["b_lam"][0])
    gdc_w = (a["c_a_log"][0], a["c_dt_bias"][0], a["c_norm_g"][0])

    h = x2
    saved = []
    for i in range(DEPTH):
        s = {"x0": h}
        s["y1"], s["z1"] = ffn_fwd(h, w["ffn1_wg"][i], w["ffn1_wu"][i], w["ffn1_wd"][i], ln_g[i, 0], ln_b[i, 0],
                                   name=f"ffn1_fwd_{i}")
        if i == 0:
            s["proj"] = matmul(s["y1"], w["ab_w_in"][0], mode="nn", name="ab_in_fwd")
            ya = attn_fwd(s["proj"], a["a_sinks"][0], bsz, name="attn_fwd")
            yb = lru_fwd(s["proj"], *lru_w, bsz, name="lru_fwd")
            s["mix"] = jnp.concatenate([ya, yb], axis=1)
            w_out = w["ab_w_out"][0]
        else:
            s["proj"] = matmul(s["y1"], c_in_main, mode="nn", name="c_in_fwd")
            s["gates"] = matmul(s["y1"], c_in_gate, mode="nn", name="c_gate_fwd")
            s["qkv"] = gdc_pre_fwd(s["proj"], w["c_conv_w"][0], bsz, name="gdc_pre_fwd")
            s["mix"], s["o_pre"], s["vnew"], s["tmat"], s["states"] = gdc_fwd(
                s["qkv"], s["proj"], s["gates"], *gdc_w, bsz, name="gdc_fwd")
            w_out = w["c_w_out"][0]
        s["y2"], s["z2"] = mm_ln_fwd(s["mix"], w_out, s["y1"], ln_g[i, 1], ln_b[i, 1], name=f"mix_out_fwd_{i}")
        s["y3"], s["z3"] = ffn_fwd(s["y2"], w["ffn2_wg"][i], w["ffn2_wu"][i], w["ffn2_wd"][i], ln_g[i, 2], ln_b[i, 2],
                                   name=f"ffn2_fwd_{i}")
        h = ple_fwd(s["y3"], p2[i], w["ple_wg"][i], a["ple_bg"][i], w["ple_wp"][i], name=f"ple_fwd_{i}")
        saved.append(s)
    loss_part, dh = loss_fwd_bwd(h, tgt, name="loss")

    g = {n: [None] * shapes[n][0] for n in ("ffn1_wg", "ffn1_wu", "ffn1_wd", "ffn2_wg", "ffn2_wu", "ffn2_wd", "ln_g",
                                             "ln_b", "ple_wg", "ple_bg", "ple_wp")}
    wide = dict(tm=1024, tn=1408, tk=512)
    tall = dict(tm=1408, tn=1024, tk=512)
    for i in reversed(range(DEPTH)):
        s = saved[i]
        dy3, dt, de, dbg = ple_bwd(dh, s["y3"], p2[i], w["ple_wg"][i], a["ple_bg"][i], w["ple_wp"][i], name=f"ple_bwd_{i}")
        g["ple_wg"][i] = matmul(s["y3"], dt, mode="tn", name=f"ple_wg_grad_{i}")
        g["ple_wp"][i] = matmul(p2[i], de, mode="tn", name=f"ple_wp_grad_{i}")
        g["ple_bg"][i] = dbg[0]
        dz3, dg2, db2 = ln_bwd(dy3, s["z3"], ln_g[i, 2], name=f"ln2_bwd_{i}")
        dy2, act, dhg, dhu = ffn_bwd(s["y2"], dz3, w["ffn2_wg"][i], w["ffn2_wu"][i], w["ffn2_wd"][i], name=f"ffn2_bwd_{i}")
        g["ffn2_wg"][i] = matmul(s["y2"], dhg, mode="tn", name=f"ffn2_wg_grad_{i}", **wide)
        g["ffn2_wu"][i] = matmul(s["y2"], dhu, mode="tn", name=f"ffn2_wu_grad_{i}", **wide)
        g["ffn2_wd"][i] = matmul(act, dz3, mode="tn", scale=0.5, name=f"ffn2_wd_grad_{i}", **tall)
        dz2, dg1, db1 = ln_bwd(dy2, s["z2"], ln_g[i, 1], name=f"ln1_bwd_{i}")
        if i == 0:
            dmix = matmul(dz2, w["ab_w_out"][0], mode="nt", name="ab_out_bwd")
            g["ab_w_out"] = matmul(s["mix"], dz2, mode="tn", name="ab_out_grad")
            dq, dk, dv, dsk = attn_bwd(s["proj"], a["a_sinks"][0], dmix, bsz, name="attn_bwd")
            dbx, dbgate, dcw, dcb, dwa, dba, dwx, dbxb, dlam = lru_bwd(s["proj"], *lru_w, dmix, bsz, name="lru_bwd")
            dproj = jnp.concatenate([dq, dk, dv, dbx, dbgate], axis=1).astype(MM)
            dy1 = matmul(dproj, w["ab_w_in"][0], mode="nt", add=dz2, add_scale=DN_ALPHA, name="ab_in_bwd")
            g["ab_w_in"] = matmul(s["y1"], dproj, mode="tn", name="ab_in_grad")
        else:
            dmix = matmul(dz2, w["c_w_out"][0], mode="nt", name="c_out_bwd")
            g["c_w_out"] = matmul(s["mix"], dz2, mode="tn", name="c_out_grad")
            dqkv, dzc, dgates, dsm = gdc_bwd(s["qkv"], s["proj"], s["gates"], *gdc_w, s["o_pre"], s["vnew"], s["tmat"],
                                             s["states"], dmix, bsz, name="gdc_bwd")
            draw, dccw = gdc_pre_bwd(s["proj"], w["c_conv_w"][0], dqkv, bsz, name="gdc_pre_bwd")
            dproj = jnp.concatenate([draw, dzc], axis=1).astype(MM)
            dgb = dgates.astype(MM)
            dy1 = matmul(dproj, c_in_main, mode="nt", add=dz2, add_scale=DN_ALPHA, name="c_in_bwd")
            dy1 = matmul(dgb, c_in_gate, mode="nt", add=dy1, name="c_gate_bwd")
            g["c_w_in"] = jnp.concatenate([matmul(s["y1"], dproj, mode="tn", name="c_in_grad"),
                                           matmul(s["y1"], dgb, mode="tn", name="c_gate_grad")[:, :2 * C_HEADS]], axis=1)
        dz1, dg0, db0 = ln_bwd(dy1, s["z1"], ln_g[i, 0], name=f"ln0_bwd_{i}")
        dh, act, dhg, dhu = ffn_bwd(s["x0"], dz1, w["ffn1_wg"][i], w["ffn1_wu"][i], w["ffn1_wd"][i], name=f"ffn1_bwd_{i}")
        g["ffn1_wg"][i] = matmul(s["x0"], dhg, mode="tn", name=f"ffn1_wg_grad_{i}", **wide)
        g["ffn1_wu"][i] = matmul(s["x0"], dhu, mode="tn", name=f"ffn1_wu_grad_{i}", **wide)
        g["ffn1_wd"][i] = matmul(act, dz1, mode="tn", scale=0.5, name=f"ffn1_wd_grad_{i}", **tall)
        g["ln_g"][i] = jnp.concatenate([dg0, dg1, dg2], axis=0)
        g["ln_b"][i] = jnp.concatenate([db0, db1, db2], axis=0)
    grad_x = dh.reshape(bsz, s_len, d)
    full = {n: jnp.stack(v) if isinstance(v, list) else v[None] for n, v in g.items()}
    full["b_conv_w"] = dcw[None]
    full["c_conv_w"] = jnp.sum(dccw, axis=0)[None]
    dsm_sum = jnp.sum(dsm, axis=0)
    full.update(a_sinks=jnp.sum(dsk, axis=0)[:, :A_HEADS], b_conv_b=dcb, b_wa=_diag_blocks(dwa)[None], b_ba=dba,
                b_wx=_diag_blocks(dwx)[None], b_bx=dbxb, b_lam=dlam, c_a_log=dsm_sum[None, :C_HEADS, 0],
                c_dt_bias=dsm_sum[None, :C_HEADS, 1], c_norm_g=jnp.sum(dsm_sum[C_HEADS:], axis=0)[None])

    small_cols = SMALL_F32 * bits_per // PACK_COLS
    repl_flat = _flat_pad([full[n] for n in REPL], F32, SMALL_F32 - n_small)
    small8 = jnp.concatenate([_flat8_pad([_split(full[n], ax) for n, ax in SMALL], F32, n_small),
                              jnp.broadcast_to(repl_flat, (N_DEV,) + repl_flat.shape)], axis=1)
    parts = jnp.concatenate([
        _flat8_pad([_split(full[n], ax) for n, ax in BIG], MM, BIG_ROWS * PACK_COLS).reshape(N_DEV, BIG_ROWS, PACK_COLS),
        _bits(small8).reshape(N_DEV, small_cols, PACK_COLS)], axis=1)
    recv = all_to_all(parts, name="exchange_grads")

    def mine(prefix, names, dtype_total):
        return _flat_pad([a[prefix + n] for n in names], F32, dtype_total)

    outs = {}
    big_total = BIG_ROWS * PACK_COLS
    res_big = adamw_sum(recv, *[mine(pre, big_names, big_total).reshape(BIG_ROWS, PACK_COLS) for pre in ("", "m_", "v_")],
                        name="adamw_big")
    small_all = small_names + REPL
    cols_f32 = PACK_COLS // bits_per
    res_small = adamw_sum(_unbits(recv[:, BIG_ROWS:]).reshape(N_DEV, small_cols, cols_f32),
                          *[mine(pre, small_all, SMALL_F32).reshape(small_cols, cols_f32) for pre in ("", "m_", "v_")],
                          name="adamw_small")
    kinds = []
    for rb, rs in zip(res_big, res_small):
        k = _take(rb.reshape(-1), big_names, shapes)
        k.update(_take(rs.reshape(-1), small_all, shapes))
        kinds.append(k)
    loss = lax.psum(loss_part[0, 0], ("x", "y", "c"))
    return (loss, grad_x, *[kinds[0][n] for n in WEIGHTS], *[kinds[1][n] for n in WEIGHTS],
            *[kinds[2][n] for n in WEIGHTS], *[kinds[3][n] for n in WEIGHTS])
```

```python
import functools
import math

import numpy as np
import jax
import jax.numpy as jnp
from jax import lax
from jax.experimental import pallas as pl
from jax.experimental.pallas import tpu as pltpu

F32 = jnp.float32
MM = jnp.bfloat16
HI = lax.Precision.HIGHEST

D_MODEL = 1024
D_FF = 2816
D_PLE = 256
DEPTH = 2
CHUNK = 64
A_HEADS = 8
A_KV_HEADS = 2
A_GROUP = 4
A_HEAD_DIM = 64
A_WIDTH = 512
A_KV_WIDTH = 128
B_WIDTH = 512
B_BLOCK = 64
RG_C = 8.0
AB_PROJ = 1792
C_HEADS = 8
C_HEAD_DIM = 128
C_WIDTH = 1024
DN_ALPHA = (2.0 * DEPTH) ** 0.25
LN_EPS = 1e-5
NORM_EPS = 1e-6
NEG = -1e30
ADAM_LR = 0.001
ADAM_B1 = 0.9
ADAM_B2 = 0.999
ADAM_EPS = 1e-08
ADAM_WD = 0.01
ADAM_STEP = 10
N_DEV = 8
VMEM_LIMIT = 56 * 1024 * 1024

NN = ((1,), (0,))
NT = ((1,), (1,))
TN = ((0,), (0,))


def _pcall(body, **kw):
    return pl.pallas_call(body, **kw)


def _cp(*sem):
    return pltpu.CompilerParams(dimension_semantics=sem, vmem_limit_bytes=VMEM_LIMIT)


def _dot(a, b, dims=NN, precision=None):
    return lax.dot_general(a, b, (dims, ((), ())), preferred_element_type=F32, precision=precision)


def _mdot(a, b, dims=NN):
    return _dot(a.astype(MM), b.astype(MM), dims)


def _tile(n, pref):
    if n <= pref:
        return n
    for c in range(pref - pref % 128, 0, -128):
        if n % c == 0:
            return c
    return n


def _sigmoid(x):
    return 1.0 / (1.0 + jnp.exp(-x))


def _softplus(x):
    return jnp.maximum(x, 0.0) + jnp.log(1.0 + jnp.exp(-jnp.abs(x)))


def _ln_stats(z):
    mu = jnp.mean(z, axis=-1, keepdims=True)
    zc = z - mu
    var = jnp.mean(zc * zc, axis=-1, keepdims=True)
    return zc, lax.rsqrt(var + LN_EPS)


def matmul(a, b, *, mode, name, tm=512, tn=512, tk=512, out_dtype=F32, scale=None, add=None, add_scale=1.0):
    if mode == "nn":
        (m, kk), (_, n) = a.shape, b.shape
        dims = NN
    elif mode == "nt":
        (m, kk), (n, _) = a.shape, b.shape
        dims = NT
    else:
        (kk, m), (_, n) = a.shape, b.shape
        dims = TN
    tm, tn, tk = _tile(m, tm), _tile(n, tn), _tile(kk, tk)
    if mode == "nn":
        a_spec = pl.BlockSpec((tm, tk), lambda i, j, k: (i, k))
        b_spec = pl.BlockSpec((tk, tn), lambda i, j, k: (k, j))
    elif mode == "nt":
        a_spec = pl.BlockSpec((tm, tk), lambda i, j, k: (i, k))
        b_spec = pl.BlockSpec((tn, tk), lambda i, j, k: (j, k))
    else:
        a_spec = pl.BlockSpec((tk, tm), lambda i, j, k: (k, i))
        b_spec = pl.BlockSpec((tk, tn), lambda i, j, k: (k, j))
    nk = kk // tk
    o_spec = pl.BlockSpec((tm, tn), lambda i, j, k: (i, j))
    has_add = add is not None

    def body(*refs):
        if has_add:
            a_ref, b_ref, add_ref, o_ref, acc_ref = refs
        else:
            a_ref, b_ref, o_ref, acc_ref = refs
        k = pl.program_id(2)

        @pl.when(k == 0)
        def _():
            acc_ref[...] = jnp.zeros_like(acc_ref)

        acc_ref[...] += _mdot(a_ref[...], b_ref[...], dims)

        @pl.when(k == nk - 1)
        def _():
            r = acc_ref[...]
            if scale is not None:
                r = r * scale
            if has_add:
                r = r + add_scale * add_ref[...].astype(F32)
            o_ref[...] = r.astype(out_dtype)

    ins = [a, b] + ([add] if has_add else [])
    in_specs = [a_spec, b_spec] + ([o_spec] if has_add else [])
    return _pcall(
        body, name=name, grid=(m // tm, n // tn, nk), in_specs=in_specs, out_specs=o_spec,
        out_shape=jax.ShapeDtypeStruct((m, n), out_dtype), scratch_shapes=[pltpu.VMEM((tm, tn), F32)],
        compiler_params=_cp("parallel", "parallel", "arbitrary"),
    )(*ins)


def ffn_fwd(x, wg, wu, wd, g, b, *, name, tm=512, tf=256):
    t, d = x.shape
    f = wg.shape[1]
    tm = min(tm, t)
    nj = f // tf

    def body(x_ref, wg_ref, wu_ref, wd_ref, g_ref, b_ref, y_ref, z_ref, xb_ref, acc_ref):
        j = pl.program_id(1)

        @pl.when(j == 0)
        def _():
            xb_ref[...] = x_ref[...].astype(MM)
            acc_ref[...] = jnp.zeros_like(acc_ref)

        xb = xb_ref[...]
        hg = _dot(xb, wg_ref[...])
        hu = _dot(xb, wu_ref[...])
        act = (hg * _sigmoid(hg) * hu).astype(MM)
        acc_ref[...] += _dot(act, wd_ref[...])

        @pl.when(j == nj - 1)
        def _():
            z = DN_ALPHA * x_ref[...] + 0.5 * acc_ref[...]
            z_ref[...] = z
            zc, rstd = _ln_stats(z)
            y_ref[...] = zc * rstd * g_ref[...] + b_ref[...]

    row = pl.BlockSpec((tm, d), lambda i, j: (i, 0))
    vec = pl.BlockSpec((1, d), lambda i, j: (0, 0))
    return _pcall(
        body, name=name, grid=(t // tm, nj),
        in_specs=[row, pl.BlockSpec((d, tf), lambda i, j: (0, j)), pl.BlockSpec((d, tf), lambda i, j: (0, j)),
                  pl.BlockSpec((tf, d), lambda i, j: (j, 0)), vec, vec],
        out_specs=[row, row],
        out_shape=[jax.ShapeDtypeStruct((t, d), F32), jax.ShapeDtypeStruct((t, d), F32)],
        scratch_shapes=[pltpu.VMEM((tm, d), MM), pltpu.VMEM((tm, d), F32)],
        compiler_params=_cp("parallel", "arbitrary"),
    )(x, wg, wu, wd, g.reshape(1, d), b.reshape(1, d))


def ffn_bwd(x, dz, wg, wu, wd, *, name, tm=512, tf=256):
    t, d = x.shape
    f = wg.shape[1]
    tm = min(tm, t)
    nj = f // tf

    def body(x_ref, dz_ref, wg_ref, wu_ref, wd_ref, dx_ref, act_ref, dhg_ref, dhu_ref, xb_ref, dfb_ref, acc_ref):
        j = pl.program_id(1)

        @pl.when(j == 0)
        def _():
            xb_ref[...] = x_ref[...].astype(MM)
            dfb_ref[...] = (0.5 * dz_ref[...]).astype(MM)
            acc_ref[...] = jnp.zeros_like(acc_ref)

        xb = xb_ref[...]
        hg = _dot(xb, wg_ref[...])
        hu = _dot(xb, wu_ref[...])
        s = _sigmoid(hg)
        dact = _dot(dfb_ref[...], wd_ref[...], NT)
        sg = hg * s
        act_ref[...] = (sg * hu).astype(MM)
        dhu = (dact * sg).astype(MM)
        dhg = (dact * hu * (s + sg * (1.0 - s))).astype(MM)
        dhu_ref[...] = dhu
        dhg_ref[...] = dhg
        acc_ref[...] += _dot(dhg, wg_ref[...], NT) + _dot(dhu, wu_ref[...], NT)

        @pl.when(j == nj - 1)
        def _():
            dx_ref[...] = DN_ALPHA * dz_ref[...] + acc_ref[...]

    row = pl.BlockSpec((tm, d), lambda i, j: (i, 0))
    hid = pl.BlockSpec((tm, tf), lambda i, j: (i, j))
    return _pcall(
        body, name=name, grid=(t // tm, nj),
        in_specs=[row, row, pl.BlockSpec((d, tf), lambda i, j: (0, j)), pl.BlockSpec((d, tf), lambda i, j: (0, j)),
                  pl.BlockSpec((tf, d), lambda i, j: (j, 0))],
        out_specs=[row, hid, hid, hid],
        out_shape=[jax.ShapeDtypeStruct((t, d), F32)] + [jax.ShapeDtypeStruct((t, f), MM)] * 3,
        scratch_shapes=[pltpu.VMEM((tm, d), MM), pltpu.VMEM((tm, d), MM), pltpu.VMEM((tm, d), F32)],
        compiler_params=_cp("parallel", "arbitrary"),
    )(x, dz, wg, wu, wd)


def ln_bwd(dy, z, g, *, name, tm=512):
    t, d = z.shape
    tm = min(tm, t)

    def body(dy_ref, z_ref, g_ref, dz_ref, dg_ref, db_ref):
        i = pl.program_id(0)

        @pl.when(i == 0)
        def _():
            dg_ref[...] = jnp.zeros_like(dg_ref)
            db_ref[...] = jnp.zeros_like(db_ref)

        dy = dy_ref[...]
        zc, rstd = _ln_stats(z_ref[...])
        xh = zc * rstd
        dg_ref[...] += jnp.sum(dy * xh, axis=0, keepdims=True)
        db_ref[...] += jnp.sum(dy, axis=0, keepdims=True)
        dxh = dy * g_ref[...]
        m1 = jnp.mean(dxh, axis=-1, keepdims=True)
        m2 = jnp.mean(dxh * xh, axis=-1, keepdims=True)
        dz_ref[...] = rstd * (dxh - m1 - xh * m2)

    row = pl.BlockSpec((tm, d), lambda i: (i, 0))
    vec = pl.BlockSpec((1, d), lambda i: (0, 0))
    return _pcall(
        body, name=name, grid=(t // tm,), in_specs=[row, row, vec], out_specs=[row, vec, vec],
        out_shape=[jax.ShapeDtypeStruct((t, d), F32), jax.ShapeDtypeStruct((1, d), F32), jax.ShapeDtypeStruct((1, d), F32)],
        compiler_params=_cp("arbitrary"),
    )(dy, z, g.reshape(1, d))


def mm_ln_fwd(a, w, res, g, b, *, name, tm=512):
    t, kk = a.shape
    d = w.shape[1]
    tm = min(tm, t)

    def body(a_ref, w_ref, res_ref, g_ref, b_ref, y_ref, z_ref):
        z = DN_ALPHA * res_ref[...] + _mdot(a_ref[...], w_ref[...])
        z_ref[...] = z
        zc, rstd = _ln_stats(z)
        y_ref[...] = zc * rstd * g_ref[...] + b_ref[...]

    row = pl.BlockSpec((tm, d), lambda i: (i, 0))
    vec = pl.BlockSpec((1, d), lambda i: (0, 0))
    return _pcall(
        body, name=name, grid=(t // tm,),
        in_specs=[pl.BlockSpec((tm, kk), lambda i: (i, 0)), pl.BlockSpec((kk, d), lambda i: (0, 0)), row, vec, vec],
        out_specs=[row, row],
        out_shape=[jax.ShapeDtypeStruct((t, d), F32), jax.ShapeDtypeStruct((t, d), F32)],
        compiler_params=_cp("parallel"),
    )(a, w, res, g.reshape(1, d), b.reshape(1, d))


def ple_fwd(y, p, wg, bg, wp, *, name, tm=512):
    t, d = y.shape
    dp = p.shape[1]
    tm = min(tm, t)

    def body(y_ref, p_ref, wg_ref, bg_ref, wp_ref, o_ref):
        yv = y_ref[...]
        gate = _sigmoid(_mdot(yv, wg_ref[...]) + bg_ref[...])
        o_ref[...] = yv + gate * _mdot(p_ref[...], wp_ref[...])

    row = pl.BlockSpec((tm, d), lambda i: (i, 0))
    return _pcall(
        body, name=name, grid=(t // tm,),
        in_specs=[row, pl.BlockSpec((tm, dp), lambda i: (i, 0)), pl.BlockSpec((d, d), lambda i: (0, 0)),
                  pl.BlockSpec((1, d), lambda i: (0, 0)), pl.BlockSpec((dp, d), lambda i: (0, 0))],
        out_specs=row, out_shape=jax.ShapeDtypeStruct((t, d), F32), compiler_params=_cp("parallel"),
    )(y, p, wg, bg.reshape(1, d), wp)


def ple_bwd(do, y, p, wg, bg, wp, *, name, tm=512):
    t, d = y.shape
    dp = p.shape[1]
    tm = min(tm, t)

    def body(do_ref, y_ref, p_ref, wg_ref, bg_ref, wp_ref, dy_ref, dt_ref, de_ref, dbg_ref):
        i = pl.program_id(0)

        @pl.when(i == 0)
        def _():
            dbg_ref[...] = jnp.zeros_like(dbg_ref)

        dov = do_ref[...]
        gate = _sigmoid(_mdot(y_ref[...], wg_ref[...]) + bg_ref[...])
        emb = _mdot(p_ref[...], wp_ref[...])
        dt = dov * emb * gate * (1.0 - gate)
        dbg_ref[...] += jnp.sum(dt, axis=0, keepdims=True)
        dtb = dt.astype(MM)
        dt_ref[...] = dtb
        de_ref[...] = (dov * gate).astype(MM)
        dy_ref[...] = dov + _dot(dtb, wg_ref[...], NT)

    row = pl.BlockSpec((tm, d), lambda i: (i, 0))
    vec = pl.BlockSpec((1, d), lambda i: (0, 0))
    return _pcall(
        body, name=name, grid=(t // tm,),
        in_specs=[row, row, pl.BlockSpec((tm, dp), lambda i: (i, 0)), pl.BlockSpec((d, d), lambda i: (0, 0)),
                  vec, pl.BlockSpec((dp, d), lambda i: (0, 0))],
        out_specs=[row, row, row, vec],
        out_shape=[jax.ShapeDtypeStruct((t, d), F32), jax.ShapeDtypeStruct((t, d), MM),
                   jax.ShapeDtypeStruct((t, d), MM), jax.ShapeDtypeStruct((1, d), F32)],
        compiler_params=_cp("arbitrary"),
    )(do, y, p, wg, bg.reshape(1, d), wp)


def loss_fwd_bwd(y, tgt, *, name, tm=512):
    t, d = y.shape
    tm = min(tm, t)

    def body(y_ref, t_ref, l_ref, dy_ref):
        i = pl.program_id(0)

        @pl.when(i == 0)
        def _():
            l_ref[...] = jnp.zeros_like(l_ref)

        err = y_ref[...] - t_ref[...]
        dy_ref[...] = err * (1.0 / d)
        l_ref[...] += (0.5 / d) * jnp.sum(jnp.sum(err * err, axis=1, keepdims=True), axis=0, keepdims=True)

    row = pl.BlockSpec((tm, d), lambda i: (i, 0))
    return _pcall(
        body, name=name, grid=(t // tm,), in_specs=[row, row],
        out_specs=[pl.BlockSpec((1, 128), lambda i: (0, 0)), row],
        out_shape=[jax.ShapeDtypeStruct((1, 128), F32), jax.ShapeDtypeStruct((t, d), F32)],
        compiler_params=_cp("arbitrary"),
    )(y, tgt)


def _shift_dn(x, s, row):
    return x if s == 0 else jnp.where(row >= s, pltpu.roll(x, s, 0), 0.0)


def _shift_up(x, s, row):
    n = x.shape[0]
    return x if s == 0 else jnp.where(row < n - s, pltpu.roll(x, n - s, 0), 0.0)


def _conv_fwd(x, w, row):
    kk = w.shape[0]
    y = w[kk - 1:kk, :] * x
    for j in range(kk - 1):
        y = y + w[j:j + 1, :] * _shift_dn(x, kk - 1 - j, row)
    return y


def _conv_bwd(x, w, dy, row):
    kk = w.shape[0]
    dx = w[kk - 1:kk, :] * dy
    dws = []
    for j in range(kk - 1):
        dx = dx + w[j:j + 1, :] * _shift_up(dy, kk - 1 - j, row)
        dws.append(jnp.sum(dy * _shift_dn(x, kk - 1 - j, row), axis=0, keepdims=True))
    dws.append(jnp.sum(dy * x, axis=0, keepdims=True))
    return dx, jnp.concatenate(dws, axis=0)


def _gelu(x):
    c = math.sqrt(2.0 / math.pi)
    th = jnp.tanh(c * (x + 0.044715 * x * x * x))
    return 0.5 * x * (1.0 + th), th


def _gelu_grad(x, th):
    c = math.sqrt(2.0 / math.pi)
    return 0.5 * (1.0 + th) + 0.5 * x * (1.0 - th * th) * c * (1.0 + 3.0 * 0.044715 * x * x)


def _neg_expm1(y):
    ser = -(y * (1.0 + y * (0.5 + y * (1.0 / 6.0 + y * (1.0 / 24.0 + y * (1.0 / 120.0))))))
    return jnp.where(y > -0.05, ser, 1.0 - jnp.exp(y))


def _attn_head(qh, kk, vv, bias, valid, sink):
    s = _mdot(qh, kk, NT) * (A_HEAD_DIM ** -0.5) - bias
    s = jnp.where(valid, s, NEG)
    m = jnp.maximum(jnp.max(s, axis=-1, keepdims=True), sink)
    pr = jnp.exp(s - m)
    den = jnp.sum(pr, axis=-1, keepdims=True) + jnp.exp(sink - m)
    return pr / den, jnp.exp(sink - m) / den


def _attn_masks(n):
    ci = lax.broadcasted_iota(jnp.int32, (CHUNK, 3 * CHUNK), 0)
    ji = lax.broadcasted_iota(jnp.int32, (CHUNK, 3 * CHUNK), 1)
    dist = jnp.abs(ci + 2 * CHUNK - ji).astype(F32)
    valid = (n * CHUNK + ji - 2 * CHUNK) >= 0
    return dist, valid


def attn_fwd(proj, sinks, bsz, *, name):
    t = proj.shape[0]
    s_len = t // bsz
    nc = s_len // CHUNK
    pad = 2 * CHUNK

    def body(q_ref, k_ref, v_ref, sk_ref, o_ref, kp_ref, vp_ref):
        kp_ref[0:pad, :] = jnp.zeros((pad, A_KV_WIDTH), F32)
        vp_ref[0:pad, :] = jnp.zeros((pad, A_KV_WIDTH), F32)
        kp_ref[pad:, :] = k_ref[...]
        vp_ref[pad:, :] = v_ref[...]

        def chunk(n, carry):
            st = pl.multiple_of(n * CHUNK, CHUNK)
            q = q_ref[pl.ds(st, CHUNK), :]
            kb = kp_ref[pl.ds(st, 3 * CHUNK), :]
            vb = vp_ref[pl.ds(st, 3 * CHUNK), :]
            dist, valid = _attn_masks(n)
            outs = []
            for h in range(A_HEADS):
                kh = h // A_GROUP
                pn, _ = _attn_head(q[:, h * 64:(h + 1) * 64], kb[:, kh * 64:(kh + 1) * 64], None,
                                   (2.0 ** -(h + 1)) * dist, valid, sk_ref[h])
                outs.append(_mdot(pn, vb[:, kh * 64:(kh + 1) * 64]))
            o_ref[pl.ds(st, CHUNK), :] = jnp.concatenate(outs, axis=-1)
            return carry

        lax.fori_loop(0, nc, chunk, 0)

    return _pcall(
        body, name=name, grid=(bsz,),
        in_specs=[pl.BlockSpec((s_len, A_WIDTH), lambda b: (b, 0)), pl.BlockSpec((s_len, 128), lambda b: (b, 4)),
                  pl.BlockSpec((s_len, 128), lambda b: (b, 5)), pl.BlockSpec(memory_space=pltpu.SMEM)],
        out_specs=pl.BlockSpec((s_len, A_WIDTH), lambda b: (b, 0)),
        out_shape=jax.ShapeDtypeStruct((t, A_WIDTH), F32),
        scratch_shapes=[pltpu.VMEM((s_len + pad, A_KV_WIDTH), F32), pltpu.VMEM((s_len + pad, A_KV_WIDTH), F32)],
        compiler_params=_cp("parallel"),
    )(proj, proj, proj, sinks)


def attn_bwd(proj, sinks, dcat, bsz, *, name):
    t = proj.shape[0]
    s_len = t // bsz
    nc = s_len // CHUNK
    pad = 2 * CHUNK

    def body(q_ref, k_ref, v_ref, do_ref, sk_ref, dq_ref, dk_ref, dv_ref, dsk_ref, kp_ref, vp_ref, dkp_ref, dvp_ref):
        kp_ref[0:pad, :] = jnp.zeros((pad, A_KV_WIDTH), F32)
        vp_ref[0:pad, :] = jnp.zeros((pad, A_KV_WIDTH), F32)
        kp_ref[pad:, :] = k_ref[...]
        vp_ref[pad:, :] = v_ref[...]
        dkp_ref[...] = jnp.zeros_like(dkp_ref)
        dvp_ref[...] = jnp.zeros_like(dvp_ref)
        lane = lax.broadcasted_iota(jnp.int32, (1, 128), 1)

        def chunk(n, dsk):
            st = pl.multiple_of(n * CHUNK, CHUNK)
            q = q_ref[pl.ds(st, CHUNK), :]
            do = do_ref[pl.ds(st, CHUNK), :]
            kb = kp_ref[pl.ds(st, 3 * CHUNK), :]
            vb = vp_ref[pl.ds(st, 3 * CHUNK), :]
            dist, valid = _attn_masks(n)
            dqs, dks, dvs = [], [], []
            for kh in range(A_KV_HEADS):
                kk = kb[:, kh * 64:(kh + 1) * 64]
                vv = vb[:, kh * 64:(kh + 1) * 64]
                dk_acc = jnp.zeros((3 * CHUNK, 64), F32)
                dv_acc = jnp.zeros((3 * CHUNK, 64), F32)
                for gi in range(A_GROUP):
                    h = kh * A_GROUP + gi
                    qh = q[:, h * 64:(h + 1) * 64]
                    doh = do[:, h * 64:(h + 1) * 64]
                    pn, psink = _attn_head(qh, kk, None, (2.0 ** -(h + 1)) * dist, valid, sk_ref[h])
                    dp = _mdot(doh, vv, NT)
                    rowdot = jnp.sum(pn * dp, axis=-1, keepdims=True)
                    ds = pn * (dp - rowdot)
                    dsk = dsk + jnp.where(lane == h, -jnp.sum(psink * rowdot, axis=0, keepdims=True), 0.0)
                    dqs.append(_mdot(ds, kk) * (A_HEAD_DIM ** -0.5))
                    dk_acc = dk_acc + _mdot(ds, qh, TN) * (A_HEAD_DIM ** -0.5)
                    dv_acc = dv_acc + _mdot(pn, doh, TN)
                dks.append(dk_acc)
                dvs.append(dv_acc)
            dq_ref[pl.ds(st, CHUNK), :] = jnp.concatenate(dqs, axis=-1)
            dkp_ref[pl.ds(st, 3 * CHUNK), :] += jnp.concatenate(dks, axis=-1)
            dvp_ref[pl.ds(st, 3 * CHUNK), :] += jnp.concatenate(dvs, axis=-1)
            return dsk

        dsk = lax.fori_loop(0, nc, chunk, jnp.zeros((1, 128), F32))
        dsk_ref[0] = dsk
        dk_ref[...] = dkp_ref[pad:, :]
        dv_ref[...] = dvp_ref[pad:, :]

    kv = jax.ShapeDtypeStruct((t, A_KV_WIDTH), F32)
    return _pcall(
        body, name=name, grid=(bsz,),
        in_specs=[pl.BlockSpec((s_len, A_WIDTH), lambda b: (b, 0)), pl.BlockSpec((s_len, 128), lambda b: (b, 4)),
                  pl.BlockSpec((s_len, 128), lambda b: (b, 5)), pl.BlockSpec((s_len, A_WIDTH), lambda b: (b, 0)),
                  pl.BlockSpec(memory_space=pltpu.SMEM)],
        out_specs=[pl.BlockSpec((s_len, A_WIDTH), lambda b: (b, 0)), pl.BlockSpec((s_len, 128), lambda b: (b, 0)),
                   pl.BlockSpec((s_len, 128), lambda b: (b, 0)), pl.BlockSpec((1, 1, 128), lambda b: (b, 0, 0))],
        out_shape=[jax.ShapeDtypeStruct((t, A_WIDTH), F32), kv, kv, jax.ShapeDtypeStruct((bsz, 1, 128), F32)],
        scratch_shapes=[pltpu.VMEM((s_len + pad, A_KV_WIDTH), F32)] * 4,
        compiler_params=_cp("parallel"),
    )(proj, proj, proj, dcat, sinks)


def _lru_gates(x, cw, cb, wa, ba, wx, bx, lam, row):
    xc = _conv_fwd(x, cw, row) + cb
    r = _sigmoid(_mdot(xc, wa) + ba)
    i = _sigmoid(_mdot(xc, wx) + bx)
    sp = _softplus(-lam)
    log_a = -RG_C * r * sp
    a = jnp.exp(log_a)
    mult = jnp.sqrt(_neg_expm1(2.0 * log_a))
    return xc, r, i, sp, a, mult


def _lru_scan(a, u, row):
    n = a.shape[0]
    d = 1
    while d < n:
        a_sh = jnp.where(row >= d, pltpu.roll(a, d, 0), 1.0)
        u_sh = jnp.where(row >= d, pltpu.roll(u, d, 0), 0.0)
        u = a * u_sh + u
        a = a * a_sh
        d *= 2
    return u


def _lru_scan_rev(a, u, row):
    n = a.shape[0]
    d = 1
    while d < n:
        a_sh = jnp.where(row < n - d, pltpu.roll(a, n - d, 0), 1.0)
        u_sh = jnp.where(row < n - d, pltpu.roll(u, n - d, 0), 0.0)
        u = a * u_sh + u
        a = a * a_sh
        d *= 2
    return u


def _lru_specs(s_len, order):
    def at(f):
        return lambda *g: f(*order(*g))
    return [pl.BlockSpec((s_len, 128), at(lambda b, cb: (b, 6 + cb))), pl.BlockSpec((s_len, 128), at(lambda b, cb: (b, 10 + cb))),
            pl.BlockSpec((4, 128), at(lambda b, cb: (0, cb))), pl.BlockSpec((1, 128), at(lambda b, cb: (0, cb))),
            pl.BlockSpec((1, 128, 128), at(lambda b, cb: (cb, 0, 0))), pl.BlockSpec((1, 128), at(lambda b, cb: (0, cb))),
            pl.BlockSpec((1, 128, 128), at(lambda b, cb: (cb, 0, 0))), pl.BlockSpec((1, 128), at(lambda b, cb: (0, cb))),
            pl.BlockSpec((1, 128), at(lambda b, cb: (0, cb)))]


def lru_fwd(proj, cw, cb, wa, ba, wx, bxb, lam, bsz, *, name):
    t = proj.shape[0]
    s_len = t // bsz

    def body(x_ref, g_ref, cw_ref, cb_ref, wa_ref, ba_ref, wx_ref, bx_ref, lam_ref, y_ref):
        row = lax.broadcasted_iota(jnp.int32, (s_len, 128), 0)
        xc, r, i, sp, a, mult = _lru_gates(x_ref[...], cw_ref[...], cb_ref[...], wa_ref[0], ba_ref[...], wx_ref[0],
                                           bx_ref[...], lam_ref[...], row)
        h = _lru_scan(a, mult * (i * xc), row)
        y_ref[...] = h * _gelu(g_ref[...])[0]

    return _pcall(
        body, name=name, grid=(bsz, 4), in_specs=_lru_specs(s_len, lambda b, cb: (b, cb)),
        out_specs=pl.BlockSpec((s_len, 128), lambda b, cb: (b, cb)),
        out_shape=jax.ShapeDtypeStruct((t, B_WIDTH), F32), compiler_params=_cp("parallel", "parallel"),
    )(proj, proj, cw, cb.reshape(1, -1), wa, ba.reshape(1, -1), wx, bxb.reshape(1, -1), lam.reshape(1, -1))


def lru_bwd(proj, cw, cb, wa, ba, wx, bxb, lam, dcat, bsz, *, name):
    t = proj.shape[0]
    s_len = t // bsz

    def body(x_ref, g_ref, cw_ref, cb_ref, wa_ref, ba_ref, wx_ref, bx_ref, lam_ref, dy_ref,
             dx_ref, dg_ref, dcw_ref, dcb_ref, dwa_ref, dba_ref, dwx_ref, dbx_ref, dlam_ref):
        b = pl.program_id(1)
        row = lax.broadcasted_iota(jnp.int32, (s_len, 128), 0)
        x = x_ref[...]
        lam = lam_ref[...]
        xc, r, i, sp, a, mult = _lru_gates(x, cw_ref[...], cb_ref[...], wa_ref[0], ba_ref[...], wx_ref[0], bx_ref[...],
                                           lam, row)
        ixc = i * xc
        h = _lru_scan(a, mult * ixc, row)
        gv = g_ref[...]
        gl, th = _gelu(gv)
        dy = dy_ref[...]
        dg_ref[...] = dy * h * _gelu_grad(gv, th)
        gr = _lru_scan_rev(_shift_up(a, 1, row), dy * gl, row)
        da = gr * _shift_dn(h, 1, row)
        dmult = gr * ixc
        di = gr * mult * xc
        dxc = gr * mult * i
        dlog_a = da * a - dmult * (a * a) / mult
        dr = dlog_a * (-RG_C * sp)
        dlam = jnp.sum(dlog_a * r, axis=0, keepdims=True) * (RG_C * _sigmoid(-lam))
        dpa = dr * r * (1.0 - r)
        dpx = di * i * (1.0 - i)
        dxc = dxc + _mdot(dpa, wa_ref[0], NT) + _mdot(dpx, wx_ref[0], NT)
        dx, dcw = _conv_bwd(x, cw_ref[...], dxc, row)
        dx_ref[...] = dx

        @pl.when(b == 0)
        def _():
            for ref in (dcw_ref, dcb_ref, dwa_ref, dba_ref, dwx_ref, dbx_ref, dlam_ref):
                ref[...] = jnp.zeros_like(ref)

        dcw_ref[...] += dcw
        dcb_ref[...] += jnp.sum(dxc, axis=0, keepdims=True)
        dwa_ref[0] += _mdot(xc, dpa, TN)
        dwx_ref[0] += _mdot(xc, dpx, TN)
        dba_ref[...] += jnp.sum(dpa, axis=0, keepdims=True)
        dbx_ref[...] += jnp.sum(dpx, axis=0, keepdims=True)
        dlam_ref[...] += dlam

    order = lambda cb, b: (b, cb)
    act = pl.BlockSpec((s_len, 128), lambda cb, b: (b, cb))
    vec = pl.BlockSpec((1, 128), lambda cb, b: (0, cb))
    mat = pl.BlockSpec((1, 128, 128), lambda cb, b: (cb, 0, 0))
    vshape = jax.ShapeDtypeStruct((1, B_WIDTH), F32)
    mshape = jax.ShapeDtypeStruct((4, 128, 128), F32)
    return _pcall(
        body, name=name, grid=(4, bsz),
        in_specs=_lru_specs(s_len, order) + [pl.BlockSpec((s_len, 128), lambda cb, b: (b, 4 + cb))],
        out_specs=[act, act, pl.BlockSpec((4, 128), lambda cb, b: (0, cb)), vec, mat, vec, mat, vec, vec],
        out_shape=[jax.ShapeDtypeStruct((t, B_WIDTH), F32), jax.ShapeDtypeStruct((t, B_WIDTH), F32),
                   jax.ShapeDtypeStruct((4, B_WIDTH), F32), vshape, mshape, vshape, mshape, vshape, vshape],
        compiler_params=_cp("parallel", "arbitrary"),
    )(proj, proj, cw, cb.reshape(1, -1), wa, ba.reshape(1, -1), wx, bxb.reshape(1, -1), lam.reshape(1, -1), dcat)


_BDIMS = {"nn": ((2,), (1,)), "nt": ((2,), (2,)), "tn": ((1,), (1,))}
C_QSCALE = C_HEAD_DIM ** -0.5


def _bmm(a, b, mode, exact=False):
    dims = (_BDIMS[mode], ((0,), (0,)))
    if exact:
        return lax.dot_general(a, b, dims, preferred_element_type=F32, precision=lax.Precision.HIGH)
    return lax.dot_general(a.astype(MM), b.astype(MM), dims, preferred_element_type=F32)


def _col(x, idx, lane):
    return jnp.broadcast_to(jnp.sum(jnp.where(lane == idx, x, 0.0), axis=-1, keepdims=True), x.shape)


def _seg_cumsum(g, row):
    pos = row & (CHUNK - 1)
    d = 1
    while d < CHUNK:
        g = g + jnp.where(pos >= d, pltpu.roll(g, d, 0), 0.0)
        d *= 2
    return g


def _seg_cumsum_rev(g, row):
    pos = row & (CHUNK - 1)
    n = g.shape[0]
    d = 1
    while d < CHUNK:
        g = g + jnp.where(pos < CHUNK - d, pltpu.roll(g, n - d, 0), 0.0)
        d *= 2
    return g


def _gdn_prep(qr, kr, vr, gates, cwq, cwk, cwv, a_log, dtb, h):
    s_len = qr.shape[0]
    nc = s_len // CHUNK
    row = lax.broadcasted_iota(jnp.int32, (s_len, 128), 0)
    lane = lax.broadcasted_iota(jnp.int32, (s_len, 128), 1)
    r = {"row": row, "lane": lane}
    for nm, x, w in (("q", qr, cwq), ("k", kr, cwk), ("v", vr, cwv)):
        c = _conv_fwd(x, w, row)
        sg = _sigmoid(c)
        r["c" + nm], r["s" + nm], r[nm + "c"] = c, sg, c * sg
    r["rq"] = lax.rsqrt(jnp.sum(r["qc"] * r["qc"], axis=-1, keepdims=True) + NORM_EPS)
    r["rk"] = lax.rsqrt(jnp.sum(r["kc"] * r["kc"], axis=-1, keepdims=True) + NORM_EPS)
    r["qn"] = r["qc"] * r["rq"]
    r["kn"] = r["kc"] * r["rk"]
    r["beta"] = _sigmoid(_col(gates, h, lane))
    r["A"] = jnp.exp(a_log)
    r["pre"] = _col(gates, 8 + h, lane) + dtb
    r["sp"] = _softplus(r["pre"])
    gc = _seg_cumsum(-r["A"] * r["sp"], row)
    sh = (nc, CHUNK, 128)
    q3 = (r["qn"] * C_QSCALE).reshape(sh)
    k3 = r["kn"].reshape(sh)
    v3 = r["vc"].reshape(sh)
    beta3 = r["beta"].reshape(sh)
    gc3 = gc.reshape(sh)
    gcl3 = gc3[:, CHUNK - 1:CHUNK, :]
    eg = jnp.exp(gc3)
    ekd = jnp.exp(gcl3 - gc3)
    col64 = gc3[:, :, :CHUNK]
    row64 = jnp.swapaxes(gc3, 1, 2)[:, :CHUNK, :]
    ii = lax.broadcasted_iota(jnp.int32, (nc, CHUNK, CHUNK), 1)
    jj = lax.broadcasted_iota(jnp.int32, (nc, CHUNK, CHUNK), 2)
    tril = ii >= jj
    strict = ii > jj
    dm = jnp.where(tril, jnp.exp(jnp.where(tril, col64 - row64, 0.0)), 0.0)
    kb = k3 * beta3
    lmat = jnp.where(strict, _bmm(kb, k3, "nt") * dm, 0.0)
    attn = _bmm(q3, k3, "nt") * dm
    r.update(q3=q3, k3=k3, v3=v3, beta3=beta3, gc3=gc3, eg=eg, ekd=ekd, gl=jnp.exp(gcl3), dm=dm, kb=kb, lmat=lmat,
             attn=attn, strict=strict, tril=tril, qg=q3 * eg, kdec=k3 * ekd)
    return r


def _neumann_inverse(lmat):
    ii = lax.broadcasted_iota(jnp.int32, lmat.shape, 1)
    jj = lax.broadcasted_iota(jnp.int32, lmat.shape, 2)
    x = -lmat
    tm = jnp.where(ii == jj, 1.0, 0.0) + x
    pw = x
    for _ in range(5):
        pw = _bmm(pw, pw, "nn", exact=True)
        tm = tm + _bmm(tm, pw, "nn", exact=True)
    return tm


def _gdn_specs(s_len):
    act = lambda off: pl.BlockSpec((s_len, 128), lambda b, h: (b, off + h))
    cw = lambda off: pl.BlockSpec((4, 128), lambda b, h: (0, off + h))
    smem = pl.BlockSpec(memory_space=pltpu.SMEM)
    return [act(0), act(8), act(16), act(24), pl.BlockSpec((s_len, 128), lambda b, h: (b, 0)), cw(0), cw(8), cw(16),
            smem, smem, pl.BlockSpec((1, 128), lambda b, h: (0, 0))]


def gdn_fwd(proj, gates, cw, a_log, dtb, ng, bsz, *, name):
    t = proj.shape[0]
    s_len = t // bsz
    nc = s_len // CHUNK

    def body(q_ref, k_ref, v_ref, z_ref, gt_ref, cwq_ref, cwk_ref, cwv_ref, al_ref, dt_ref, ng_ref,
             out_ref, o_ref, vn_ref, tm_ref, st_ref, u_s, w_s, qg_s, kd_s, at_s, gl_s):
        h = pl.program_id(1)
        r = _gdn_prep(q_ref[...], k_ref[...], v_ref[...], gt_ref[...], cwq_ref[...], cwk_ref[...], cwv_ref[...],
                      al_ref[h], dt_ref[h], h)
        tm = _neumann_inverse(r["lmat"])
        tm_ref[0, 0] = tm
        u_s[...] = _bmm(tm, r["v3"] * r["beta3"], "nn", exact=True)
        w_s[...] = _bmm(tm, r["kb"] * r["eg"], "nn", exact=True)
        qg_s[...] = r["qg"]
        kd_s[...] = r["kdec"]
        at_s[...] = r["attn"]
        gl_s[...] = r["gl"]

        def chunk(n, state):
            st = pl.multiple_of(n * CHUNK, CHUNK)
            st_ref[0, 0, n] = state
            v_new = u_s[n] - _mdot(w_s[n], state)
            o_ref[pl.ds(st, CHUNK), :] = _mdot(qg_s[n], state) + _mdot(at_s[n], v_new)
            vn_ref[pl.ds(st, CHUNK), :] = v_new
            return state * gl_s[n] + _mdot(kd_s[n], v_new, TN)

        lax.fori_loop(0, nc, chunk, jnp.zeros((128, 128), F32))
        o = o_ref[...]
        rms = lax.rsqrt(jnp.mean(o * o, axis=-1, keepdims=True) + NORM_EPS)
        z = z_ref[...]
        out_ref[...] = o * rms * ng_ref[...] * (z * _sigmoid(z))

    blk = pl.BlockSpec((s_len, 128), lambda b, h: (b, h))
    full = jax.ShapeDtypeStruct((t, C_WIDTH), F32)
    return _pcall(
        body, name=name, grid=(bsz, C_HEADS), in_specs=_gdn_specs(s_len),
        out_specs=[blk, blk, blk, pl.BlockSpec((1, 1, nc, CHUNK, CHUNK), lambda b, h: (b, h, 0, 0, 0)),
                   pl.BlockSpec((1, 1, nc, 128, 128), lambda b, h: (b, h, 0, 0, 0))],
        out_shape=[full, full, full, jax.ShapeDtypeStruct((bsz, C_HEADS, nc, CHUNK, CHUNK), F32),
                   jax.ShapeDtypeStruct((bsz, C_HEADS, nc, 128, 128), F32)],
        scratch_shapes=[pltpu.VMEM((nc, CHUNK, 128), F32)] * 4 + [pltpu.VMEM((nc, CHUNK, CHUNK), F32),
                                                                   pltpu.VMEM((nc, 1, 128), F32)],
        compiler_params=_cp("parallel", "parallel"),
    )(proj, proj, proj, proj, gates, cw, cw, cw, a_log, dtb, ng.reshape(1, 128))


def gdn_bwd(proj, gates, cw, a_log, dtb, ng, o_pre, vnew, tmat, states, dout, bsz, *, name):
    t = proj.shape[0]
    s_len = t // bsz
    nc = s_len // CHUNK

    def body(q_ref, k_ref, v_ref, z_ref, gt_ref, cwq_ref, cwk_ref, cwv_ref, al_ref, dt_ref, ng_ref,
             o_ref, vn_ref, tm_ref, st_ref, do_ref,
             dq_ref, dk_ref, dv_ref, dz_ref, dgt_ref, dcq_ref, dck_ref, dcv_ref, dsm_ref,
             w_s, qg_s, kd_s, at_s, gl_s, dop_s, du_s, dw_s, dat_s, dqg_s, dkd_s, dgl_s):
        h = pl.program_id(1)
        qr, kr, vr = q_ref[...], k_ref[...], v_ref[...]
        r = _gdn_prep(qr, kr, vr, gt_ref[...], cwq_ref[...], cwk_ref[...], cwv_ref[...], al_ref[h], dt_ref[h], h)
        row, lane = r["row"], r["lane"]
        tm = tm_ref[0, 0]
        q3, k3, v3, beta3, eg, kb, dm = r["q3"], r["k3"], r["v3"], r["beta3"], r["eg"], r["kb"], r["dm"]
        u3 = _bmm(tm, v3 * beta3, "nn", exact=True)
        w3 = _bmm(tm, kb * eg, "nn", exact=True)
        w_s[...] = w3
        qg_s[...] = r["qg"]
        kd_s[...] = r["kdec"]
        at_s[...] = r["attn"]
        gl_s[...] = r["gl"]

        z = z_ref[...]
        sz = _sigmoid(z)
        o = o_ref[...]
        rms = lax.rsqrt(jnp.mean(o * o, axis=-1, keepdims=True) + NORM_EPS)
        on = o * rms
        dout_v = do_ref[...]
        ngv = ng_ref[...]
        dz_ref[...] = dout_v * on * ngv * (sz * (1.0 + z * (1.0 - sz)))
        dos = dout_v * (z * sz)
        dng = jnp.sum(dos * on, axis=0, keepdims=True)
        don = dos * ngv
        dop_s[...] = (rms * (don - on * jnp.mean(don * on, axis=-1, keepdims=True))).reshape(nc, CHUNK, 128)

        def chunk(i, dstate):
            n = nc - 1 - i
            st = pl.multiple_of(n * CHUNK, CHUNK)
            state = st_ref[0, 0, n]
            vn = vn_ref[pl.ds(st, CHUNK), :]
            do_n = dop_s[n]
            dvn = _mdot(at_s[n], do_n, TN) + _mdot(kd_s[n], dstate)
            du_s[n] = dvn
            dat_s[n] = _mdot(do_n, vn, NT)
            dqg_s[n] = _mdot(do_n, state, NT)
            dkd_s[n] = _mdot(vn, dstate, NT)
            dgl_s[n] = jnp.broadcast_to(jnp.sum(jnp.sum(state * dstate, axis=1, keepdims=True), axis=0, keepdims=True), (1, 128))
            dw_s[n] = -_mdot(dvn, state, NT)
            return dstate * gl_s[n] + _mdot(qg_s[n], do_n, TN) - _mdot(w_s[n], dvn, TN)

        lax.fori_loop(0, nc, chunk, jnp.zeros((128, 128), F32))

        du, dw, dqg, dkd = du_s[...], dw_s[...], dqg_s[...], dkd_s[...]
        dat = jnp.where(r["tril"], dat_s[...], 0.0)
        dvb = _bmm(tm, du, "tn", exact=True)
        dkbg = _bmm(tm, dw, "tn", exact=True)
        dl = -jnp.where(r["strict"], _bmm(dvb, u3, "nt") + _bmm(dkbg, w3, "nt"), 0.0)
        dml = dl * dm
        dn = dat * dm
        dkb = _bmm(dml, k3, "nn") + dkbg * eg
        dk3 = _bmm(dml, kb, "tn") + _bmm(dn, q3, "tn") + dkd * r["ekd"] + dkb * beta3
        dq3 = dqg * eg + _bmm(dn, k3, "nn")
        e = dl * r["lmat"] + dat * r["attn"]
        ones = jnp.ones((nc, CHUNK, 128), F32)
        colsum = lax.dot_general(e, ones, (_BDIMS["tn"], ((0,), (0,))), preferred_element_type=F32, precision=HI)
        dgc = jnp.sum(e, axis=-1, keepdims=True) - colsum
        dgc = dgc + eg * (jnp.sum(dqg * q3, axis=-1, keepdims=True) + jnp.sum(dkbg * kb, axis=-1, keepdims=True))
        skd = jnp.sum(dkd * r["kdec"], axis=-1, keepdims=True)
        dgcl = jnp.sum(skd, axis=1, keepdims=True) + dgl_s[...] * r["gl"]
        pos3 = lax.broadcasted_iota(jnp.int32, (nc, CHUNK, 128), 1)
        dgc = dgc - skd + jnp.where(pos3 == CHUNK - 1, dgcl, 0.0)
        dbeta = jnp.sum(dkb * k3, axis=-1, keepdims=True) + jnp.sum(dvb * v3, axis=-1, keepdims=True)
        dv3 = dvb * beta3

        dg = _seg_cumsum_rev(dgc.reshape(s_len, 128), row)
        beta = r["beta"]
        dbl = jnp.broadcast_to(dbeta, (nc, CHUNK, 128)).reshape(s_len, 128) * beta * (1.0 - beta)
        dai = dg * (-r["A"]) * _sigmoid(r["pre"])
        d_dtb = jnp.sum(dai, axis=0, keepdims=True)
        d_alog = jnp.sum(dg * (-r["sp"]), axis=0, keepdims=True) * r["A"]

        @pl.when(h == 0)
        def _():
            dgt_ref[...] = jnp.zeros_like(dgt_ref)
            dsm_ref[...] = jnp.zeros_like(dsm_ref)

        dgt_ref[...] += jnp.where(lane == h, dbl, 0.0) + jnp.where(lane == 8 + h, dai, 0.0)
        r16 = lax.broadcasted_iota(jnp.int32, (16, 128), 0)
        l16 = lax.broadcasted_iota(jnp.int32, (16, 128), 1)
        small = jnp.where((r16 == h) & (l16 == 0), d_alog, 0.0) + jnp.where((r16 == h) & (l16 == 1), d_dtb, 0.0)
        dsm_ref[0] += small + jnp.where(r16 == 8 + h, dng, 0.0)

        dqn = dq3.reshape(s_len, 128) * C_QSCALE
        dkn = dk3.reshape(s_len, 128)
        dqc = r["rq"] * (dqn - r["qn"] * jnp.sum(dqn * r["qn"], axis=-1, keepdims=True))
        dkc = r["rk"] * (dkn - r["kn"] * jnp.sum(dkn * r["kn"], axis=-1, keepdims=True))
        dvc = dv3.reshape(s_len, 128)
        for nm, x, w_ref, dxc, dx_ref, dc_ref in (("q", qr, cwq_ref, dqc, dq_ref, dcq_ref), ("k", kr, cwk_ref, dkc, dk_ref, dck_ref),
                                                 ("v", vr, cwv_ref, dvc, dv_ref, dcv_ref)):
            c, sg = r["c" + nm], r["s" + nm]
            dc = dxc * (sg * (1.0 + c * (1.0 - sg)))
            dx, dwc = _conv_bwd(x, w_ref[...], dc, row)
            dx_ref[...] = dx
            dc_ref[0] = dwc

    blk = pl.BlockSpec((s_len, 128), lambda b, h: (b, h))
    full = jax.ShapeDtypeStruct((t, C_WIDTH), F32)
    cwo = pl.BlockSpec((1, 4, 128), lambda b, h: (b, 0, h))
    cws = jax.ShapeDtypeStruct((bsz, 4, C_WIDTH), F32)
    c128 = pltpu.VMEM((nc, CHUNK, 128), F32)
    outs = _pcall(
        body, name=name, grid=(bsz, C_HEADS),
        in_specs=_gdn_specs(s_len) + [blk, blk, pl.BlockSpec((1, 1, nc, CHUNK, CHUNK), lambda b, h: (b, h, 0, 0, 0)),
                                      pl.BlockSpec((1, 1, nc, 128, 128), lambda b, h: (b, h, 0, 0, 0)), blk],
        out_specs=[blk, blk, blk, blk, pl.BlockSpec((s_len, 128), lambda b, h: (b, 0)), cwo, cwo, cwo,
                   pl.BlockSpec((1, 16, 128), lambda b, h: (b, 0, 0))],
        out_shape=[full, full, full, full, jax.ShapeDtypeStruct((t, 128), F32), cws, cws, cws,
                   jax.ShapeDtypeStruct((bsz, 16, 128), F32)],
        scratch_shapes=[c128, c128, c128, pltpu.VMEM((nc, CHUNK, CHUNK), F32), pltpu.VMEM((nc, 1, 128), F32), c128,
                        c128, c128, pltpu.VMEM((nc, CHUNK, CHUNK), F32), c128, c128, pltpu.VMEM((nc, 1, 128), F32)],
        compiler_params=_cp("parallel", "arbitrary"),
    )(proj, proj, proj, proj, gates, cw, cw, cw, a_log, dtb, ng.reshape(1, 128), o_pre, vnew, tmat, states, dout)
    dq, dk, dv, dz, dgates, dcq, dck, dcv, dsm = outs
    return dq, dk, dv, dz, dgates, jnp.concatenate([dcq, dck, dcv], axis=-1), dsm


def gdc_pre_fwd(proj, cw, bsz, *, name):
    t = proj.shape[0]
    s_len = t // bsz

    def body(x_ref, w_ref, y_ref):
        row = lax.broadcasted_iota(jnp.int32, (s_len, 128), 0)
        c = _conv_fwd(x_ref[...], w_ref[...], row)
        xc = c * _sigmoid(c)
        rn = lax.rsqrt(jnp.sum(xc * xc, axis=-1, keepdims=True) + NORM_EPS)
        y_ref[...] = jnp.where(pl.program_id(1) < 2 * C_HEADS, xc * rn, xc)

    blk = pl.BlockSpec((s_len, 128), lambda b, j: (b, j))
    return _pcall(
        body, name=name, grid=(bsz, 3 * C_HEADS), in_specs=[blk, pl.BlockSpec((4, 128), lambda b, j: (0, j))],
        out_specs=blk, out_shape=jax.ShapeDtypeStruct((t, 3 * C_WIDTH), F32), compiler_params=_cp("parallel", "parallel"),
    )(proj, cw)


def gdc_pre_bwd(proj, cw, dy, bsz, *, name):
    t = proj.shape[0]
    s_len = t // bsz

    def body(x_ref, w_ref, dy_ref, dx_ref, dw_ref):
        row = lax.broadcasted_iota(jnp.int32, (s_len, 128), 0)
        x = x_ref[...]
        c = _conv_fwd(x, w_ref[...], row)
        sg = _sigmoid(c)
        xc = c * sg
        rn = lax.rsqrt(jnp.sum(xc * xc, axis=-1, keepdims=True) + NORM_EPS)
        dyv = dy_ref[...]
        xn = xc * rn
        dxc = jnp.where(pl.program_id(1) < 2 * C_HEADS, rn * (dyv - xn * jnp.sum(dyv * xn, axis=-1, keepdims=True)), dyv)
        dc = dxc * (sg * (1.0 + c * (1.0 - sg)))
        dx, dw = _conv_bwd(x, w_ref[...], dc, row)
        dx_ref[...] = dx
        dw_ref[0] = dw

    blk = pl.BlockSpec((s_len, 128), lambda b, j: (b, j))
    return _pcall(
        body, name=name, grid=(bsz, 3 * C_HEADS), in_specs=[blk, pl.BlockSpec((4, 128), lambda b, j: (0, j)), blk],
        out_specs=[blk, pl.BlockSpec((1, 4, 128), lambda b, j: (b, 0, j))],
        out_shape=[jax.ShapeDtypeStruct((t, 3 * C_WIDTH), F32), jax.ShapeDtypeStruct((bsz, 4, 3 * C_WIDTH), F32)],
        compiler_params=_cp("parallel", "parallel"),
    )(proj, cw, dy)


GDC_GROUP = 8


def _gdc_local(qn, kn, vc, gates, a_log, dtb, h):
    rows = qn.shape[0]
    nc = rows // CHUNK
    row = lax.broadcasted_iota(jnp.int32, (rows, 128), 0)
    lane = lax.broadcasted_iota(jnp.int32, (rows, 128), 1)
    r = {"row": row, "lane": lane}
    r["beta"] = _sigmoid(_col(gates, h, lane))
    r["A"] = jnp.exp(a_log)
    r["pre"] = _col(gates, 8 + h, lane) + dtb
    r["sp"] = _softplus(r["pre"])
    gc = _seg_cumsum(-r["A"] * r["sp"], row)
    sh = (nc, CHUNK, 128)
    q3 = (qn * C_QSCALE).reshape(sh)
    k3 = kn.reshape(sh)
    v3 = vc.reshape(sh)
    beta3 = r["beta"].reshape(sh)
    gc3 = gc.reshape(sh)
    gcl3 = gc3[:, CHUNK - 1:CHUNK, :]
    eg = jnp.exp(gc3)
    ekd = jnp.exp(gcl3 - gc3)
    col64 = gc3[:, :, :CHUNK]
    row64 = jnp.swapaxes(gc3, 1, 2)[:, :CHUNK, :]
    ii = lax.broadcasted_iota(jnp.int32, (nc, CHUNK, CHUNK), 1)
    jj = lax.broadcasted_iota(jnp.int32, (nc, CHUNK, CHUNK), 2)
    tril = ii >= jj
    strict = ii > jj
    dm = jnp.where(tril, jnp.exp(jnp.where(tril, col64 - row64, 0.0)), 0.0)
    kb = k3 * beta3
    lmat = jnp.where(strict, _bmm(kb, k3, "nt") * dm, 0.0)
    attn = _bmm(q3, k3, "nt") * dm
    r.update(q3=q3, k3=k3, v3=v3, beta3=beta3, eg=eg, ekd=ekd, gl=jnp.exp(gcl3), dm=dm, kb=kb, lmat=lmat,
             attn=attn, strict=strict, tril=tril, qg=q3 * eg, kdec=k3 * ekd)
    return r


def _gdc_specs(s_len):
    act = lambda off: pl.BlockSpec((s_len, 128), lambda b, h: (b, off + h))
    smem = pl.BlockSpec(memory_space=pltpu.SMEM)
    return [act(0), act(8), act(16), act(24), pl.BlockSpec((s_len, 128), lambda b, h: (b, 0)), smem, smem,
            pl.BlockSpec((1, 128), lambda b, h: (0, 0))]


def gdc_fwd(qkv, proj, gates, a_log, dtb, ng, bsz, *, name):
    t = proj.shape[0]
    s_len = t // bsz
    nc = s_len // CHUNK
    grp = min(GDC_GROUP, nc)
    gr = grp * CHUNK

    def body(q_ref, k_ref, v_ref, z_ref, gt_ref, al_ref, dt_ref, ng_ref,
             out_ref, o_ref, vn_ref, tm_ref, st_ref, u_s, w_s, qg_s, kd_s, at_s, gl_s):
        h = pl.program_id(1)

        def local(gi, carry):
            rs = pl.ds(pl.multiple_of(gi * gr, gr), gr)
            cs = pl.ds(gi * grp, grp)
            r = _gdc_local(q_ref[rs, :], k_ref[rs, :], v_ref[rs, :], gt_ref[rs, :], al_ref[h], dt_ref[h], h)
            tm = _neumann_inverse(r["lmat"])
            tm_ref[0, 0, cs] = tm
            u_s[cs] = _bmm(tm, r["v3"] * r["beta3"], "nn", exact=True)
            w_s[cs] = _bmm(tm, r["kb"] * r["eg"], "nn", exact=True)
            qg_s[cs] = r["qg"]
            kd_s[cs] = r["kdec"]
            at_s[cs] = r["attn"]
            gl_s[cs] = r["gl"]
            return carry

        lax.fori_loop(0, nc // grp, local, 0)

        def chunk(n, state):
            st = pl.multiple_of(n * CHUNK, CHUNK)
            st_ref[0, 0, n] = state
            v_new = u_s[n] - _mdot(w_s[n], state)
            o_ref[pl.ds(st, CHUNK), :] = _mdot(qg_s[n], state) + _mdot(at_s[n], v_new)
            vn_ref[pl.ds(st, CHUNK), :] = v_new
            return state * gl_s[n] + _mdot(kd_s[n], v_new, TN)

        lax.fori_loop(0, nc, chunk, jnp.zeros((128, 128), F32))
        o = o_ref[...]
        rms = lax.rsqrt(jnp.mean(o * o, axis=-1, keepdims=True) + NORM_EPS)
        z = z_ref[...]
        out_ref[...] = o * rms * ng_ref[...] * (z * _sigmoid(z))

    blk = pl.BlockSpec((s_len, 128), lambda b, h: (b, h))
    full = jax.ShapeDtypeStruct((t, C_WIDTH), F32)
    return _pcall(
        body, name=name, grid=(bsz, C_HEADS), in_specs=_gdc_specs(s_len),
        out_specs=[blk, blk, blk, pl.BlockSpec((1, 1, nc, CHUNK, CHUNK), lambda b, h: (b, h, 0, 0, 0)),
                   pl.BlockSpec((1, 1, nc, 128, 128), lambda b, h: (b, h, 0, 0, 0))],
        out_shape=[full, full, full, jax.ShapeDtypeStruct((bsz, C_HEADS, nc, CHUNK, CHUNK), F32),
                   jax.ShapeDtypeStruct((bsz, C_HEADS, nc, 128, 128), F32)],
        scratch_shapes=[pltpu.VMEM((nc, CHUNK, 128), F32)] * 4 + [pltpu.VMEM((nc, CHUNK, CHUNK), F32),
                                                                   pltpu.VMEM((nc, 1, 128), F32)],
        compiler_params=_cp("parallel", "parallel"),
    )(qkv, qkv, qkv, proj, gates, a_log, dtb, ng.reshape(1, 128))


def gdc_bwd(qkv, proj, gates, a_log, dtb, ng, o_pre, vnew, tmat, states, dout, bsz, *, name):
    t = proj.shape[0]
    s_len = t // bsz
    nc = s_len // CHUNK
    grp = min(GDC_GROUP, nc)
    gr = grp * CHUNK

    def body(q_ref, k_ref, v_ref, z_ref, gt_ref, al_ref, dt_ref, ng_ref, o_ref, vn_ref, tm_ref, st_ref, do_ref,
             dq_ref, dk_ref, dv_ref, dz_ref, dgt_ref, dsm_ref,
             w_s, qg_s, kd_s, at_s, gl_s, dop_s, du_s, dw_s, dat_s, dqg_s, dkd_s, dgl_s):
        h = pl.program_id(1)
        a_log_h, dtb_h = al_ref[h], dt_ref[h]

        z = z_ref[...]
        sz = _sigmoid(z)
        o = o_ref[...]
        rms = lax.rsqrt(jnp.mean(o * o, axis=-1, keepdims=True) + NORM_EPS)
        on = o * rms
        dout_v = do_ref[...]
        ngv = ng_ref[...]
        dz_ref[...] = dout_v * on * ngv * (sz * (1.0 + z * (1.0 - sz)))
        dos = dout_v * (z * sz)
        dng = jnp.sum(dos * on, axis=0, keepdims=True)
        don = dos * ngv
        dop_s[...] = (rms * (don - on * jnp.mean(don * on, axis=-1, keepdims=True))).reshape(nc, CHUNK, 128)

        def local(gi, carry):
            rs = pl.ds(pl.multiple_of(gi * gr, gr), gr)
            cs = pl.ds(gi * grp, grp)
            r = _gdc_local(q_ref[rs, :], k_ref[rs, :], v_ref[rs, :], gt_ref[rs, :], a_log_h, dtb_h, h)
            w_s[cs] = _bmm(tm_ref[0, 0, cs], r["kb"] * r["eg"], "nn", exact=True)
            qg_s[cs] = r["qg"]
            kd_s[cs] = r["kdec"]
            at_s[cs] = r["attn"]
            gl_s[cs] = r["gl"]
            return carry

        lax.fori_loop(0, nc // grp, local, 0)

        def chunk(i, dstate):
            n = nc - 1 - i
            st = pl.multiple_of(n * CHUNK, CHUNK)
            state = st_ref[0, 0, n]
            vn = vn_ref[pl.ds(st, CHUNK), :]
            do_n = dop_s[n]
            dvn = _mdot(at_s[n], do_n, TN) + _mdot(kd_s[n], dstate)
            du_s[n] = dvn
            dat_s[n] = _mdot(do_n, vn, NT)
            dqg_s[n] = _mdot(do_n, state, NT)
            dkd_s[n] = _mdot(vn, dstate, NT)
            dgl_s[n] = jnp.broadcast_to(jnp.sum(jnp.sum(state * dstate, axis=1, keepdims=True), axis=0, keepdims=True), (1, 128))
            dw_s[n] = -_mdot(dvn, state, NT)
            return dstate * gl_s[n] + _mdot(qg_s[n], do_n, TN) - _mdot(w_s[n], dvn, TN)

        lax.fori_loop(0, nc, chunk, jnp.zeros((128, 128), F32))

        @pl.when(h == 0)
        def _():
            dgt_ref[...] = jnp.zeros_like(dgt_ref)
            dsm_ref[...] = jnp.zeros_like(dsm_ref)

        def local_bwd(gi, carry):
            d_alog, d_dtb = carry
            rs = pl.ds(pl.multiple_of(gi * gr, gr), gr)
            cs = pl.ds(gi * grp, grp)
            r = _gdc_local(q_ref[rs, :], k_ref[rs, :], v_ref[rs, :], gt_ref[rs, :], a_log_h, dtb_h, h)
            row, lane = r["row"], r["lane"]
            q3, k3, v3, beta3, eg, kb, dm = r["q3"], r["k3"], r["v3"], r["beta3"], r["eg"], r["kb"], r["dm"]
            tm = tm_ref[0, 0, cs]
            u3 = _bmm(tm, v3 * beta3, "nn", exact=True)
            w3 = w_s[cs]
            du, dw, dqg, dkd = du_s[cs], dw_s[cs], dqg_s[cs], dkd_s[cs]
            dat = jnp.where(r["tril"], dat_s[cs], 0.0)
            dvb = _bmm(tm, du, "tn", exact=True)
            dkbg = _bmm(tm, dw, "tn", exact=True)
            dl = -jnp.where(r["strict"], _bmm(dvb, u3, "nt") + _bmm(dkbg, w3, "nt"), 0.0)
            dml = dl * dm
            dn = dat * dm
            dkb = _bmm(dml, k3, "nn") + dkbg * eg
            dk3 = _bmm(dml, kb, "tn") + _bmm(dn, q3, "tn") + dkd * r["ekd"] + dkb * beta3
            dq3 = dqg * eg + _bmm(dn, k3, "nn")
            e = dl * r["lmat"] + dat * r["attn"]
            ones = jnp.ones((grp, CHUNK, 128), F32)
            colsum = lax.dot_general(e, ones, (_BDIMS["tn"], ((0,), (0,))), preferred_element_type=F32, precision=HI)
            dgc = jnp.sum(e, axis=-1, keepdims=True) - colsum
            dgc = dgc + eg * (jnp.sum(dqg * q3, axis=-1, keepdims=True) + jnp.sum(dkbg * kb, axis=-1, keepdims=True))
            skd = jnp.sum(dkd * r["kdec"], axis=-1, keepdims=True)
            dgcl = jnp.sum(skd, axis=1, keepdims=True) + dgl_s[cs] * r["gl"]
            pos3 = lax.broadcasted_iota(jnp.int32, (grp, CHUNK, 128), 1)
            dgc = dgc - skd + jnp.where(pos3 == CHUNK - 1, dgcl, 0.0)
            dbeta = jnp.sum(dkb * k3, axis=-1, keepdims=True) + jnp.sum(dvb * v3, axis=-1, keepdims=True)
            dg = _seg_cumsum_rev(dgc.reshape(gr, 128), row)
            beta = r["beta"]
            dbl = jnp.broadcast_to(dbeta, (grp, CHUNK, 128)).reshape(gr, 128) * beta * (1.0 - beta)
            dai = dg * (-r["A"]) * _sigmoid(r["pre"])
            dgt_ref[rs, :] += jnp.where(lane == h, dbl, 0.0) + jnp.where(lane == 8 + h, dai, 0.0)
            dq_ref[rs, :] = dq3.reshape(gr, 128) * C_QSCALE
            dk_ref[rs, :] = dk3.reshape(gr, 128)
            dv_ref[rs, :] = (dvb * beta3).reshape(gr, 128)
            return (d_alog + jnp.sum(dg * (-r["sp"]), axis=0, keepdims=True) * r["A"],
                    d_dtb + jnp.sum(dai, axis=0, keepdims=True))

        zero = jnp.zeros((1, 128), F32)
        d_alog, d_dtb = lax.fori_loop(0, nc // grp, local_bwd, (zero, zero))
        r16 = lax.broadcasted_iota(jnp.int32, (16, 128), 0)
        l16 = lax.broadcasted_iota(jnp.int32, (16, 128), 1)
        small = jnp.where((r16 == h) & (l16 == 0), d_alog, 0.0) + jnp.where((r16 == h) & (l16 == 1), d_dtb, 0.0)
        dsm_ref[0] += small + jnp.where(r16 == 8 + h, dng, 0.0)

    blk = pl.BlockSpec((s_len, 128), lambda b, h: (b, h))
    blk3 = lambda off: pl.BlockSpec((s_len, 128), lambda b, h: (b, off + h))
    full = jax.ShapeDtypeStruct((t, C_WIDTH), F32)
    c128 = pltpu.VMEM((nc, CHUNK, 128), F32)
    dq, dk, dv, dz, dgates, dsm = _pcall(
        body, name=name, grid=(bsz, C_HEADS),
        in_specs=_gdc_specs(s_len) + [blk, blk, pl.BlockSpec((1, 1, nc, CHUNK, CHUNK), lambda b, h: (b, h, 0, 0, 0)),
                                      pl.BlockSpec((1, 1, nc, 128, 128), lambda b, h: (b, h, 0, 0, 0)), blk],
        out_specs=[blk, blk, blk, blk, pl.BlockSpec((s_len, 128), lambda b, h: (b, 0)),
                   pl.BlockSpec((1, 16, 128), lambda b, h: (b, 0, 0))],
        out_shape=[full, full, full, full, jax.ShapeDtypeStruct((t, 128), F32), jax.ShapeDtypeStruct((bsz, 16, 128), F32)],
        scratch_shapes=[c128, c128, c128, pltpu.VMEM((nc, CHUNK, CHUNK), F32), pltpu.VMEM((nc, 1, 128), F32), c128,
                        c128, c128, pltpu.VMEM((nc, CHUNK, CHUNK), F32), c128, c128, pltpu.VMEM((nc, 1, 128), F32)],
        compiler_params=_cp("parallel", "arbitrary"),
    )(qkv, qkv, qkv, proj, gates, a_log, dtb, ng.reshape(1, 128), o_pre, vnew, tmat, states, dout)
    return jnp.concatenate([dq, dk, dv], axis=-1), dz, dgates, dsm


MESH_ID = pl.DeviceIdType.MESH
_FLIPS = [(0, 0, 1), (1, 0, 0), (0, 1, 0), (1, 1, 0), (1, 0, 1), (0, 1, 1), (1, 1, 1)]


def _me():
    return lax.axis_index("x"), lax.axis_index("y"), lax.axis_index("c")


def _flip(coord, d):
    return 1 - coord if d else coord


def all_gather(shard, *, name):
    def body(x_ref, o_ref, send_sems, recv_sems, local_sem):
        x, y, c = _me()
        mine = 4 * x + 2 * y + c
        own = pltpu.make_async_copy(x_ref, o_ref.at[mine], local_sem)
        own.start()
        copies = []
        for k, (dx, dy, dc) in enumerate(_FLIPS):
            cp = pltpu.make_async_remote_copy(
                src_ref=x_ref, dst_ref=o_ref.at[mine], send_sem=send_sems.at[k], recv_sem=recv_sems.at[k],
                device_id=(_flip(x, dx), _flip(y, dy), _flip(c, dc)), device_id_type=MESH_ID)
            cp.start()
            copies.append(cp)
        for cp in copies:
            cp.wait()
        own.wait()

    hbm = pl.BlockSpec(memory_space=pl.ANY)
    return _pcall(
        body, name=name, in_specs=[hbm], out_specs=hbm,
        out_shape=jax.ShapeDtypeStruct((N_DEV,) + shard.shape, shard.dtype),
        scratch_shapes=[pltpu.SemaphoreType.DMA((7,)), pltpu.SemaphoreType.DMA((7,)), pltpu.SemaphoreType.DMA(())],
    )(shard)


def all_to_all(parts, *, name):
    def body(x_ref, o_ref, send_sems, recv_sems, local_sem):
        x, y, c = _me()
        mine = 4 * x + 2 * y + c
        own = pltpu.make_async_copy(x_ref.at[mine], o_ref.at[mine], local_sem)
        own.start()
        copies = []
        for k, (dx, dy, dc) in enumerate(_FLIPS):
            px, py, pc = _flip(x, dx), _flip(y, dy), _flip(c, dc)
            cp = pltpu.make_async_remote_copy(
                src_ref=x_ref.at[4 * px + 2 * py + pc], dst_ref=o_ref.at[mine], send_sem=send_sems.at[k],
                recv_sem=recv_sems.at[k], device_id=(px, py, pc), device_id_type=MESH_ID)
            cp.start()
            copies.append(cp)
        for cp in copies:
            cp.wait()
        own.wait()

    hbm = pl.BlockSpec(memory_space=pl.ANY)
    return _pcall(
        body, name=name, in_specs=[hbm], out_specs=hbm, out_shape=jax.ShapeDtypeStruct(parts.shape, parts.dtype),
        scratch_shapes=[pltpu.SemaphoreType.DMA((7,)), pltpu.SemaphoreType.DMA((7,)), pltpu.SemaphoreType.DMA(())],
    )(parts)


def adamw_sum(parts, w, m, v, *, name, tr=256):
    r, cdim = w.shape
    tr = _tile8(r, tr)

    def body(p_ref, w_ref, m_ref, v_ref, g_ref, d_ref, mo_ref, vo_ref):
        g = p_ref[0].astype(F32)
        for j in range(1, N_DEV):
            g = g + p_ref[j].astype(F32)
        g_ref[...] = g
        mn = ADAM_B1 * m_ref[...] + (1.0 - ADAM_B1) * g
        vn = ADAM_B2 * v_ref[...] + (1.0 - ADAM_B2) * (g * g)
        mo_ref[...] = mn
        vo_ref[...] = vn
        m_hat = mn / (1.0 - ADAM_B1 ** ADAM_STEP)
        v_hat = vn / (1.0 - ADAM_B2 ** ADAM_STEP)
        d_ref[...] = -ADAM_LR * (m_hat / (jnp.sqrt(v_hat) + ADAM_EPS) + ADAM_WD * w_ref[...])

    blk = pl.BlockSpec((tr, cdim), lambda i: (i, 0))
    shp = jax.ShapeDtypeStruct((r, cdim), F32)
    return _pcall(
        body, name=name, grid=(r // tr,), in_specs=[pl.BlockSpec((N_DEV, tr, cdim), lambda i: (0, i, 0)), blk, blk, blk],
        out_specs=[blk, blk, blk, blk], out_shape=[shp, shp, shp, shp], compiler_params=_cp("parallel"),
    )(parts, w, m, v)


def _tile8(n, pref):
    for c in range(min(pref, n) - min(pref, n) % 16, 0, -16):
        if n % c == 0:
            return c
    return n


BIG = [("ffn1_wg", 2), ("ffn1_wu", 2), ("ffn1_wd", 1), ("ffn2_wg", 2), ("ffn2_wu", 2), ("ffn2_wd", 1), ("ple_wg", 1),
       ("ple_wp", 2), ("ab_w_in", 2), ("ab_w_out", 1), ("c_w_in", 2), ("c_w_out", 1)]
SMALL = [("ln_g", 2), ("ln_b", 2), ("b_conv_w", 2), ("c_conv_w", 2)]
REPL = ["ple_bg", "a_sinks", "b_conv_b", "b_wa", "b_ba", "b_wx", "b_bx", "b_lam", "c_a_log", "c_dt_bias", "c_norm_g"]
WEIGHTS = ["ffn1_wg", "ffn1_wu", "ffn1_wd", "ffn2_wg", "ffn2_wu", "ffn2_wd", "ln_g", "ln_b", "ple_wg", "ple_bg", "ple_wp",
           "ab_w_in", "a_sinks", "b_conv_w", "b_conv_b", "b_wa", "b_ba", "b_wx", "b_bx", "b_lam", "ab_w_out", "c_w_in",
           "c_conv_w", "c_a_log", "c_dt_bias", "c_norm_g", "c_w_out"]
PACK_COLS = 1024
PACK_ALIGN = 16 * PACK_COLS


def _as_bf16_bits(a):
    return lax.bitcast_convert_type(a, jnp.bfloat16).reshape(a.shape[:-1] + (2 * a.shape[-1],))


def _from_bf16_bits(a):
    return lax.bitcast_convert_type(a.reshape(a.shape[:-1] + (a.shape[-1] // 2, 2)), F32)


def _pad_rows(flat, align=PACK_ALIGN):
    n = flat.shape[-1]
    total = -(-n // align) * align
    flat = jnp.pad(flat, [(0, 0)] * (flat.ndim - 1) + [(0, total - n)])
    return flat.reshape(flat.shape[:-1] + (total // PACK_COLS, PACK_COLS))


def _join(blocks, axis):
    moved = jnp.moveaxis(blocks, 0, axis)
    shp = list(moved.shape)
    return moved.reshape(shp[:axis] + [shp[axis] * shp[axis + 1]] + shp[axis + 2:])


def _split(full, axis):
    shp = list(full.shape)
    return jnp.moveaxis(full.reshape(shp[:axis] + [N_DEV, shp[axis] // N_DEV] + shp[axis + 1:]), axis, 0)


def _dense_blocks(w):
    z = jnp.zeros((4, 2, 64, 2, 64), w.dtype)
    w4 = w.reshape(4, 2, 64, 64)
    z = z.at[:, 0, :, 0, :].set(w4[:, 0]).at[:, 1, :, 1, :].set(w4[:, 1])
    return z.reshape(4, 128, 128)


def _diag_blocks(d):
    d5 = d.reshape(4, 2, 64, 2, 64)
    return jnp.stack([d5[:, 0, :, 0, :], d5[:, 1, :, 1, :]], axis=1).reshape(8, 64, 64)


def kernel(x, p, ffn1_wg, ffn1_wu, ffn1_wd, ffn2_wg, ffn2_wu, ffn2_wd, ln_g, ln_b, ple_wg, ple_bg, ple_wp, ab_w_in, a_sinks, b_conv_w, b_conv_b, b_wa, b_ba, b_wx, b_bx, b_lam, ab_w_out, c_w_in, c_conv_w, c_a_log, c_dt_bias, c_norm_g, c_w_out, loss_target, m_ffn1_wg, m_ffn1_wu, m_ffn1_wd, m_ffn2_wg, m_ffn2_wu, m_ffn2_wd, m_ln_g, m_ln_b, m_ple_wg, m_ple_bg, m_ple_wp, m_ab_w_in, m_a_sinks, m_b_conv_w, m_b_conv_b, m_b_wa, m_b_ba, m_b_wx, m_b_bx, m_b_lam, m_ab_w_out, m_c_w_in, m_c_conv_w, m_c_a_log, m_c_dt_bias, m_c_norm_g, m_c_w_out, v_ffn1_wg, v_ffn1_wu, v_ffn1_wd, v_ffn2_wg, v_ffn2_wu, v_ffn2_wd, v_ln_g, v_ln_b, v_ple_wg, v_ple_bg, v_ple_wp, v_ab_w_in, v_a_sinks, v_b_conv_w, v_b_conv_b, v_b_wa, v_b_ba, v_b_wx, v_b_bx, v_b_lam, v_ab_w_out, v_c_w_in, v_c_conv_w, v_c_a_log, v_c_dt_bias, v_c_norm_g, v_c_w_out):
    a = dict(locals())
    return _step2(a)


def join_cols(x, *, name, outs=None, tk=256):
    _, kk, n = x.shape
    tk = _tile8(kk, tk)
    outs = outs or [(0, N_DEV * n, N_DEV * n)]

    def body(x_ref, *o_refs):
        full = jnp.concatenate([x_ref[k] for k in range(N_DEV)], axis=-1)
        for (lo, hi, wd), o_ref in zip(outs, o_refs):
            piece = full[:, lo:hi]
            if wd > hi - lo:
                piece = jnp.concatenate([piece, jnp.zeros((tk, wd - (hi - lo)), piece.dtype)], axis=-1)
            o_ref[...] = piece

    res = _pcall(
        body, name=name, grid=(kk // tk,), in_specs=[pl.BlockSpec((N_DEV, tk, n), lambda i: (0, i, 0))],
        out_specs=[pl.BlockSpec((tk, wd), lambda i: (i, 0)) for _, _, wd in outs],
        out_shape=[jax.ShapeDtypeStruct((kk, wd), x.dtype) for _, _, wd in outs], compiler_params=_cp("parallel"),
    )(x)
    return res if len(outs) > 1 else res[0]


def split_cols(pieces, n, *, name, tk=256):
    kk = pieces[0][0].shape[0]
    tk = _tile8(kk, tk)

    def body(*refs):
        o_ref = refs[-1]
        vals = [r[...][:, :used] for r, (_, used) in zip(refs[:-1], pieces)]
        full = vals[0] if len(vals) == 1 else jnp.concatenate(vals, axis=-1)
        for k in range(N_DEV):
            o_ref[k] = full[:, k * n:(k + 1) * n].astype(MM)

    return _pcall(
        body, name=name, grid=(kk // tk,),
        in_specs=[pl.BlockSpec((tk, arr.shape[1]), lambda i: (i, 0)) for arr, _ in pieces],
        out_specs=pl.BlockSpec((N_DEV, tk, n), lambda i: (0, i, 0)),
        out_shape=jax.ShapeDtypeStruct((N_DEV, kk, n), MM), compiler_params=_cp("parallel"),
    )(*[arr for arr, _ in pieces])


def gather_multi(shards, *, name):
    ng = len(shards)

    def body(*refs):
        x_refs, o_refs = refs[:ng], refs[ng:2 * ng]
        send_sems, recv_sems, local_sems = refs[2 * ng:]
        x, y, c = _me()
        sibling = (x, y, 1 - c)
        chips = [(1 - x, y), (x, 1 - y), (1 - x, 1 - y)]

        def slot(px, py, pc):
            return 4 * px + 2 * py + pc

        def copy(gi, k, block, to, src=None):
            dst = o_refs[gi].at[slot(*block)]
            return pltpu.make_async_remote_copy(
                src_ref=dst if src is None else src, dst_ref=dst, send_sem=send_sems.at[7 * gi + k],
                recv_sem=recv_sems.at[7 * gi + k], device_id=to, device_id_type=MESH_ID)

        own = [pltpu.make_async_copy(x_refs[gi], o_refs[gi].at[slot(x, y, c)], local_sems.at[gi]) for gi in range(ng)]
        for cp in own:
            cp.start()
        first = []
        for gi in range(ng):
            first.append(copy(gi, 0, (x, y, c), sibling, src=x_refs[gi]))
            first += [copy(gi, 1 + j, (x, y, c), (*chip, c), src=x_refs[gi]) for j, chip in enumerate(chips)]
        for cp in first:
            cp.start()
        passed = []
        for j, chip in enumerate(chips):
            for gi in range(ng):
                copy(gi, 1 + j, (*chip, c), (x, y, c)).wait_recv()
                fwd = copy(gi, 4 + j, (*chip, c), sibling)
                fwd.start()
                passed.append(fwd)
        for gi in range(ng):
            copy(gi, 0, sibling, (x, y, c)).wait_recv()
            for j, chip in enumerate(chips):
                copy(gi, 4 + j, (*chip, 1 - c), (x, y, c)).wait_recv()
        for cp in first + passed:
            cp.wait_send()
        for cp in own:
            cp.wait()

    hbm = pl.BlockSpec(memory_space=pl.ANY)
    return _pcall(
        body, name=name, in_specs=[hbm] * ng, out_specs=[hbm] * ng,
        out_shape=[jax.ShapeDtypeStruct((N_DEV,) + s.shape, s.dtype) for s in shards],
        scratch_shapes=[pltpu.SemaphoreType.DMA((7 * ng,)), pltpu.SemaphoreType.DMA((7 * ng,)),
                        pltpu.SemaphoreType.DMA((ng,))],
    )(*shards)


def exchange_multi(parts, *, name):
    ng = len(parts)

    def body(*refs):
        x_refs, o_refs = refs[:ng], refs[ng:2 * ng]
        send_sems, recv_sems, local_sems = refs[2 * ng:]
        x, y, c = _me()
        mine = 4 * x + 2 * y + c
        own = [pltpu.make_async_copy(x_refs[gi].at[mine], o_refs[gi].at[mine], local_sems.at[gi]) for gi in range(ng)]
        for cp in own:
            cp.start()
        copies = []
        for k, (dx, dy, dc) in enumerate(_FLIPS):
            px, py, pc = _flip(x, dx), _flip(y, dy), _flip(c, dc)
            for gi in range(ng):
                cp = pltpu.make_async_remote_copy(
                    src_ref=x_refs[gi].at[4 * px + 2 * py + pc], dst_ref=o_refs[gi].at[mine],
                    send_sem=send_sems.at[7 * gi + k], recv_sem=recv_sems.at[7 * gi + k], device_id=(px, py, pc),
                    device_id_type=MESH_ID)
                cp.start()
                copies.append(cp)
        for cp in copies:
            cp.wait()
        for cp in own:
            cp.wait()

    hbm = pl.BlockSpec(memory_space=pl.ANY)
    return _pcall(
        body, name=name, in_specs=[hbm] * ng, out_specs=[hbm] * ng,
        out_shape=[jax.ShapeDtypeStruct(s.shape, s.dtype) for s in parts],
        scratch_shapes=[pltpu.SemaphoreType.DMA((7 * ng,)), pltpu.SemaphoreType.DMA((7 * ng,)),
                        pltpu.SemaphoreType.DMA((ng,))],
    )(*parts)


def adamw_rows(parts, row0, w, m, v, *, name, tr=256):
    r, cdim = w.shape
    tr = _tile8(math.gcd(r, row0) if row0 else r, tr)
    blk0 = row0 // tr

    def body(p_ref, w_ref, m_ref, v_ref, g_ref, d_ref, mo_ref, vo_ref):
        g = p_ref[0].astype(F32)
        for j in range(1, N_DEV):
            g = g + p_ref[j].astype(F32)
        g_ref[...] = g
        mn = ADAM_B1 * m_ref[...] + (1.0 - ADAM_B1) * g
        vn = ADAM_B2 * v_ref[...] + (1.0 - ADAM_B2) * (g * g)
        mo_ref[...] = mn
        vo_ref[...] = vn
        m_hat = mn / (1.0 - ADAM_B1 ** ADAM_STEP)
        v_hat = vn / (1.0 - ADAM_B2 ** ADAM_STEP)
        d_ref[...] = -ADAM_LR * (m_hat / (jnp.sqrt(v_hat) + ADAM_EPS) + ADAM_WD * w_ref[...])

    blk = pl.BlockSpec((tr, cdim), lambda i: (i, 0))
    shp = jax.ShapeDtypeStruct((r, cdim), F32)
    return _pcall(
        body, name=name, grid=(r // tr,),
        in_specs=[pl.BlockSpec((N_DEV, tr, cdim), lambda i: (0, blk0 + i, 0)), blk, blk, blk],
        out_specs=[blk, blk, blk, blk], out_shape=[shp, shp, shp, shp], compiler_params=_cp("parallel"),
    )(parts, w, m, v)


GROUP_A = ["ffn1_wg", "ffn1_wu", "ffn2_wg", "ffn2_wu"]
GROUP_B = ["ffn1_wd", "ffn2_wd", "ple_wg", "ab_w_out", "c_w_out"]
SMALL_NAMES = ["ln_g", "ln_b", "b_conv_w", "c_conv_w"]
LANES = 128
PLE_WP_ROWS = DEPTH * D_PLE


def _step2(a):
    x, p = a["x"], a["p"]
    bsz, s_len, d = x.shape
    t = bsz * s_len
    x2 = x.reshape(t, d)
    tgt = a["loss_target"].reshape(t, d)
    p2 = p.reshape(DEPTH, t, D_PLE)
    shapes = {n: a[n].shape for n in WEIGHTS}
    bits_per = 1 if MM == F32 else 2
    n_small = sum(int(np.prod(shapes[n])) for n in SMALL_NAMES)
    small_all = SMALL_NAMES + REPL
    f_ff = shapes["ffn1_wg"][2]
    rows_b = {n: shapes[n][0] * shapes[n][1] for n in GROUP_B}
    off_b = dict(zip(GROUP_B, np.cumsum([0] + [rows_b[n] for n in GROUP_B])[:-1].tolist()))

    small_send_f32 = 64 * LANES // bits_per
    send = [
        jnp.concatenate([a[n].astype(MM).reshape(-1, f_ff) for n in GROUP_A], axis=0),
        jnp.concatenate([a[n].astype(MM).reshape(-1, D_MODEL) for n in GROUP_B], axis=0),
        a["ab_w_in"][0].astype(MM),
        a["c_w_in"][0].astype(MM),
        jnp.concatenate([a["ple_wp"].astype(MM).reshape(PLE_WP_ROWS, LANES),
                         _bits(_flat_pad([a[n] for n in SMALL_NAMES], F32, small_send_f32)).reshape(64, LANES)], axis=0),
    ]
    ga, gb, gc, gd, ge = gather_multi(send, name="gather_weights")
    wa_full = join_cols(ga, name="join_ffn").reshape(len(GROUP_A), DEPTH, D_MODEL, N_DEV * f_ff)
    w = {n: wa_full[i] for i, n in enumerate(GROUP_A)}
    for n in GROUP_B:
        lyr, rws = shapes[n][0], shapes[n][1]
        blk = gb[:, off_b[n]:off_b[n] + rows_b[n]].reshape(N_DEV, lyr, rws, D_MODEL)
        w[n] = jnp.swapaxes(blk, 0, 1).reshape(lyr, N_DEV * rws, D_MODEL)
    w["ab_w_in"] = join_cols(gc, name="join_ab_in")
    c_in_main, c_in_gate = join_cols(gd, name="join_c_in", outs=[(0, 4 * C_WIDTH, 4 * C_WIDTH),
                                                                  (4 * C_WIDTH, 4 * C_WIDTH + 2 * C_HEADS, LANES)])
    w["ple_wp"] = _join(ge[:, :PLE_WP_ROWS].reshape(N_DEV, DEPTH, D_PLE, LANES), 2)
    ws = _take(_unbits(ge[:, PLE_WP_ROWS:].reshape(N_DEV, -1)), SMALL_NAMES, shapes)
    w.update({n: _join(ws[n], 2) for n in SMALL_NAMES})
    ln_g, ln_b = w["ln_g"], w["ln_b"]
    wa_d, wx_d = _dense_blocks(a["b_wa"][0]), _dense_blocks(a["b_wx"][0])
    lru_w = (w["b_conv_w"][0], a["b_conv_b"][0], wa_d, a["b_ba"][0], wx_d, a["b_bx"][0], a["b_lam"][0])
    gdc_w = (a["c_a_log"][0], a["c_dt_bias"][0], a["c_norm_g"][0])

    h = x2
    saved = []
    for i in range(DEPTH):
        s = {"x0": h}
        s["y1"], s["z1"] = ffn_fwd(h, w["ffn1_wg"][i], w["ffn1_wu"][i], w["ffn1_wd"][i], ln_g[i, 0], ln_b[i, 0],
                                   name=f"ffn1_fwd_{i}")
        if i == 0:
            s["proj"] = matmul(s["y1"], w["ab_w_in"], mode="nn", name="ab_in_fwd", tn=896, tk=1024)
            ya = attn_fwd(s["proj"], a["a_sinks"][0], bsz, name="attn_fwd")
            yb = lru_fwd(s["proj"], *lru_w, bsz, name="lru_fwd")
            s["mix"] = jnp.concatenate([ya, yb], axis=1)
            w_out = w["ab_w_out"][0]
        else:
            s["proj"] = matmul(s["y1"], c_in_main, mode="nn", name="c_in_fwd", tn=1024, tk=1024)
            s["gates"] = matmul(s["y1"], c_in_gate, mode="nn", name="c_gate_fwd", tk=1024)
            s["qkv"] = gdc_pre_fwd(s["proj"], w["c_conv_w"][0], bsz, name="gdc_pre_fwd")
            s["mix"], s["o_pre"], s["vnew"], s["tmat"], s["states"] = gdc_fwd(
                s["qkv"], s["proj"], s["gates"], *gdc_w, bsz, name="gdc_fwd")
            w_out = w["c_w_out"][0]
        s["y2"], s["z2"] = mm_ln_fwd(s["mix"], w_out, s["y1"], ln_g[i, 1], ln_b[i, 1], name=f"mix_out_fwd_{i}")
        s["y3"], s["z3"] = ffn_fwd(s["y2"], w["ffn2_wg"][i], w["ffn2_wu"][i], w["ffn2_wd"][i], ln_g[i, 2], ln_b[i, 2],
                                   name=f"ffn2_fwd_{i}")
        h = ple_fwd(s["y3"], p2[i], w["ple_wg"][i], a["ple_bg"][i], w["ple_wp"][i], name=f"ple_fwd_{i}")
        saved.append(s)
    loss_part, dh = loss_fwd_bwd(h, tgt, name="loss")

    g = {n: [None] * shapes[n][0] for n in ("ffn1_wg", "ffn1_wu", "ffn1_wd", "ffn2_wg", "ffn2_wu", "ffn2_wd", "ln_g",
                                             "ln_b", "ple_wg", "ple_bg", "ple_wp")}
    wide = dict(tm=1024, tn=1408, tk=512)
    tall = dict(tm=1408, tn=1024, tk=512)
    for i in reversed(range(DEPTH)):
        s = saved[i]
        dy3, dt, de, dbg = ple_bwd(dh, s["y3"], p2[i], w["ple_wg"][i], a["ple_bg"][i], w["ple_wp"][i], name=f"ple_bwd_{i}")
        g["ple_wg"][i] = matmul(s["y3"], dt, mode="tn", name=f"ple_wg_grad_{i}", tm=1024, tn=1024)
        g["ple_wp"][i] = matmul(p2[i], de, mode="tn", name=f"ple_wp_grad_{i}", tn=1024)
        g["ple_bg"][i] = dbg[0]
        dz3, dg2, db2 = ln_bwd(dy3, s["z3"], ln_g[i, 2], name=f"ln2_bwd_{i}")
        dy2, act, dhg, dhu = ffn_bwd(s["y2"], dz3, w["ffn2_wg"][i], w["ffn2_wu"][i], w["ffn2_wd"][i], name=f"ffn2_bwd_{i}")
        g["ffn2_wg"][i] = matmul(s["y2"], dhg, mode="tn", name=f"ffn2_wg_grad_{i}", **wide)
        g["ffn2_wu"][i] = matmul(s["y2"], dhu, mode="tn", name=f"ffn2_wu_grad_{i}", **wide)
        g["ffn2_wd"][i] = matmul(act, dz3, mode="tn", scale=0.5, name=f"ffn2_wd_grad_{i}", **tall)
        dz2, dg1, db1 = ln_bwd(dy2, s["z2"], ln_g[i, 1], name=f"ln1_bwd_{i}")
        if i == 0:
            dmix = matmul(dz2, w["ab_w_out"][0], mode="nt", name="ab_out_bwd", tn=1024, tk=1024)
            g["ab_w_out"] = matmul(s["mix"], dz2, mode="tn", name="ab_out_grad", tm=1024, tn=1024)
            dq, dk, dv, dsk = attn_bwd(s["proj"], a["a_sinks"][0], dmix, bsz, name="attn_bwd")
            dbx, dbgate, dcw, dcb, dwa, dba, dwx, dbxb, dlam = lru_bwd(s["proj"], *lru_w, dmix, bsz, name="lru_bwd")
            dproj = jnp.concatenate([dq, dk, dv, dbx, dbgate], axis=1).astype(MM)
            dy1 = matmul(dproj, w["ab_w_in"], mode="nt", add=dz2, add_scale=DN_ALPHA, name="ab_in_bwd", tn=1024, tk=1792)
            g_ab_in = matmul(s["y1"], dproj, mode="tn", name="ab_in_grad", tm=1024, tn=896)
        else:
            dmix = matmul(dz2, w["c_w_out"][0], mode="nt", name="c_out_bwd", tn=1024, tk=1024)
            g["c_w_out"] = matmul(s["mix"], dz2, mode="tn", name="c_out_grad", tm=1024, tn=1024)
            dqkv, dzc, dgates, dsm = gdc_bwd(s["qkv"], s["proj"], s["gates"], *gdc_w, s["o_pre"], s["vnew"], s["tmat"],
                                             s["states"], dmix, bsz, name="gdc_bwd")
            draw, dccw = gdc_pre_bwd(s["proj"], w["c_conv_w"][0], dqkv, bsz, name="gdc_pre_bwd")
            dproj = jnp.concatenate([draw, dzc], axis=1).astype(MM)
            dgb = dgates.astype(MM)
            dy1 = matmul(dproj, c_in_main, mode="nt", add=dz2, add_scale=DN_ALPHA, name="c_in_bwd", tn=1024, tk=1024)
            dy1 = matmul(dgb, c_in_gate, mode="nt", add=dy1, name="c_gate_bwd", tn=1024)
            g_c_main = matmul(s["y1"], dproj, mode="tn", name="c_in_grad", tm=1024, tn=1024)
            g_c_gate = matmul(s["y1"], dgb, mode="tn", name="c_gate_grad", tm=1024)
        dz1, dg0, db0 = ln_bwd(dy1, s["z1"], ln_g[i, 0], name=f"ln0_bwd_{i}")
        dh, act, dhg, dhu = ffn_bwd(s["x0"], dz1, w["ffn1_wg"][i], w["ffn1_wu"][i], w["ffn1_wd"][i], name=f"ffn1_bwd_{i}")
        g["ffn1_wg"][i] = matmul(s["x0"], dhg, mode="tn", name=f"ffn1_wg_grad_{i}", **wide)
        g["ffn1_wu"][i] = matmul(s["x0"], dhu, mode="tn", name=f"ffn1_wu_grad_{i}", **wide)
        g["ffn1_wd"][i] = matmul(act, dz1, mode="tn", scale=0.5, name=f"ffn1_wd_grad_{i}", **tall)
        g["ln_g"][i] = jnp.concatenate([dg0, dg1, dg2], axis=0)
        g["ln_b"][i] = jnp.concatenate([db0, db1, db2], axis=0)
    grad_x = dh.reshape(bsz, s_len, d)
    full = {n: jnp.stack(v) if isinstance(v, list) else v[None] for n, v in g.items()}
    full["b_conv_w"] = dcw[None]
    full["c_conv_w"] = jnp.sum(dccw, axis=0)[None]
    dsm_sum = jnp.sum(dsm, axis=0)
    full.update(a_sinks=jnp.sum(dsk, axis=0)[:, :A_HEADS], b_conv_b=dcb, b_wa=_diag_blocks(dwa)[None], b_ba=dba,
                b_wx=_diag_blocks(dwx)[None], b_bx=dbxb, b_lam=dlam, c_a_log=dsm_sum[None, :C_HEADS, 0],
                c_dt_bias=dsm_sum[None, :C_HEADS, 1], c_norm_g=jnp.sum(dsm_sum[C_HEADS:], axis=0)[None])

    small_f32_rows = SMALL_F32 // LANES
    repl_flat = _flat_pad([full[n] for n in REPL], F32, SMALL_F32 - n_small)
    small8 = jnp.concatenate([_flat8_pad([_split(full[n], 2) for n in SMALL_NAMES], F32, n_small),
                              jnp.broadcast_to(repl_flat, (N_DEV,) + repl_flat.shape)], axis=1)
    parts = [
        split_cols([(jnp.concatenate([full[n].reshape(-1, N_DEV * f_ff) for n in GROUP_A], axis=0), N_DEV * f_ff)], f_ff,
                   name="split_ffn"),
        jnp.concatenate([_split(full[n], 1).astype(MM).reshape(N_DEV, -1, D_MODEL) for n in GROUP_B], axis=1),
        split_cols([(g_ab_in, AB_PROJ)], AB_PROJ // N_DEV, name="split_ab_in"),
        split_cols([(g_c_main, 4 * C_WIDTH), (g_c_gate, 2 * C_HEADS)], (4 * C_WIDTH + 2 * C_HEADS) // N_DEV,
                   name="split_c_in"),
        jnp.concatenate([_split(full["ple_wp"], 2).astype(MM).reshape(N_DEV, PLE_WP_ROWS, LANES),
                         _bits(small8).reshape(N_DEV, small_f32_rows * bits_per, LANES)], axis=1),
    ]
    ra, rb, rc, rd, re = exchange_multi(parts, name="exchange_grads")

    def wmv(n, shape2d):
        return [a[pre + n].reshape(shape2d) for pre in ("", "m_", "v_")]

    res = {}
    for i, n in enumerate(GROUP_A):
        res[n] = adamw_rows(ra, i * DEPTH * D_MODEL, *wmv(n, (DEPTH * D_MODEL, f_ff)), name=f"adamw_{n}")
    for n in GROUP_B:
        res[n] = adamw_rows(rb, off_b[n], *wmv(n, (rows_b[n], D_MODEL)), name=f"adamw_{n}", tr=64)
    res["ab_w_in"] = adamw_rows(rc, 0, *wmv("ab_w_in", (D_MODEL, AB_PROJ // N_DEV)), name="adamw_ab_w_in")
    res["c_w_in"] = adamw_rows(rd, 0, *wmv("c_w_in", (D_MODEL, shapes["c_w_in"][2])), name="adamw_c_w_in")
    res["ple_wp"] = adamw_rows(re, 0, *wmv("ple_wp", (PLE_WP_ROWS, LANES)), name="adamw_ple_wp")
    small_parts = _unbits(re[:, PLE_WP_ROWS:]).reshape(N_DEV, small_f32_rows, LANES)
    res_small = adamw_rows(small_parts, 0, *[_flat_pad([a[pre + n] for n in small_all], F32, SMALL_F32).reshape(
        small_f32_rows, LANES) for pre in ("", "m_", "v_")], name="adamw_small", tr=576)
    kinds = []
    for k in range(4):
        kd = {n: res[n][k].reshape(shapes[n]) for n in res}
        kd.update(_take(res_small[k].reshape(-1), small_all, shapes))
        kinds.append(kd)
    loss = lax.psum(loss_part[0, 0], ("x", "y", "c"))
    return (loss, grad_x, *[kinds[0][n] for n in WEIGHTS], *[kinds[1][n] for n in WEIGHTS],
            *[kinds[2][n] for n in WEIGHTS], *[kinds[3][n] for n in WEIGHTS])


BIG_ROWS = 5632
SMALL_F32 = 73728


def _flat_pad(arrs, dtype, total):
    flat = jnp.concatenate([z.astype(dtype).reshape(-1) for z in arrs])
    return jnp.pad(flat, (0, total - flat.shape[0]))


def _flat8_pad(arrs, dtype, total):
    flat = jnp.concatenate([z.astype(dtype).reshape(N_DEV, -1) for z in arrs], axis=1)
    return jnp.pad(flat, ((0, 0), (0, total - flat.shape[1])))


def _bits(z):
    return z if MM == F32 else _as_bf16_bits(z)


def _unbits(z):
    return z if MM == F32 else _from_bf16_bits(z)


def _take(flat, names, shapes):
    out, off = {}, 0
    for n in names:
        sz = int(np.prod(shapes[n]))
        out[n] = flat[..., off:off + sz].reshape(flat.shape[:-1] + tuple(shapes[n]))
        off += sz
    return out


def _step(a):
    x, p = a["x"], a["p"]
    bsz, s_len, d = x.shape
    t = bsz * s_len
    x2 = x.reshape(t, d)
    tgt = a["loss_target"].reshape(t, d)
    p2 = p.reshape(DEPTH, t, D_PLE)
    shapes = {n: a[n].shape for n in WEIGHTS}
    big_names = [n for n, _ in BIG]
    small_names = [n for n, _ in SMALL]
    n_small = sum(int(np.prod(shapes[n])) for n in small_names)
    bits_per = 1 if MM == F32 else 2
    small_rows = -(-(n_small * bits_per) // PACK_ALIGN) * (PACK_ALIGN // PACK_COLS)

    send = jnp.concatenate([
        _flat_pad([a[n] for n in big_names], MM, BIG_ROWS * PACK_COLS).reshape(BIG_ROWS, PACK_COLS),
        _bits(_flat_pad([a[n] for n in small_names], F32, small_rows * PACK_COLS // bits_per)).reshape(small_rows, PACK_COLS),
    ], axis=0)
    gathered = all_gather(send, name="gather_weights")
    wb = _take(gathered[:, :BIG_ROWS].reshape(N_DEV, -1), big_names, shapes)
    ws = _take(_unbits(gathered[:, BIG_ROWS:].reshape(N_DEV, -1)), small_names, shapes)
    w = {n: _join(wb[n], ax) for n, ax in BIG}
    w.update({n: _join(ws[n], ax) for n, ax in SMALL})
    ln_g, ln_b = w["ln_g"], w["ln_b"]
    c_in_main = w["c_w_in"][0][:, :4 * C_WIDTH]
    c_in_gate = jnp.pad(w["c_w_in"][0][:, 4 * C_WIDTH:], ((0, 0), (0, 128 - 2 * C_HEADS)))
    wa_d, wx_d = _dense_blocks(a["b_wa"][0]), _dense_blocks(a["b_wx"][0])
    lru_w = (w["b_conv_w"][0], a["b_conv_b"][0], wa_d, a["b_ba"][0], wx_d, a["b_bx"][0], a["b_lam"][0])
    gdc_w = (a["c_a_log"][0], a["c_dt_bias"][0], a["c_norm_g"][0])

    h = x2
    saved = []
    for i in range(DEPTH):
        s = {"x0": h}
        s["y1"], s["z1"] = ffn_fwd(h, w["ffn1_wg"][i], w["ffn1_wu"][i], w["ffn1_wd"][i], ln_g[i, 0], ln_b[i, 0],
                                   name=f"ffn1_fwd_{i}")
        if i == 0:
            s["proj"] = matmul(s["y1"], w["ab_w_in"][0], mode="nn", name="ab_in_fwd")
            ya = attn_fwd(s["proj"], a["a_sinks"][0], bsz, name="attn_fwd")
            yb = lru_fwd(s["proj"], *lru_w, bsz, name="lru_fwd")
            s["mix"] = jnp.concatenate([ya, yb], axis=1)
            w_out = w["ab_w_out"][0]
        else:
            s["proj"] = matmul(s["y1"], c_in_main, mode="nn", name="c_in_fwd")
            s["gates"] = matmul(s["y1"], c_in_gate, mode="nn", name="c_gate_fwd")
            s["qkv"] = gdc_pre_fwd(s["proj"], w["c_conv_w"][0], bsz, name="gdc_pre_fwd")
            s["mix"], s["o_pre"], s["vnew"], s["tmat"], s["states"] = gdc_fwd(
                s["qkv"], s["proj"], s["gates"], *gdc_w, bsz, name="gdc_fwd")
            w_out = w["c_w_out"][0]
        s["y2"], s["z2"] = mm_ln_fwd(s["mix"], w_out, s["y1"], ln_g[i, 1], ln_b[i, 1], name=f"mix_out_fwd_{i}")
        s["y3"], s["z3"] = ffn_fwd(s["y2"], w["ffn2_wg"][i], w["ffn2_wu"][i], w["ffn2_wd"][i], ln_g[i, 2], ln_b[i, 2],
                                   name=f"ffn2_fwd_{i}")
        h = ple_fwd(s["y3"], p2[i], w["ple_wg"][i], a["ple_bg"][i], w["ple_wp"][i], name=f"ple_fwd_{i}")
        saved.append(s)
    loss_part, dh = loss_fwd_bwd(h, tgt, name="loss")

    g = {n: [None] * shapes[n][0] for n in ("ffn1_wg", "ffn1_wu", "ffn1_wd", "ffn2_wg", "ffn2_wu", "ffn2_wd", "ln_g",
                                             "ln_b", "ple_wg", "ple_bg", "ple_wp")}
    wide = dict(tm=1024, tn=1408, tk=512)
    tall = dict(tm=1408, tn=1024, tk=512)
    for i in reversed(range(DEPTH)):
        s = saved[i]
        dy3, dt, de, dbg = ple_bwd(dh, s["y3"], p2[i], w["ple_wg"][i], a["ple_bg"][i], w["ple_wp"][i], name=f"ple_bwd_{i}")
        g["ple_wg"][i] = matmul(s["y3"], dt, mode="tn", name=f"ple_wg_grad_{i}")
        g["ple_wp"][i] = matmul(p2[i], de, mode="tn", name=f"ple_wp_grad_{i}")
        g["ple_bg"][i] = dbg[0]
        dz3, dg2, db2 = ln_bwd(dy3, s["z3"], ln_g[i, 2], name=f"ln2_bwd_{i}")
        dy2, act, dhg, dhu = ffn_bwd(s["y2"], dz3, w["ffn2_wg"][i], w["ffn2_wu"][i], w["ffn2_wd"][i], name=f"ffn2_bwd_{i}")
        g["ffn2_wg"][i] = matmul(s["y2"], dhg, mode="tn", name=f"ffn2_wg_grad_{i}", **wide)
        g["ffn2_wu"][i] = matmul(s["y2"], dhu, mode="tn", name=f"ffn2_wu_grad_{i}", **wide)
        g["ffn2_wd"][i] = matmul(act, dz3, mode="tn", scale=0.5, name=f"ffn2_wd_grad_{i}", **tall)
        dz2, dg1, db1 = ln_bwd(dy2, s["z2"], ln_g[i, 1], name=f"ln1_bwd_{i}")
        if i == 0:
            dmix = matmul(dz2, w["ab_w_out"][0], mode="nt", name="ab_out_bwd")
            g["ab_w_out"] = matmul(s["mix"], dz2, mode="tn", name="ab_out_grad")
            dq, dk, dv, dsk = attn_bwd(s["proj"], a["a_sinks"][0], dmix, bsz, name="attn_bwd")
            dbx, dbgate, dcw, dcb, dwa, dba, dwx, dbxb, dlam = lru_bwd(s["proj"], *lru_w, dmix, bsz, name="lru_bwd")
            dproj = jnp.concatenate([dq, dk, dv, dbx, dbgate], axis=1).astype(MM)
            dy1 = matmul(dproj, w["ab_w_in"][0], mode="nt", add=dz2, add_scale=DN_ALPHA, name="ab_in_bwd")
            g["ab_w_in"] = matmul(s["y1"], dproj, mode="tn", name="ab_in_grad")
        else:
            dmix = matmul(dz2, w["c_w_out"][0], mode="nt", name="c_out_bwd")
            g["c_w_out"] = matmul(s["mix"], dz2, mode="tn", name="c_out_grad")
            dqkv, dzc, dgates, dsm = gdc_bwd(s["qkv"], s["proj"], s["gates"], *gdc_w, s["o_pre"], s["vnew"], s["tmat"],
                                             s["states"], dmix, bsz, name="gdc_bwd")
            draw, dccw = gdc_pre_bwd(s["proj"], w["c_conv_w"][0], dqkv, bsz, name="gdc_pre_bwd")
            dproj = jnp.concatenate([draw, dzc], axis=1).astype(MM)
            dgb = dgates.astype(MM)
            dy1 = matmul(dproj, c_in_main, mode="nt", add=dz2, add_scale=DN_ALPHA, name="c_in_bwd")
            dy1 = matmul(dgb, c_in_gate, mode="nt", add=dy1, name="c_gate_bwd")
            g["c_w_in"] = jnp.concatenate([matmul(s["y1"], dproj, mode="tn", name="c_in_grad"),
                                           matmul(s["y1"], dgb, mode="tn", name="c_gate_grad")[:, :2 * C_HEADS]], axis=1)
        dz1, dg0, db0 = ln_bwd(dy1, s["z1"], ln_g[i, 0], name=f"ln0_bwd_{i}")
        dh, act, dhg, dhu = ffn_bwd(s["x0"], dz1, w["ffn1_wg"][i], w["ffn1_wu"][i], w["ffn1_wd"][i], name=f"ffn1_bwd_{i}")
        g["ffn1_wg"][i] = matmul(s["x0"], dhg, mode="tn", name=f"ffn1_wg_grad_{i}", **wide)
        g["ffn1_wu"][i] = matmul(s["x0"], dhu, mode="tn", name=f"ffn1_wu_grad_{i}", **wide)
        g["ffn1_wd"][i] = matmul(act, dz1, mode="tn", scale=0.5, name=f"ffn1_wd_grad_{i}", **tall)
        g["ln_g"][i] = jnp.concatenate([dg0, dg1, dg2], axis=0)
        g["ln_b"][i] = jnp.concatenate([db0, db1, db2], axis=0)
    grad_x = dh.reshape(bsz, s_len, d)
    full = {n: jnp.stack(v) if isinstance(v, list) else v[None] for n, v in g.items()}
    full["b_conv_w"] = dcw[None]
    full["c_conv_w"] = jnp.sum(dccw, axis=0)[None]
    dsm_sum = jnp.sum(dsm, axis=0)
    full.update(a_sinks=jnp.sum(dsk, axis=0)[:, :A_HEADS], b_conv_b=dcb, b_wa=_diag_blocks(dwa)[None], b_ba=dba,
                b_wx=_diag_blocks(dwx)[None], b_bx=dbxb, b_lam=dlam, c_a_log=dsm_sum[None, :C_HEADS, 0],
                c_dt_bias=dsm_sum[None, :C_HEADS, 1], c_norm_g=jnp.sum(dsm_sum[C_HEADS:], axis=0)[None])

    small_cols = SMALL_F32 * bits_per // PACK_COLS
    repl_flat = _flat_pad([full[n] for n in REPL], F32, SMALL_F32 - n_small)
    small8 = jnp.concatenate([_flat8_pad([_split(full[n], ax) for n, ax in SMALL], F32, n_small),
                              jnp.broadcast_to(repl_flat, (N_DEV,) + repl_flat.shape)], axis=1)
    parts = jnp.concatenate([
        _flat8_pad([_split(full[n], ax) for n, ax in BIG], MM, BIG_ROWS * PACK_COLS).reshape(N_DEV, BIG_ROWS, PACK_COLS),
        _bits(small8).reshape(N_DEV, small_cols, PACK_COLS)], axis=1)
    recv = all_to_all(parts, name="exchange_grads")

    def mine(prefix, names, dtype_total):
        return _flat_pad([a[prefix + n] for n in names], F32, dtype_total)

    outs = {}
    big_total = BIG_ROWS * PACK_COLS
    res_big = adamw_sum(recv, *[mine(pre, big_names, big_total).reshape(BIG_ROWS, PACK_COLS) for pre in ("", "m_", "v_")],
                        name="adamw_big")
    small_all = small_names + REPL
    cols_f32 = PACK_COLS // bits_per
    res_small = adamw_sum(_unbits(recv[:, BIG_ROWS:]).reshape(N_DEV, small_cols, cols_f32),
                          *[mine(pre, small_all, SMALL_F32).reshape(small_cols, cols_f32) for pre in ("", "m_", "v_")],
                          name="adamw_small")
    kinds = []
    for rb, rs in zip(res_big, res_small):
        k = _take(rb.reshape(-1), big_names, shapes)
        k.update(_take(rs.reshape(-1), small_all, shapes))
        kinds.append(k)
    loss = lax.psum(loss_part[0, 0], ("x", "y", "c"))
    return (loss, grad_x, *[kinds[0][n] for n in WEIGHTS], *[kinds[1][n] for n in WEIGHTS],
            *[kinds[2][n] for n in WEIGHTS], *[kinds[3][n] for n in WEIGHTS])
```

```python
import functools
import math

import numpy as np
import jax
import jax.numpy as jnp
from jax import lax
from jax.experimental import pallas as pl
from jax.experimental.pallas import tpu as pltpu

F32 = jnp.float32
MM = jnp.bfloat16
HI = lax.Precision.HIGHEST

D_MODEL = 1024
D_FF = 2816
D_PLE = 256
DEPTH = 2
CHUNK = 64
A_HEADS = 8
A_KV_HEADS = 2
A_GROUP = 4
A_HEAD_DIM = 64
A_WIDTH = 512
A_KV_WIDTH = 128
B_WIDTH = 512
B_BLOCK = 64
RG_C = 8.0
AB_PROJ = 1792
C_HEADS = 8
C_HEAD_DIM = 128
C_WIDTH = 1024
DN_ALPHA = (2.0 * DEPTH) ** 0.25
LN_EPS = 1e-5
NORM_EPS = 1e-6
NEG = -1e30
ADAM_LR = 0.001
ADAM_B1 = 0.9
ADAM_B2 = 0.999
ADAM_EPS = 1e-08
ADAM_WD = 0.01
ADAM_STEP = 10
N_DEV = 8
VMEM_LIMIT = 56 * 1024 * 1024

NN = ((1,), (0,))
NT = ((1,), (1,))
TN = ((0,), (0,))


def _pcall(body, **kw):
    return pl.pallas_call(body, **kw)


def _cp(*sem):
    return pltpu.CompilerParams(dimension_semantics=sem, vmem_limit_bytes=VMEM_LIMIT)


def _dot(a, b, dims=NN, precision=None):
    return lax.dot_general(a, b, (dims, ((), ())), preferred_element_type=F32, precision=precision)


def _mdot(a, b, dims=NN):
    return _dot(a.astype(MM), b.astype(MM), dims)


def _tile(n, pref):
    if n <= pref:
        return n
    for c in range(pref - pref % 128, 0, -128):
        if n % c == 0:
            return c
    return n


def _sigmoid(x):
    return 1.0 / (1.0 + jnp.exp(-x))


def _softplus(x):
    return jnp.maximum(x, 0.0) + jnp.log(1.0 + jnp.exp(-jnp.abs(x)))


def _ln_stats(z):
    mu = jnp.mean(z, axis=-1, keepdims=True)
    zc = z - mu
    var = jnp.mean(zc * zc, axis=-1, keepdims=True)
    return zc, lax.rsqrt(var + LN_EPS)


def matmul(a, b, *, mode, name, tm=512, tn=512, tk=512, out_dtype=F32, scale=None, add=None, add_scale=1.0):
    if mode == "nn":
        (m, kk), (_, n) = a.shape, b.shape
        dims = NN
    elif mode == "nt":
        (m, kk), (n, _) = a.shape, b.shape
        dims = NT
    else:
        (kk, m), (_, n) = a.shape, b.shape
        dims = TN
    tm, tn, tk = _tile(m, tm), _tile(n, tn), _tile(kk, tk)
    if mode == "nn":
        a_spec = pl.BlockSpec((tm, tk), lambda i, j, k: (i, k))
        b_spec = pl.BlockSpec((tk, tn), lambda i, j, k: (k, j))
    elif mode == "nt":
        a_spec = pl.BlockSpec((tm, tk), lambda i, j, k: (i, k))
        b_spec = pl.BlockSpec((tn, tk), lambda i, j, k: (j, k))
    else:
        a_spec = pl.BlockSpec((tk, tm), lambda i, j, k: (k, i))
        b_spec = pl.BlockSpec((tk, tn), lambda i, j, k: (k, j))
    nk = kk // tk
    o_spec = pl.BlockSpec((tm, tn), lambda i, j, k: (i, j))
    has_add = add is not None

    def body(*refs):
        if has_add:
            a_ref, b_ref, add_ref, o_ref, acc_ref = refs
        else:
            a_ref, b_ref, o_ref, acc_ref = refs
        k = pl.program_id(2)

        @pl.when(k == 0)
        def _():
            acc_ref[...] = jnp.zeros_like(acc_ref)

        acc_ref[...] += _mdot(a_ref[...], b_ref[...], dims)

        @pl.when(k == nk - 1)
        def _():
            r = acc_ref[...]
            if scale is not None:
                r = r * scale
            if has_add:
                r = r + add_scale * add_ref[...].astype(F32)
            o_ref[...] = r.astype(out_dtype)

    ins = [a, b] + ([add] if has_add else [])
    in_specs = [a_spec, b_spec] + ([o_spec] if has_add else [])
    return _pcall(
        body, name=name, grid=(m // tm, n // tn, nk), in_specs=in_specs, out_specs=o_spec,
        out_shape=jax.ShapeDtypeStruct((m, n), out_dtype), scratch_shapes=[pltpu.VMEM((tm, tn), F32)],
        compiler_params=_cp("parallel", "parallel", "arbitrary"),
    )(*ins)


def ffn_fwd(x, wg, wu, wd, g, b, *, name, tm=512, tf=256):
    t, d = x.shape
    f = wg.shape[1]
    tm = min(tm, t)
    nj = f // tf

    def body(x_ref, wg_ref, wu_ref, wd_ref, g_ref, b_ref, y_ref, z_ref, hg_ref, hu_ref, xb_ref, acc_ref):
        j = pl.program_id(1)

        @pl.when(j == 0)
        def _():
            xb_ref[...] = x_ref[...].astype(MM)
            acc_ref[...] = jnp.zeros_like(acc_ref)

        xb = xb_ref[...]
        hg = _dot(xb, wg_ref[...])
        hu = _dot(xb, wu_ref[...])
        hg_ref[...] = hg.astype(MM)
        hu_ref[...] = hu.astype(MM)
        act = (hg * _sigmoid(hg) * hu).astype(MM)
        acc_ref[...] += _dot(act, wd_ref[...])

        @pl.when(j == nj - 1)
        def _():
            z = DN_ALPHA * x_ref[...] + 0.5 * acc_ref[...]
            z_ref[...] = z
            zc, rstd = _ln_stats(z)
            y_ref[...] = zc * rstd * g_ref[...] + b_ref[...]

    row = pl.BlockSpec((tm, d), lambda i, j: (i, 0))
    hid = pl.BlockSpec((tm, tf), lambda i, j: (i, j))
    vec = pl.BlockSpec((1, d), lambda i, j: (0, 0))
    return _pcall(
        body, name=name, grid=(t // tm, nj),
        in_specs=[row, pl.BlockSpec((d, tf), lambda i, j: (0, j)), pl.BlockSpec((d, tf), lambda i, j: (0, j)),
                  pl.BlockSpec((tf, d), lambda i, j: (j, 0)), vec, vec],
        out_specs=[row, row, hid, hid],
        out_shape=[jax.ShapeDtypeStruct((t, d), F32), jax.ShapeDtypeStruct((t, d), F32),
                   jax.ShapeDtypeStruct((t, f), MM), jax.ShapeDtypeStruct((t, f), MM)],
        scratch_shapes=[pltpu.VMEM((tm, d), MM), pltpu.VMEM((tm, d), F32)],
        compiler_params=_cp("parallel", "arbitrary"),
    )(x, wg, wu, wd, g.reshape(1, d), b.reshape(1, d))


def ffn_bwd(dz, hg, hu, wg, wu, wd, *, name, tm=512, tf=256):
    t, d = dz.shape
    f = wg.shape[1]
    tm = min(tm, t)
    nj = f // tf

    def body(dz_ref, hg_ref, hu_ref, wg_ref, wu_ref, wd_ref, dx_ref, act_ref, dhg_ref, dhu_ref, dfb_ref, acc_ref):
        j = pl.program_id(1)

        @pl.when(j == 0)
        def _():
            dfb_ref[...] = (0.5 * dz_ref[...]).astype(MM)
            acc_ref[...] = jnp.zeros_like(acc_ref)

        hg = hg_ref[...].astype(F32)
        hu = hu_ref[...].astype(F32)
        s = _sigmoid(hg)
        dact = _dot(dfb_ref[...], wd_ref[...], NT)
        sg = hg * s
        act_ref[...] = (sg * hu).astype(MM)
        dhu = (dact * sg).astype(MM)
        dhg = (dact * hu * (s + sg * (1.0 - s))).astype(MM)
        dhu_ref[...] = dhu
        dhg_ref[...] = dhg
        acc_ref[...] += _dot(dhg, wg_ref[...], NT) + _dot(dhu, wu_ref[...], NT)

        @pl.when(j == nj - 1)
        def _():
            dx_ref[...] = DN_ALPHA * dz_ref[...] + acc_ref[...]

    row = pl.BlockSpec((tm, d), lambda i, j: (i, 0))
    hid = pl.BlockSpec((tm, tf), lambda i, j: (i, j))
    return _pcall(
        body, name=name, grid=(t // tm, nj),
        in_specs=[row, hid, hid, pl.BlockSpec((d, tf), lambda i, j: (0, j)), pl.BlockSpec((d, tf), lambda i, j: (0, j)),
                  pl.BlockSpec((tf, d), lambda i, j: (j, 0))],
        out_specs=[row, hid, hid, hid],
        out_shape=[jax.ShapeDtypeStruct((t, d), F32)] + [jax.ShapeDtypeStruct((t, f), MM)] * 3,
        scratch_shapes=[pltpu.VMEM((tm, d), MM), pltpu.VMEM((tm, d), F32)],
        compiler_params=_cp("parallel", "arbitrary"),
    )(dz, hg, hu, wg, wu, wd)


def ln_bwd(dy, z, g, *, name, tm=512):
    t, d = z.shape
    tm = min(tm, t)

    def body(dy_ref, z_ref, g_ref, dz_ref, dg_ref, db_ref):
        i = pl.program_id(0)

        @pl.when(i == 0)
        def _():
            dg_ref[...] = jnp.zeros_like(dg_ref)
            db_ref[...] = jnp.zeros_like(db_ref)

        dy = dy_ref[...]
        zc, rstd = _ln_stats(z_ref[...])
        xh = zc * rstd
        dg_ref[...] += jnp.sum(dy * xh, axis=0, keepdims=True)
        db_ref[...] += jnp.sum(dy, axis=0, keepdims=True)
        dxh = dy * g_ref[...]
        m1 = jnp.mean(dxh, axis=-1, keepdims=True)
        m2 = jnp.mean(dxh * xh, axis=-1, keepdims=True)
        dz_ref[...] = rstd * (dxh - m1 - xh * m2)

    row = pl.BlockSpec((tm, d), lambda i: (i, 0))
    vec = pl.BlockSpec((1, d), lambda i: (0, 0))
    return _pcall(
        body, name=name, grid=(t // tm,), in_specs=[row, row, vec], out_specs=[row, vec, vec],
        out_shape=[jax.ShapeDtypeStruct((t, d), F32), jax.ShapeDtypeStruct((1, d), F32), jax.ShapeDtypeStruct((1, d), F32)],
        compiler_params=_cp("arbitrary"),
    )(dy, z, g.reshape(1, d))


def mm_ln_fwd(a, w, res, g, b, *, name, tm=512):
    t, kk = a.shape
    d = w.shape[1]
    tm = min(tm, t)

    def body(a_ref, w_ref, res_ref, g_ref, b_ref, y_ref, z_ref):
        z = DN_ALPHA * res_ref[...] + _mdot(a_ref[...], w_ref[...])
        z_ref[...] = z
        zc, rstd = _ln_stats(z)
        y_ref[...] = zc * rstd * g_ref[...] + b_ref[...]

    row = pl.BlockSpec((tm, d), lambda i: (i, 0))
    vec = pl.BlockSpec((1, d), lambda i: (0, 0))
    return _pcall(
        body, name=name, grid=(t // tm,),
        in_specs=[pl.BlockSpec((tm, kk), lambda i: (i, 0)), pl.BlockSpec((kk, d), lambda i: (0, 0)), row, vec, vec],
        out_specs=[row, row],
        out_shape=[jax.ShapeDtypeStruct((t, d), F32), jax.ShapeDtypeStruct((t, d), F32)],
        compiler_params=_cp("parallel"),
    )(a, w, res, g.reshape(1, d), b.reshape(1, d))


def ple_fwd(y, p, wg, bg, wp, *, name, tm=512):
    t, d = y.shape
    dp = p.shape[1]
    tm = min(tm, t)

    def body(y_ref, p_ref, wg_ref, bg_ref, wp_ref, o_ref):
        yv = y_ref[...]
        gate = _sigmoid(_mdot(yv, wg_ref[...]) + bg_ref[...])
        o_ref[...] = yv + gate * _mdot(p_ref[...], wp_ref[...])

    row = pl.BlockSpec((tm, d), lambda i: (i, 0))
    return _pcall(
        body, name=name, grid=(t // tm,),
        in_specs=[row, pl.BlockSpec((tm, dp), lambda i: (i, 0)), pl.BlockSpec((d, d), lambda i: (0, 0)),
                  pl.BlockSpec((1, d), lambda i: (0, 0)), pl.BlockSpec((dp, d), lambda i: (0, 0))],
        out_specs=row, out_shape=jax.ShapeDtypeStruct((t, d), F32), compiler_params=_cp("parallel"),
    )(y, p, wg, bg.reshape(1, d), wp)


def ple_bwd(do, y, p, wg, bg, wp, *, name, tm=512):
    t, d = y.shape
    dp = p.shape[1]
    tm = min(tm, t)

    def body(do_ref, y_ref, p_ref, wg_ref, bg_ref, wp_ref, dy_ref, dt_ref, de_ref, dbg_ref):
        i = pl.program_id(0)

        @pl.when(i == 0)
        def _():
            dbg_ref[...] = jnp.zeros_like(dbg_ref)

        dov = do_ref[...]
        gate = _sigmoid(_mdot(y_ref[...], wg_ref[...]) + bg_ref[...])
        emb = _mdot(p_ref[...], wp_ref[...])
        dt = dov * emb * gate * (1.0 - gate)
        dbg_ref[...] += jnp.sum(dt, axis=0, keepdims=True)
        dtb = dt.astype(MM)
        dt_ref[...] = dtb
        de_ref[...] = (dov * gate).astype(MM)
        dy_ref[...] = dov + _dot(dtb, wg_ref[...], NT)

    row = pl.BlockSpec((tm, d), lambda i: (i, 0))
    vec = pl.BlockSpec((1, d), lambda i: (0, 0))
    return _pcall(
        body, name=name, grid=(t // tm,),
        in_specs=[row, row, pl.BlockSpec((tm, dp), lambda i: (i, 0)), pl.BlockSpec((d, d), lambda i: (0, 0)),
                  vec, pl.BlockSpec((dp, d), lambda i: (0, 0))],
        out_specs=[row, row, row, vec],
        out_shape=[jax.ShapeDtypeStruct((t, d), F32), jax.ShapeDtypeStruct((t, d), MM),
                   jax.ShapeDtypeStruct((t, d), MM), jax.ShapeDtypeStruct((1, d), F32)],
        compiler_params=_cp("arbitrary"),
    )(do, y, p, wg, bg.reshape(1, d), wp)


def loss_fwd_bwd(y, tgt, *, name, tm=512):
    t, d = y.shape
    tm = min(tm, t)

    def body(y_ref, t_ref, l_ref, dy_ref):
        i = pl.program_id(0)

        @pl.when(i == 0)
        def _():
            l_ref[...] = jnp.zeros_like(l_ref)

        err = y_ref[...] - t_ref[...]
        dy_ref[...] = err * (1.0 / d)
        l_ref[...] += (0.5 / d) * jnp.sum(jnp.sum(err * err, axis=1, keepdims=True), axis=0, keepdims=True)

    row = pl.BlockSpec((tm, d), lambda i: (i, 0))
    return _pcall(
        body, name=name, grid=(t // tm,), in_specs=[row, row],
        out_specs=[pl.BlockSpec((1, 128), lambda i: (0, 0)), row],
        out_shape=[jax.ShapeDtypeStruct((1, 128), F32), jax.ShapeDtypeStruct((t, d), F32)],
        compiler_params=_cp("arbitrary"),
    )(y, tgt)


def _shift_dn(x, s, row):
    return x if s == 0 else jnp.where(row >= s, pltpu.roll(x, s, 0), 0.0)


def _shift_up(x, s, row):
    n = x.shape[0]
    return x if s == 0 else jnp.where(row < n - s, pltpu.roll(x, n - s, 0), 0.0)


def _conv_fwd(x, w, row):
    kk = w.shape[0]
    y = w[kk - 1:kk, :] * x
    for j in range(kk - 1):
        y = y + w[j:j + 1, :] * _shift_dn(x, kk - 1 - j, row)
    return y


def _conv_bwd(x, w, dy, row):
    kk = w.shape[0]
    dx = w[kk - 1:kk, :] * dy
    dws = []
    for j in range(kk - 1):
        dx = dx + w[j:j + 1, :] * _shift_up(dy, kk - 1 - j, row)
        dws.append(jnp.sum(dy * _shift_dn(x, kk - 1 - j, row), axis=0, keepdims=True))
    dws.append(jnp.sum(dy * x, axis=0, keepdims=True))
    return dx, jnp.concatenate(dws, axis=0)


def _gelu(x):
    c = math.sqrt(2.0 / math.pi)
    th = jnp.tanh(c * (x + 0.044715 * x * x * x))
    return 0.5 * x * (1.0 + th), th


def _gelu_grad(x, th):
    c = math.sqrt(2.0 / math.pi)
    return 0.5 * (1.0 + th) + 0.5 * x * (1.0 - th * th) * c * (1.0 + 3.0 * 0.044715 * x * x)


def _neg_expm1(y):
    ser = -(y * (1.0 + y * (0.5 + y * (1.0 / 6.0 + y * (1.0 / 24.0 + y * (1.0 / 120.0))))))
    return jnp.where(y > -0.05, ser, 1.0 - jnp.exp(y))


def _attn_head(qh, kk, vv, bias, valid, sink):
    s = _mdot(qh, kk, NT) * (A_HEAD_DIM ** -0.5) - bias
    s = jnp.where(valid, s, NEG)
    m = jnp.maximum(jnp.max(s, axis=-1, keepdims=True), sink)
    pr = jnp.exp(s - m)
    den = jnp.sum(pr, axis=-1, keepdims=True) + jnp.exp(sink - m)
    return pr / den, jnp.exp(sink - m) / den


def _attn_valid(n):
    ji = lax.broadcasted_iota(jnp.int32, (1, 3 * CHUNK), 1)
    return (n * CHUNK + ji - 2 * CHUNK) >= 0


def _attn_group_consts(kh, sk_ref):
    rows = A_GROUP * CHUNK
    ri = lax.broadcasted_iota(jnp.int32, (rows, 3 * CHUNK), 0)
    ji = lax.broadcasted_iota(jnp.int32, (rows, 3 * CHUNK), 1)
    dist = jnp.abs((ri & (CHUNK - 1)) + 2 * CHUNK - ji).astype(F32)
    rcol = lax.broadcasted_iota(jnp.int32, (rows, 1), 0)
    slope = jnp.zeros((rows, 1), F32)
    sink = jnp.zeros((rows, 1), F32)
    for gi in range(A_GROUP):
        h = kh * A_GROUP + gi
        inblk = (rcol >= gi * CHUNK) & (rcol < (gi + 1) * CHUNK)
        slope = jnp.where(inblk, 2.0 ** -(h + 1), slope)
        sink = jnp.where(inblk, sk_ref[h], sink)
    return slope * dist, sink


def _stack_heads(x, kh):
    return jnp.concatenate([x[:, (kh * A_GROUP + gi) * 64:(kh * A_GROUP + gi + 1) * 64] for gi in range(A_GROUP)], axis=0)


def _attn_masks(n):
    ci = lax.broadcasted_iota(jnp.int32, (CHUNK, 3 * CHUNK), 0)
    ji = lax.broadcasted_iota(jnp.int32, (CHUNK, 3 * CHUNK), 1)
    dist = jnp.abs(ci + 2 * CHUNK - ji).astype(F32)
    valid = (n * CHUNK + ji - 2 * CHUNK) >= 0
    return dist, valid


def attn_fwd(proj, sinks, bsz, *, name):
    t = proj.shape[0]
    s_len = t // bsz
    nc = s_len // CHUNK
    pad = 2 * CHUNK

    def body(q_ref, k_ref, v_ref, sk_ref, o_ref, kp_ref, vp_ref):
        kp_ref[0:pad, :] = jnp.zeros((pad, A_KV_WIDTH), F32)
        vp_ref[0:pad, :] = jnp.zeros((pad, A_KV_WIDTH), F32)
        kp_ref[pad:, :] = k_ref[...]
        vp_ref[pad:, :] = v_ref[...]

        consts = [_attn_group_consts(kh, sk_ref) for kh in range(A_KV_HEADS)]

        def chunk(n, carry):
            st = pl.multiple_of(n * CHUNK, CHUNK)
            q = q_ref[pl.ds(st, CHUNK), :]
            kb = kp_ref[pl.ds(st, 3 * CHUNK), :]
            vb = vp_ref[pl.ds(st, 3 * CHUNK), :]
            valid = _attn_valid(n)
            outs = []
            for kh in range(A_KV_HEADS):
                bias, sink = consts[kh]
                pn, _ = _attn_head(_stack_heads(q, kh), kb[:, kh * 64:(kh + 1) * 64], None, bias, valid, sink)
                o = _mdot(pn, vb[:, kh * 64:(kh + 1) * 64])
                outs += [o[gi * CHUNK:(gi + 1) * CHUNK] for gi in range(A_GROUP)]
            o_ref[pl.ds(st, CHUNK), :] = jnp.concatenate(outs, axis=-1)
            return carry

        lax.fori_loop(0, nc, chunk, 0)

    return _pcall(
        body, name=name, grid=(bsz,),
        in_specs=[pl.BlockSpec((s_len, A_WIDTH), lambda b: (b, 0)), pl.BlockSpec((s_len, 128), lambda b: (b, 4)),
                  pl.BlockSpec((s_len, 128), lambda b: (b, 5)), pl.BlockSpec(memory_space=pltpu.SMEM)],
        out_specs=pl.BlockSpec((s_len, A_WIDTH), lambda b: (b, 0)),
        out_shape=jax.ShapeDtypeStruct((t, A_WIDTH), F32),
        scratch_shapes=[pltpu.VMEM((s_len + pad, A_KV_WIDTH), F32), pltpu.VMEM((s_len + pad, A_KV_WIDTH), F32)],
        compiler_params=_cp("parallel"),
    )(proj, proj, proj, sinks)


def attn_bwd(proj, sinks, dcat, bsz, *, name):
    t = proj.shape[0]
    s_len = t // bsz
    nc = s_len // CHUNK
    pad = 2 * CHUNK

    def body(q_ref, k_ref, v_ref, do_ref, sk_ref, dq_ref, dk_ref, dv_ref, dsk_ref, kp_ref, vp_ref, dkp_ref, dvp_ref):
        kp_ref[0:pad, :] = jnp.zeros((pad, A_KV_WIDTH), F32)
        vp_ref[0:pad, :] = jnp.zeros((pad, A_KV_WIDTH), F32)
        kp_ref[pad:, :] = k_ref[...]
        vp_ref[pad:, :] = v_ref[...]
        dkp_ref[...] = jnp.zeros_like(dkp_ref)
        dvp_ref[...] = jnp.zeros_like(dvp_ref)
        lane = lax.broadcasted_iota(jnp.int32, (1, 128), 1)

        consts = [_attn_group_consts(kh, sk_ref) for kh in range(A_KV_HEADS)]

        def chunk(n, dsk):
            st = pl.multiple_of(n * CHUNK, CHUNK)
            q = q_ref[pl.ds(st, CHUNK), :]
            do = do_ref[pl.ds(st, CHUNK), :]
            kb = kp_ref[pl.ds(st, 3 * CHUNK), :]
            vb = vp_ref[pl.ds(st, 3 * CHUNK), :]
            valid = _attn_valid(n)
            dqs, dks, dvs = [], [], []
            for kh in range(A_KV_HEADS):
                kk = kb[:, kh * 64:(kh + 1) * 64]
                vv = vb[:, kh * 64:(kh + 1) * 64]
                bias, sink = consts[kh]
                qs = _stack_heads(q, kh)
                dos = _stack_heads(do, kh)
                pn, psink = _attn_head(qs, kk, None, bias, valid, sink)
                dp = _mdot(dos, vv, NT)
                rowdot = jnp.sum(pn * dp, axis=-1, keepdims=True)
                ds = pn * (dp - rowdot)
                sink_part = psink * rowdot
                for gi in range(A_GROUP):
                    part = jnp.sum(sink_part[gi * CHUNK:(gi + 1) * CHUNK], axis=0, keepdims=True)
                    dsk = dsk + jnp.where(lane == kh * A_GROUP + gi, -part, 0.0)
                dq = _mdot(ds, kk) * (A_HEAD_DIM ** -0.5)
                dqs += [dq[gi * CHUNK:(gi + 1) * CHUNK] for gi in range(A_GROUP)]
                dks.append(_mdot(ds, qs, TN) * (A_HEAD_DIM ** -0.5))
                dvs.append(_mdot(pn, dos, TN))
            dq_ref[pl.ds(st, CHUNK), :] = jnp.concatenate(dqs, axis=-1)
            dkp_ref[pl.ds(st, 3 * CHUNK), :] += jnp.concatenate(dks, axis=-1)
            dvp_ref[pl.ds(st, 3 * CHUNK), :] += jnp.concatenate(dvs, axis=-1)
            return dsk

        dsk = lax.fori_loop(0, nc, chunk, jnp.zeros((1, 128), F32))
        dsk_ref[0] = dsk
        dk_ref[...] = dkp_ref[pad:, :]
        dv_ref[...] = dvp_ref[pad:, :]

    kv = jax.ShapeDtypeStruct((t, A_KV_WIDTH), F32)
    return _pcall(
        body, name=name, grid=(bsz,),
        in_specs=[pl.BlockSpec((s_len, A_WIDTH), lambda b: (b, 0)), pl.BlockSpec((s_len, 128), lambda b: (b, 4)),
                  pl.BlockSpec((s_len, 128), lambda b: (b, 5)), pl.BlockSpec((s_len, A_WIDTH), lambda b: (b, 0)),
                  pl.BlockSpec(memory_space=pltpu.SMEM)],
        out_specs=[pl.BlockSpec((s_len, A_WIDTH), lambda b: (b, 0)), pl.BlockSpec((s_len, 128), lambda b: (b, 0)),
                   pl.BlockSpec((s_len, 128), lambda b: (b, 0)), pl.BlockSpec((1, 1, 128), lambda b: (b, 0, 0))],
        out_shape=[jax.ShapeDtypeStruct((t, A_WIDTH), F32), kv, kv, jax.ShapeDtypeStruct((bsz, 1, 128), F32)],
        scratch_shapes=[pltpu.VMEM((s_len + pad, A_KV_WIDTH), F32)] * 4,
        compiler_params=_cp("parallel"),
    )(proj, proj, proj, dcat, sinks)


def _lru_gates(x, cw, cb, wa, ba, wx, bx, lam, row):
    xc = _conv_fwd(x, cw, row) + cb
    r = _sigmoid(_mdot(xc, wa) + ba)
    i = _sigmoid(_mdot(xc, wx) + bx)
    sp = _softplus(-lam)
    log_a = -RG_C * r * sp
    a = jnp.exp(log_a)
    mult = jnp.sqrt(_neg_expm1(2.0 * log_a))
    return xc, r, i, sp, a, mult


def _lru_scan(a, u, row):
    n = a.shape[0]
    d = 1
    while d < n:
        a_sh = jnp.where(row >= d, pltpu.roll(a, d, 0), 1.0)
        u_sh = jnp.where(row >= d, pltpu.roll(u, d, 0), 0.0)
        u = a * u_sh + u
        a = a * a_sh
        d *= 2
    return u


def _lru_scan_rev(a, u, row):
    n = a.shape[0]
    d = 1
    while d < n:
        a_sh = jnp.where(row < n - d, pltpu.roll(a, n - d, 0), 1.0)
        u_sh = jnp.where(row < n - d, pltpu.roll(u, n - d, 0), 0.0)
        u = a * u_sh + u
        a = a * a_sh
        d *= 2
    return u


def _lru_specs(s_len, order):
    def at(f):
        return lambda *g: f(*order(*g))
    return [pl.BlockSpec((s_len, 128), at(lambda b, cb: (b, 6 + cb))), pl.BlockSpec((s_len, 128), at(lambda b, cb: (b, 10 + cb))),
            pl.BlockSpec((4, 128), at(lambda b, cb: (0, cb))), pl.BlockSpec((1, 128), at(lambda b, cb: (0, cb))),
            pl.BlockSpec((1, 128, 128), at(lambda b, cb: (cb, 0, 0))), pl.BlockSpec((1, 128), at(lambda b, cb: (0, cb))),
            pl.BlockSpec((1, 128, 128), at(lambda b, cb: (cb, 0, 0))), pl.BlockSpec((1, 128), at(lambda b, cb: (0, cb))),
            pl.BlockSpec((1, 128), at(lambda b, cb: (0, cb)))]


def lru_fwd(proj, cw, cb, wa, ba, wx, bxb, lam, bsz, *, name):
    t = proj.shape[0]
    s_len = t // bsz

    def body(x_ref, g_ref, cw_ref, cb_ref, wa_ref, ba_ref, wx_ref, bx_ref, lam_ref, y_ref):
        row = lax.broadcasted_iota(jnp.int32, (s_len, 128), 0)
        xc, r, i, sp, a, mult = _lru_gates(x_ref[...], cw_ref[...], cb_ref[...], wa_ref[0], ba_ref[...], wx_ref[0],
                                           bx_ref[...], lam_ref[...], row)
        h = _lru_scan(a, mult * (i * xc), row)
        y_ref[...] = h * _gelu(g_ref[...])[0]

    return _pcall(
        body, name=name, grid=(bsz, 4), in_specs=_lru_specs(s_len, lambda b, cb: (b, cb)),
        out_specs=pl.BlockSpec((s_len, 128), lambda b, cb: (b, cb)),
        out_shape=jax.ShapeDtypeStruct((t, B_WIDTH), F32), compiler_params=_cp("parallel", "parallel"),
    )(proj, proj, cw, cb.reshape(1, -1), wa, ba.reshape(1, -1), wx, bxb.reshape(1, -1), lam.reshape(1, -1))


def lru_bwd(proj, cw, cb, wa, ba, wx, bxb, lam, dcat, bsz, *, name):
    t = proj.shape[0]
    s_len = t // bsz

    def body(x_ref, g_ref, cw_ref, cb_ref, wa_ref, ba_ref, wx_ref, bx_ref, lam_ref, dy_ref,
             dx_ref, dg_ref, dcw_ref, dcb_ref, dwa_ref, dba_ref, dwx_ref, dbx_ref, dlam_ref):
        b = pl.program_id(1)
        row = lax.broadcasted_iota(jnp.int32, (s_len, 128), 0)
        x = x_ref[...]
        lam = lam_ref[...]
        xc, r, i, sp, a, mult = _lru_gates(x, cw_ref[...], cb_ref[...], wa_ref[0], ba_ref[...], wx_ref[0], bx_ref[...],
                                           lam, row)
        ixc = i * xc
        h = _lru_scan(a, mult * ixc, row)
        gv = g_ref[...]
        gl, th = _gelu(gv)
        dy = dy_ref[...]
        dg_ref[...] = dy * h * _gelu_grad(gv, th)
        gr = _lru_scan_rev(_shift_up(a, 1, row), dy * gl, row)
        da = gr * _shift_dn(h, 1, row)
        dmult = gr * ixc
        di = gr * mult * xc
        dxc = gr * mult * i
        dlog_a = da * a - dmult * (a * a) / mult
        dr = dlog_a * (-RG_C * sp)
        dlam = jnp.sum(dlog_a * r, axis=0, keepdims=True) * (RG_C * _sigmoid(-lam))
        dpa = dr * r * (1.0 - r)
        dpx = di * i * (1.0 - i)
        dxc = dxc + _mdot(dpa, wa_ref[0], NT) + _mdot(dpx, wx_ref[0], NT)
        dx, dcw = _conv_bwd(x, cw_ref[...], dxc, row)
        dx_ref[...] = dx

        @pl.when(b == 0)
        def _():
            for ref in (dcw_ref, dcb_ref, dwa_ref, dba_ref, dwx_ref, dbx_ref, dlam_ref):
                ref[...] = jnp.zeros_like(ref)

        dcw_ref[...] += dcw
        dcb_ref[...] += jnp.sum(dxc, axis=0, keepdims=True)
        dwa_ref[0] += _mdot(xc, dpa, TN)
        dwx_ref[0] += _mdot(xc, dpx, TN)
        dba_ref[...] += jnp.sum(dpa, axis=0, keepdims=True)
        dbx_ref[...] += jnp.sum(dpx, axis=0, keepdims=True)
        dlam_ref[...] += dlam

    order = lambda cb, b: (b, cb)
    act = pl.BlockSpec((s_len, 128), lambda cb, b: (b, cb))
    vec = pl.BlockSpec((1, 128), lambda cb, b: (0, cb))
    mat = pl.BlockSpec((1, 128, 128), lambda cb, b: (cb, 0, 0))
    vshape = jax.ShapeDtypeStruct((1, B_WIDTH), F32)
    mshape = jax.ShapeDtypeStruct((4, 128, 128), F32)
    return _pcall(
        body, name=name, grid=(4, bsz),
        in_specs=_lru_specs(s_len, order) + [pl.BlockSpec((s_len, 128), lambda cb, b: (b, 4 + cb))],
        out_specs=[act, act, pl.BlockSpec((4, 128), lambda cb, b: (0, cb)), vec, mat, vec, mat, vec, vec],
        out_shape=[jax.ShapeDtypeStruct((t, B_WIDTH), F32), jax.ShapeDtypeStruct((t, B_WIDTH), F32),
                   jax.ShapeDtypeStruct((4, B_WIDTH), F32), vshape, mshape, vshape, mshape, vshape, vshape],
        compiler_params=_cp("parallel", "arbitrary"),
    )(proj, proj, cw, cb.reshape(1, -1), wa, ba.reshape(1, -1), wx, bxb.reshape(1, -1), lam.reshape(1, -1), dcat)


_BDIMS = {"nn": ((2,), (1,)), "nt": ((2,), (2,)), "tn": ((1,), (1,))}
C_QSCALE = C_HEAD_DIM ** -0.5


def _bmm(a, b, mode, exact=False):
    dims = (_BDIMS[mode], ((0,), (0,)))
    if exact:
        return lax.dot_general(a, b, dims, preferred_element_type=F32, precision=lax.Precision.HIGH)
    return lax.dot_general(a.astype(MM), b.astype(MM), dims, preferred_element_type=F32)


def _col(x, idx, lane):
    return jnp.broadcast_to(jnp.sum(jnp.where(lane == idx, x, 0.0), axis=-1, keepdims=True), x.shape)


def _seg_cumsum(g, row):
    pos = row & (CHUNK - 1)
    d = 1
    while d < CHUNK:
        g = g + jnp.where(pos >= d, pltpu.roll(g, d, 0), 0.0)
        d *= 2
    return g


def _seg_cumsum_rev(g, row):
    pos = row & (CHUNK - 1)
    n = g.shape[0]
    d = 1
    while d < CHUNK:
        g = g + jnp.where(pos < CHUNK - d, pltpu.roll(g, n - d, 0), 0.0)
        d *= 2
    return g


def _gdn_prep(qr, kr, vr, gates, cwq, cwk, cwv, a_log, dtb, h):
    s_len = qr.shape[0]
    nc = s_len // CHUNK
    row = lax.broadcasted_iota(jnp.int32, (s_len, 128), 0)
    lane = lax.broadcasted_iota(jnp.int32, (s_len, 128), 1)
    r = {"row": row, "lane": lane}
    for nm, x, w in (("q", qr, cwq), ("k", kr, cwk), ("v", vr, cwv)):
        c = _conv_fwd(x, w, row)
        sg = _sigmoid(c)
        r["c" + nm], r["s" + nm], r[nm + "c"] = c, sg, c * sg
    r["rq"] = lax.rsqrt(jnp.sum(r["qc"] * r["qc"], axis=-1, keepdims=True) + NORM_EPS)
    r["rk"] = lax.rsqrt(jnp.sum(r["kc"] * r["kc"], axis=-1, keepdims=True) + NORM_EPS)
    r["qn"] = r["qc"] * r["rq"]
    r["kn"] = r["kc"] * r["rk"]
    r["beta"] = _sigmoid(_col(gates, h, lane))
    r["A"] = jnp.exp(a_log)
    r["pre"] = _col(gates, 8 + h, lane) + dtb
    r["sp"] = _softplus(r["pre"])
    gc = _seg_cumsum(-r["A"] * r["sp"], row)
    sh = (nc, CHUNK, 128)
    q3 = (r["qn"] * C_QSCALE).reshape(sh)
    k3 = r["kn"].reshape(sh)
    v3 = r["vc"].reshape(sh)
    beta3 = r["beta"].reshape(sh)
    gc3 = gc.reshape(sh)
    gcl3 = gc3[:, CHUNK - 1:CHUNK, :]
    eg = jnp.exp(gc3)
    ekd = jnp.exp(gcl3 - gc3)
    col64 = gc3[:, :, :CHUNK]
    row64 = jnp.swapaxes(gc3, 1, 2)[:, :CHUNK, :]
    ii = lax.broadcasted_iota(jnp.int32, (nc, CHUNK, CHUNK), 1)
    jj = lax.broadcasted_iota(jnp.int32, (nc, CHUNK, CHUNK), 2)
    tril = ii >= jj
    strict = ii > jj
    dm = jnp.where(tril, jnp.exp(jnp.where(tril, col64 - row64, 0.0)), 0.0)
    kb = k3 * beta3
    lmat = jnp.where(strict, _bmm(kb, k3, "nt") * dm, 0.0)
    attn = _bmm(q3, k3, "nt") * dm
    r.update(q3=q3, k3=k3, v3=v3, beta3=beta3, gc3=gc3, eg=eg, ekd=ekd, gl=jnp.exp(gcl3), dm=dm, kb=kb, lmat=lmat,
             attn=attn, strict=strict, tril=tril, qg=q3 * eg, kdec=k3 * ekd)
    return r


def _neumann_inverse(lmat):
    ii = lax.broadcasted_iota(jnp.int32, lmat.shape, 1)
    jj = lax.broadcasted_iota(jnp.int32, lmat.shape, 2)
    x = -lmat
    tm = jnp.where(ii == jj, 1.0, 0.0) + x
    pw = x
    for _ in range(5):
        pw = _bmm(pw, pw, "nn", exact=True)
        tm = tm + _bmm(tm, pw, "nn", exact=True)
    return tm


def _gdn_specs(s_len):
    act = lambda off: pl.BlockSpec((s_len, 128), lambda b, h: (b, off + h))
    cw = lambda off: pl.BlockSpec((4, 128), lambda b, h: (0, off + h))
    smem = pl.BlockSpec(memory_space=pltpu.SMEM)
    return [act(0), act(8), act(16), act(24), pl.BlockSpec((s_len, 128), lambda b, h: (b, 0)), cw(0), cw(8), cw(16),
            smem, smem, pl.BlockSpec((1, 128), lambda b, h: (0, 0))]


def gdn_fwd(proj, gates, cw, a_log, dtb, ng, bsz, *, name):
    t = proj.shape[0]
    s_len = t // bsz
    nc = s_len // CHUNK

    def body(q_ref, k_ref, v_ref, z_ref, gt_ref, cwq_ref, cwk_ref, cwv_ref, al_ref, dt_ref, ng_ref,
             out_ref, o_ref, vn_ref, tm_ref, st_ref, u_s, w_s, qg_s, kd_s, at_s, gl_s):
        h = pl.program_id(1)
        r = _gdn_prep(q_ref[...], k_ref[...], v_ref[...], gt_ref[...], cwq_ref[...], cwk_ref[...], cwv_ref[...],
                      al_ref[h], dt_ref[h], h)
        tm = _neumann_inverse(r["lmat"])
        tm_ref[0, 0] = tm
        u_s[...] = _bmm(tm, r["v3"] * r["beta3"], "nn", exact=True)
        w_s[...] = _bmm(tm, r["kb"] * r["eg"], "nn", exact=True)
        qg_s[...] = r["qg"]
        kd_s[...] = r["kdec"]
        at_s[...] = r["attn"]
        gl_s[...] = r["gl"]

        def chunk(n, state):
            st = pl.multiple_of(n * CHUNK, CHUNK)
            st_ref[0, 0, n] = state
            v_new = u_s[n] - _mdot(w_s[n], state)
            o_ref[pl.ds(st, CHUNK), :] = _mdot(qg_s[n], state) + _mdot(at_s[n], v_new)
            vn_ref[pl.ds(st, CHUNK), :] = v_new
            return state * gl_s[n] + _mdot(kd_s[n], v_new, TN)

        lax.fori_loop(0, nc, chunk, jnp.zeros((128, 128), F32))
        o = o_ref[...]
        rms = lax.rsqrt(jnp.mean(o * o, axis=-1, keepdims=True) + NORM_EPS)
        z = z_ref[...]
        out_ref[...] = o * rms * ng_ref[...] * (z * _sigmoid(z))

    blk = pl.BlockSpec((s_len, 128), lambda b, h: (b, h))
    full = jax.ShapeDtypeStruct((t, C_WIDTH), F32)
    return _pcall(
        body, name=name, grid=(bsz, C_HEADS), in_specs=_gdn_specs(s_len),
        out_specs=[blk, blk, blk, pl.BlockSpec((1, 1, nc, CHUNK, CHUNK), lambda b, h: (b, h, 0, 0, 0)),
                   pl.BlockSpec((1, 1, nc, 128, 128), lambda b, h: (b, h, 0, 0, 0))],
        out_shape=[full, full, full, jax.ShapeDtypeStruct((bsz, C_HEADS, nc, CHUNK, CHUNK), F32),
                   jax.ShapeDtypeStruct((bsz, C_HEADS, nc, 128, 128), F32)],
        scratch_shapes=[pltpu.VMEM((nc, CHUNK, 128), F32)] * 4 + [pltpu.VMEM((nc, CHUNK, CHUNK), F32),
                                                                   pltpu.VMEM((nc, 1, 128), F32)],
        compiler_params=_cp("parallel", "parallel"),
    )(proj, proj, proj, proj, gates, cw, cw, cw, a_log, dtb, ng.reshape(1, 128))


def gdn_bwd(proj, gates, cw, a_log, dtb, ng, o_pre, vnew, tmat, states, dout, bsz, *, name):
    t = proj.shape[0]
    s_len = t // bsz
    nc = s_len // CHUNK

    def body(q_ref, k_ref, v_ref, z_ref, gt_ref, cwq_ref, cwk_ref, cwv_ref, al_ref, dt_ref, ng_ref,
             o_ref, vn_ref, tm_ref, st_ref, do_ref,
             dq_ref, dk_ref, dv_ref, dz_ref, dgt_ref, dcq_ref, dck_ref, dcv_ref, dsm_ref,
             w_s, qg_s, kd_s, at_s, gl_s, dop_s, du_s, dw_s, dat_s, dqg_s, dkd_s, dgl_s):
        h = pl.program_id(1)
        qr, kr, vr = q_ref[...], k_ref[...], v_ref[...]
        r = _gdn_prep(qr, kr, vr, gt_ref[...], cwq_ref[...], cwk_ref[...], cwv_ref[...], al_ref[h], dt_ref[h], h)
        row, lane = r["row"], r["lane"]
        tm = tm_ref[0, 0]
        q3, k3, v3, beta3, eg, kb, dm = r["q3"], r["k3"], r["v3"], r["beta3"], r["eg"], r["kb"], r["dm"]
        u3 = _bmm(tm, v3 * beta3, "nn", exact=True)
        w3 = _bmm(tm, kb * eg, "nn", exact=True)
        w_s[...] = w3
        qg_s[...] = r["qg"]
        kd_s[...] = r["kdec"]
        at_s[...] = r["attn"]
        gl_s[...] = r["gl"]

        z = z_ref[...]
        sz = _sigmoid(z)
        o = o_ref[...]
        rms = lax.rsqrt(jnp.mean(o * o, axis=-1, keepdims=True) + NORM_EPS)
        on = o * rms
        dout_v = do_ref[...]
        ngv = ng_ref[...]
        dz_ref[...] = dout_v * on * ngv * (sz * (1.0 + z * (1.0 - sz)))
        dos = dout_v * (z * sz)
        dng = jnp.sum(dos * on, axis=0, keepdims=True)
        don = dos * ngv
        dop_s[...] = (rms * (don - on * jnp.mean(don * on, axis=-1, keepdims=True))).reshape(nc, CHUNK, 128)

        def chunk(i, dstate):
            n = nc - 1 - i
            st = pl.multiple_of(n * CHUNK, CHUNK)
            state = st_ref[0, 0, n]
            vn = vn_ref[pl.ds(st, CHUNK), :]
            do_n = dop_s[n]
            dvn = _mdot(at_s[n], do_n, TN) + _mdot(kd_s[n], dstate)
            du_s[n] = dvn
            dat_s[n] = _mdot(do_n, vn, NT)
            dqg_s[n] = _mdot(do_n, state, NT)
            dkd_s[n] = _mdot(vn, dstate, NT)
            dgl_s[n] = jnp.broadcast_to(jnp.sum(jnp.sum(state * dstate, axis=1, keepdims=True), axis=0, keepdims=True), (1, 128))
            dw_s[n] = -_mdot(dvn, state, NT)
            return dstate * gl_s[n] + _mdot(qg_s[n], do_n, TN) - _mdot(w_s[n], dvn, TN)

        lax.fori_loop(0, nc, chunk, jnp.zeros((128, 128), F32))

        du, dw, dqg, dkd = du_s[...], dw_s[...], dqg_s[...], dkd_s[...]
        dat = jnp.where(r["tril"], dat_s[...], 0.0)
        dvb = _bmm(tm, du, "tn", exact=True)
        dkbg = _bmm(tm, dw, "tn", exact=True)
        dl = -jnp.where(r["strict"], _bmm(dvb, u3, "nt") + _bmm(dkbg, w3, "nt"), 0.0)
        dml = dl * dm
        dn = dat * dm
        dkb = _bmm(dml, k3, "nn") + dkbg * eg
        dk3 = _bmm(dml, kb, "tn") + _bmm(dn, q3, "tn") + dkd * r["ekd"] + dkb * beta3
        dq3 = dqg * eg + _bmm(dn, k3, "nn")
        e = dl * r["lmat"] + dat * r["attn"]
        ones = jnp.ones((nc, CHUNK, 128), F32)
        colsum = lax.dot_general(e, ones, (_BDIMS["tn"], ((0,), (0,))), preferred_element_type=F32, precision=HI)
        dgc = jnp.sum(e, axis=-1, keepdims=True) - colsum
        dgc = dgc + eg * (jnp.sum(dqg * q3, axis=-1, keepdims=True) + jnp.sum(dkbg * kb, axis=-1, keepdims=True))
        skd = jnp.sum(dkd * r["kdec"], axis=-1, keepdims=True)
        dgcl = jnp.sum(skd, axis=1, keepdims=True) + dgl_s[...] * r["gl"]
        pos3 = lax.broadcasted_iota(jnp.int32, (nc, CHUNK, 128), 1)
        dgc = dgc - skd + jnp.where(pos3 == CHUNK - 1, dgcl, 0.0)
        dbeta = jnp.sum(dkb * k3, axis=-1, keepdims=True) + jnp.sum(dvb * v3, axis=-1, keepdims=True)
        dv3 = dvb * beta3

        dg = _seg_cumsum_rev(dgc.reshape(s_len, 128), row)
        beta = r["beta"]
        dbl = jnp.broadcast_to(dbeta, (nc, CHUNK, 128)).reshape(s_len, 128) * beta * (1.0 - beta)
        dai = dg * (-r["A"]) * _sigmoid(r["pre"])
        d_dtb = jnp.sum(dai, axis=0, keepdims=True)
        d_alog = jnp.sum(dg * (-r["sp"]), axis=0, keepdims=True) * r["A"]

        @pl.when(h == 0)
        def _():
            dgt_ref[...] = jnp.zeros_like(dgt_ref)
            dsm_ref[...] = jnp.zeros_like(dsm_ref)

        dgt_ref[...] += jnp.where(lane == h, dbl, 0.0) + jnp.where(lane == 8 + h, dai, 0.0)
        r16 = lax.broadcasted_iota(jnp.int32, (16, 128), 0)
        l16 = lax.broadcasted_iota(jnp.int32, (16, 128), 1)
        small = jnp.where((r16 == h) & (l16 == 0), d_alog, 0.0) + jnp.where((r16 == h) & (l16 == 1), d_dtb, 0.0)
        dsm_ref[0] += small + jnp.where(r16 == 8 + h, dng, 0.0)

        dqn = dq3.reshape(s_len, 128) * C_QSCALE
        dkn = dk3.reshape(s_len, 128)
        dqc = r["rq"] * (dqn - r["qn"] * jnp.sum(dqn * r["qn"], axis=-1, keepdims=True))
        dkc = r["rk"] * (dkn - r["kn"] * jnp.sum(dkn * r["kn"], axis=-1, keepdims=True))
        dvc = dv3.reshape(s_len, 128)
        for nm, x, w_ref, dxc, dx_ref, dc_ref in (("q", qr, cwq_ref, dqc, dq_ref, dcq_ref), ("k", kr, cwk_ref, dkc, dk_ref, dck_ref),
                                                 ("v", vr, cwv_ref, dvc, dv_ref, dcv_ref)):
            c, sg = r["c" + nm], r["s" + nm]
            dc = dxc * (sg * (1.0 + c * (1.0 - sg)))
            dx, dwc = _conv_bwd(x, w_ref[...], dc, row)
            dx_ref[...] = dx
            dc_ref[0] = dwc

    blk = pl.BlockSpec((s_len, 128), lambda b, h: (b, h))
    full = jax.ShapeDtypeStruct((t, C_WIDTH), F32)
    cwo = pl.BlockSpec((1, 4, 128), lambda b, h: (b, 0, h))
    cws = jax.ShapeDtypeStruct((bsz, 4, C_WIDTH), F32)
    c128 = pltpu.VMEM((nc, CHUNK, 128), F32)
    outs = _pcall(
        body, name=name, grid=(bsz, C_HEADS),
        in_specs=_gdn_specs(s_len) + [blk, blk, pl.BlockSpec((1, 1, nc, CHUNK, CHUNK), lambda b, h: (b, h, 0, 0, 0)),
                                      pl.BlockSpec((1, 1, nc, 128, 128), lambda b, h: (b, h, 0, 0, 0)), blk],
        out_specs=[blk, blk, blk, blk, pl.BlockSpec((s_len, 128), lambda b, h: (b, 0)), cwo, cwo, cwo,
                   pl.BlockSpec((1, 16, 128), lambda b, h: (b, 0, 0))],
        out_shape=[full, full, full, full, jax.ShapeDtypeStruct((t, 128), F32), cws, cws, cws,
                   jax.ShapeDtypeStruct((bsz, 16, 128), F32)],
        scratch_shapes=[c128, c128, c128, pltpu.VMEM((nc, CHUNK, CHUNK), F32), pltpu.VMEM((nc, 1, 128), F32), c128,
                        c128, c128, pltpu.VMEM((nc, CHUNK, CHUNK), F32), c128, c128, pltpu.VMEM((nc, 1, 128), F32)],
        compiler_params=_cp("parallel", "arbitrary"),
    )(proj, proj, proj, proj, gates, cw, cw, cw, a_log, dtb, ng.reshape(1, 128), o_pre, vnew, tmat, states, dout)
    dq, dk, dv, dz, dgates, dcq, dck, dcv, dsm = outs
    return dq, dk, dv, dz, dgates, jnp.concatenate([dcq, dck, dcv], axis=-1), dsm


def gdc_pre_fwd(proj, cw, bsz, *, name):
    t = proj.shape[0]
    s_len = t // bsz

    def body(x_ref, w_ref, y_ref):
        row = lax.broadcasted_iota(jnp.int32, (s_len, 128), 0)
        c = _conv_fwd(x_ref[...], w_ref[...], row)
        xc = c * _sigmoid(c)
        rn = lax.rsqrt(jnp.sum(xc * xc, axis=-1, keepdims=True) + NORM_EPS)
        y_ref[...] = jnp.where(pl.program_id(1) < 2 * C_HEADS, xc * rn, xc)

    blk = pl.BlockSpec((s_len, 128), lambda b, j: (b, j))
    return _pcall(
        body, name=name, grid=(bsz, 3 * C_HEADS), in_specs=[blk, pl.BlockSpec((4, 128), lambda b, j: (0, j))],
        out_specs=blk, out_shape=jax.ShapeDtypeStruct((t, 3 * C_WIDTH), F32), compiler_params=_cp("parallel", "parallel"),
    )(proj, cw)


def gdc_pre_bwd(proj, cw, dy, bsz, *, name):
    t = proj.shape[0]
    s_len = t // bsz

    def body(x_ref, w_ref, dy_ref, dx_ref, dw_ref):
        row = lax.broadcasted_iota(jnp.int32, (s_len, 128), 0)
        x = x_ref[...]
        c = _conv_fwd(x, w_ref[...], row)
        sg = _sigmoid(c)
        xc = c * sg
        rn = lax.rsqrt(jnp.sum(xc * xc, axis=-1, keepdims=True) + NORM_EPS)
        dyv = dy_ref[...]
        xn = xc * rn
        dxc = jnp.where(pl.program_id(1) < 2 * C_HEADS, rn * (dyv - xn * jnp.sum(dyv * xn, axis=-1, keepdims=True)), dyv)
        dc = dxc * (sg * (1.0 + c * (1.0 - sg)))
        dx, dw = _conv_bwd(x, w_ref[...], dc, row)
        dx_ref[...] = dx
        dw_ref[0] = dw

    blk = pl.BlockSpec((s_len, 128), lambda b, j: (b, j))
    return _pcall(
        body, name=name, grid=(bsz, 3 * C_HEADS), in_specs=[blk, pl.BlockSpec((4, 128), lambda b, j: (0, j)), blk],
        out_specs=[blk, pl.BlockSpec((1, 4, 128), lambda b, j: (b, 0, j))],
        out_shape=[jax.ShapeDtypeStruct((t, 3 * C_WIDTH), F32), jax.ShapeDtypeStruct((bsz, 4, 3 * C_WIDTH), F32)],
        compiler_params=_cp("parallel", "parallel"),
    )(proj, cw, dy)


GDC_GROUP = 8


def _gdc_local(qn, kn, vc, gates, a_log, dtb, h):
    rows = qn.shape[0]
    nc = rows // CHUNK
    row = lax.broadcasted_iota(jnp.int32, (rows, 128), 0)
    lane = lax.broadcasted_iota(jnp.int32, (rows, 128), 1)
    r = {"row": row, "lane": lane}
    r["beta"] = _sigmoid(_col(gates, h, lane))
    r["A"] = jnp.exp(a_log)
    r["pre"] = _col(gates, 8 + h, lane) + dtb
    r["sp"] = _softplus(r["pre"])
    gc = _seg_cumsum(-r["A"] * r["sp"], row)
    sh = (nc, CHUNK, 128)
    q3 = (qn * C_QSCALE).reshape(sh)
    k3 = kn.reshape(sh)
    v3 = vc.reshape(sh)
    beta3 = r["beta"].reshape(sh)
    gc3 = gc.reshape(sh)
    gcl3 = gc3[:, CHUNK - 1:CHUNK, :]
    eg = jnp.exp(gc3)
    ekd = jnp.exp(gcl3 - gc3)
    col64 = gc3[:, :, :CHUNK]
    row64 = jnp.swapaxes(gc3, 1, 2)[:, :CHUNK, :]
    ii = lax.broadcasted_iota(jnp.int32, (nc, CHUNK, CHUNK), 1)
    jj = lax.broadcasted_iota(jnp.int32, (nc, CHUNK, CHUNK), 2)
    tril = ii >= jj
    strict = ii > jj
    dm = jnp.where(tril, jnp.exp(jnp.where(tril, col64 - row64, 0.0)), 0.0)
    kb = k3 * beta3
    lmat = jnp.where(strict, _bmm(kb, k3, "nt") * dm, 0.0)
    attn = _bmm(q3, k3, "nt") * dm
    r.update(q3=q3, k3=k3, v3=v3, beta3=beta3, eg=eg, ekd=ekd, gl=jnp.exp(gcl3), dm=dm, kb=kb, lmat=lmat,
             attn=attn, strict=strict, tril=tril, qg=q3 * eg, kdec=k3 * ekd)
    return r


def _gdc_specs(s_len):
    act = lambda off: pl.BlockSpec((s_len, 128), lambda b, h: (b, off + h))
    smem = pl.BlockSpec(memory_space=pltpu.SMEM)
    return [act(0), act(8), act(16), act(24), pl.BlockSpec((s_len, 128), lambda b, h: (b, 0)), smem, smem,
            pl.BlockSpec((1, 128), lambda b, h: (0, 0))]


def gdc_fwd(qkv, proj, gates, a_log, dtb, ng, bsz, *, name):
    t = proj.shape[0]
    s_len = t // bsz
    nc = s_len // CHUNK
    grp = min(GDC_GROUP, nc)
    gr = grp * CHUNK

    def body(q_ref, k_ref, v_ref, z_ref, gt_ref, al_ref, dt_ref, ng_ref,
             out_ref, o_ref, vn_ref, tm_ref, st_ref, u_s, w_s, qg_s, kd_s, at_s, gl_s):
        h = pl.program_id(1)

        def local(gi, carry):
            rs = pl.ds(pl.multiple_of(gi * gr, gr), gr)
            cs = pl.ds(gi * grp, grp)
            r = _gdc_local(q_ref[rs, :], k_ref[rs, :], v_ref[rs, :], gt_ref[rs, :], al_ref[h], dt_ref[h], h)
            tm = _neumann_inverse(r["lmat"])
            tm_ref[0, 0, cs] = tm
            u_s[cs] = _bmm(tm, r["v3"] * r["beta3"], "nn", exact=True)
            w_s[cs] = _bmm(tm, r["kb"] * r["eg"], "nn", exact=True)
            qg_s[cs] = r["qg"]
            kd_s[cs] = r["kdec"]
            at_s[cs] = r["attn"]
            gl_s[cs] = r["gl"]
            return carry

        lax.fori_loop(0, nc // grp, local, 0)

        def chunk(n, state):
            st = pl.multiple_of(n * CHUNK, CHUNK)
            st_ref[0, 0, n] = state
            v_new = u_s[n] - _mdot(w_s[n], state)
            o_ref[pl.ds(st, CHUNK), :] = _mdot(qg_s[n], state) + _mdot(at_s[n], v_new)
            vn_ref[pl.ds(st, CHUNK), :] = v_new
            return state * gl_s[n] + _mdot(kd_s[n], v_new, TN)

        lax.fori_loop(0, nc, chunk, jnp.zeros((128, 128), F32))
        o = o_ref[...]
        rms = lax.rsqrt(jnp.mean(o * o, axis=-1, keepdims=True) + NORM_EPS)
        z = z_ref[...]
        out_ref[...] = o * rms * ng_ref[...] * (z * _sigmoid(z))

    blk = pl.BlockSpec((s_len, 128), lambda b, h: (b, h))
    full = jax.ShapeDtypeStruct((t, C_WIDTH), F32)
    return _pcall(
        body, name=name, grid=(bsz, C_HEADS), in_specs=_gdc_specs(s_len),
        out_specs=[blk, blk, blk, pl.BlockSpec((1, 1, nc, CHUNK, CHUNK), lambda b, h: (b, h, 0, 0, 0)),
                   pl.BlockSpec((1, 1, nc, 128, 128), lambda b, h: (b, h, 0, 0, 0))],
        out_shape=[full, full, full, jax.ShapeDtypeStruct((bsz, C_HEADS, nc, CHUNK, CHUNK), F32),
                   jax.ShapeDtypeStruct((bsz, C_HEADS, nc, 128, 128), F32)],
        scratch_shapes=[pltpu.VMEM((nc, CHUNK, 128), F32)] * 4 + [pltpu.VMEM((nc, CHUNK, CHUNK), F32),
                                                                   pltpu.VMEM((nc, 1, 128), F32)],
        compiler_params=_cp("parallel", "parallel"),
    )(qkv, qkv, qkv, proj, gates, a_log, dtb, ng.reshape(1, 128))


def gdc_bwd(qkv, proj, gates, a_log, dtb, ng, o_pre, vnew, tmat, states, dout, bsz, *, name):
    t = proj.shape[0]
    s_len = t // bsz
    nc = s_len // CHUNK
    grp = min(GDC_GROUP, nc)
    gr = grp * CHUNK

    def body(q_ref, k_ref, v_ref, z_ref, gt_ref, al_ref, dt_ref, ng_ref, o_ref, vn_ref, tm_ref, st_ref, do_ref,
             dq_ref, dk_ref, dv_ref, dz_ref, dgt_ref, dsm_ref,
             w_s, qg_s, kd_s, at_s, gl_s, dop_s, du_s, dw_s, dat_s, dqg_s, dkd_s, dgl_s):
        h = pl.program_id(1)
        a_log_h, dtb_h = al_ref[h], dt_ref[h]

        z = z_ref[...]
        sz = _sigmoid(z)
        o = o_ref[...]
        rms = lax.rsqrt(jnp.mean(o * o, axis=-1, keepdims=True) + NORM_EPS)
        on = o * rms
        dout_v = do_ref[...]
        ngv = ng_ref[...]
        dz_ref[...] = dout_v * on * ngv * (sz * (1.0 + z * (1.0 - sz)))
        dos = dout_v * (z * sz)
        dng = jnp.sum(dos * on, axis=0, keepdims=True)
        don = dos * ngv
        dop_s[...] = (rms * (don - on * jnp.mean(don * on, axis=-1, keepdims=True))).reshape(nc, CHUNK, 128)

        def local(gi, carry):
            rs = pl.ds(pl.multiple_of(gi * gr, gr), gr)
            cs = pl.ds(gi * grp, grp)
            r = _gdc_local(q_ref[rs, :], k_ref[rs, :], v_ref[rs, :], gt_ref[rs, :], a_log_h, dtb_h, h)
            w_s[cs] = _bmm(tm_ref[0, 0, cs], r["kb"] * r["eg"], "nn", exact=True)
            qg_s[cs] = r["qg"]
            kd_s[cs] = r["kdec"]
            at_s[cs] = r["attn"]
            gl_s[cs] = r["gl"]
            return carry

        lax.fori_loop(0, nc // grp, local, 0)

        def chunk(i, dstate):
            n = nc - 1 - i
            st = pl.multiple_of(n * CHUNK, CHUNK)
            state = st_ref[0, 0, n]
            vn = vn_ref[pl.ds(st, CHUNK), :]
            do_n = dop_s[n]
            dvn = _mdot(at_s[n], do_n, TN) + _mdot(kd_s[n], dstate)
            du_s[n] = dvn
            dat_s[n] = _mdot(do_n, vn, NT)
            dqg_s[n] = _mdot(do_n, state, NT)
            dkd_s[n] = _mdot(vn, dstate, NT)
            dgl_s[n] = jnp.broadcast_to(jnp.sum(jnp.sum(state * dstate, axis=1, keepdims=True), axis=0, keepdims=True), (1, 128))
            dw_s[n] = -_mdot(dvn, state, NT)
            return dstate * gl_s[n] + _mdot(qg_s[n], do_n, TN) - _mdot(w_s[n], dvn, TN)

        lax.fori_loop(0, nc, chunk, jnp.zeros((128, 128), F32))

        @pl.when(h == 0)
        def _():
            dgt_ref[...] = jnp.zeros_like(dgt_ref)
            dsm_ref[...] = jnp.zeros_like(dsm_ref)

        def local_bwd(gi, carry):
            d_alog, d_dtb = carry
            rs = pl.ds(pl.multiple_of(gi * gr, gr), gr)
            cs = pl.ds(gi * grp, grp)
            r = _gdc_local(q_ref[rs, :], k_ref[rs, :], v_ref[rs, :], gt_ref[rs, :], a_log_h, dtb_h, h)
            row, lane = r["row"], r["lane"]
            q3, k3, v3, beta3, eg, kb, dm = r["q3"], r["k3"], r["v3"], r["beta3"], r["eg"], r["kb"], r["dm"]
            tm = tm_ref[0, 0, cs]
            u3 = _bmm(tm, v3 * beta3, "nn", exact=True)
            w3 = w_s[cs]
            du, dw, dqg, dkd = du_s[cs], dw_s[cs], dqg_s[cs], dkd_s[cs]
            dat = jnp.where(r["tril"], dat_s[cs], 0.0)
            dvb = _bmm(tm, du, "tn", exact=True)
            dkbg = _bmm(tm, dw, "tn", exact=True)
            dl = -jnp.where(r["strict"], _bmm(dvb, u3, "nt") + _bmm(dkbg, w3, "nt"), 0.0)
            dml = dl * dm
            dn = dat * dm
            dkb = _bmm(dml, k3, "nn") + dkbg * eg
            dk3 = _bmm(dml, kb, "tn") + _bmm(dn, q3, "tn") + dkd * r["ekd"] + dkb * beta3
            dq3 = dqg * eg + _bmm(dn, k3, "nn")
            e = dl * r["lmat"] + dat * r["attn"]
            ones = jnp.ones((grp, CHUNK, 128), F32)
            colsum = lax.dot_general(e, ones, (_BDIMS["tn"], ((0,), (0,))), preferred_element_type=F32, precision=HI)
            dgc = jnp.sum(e, axis=-1, keepdims=True) - colsum
            dgc = dgc + eg * (jnp.sum(dqg * q3, axis=-1, keepdims=True) + jnp.sum(dkbg * kb, axis=-1, keepdims=True))
            skd = jnp.sum(dkd * r["kdec"], axis=-1, keepdims=True)
            dgcl = jnp.sum(skd, axis=1, keepdims=True) + dgl_s[cs] * r["gl"]
            pos3 = lax.broadcasted_iota(jnp.int32, (grp, CHUNK, 128), 1)
            dgc = dgc - skd + jnp.where(pos3 == CHUNK - 1, dgcl, 0.0)
            dbeta = jnp.sum(dkb * k3, axis=-1, keepdims=True) + jnp.sum(dvb * v3, axis=-1, keepdims=True)
            dg = _seg_cumsum_rev(dgc.reshape(gr, 128), row)
            beta = r["beta"]
            dbl = jnp.broadcast_to(dbeta, (grp, CHUNK, 128)).reshape(gr, 128) * beta * (1.0 - beta)
            dai = dg * (-r["A"]) * _sigmoid(r["pre"])
            dgt_ref[rs, :] += jnp.where(lane == h, dbl, 0.0) + jnp.where(lane == 8 + h, dai, 0.0)
            dq_ref[rs, :] = dq3.reshape(gr, 128) * C_QSCALE
            dk_ref[rs, :] = dk3.reshape(gr, 128)
            dv_ref[rs, :] = (dvb * beta3).reshape(gr, 128)
            return (d_alog + jnp.sum(dg * (-r["sp"]), axis=0, keepdims=True) * r["A"],
                    d_dtb + jnp.sum(dai, axis=0, keepdims=True))

        zero = jnp.zeros((1, 128), F32)
        d_alog, d_dtb = lax.fori_loop(0, nc // grp, local_bwd, (zero, zero))
        r16 = lax.broadcasted_iota(jnp.int32, (16, 128), 0)
        l16 = lax.broadcasted_iota(jnp.int32, (16, 128), 1)
        small = jnp.where((r16 == h) & (l16 == 0), d_alog, 0.0) + jnp.where((r16 == h) & (l16 == 1), d_dtb, 0.0)
        dsm_ref[0] += small + jnp.where(r16 == 8 + h, dng, 0.0)

    blk = pl.BlockSpec((s_len, 128), lambda b, h: (b, h))
    blk3 = lambda off: pl.BlockSpec((s_len, 128), lambda b, h: (b, off + h))
    full = jax.ShapeDtypeStruct((t, C_WIDTH), F32)
    c128 = pltpu.VMEM((nc, CHUNK, 128), F32)
    dq, dk, dv, dz, dgates, dsm = _pcall(
        body, name=name, grid=(bsz, C_HEADS),
        in_specs=_gdc_specs(s_len) + [blk, blk, pl.BlockSpec((1, 1, nc, CHUNK, CHUNK), lambda b, h: (b, h, 0, 0, 0)),
                                      pl.BlockSpec((1, 1, nc, 128, 128), lambda b, h: (b, h, 0, 0, 0)), blk],
        out_specs=[blk, blk, blk, blk, pl.BlockSpec((s_len, 128), lambda b, h: (b, 0)),
                   pl.BlockSpec((1, 16, 128), lambda b, h: (b, 0, 0))],
        out_shape=[full, full, full, full, jax.ShapeDtypeStruct((t, 128), F32), jax.ShapeDtypeStruct((bsz, 16, 128), F32)],
        scratch_shapes=[c128, c128, c128, pltpu.VMEM((nc, CHUNK, CHUNK), F32), pltpu.VMEM((nc, 1, 128), F32), c128,
                        c128, c128, pltpu.VMEM((nc, CHUNK, CHUNK), F32), c128, c128, pltpu.VMEM((nc, 1, 128), F32)],
        compiler_params=_cp("parallel", "arbitrary"),
    )(qkv, qkv, qkv, proj, gates, a_log, dtb, ng.reshape(1, 128), o_pre, vnew, tmat, states, dout)
    return jnp.concatenate([dq, dk, dv], axis=-1), dz, dgates, dsm


MESH_ID = pl.DeviceIdType.MESH
_FLIPS = [(0, 0, 1), (1, 0, 0), (0, 1, 0), (1, 1, 0), (1, 0, 1), (0, 1, 1), (1, 1, 1)]


def _me():
    return lax.axis_index("x"), lax.axis_index("y"), lax.axis_index("c")


def _flip(coord, d):
    return 1 - coord if d else coord


def all_gather(shard, *, name):
    def body(x_ref, o_ref, send_sems, recv_sems, local_sem):
        x, y, c = _me()
        mine = 4 * x + 2 * y + c
        own = pltpu.make_async_copy(x_ref, o_ref.at[mine], local_sem)
        own.start()
        copies = []
        for k, (dx, dy, dc) in enumerate(_FLIPS):
            cp = pltpu.make_async_remote_copy(
                src_ref=x_ref, dst_ref=o_ref.at[mine], send_sem=send_sems.at[k], recv_sem=recv_sems.at[k],
                device_id=(_flip(x, dx), _flip(y, dy), _flip(c, dc)), device_id_type=MESH_ID)
            cp.start()
            copies.append(cp)
        for cp in copies:
            cp.wait()
        own.wait()

    hbm = pl.BlockSpec(memory_space=pl.ANY)
    return _pcall(
        body, name=name, in_specs=[hbm], out_specs=hbm,
        out_shape=jax.ShapeDtypeStruct((N_DEV,) + shard.shape, shard.dtype),
        scratch_shapes=[pltpu.SemaphoreType.DMA((7,)), pltpu.SemaphoreType.DMA((7,)), pltpu.SemaphoreType.DMA(())],
    )(shard)


def all_to_all(parts, *, name):
    def body(x_ref, o_ref, send_sems, recv_sems, local_sem):
        x, y, c = _me()
        mine = 4 * x + 2 * y + c
        own = pltpu.make_async_copy(x_ref.at[mine], o_ref.at[mine], local_sem)
        own.start()
        copies = []
        for k, (dx, dy, dc) in enumerate(_FLIPS):
            px, py, pc = _flip(x, dx), _flip(y, dy), _flip(c, dc)
            cp = pltpu.make_async_remote_copy(
                src_ref=x_ref.at[4 * px + 2 * py + pc], dst_ref=o_ref.at[mine], send_sem=send_sems.at[k],
                recv_sem=recv_sems.at[k], device_id=(px, py, pc), device_id_type=MESH_ID)
            cp.start()
            copies.append(cp)
        for cp in copies:
            cp.wait()
        own.wait()

    hbm = pl.BlockSpec(memory_space=pl.ANY)
    return _pcall(
        body, name=name, in_specs=[hbm], out_specs=hbm, out_shape=jax.ShapeDtypeStruct(parts.shape, parts.dtype),
        scratch_shapes=[pltpu.SemaphoreType.DMA((7,)), pltpu.SemaphoreType.DMA((7,)), pltpu.SemaphoreType.DMA(())],
    )(parts)


def adamw_sum(parts, w, m, v, *, name, tr=256):
    r, cdim = w.shape
    tr = _tile8(r, tr)

    def body(p_ref, w_ref, m_ref, v_ref, g_ref, d_ref, mo_ref, vo_ref):
        g = p_ref[0].astype(F32)
        for j in range(1, N_DEV):
            g = g + p_ref[j].astype(F32)
        g_ref[...] = g
        mn = ADAM_B1 * m_ref[...] + (1.0 - ADAM_B1) * g
        vn = ADAM_B2 * v_ref[...] + (1.0 - ADAM_B2) * (g * g)
        mo_ref[...] = mn
        vo_ref[...] = vn
        m_hat = mn / (1.0 - ADAM_B1 ** ADAM_STEP)
        v_hat = vn / (1.0 - ADAM_B2 ** ADAM_STEP)
        d_ref[...] = -ADAM_LR * (m_hat / (jnp.sqrt(v_hat) + ADAM_EPS) + ADAM_WD * w_ref[...])

    blk = pl.BlockSpec((tr, cdim), lambda i: (i, 0))
    shp = jax.ShapeDtypeStruct((r, cdim), F32)
    return _pcall(
        body, name=name, grid=(r // tr,), in_specs=[pl.BlockSpec((N_DEV, tr, cdim), lambda i: (0, i, 0)), blk, blk, blk],
        out_specs=[blk, blk, blk, blk], out_shape=[shp, shp, shp, shp], compiler_params=_cp("parallel"),
    )(parts, w, m, v)


def _tile8(n, pref):
    for c in range(min(pref, n) - min(pref, n) % 16, 0, -16):
        if n % c == 0:
            return c
    return n


BIG = [("ffn1_wg", 2), ("ffn1_wu", 2), ("ffn1_wd", 1), ("ffn2_wg", 2), ("ffn2_wu", 2), ("ffn2_wd", 1), ("ple_wg", 1),
       ("ple_wp", 2), ("ab_w_in", 2), ("ab_w_out", 1), ("c_w_in", 2), ("c_w_out", 1)]
SMALL = [("ln_g", 2), ("ln_b", 2), ("b_conv_w", 2), ("c_conv_w", 2)]
REPL = ["ple_bg", "a_sinks", "b_conv_b", "b_wa", "b_ba", "b_wx", "b_bx", "b_lam", "c_a_log", "c_dt_bias", "c_norm_g"]
WEIGHTS = ["ffn1_wg", "ffn1_wu", "ffn1_wd", "ffn2_wg", "ffn2_wu", "ffn2_wd", "ln_g", "ln_b", "ple_wg", "ple_bg", "ple_wp",
           "ab_w_in", "a_sinks", "b_conv_w", "b_conv_b", "b_wa", "b_ba", "b_wx", "b_bx", "b_lam", "ab_w_out", "c_w_in",
           "c_conv_w", "c_a_log", "c_dt_bias", "c_norm_g", "c_w_out"]
PACK_COLS = 1024
PACK_ALIGN = 16 * PACK_COLS


def _as_bf16_bits(a):
    return lax.bitcast_convert_type(a, jnp.bfloat16).reshape(a.shape[:-1] + (2 * a.shape[-1],))


def _from_bf16_bits(a):
    return lax.bitcast_convert_type(a.reshape(a.shape[:-1] + (a.shape[-1] // 2, 2)), F32)


def _pad_rows(flat, align=PACK_ALIGN):
    n = flat.shape[-1]
    total = -(-n // align) * align
    flat = jnp.pad(flat, [(0, 0)] * (flat.ndim - 1) + [(0, total - n)])
    return flat.reshape(flat.shape[:-1] + (total // PACK_COLS, PACK_COLS))


def _join(blocks, axis):
    moved = jnp.moveaxis(blocks, 0, axis)
    shp = list(moved.shape)
    return moved.reshape(shp[:axis] + [shp[axis] * shp[axis + 1]] + shp[axis + 2:])


def _split(full, axis):
    shp = list(full.shape)
    return jnp.moveaxis(full.reshape(shp[:axis] + [N_DEV, shp[axis] // N_DEV] + shp[axis + 1:]), axis, 0)


def _dense_blocks(w):
    z = jnp.zeros((4, 2, 64, 2, 64), w.dtype)
    w4 = w.reshape(4, 2, 64, 64)
    z = z.at[:, 0, :, 0, :].set(w4[:, 0]).at[:, 1, :, 1, :].set(w4[:, 1])
    return z.reshape(4, 128, 128)


def _diag_blocks(d):
    d5 = d.reshape(4, 2, 64, 2, 64)
    return jnp.stack([d5[:, 0, :, 0, :], d5[:, 1, :, 1, :]], axis=1).reshape(8, 64, 64)


def kernel(x, p, ffn1_wg, ffn1_wu, ffn1_wd, ffn2_wg, ffn2_wu, ffn2_wd, ln_g, ln_b, ple_wg, ple_bg, ple_wp, ab_w_in, a_sinks, b_conv_w, b_conv_b, b_wa, b_ba, b_wx, b_bx, b_lam, ab_w_out, c_w_in, c_conv_w, c_a_log, c_dt_bias, c_norm_g, c_w_out, loss_target, m_ffn1_wg, m_ffn1_wu, m_ffn1_wd, m_ffn2_wg, m_ffn2_wu, m_ffn2_wd, m_ln_g, m_ln_b, m_ple_wg, m_ple_bg, m_ple_wp, m_ab_w_in, m_a_sinks, m_b_conv_w, m_b_conv_b, m_b_wa, m_b_ba, m_b_wx, m_b_bx, m_b_lam, m_ab_w_out, m_c_w_in, m_c_conv_w, m_c_a_log, m_c_dt_bias, m_c_norm_g, m_c_w_out, v_ffn1_wg, v_ffn1_wu, v_ffn1_wd, v_ffn2_wg, v_ffn2_wu, v_ffn2_wd, v_ln_g, v_ln_b, v_ple_wg, v_ple_bg, v_ple_wp, v_ab_w_in, v_a_sinks, v_b_conv_w, v_b_conv_b, v_b_wa, v_b_ba, v_b_wx, v_b_bx, v_b_lam, v_ab_w_out, v_c_w_in, v_c_conv_w, v_c_a_log, v_c_dt_bias, v_c_norm_g, v_c_w_out):
    a = dict(locals())
    return _step2(a)


def join_cols(x, *, name, outs=None, tk=256):
    _, kk, n = x.shape
    tk = _tile8(kk, tk)
    outs = outs or [(0, N_DEV * n, N_DEV * n)]

    def body(x_ref, *o_refs):
        full = jnp.concatenate([x_ref[k] for k in range(N_DEV)], axis=-1)
        for (lo, hi, wd), o_ref in zip(outs, o_refs):
            piece = full[:, lo:hi]
            if wd > hi - lo:
                piece = jnp.concatenate([piece, jnp.zeros((tk, wd - (hi - lo)), piece.dtype)], axis=-1)
            o_ref[...] = piece

    res = _pcall(
        body, name=name, grid=(kk // tk,), in_specs=[pl.BlockSpec((N_DEV, tk, n), lambda i: (0, i, 0))],
        out_specs=[pl.BlockSpec((tk, wd), lambda i: (i, 0)) for _, _, wd in outs],
        out_shape=[jax.ShapeDtypeStruct((kk, wd), x.dtype) for _, _, wd in outs], compiler_params=_cp("parallel"),
    )(x)
    return res if len(outs) > 1 else res[0]


def split_cols(pieces, n, *, name, tk=256):
    kk = pieces[0][0].shape[0]
    tk = _tile8(kk, tk)

    def body(*refs):
        o_ref = refs[-1]
        vals = [r[...][:, :used] for r, (_, used) in zip(refs[:-1], pieces)]
        full = vals[0] if len(vals) == 1 else jnp.concatenate(vals, axis=-1)
        for k in range(N_DEV):
            o_ref[k] = full[:, k * n:(k + 1) * n].astype(MM)

    return _pcall(
        body, name=name, grid=(kk // tk,),
        in_specs=[pl.BlockSpec((tk, arr.shape[1]), lambda i: (i, 0)) for arr, _ in pieces],
        out_specs=pl.BlockSpec((N_DEV, tk, n), lambda i: (0, i, 0)),
        out_shape=jax.ShapeDtypeStruct((N_DEV, kk, n), MM), compiler_params=_cp("parallel"),
    )(*[arr for arr, _ in pieces])


def gather_multi(shards, *, name):
    ng = len(shards)

    def body(*refs):
        x_refs, o_refs = refs[:ng], refs[ng:2 * ng]
        send_sems, recv_sems, local_sems = refs[2 * ng:]
        x, y, c = _me()
        sibling = (x, y, 1 - c)
        chips = [(1 - x, y), (x, 1 - y), (1 - x, 1 - y)]

        def slot(px, py, pc):
            return 4 * px + 2 * py + pc

        def copy(gi, k, block, to, src=None):
            dst = o_refs[gi].at[slot(*block)]
            return pltpu.make_async_remote_copy(
                src_ref=dst if src is None else src, dst_ref=dst, send_sem=send_sems.at[7 * gi + k],
                recv_sem=recv_sems.at[7 * gi + k], device_id=to, device_id_type=MESH_ID)

        own = [pltpu.make_async_copy(x_refs[gi], o_refs[gi].at[slot(x, y, c)], local_sems.at[gi]) for gi in range(ng)]
        for cp in own:
            cp.start()
        first = []
        for gi in range(ng):
            first.append(copy(gi, 0, (x, y, c), sibling, src=x_refs[gi]))
            first += [copy(gi, 1 + j, (x, y, c), (*chip, c), src=x_refs[gi]) for j, chip in enumerate(chips)]
        for cp in first:
            cp.start()
        passed = []
        for j, chip in enumerate(chips):
            for gi in range(ng):
                copy(gi, 1 + j, (*chip, c), (x, y, c)).wait_recv()
                fwd = copy(gi, 4 + j, (*chip, c), sibling)
                fwd.start()
                passed.append(fwd)
        for gi in range(ng):
            copy(gi, 0, sibling, (x, y, c)).wait_recv()
            for j, chip in enumerate(chips):
                copy(gi, 4 + j, (*chip, 1 - c), (x, y, c)).wait_recv()
        for cp in first + passed:
            cp.wait_send()
        for cp in own:
            cp.wait()

    hbm = pl.BlockSpec(memory_space=pl.ANY)
    return _pcall(
        body, name=name, in_specs=[hbm] * ng, out_specs=[hbm] * ng,
        out_shape=[jax.ShapeDtypeStruct((N_DEV,) + s.shape, s.dtype) for s in shards],
        scratch_shapes=[pltpu.SemaphoreType.DMA((7 * ng,)), pltpu.SemaphoreType.DMA((7 * ng,)),
                        pltpu.SemaphoreType.DMA((ng,))],
    )(*shards)


def exchange_multi(parts, *, name):
    ng = len(parts)

    def body(*refs):
        x_refs, o_refs = refs[:ng], refs[ng:2 * ng]
        send_sems, recv_sems, local_sems = refs[2 * ng:]
        x, y, c = _me()
        mine = 4 * x + 2 * y + c
        own = [pltpu.make_async_copy(x_refs[gi].at[mine], o_refs[gi].at[mine], local_sems.at[gi]) for gi in range(ng)]
        for cp in own:
            cp.start()
        copies = []
        for k, (dx, dy, dc) in enumerate(_FLIPS):
            px, py, pc = _flip(x, dx), _flip(y, dy), _flip(c, dc)
            for gi in range(ng):
                cp = pltpu.make_async_remote_copy(
                    src_ref=x_refs[gi].at[4 * px + 2 * py + pc], dst_ref=o_refs[gi].at[mine],
                    send_sem=send_sems.at[7 * gi + k], recv_sem=recv_sems.at[7 * gi + k], device_id=(px, py, pc),
                    device_id_type=MESH_ID)
                cp.start()
                copies.append(cp)
        for cp in copies:
            cp.wait()
        for cp in own:
            cp.wait()

    hbm = pl.BlockSpec(memory_space=pl.ANY)
    return _pcall(
        body, name=name, in_specs=[hbm] * ng, out_specs=[hbm] * ng,
        out_shape=[jax.ShapeDtypeStruct(s.shape, s.dtype) for s in parts],
        scratch_shapes=[pltpu.SemaphoreType.DMA((7 * ng,)), pltpu.SemaphoreType.DMA((7 * ng,)),
                        pltpu.SemaphoreType.DMA((ng,))],
    )(*parts)


def adamw_rows(parts, row0, w, m, v, *, name, tr=256):
    r, cdim = w.shape
    tr = _tile8(math.gcd(r, row0) if row0 else r, tr)
    blk0 = row0 // tr

    def body(p_ref, w_ref, m_ref, v_ref, g_ref, d_ref, mo_ref, vo_ref):
        g = p_ref[0].astype(F32)
        for j in range(1, N_DEV):
            g = g + p_ref[j].astype(F32)
        g_ref[...] = g
        mn = ADAM_B1 * m_ref[...] + (1.0 - ADAM_B1) * g
        vn = ADAM_B2 * v_ref[...] + (1.0 - ADAM_B2) * (g * g)
        mo_ref[...] = mn
        vo_ref[...] = vn
        m_hat = mn / (1.0 - ADAM_B1 ** ADAM_STEP)
        v_hat = vn / (1.0 - ADAM_B2 ** ADAM_STEP)
        d_ref[...] = -ADAM_LR * (m_hat / (jnp.sqrt(v_hat) + ADAM_EPS) + ADAM_WD * w_ref[...])

    blk = pl.BlockSpec((tr, cdim), lambda i: (i, 0))
    shp = jax.ShapeDtypeStruct((r, cdim), F32)
    return _pcall(
        body, name=name, grid=(r // tr,),
        in_specs=[pl.BlockSpec((N_DEV, tr, cdim), lambda i: (0, blk0 + i, 0)), blk, blk, blk],
        out_specs=[blk, blk, blk, blk], out_shape=[shp, shp, shp, shp], compiler_params=_cp("parallel"),
    )(parts, w, m, v)


GROUP_A = ["ffn1_wg", "ffn1_wu", "ffn2_wg", "ffn2_wu"]
GROUP_B = ["ffn1_wd", "ffn2_wd", "ple_wg", "ab_w_out", "c_w_out"]
SMALL_NAMES = ["ln_g", "ln_b", "b_conv_w", "c_conv_w"]
LANES = 128
FFN_BWD_TM = 1024
PLE_WP_ROWS = DEPTH * D_PLE


def _step2(a):
    x, p = a["x"], a["p"]
    bsz, s_len, d = x.shape
    t = bsz * s_len
    x2 = x.reshape(t, d)
    tgt = a["loss_target"].reshape(t, d)
    p2 = p.reshape(DEPTH, t, D_PLE)
    shapes = {n: a[n].shape for n in WEIGHTS}
    bits_per = 1 if MM == F32 else 2
    n_small = sum(int(np.prod(shapes[n])) for n in SMALL_NAMES)
    small_all = SMALL_NAMES + REPL
    f_ff = shapes["ffn1_wg"][2]
    rows_b = {n: shapes[n][0] * shapes[n][1] for n in GROUP_B}
    off_b = dict(zip(GROUP_B, np.cumsum([0] + [rows_b[n] for n in GROUP_B])[:-1].tolist()))

    send = [
        jnp.concatenate([a[n].astype(MM).reshape(-1, f_ff) for n in GROUP_A], axis=0),
        jnp.concatenate([a[n].astype(MM).reshape(-1, D_MODEL) for n in GROUP_B], axis=0),
        a["ab_w_in"][0].astype(MM),
        a["c_w_in"][0].astype(MM),
        a["ple_wp"].astype(MM).reshape(PLE_WP_ROWS, LANES),
        _flat_pad([a[n] for n in SMALL_NAMES], F32, 32 * LANES).reshape(32, LANES),
    ]
    ga, gb, gc, gd, ge, gf = gather_multi(send, name="gather_weights")
    wa_full = join_cols(ga, name="join_ffn").reshape(len(GROUP_A), DEPTH, D_MODEL, N_DEV * f_ff)
    w = {n: wa_full[i] for i, n in enumerate(GROUP_A)}
    for n in GROUP_B:
        lyr, rws = shapes[n][0], shapes[n][1]
        blk = gb[:, off_b[n]:off_b[n] + rows_b[n]].reshape(N_DEV, lyr, rws, D_MODEL)
        w[n] = jnp.swapaxes(blk, 0, 1).reshape(lyr, N_DEV * rws, D_MODEL)
    w["ab_w_in"] = join_cols(gc, name="join_ab_in")
    c_in_main, c_in_gate = join_cols(gd, name="join_c_in", outs=[(0, 4 * C_WIDTH, 4 * C_WIDTH),
                                                                  (4 * C_WIDTH, 4 * C_WIDTH + 2 * C_HEADS, LANES)])
    w["ple_wp"] = _join(ge.reshape(N_DEV, DEPTH, D_PLE, LANES), 2)
    ws = _take(gf.reshape(N_DEV, -1), SMALL_NAMES, shapes)
    w.update({n: _join(ws[n], 2) for n in SMALL_NAMES})
    ln_g, ln_b = w["ln_g"], w["ln_b"]
    wa_d, wx_d = _dense_blocks(a["b_wa"][0]), _dense_blocks(a["b_wx"][0])
    lru_w = (w["b_conv_w"][0], a["b_conv_b"][0], wa_d, a["b_ba"][0], wx_d, a["b_bx"][0], a["b_lam"][0])
    gdc_w = (a["c_a_log"][0], a["c_dt_bias"][0], a["c_norm_g"][0])

    h = x2
    saved = []
    for i in range(DEPTH):
        s = {"x0": h}
        s["y1"], s["z1"], s["hg1"], s["hu1"] = ffn_fwd(h, w["ffn1_wg"][i], w["ffn1_wu"][i], w["ffn1_wd"][i], ln_g[i, 0], ln_b[i, 0],
                                   name=f"ffn1_fwd_{i}")
        if i == 0:
            s["proj"] = matmul(s["y1"], w["ab_w_in"], mode="nn", name="ab_in_fwd", tn=896, tk=1024)
            ya = attn_fwd(s["proj"], a["a_sinks"][0], bsz, name="attn_fwd")
            yb = lru_fwd(s["proj"], *lru_w, bsz, name="lru_fwd")
            s["mix"] = jnp.concatenate([ya, yb], axis=1)
            w_out = w["ab_w_out"][0]
        else:
            s["proj"] = matmul(s["y1"], c_in_main, mode="nn", name="c_in_fwd", tn=1024, tk=1024)
            s["gates"] = matmul(s["y1"], c_in_gate, mode="nn", name="c_gate_fwd", tk=1024)
            s["qkv"] = gdc_pre_fwd(s["proj"], w["c_conv_w"][0], bsz, name="gdc_pre_fwd")
            s["mix"], s["o_pre"], s["vnew"], s["tmat"], s["states"] = gdc_fwd(
                s["qkv"], s["proj"], s["gates"], *gdc_w, bsz, name="gdc_fwd")
            w_out = w["c_w_out"][0]
        s["y2"], s["z2"] = mm_ln_fwd(s["mix"], w_out, s["y1"], ln_g[i, 1], ln_b[i, 1], name=f"mix_out_fwd_{i}")
        s["y3"], s["z3"], s["hg2"], s["hu2"] = ffn_fwd(s["y2"], w["ffn2_wg"][i], w["ffn2_wu"][i], w["ffn2_wd"][i], ln_g[i, 2], ln_b[i, 2],
                                   name=f"ffn2_fwd_{i}")
        h = ple_fwd(s["y3"], p2[i], w["ple_wg"][i], a["ple_bg"][i], w["ple_wp"][i], name=f"ple_fwd_{i}")
        saved.append(s)
    loss_part, dh = loss_fwd_bwd(h, tgt, name="loss")

    g = {n: [None] * shapes[n][0] for n in ("ffn1_wg", "ffn1_wu", "ffn1_wd", "ffn2_wg", "ffn2_wu", "ffn2_wd", "ln_g",
                                             "ln_b", "ple_wg", "ple_bg", "ple_wp")}
    wide = dict(tm=1024, tn=1408, tk=512)
    tall = dict(tm=1408, tn=1024, tk=512)
    for i in reversed(range(DEPTH)):
        s = saved[i]
        dy3, dt, de, dbg = ple_bwd(dh, s["y3"], p2[i], w["ple_wg"][i], a["ple_bg"][i], w["ple_wp"][i], name=f"ple_bwd_{i}")
        g["ple_wg"][i] = matmul(s["y3"], dt, mode="tn", name=f"ple_wg_grad_{i}", tm=1024, tn=1024)
        g["ple_wp"][i] = matmul(p2[i], de, mode="tn", name=f"ple_wp_grad_{i}", tn=1024)
        g["ple_bg"][i] = dbg[0]
        dz3, dg2, db2 = ln_bwd(dy3, s["z3"], ln_g[i, 2], name=f"ln2_bwd_{i}")
        dy2, act, dhg, dhu = ffn_bwd(dz3, s["hg2"], s["hu2"], w["ffn2_wg"][i], w["ffn2_wu"][i], w["ffn2_wd"][i], name=f"ffn2_bwd_{i}", tm=FFN_BWD_TM)
        g["ffn2_wg"][i] = matmul(s["y2"], dhg, mode="tn", name=f"ffn2_wg_grad_{i}", **wide)
        g["ffn2_wu"][i] = matmul(s["y2"], dhu, mode="tn", name=f"ffn2_wu_grad_{i}", **wide)
        g["ffn2_wd"][i] = matmul(act, dz3, mode="tn", scale=0.5, name=f"ffn2_wd_grad_{i}", **tall)
        dz2, dg1, db1 = ln_bwd(dy2, s["z2"], ln_g[i, 1], name=f"ln1_bwd_{i}")
        if i == 0:
            dmix = matmul(dz2, w["ab_w_out"][0], mode="nt", name="ab_out_bwd", tn=1024, tk=1024)
            g["ab_w_out"] = matmul(s["mix"], dz2, mode="tn", name="ab_out_grad", tm=1024, tn=1024)
            dq, dk, dv, dsk = attn_bwd(s["proj"], a["a_sinks"][0], dmix, bsz, name="attn_bwd")
            dbx, dbgate, dcw, dcb, dwa, dba, dwx, dbxb, dlam = lru_bwd(s["proj"], *lru_w, dmix, bsz, name="lru_bwd")
            dproj = jnp.concatenate([dq, dk, dv, dbx, dbgate], axis=1).astype(MM)
            dy1 = matmul(dproj, w["ab_w_in"], mode="nt", add=dz2, add_scale=DN_ALPHA, name="ab_in_bwd", tn=1024, tk=1792)
            g_ab_in = matmul(s["y1"], dproj, mode="tn", name="ab_in_grad", tm=1024, tn=896)
        else:
            dmix = matmul(dz2, w["c_w_out"][0], mode="nt", name="c_out_bwd", tn=1024, tk=1024)
            g["c_w_out"] = matmul(s["mix"], dz2, mode="tn", name="c_out_grad", tm=1024, tn=1024)
            dqkv, dzc, dgates, dsm = gdc_bwd(s["qkv"], s["proj"], s["gates"], *gdc_w, s["o_pre"], s["vnew"], s["tmat"],
                                             s["states"], dmix, bsz, name="gdc_bwd")
            draw, dccw = gdc_pre_bwd(s["proj"], w["c_conv_w"][0], dqkv, bsz, name="gdc_pre_bwd")
            dproj = jnp.concatenate([draw, dzc], axis=1).astype(MM)
            dgb = dgates.astype(MM)
            dy1 = matmul(dproj, c_in_main, mode="nt", add=dz2, add_scale=DN_ALPHA, name="c_in_bwd", tn=1024, tk=1024)
            dy1 = matmul(dgb, c_in_gate, mode="nt", add=dy1, name="c_gate_bwd", tn=1024)
            g_c_main = matmul(s["y1"], dproj, mode="tn", name="c_in_grad", tm=1024, tn=1024)
            g_c_gate = matmul(s["y1"], dgb, mode="tn", name="c_gate_grad", tm=1024)
        dz1, dg0, db0 = ln_bwd(dy1, s["z1"], ln_g[i, 0], name=f"ln0_bwd_{i}")
        dh, act, dhg, dhu = ffn_bwd(dz1, s["hg1"], s["hu1"], w["ffn1_wg"][i], w["ffn1_wu"][i], w["ffn1_wd"][i], name=f"ffn1_bwd_{i}", tm=FFN_BWD_TM)
        g["ffn1_wg"][i] = matmul(s["x0"], dhg, mode="tn", name=f"ffn1_wg_grad_{i}", **wide)
        g["ffn1_wu"][i] = matmul(s["x0"], dhu, mode="tn", name=f"ffn1_wu_grad_{i}", **wide)
        g["ffn1_wd"][i] = matmul(act, dz1, mode="tn", scale=0.5, name=f"ffn1_wd_grad_{i}", **tall)
        g["ln_g"][i] = jnp.concatenate([dg0, dg1, dg2], axis=0)
        g["ln_b"][i] = jnp.concatenate([db0, db1, db2], axis=0)
    grad_x = dh.reshape(bsz, s_len, d)
    full = {n: jnp.stack(v) if isinstance(v, list) else v[None] for n, v in g.items()}
    full["b_conv_w"] = dcw[None]
    full["c_conv_w"] = jnp.sum(dccw, axis=0)[None]
    dsm_sum = jnp.sum(dsm, axis=0)
    full.update(a_sinks=jnp.sum(dsk, axis=0)[:, :A_HEADS], b_conv_b=dcb, b_wa=_diag_blocks(dwa)[None], b_ba=dba,
                b_wx=_diag_blocks(dwx)[None], b_bx=dbxb, b_lam=dlam, c_a_log=dsm_sum[None, :C_HEADS, 0],
                c_dt_bias=dsm_sum[None, :C_HEADS, 1], c_norm_g=jnp.sum(dsm_sum[C_HEADS:], axis=0)[None])

    small_f32_rows = SMALL_F32 // LANES
    repl_flat = _flat_pad([full[n] for n in REPL], F32, SMALL_F32 - n_small)
    small8 = jnp.concatenate([_flat8_pad([_split(full[n], 2) for n in SMALL_NAMES], F32, n_small),
                              jnp.broadcast_to(repl_flat, (N_DEV,) + repl_flat.shape)], axis=1)
    parts = [
        split_cols([(jnp.concatenate([full[n].reshape(-1, N_DEV * f_ff) for n in GROUP_A], axis=0), N_DEV * f_ff)], f_ff,
                   name="split_ffn"),
        jnp.concatenate([_split(full[n], 1).astype(MM).reshape(N_DEV, -1, D_MODEL) for n in GROUP_B], axis=1),
        split_cols([(g_ab_in, AB_PROJ)], AB_PROJ // N_DEV, name="split_ab_in"),
        split_cols([(g_c_main, 4 * C_WIDTH), (g_c_gate, 2 * C_HEADS)], (4 * C_WIDTH + 2 * C_HEADS) // N_DEV,
                   name="split_c_in"),
        _split(full["ple_wp"], 2).astype(MM).reshape(N_DEV, PLE_WP_ROWS, LANES),
        small8.reshape(N_DEV, small_f32_rows, LANES),
    ]
    ra, rb, rc, rd, re, small_parts = exchange_multi(parts, name="exchange_grads")

    def wmv(n, shape2d):
        return [a[pre + n].reshape(shape2d) for pre in ("", "m_", "v_")]

    res = {}
    for i, n in enumerate(GROUP_A):
        res[n] = adamw_rows(ra, i * DEPTH * D_MODEL, *wmv(n, (DEPTH * D_MODEL, f_ff)), name=f"adamw_{n}")
    for n in GROUP_B:
        res[n] = adamw_rows(rb, off_b[n], *wmv(n, (rows_b[n], D_MODEL)), name=f"adamw_{n}", tr=64)
    res["ab_w_in"] = adamw_rows(rc, 0, *wmv("ab_w_in", (D_MODEL, AB_PROJ // N_DEV)), name="adamw_ab_w_in")
    res["c_w_in"] = adamw_rows(rd, 0, *wmv("c_w_in", (D_MODEL, shapes["c_w_in"][2])), name="adamw_c_w_in")
    res["ple_wp"] = adamw_rows(re, 0, *wmv("ple_wp", (PLE_WP_ROWS, LANES)), name="adamw_ple_wp")
    res_small = adamw_rows(small_parts, 0, *[_flat_pad([a[pre + n] for n in small_all], F32, SMALL_F32).reshape(
        small_f32_rows, LANES) for pre in ("", "m_", "v_")], name="adamw_small", tr=576)
    kinds = []
    for k in range(4):
        kd = {n: res[n][k].reshape(shapes[n]) for n in res}
        kd.update(_take(res_small[k].reshape(-1), small_all, shapes))
        kinds.append(kd)
    loss = lax.psum(loss_part[0, 0], ("x", "y", "c"))
    return (loss, grad_x, *[kinds[0][n] for n in WEIGHTS], *[kinds[1][n] for n in WEIGHTS],
            *[kinds[2][n] for n in WEIGHTS], *[kinds[3][n] for n in WEIGHTS])


BIG_ROWS = 5632
SMALL_F32 = 73728


def _flat_pad(arrs, dtype, total):
    flat = jnp.concatenate([z.astype(dtype).reshape(-1) for z in arrs])
    return jnp.pad(flat, (0, total - flat.shape[0]))


def _flat8_pad(arrs, dtype, total):
    flat = jnp.concatenate([z.astype(dtype).reshape(N_DEV, -1) for z in arrs], axis=1)
    return jnp.pad(flat, ((0, 0), (0, total - flat.shape[1])))


def _bits(z):
    return z if MM == F32 else _as_bf16_bits(z)


def _unbits(z):
    return z if MM == F32 else _from_bf16_bits(z)


def _take(flat, names, shapes):
    out, off = {}, 0
    for n in names:
        sz = int(np.prod(shapes[n]))
        out[n] = flat[..., off:off + sz].reshape(flat.shape[:-1] + tuple(shapes[n]))
        off += sz
    return out


def _step(a):
    x, p = a["x"], a["p"]
    bsz, s_len, d = x.shape
    t = bsz * s_len
    x2 = x.reshape(t, d)
    tgt = a["loss_target"].reshape(t, d)
    p2 = p.reshape(DEPTH, t, D_PLE)
    shapes = {n: a[n].shape for n in WEIGHTS}
    big_names = [n for n, _ in BIG]
    small_names = [n for n, _ in SMALL]
    n_small = sum(int(np.prod(shapes[n])) for n in small_names)
    bits_per = 1 if MM == F32 else 2
    small_rows = -(-(n_small * bits_per) // PACK_ALIGN) * (PACK_ALIGN // PACK_COLS)

    send = jnp.concatenate([
        _flat_pad([a[n] for n in big_names], MM, BIG_ROWS * PACK_COLS).reshape(BIG_ROWS, PACK_COLS),
        _bits(_flat_pad([a[n] for n in small_names], F32, small_rows * PACK_COLS // bits_per)).reshape(small_rows, PACK_COLS),
    ], axis=0)
    gathered = all_gather(send, name="gather_weights")
    wb = _take(gathered[:, :BIG_ROWS].reshape(N_DEV, -1), big_names, shapes)
    ws = _take(_unbits(gathered[:, BIG_ROWS:].reshape(N_DEV, -1)), small_names, shapes)
    w = {n: _join(wb[n], ax) for n, ax in BIG}
    w.update({n: _join(ws[n], ax) for n, ax in SMALL})
    ln_g, ln_b = w["ln_g"], w["ln_b"]
    c_in_main = w["c_w_in"][0][:, :4 * C_WIDTH]
    c_in_gate = jnp.pad(w["c_w_in"][0][:, 4 * C_WIDTH:], ((0, 0), (0, 128 - 2 * C_HEADS)))
    wa_d, wx_d = _dense_blocks(a["b_wa"][0]), _dense_blocks(a["b_wx"][0])
    lru_w = (w["b_conv_w"][0], a["b_conv_b"][0], wa_d, a["b_ba"][0], wx_d, a["b_bx"][0], a["b_lam"][0])
    gdc_w = (a["c_a_log"][0], a["c_dt_bias"][0], a["c_norm_g"][0])

    h = x2
    saved = []
    for i in range(DEPTH):
        s = {"x0": h}
        s["y1"], s["z1"], s["hg1"], s["hu1"] = ffn_fwd(h, w["ffn1_wg"][i], w["ffn1_wu"][i], w["ffn1_wd"][i], ln_g[i, 0], ln_b[i, 0],
                                   name=f"ffn1_fwd_{i}")
        if i == 0:
            s["proj"] = matmul(s["y1"], w["ab_w_in"][0], mode="nn", name="ab_in_fwd")
            ya = attn_fwd(s["proj"], a["a_sinks"][0], bsz, name="attn_fwd")
            yb = lru_fwd(s["proj"], *lru_w, bsz, name="lru_fwd")
            s["mix"] = jnp.concatenate([ya, yb], axis=1)
            w_out = w["ab_w_out"][0]
        else:
            s["proj"] = matmul(s["y1"], c_in_main, mode="nn", name="c_in_fwd")
            s["gates"] = matmul(s["y1"], c_in_gate, mode="nn", name="c_gate_fwd")
            s["qkv"] = gdc_pre_fwd(s["proj"], w["c_conv_w"][0], bsz, name="gdc_pre_fwd")
            s["mix"], s["o_pre"], s["vnew"], s["tmat"], s["states"] = gdc_fwd(
                s["qkv"], s["proj"], s["gates"], *gdc_w, bsz, name="gdc_fwd")
            w_out = w["c_w_out"][0]
        s["y2"], s["z2"] = mm_ln_fwd(s["mix"], w_out, s["y1"], ln_g[i, 1], ln_b[i, 1], name=f"mix_out_fwd_{i}")
        s["y3"], s["z3"], s["hg2"], s["hu2"] = ffn_fwd(s["y2"], w["ffn2_wg"][i], w["ffn2_wu"][i], w["ffn2_wd"][i], ln_g[i, 2], ln_b[i, 2],
                                   name=f"ffn2_fwd_{i}")
        h = ple_fwd(s["y3"], p2[i], w["ple_wg"][i], a["ple_bg"][i], w["ple_wp"][i], name=f"ple_fwd_{i}")
        saved.append(s)
    loss_part, dh = loss_fwd_bwd(h, tgt, name="loss")

    g = {n: [None] * shapes[n][0] for n in ("ffn1_wg", "ffn1_wu", "ffn1_wd", "ffn2_wg", "ffn2_wu", "ffn2_wd", "ln_g",
                                             "ln_b", "ple_wg", "ple_bg", "ple_wp")}
    wide = dict(tm=1024, tn=1408, tk=512)
    tall = dict(tm=1408, tn=1024, tk=512)
    for i in reversed(range(DEPTH)):
        s = saved[i]
        dy3, dt, de, dbg = ple_bwd(dh, s["y3"], p2[i], w["ple_wg"][i], a["ple_bg"][i], w["ple_wp"][i], name=f"ple_bwd_{i}")
        g["ple_wg"][i] = matmul(s["y3"], dt, mode="tn", name=f"ple_wg_grad_{i}")
        g["ple_wp"][i] = matmul(p2[i], de, mode="tn", name=f"ple_wp_grad_{i}")
        g["ple_bg"][i] = dbg[0]
        dz3, dg2, db2 = ln_bwd(dy3, s["z3"], ln_g[i, 2], name=f"ln2_bwd_{i}")
        dy2, act, dhg, dhu = ffn_bwd(dz3, s["hg2"], s["hu2"], w["ffn2_wg"][i], w["ffn2_wu"][i], w["ffn2_wd"][i], name=f"ffn2_bwd_{i}", tm=FFN_BWD_TM)
        g["ffn2_wg"][i] = matmul(s["y2"], dhg, mode="tn", name=f"ffn2_wg_grad_{i}", **wide)
        g["ffn2_wu"][i] = matmul(s["y2"], dhu, mode="tn", name=f"ffn2_wu_grad_{i}", **wide)
        g["ffn2_wd"][i] = matmul(act, dz3, mode="tn", scale=0.5, name=f"ffn2_wd_grad_{i}", **tall)
        dz2, dg1, db1 = ln_bwd(dy2, s["z2"], ln_g[i, 1], name=f"ln1_bwd_{i}")
        if i == 0:
            dmix = matmul(dz2, w["ab_w_out"][0], mode="nt", name="ab_out_bwd")
            g["ab_w_out"] = matmul(s["mix"], dz2, mode="tn", name="ab_out_grad")
            dq, dk, dv, dsk = attn_bwd(s["proj"], a["a_sinks"][0], dmix, bsz, name="attn_bwd")
            dbx, dbgate, dcw, dcb, dwa, dba, dwx, dbxb, dlam = lru_bwd(s["proj"], *lru_w, dmix, bsz, name="lru_bwd")
            dproj = jnp.concatenate([dq, dk, dv, dbx, dbgate], axis=1).astype(MM)
            dy1 = matmul(dproj, w["ab_w_in"][0], mode="nt", add=dz2, add_scale=DN_ALPHA, name="ab_in_bwd")
            g["ab_w_in"] = matmul(s["y1"], dproj, mode="tn", name="ab_in_grad")
        else:
            dmix = matmul(dz2, w["c_w_out"][0], mode="nt", name="c_out_bwd")
            g["c_w_out"] = matmul(s["mix"], dz2, mode="tn", name="c_out_grad")
            dqkv, dzc, dgates, dsm = gdc_bwd(s["qkv"], s["proj"], s["gates"], *gdc_w, s["o_pre"], s["vnew"], s["tmat"],
                                             s["states"], dmix, bsz, name="gdc_bwd")
            draw, dccw = gdc_pre_bwd(s["proj"], w["c_conv_w"][0], dqkv, bsz, name="gdc_pre_bwd")
            dproj = jnp.concatenate([draw, dzc], axis=1).astype(MM)
            dgb = dgates.astype(MM)
            dy1 = matmul(dproj, c_in_main, mode="nt", add=dz2, add_scale=DN_ALPHA, name="c_in_bwd")
            dy1 = matmul(dgb, c_in_gate, mode="nt", add=dy1, name="c_gate_bwd")
            g["c_w_in"] = jnp.concatenate([matmul(s["y1"], dproj, mode="tn", name="c_in_grad"),
                                           matmul(s["y1"], dgb, mode="tn", name="c_gate_grad")[:, :2 * C_HEADS]], axis=1)
        dz1, dg0, db0 = ln_bwd(dy1, s["z1"], ln_g[i, 0], name=f"ln0_bwd_{i}")
        dh, act, dhg, dhu = ffn_bwd(dz1, s["hg1"], s["hu1"], w["ffn1_wg"][i], w["ffn1_wu"][i], w["ffn1_wd"][i], name=f"ffn1_bwd_{i}", tm=FFN_BWD_TM)
        g["ffn1_wg"][i] = matmul(s["x0"], dhg, mode="tn", name=f"ffn1_wg_grad_{i}", **wide)
        g["ffn1_wu"][i] = matmul(s["x0"], dhu, mode="tn", name=f"ffn1_wu_grad_{i}", **wide)
        g["ffn1_wd"][i] = matmul(act, dz1, mode="tn", scale=0.5, name=f"ffn1_wd_grad_{i}", **tall)
        g["ln_g"][i] = jnp.concatenate([dg0, dg1, dg2], axis=0)
        g["ln_b"][i] = jnp.concatenate([db0, db1, db2], axis=0)
    grad_x = dh.reshape(bsz, s_len, d)
    full = {n: jnp.stack(v) if isinstance(v, list) else v[None] for n, v in g.items()}
    full["b_conv_w"] = dcw[None]
    full["c_conv_w"] = jnp.sum(dccw, axis=0)[None]
    dsm_sum = jnp.sum(dsm, axis=0)
    full.update(a_sinks=jnp.sum(dsk, axis=0)[:, :A_HEADS], b_conv_b=dcb, b_wa=_diag_blocks(dwa)[None], b_ba=dba,
                b_wx=_diag_blocks(dwx)[None], b_bx=dbxb, b_lam=dlam, c_a_log=dsm_sum[None, :C_HEADS, 0],
                c_dt_bias=dsm_sum[None, :C_HEADS, 1], c_norm_g=jnp.sum(dsm_sum[C_HEADS:], axis=0)[None])

    small_cols = SMALL_F32 * bits_per // PACK_COLS
    repl_flat = _flat_pad([full[n] for n in REPL], F32, SMALL_F32 - n_small)
    small8 = jnp.concatenate([_flat8_pad([_split(full[n], ax) for n, ax in SMALL], F32, n_small),
                              jnp.broadcast_to(repl_flat, (N_DEV,) + repl_flat.shape)], axis=1)
    parts = jnp.concatenate([
        _flat8_pad([_split(full[n], ax) for n, ax in BIG], MM, BIG_ROWS * PACK_COLS).reshape(N_DEV, BIG_ROWS, PACK_COLS),
        _bits(small8).reshape(N_DEV, small_cols, PACK_COLS)], axis=1)
    recv = all_to_all(parts, name="exchange_grads")

    def mine(prefix, names, dtype_total):
        return _flat_pad([a[prefix + n] for n in names], F32, dtype_total)

    outs = {}
    big_total = BIG_ROWS * PACK_COLS
    res_big = adamw_sum(recv, *[mine(pre, big_names, big_total).reshape(BIG_ROWS, PACK_COLS) for pre in ("", "m_", "v_")],
                        name="adamw_big")
    small_all = small_names + REPL
    cols_f32 = PACK_COLS // bits_per
    res_small = adamw_sum(_unbits(recv[:, BIG_ROWS:]).reshape(N_DEV, small_cols, cols_f32),
                          *[mine(pre, small_all, SMALL_F32).reshape(small_cols, cols_f32) for pre in ("", "m_", "v_")],
                          name="adamw_small")
    kinds = []
    for rb, rs in zip(res_big, res_small):
        k = _take(rb.reshape(-1), big_names, shapes)
        k.update(_take(rs.reshape(-1), small_all, shapes))
        kinds.append(k)
    loss = lax.psum(loss_part[0, 0], ("x", "y", "c"))
    return (loss, grad_x, *[kinds[0][n] for n in WEIGHTS], *[kinds[1][n] for n in WEIGHTS],
            *[kinds[2][n] for n in WEIGHTS], *[kinds[3][n] for n in WEIGHTS])
```

```python
import functools
import math

import numpy as np
import jax
import jax.numpy as jnp
from jax import lax
from jax.experimental import pallas as pl
from jax.experimental.pallas import tpu as pltpu

F32 = jnp.float32
MM = jnp.bfloat16
HI = lax.Precision.HIGHEST

D_MODEL = 1024
D_FF = 2816
D_PLE = 256
DEPTH = 2
CHUNK = 64
A_HEADS = 8
A_KV_HEADS = 2
A_GROUP = 4
A_HEAD_DIM = 64
A_WIDTH = 512
A_KV_WIDTH = 128
B_WIDTH = 512
B_BLOCK = 64
RG_C = 8.0
AB_PROJ = 1792
C_HEADS = 8
C_HEAD_DIM = 128
C_WIDTH = 1024
DN_ALPHA = (2.0 * DEPTH) ** 0.25
LN_EPS = 1e-5
NORM_EPS = 1e-6
NEG = -1e30
ADAM_LR = 0.001
ADAM_B1 = 0.9
ADAM_B2 = 0.999
ADAM_EPS = 1e-08
ADAM_WD = 0.01
ADAM_STEP = 10
N_DEV = 8
VMEM_LIMIT = 56 * 1024 * 1024

NN = ((1,), (0,))
NT = ((1,), (1,))
TN = ((0,), (0,))


def _pcall(body, **kw):
    return pl.pallas_call(body, **kw)


def _cp(*sem):
    return pltpu.CompilerParams(dimension_semantics=sem, vmem_limit_bytes=VMEM_LIMIT)


MESH_ID = pl.DeviceIdType.MESH
_FLIPS = [(0, 0, 1), (1, 0, 0), (0, 1, 0), (1, 1, 0), (1, 0, 1), (0, 1, 1), (1, 1, 1)]


def _me():
    return lax.axis_index("x"), lax.axis_index("y"), lax.axis_index("c")


def _flip(coord, d):
    return 1 - coord if d else coord


def _side_copies(kind, x_refs, o_refs, send_sems, recv_sems, local_sems, start):
    x, y, c = _me()
    mine = 4 * x + 2 * y + c
    for gi, (x_ref, o_ref) in enumerate(zip(x_refs, o_refs)):
        src_own = x_ref if kind == "gather" else x_ref.at[mine]
        own = pltpu.make_async_copy(src_own, o_ref.at[mine], local_sems.at[gi])
        own.start() if start else own.wait()
        for k, (dx, dy, dc) in enumerate(_FLIPS):
            px, py, pc = _flip(x, dx), _flip(y, dy), _flip(c, dc)
            src = x_ref if kind == "gather" else x_ref.at[4 * px + 2 * py + pc]
            cp = pltpu.make_async_remote_copy(
                src_ref=src, dst_ref=o_ref.at[mine], send_sem=send_sems.at[7 * gi + k], recv_sem=recv_sems.at[7 * gi + k],
                device_id=(px, py, pc), device_id_type=MESH_ID)
            cp.start() if start else cp.wait()


def _call(body, args, side, grid, **kw):
    if side is None:
        return _pcall(body, grid=grid, **kw)(*args)
    kind, arrs = side
    ns, n_in, n_out = len(arrs), len(args), len(kw["out_specs"])
    scratch = list(kw.get("scratch_shapes", []))
    n_scr = len(scratch)

    def edge(at_end):
        conds = [pl.program_id(ax) == (n - 1 if at_end else 0) for ax, n in enumerate(grid)]
        return functools.reduce(jnp.logical_and, conds)

    def wrapped(*refs):
        ins, sx = refs[:n_in], refs[n_in:n_in + ns]
        outs, so = refs[n_in + ns:n_in + ns + n_out], refs[n_in + ns + n_out:n_in + 2 * ns + n_out]
        rest = refs[n_in + 2 * ns + n_out:]
        scr, sems = rest[:n_scr], rest[n_scr:]

        @pl.when(edge(False))
        def _():
            _side_copies(kind, sx, so, *sems, start=True)

        body(*ins, *outs, *scr)

        @pl.when(edge(True))
        def _():
            _side_copies(kind, sx, so, *sems, start=False)

    hbm = pl.BlockSpec(memory_space=pl.ANY)
    side_shapes = [jax.ShapeDtypeStruct(((N_DEV,) if kind == "gather" else ()) + z.shape, z.dtype) for z in arrs]
    kw = dict(kw)
    kw["in_specs"] = list(kw["in_specs"]) + [hbm] * ns
    kw["out_specs"] = list(kw["out_specs"]) + [hbm] * ns
    kw["out_shape"] = list(kw["out_shape"]) + side_shapes
    kw["scratch_shapes"] = scratch + [pltpu.SemaphoreType.DMA((7 * ns,)), pltpu.SemaphoreType.DMA((7 * ns,)),
                                      pltpu.SemaphoreType.DMA((ns,))]
    kw["compiler_params"] = _cp(*["arbitrary"] * len(grid))
    res = _pcall(wrapped, grid=grid, **kw)(*args, *arrs)
    return list(res[:n_out]), list(res[n_out:])


def _dot(a, b, dims=NN, precision=None):
    return lax.dot_general(a, b, (dims, ((), ())), preferred_element_type=F32, precision=precision)


def _mdot(a, b, dims=NN):
    return _dot(a.astype(MM), b.astype(MM), dims)


def _tile(n, pref):
    if n <= pref:
        return n
    for c in range(pref - pref % 128, 0, -128):
        if n % c == 0:
            return c
    return n


def _sigmoid(x):
    return 1.0 / (1.0 + jnp.exp(-x))


def _softplus(x):
    return jnp.maximum(x, 0.0) + jnp.log(1.0 + jnp.exp(-jnp.abs(x)))


def _ln_stats(z):
    mu = jnp.mean(z, axis=-1, keepdims=True)
    zc = z - mu
    var = jnp.mean(zc * zc, axis=-1, keepdims=True)
    return zc, lax.rsqrt(var + LN_EPS)


def matmul(a, b, *, mode, name, tm=512, tn=512, tk=512, out_dtype=F32, scale=None, add=None, add_scale=1.0):
    if mode == "nn":
        (m, kk), (_, n) = a.shape, b.shape
        dims = NN
    elif mode == "nt":
        (m, kk), (n, _) = a.shape, b.shape
        dims = NT
    else:
        (kk, m), (_, n) = a.shape, b.shape
        dims = TN
    tm, tn, tk = _tile(m, tm), _tile(n, tn), _tile(kk, tk)
    if mode == "nn":
        a_spec = pl.BlockSpec((tm, tk), lambda i, j, k: (i, k))
        b_spec = pl.BlockSpec((tk, tn), lambda i, j, k: (k, j))
    elif mode == "nt":
        a_spec = pl.BlockSpec((tm, tk), lambda i, j, k: (i, k))
        b_spec = pl.BlockSpec((tn, tk), lambda i, j, k: (j, k))
    else:
        a_spec = pl.BlockSpec((tk, tm), lambda i, j, k: (k, i))
        b_spec = pl.BlockSpec((tk, tn), lambda i, j, k: (k, j))
    nk = kk // tk
    o_spec = pl.BlockSpec((tm, tn), lambda i, j, k: (i, j))
    has_add = add is not None

    def body(*refs):
        if has_add:
            a_ref, b_ref, add_ref, o_ref, acc_ref = refs
        else:
            a_ref, b_ref, o_ref, acc_ref = refs
        k = pl.program_id(2)

        @pl.when(k == 0)
        def _():
            acc_ref[...] = jnp.zeros_like(acc_ref)

        acc_ref[...] += _mdot(a_ref[...], b_ref[...], dims)

        @pl.when(k == nk - 1)
        def _():
            r = acc_ref[...]
            if scale is not None:
                r = r * scale
            if has_add:
                r = r + add_scale * add_ref[...].astype(F32)
            o_ref[...] = r.astype(out_dtype)

    ins = [a, b] + ([add] if has_add else [])
    in_specs = [a_spec, b_spec] + ([o_spec] if has_add else [])
    return _pcall(
        body, name=name, grid=(m // tm, n // tn, nk), in_specs=in_specs, out_specs=o_spec,
        out_shape=jax.ShapeDtypeStruct((m, n), out_dtype), scratch_shapes=[pltpu.VMEM((tm, tn), F32)],
        compiler_params=_cp("parallel", "parallel", "arbitrary"),
    )(*ins)


def ffn_fwd(x, wg, wu, wd, g, b, *, name, tm=512, tf=256, side=None):
    t, d = x.shape
    f = wg.shape[1]
    tm = min(tm, t)
    nj = f // tf

    def body(x_ref, wg_ref, wu_ref, wd_ref, g_ref, b_ref, y_ref, z_ref, hg_ref, hu_ref, xb_ref, acc_ref):
        j = pl.program_id(1)

        @pl.when(j == 0)
        def _():
            xb_ref[...] = x_ref[...].astype(MM)
            acc_ref[...] = jnp.zeros_like(acc_ref)

        xb = xb_ref[...]
        hg = _dot(xb, wg_ref[...])
        hu = _dot(xb, wu_ref[...])
        hg_ref[...] = hg.astype(MM)
        hu_ref[...] = hu.astype(MM)
        act = (hg * _sigmoid(hg) * hu).astype(MM)
        acc_ref[...] += _dot(act, wd_ref[...])

        @pl.when(j == nj - 1)
        def _():
            z = DN_ALPHA * x_ref[...] + 0.5 * acc_ref[...]
            z_ref[...] = z
            zc, rstd = _ln_stats(z)
            y_ref[...] = zc * rstd * g_ref[...] + b_ref[...]

    row = pl.BlockSpec((tm, d), lambda i, j: (i, 0))
    hid = pl.BlockSpec((tm, tf), lambda i, j: (i, j))
    vec = pl.BlockSpec((1, d), lambda i, j: (0, 0))
    return _call(
        body, (x, wg, wu, wd, g.reshape(1, d), b.reshape(1, d)), side, (t // tm, nj), name=name,
        in_specs=[row, pl.BlockSpec((d, tf), lambda i, j: (0, j)), pl.BlockSpec((d, tf), lambda i, j: (0, j)),
                  pl.BlockSpec((tf, d), lambda i, j: (j, 0)), vec, vec],
        out_specs=[row, row, hid, hid],
        out_shape=[jax.ShapeDtypeStruct((t, d), F32), jax.ShapeDtypeStruct((t, d), F32),
                   jax.ShapeDtypeStruct((t, f), MM), jax.ShapeDtypeStruct((t, f), MM)],
        scratch_shapes=[pltpu.VMEM((tm, d), MM), pltpu.VMEM((tm, d), F32)],
        compiler_params=_cp("parallel", "arbitrary"),
    )


def ffn_bwd(dz, hg, hu, wg, wu, wd, *, name, tm=512, tf=256, side=None):
    t, d = dz.shape
    f = wg.shape[1]
    tm = min(tm, t)
    nj = f // tf

    def body(dz_ref, hg_ref, hu_ref, wg_ref, wu_ref, wd_ref, dx_ref, act_ref, dhg_ref, dhu_ref, dfb_ref, acc_ref):
        j = pl.program_id(1)

        @pl.when(j == 0)
        def _():
            dfb_ref[...] = (0.5 * dz_ref[...]).astype(MM)
            acc_ref[...] = jnp.zeros_like(acc_ref)

        hg = hg_ref[...].astype(F32)
        hu = hu_ref[...].astype(F32)
        s = _sigmoid(hg)
        dact = _dot(dfb_ref[...], wd_ref[...], NT)
        sg = hg * s
        act_ref[...] = (sg * hu).astype(MM)
        dhu = (dact * sg).astype(MM)
        dhg = (dact * hu * (s + sg * (1.0 - s))).astype(MM)
        dhu_ref[...] = dhu
        dhg_ref[...] = dhg
        acc_ref[...] += _dot(dhg, wg_ref[...], NT) + _dot(dhu, wu_ref[...], NT)

        @pl.when(j == nj - 1)
        def _():
            dx_ref[...] = DN_ALPHA * dz_ref[...] + acc_ref[...]

    row = pl.BlockSpec((tm, d), lambda i, j: (i, 0))
    hid = pl.BlockSpec((tm, tf), lambda i, j: (i, j))
    return _call(
        body, (dz, hg, hu, wg, wu, wd), side, (t // tm, nj), name=name,
        in_specs=[row, hid, hid, pl.BlockSpec((d, tf), lambda i, j: (0, j)), pl.BlockSpec((d, tf), lambda i, j: (0, j)),
                  pl.BlockSpec((tf, d), lambda i, j: (j, 0))],
        out_specs=[row, hid, hid, hid],
        out_shape=[jax.ShapeDtypeStruct((t, d), F32)] + [jax.ShapeDtypeStruct((t, f), MM)] * 3,
        scratch_shapes=[pltpu.VMEM((tm, d), MM), pltpu.VMEM((tm, d), F32)],
        compiler_params=_cp("parallel", "arbitrary"),
    )


def ln_bwd(dy, z, g, *, name, tm=512):
    t, d = z.shape
    tm = min(tm, t)

    def body(dy_ref, z_ref, g_ref, dz_ref, dg_ref, db_ref):
        i = pl.program_id(0)

        @pl.when(i == 0)
        def _():
            dg_ref[...] = jnp.zeros_like(dg_ref)
            db_ref[...] = jnp.zeros_like(db_ref)

        dy = dy_ref[...]
        zc, rstd = _ln_stats(z_ref[...])
        xh = zc * rstd
        dg_ref[...] += jnp.sum(dy * xh, axis=0, keepdims=True)
        db_ref[...] += jnp.sum(dy, axis=0, keepdims=True)
        dxh = dy * g_ref[...]
        m1 = jnp.mean(dxh, axis=-1, keepdims=True)
        m2 = jnp.mean(dxh * xh, axis=-1, keepdims=True)
        dz_ref[...] = rstd * (dxh - m1 - xh * m2)

    row = pl.BlockSpec((tm, d), lambda i: (i, 0))
    vec = pl.BlockSpec((1, d), lambda i: (0, 0))
    return _pcall(
        body, name=name, grid=(t // tm,), in_specs=[row, row, vec], out_specs=[row, vec, vec],
        out_shape=[jax.ShapeDtypeStruct((t, d), F32), jax.ShapeDtypeStruct((1, d), F32), jax.ShapeDtypeStruct((1, d), F32)],
        compiler_params=_cp("arbitrary"),
    )(dy, z, g.reshape(1, d))


def mm_ln_fwd(a, w, res, g, b, *, name, tm=512):
    t, kk = a.shape
    d = w.shape[1]
    tm = min(tm, t)

    def body(a_ref, w_ref, res_ref, g_ref, b_ref, y_ref, z_ref):
        z = DN_ALPHA * res_ref[...] + _mdot(a_ref[...], w_ref[...])
        z_ref[...] = z
        zc, rstd = _ln_stats(z)
        y_ref[...] = zc * rstd * g_ref[...] + b_ref[...]

    row = pl.BlockSpec((tm, d), lambda i: (i, 0))
    vec = pl.BlockSpec((1, d), lambda i: (0, 0))
    return _pcall(
        body, name=name, grid=(t // tm,),
        in_specs=[pl.BlockSpec((tm, kk), lambda i: (i, 0)), pl.BlockSpec((kk, d), lambda i: (0, 0)), row, vec, vec],
        out_specs=[row, row],
        out_shape=[jax.ShapeDtypeStruct((t, d), F32), jax.ShapeDtypeStruct((t, d), F32)],
        compiler_params=_cp("parallel"),
    )(a, w, res, g.reshape(1, d), b.reshape(1, d))


def ple_fwd(y, p, wg, bg, wp, *, name, tm=512):
    t, d = y.shape
    dp = p.shape[1]
    tm = min(tm, t)

    def body(y_ref, p_ref, wg_ref, bg_ref, wp_ref, o_ref):
        yv = y_ref[...]
        gate = _sigmoid(_mdot(yv, wg_ref[...]) + bg_ref[...])
        o_ref[...] = yv + gate * _mdot(p_ref[...], wp_ref[...])

    row = pl.BlockSpec((tm, d), lambda i: (i, 0))
    return _pcall(
        body, name=name, grid=(t // tm,),
        in_specs=[row, pl.BlockSpec((tm, dp), lambda i: (i, 0)), pl.BlockSpec((d, d), lambda i: (0, 0)),
                  pl.BlockSpec((1, d), lambda i: (0, 0)), pl.BlockSpec((dp, d), lambda i: (0, 0))],
        out_specs=row, out_shape=jax.ShapeDtypeStruct((t, d), F32), compiler_params=_cp("parallel"),
    )(y, p, wg, bg.reshape(1, d), wp)


def ple_bwd(do, y, p, wg, bg, wp, *, name, tm=512):
    t, d = y.shape
    dp = p.shape[1]
    tm = min(tm, t)

    def body(do_ref, y_ref, p_ref, wg_ref, bg_ref, wp_ref, dy_ref, dt_ref, de_ref, dbg_ref):
        i = pl.program_id(0)

        @pl.when(i == 0)
        def _():
            dbg_ref[...] = jnp.zeros_like(dbg_ref)

        dov = do_ref[...]
        gate = _sigmoid(_mdot(y_ref[...], wg_ref[...]) + bg_ref[...])
        emb = _mdot(p_ref[...], wp_ref[...])
        dt = dov * emb * gate * (1.0 - gate)
        dbg_ref[...] += jnp.sum(dt, axis=0, keepdims=True)
        dtb = dt.astype(MM)
        dt_ref[...] = dtb
        de_ref[...] = (dov * gate).astype(MM)
        dy_ref[...] = dov + _dot(dtb, wg_ref[...], NT)

    row = pl.BlockSpec((tm, d), lambda i: (i, 0))
    vec = pl.BlockSpec((1, d), lambda i: (0, 0))
    return _pcall(
        body, name=name, grid=(t // tm,),
        in_specs=[row, row, pl.BlockSpec((tm, dp), lambda i: (i, 0)), pl.BlockSpec((d, d), lambda i: (0, 0)),
                  vec, pl.BlockSpec((dp, d), lambda i: (0, 0))],
        out_specs=[row, row, row, vec],
        out_shape=[jax.ShapeDtypeStruct((t, d), F32), jax.ShapeDtypeStruct((t, d), MM),
                   jax.ShapeDtypeStruct((t, d), MM), jax.ShapeDtypeStruct((1, d), F32)],
        compiler_params=_cp("arbitrary"),
    )(do, y, p, wg, bg.reshape(1, d), wp)


def loss_fwd_bwd(y, tgt, *, name, tm=512):
    t, d = y.shape
    tm = min(tm, t)

    def body(y_ref, t_ref, l_ref, dy_ref):
        i = pl.program_id(0)

        @pl.when(i == 0)
        def _():
            l_ref[...] = jnp.zeros_like(l_ref)

        err = y_ref[...] - t_ref[...]
        dy_ref[...] = err * (1.0 / d)
        l_ref[...] += (0.5 / d) * jnp.sum(jnp.sum(err * err, axis=1, keepdims=True), axis=0, keepdims=True)

    row = pl.BlockSpec((tm, d), lambda i: (i, 0))
    return _pcall(
        body, name=name, grid=(t // tm,), in_specs=[row, row],
        out_specs=[pl.BlockSpec((1, 128), lambda i: (0, 0)), row],
        out_shape=[jax.ShapeDtypeStruct((1, 128), F32), jax.ShapeDtypeStruct((t, d), F32)],
        compiler_params=_cp("arbitrary"),
    )(y, tgt)


def _shift_dn(x, s, row):
    return x if s == 0 else jnp.where(row >= s, pltpu.roll(x, s, 0), 0.0)


def _shift_up(x, s, row):
    n = x.shape[0]
    return x if s == 0 else jnp.where(row < n - s, pltpu.roll(x, n - s, 0), 0.0)


def _conv_fwd(x, w, row):
    kk = w.shape[0]
    y = w[kk - 1:kk, :] * x
    for j in range(kk - 1):
        y = y + w[j:j + 1, :] * _shift_dn(x, kk - 1 - j, row)
    return y


def _conv_bwd(x, w, dy, row):
    kk = w.shape[0]
    dx = w[kk - 1:kk, :] * dy
    dws = []
    for j in range(kk - 1):
        dx = dx + w[j:j + 1, :] * _shift_up(dy, kk - 1 - j, row)
        dws.append(jnp.sum(dy * _shift_dn(x, kk - 1 - j, row), axis=0, keepdims=True))
    dws.append(jnp.sum(dy * x, axis=0, keepdims=True))
    return dx, jnp.concatenate(dws, axis=0)


def _gelu(x):
    c = math.sqrt(2.0 / math.pi)
    th = jnp.tanh(c * (x + 0.044715 * x * x * x))
    return 0.5 * x * (1.0 + th), th


def _gelu_grad(x, th):
    c = math.sqrt(2.0 / math.pi)
    return 0.5 * (1.0 + th) + 0.5 * x * (1.0 - th * th) * c * (1.0 + 3.0 * 0.044715 * x * x)


def _neg_expm1(y):
    ser = -(y * (1.0 + y * (0.5 + y * (1.0 / 6.0 + y * (1.0 / 24.0 + y * (1.0 / 120.0))))))
    return jnp.where(y > -0.05, ser, 1.0 - jnp.exp(y))


def _attn_head(qh, kk, vv, bias, valid, sink):
    s = _mdot(qh, kk, NT) * (A_HEAD_DIM ** -0.5) - bias
    s = jnp.where(valid, s, NEG)
    m = jnp.maximum(jnp.max(s, axis=-1, keepdims=True), sink)
    pr = jnp.exp(s - m)
    den = jnp.sum(pr, axis=-1, keepdims=True) + jnp.exp(sink - m)
    return pr / den, jnp.exp(sink - m) / den


def _attn_valid(n):
    ji = lax.broadcasted_iota(jnp.int32, (1, 3 * CHUNK), 1)
    return (n * CHUNK + ji - 2 * CHUNK) >= 0


def _attn_group_consts(kh, sk_ref):
    rows = A_GROUP * CHUNK
    ri = lax.broadcasted_iota(jnp.int32, (rows, 3 * CHUNK), 0)
    ji = lax.broadcasted_iota(jnp.int32, (rows, 3 * CHUNK), 1)
    dist = jnp.abs((ri & (CHUNK - 1)) + 2 * CHUNK - ji).astype(F32)
    rcol = lax.broadcasted_iota(jnp.int32, (rows, 1), 0)
    slope = jnp.zeros((rows, 1), F32)
    sink = jnp.zeros((rows, 1), F32)
    for gi in range(A_GROUP):
        h = kh * A_GROUP + gi
        inblk = (rcol >= gi * CHUNK) & (rcol < (gi + 1) * CHUNK)
        slope = jnp.where(inblk, 2.0 ** -(h + 1), slope)
        sink = jnp.where(inblk, sk_ref[h], sink)
    return slope * dist, sink


def _stack_heads(x, kh):
    return jnp.concatenate([x[:, (kh * A_GROUP + gi) * 64:(kh * A_GROUP + gi + 1) * 64] for gi in range(A_GROUP)], axis=0)


def _attn_masks(n):
    ci = lax.broadcasted_iota(jnp.int32, (CHUNK, 3 * CHUNK), 0)
    ji = lax.broadcasted_iota(jnp.int32, (CHUNK, 3 * CHUNK), 1)
    dist = jnp.abs(ci + 2 * CHUNK - ji).astype(F32)
    valid = (n * CHUNK + ji - 2 * CHUNK) >= 0
    return dist, valid


def attn_fwd(proj, sinks, bsz, *, name):
    t = proj.shape[0]
    s_len = t // bsz
    nc = s_len // CHUNK
    pad = 2 * CHUNK

    def body(q_ref, k_ref, v_ref, sk_ref, o_ref, kp_ref, vp_ref):
        kp_ref[0:pad, :] = jnp.zeros((pad, A_KV_WIDTH), F32)
        vp_ref[0:pad, :] = jnp.zeros((pad, A_KV_WIDTH), F32)
        kp_ref[pad:, :] = k_ref[...]
        vp_ref[pad:, :] = v_ref[...]

        consts = [_attn_group_consts(kh, sk_ref) for kh in range(A_KV_HEADS)]

        def chunk(n, carry):
            st = pl.multiple_of(n * CHUNK, CHUNK)
            q = q_ref[pl.ds(st, CHUNK), :]
            kb = kp_ref[pl.ds(st, 3 * CHUNK), :]
            vb = vp_ref[pl.ds(st, 3 * CHUNK), :]
            valid = _attn_valid(n)
            outs = []
            for kh in range(A_KV_HEADS):
                bias, sink = consts[kh]
                pn, _ = _attn_head(_stack_heads(q, kh), kb[:, kh * 64:(kh + 1) * 64], None, bias, valid, sink)
                o = _mdot(pn, vb[:, kh * 64:(kh + 1) * 64])
                outs += [o[gi * CHUNK:(gi + 1) * CHUNK] for gi in range(A_GROUP)]
            o_ref[pl.ds(st, CHUNK), :] = jnp.concatenate(outs, axis=-1)
            return carry

        lax.fori_loop(0, nc, chunk, 0)

    return _pcall(
        body, name=name, grid=(bsz,),
        in_specs=[pl.BlockSpec((s_len, A_WIDTH), lambda b: (b, 0)), pl.BlockSpec((s_len, 128), lambda b: (b, 4)),
                  pl.BlockSpec((s_len, 128), lambda b: (b, 5)), pl.BlockSpec(memory_space=pltpu.SMEM)],
        out_specs=pl.BlockSpec((s_len, A_WIDTH), lambda b: (b, 0)),
        out_shape=jax.ShapeDtypeStruct((t, A_WIDTH), F32),
        scratch_shapes=[pltpu.VMEM((s_len + pad, A_KV_WIDTH), F32), pltpu.VMEM((s_len + pad, A_KV_WIDTH), F32)],
        compiler_params=_cp("parallel"),
    )(proj, proj, proj, sinks)


def attn_bwd(proj, sinks, dcat, bsz, *, name):
    t = proj.shape[0]
    s_len = t // bsz
    nc = s_len // CHUNK
    pad = 2 * CHUNK

    def body(q_ref, k_ref, v_ref, do_ref, sk_ref, dq_ref, dk_ref, dv_ref, dsk_ref, kp_ref, vp_ref, dkp_ref, dvp_ref):
        kp_ref[0:pad, :] = jnp.zeros((pad, A_KV_WIDTH), F32)
        vp_ref[0:pad, :] = jnp.zeros((pad, A_KV_WIDTH), F32)
        kp_ref[pad:, :] = k_ref[...]
        vp_ref[pad:, :] = v_ref[...]
        dkp_ref[...] = jnp.zeros_like(dkp_ref)
        dvp_ref[...] = jnp.zeros_like(dvp_ref)
        lane = lax.broadcasted_iota(jnp.int32, (1, 128), 1)

        consts = [_attn_group_consts(kh, sk_ref) for kh in range(A_KV_HEADS)]

        def chunk(n, dsk):
            st = pl.multiple_of(n * CHUNK, CHUNK)
            q = q_ref[pl.ds(st, CHUNK), :]
            do = do_ref[pl.ds(st, CHUNK), :]
            kb = kp_ref[pl.ds(st, 3 * CHUNK), :]
            vb = vp_ref[pl.ds(st, 3 * CHUNK), :]
            valid = _attn_valid(n)
            dqs, dks, dvs = [], [], []
            for kh in range(A_KV_HEADS):
                kk = kb[:, kh * 64:(kh + 1) * 64]
                vv = vb[:, kh * 64:(kh + 1) * 64]
                bias, sink = consts[kh]
                qs = _stack_heads(q, kh)
                dos = _stack_heads(do, kh)
                pn, psink = _attn_head(qs, kk, None, bias, valid, sink)
                dp = _mdot(dos, vv, NT)
                rowdot = jnp.sum(pn * dp, axis=-1, keepdims=True)
                ds = pn * (dp - rowdot)
                sink_part = psink * rowdot
                for gi in range(A_GROUP):
                    part = jnp.sum(sink_part[gi * CHUNK:(gi + 1) * CHUNK], axis=0, keepdims=True)
                    dsk = dsk + jnp.where(lane == kh * A_GROUP + gi, -part, 0.0)
                dq = _mdot(ds, kk) * (A_HEAD_DIM ** -0.5)
                dqs += [dq[gi * CHUNK:(gi + 1) * CHUNK] for gi in range(A_GROUP)]
                dks.append(_mdot(ds, qs, TN) * (A_HEAD_DIM ** -0.5))
                dvs.append(_mdot(pn, dos, TN))
            dq_ref[pl.ds(st, CHUNK), :] = jnp.concatenate(dqs, axis=-1)
            dkp_ref[pl.ds(st, 3 * CHUNK), :] += jnp.concatenate(dks, axis=-1)
            dvp_ref[pl.ds(st, 3 * CHUNK), :] += jnp.concatenate(dvs, axis=-1)
            return dsk

        dsk = lax.fori_loop(0, nc, chunk, jnp.zeros((1, 128), F32))
        dsk_ref[0] = dsk
        dk_ref[...] = dkp_ref[pad:, :]
        dv_ref[...] = dvp_ref[pad:, :]

    kv = jax.ShapeDtypeStruct((t, A_KV_WIDTH), F32)
    return _pcall(
        body, name=name, grid=(bsz,),
        in_specs=[pl.BlockSpec((s_len, A_WIDTH), lambda b: (b, 0)), pl.BlockSpec((s_len, 128), lambda b: (b, 4)),
                  pl.BlockSpec((s_len, 128), lambda b: (b, 5)), pl.BlockSpec((s_len, A_WIDTH), lambda b: (b, 0)),
                  pl.BlockSpec(memory_space=pltpu.SMEM)],
        out_specs=[pl.BlockSpec((s_len, A_WIDTH), lambda b: (b, 0)), pl.BlockSpec((s_len, 128), lambda b: (b, 0)),
                   pl.BlockSpec((s_len, 128), lambda b: (b, 0)), pl.BlockSpec((1, 1, 128), lambda b: (b, 0, 0))],
        out_shape=[jax.ShapeDtypeStruct((t, A_WIDTH), F32), kv, kv, jax.ShapeDtypeStruct((bsz, 1, 128), F32)],
        scratch_shapes=[pltpu.VMEM((s_len + pad, A_KV_WIDTH), F32)] * 4,
        compiler_params=_cp("parallel"),
    )(proj, proj, proj, dcat, sinks)


def _lru_gates(x, cw, cb, wa, ba, wx, bx, lam, row):
    xc = _conv_fwd(x, cw, row) + cb
    r = _sigmoid(_mdot(xc, wa) + ba)
    i = _sigmoid(_mdot(xc, wx) + bx)
    sp = _softplus(-lam)
    log_a = -RG_C * r * sp
    a = jnp.exp(log_a)
    mult = jnp.sqrt(_neg_expm1(2.0 * log_a))
    return xc, r, i, sp, a, mult


def _lru_scan(a, u, row):
    n = a.shape[0]
    d = 1
    while d < n:
        a_sh = jnp.where(row >= d, pltpu.roll(a, d, 0), 1.0)
        u_sh = jnp.where(row >= d, pltpu.roll(u, d, 0), 0.0)
        u = a * u_sh + u
        a = a * a_sh
        d *= 2
    return u


def _lru_scan_rev(a, u, row):
    n = a.shape[0]
    d = 1
    while d < n:
        a_sh = jnp.where(row < n - d, pltpu.roll(a, n - d, 0), 1.0)
        u_sh = jnp.where(row < n - d, pltpu.roll(u, n - d, 0), 0.0)
        u = a * u_sh + u
        a = a * a_sh
        d *= 2
    return u


def _lru_specs(s_len, order):
    def at(f):
        return lambda *g: f(*order(*g))
    return [pl.BlockSpec((s_len, 128), at(lambda b, cb: (b, 6 + cb))), pl.BlockSpec((s_len, 128), at(lambda b, cb: (b, 10 + cb))),
            pl.BlockSpec((4, 128), at(lambda b, cb: (0, cb))), pl.BlockSpec((1, 128), at(lambda b, cb: (0, cb))),
            pl.BlockSpec((1, 128, 128), at(lambda b, cb: (cb, 0, 0))), pl.BlockSpec((1, 128), at(lambda b, cb: (0, cb))),
            pl.BlockSpec((1, 128, 128), at(lambda b, cb: (cb, 0, 0))), pl.BlockSpec((1, 128), at(lambda b, cb: (0, cb))),
            pl.BlockSpec((1, 128), at(lambda b, cb: (0, cb)))]


def lru_fwd(proj, cw, cb, wa, ba, wx, bxb, lam, bsz, *, name):
    t = proj.shape[0]
    s_len = t // bsz

    def body(x_ref, g_ref, cw_ref, cb_ref, wa_ref, ba_ref, wx_ref, bx_ref, lam_ref, y_ref):
        row = lax.broadcasted_iota(jnp.int32, (s_len, 128), 0)
        xc, r, i, sp, a, mult = _lru_gates(x_ref[...], cw_ref[...], cb_ref[...], wa_ref[0], ba_ref[...], wx_ref[0],
                                           bx_ref[...], lam_ref[...], row)
        h = _lru_scan(a, mult * (i * xc), row)
        y_ref[...] = h * _gelu(g_ref[...])[0]

    return _pcall(
        body, name=name, grid=(bsz, 4), in_specs=_lru_specs(s_len, lambda b, cb: (b, cb)),
        out_specs=pl.BlockSpec((s_len, 128), lambda b, cb: (b, cb)),
        out_shape=jax.ShapeDtypeStruct((t, B_WIDTH), F32), compiler_params=_cp("parallel", "parallel"),
    )(proj, proj, cw, cb.reshape(1, -1), wa, ba.reshape(1, -1), wx, bxb.reshape(1, -1), lam.reshape(1, -1))


def lru_bwd(proj, cw, cb, wa, ba, wx, bxb, lam, dcat, bsz, *, name):
    t = proj.shape[0]
    s_len = t // bsz

    def body(x_ref, g_ref, cw_ref, cb_ref, wa_ref, ba_ref, wx_ref, bx_ref, lam_ref, dy_ref,
             dx_ref, dg_ref, dcw_ref, dcb_ref, dwa_ref, dba_ref, dwx_ref, dbx_ref, dlam_ref):
        b = pl.program_id(1)
        row = lax.broadcasted_iota(jnp.int32, (s_len, 128), 0)
        x = x_ref[...]
        lam = lam_ref[...]
        xc, r, i, sp, a, mult = _lru_gates(x, cw_ref[...], cb_ref[...], wa_ref[0], ba_ref[...], wx_ref[0], bx_ref[...],
                                           lam, row)
        ixc = i * xc
        h = _lru_scan(a, mult * ixc, row)
        gv = g_ref[...]
        gl, th = _gelu(gv)
        dy = dy_ref[...]
        dg_ref[...] = dy * h * _gelu_grad(gv, th)
        gr = _lru_scan_rev(_shift_up(a, 1, row), dy * gl, row)
        da = gr * _shift_dn(h, 1, row)
        dmult = gr * ixc
        di = gr * mult * xc
        dxc = gr * mult * i
        dlog_a = da * a - dmult * (a * a) / mult
        dr = dlog_a * (-RG_C * sp)
        dlam = jnp.sum(dlog_a * r, axis=0, keepdims=True) * (RG_C * _sigmoid(-lam))
        dpa = dr * r * (1.0 - r)
        dpx = di * i * (1.0 - i)
        dxc = dxc + _mdot(dpa, wa_ref[0], NT) + _mdot(dpx, wx_ref[0], NT)
        dx, dcw = _conv_bwd(x, cw_ref[...], dxc, row)
        dx_ref[...] = dx

        @pl.when(b == 0)
        def _():
            for ref in (dcw_ref, dcb_ref, dwa_ref, dba_ref, dwx_ref, dbx_ref, dlam_ref):
                ref[...] = jnp.zeros_like(ref)

        dcw_ref[...] += dcw
        dcb_ref[...] += jnp.sum(dxc, axis=0, keepdims=True)
        dwa_ref[0] += _mdot(xc, dpa, TN)
        dwx_ref[0] += _mdot(xc, dpx, TN)
        dba_ref[...] += jnp.sum(dpa, axis=0, keepdims=True)
        dbx_ref[...] += jnp.sum(dpx, axis=0, keepdims=True)
        dlam_ref[...] += dlam

    order = lambda cb, b: (b, cb)
    act = pl.BlockSpec((s_len, 128), lambda cb, b: (b, cb))
    vec = pl.BlockSpec((1, 128), lambda cb, b: (0, cb))
    mat = pl.BlockSpec((1, 128, 128), lambda cb, b: (cb, 0, 0))
    vshape = jax.ShapeDtypeStruct((1, B_WIDTH), F32)
    mshape = jax.ShapeDtypeStruct((4, 128, 128), F32)
    return _pcall(
        body, name=name, grid=(4, bsz),
        in_specs=_lru_specs(s_len, order) + [pl.BlockSpec((s_len, 128), lambda cb, b: (b, 4 + cb))],
        out_specs=[act, act, pl.BlockSpec((4, 128), lambda cb, b: (0, cb)), vec, mat, vec, mat, vec, vec],
        out_shape=[jax.ShapeDtypeStruct((t, B_WIDTH), F32), jax.ShapeDtypeStruct((t, B_WIDTH), F32),
                   jax.ShapeDtypeStruct((4, B_WIDTH), F32), vshape, mshape, vshape, mshape, vshape, vshape],
        compiler_params=_cp("parallel", "arbitrary"),
    )(proj, proj, cw, cb.reshape(1, -1), wa, ba.reshape(1, -1), wx, bxb.reshape(1, -1), lam.reshape(1, -1), dcat)


_BDIMS = {"nn": ((2,), (1,)), "nt": ((2,), (2,)), "tn": ((1,), (1,))}
C_QSCALE = C_HEAD_DIM ** -0.5


def _bmm(a, b, mode, exact=False):
    dims = (_BDIMS[mode], ((0,), (0,)))
    if exact:
        return lax.dot_general(a, b, dims, preferred_element_type=F32, precision=lax.Precision.HIGH)
    return lax.dot_general(a.astype(MM), b.astype(MM), dims, preferred_element_type=F32)


def _col(x, idx, lane):
    return jnp.broadcast_to(jnp.sum(jnp.where(lane == idx, x, 0.0), axis=-1, keepdims=True), x.shape)


def _seg_cumsum(g, row):
    pos = row & (CHUNK - 1)
    d = 1
    while d < CHUNK:
        g = g + jnp.where(pos >= d, pltpu.roll(g, d, 0), 0.0)
        d *= 2
    return g


def _seg_cumsum_rev(g, row):
    pos = row & (CHUNK - 1)
    n = g.shape[0]
    d = 1
    while d < CHUNK:
        g = g + jnp.where(pos < CHUNK - d, pltpu.roll(g, n - d, 0), 0.0)
        d *= 2
    return g


def _gdn_prep(qr, kr, vr, gates, cwq, cwk, cwv, a_log, dtb, h):
    s_len = qr.shape[0]
    nc = s_len // CHUNK
    row = lax.broadcasted_iota(jnp.int32, (s_len, 128), 0)
    lane = lax.broadcasted_iota(jnp.int32, (s_len, 128), 1)
    r = {"row": row, "lane": lane}
    for nm, x, w in (("q", qr, cwq), ("k", kr, cwk), ("v", vr, cwv)):
        c = _conv_fwd(x, w, row)
        sg = _sigmoid(c)
        r["c" + nm], r["s" + nm], r[nm + "c"] = c, sg, c * sg
    r["rq"] = lax.rsqrt(jnp.sum(r["qc"] * r["qc"], axis=-1, keepdims=True) + NORM_EPS)
    r["rk"] = lax.rsqrt(jnp.sum(r["kc"] * r["kc"], axis=-1, keepdims=True) + NORM_EPS)
    r["qn"] = r["qc"] * r["rq"]
    r["kn"] = r["kc"] * r["rk"]
    r["beta"] = _sigmoid(_col(gates, h, lane))
    r["A"] = jnp.exp(a_log)
    r["pre"] = _col(gates, 8 + h, lane) + dtb
    r["sp"] = _softplus(r["pre"])
    gc = _seg_cumsum(-r["A"] * r["sp"], row)
    sh = (nc, CHUNK, 128)
    q3 = (r["qn"] * C_QSCALE).reshape(sh)
    k3 = r["kn"].reshape(sh)
    v3 = r["vc"].reshape(sh)
    beta3 = r["beta"].reshape(sh)
    gc3 = gc.reshape(sh)
    gcl3 = gc3[:, CHUNK - 1:CHUNK, :]
    eg = jnp.exp(gc3)
    ekd = jnp.exp(gcl3 - gc3)
    col64 = gc3[:, :, :CHUNK]
    row64 = jnp.swapaxes(gc3, 1, 2)[:, :CHUNK, :]
    ii = lax.broadcasted_iota(jnp.int32, (nc, CHUNK, CHUNK), 1)
    jj = lax.broadcasted_iota(jnp.int32, (nc, CHUNK, CHUNK), 2)
    tril = ii >= jj
    strict = ii > jj
    dm = jnp.where(tril, jnp.exp(jnp.where(tril, col64 - row64, 0.0)), 0.0)
    kb = k3 * beta3
    lmat = jnp.where(strict, _bmm(kb, k3, "nt") * dm, 0.0)
    attn = _bmm(q3, k3, "nt") * dm
    r.update(q3=q3, k3=k3, v3=v3, beta3=beta3, gc3=gc3, eg=eg, ekd=ekd, gl=jnp.exp(gcl3), dm=dm, kb=kb, lmat=lmat,
             attn=attn, strict=strict, tril=tril, qg=q3 * eg, kdec=k3 * ekd)
    return r


def _neumann_inverse(lmat):
    ii = lax.broadcasted_iota(jnp.int32, lmat.shape, 1)
    jj = lax.broadcasted_iota(jnp.int32, lmat.shape, 2)
    x = -lmat
    tm = jnp.where(ii == jj, 1.0, 0.0) + x
    pw = x
    for _ in range(5):
        pw = _bmm(pw, pw, "nn", exact=True)
        tm = tm + _bmm(tm, pw, "nn", exact=True)
    return tm


def _gdn_specs(s_len):
    act = lambda off: pl.BlockSpec((s_len, 128), lambda b, h: (b, off + h))
    cw = lambda off: pl.BlockSpec((4, 128), lambda b, h: (0, off + h))
    smem = pl.BlockSpec(memory_space=pltpu.SMEM)
    return [act(0), act(8), act(16), act(24), pl.BlockSpec((s_len, 128), lambda b, h: (b, 0)), cw(0), cw(8), cw(16),
            smem, smem, pl.BlockSpec((1, 128), lambda b, h: (0, 0))]


def gdn_fwd(proj, gates, cw, a_log, dtb, ng, bsz, *, name):
    t = proj.shape[0]
    s_len = t // bsz
    nc = s_len // CHUNK

    def body(q_ref, k_ref, v_ref, z_ref, gt_ref, cwq_ref, cwk_ref, cwv_ref, al_ref, dt_ref, ng_ref,
             out_ref, o_ref, vn_ref, tm_ref, st_ref, u_s, w_s, qg_s, kd_s, at_s, gl_s):
        h = pl.program_id(1)
        r = _gdn_prep(q_ref[...], k_ref[...], v_ref[...], gt_ref[...], cwq_ref[...], cwk_ref[...], cwv_ref[...],
                      al_ref[h], dt_ref[h], h)
        tm = _neumann_inverse(r["lmat"])
        tm_ref[0, 0] = tm
        u_s[...] = _bmm(tm, r["v3"] * r["beta3"], "nn", exact=True)
        w_s[...] = _bmm(tm, r["kb"] * r["eg"], "nn", exact=True)
        qg_s[...] = r["qg"]
        kd_s[...] = r["kdec"]
        at_s[...] = r["attn"]
        gl_s[...] = r["gl"]

        def chunk(n, state):
            st = pl.multiple_of(n * CHUNK, CHUNK)
            st_ref[0, 0, n] = state
            v_new = u_s[n] - _mdot(w_s[n], state)
            o_ref[pl.ds(st, CHUNK), :] = _mdot(qg_s[n], state) + _mdot(at_s[n], v_new)
            vn_ref[pl.ds(st, CHUNK), :] = v_new
            return state * gl_s[n] + _mdot(kd_s[n], v_new, TN)

        lax.fori_loop(0, nc, chunk, jnp.zeros((128, 128), F32))
        o = o_ref[...]
        rms = lax.rsqrt(jnp.mean(o * o, axis=-1, keepdims=True) + NORM_EPS)
        z = z_ref[...]
        out_ref[...] = o * rms * ng_ref[...] * (z * _sigmoid(z))

    blk = pl.BlockSpec((s_len, 128), lambda b, h: (b, h))
    full = jax.ShapeDtypeStruct((t, C_WIDTH), F32)
    return _pcall(
        body, name=name, grid=(bsz, C_HEADS), in_specs=_gdn_specs(s_len),
        out_specs=[blk, blk, blk, pl.BlockSpec((1, 1, nc, CHUNK, CHUNK), lambda b, h: (b, h, 0, 0, 0)),
                   pl.BlockSpec((1, 1, nc, 128, 128), lambda b, h: (b, h, 0, 0, 0))],
        out_shape=[full, full, full, jax.ShapeDtypeStruct((bsz, C_HEADS, nc, CHUNK, CHUNK), F32),
                   jax.ShapeDtypeStruct((bsz, C_HEADS, nc, 128, 128), F32)],
        scratch_shapes=[pltpu.VMEM((nc, CHUNK, 128), F32)] * 4 + [pltpu.VMEM((nc, CHUNK, CHUNK), F32),
                                                                   pltpu.VMEM((nc, 1, 128), F32)],
        compiler_params=_cp("parallel", "parallel"),
    )(proj, proj, proj, proj, gates, cw, cw, cw, a_log, dtb, ng.reshape(1, 128))


def gdn_bwd(proj, gates, cw, a_log, dtb, ng, o_pre, vnew, tmat, states, dout, bsz, *, name):
    t = proj.shape[0]
    s_len = t // bsz
    nc = s_len // CHUNK

    def body(q_ref, k_ref, v_ref, z_ref, gt_ref, cwq_ref, cwk_ref, cwv_ref, al_ref, dt_ref, ng_ref,
             o_ref, vn_ref, tm_ref, st_ref, do_ref,
             dq_ref, dk_ref, dv_ref, dz_ref, dgt_ref, dcq_ref, dck_ref, dcv_ref, dsm_ref,
             w_s, qg_s, kd_s, at_s, gl_s, dop_s, du_s, dw_s, dat_s, dqg_s, dkd_s, dgl_s):
        h = pl.program_id(1)
        qr, kr, vr = q_ref[...], k_ref[...], v_ref[...]
        r = _gdn_prep(qr, kr, vr, gt_ref[...], cwq_ref[...], cwk_ref[...], cwv_ref[...], al_ref[h], dt_ref[h], h)
        row, lane = r["row"], r["lane"]
        tm = tm_ref[0, 0]
        q3, k3, v3, beta3, eg, kb, dm = r["q3"], r["k3"], r["v3"], r["beta3"], r["eg"], r["kb"], r["dm"]
        u3 = _bmm(tm, v3 * beta3, "nn", exact=True)
        w3 = _bmm(tm, kb * eg, "nn", exact=True)
        w_s[...] = w3
        qg_s[...] = r["qg"]
        kd_s[...] = r["kdec"]
        at_s[...] = r["attn"]
        gl_s[...] = r["gl"]

        z = z_ref[...]
        sz = _sigmoid(z)
        o = o_ref[...]
        rms = lax.rsqrt(jnp.mean(o * o, axis=-1, keepdims=True) + NORM_EPS)
        on = o * rms
        dout_v = do_ref[...]
        ngv = ng_ref[...]
        dz_ref[...] = dout_v * on * ngv * (sz * (1.0 + z * (1.0 - sz)))
        dos = dout_v * (z * sz)
        dng = jnp.sum(dos * on, axis=0, keepdims=True)
        don = dos * ngv
        dop_s[...] = (rms * (don - on * jnp.mean(don * on, axis=-1, keepdims=True))).reshape(nc, CHUNK, 128)

        def chunk(i, dstate):
            n = nc - 1 - i
            st = pl.multiple_of(n * CHUNK, CHUNK)
            state = st_ref[0, 0, n]
            vn = vn_ref[pl.ds(st, CHUNK), :]
            do_n = dop_s[n]
            dvn = _mdot(at_s[n], do_n, TN) + _mdot(kd_s[n], dstate)
            du_s[n] = dvn
            dat_s[n] = _mdot(do_n, vn, NT)
            dqg_s[n] = _mdot(do_n, state, NT)
            dkd_s[n] = _mdot(vn, dstate, NT)
            dgl_s[n] = jnp.broadcast_to(jnp.sum(jnp.sum(state * dstate, axis=1, keepdims=True), axis=0, keepdims=True), (1, 128))
            dw_s[n] = -_mdot(dvn, state, NT)
            return dstate * gl_s[n] + _mdot(qg_s[n], do_n, TN) - _mdot(w_s[n], dvn, TN)

        lax.fori_loop(0, nc, chunk, jnp.zeros((128, 128), F32))

        du, dw, dqg, dkd = du_s[...], dw_s[...], dqg_s[...], dkd_s[...]
        dat = jnp.where(r["tril"], dat_s[...], 0.0)
        dvb = _bmm(tm, du, "tn", exact=True)
        dkbg = _bmm(tm, dw, "tn", exact=True)
        dl = -jnp.where(r["strict"], _bmm(dvb, u3, "nt") + _bmm(dkbg, w3, "nt"), 0.0)
        dml = dl * dm
        dn = dat * dm
        dkb = _bmm(dml, k3, "nn") + dkbg * eg
        dk3 = _bmm(dml, kb, "tn") + _bmm(dn, q3, "tn") + dkd * r["ekd"] + dkb * beta3
        dq3 = dqg * eg + _bmm(dn, k3, "nn")
        e = dl * r["lmat"] + dat * r["attn"]
        ones = jnp.ones((nc, CHUNK, 128), F32)
        colsum = lax.dot_general(e, ones, (_BDIMS["tn"], ((0,), (0,))), preferred_element_type=F32, precision=HI)
        dgc = jnp.sum(e, axis=-1, keepdims=True) - colsum
        dgc = dgc + eg * (jnp.sum(dqg * q3, axis=-1, keepdims=True) + jnp.sum(dkbg * kb, axis=-1, keepdims=True))
        skd = jnp.sum(dkd * r["kdec"], axis=-1, keepdims=True)
        dgcl = jnp.sum(skd, axis=1, keepdims=True) + dgl_s[...] * r["gl"]
        pos3 = lax.broadcasted_iota(jnp.int32, (nc, CHUNK, 128), 1)
        dgc = dgc - skd + jnp.where(pos3 == CHUNK - 1, dgcl, 0.0)
        dbeta = jnp.sum(dkb * k3, axis=-1, keepdims=True) + jnp.sum(dvb * v3, axis=-1, keepdims=True)
        dv3 = dvb * beta3

        dg = _seg_cumsum_rev(dgc.reshape(s_len, 128), row)
        beta = r["beta"]
        dbl = jnp.broadcast_to(dbeta, (nc, CHUNK, 128)).reshape(s_len, 128) * beta * (1.0 - beta)
        dai = dg * (-r["A"]) * _sigmoid(r["pre"])
        d_dtb = jnp.sum(dai, axis=0, keepdims=True)
        d_alog = jnp.sum(dg * (-r["sp"]), axis=0, keepdims=True) * r["A"]

        @pl.when(h == 0)
        def _():
            dgt_ref[...] = jnp.zeros_like(dgt_ref)
            dsm_ref[...] = jnp.zeros_like(dsm_ref)

        dgt_ref[...] += jnp.where(lane == h, dbl, 0.0) + jnp.where(lane == 8 + h, dai, 0.0)
        r16 = lax.broadcasted_iota(jnp.int32, (16, 128), 0)
        l16 = lax.broadcasted_iota(jnp.int32, (16, 128), 1)
        small = jnp.where((r16 == h) & (l16 == 0), d_alog, 0.0) + jnp.where((r16 == h) & (l16 == 1), d_dtb, 0.0)
        dsm_ref[0] += small + jnp.where(r16 == 8 + h, dng, 0.0)

        dqn = dq3.reshape(s_len, 128) * C_QSCALE
        dkn = dk3.reshape(s_len, 128)
        dqc = r["rq"] * (dqn - r["qn"] * jnp.sum(dqn * r["qn"], axis=-1, keepdims=True))
        dkc = r["rk"] * (dkn - r["kn"] * jnp.sum(dkn * r["kn"], axis=-1, keepdims=True))
        dvc = dv3.reshape(s_len, 128)
        for nm, x, w_ref, dxc, dx_ref, dc_ref in (("q", qr, cwq_ref, dqc, dq_ref, dcq_ref), ("k", kr, cwk_ref, dkc, dk_ref, dck_ref),
                                                 ("v", vr, cwv_ref, dvc, dv_ref, dcv_ref)):
            c, sg = r["c" + nm], r["s" + nm]
            dc = dxc * (sg * (1.0 + c * (1.0 - sg)))
            dx, dwc = _conv_bwd(x, w_ref[...], dc, row)
            dx_ref[...] = dx
            dc_ref[0] = dwc

    blk = pl.BlockSpec((s_len, 128), lambda b, h: (b, h))
    full = jax.ShapeDtypeStruct((t, C_WIDTH), F32)
    cwo = pl.BlockSpec((1, 4, 128), lambda b, h: (b, 0, h))
    cws = jax.ShapeDtypeStruct((bsz, 4, C_WIDTH), F32)
    c128 = pltpu.VMEM((nc, CHUNK, 128), F32)
    outs = _pcall(
        body, name=name, grid=(bsz, C_HEADS),
        in_specs=_gdn_specs(s_len) + [blk, blk, pl.BlockSpec((1, 1, nc, CHUNK, CHUNK), lambda b, h: (b, h, 0, 0, 0)),
                                      pl.BlockSpec((1, 1, nc, 128, 128), lambda b, h: (b, h, 0, 0, 0)), blk],
        out_specs=[blk, blk, blk, blk, pl.BlockSpec((s_len, 128), lambda b, h: (b, 0)), cwo, cwo, cwo,
                   pl.BlockSpec((1, 16, 128), lambda b, h: (b, 0, 0))],
        out_shape=[full, full, full, full, jax.ShapeDtypeStruct((t, 128), F32), cws, cws, cws,
                   jax.ShapeDtypeStruct((bsz, 16, 128), F32)],
        scratch_shapes=[c128, c128, c128, pltpu.VMEM((nc, CHUNK, CHUNK), F32), pltpu.VMEM((nc, 1, 128), F32), c128,
                        c128, c128, pltpu.VMEM((nc, CHUNK, CHUNK), F32), c128, c128, pltpu.VMEM((nc, 1, 128), F32)],
        compiler_params=_cp("parallel", "arbitrary"),
    )(proj, proj, proj, proj, gates, cw, cw, cw, a_log, dtb, ng.reshape(1, 128), o_pre, vnew, tmat, states, dout)
    dq, dk, dv, dz, dgates, dcq, dck, dcv, dsm = outs
    return dq, dk, dv, dz, dgates, jnp.concatenate([dcq, dck, dcv], axis=-1), dsm


def gdc_pre_fwd(proj, cw, bsz, *, name):
    t = proj.shape[0]
    s_len = t // bsz

    def body(x_ref, w_ref, y_ref):
        row = lax.broadcasted_iota(jnp.int32, (s_len, 128), 0)
        c = _conv_fwd(x_ref[...], w_ref[...], row)
        xc = c * _sigmoid(c)
        rn = lax.rsqrt(jnp.sum(xc * xc, axis=-1, keepdims=True) + NORM_EPS)
        y_ref[...] = jnp.where(pl.program_id(1) < 2 * C_HEADS, xc * rn, xc)

    blk = pl.BlockSpec((s_len, 128), lambda b, j: (b, j))
    return _pcall(
        body, name=name, grid=(bsz, 3 * C_HEADS), in_specs=[blk, pl.BlockSpec((4, 128), lambda b, j: (0, j))],
        out_specs=blk, out_shape=jax.ShapeDtypeStruct((t, 3 * C_WIDTH), F32), compiler_params=_cp("parallel", "parallel"),
    )(proj, cw)


def gdc_pre_bwd(proj, cw, dy, bsz, *, name):
    t = proj.shape[0]
    s_len = t // bsz

    def body(x_ref, w_ref, dy_ref, dx_ref, dw_ref):
        row = lax.broadcasted_iota(jnp.int32, (s_len, 128), 0)
        x = x_ref[...]
        c = _conv_fwd(x, w_ref[...], row)
        sg = _sigmoid(c)
        xc = c * sg
        rn = lax.rsqrt(jnp.sum(xc * xc, axis=-1, keepdims=True) + NORM_EPS)
        dyv = dy_ref[...]
        xn = xc * rn
        dxc = jnp.where(pl.program_id(1) < 2 * C_HEADS, rn * (dyv - xn * jnp.sum(dyv * xn, axis=-1, keepdims=True)), dyv)
        dc = dxc * (sg * (1.0 + c * (1.0 - sg)))
        dx, dw = _conv_bwd(x, w_ref[...], dc, row)
        dx_ref[...] = dx
        dw_ref[0] = dw

    blk = pl.BlockSpec((s_len, 128), lambda b, j: (b, j))
    return _pcall(
        body, name=name, grid=(bsz, 3 * C_HEADS), in_specs=[blk, pl.BlockSpec((4, 128), lambda b, j: (0, j)), blk],
        out_specs=[blk, pl.BlockSpec((1, 4, 128), lambda b, j: (b, 0, j))],
        out_shape=[jax.ShapeDtypeStruct((t, 3 * C_WIDTH), F32), jax.ShapeDtypeStruct((bsz, 4, 3 * C_WIDTH), F32)],
        compiler_params=_cp("parallel", "parallel"),
    )(proj, cw, dy)


GDC_GROUP = 8


def _gdc_local(qn, kn, vc, gates, a_log, dtb, h):
    rows = qn.shape[0]
    nc = rows // CHUNK
    row = lax.broadcasted_iota(jnp.int32, (rows, 128), 0)
    lane = lax.broadcasted_iota(jnp.int32, (rows, 128), 1)
    r = {"row": row, "lane": lane}
    r["beta"] = _sigmoid(_col(gates, h, lane))
    r["A"] = jnp.exp(a_log)
    r["pre"] = _col(gates, 8 + h, lane) + dtb
    r["sp"] = _softplus(r["pre"])
    gc = _seg_cumsum(-r["A"] * r["sp"], row)
    sh = (nc, CHUNK, 128)
    q3 = (qn * C_QSCALE).reshape(sh)
    k3 = kn.reshape(sh)
    v3 = vc.reshape(sh)
    beta3 = r["beta"].reshape(sh)
    gc3 = gc.reshape(sh)
    gcl3 = gc3[:, CHUNK - 1:CHUNK, :]
    eg = jnp.exp(gc3)
    ekd = jnp.exp(gcl3 - gc3)
    col64 = gc3[:, :, :CHUNK]
    row64 = jnp.swapaxes(gc3, 1, 2)[:, :CHUNK, :]
    ii = lax.broadcasted_iota(jnp.int32, (nc, CHUNK, CHUNK), 1)
    jj = lax.broadcasted_iota(jnp.int32, (nc, CHUNK, CHUNK), 2)
    tril = ii >= jj
    strict = ii > jj
    dm = jnp.where(tril, jnp.exp(jnp.where(tril, col64 - row64, 0.0)), 0.0)
    kb = k3 * beta3
    lmat = jnp.where(strict, _bmm(kb, k3, "nt") * dm, 0.0)
    attn = _bmm(q3, k3, "nt") * dm
    r.update(q3=q3, k3=k3, v3=v3, beta3=beta3, eg=eg, ekd=ekd, gl=jnp.exp(gcl3), dm=dm, kb=kb, lmat=lmat,
             attn=attn, strict=strict, tril=tril, qg=q3 * eg, kdec=k3 * ekd)
    return r


def _gdc_specs(s_len):
    act = lambda off: pl.BlockSpec((s_len, 128), lambda b, h: (b, off + h))
    smem = pl.BlockSpec(memory_space=pltpu.SMEM)
    return [act(0), act(8), act(16), act(24), pl.BlockSpec((s_len, 128), lambda b, h: (b, 0)), smem, smem,
            pl.BlockSpec((1, 128), lambda b, h: (0, 0))]


def gdc_fwd(qkv, proj, gates, a_log, dtb, ng, bsz, *, name):
    t = proj.shape[0]
    s_len = t // bsz
    nc = s_len // CHUNK
    grp = min(GDC_GROUP, nc)
    gr = grp * CHUNK

    def body(q_ref, k_ref, v_ref, z_ref, gt_ref, al_ref, dt_ref, ng_ref,
             out_ref, o_ref, vn_ref, tm_ref, st_ref, u_s, w_s, qg_s, kd_s, at_s, gl_s):
        h = pl.program_id(1)

        def local(gi, carry):
            rs = pl.ds(pl.multiple_of(gi * gr, gr), gr)
            cs = pl.ds(gi * grp, grp)
            r = _gdc_local(q_ref[rs, :], k_ref[rs, :], v_ref[rs, :], gt_ref[rs, :], al_ref[h], dt_ref[h], h)
            tm = _neumann_inverse(r["lmat"])
            tm_ref[0, 0, cs] = tm
            u_s[cs] = _bmm(tm, r["v3"] * r["beta3"], "nn", exact=True)
            w_s[cs] = _bmm(tm, r["kb"] * r["eg"], "nn", exact=True)
            qg_s[cs] = r["qg"]
            kd_s[cs] = r["kdec"]
            at_s[cs] = r["attn"]
            gl_s[cs] = r["gl"]
            return carry

        lax.fori_loop(0, nc // grp, local, 0)

        def chunk(n, state):
            st = pl.multiple_of(n * CHUNK, CHUNK)
            st_ref[0, 0, n] = state
            v_new = u_s[n] - _mdot(w_s[n], state)
            o_ref[pl.ds(st, CHUNK), :] = _mdot(qg_s[n], state) + _mdot(at_s[n], v_new)
            vn_ref[pl.ds(st, CHUNK), :] = v_new
            return state * gl_s[n] + _mdot(kd_s[n], v_new, TN)

        lax.fori_loop(0, nc, chunk, jnp.zeros((128, 128), F32))
        o = o_ref[...]
        rms = lax.rsqrt(jnp.mean(o * o, axis=-1, keepdims=True) + NORM_EPS)
        z = z_ref[...]
        out_ref[...] = o * rms * ng_ref[...] * (z * _sigmoid(z))

    blk = pl.BlockSpec((s_len, 128), lambda b, h: (b, h))
    full = jax.ShapeDtypeStruct((t, C_WIDTH), F32)
    return _pcall(
        body, name=name, grid=(bsz, C_HEADS), in_specs=_gdc_specs(s_len),
        out_specs=[blk, blk, blk, pl.BlockSpec((1, 1, nc, CHUNK, CHUNK), lambda b, h: (b, h, 0, 0, 0)),
                   pl.BlockSpec((1, 1, nc, 128, 128), lambda b, h: (b, h, 0, 0, 0))],
        out_shape=[full, full, full, jax.ShapeDtypeStruct((bsz, C_HEADS, nc, CHUNK, CHUNK), F32),
                   jax.ShapeDtypeStruct((bsz, C_HEADS, nc, 128, 128), F32)],
        scratch_shapes=[pltpu.VMEM((nc, CHUNK, 128), F32)] * 4 + [pltpu.VMEM((nc, CHUNK, CHUNK), F32),
                                                                   pltpu.VMEM((nc, 1, 128), F32)],
        compiler_params=_cp("parallel", "parallel"),
    )(qkv, qkv, qkv, proj, gates, a_log, dtb, ng.reshape(1, 128))


def gdc_bwd(qkv, proj, gates, a_log, dtb, ng, o_pre, vnew, tmat, states, dout, bsz, *, name, side=None):
    t = proj.shape[0]
    s_len = t // bsz
    nc = s_len // CHUNK
    grp = min(GDC_GROUP, nc)
    gr = grp * CHUNK

    def body(q_ref, k_ref, v_ref, z_ref, gt_ref, al_ref, dt_ref, ng_ref, o_ref, vn_ref, tm_ref, st_ref, do_ref,
             dq_ref, dk_ref, dv_ref, dz_ref, dgt_ref, dsm_ref,
             w_s, qg_s, kd_s, at_s, gl_s, dop_s, du_s, dw_s, dat_s, dqg_s, dkd_s, dgl_s):
        h = pl.program_id(1)
        a_log_h, dtb_h = al_ref[h], dt_ref[h]

        z = z_ref[...]
        sz = _sigmoid(z)
        o = o_ref[...]
        rms = lax.rsqrt(jnp.mean(o * o, axis=-1, keepdims=True) + NORM_EPS)
        on = o * rms
        dout_v = do_ref[...]
        ngv = ng_ref[...]
        dz_ref[...] = dout_v * on * ngv * (sz * (1.0 + z * (1.0 - sz)))
        dos = dout_v * (z * sz)
        dng = jnp.sum(dos * on, axis=0, keepdims=True)
        don = dos * ngv
        dop_s[...] = (rms * (don - on * jnp.mean(don * on, axis=-1, keepdims=True))).reshape(nc, CHUNK, 128)

        def local(gi, carry):
            rs = pl.ds(pl.multiple_of(gi * gr, gr), gr)
            cs = pl.ds(gi * grp, grp)
            r = _gdc_local(q_ref[rs, :], k_ref[rs, :], v_ref[rs, :], gt_ref[rs, :], a_log_h, dtb_h, h)
            w_s[cs] = _bmm(tm_ref[0, 0, cs], r["kb"] * r["eg"], "nn", exact=True)
            qg_s[cs] = r["qg"]
            kd_s[cs] = r["kdec"]
            at_s[cs] = r["attn"]
            gl_s[cs] = r["gl"]
            return carry

        lax.fori_loop(0, nc // grp, local, 0)

        def chunk(i, dstate):
            n = nc - 1 - i
            st = pl.multiple_of(n * CHUNK, CHUNK)
            state = st_ref[0, 0, n]
            vn = vn_ref[pl.ds(st, CHUNK), :]
            do_n = dop_s[n]
            dvn = _mdot(at_s[n], do_n, TN) + _mdot(kd_s[n], dstate)
            du_s[n] = dvn
            dat_s[n] = _mdot(do_n, vn, NT)
            dqg_s[n] = _mdot(do_n, state, NT)
            dkd_s[n] = _mdot(vn, dstate, NT)
            dgl_s[n] = jnp.broadcast_to(jnp.sum(jnp.sum(state * dstate, axis=1, keepdims=True), axis=0, keepdims=True), (1, 128))
            dw_s[n] = -_mdot(dvn, state, NT)
            return dstate * gl_s[n] + _mdot(qg_s[n], do_n, TN) - _mdot(w_s[n], dvn, TN)

        lax.fori_loop(0, nc, chunk, jnp.zeros((128, 128), F32))

        @pl.when(h == 0)
        def _():
            dgt_ref[...] = jnp.zeros_like(dgt_ref)
            dsm_ref[...] = jnp.zeros_like(dsm_ref)

        def local_bwd(gi, carry):
            d_alog, d_dtb = carry
            rs = pl.ds(pl.multiple_of(gi * gr, gr), gr)
            cs = pl.ds(gi * grp, grp)
            r = _gdc_local(q_ref[rs, :], k_ref[rs, :], v_ref[rs, :], gt_ref[rs, :], a_log_h, dtb_h, h)
            row, lane = r["row"], r["lane"]
            q3, k3, v3, beta3, eg, kb, dm = r["q3"], r["k3"], r["v3"], r["beta3"], r["eg"], r["kb"], r["dm"]
            tm = tm_ref[0, 0, cs]
            u3 = _bmm(tm, v3 * beta3, "nn", exact=True)
            w3 = w_s[cs]
            du, dw, dqg, dkd = du_s[cs], dw_s[cs], dqg_s[cs], dkd_s[cs]
            dat = jnp.where(r["tril"], dat_s[cs], 0.0)
            dvb = _bmm(tm, du, "tn", exact=True)
            dkbg = _bmm(tm, dw, "tn", exact=True)
            dl = -jnp.where(r["strict"], _bmm(dvb, u3, "nt") + _bmm(dkbg, w3, "nt"), 0.0)
            dml = dl * dm
            dn = dat * dm
            dkb = _bmm(dml, k3, "nn") + dkbg * eg
            dk3 = _bmm(dml, kb, "tn") + _bmm(dn, q3, "tn") + dkd * r["ekd"] + dkb * beta3
            dq3 = dqg * eg + _bmm(dn, k3, "nn")
            e = dl * r["lmat"] + dat * r["attn"]
            ones = jnp.ones((grp, CHUNK, 128), F32)
            colsum = lax.dot_general(e, ones, (_BDIMS["tn"], ((0,), (0,))), preferred_element_type=F32, precision=HI)
            dgc = jnp.sum(e, axis=-1, keepdims=True) - colsum
            dgc = dgc + eg * (jnp.sum(dqg * q3, axis=-1, keepdims=True) + jnp.sum(dkbg * kb, axis=-1, keepdims=True))
            skd = jnp.sum(dkd * r["kdec"], axis=-1, keepdims=True)
            dgcl = jnp.sum(skd, axis=1, keepdims=True) + dgl_s[cs] * r["gl"]
            pos3 = lax.broadcasted_iota(jnp.int32, (grp, CHUNK, 128), 1)
            dgc = dgc - skd + jnp.where(pos3 == CHUNK - 1, dgcl, 0.0)
            dbeta = jnp.sum(dkb * k3, axis=-1, keepdims=True) + jnp.sum(dvb * v3, axis=-1, keepdims=True)
            dg = _seg_cumsum_rev(dgc.reshape(gr, 128), row)
            beta = r["beta"]
            dbl = jnp.broadcast_to(dbeta, (grp, CHUNK, 128)).reshape(gr, 128) * beta * (1.0 - beta)
            dai = dg * (-r["A"]) * _sigmoid(r["pre"])
            dgt_ref[rs, :] += jnp.where(lane == h, dbl, 0.0) + jnp.where(lane == 8 + h, dai, 0.0)
            dq_ref[rs, :] = dq3.reshape(gr, 128) * C_QSCALE
            dk_ref[rs, :] = dk3.reshape(gr, 128)
            dv_ref[rs, :] = (dvb * beta3).reshape(gr, 128)
            return (d_alog + jnp.sum(dg * (-r["sp"]), axis=0, keepdims=True) * r["A"],
                    d_dtb + jnp.sum(dai, axis=0, keepdims=True))

        zero = jnp.zeros((1, 128), F32)
        d_alog, d_dtb = lax.fori_loop(0, nc // grp, local_bwd, (zero, zero))
        r16 = lax.broadcasted_iota(jnp.int32, (16, 128), 0)
        l16 = lax.broadcasted_iota(jnp.int32, (16, 128), 1)
        small = jnp.where((r16 == h) & (l16 == 0), d_alog, 0.0) + jnp.where((r16 == h) & (l16 == 1), d_dtb, 0.0)
        dsm_ref[0] += small + jnp.where(r16 == 8 + h, dng, 0.0)

    blk = pl.BlockSpec((s_len, 128), lambda b, h: (b, h))
    blk3 = lambda off: pl.BlockSpec((s_len, 128), lambda b, h: (b, off + h))
    full = jax.ShapeDtypeStruct((t, C_WIDTH), F32)
    c128 = pltpu.VMEM((nc, CHUNK, 128), F32)
    res = _call(
        body, (qkv, qkv, qkv, proj, gates, a_log, dtb, ng.reshape(1, 128), o_pre, vnew, tmat, states, dout), side,
        (bsz, C_HEADS), name=name,
        in_specs=_gdc_specs(s_len) + [blk, blk, pl.BlockSpec((1, 1, nc, CHUNK, CHUNK), lambda b, h: (b, h, 0, 0, 0)),
                                      pl.BlockSpec((1, 1, nc, 128, 128), lambda b, h: (b, h, 0, 0, 0)), blk],
        out_specs=[blk, blk, blk, blk, pl.BlockSpec((s_len, 128), lambda b, h: (b, 0)),
                   pl.BlockSpec((1, 16, 128), lambda b, h: (b, 0, 0))],
        out_shape=[full, full, full, full, jax.ShapeDtypeStruct((t, 128), F32), jax.ShapeDtypeStruct((bsz, 16, 128), F32)],
        scratch_shapes=[c128, c128, c128, pltpu.VMEM((nc, CHUNK, CHUNK), F32), pltpu.VMEM((nc, 1, 128), F32), c128,
                        c128, c128, pltpu.VMEM((nc, CHUNK, CHUNK), F32), c128, c128, pltpu.VMEM((nc, 1, 128), F32)],
        compiler_params=_cp("parallel", "arbitrary"),
    )
    (dq, dk, dv, dz, dgates, dsm), extra = res if side is not None else (res, None)
    out = (jnp.concatenate([dq, dk, dv], axis=-1), dz, dgates, dsm)
    return out if side is None else (out, extra)


MESH_ID = pl.DeviceIdType.MESH
_FLIPS = [(0, 0, 1), (1, 0, 0), (0, 1, 0), (1, 1, 0), (1, 0, 1), (0, 1, 1), (1, 1, 1)]


def _me():
    return lax.axis_index("x"), lax.axis_index("y"), lax.axis_index("c")


def _flip(coord, d):
    return 1 - coord if d else coord


def all_gather(shard, *, name):
    def body(x_ref, o_ref, send_sems, recv_sems, local_sem):
        x, y, c = _me()
        mine = 4 * x + 2 * y + c
        own = pltpu.make_async_copy(x_ref, o_ref.at[mine], local_sem)
        own.start()
        copies = []
        for k, (dx, dy, dc) in enumerate(_FLIPS):
            cp = pltpu.make_async_remote_copy(
                src_ref=x_ref, dst_ref=o_ref.at[mine], send_sem=send_sems.at[k], recv_sem=recv_sems.at[k],
                device_id=(_flip(x, dx), _flip(y, dy), _flip(c, dc)), device_id_type=MESH_ID)
            cp.start()
            copies.append(cp)
        for cp in copies:
            cp.wait()
        own.wait()

    hbm = pl.BlockSpec(memory_space=pl.ANY)
    return _pcall(
        body, name=name, in_specs=[hbm], out_specs=hbm,
        out_shape=jax.ShapeDtypeStruct((N_DEV,) + shard.shape, shard.dtype),
        scratch_shapes=[pltpu.SemaphoreType.DMA((7,)), pltpu.SemaphoreType.DMA((7,)), pltpu.SemaphoreType.DMA(())],
    )(shard)


def all_to_all(parts, *, name):
    def body(x_ref, o_ref, send_sems, recv_sems, local_sem):
        x, y, c = _me()
        mine = 4 * x + 2 * y + c
        own = pltpu.make_async_copy(x_ref.at[mine], o_ref.at[mine], local_sem)
        own.start()
        copies = []
        for k, (dx, dy, dc) in enumerate(_FLIPS):
            px, py, pc = _flip(x, dx), _flip(y, dy), _flip(c, dc)
            cp = pltpu.make_async_remote_copy(
                src_ref=x_ref.at[4 * px + 2 * py + pc], dst_ref=o_ref.at[mine], send_sem=send_sems.at[k],
                recv_sem=recv_sems.at[k], device_id=(px, py, pc), device_id_type=MESH_ID)
            cp.start()
            copies.append(cp)
        for cp in copies:
            cp.wait()
        own.wait()

    hbm = pl.BlockSpec(memory_space=pl.ANY)
    return _pcall(
        body, name=name, in_specs=[hbm], out_specs=hbm, out_shape=jax.ShapeDtypeStruct(parts.shape, parts.dtype),
        scratch_shapes=[pltpu.SemaphoreType.DMA((7,)), pltpu.SemaphoreType.DMA((7,)), pltpu.SemaphoreType.DMA(())],
    )(parts)


def adamw_sum(parts, w, m, v, *, name, tr=256):
    r, cdim = w.shape
    tr = _tile8(r, tr)

    def body(p_ref, w_ref, m_ref, v_ref, g_ref, d_ref, mo_ref, vo_ref):
        g = p_ref[0].astype(F32)
        for j in range(1, N_DEV):
            g = g + p_ref[j].astype(F32)
        g_ref[...] = g
        mn = ADAM_B1 * m_ref[...] + (1.0 - ADAM_B1) * g
        vn = ADAM_B2 * v_ref[...] + (1.0 - ADAM_B2) * (g * g)
        mo_ref[...] = mn
        vo_ref[...] = vn
        m_hat = mn / (1.0 - ADAM_B1 ** ADAM_STEP)
        v_hat = vn / (1.0 - ADAM_B2 ** ADAM_STEP)
        d_ref[...] = -ADAM_LR * (m_hat / (jnp.sqrt(v_hat) + ADAM_EPS) + ADAM_WD * w_ref[...])

    blk = pl.BlockSpec((tr, cdim), lambda i: (i, 0))
    shp = jax.ShapeDtypeStruct((r, cdim), F32)
    return _pcall(
        body, name=name, grid=(r // tr,), in_specs=[pl.BlockSpec((N_DEV, tr, cdim), lambda i: (0, i, 0)), blk, blk, blk],
        out_specs=[blk, blk, blk, blk], out_shape=[shp, shp, shp, shp], compiler_params=_cp("parallel"),
    )(parts, w, m, v)


def _tile8(n, pref):
    for c in range(min(pref, n) - min(pref, n) % 16, 0, -16):
        if n % c == 0:
            return c
    return n


BIG = [("ffn1_wg", 2), ("ffn1_wu", 2), ("ffn1_wd", 1), ("ffn2_wg", 2), ("ffn2_wu", 2), ("ffn2_wd", 1), ("ple_wg", 1),
       ("ple_wp", 2), ("ab_w_in", 2), ("ab_w_out", 1), ("c_w_in", 2), ("c_w_out", 1)]
SMALL = [("ln_g", 2), ("ln_b", 2), ("b_conv_w", 2), ("c_conv_w", 2)]
REPL = ["ple_bg", "a_sinks", "b_conv_b", "b_wa", "b_ba", "b_wx", "b_bx", "b_lam", "c_a_log", "c_dt_bias", "c_norm_g"]
WEIGHTS = ["ffn1_wg", "ffn1_wu", "ffn1_wd", "ffn2_wg", "ffn2_wu", "ffn2_wd", "ln_g", "ln_b", "ple_wg", "ple_bg", "ple_wp",
           "ab_w_in", "a_sinks", "b_conv_w", "b_conv_b", "b_wa", "b_ba", "b_wx", "b_bx", "b_lam", "ab_w_out", "c_w_in",
           "c_conv_w", "c_a_log", "c_dt_bias", "c_norm_g", "c_w_out"]
PACK_COLS = 1024
PACK_ALIGN = 16 * PACK_COLS


def _as_bf16_bits(a):
    return lax.bitcast_convert_type(a, jnp.bfloat16).reshape(a.shape[:-1] + (2 * a.shape[-1],))


def _from_bf16_bits(a):
    return lax.bitcast_convert_type(a.reshape(a.shape[:-1] + (a.shape[-1] // 2, 2)), F32)


def _pad_rows(flat, align=PACK_ALIGN):
    n = flat.shape[-1]
    total = -(-n // align) * align
    flat = jnp.pad(flat, [(0, 0)] * (flat.ndim - 1) + [(0, total - n)])
    return flat.reshape(flat.shape[:-1] + (total // PACK_COLS, PACK_COLS))


def _join(blocks, axis):
    moved = jnp.moveaxis(blocks, 0, axis)
    shp = list(moved.shape)
    return moved.reshape(shp[:axis] + [shp[axis] * shp[axis + 1]] + shp[axis + 2:])


def _split(full, axis):
    shp = list(full.shape)
    return jnp.moveaxis(full.reshape(shp[:axis] + [N_DEV, shp[axis] // N_DEV] + shp[axis + 1:]), axis, 0)


def _dense_blocks(w):
    z = jnp.zeros((4, 2, 64, 2, 64), w.dtype)
    w4 = w.reshape(4, 2, 64, 64)
    z = z.at[:, 0, :, 0, :].set(w4[:, 0]).at[:, 1, :, 1, :].set(w4[:, 1])
    return z.reshape(4, 128, 128)


def _diag_blocks(d):
    d5 = d.reshape(4, 2, 64, 2, 64)
    return jnp.stack([d5[:, 0, :, 0, :], d5[:, 1, :, 1, :]], axis=1).reshape(8, 64, 64)


def kernel(x, p, ffn1_wg, ffn1_wu, ffn1_wd, ffn2_wg, ffn2_wu, ffn2_wd, ln_g, ln_b, ple_wg, ple_bg, ple_wp, ab_w_in, a_sinks, b_conv_w, b_conv_b, b_wa, b_ba, b_wx, b_bx, b_lam, ab_w_out, c_w_in, c_conv_w, c_a_log, c_dt_bias, c_norm_g, c_w_out, loss_target, m_ffn1_wg, m_ffn1_wu, m_ffn1_wd, m_ffn2_wg, m_ffn2_wu, m_ffn2_wd, m_ln_g, m_ln_b, m_ple_wg, m_ple_bg, m_ple_wp, m_ab_w_in, m_a_sinks, m_b_conv_w, m_b_conv_b, m_b_wa, m_b_ba, m_b_wx, m_b_bx, m_b_lam, m_ab_w_out, m_c_w_in, m_c_conv_w, m_c_a_log, m_c_dt_bias, m_c_norm_g, m_c_w_out, v_ffn1_wg, v_ffn1_wu, v_ffn1_wd, v_ffn2_wg, v_ffn2_wu, v_ffn2_wd, v_ln_g, v_ln_b, v_ple_wg, v_ple_bg, v_ple_wp, v_ab_w_in, v_a_sinks, v_b_conv_w, v_b_conv_b, v_b_wa, v_b_ba, v_b_wx, v_b_bx, v_b_lam, v_ab_w_out, v_c_w_in, v_c_conv_w, v_c_a_log, v_c_dt_bias, v_c_norm_g, v_c_w_out):
    a = dict(locals())
    return _step3(a)


def _step3(a):
    x, p = a["x"], a["p"]
    bsz, s_len, d = x.shape
    t = bsz * s_len
    x2 = x.reshape(t, d)
    tgt = a["loss_target"].reshape(t, d)
    p2 = p.reshape(DEPTH, t, D_PLE)
    shapes = {n: a[n].shape for n in WEIGHTS}
    n_small = sum(int(np.prod(shapes[n])) for n in SMALL_NAMES)
    small_all = SMALL_NAMES + REPL
    f_ff = shapes["ffn1_wg"][2]
    c_cols = shapes["c_w_in"][2]
    wide = dict(tm=1024, tn=1408, tk=512)
    tall = dict(tm=1408, tn=1024, tk=512)

    def cast(z):
        return z.astype(MM)

    def ffn_shards(which, l):
        return [cast(a[which + "_wg"][l]), cast(a[which + "_wu"][l]), cast(a[which + "_wd"][l])]

    def ffn_weights(gat, tag):
        return (join_cols(gat[0], name=f"join_{tag}_wg"), join_cols(gat[1], name=f"join_{tag}_wu"),
                gat[2].reshape(N_DEV * gat[2].shape[1], D_MODEL))

    def rows_full(gat):
        return gat.reshape(N_DEV * gat.shape[1], D_MODEL)

    small_send = _flat_pad([a[n] for n in SMALL_NAMES], F32, 32 * LANES).reshape(32, LANES)
    g0 = gather_multi(ffn_shards("ffn1", 0) + [small_send], name="gather_first")
    ws = _take(g0[3].reshape(N_DEV, -1), SMALL_NAMES, shapes)
    small = {n: _join(ws[n], 2) for n in SMALL_NAMES}
    ln_g, ln_b = small["ln_g"], small["ln_b"]
    wa_d, wx_d = _dense_blocks(a["b_wa"][0]), _dense_blocks(a["b_wx"][0])
    lru_w = (small["b_conv_w"][0], a["b_conv_b"][0], wa_d, a["b_ba"][0], wx_d, a["b_bx"][0], a["b_lam"][0])
    gdc_w = (a["c_a_log"][0], a["c_dt_bias"][0], a["c_norm_g"][0])
    wf = {("ffn1", 0): ffn_weights(g0[:3], "ffn1_0")}

    s0 = {"x0": x2}
    side = ("gather", ffn_shards("ffn2", 0) + [cast(a["ab_w_in"][0]), cast(a["ab_w_out"][0])])
    (s0["y1"], s0["z1"], s0["hg1"], s0["hu1"]), got = ffn_fwd(x2, *wf["ffn1", 0], ln_g[0, 0], ln_b[0, 0],
                                                             name="ffn1_fwd_0", side=side)
    wf["ffn2", 0] = ffn_weights(got[:3], "ffn2_0")
    ab_w_in, ab_w_out = join_cols(got[3], name="join_ab_in"), rows_full(got[4])
    s0["proj"] = matmul(s0["y1"], ab_w_in, mode="nn", name="ab_in_fwd", tn=896, tk=1024)
    ya = attn_fwd(s0["proj"], a["a_sinks"][0], bsz, name="attn_fwd")
    yb = lru_fwd(s0["proj"], *lru_w, bsz, name="lru_fwd")
    s0["mix"] = jnp.concatenate([ya, yb], axis=1)
    s0["y2"], s0["z2"] = mm_ln_fwd(s0["mix"], ab_w_out, s0["y1"], ln_g[0, 1], ln_b[0, 1], name="mix_out_fwd_0")
    side = ("gather", ffn_shards("ffn1", 1) + [cast(a["ple_wg"][0]), cast(a["ple_wp"][0])])
    (s0["y3"], s0["z3"], s0["hg2"], s0["hu2"]), got = ffn_fwd(s0["y2"], *wf["ffn2", 0], ln_g[0, 2], ln_b[0, 2],
                                                             name="ffn2_fwd_0", side=side)
    wf["ffn1", 1] = ffn_weights(got[:3], "ffn1_1")
    ple_wg = [rows_full(got[3]), None]
    ple_wp = [_join(got[4], 1), None]
    h1 = ple_fwd(s0["y3"], p2[0], ple_wg[0], a["ple_bg"][0], ple_wp[0], name="ple_fwd_0")

    s1 = {"x0": h1}
    side = ("gather", ffn_shards("ffn2", 1) + [cast(a["c_w_in"][0]), cast(a["c_w_out"][0]), cast(a["ple_wg"][1]),
                                               cast(a["ple_wp"][1])])
    (s1["y1"], s1["z1"], s1["hg1"], s1["hu1"]), got = ffn_fwd(h1, *wf["ffn1", 1], ln_g[1, 0], ln_b[1, 0],
                                                             name="ffn1_fwd_1", side=side)
    wf["ffn2", 1] = ffn_weights(got[:3], "ffn2_1")
    c_in_main, c_in_gate = join_cols(got[3], name="join_c_in", outs=[(0, 4 * C_WIDTH, 4 * C_WIDTH),
                                                                      (4 * C_WIDTH, 4 * C_WIDTH + 2 * C_HEADS, LANES)])
    c_w_out = rows_full(got[4])
    ple_wg[1], ple_wp[1] = rows_full(got[5]), _join(got[6], 1)
    s1["proj"] = matmul(s1["y1"], c_in_main, mode="nn", name="c_in_fwd", tn=1024, tk=1024)
    s1["gates"] = matmul(s1["y1"], c_in_gate, mode="nn", name="c_gate_fwd", tk=1024)
    s1["qkv"] = gdc_pre_fwd(s1["proj"], small["c_conv_w"][0], bsz, name="gdc_pre_fwd")
    s1["mix"], s1["o_pre"], s1["vnew"], s1["tmat"], s1["states"] = gdc_fwd(
        s1["qkv"], s1["proj"], s1["gates"], *gdc_w, bsz, name="gdc_fwd")
    s1["y2"], s1["z2"] = mm_ln_fwd(s1["mix"], c_w_out, s1["y1"], ln_g[1, 1], ln_b[1, 1], name="mix_out_fwd_1")
    s1["y3"], s1["z3"], s1["hg2"], s1["hu2"] = ffn_fwd(s1["y2"], *wf["ffn2", 1], ln_g[1, 2], ln_b[1, 2], name="ffn2_fwd_1")
    h2 = ple_fwd(s1["y3"], p2[1], ple_wg[1], a["ple_bg"][1], ple_wp[1], name="ple_fwd_1")
    loss_part, dh = loss_fwd_bwd(h2, tgt, name="loss")

    def ffn_parts(xin, act, dhg, dhu, dz, tag):
        dwg = matmul(xin, dhg, mode="tn", name=f"{tag}_wg_grad", **wide)
        dwu = matmul(xin, dhu, mode="tn", name=f"{tag}_wu_grad", **wide)
        dwd = matmul(act, dz, mode="tn", scale=0.5, out_dtype=MM, name=f"{tag}_wd_grad", **tall)
        return [split_cols([(dwg, N_DEV * f_ff)], f_ff, name=f"split_{tag}_wg"),
                split_cols([(dwu, N_DEV * f_ff)], f_ff, name=f"split_{tag}_wu"), dwd.reshape(N_DEV, f_ff, D_MODEL)]

    def ple_parts(i, s, dt, de):
        gwg = matmul(s["y3"], dt, mode="tn", out_dtype=MM, name=f"ple_wg_grad_{i}", tm=1024, tn=1024)
        gwp = matmul(p2[i], de, mode="tn", out_dtype=MM, name=f"ple_wp_grad_{i}", tn=1024)
        return [gwg.reshape(N_DEV, D_MODEL // N_DEV, D_MODEL), _split(gwp, 1)]

    gln = {"ln_g": [None, None], "ln_b": [None, None]}
    gple_bg = [None, None]

    dy3, dt, de, dbg = ple_bwd(dh, s1["y3"], p2[1], ple_wg[1], a["ple_bg"][1], ple_wp[1], name="ple_bwd_1")
    gple_bg[1] = dbg[0]
    parts_ple1 = ple_parts(1, s1, dt, de)
    dz3, dg2, db2 = ln_bwd(dy3, s1["z3"], ln_g[1, 2], name="ln2_bwd_1")
    dy2, act, dhg, dhu = ffn_bwd(dz3, s1["hg2"], s1["hu2"], *wf["ffn2", 1], name="ffn2_bwd_1", tm=FFN_BWD_TM)
    parts_ffn2_1 = ffn_parts(s1["y2"], act, dhg, dhu, dz3, "ffn2_1")
    dz2, dg1, db1 = ln_bwd(dy2, s1["z2"], ln_g[1, 1], name="ln1_bwd_1")
    dmix = matmul(dz2, c_w_out, mode="nt", name="c_out_bwd", tn=1024, tk=1024)
    parts_c_out = matmul(s1["mix"], dz2, mode="tn", out_dtype=MM, name="c_out_grad", tm=1024, tn=1024).reshape(
        N_DEV, D_MODEL // N_DEV, D_MODEL)
    (dqkv, dzc, dgates, dsm), recv1 = gdc_bwd(s1["qkv"], s1["proj"], s1["gates"], *gdc_w, s1["o_pre"], s1["vnew"],
                                              s1["tmat"], s1["states"], dmix, bsz, name="gdc_bwd",
                                              side=("exchange", parts_ffn2_1 + parts_ple1))
    draw, dccw = gdc_pre_bwd(s1["proj"], small["c_conv_w"][0], dqkv, bsz, name="gdc_pre_bwd")
    dproj = jnp.concatenate([draw, dzc], axis=1).astype(MM)
    dgb = dgates.astype(MM)
    dy1 = matmul(dproj, c_in_main, mode="nt", add=dz2, add_scale=DN_ALPHA, name="c_in_bwd", tn=1024, tk=1024)
    dy1 = matmul(dgb, c_in_gate, mode="nt", add=dy1, name="c_gate_bwd", tn=1024)
    g_c_main = matmul(s1["y1"], dproj, mode="tn", name="c_in_grad", tm=1024, tn=1024)
    g_c_gate = matmul(s1["y1"], dgb, mode="tn", name="c_gate_grad", tm=1024)
    parts_c_in = split_cols([(g_c_main, 4 * C_WIDTH), (g_c_gate, 2 * C_HEADS)], c_cols, name="split_c_in")
    dz1, dg0, db0 = ln_bwd(dy1, s1["z1"], ln_g[1, 0], name="ln0_bwd_1")
    dh, act, dhg, dhu = ffn_bwd(dz1, s1["hg1"], s1["hu1"], *wf["ffn1", 1], name="ffn1_bwd_1", tm=FFN_BWD_TM)
    parts_ffn1_1 = ffn_parts(s1["x0"], act, dhg, dhu, dz1, "ffn1_1")
    gln["ln_g"][1] = jnp.concatenate([dg0, dg1, dg2], axis=0)
    gln["ln_b"][1] = jnp.concatenate([db0, db1, db2], axis=0)

    dy3, dt, de, dbg = ple_bwd(dh, s0["y3"], p2[0], ple_wg[0], a["ple_bg"][0], ple_wp[0], name="ple_bwd_0")
    gple_bg[0] = dbg[0]
    parts_ple0 = ple_parts(0, s0, dt, de)
    dz3, dg2, db2 = ln_bwd(dy3, s0["z3"], ln_g[0, 2], name="ln2_bwd_0")
    (dy2, act, dhg, dhu), recv2 = ffn_bwd(dz3, s0["hg2"], s0["hu2"], *wf["ffn2", 0], name="ffn2_bwd_0", tm=FFN_BWD_TM,
                                          side=("exchange", parts_ffn1_1 + [parts_c_in, parts_c_out]))
    parts_ffn2_0 = ffn_parts(s0["y2"], act, dhg, dhu, dz3, "ffn2_0")
    dz2, dg1, db1 = ln_bwd(dy2, s0["z2"], ln_g[0, 1], name="ln1_bwd_0")
    dmix = matmul(dz2, ab_w_out, mode="nt", name="ab_out_bwd", tn=1024, tk=1024)
    parts_ab_out = matmul(s0["mix"], dz2, mode="tn", out_dtype=MM, name="ab_out_grad", tm=1024, tn=1024).reshape(
        N_DEV, D_MODEL // N_DEV, D_MODEL)
    dq, dk, dv, dsk = attn_bwd(s0["proj"], a["a_sinks"][0], dmix, bsz, name="attn_bwd")
    dbx, dbgate, dcw, dcb, dwa, dba, dwx, dbxb, dlam = lru_bwd(s0["proj"], *lru_w, dmix, bsz, name="lru_bwd")
    dproj = jnp.concatenate([dq, dk, dv, dbx, dbgate], axis=1).astype(MM)
    dy1 = matmul(dproj, ab_w_in, mode="nt", add=dz2, add_scale=DN_ALPHA, name="ab_in_bwd", tn=1024, tk=1792)
    g_ab_in = matmul(s0["y1"], dproj, mode="tn", name="ab_in_grad", tm=1024, tn=896)
    parts_ab_in = split_cols([(g_ab_in, AB_PROJ)], AB_PROJ // N_DEV, name="split_ab_in")
    dz1, dg0, db0 = ln_bwd(dy1, s0["z1"], ln_g[0, 0], name="ln0_bwd_0")
    (dh, act, dhg, dhu), recv3 = ffn_bwd(dz1, s0["hg1"], s0["hu1"], *wf["ffn1", 0], name="ffn1_bwd_0", tm=FFN_BWD_TM,
                                         side=("exchange", parts_ffn2_0 + parts_ple0))
    parts_ffn1_0 = ffn_parts(s0["x0"], act, dhg, dhu, dz1, "ffn1_0")
    gln["ln_g"][0] = jnp.concatenate([dg0, dg1, dg2], axis=0)
    gln["ln_b"][0] = jnp.concatenate([db0, db1, db2], axis=0)
    grad_x = dh.reshape(bsz, s_len, d)

    dsm_sum = jnp.sum(dsm, axis=0)
    full = dict(ln_g=jnp.stack(gln["ln_g"]), ln_b=jnp.stack(gln["ln_b"]), b_conv_w=dcw[None],
                c_conv_w=jnp.sum(dccw, axis=0)[None], ple_bg=jnp.stack(gple_bg),
                a_sinks=jnp.sum(dsk, axis=0)[:, :A_HEADS], b_conv_b=dcb, b_wa=_diag_blocks(dwa)[None], b_ba=dba,
                b_wx=_diag_blocks(dwx)[None], b_bx=dbxb, b_lam=dlam, c_a_log=dsm_sum[None, :C_HEADS, 0],
                c_dt_bias=dsm_sum[None, :C_HEADS, 1], c_norm_g=jnp.sum(dsm_sum[C_HEADS:], axis=0)[None])
    small_rows = SMALL_F32 // LANES
    repl_flat = _flat_pad([full[n] for n in REPL], F32, SMALL_F32 - n_small)
    small8 = jnp.concatenate([_flat8_pad([_split(full[n], 2) for n in SMALL_NAMES], F32, n_small),
                              jnp.broadcast_to(repl_flat, (N_DEV,) + repl_flat.shape)], axis=1)
    recv4 = exchange_multi(parts_ffn1_0 + [parts_ab_in, parts_ab_out, small8.reshape(N_DEV, small_rows, LANES)],
                           name="exchange_last")

    def upd(parts, n, l, shape2d, **kw):
        wmv = [a[pre + n][l].reshape(shape2d) for pre in ("", "m_", "v_")]
        return adamw_rows(parts, 0, *wmv, name=f"adamw_{n}_{l}", **kw)

    def upd_ffn(parts, which, l):
        return {(which + "_wg", l): upd(parts[0], which + "_wg", l, (D_MODEL, f_ff)),
                (which + "_wu", l): upd(parts[1], which + "_wu", l, (D_MODEL, f_ff)),
                (which + "_wd", l): upd(parts[2], which + "_wd", l, (f_ff, D_MODEL), tr=176)}

    rows8 = D_MODEL // N_DEV
    res = {}
    res.update(upd_ffn(recv1[:3], "ffn2", 1))
    res["ple_wg", 1] = upd(recv1[3], "ple_wg", 1, (rows8, D_MODEL), tr=128)
    res["ple_wp", 1] = upd(recv1[4], "ple_wp", 1, (D_PLE, LANES))
    res.update(upd_ffn(recv2[:3], "ffn1", 1))
    res["c_w_in", 0] = upd(recv2[3], "c_w_in", 0, (D_MODEL, c_cols))
    res["c_w_out", 0] = upd(recv2[4], "c_w_out", 0, (rows8, D_MODEL), tr=128)
    res.update(upd_ffn(recv3[:3], "ffn2", 0))
    res["ple_wg", 0] = upd(recv3[3], "ple_wg", 0, (rows8, D_MODEL), tr=128)
    res["ple_wp", 0] = upd(recv3[4], "ple_wp", 0, (D_PLE, LANES))
    res.update(upd_ffn(recv4[:3], "ffn1", 0))
    res["ab_w_in", 0] = upd(recv4[3], "ab_w_in", 0, (D_MODEL, AB_PROJ // N_DEV))
    res["ab_w_out", 0] = upd(recv4[4], "ab_w_out", 0, (rows8, D_MODEL), tr=128)
    res_small = adamw_rows(recv4[5], 0, *[_flat_pad([a[pre + n] for n in small_all], F32, SMALL_F32).reshape(
        small_rows, LANES) for pre in ("", "m_", "v_")], name="adamw_small", tr=small_rows)
    kinds = []
    for k in range(4):
        kd = _take(res_small[k].reshape(-1), small_all, shapes)
        for n in WEIGHTS:
            if n not in kd:
                kd[n] = jnp.stack([res[n, l][k] for l in range(shapes[n][0])]).reshape(shapes[n])
        kinds.append(kd)
    loss = lax.psum(loss_part[0, 0], ("x", "y", "c"))
    return (loss, grad_x, *[kinds[0][n] for n in WEIGHTS], *[kinds[1][n] for n in WEIGHTS],
            *[kinds[2][n] for n in WEIGHTS], *[kinds[3][n] for n in WEIGHTS])


def join_cols(x, *, name, outs=None, tk=256):
    _, kk, n = x.shape
    tk = _tile8(kk, tk)
    outs = outs or [(0, N_DEV * n, N_DEV * n)]

    def body(x_ref, *o_refs):
        full = jnp.concatenate([x_ref[k] for k in range(N_DEV)], axis=-1)
        for (lo, hi, wd), o_ref in zip(outs, o_refs):
            piece = full[:, lo:hi]
            if wd > hi - lo:
                piece = jnp.concatenate([piece, jnp.zeros((tk, wd - (hi - lo)), piece.dtype)], axis=-1)
            o_ref[...] = piece

    res = _pcall(
        body, name=name, grid=(kk // tk,), in_specs=[pl.BlockSpec((N_DEV, tk, n), lambda i: (0, i, 0))],
        out_specs=[pl.BlockSpec((tk, wd), lambda i: (i, 0)) for _, _, wd in outs],
        out_shape=[jax.ShapeDtypeStruct((kk, wd), x.dtype) for _, _, wd in outs], compiler_params=_cp("parallel"),
    )(x)
    return res if len(outs) > 1 else res[0]


def split_cols(pieces, n, *, name, tk=256):
    kk = pieces[0][0].shape[0]
    tk = _tile8(kk, tk)

    def body(*refs):
        o_ref = refs[-1]
        vals = [r[...][:, :used] for r, (_, used) in zip(refs[:-1], pieces)]
        full = vals[0] if len(vals) == 1 else jnp.concatenate(vals, axis=-1)
        for k in range(N_DEV):
            o_ref[k] = full[:, k * n:(k + 1) * n].astype(MM)

    return _pcall(
        body, name=name, grid=(kk // tk,),
        in_specs=[pl.BlockSpec((tk, arr.shape[1]), lambda i: (i, 0)) for arr, _ in pieces],
        out_specs=pl.BlockSpec((N_DEV, tk, n), lambda i: (0, i, 0)),
        out_shape=jax.ShapeDtypeStruct((N_DEV, kk, n), MM), compiler_params=_cp("parallel"),
    )(*[arr for arr, _ in pieces])


def gather_multi(shards, *, name):
    ng = len(shards)

    def body(*refs):
        x_refs, o_refs = refs[:ng], refs[ng:2 * ng]
        send_sems, recv_sems, local_sems = refs[2 * ng:]
        x, y, c = _me()
        sibling = (x, y, 1 - c)
        chips = [(1 - x, y), (x, 1 - y), (1 - x, 1 - y)]

        def slot(px, py, pc):
            return 4 * px + 2 * py + pc

        def copy(gi, k, block, to, src=None):
            dst = o_refs[gi].at[slot(*block)]
            return pltpu.make_async_remote_copy(
                src_ref=dst if src is None else src, dst_ref=dst, send_sem=send_sems.at[7 * gi + k],
                recv_sem=recv_sems.at[7 * gi + k], device_id=to, device_id_type=MESH_ID)

        own = [pltpu.make_async_copy(x_refs[gi], o_refs[gi].at[slot(x, y, c)], local_sems.at[gi]) for gi in range(ng)]
        for cp in own:
            cp.start()
        first = []
        for gi in range(ng):
            first.append(copy(gi, 0, (x, y, c), sibling, src=x_refs[gi]))
            first += [copy(gi, 1 + j, (x, y, c), (*chip, c), src=x_refs[gi]) for j, chip in enumerate(chips)]
        for cp in first:
            cp.start()
        passed = []
        for j, chip in enumerate(chips):
            for gi in range(ng):
                copy(gi, 1 + j, (*chip, c), (x, y, c)).wait_recv()
                fwd = copy(gi, 4 + j, (*chip, c), sibling)
                fwd.start()
                passed.append(fwd)
        for gi in range(ng):
            copy(gi, 0, sibling, (x, y, c)).wait_recv()
            for j, chip in enumerate(chips):
                copy(gi, 4 + j, (*chip, 1 - c), (x, y, c)).wait_recv()
        for cp in first + passed:
            cp.wait_send()
        for cp in own:
            cp.wait()

    hbm = pl.BlockSpec(memory_space=pl.ANY)
    return _pcall(
        body, name=name, in_specs=[hbm] * ng, out_specs=[hbm] * ng,
        out_shape=[jax.ShapeDtypeStruct((N_DEV,) + s.shape, s.dtype) for s in shards],
        scratch_shapes=[pltpu.SemaphoreType.DMA((7 * ng,)), pltpu.SemaphoreType.DMA((7 * ng,)),
                        pltpu.SemaphoreType.DMA((ng,))],
    )(*shards)


def exchange_multi(parts, *, name):
    ng = len(parts)

    def body(*refs):
        x_refs, o_refs = refs[:ng], refs[ng:2 * ng]
        send_sems, recv_sems, local_sems = refs[2 * ng:]
        x, y, c = _me()
        mine = 4 * x + 2 * y + c
        own = [pltpu.make_async_copy(x_refs[gi].at[mine], o_refs[gi].at[mine], local_sems.at[gi]) for gi in range(ng)]
        for cp in own:
            cp.start()
        copies = []
        for k, (dx, dy, dc) in enumerate(_FLIPS):
            px, py, pc = _flip(x, dx), _flip(y, dy), _flip(c, dc)
            for gi in range(ng):
                cp = pltpu.make_async_remote_copy(
                    src_ref=x_refs[gi].at[4 * px + 2 * py + pc], dst_ref=o_refs[gi].at[mine],
                    send_sem=send_sems.at[7 * gi + k], recv_sem=recv_sems.at[7 * gi + k], device_id=(px, py, pc),
                    device_id_type=MESH_ID)
                cp.start()
                copies.append(cp)
        for cp in copies:
            cp.wait()
        for cp in own:
            cp.wait()

    hbm = pl.BlockSpec(memory_space=pl.ANY)
    return _pcall(
        body, name=name, in_specs=[hbm] * ng, out_specs=[hbm] * ng,
        out_shape=[jax.ShapeDtypeStruct(s.shape, s.dtype) for s in parts],
        scratch_shapes=[pltpu.SemaphoreType.DMA((7 * ng,)), pltpu.SemaphoreType.DMA((7 * ng,)),
                        pltpu.SemaphoreType.DMA((ng,))],
    )(*parts)


def adamw_rows(parts, row0, w, m, v, *, name, tr=256):
    r, cdim = w.shape
    tr = _tile8(math.gcd(r, row0) if row0 else r, tr)
    blk0 = row0 // tr

    def body(p_ref, w_ref, m_ref, v_ref, g_ref, d_ref, mo_ref, vo_ref):
        g = p_ref[0].astype(F32)
        for j in range(1, N_DEV):
            g = g + p_ref[j].astype(F32)
        g_ref[...] = g
        mn = ADAM_B1 * m_ref[...] + (1.0 - ADAM_B1) * g
        vn = ADAM_B2 * v_ref[...] + (1.0 - ADAM_B2) * (g * g)
        mo_ref[...] = mn
        vo_ref[...] = vn
        m_hat = mn / (1.0 - ADAM_B1 ** ADAM_STEP)
        v_hat = vn / (1.0 - ADAM_B2 ** ADAM_STEP)
        d_ref[...] = -ADAM_LR * (m_hat / (jnp.sqrt(v_hat) + ADAM_EPS) + ADAM_WD * w_ref[...])

    blk = pl.BlockSpec((tr, cdim), lambda i: (i, 0))
    shp = jax.ShapeDtypeStruct((r, cdim), F32)
    return _pcall(
        body, name=name, grid=(r // tr,),
        in_specs=[pl.BlockSpec((N_DEV, tr, cdim), lambda i: (0, blk0 + i, 0)), blk, blk, blk],
        out_specs=[blk, blk, blk, blk], out_shape=[shp, shp, shp, shp], compiler_params=_cp("parallel"),
    )(parts, w, m, v)


GROUP_A = ["ffn1_wg", "ffn1_wu", "ffn2_wg", "ffn2_wu"]
GROUP_B = ["ffn1_wd", "ffn2_wd", "ple_wg", "ab_w_out", "c_w_out"]
SMALL_NAMES = ["ln_g", "ln_b", "b_conv_w", "c_conv_w"]
LANES = 128
FFN_BWD_TM = 1024
PLE_WP_ROWS = DEPTH * D_PLE


def _step2(a):
    x, p = a["x"], a["p"]
    bsz, s_len, d = x.shape
    t = bsz * s_len
    x2 = x.reshape(t, d)
    tgt = a["loss_target"].reshape(t, d)
    p2 = p.reshape(DEPTH, t, D_PLE)
    shapes = {n: a[n].shape for n in WEIGHTS}
    bits_per = 1 if MM == F32 else 2
    n_small = sum(int(np.prod(shapes[n])) for n in SMALL_NAMES)
    small_all = SMALL_NAMES + REPL
    f_ff = shapes["ffn1_wg"][2]
    rows_b = {n: shapes[n][0] * shapes[n][1] for n in GROUP_B}
    off_b = dict(zip(GROUP_B, np.cumsum([0] + [rows_b[n] for n in GROUP_B])[:-1].tolist()))

    send = [
        jnp.concatenate([a[n].astype(MM).reshape(-1, f_ff) for n in GROUP_A], axis=0),
        jnp.concatenate([a[n].astype(MM).reshape(-1, D_MODEL) for n in GROUP_B], axis=0),
        a["ab_w_in"][0].astype(MM),
        a["c_w_in"][0].astype(MM),
        a["ple_wp"].astype(MM).reshape(PLE_WP_ROWS, LANES),
        _flat_pad([a[n] for n in SMALL_NAMES], F32, 32 * LANES).reshape(32, LANES),
    ]
    ga, gb, gc, gd, ge, gf = gather_multi(send, name="gather_weights")
    wa_full = join_cols(ga, name="join_ffn").reshape(len(GROUP_A), DEPTH, D_MODEL, N_DEV * f_ff)
    w = {n: wa_full[i] for i, n in enumerate(GROUP_A)}
    for n in GROUP_B:
        lyr, rws = shapes[n][0], shapes[n][1]
        blk = gb[:, off_b[n]:off_b[n] + rows_b[n]].reshape(N_DEV, lyr, rws, D_MODEL)
        w[n] = jnp.swapaxes(blk, 0, 1).reshape(lyr, N_DEV * rws, D_MODEL)
    w["ab_w_in"] = join_cols(gc, name="join_ab_in")
    c_in_main, c_in_gate = join_cols(gd, name="join_c_in", outs=[(0, 4 * C_WIDTH, 4 * C_WIDTH),
                                                                  (4 * C_WIDTH, 4 * C_WIDTH + 2 * C_HEADS, LANES)])
    w["ple_wp"] = _join(ge.reshape(N_DEV, DEPTH, D_PLE, LANES), 2)
    ws = _take(gf.reshape(N_DEV, -1), SMALL_NAMES, shapes)
    w.update({n: _join(ws[n], 2) for n in SMALL_NAMES})
    ln_g, ln_b = w["ln_g"], w["ln_b"]
    wa_d, wx_d = _dense_blocks(a["b_wa"][0]), _dense_blocks(a["b_wx"][0])
    lru_w = (w["b_conv_w"][0], a["b_conv_b"][0], wa_d, a["b_ba"][0], wx_d, a["b_bx"][0], a["b_lam"][0])
    gdc_w = (a["c_a_log"][0], a["c_dt_bias"][0], a["c_norm_g"][0])

    h = x2
    saved = []
    for i in range(DEPTH):
        s = {"x0": h}
        s["y1"], s["z1"], s["hg1"], s["hu1"] = ffn_fwd(h, w["ffn1_wg"][i], w["ffn1_wu"][i], w["ffn1_wd"][i], ln_g[i, 0], ln_b[i, 0],
                                   name=f"ffn1_fwd_{i}")
        if i == 0:
            s["proj"] = matmul(s["y1"], w["ab_w_in"], mode="nn", name="ab_in_fwd", tn=896, tk=1024)
            ya = attn_fwd(s["proj"], a["a_sinks"][0], bsz, name="attn_fwd")
            yb = lru_fwd(s["proj"], *lru_w, bsz, name="lru_fwd")
            s["mix"] = jnp.concatenate([ya, yb], axis=1)
            w_out = w["ab_w_out"][0]
        else:
            s["proj"] = matmul(s["y1"], c_in_main, mode="nn", name="c_in_fwd", tn=1024, tk=1024)
            s["gates"] = matmul(s["y1"], c_in_gate, mode="nn", name="c_gate_fwd", tk=1024)
            s["qkv"] = gdc_pre_fwd(s["proj"], w["c_conv_w"][0], bsz, name="gdc_pre_fwd")
            s["mix"], s["o_pre"], s["vnew"], s["tmat"], s["states"] = gdc_fwd(
                s["qkv"], s["proj"], s["gates"], *gdc_w, bsz, name="gdc_fwd")
            w_out = w["c_w_out"][0]
        s["y2"], s["z2"] = mm_ln_fwd(s["mix"], w_out, s["y1"], ln_g[i, 1], ln_b[i, 1], name=f"mix_out_fwd_{i}")
        s["y3"], s["z3"], s["hg2"], s["hu2"] = ffn_fwd(s["y2"], w["ffn2_wg"][i], w["ffn2_wu"][i], w["ffn2_wd"][i], ln_g[i, 2], ln_b[i, 2],
                                   name=f"ffn2_fwd_{i}")
        h = ple_fwd(s["y3"], p2[i], w["ple_wg"][i], a["ple_bg"][i], w["ple_wp"][i], name=f"ple_fwd_{i}")
        saved.append(s)
    loss_part, dh = loss_fwd_bwd(h, tgt, name="loss")

    g = {n: [None] * shapes[n][0] for n in ("ffn1_wg", "ffn1_wu", "ffn1_wd", "ffn2_wg", "ffn2_wu", "ffn2_wd", "ln_g",
                                             "ln_b", "ple_wg", "ple_bg", "ple_wp")}
    wide = dict(tm=1024, tn=1408, tk=512)
    tall = dict(tm=1408, tn=1024, tk=512)
    for i in reversed(range(DEPTH)):
        s = saved[i]
        dy3, dt, de, dbg = ple_bwd(dh, s["y3"], p2[i], w["ple_wg"][i], a["ple_bg"][i], w["ple_wp"][i], name=f"ple_bwd_{i}")
        g["ple_wg"][i] = matmul(s["y3"], dt, mode="tn", name=f"ple_wg_grad_{i}", tm=1024, tn=1024)
        g["ple_wp"][i] = matmul(p2[i], de, mode="tn", name=f"ple_wp_grad_{i}", tn=1024)
        g["ple_bg"][i] = dbg[0]
        dz3, dg2, db2 = ln_bwd(dy3, s["z3"], ln_g[i, 2], name=f"ln2_bwd_{i}")
        dy2, act, dhg, dhu = ffn_bwd(dz3, s["hg2"], s["hu2"], w["ffn2_wg"][i], w["ffn2_wu"][i], w["ffn2_wd"][i], name=f"ffn2_bwd_{i}", tm=FFN_BWD_TM)
        g["ffn2_wg"][i] = matmul(s["y2"], dhg, mode="tn", name=f"ffn2_wg_grad_{i}", **wide)
        g["ffn2_wu"][i] = matmul(s["y2"], dhu, mode="tn", name=f"ffn2_wu_grad_{i}", **wide)
        g["ffn2_wd"][i] = matmul(act, dz3, mode="tn", scale=0.5, name=f"ffn2_wd_grad_{i}", **tall)
        dz2, dg1, db1 = ln_bwd(dy2, s["z2"], ln_g[i, 1], name=f"ln1_bwd_{i}")
        if i == 0:
            dmix = matmul(dz2, w["ab_w_out"][0], mode="nt", name="ab_out_bwd", tn=1024, tk=1024)
            g["ab_w_out"] = matmul(s["mix"], dz2, mode="tn", name="ab_out_grad", tm=1024, tn=1024)
            dq, dk, dv, dsk = attn_bwd(s["proj"], a["a_sinks"][0], dmix, bsz, name="attn_bwd")
            dbx, dbgate, dcw, dcb, dwa, dba, dwx, dbxb, dlam = lru_bwd(s["proj"], *lru_w, dmix, bsz, name="lru_bwd")
            dproj = jnp.concatenate([dq, dk, dv, dbx, dbgate], axis=1).astype(MM)
            dy1 = matmul(dproj, w["ab_w_in"], mode="nt", add=dz2, add_scale=DN_ALPHA, name="ab_in_bwd", tn=1024, tk=1792)
            g_ab_in = matmul(s["y1"], dproj, mode="tn", name="ab_in_grad", tm=1024, tn=896)
        else:
            dmix = matmul(dz2, w["c_w_out"][0], mode="nt", name="c_out_bwd", tn=1024, tk=1024)
            g["c_w_out"] = matmul(s["mix"], dz2, mode="tn", name="c_out_grad", tm=1024, tn=1024)
            dqkv, dzc, dgates, dsm = gdc_bwd(s["qkv"], s["proj"], s["gates"], *gdc_w, s["o_pre"], s["vnew"], s["tmat"],
                                             s["states"], dmix, bsz, name="gdc_bwd")
            draw, dccw = gdc_pre_bwd(s["proj"], w["c_conv_w"][0], dqkv, bsz, name="gdc_pre_bwd")
            dproj = jnp.concatenate([draw, dzc], axis=1).astype(MM)
            dgb = dgates.astype(MM)
            dy1 = matmul(dproj, c_in_main, mode="nt", add=dz2, add_scale=DN_ALPHA, name="c_in_bwd", tn=1024, tk=1024)
            dy1 = matmul(dgb, c_in_gate, mode="nt", add=dy1, name="c_gate_bwd", tn=1024)
            g_c_main = matmul(s["y1"], dproj, mode="tn", name="c_in_grad", tm=1024, tn=1024)
            g_c_gate = matmul(s["y1"], dgb, mode="tn", name="c_gate_grad", tm=1024)
        dz1, dg0, db0 = ln_bwd(dy1, s["z1"], ln_g[i, 0], name=f"ln0_bwd_{i}")
        dh, act, dhg, dhu = ffn_bwd(dz1, s["hg1"], s["hu1"], w["ffn1_wg"][i], w["ffn1_wu"][i], w["ffn1_wd"][i], name=f"ffn1_bwd_{i}", tm=FFN_BWD_TM)
        g["ffn1_wg"][i] = matmul(s["x0"], dhg, mode="tn", name=f"ffn1_wg_grad_{i}", **wide)
        g["ffn1_wu"][i] = matmul(s["x0"], dhu, mode="tn", name=f"ffn1_wu_grad_{i}", **wide)
        g["ffn1_wd"][i] = matmul(act, dz1, mode="tn", scale=0.5, name=f"ffn1_wd_grad_{i}", **tall)
        g["ln_g"][i] = jnp.concatenate([dg0, dg1, dg2], axis=0)
        g["ln_b"][i] = jnp.concatenate([db0, db1, db2], axis=0)
    grad_x = dh.reshape(bsz, s_len, d)
    full = {n: jnp.stack(v) if isinstance(v, list) else v[None] for n, v in g.items()}
    full["b_conv_w"] = dcw[None]
    full["c_conv_w"] = jnp.sum(dccw, axis=0)[None]
    dsm_sum = jnp.sum(dsm, axis=0)
    full.update(a_sinks=jnp.sum(dsk, axis=0)[:, :A_HEADS], b_conv_b=dcb, b_wa=_diag_blocks(dwa)[None], b_ba=dba,
                b_wx=_diag_blocks(dwx)[None], b_bx=dbxb, b_lam=dlam, c_a_log=dsm_sum[None, :C_HEADS, 0],
                c_dt_bias=dsm_sum[None, :C_HEADS, 1], c_norm_g=jnp.sum(dsm_sum[C_HEADS:], axis=0)[None])

    small_f32_rows = SMALL_F32 // LANES
    repl_flat = _flat_pad([full[n] for n in REPL], F32, SMALL_F32 - n_small)
    small8 = jnp.concatenate([_flat8_pad([_split(full[n], 2) for n in SMALL_NAMES], F32, n_small),
                              jnp.broadcast_to(repl_flat, (N_DEV,) + repl_flat.shape)], axis=1)
    parts = [
        split_cols([(jnp.concatenate([full[n].reshape(-1, N_DEV * f_ff) for n in GROUP_A], axis=0), N_DEV * f_ff)], f_ff,
                   name="split_ffn"),
        jnp.concatenate([_split(full[n], 1).astype(MM).reshape(N_DEV, -1, D_MODEL) for n in GROUP_B], axis=1),
        split_cols([(g_ab_in, AB_PROJ)], AB_PROJ // N_DEV, name="split_ab_in"),
        split_cols([(g_c_main, 4 * C_WIDTH), (g_c_gate, 2 * C_HEADS)], (4 * C_WIDTH + 2 * C_HEADS) // N_DEV,
                   name="split_c_in"),
        _split(full["ple_wp"], 2).astype(MM).reshape(N_DEV, PLE_WP_ROWS, LANES),
        small8.reshape(N_DEV, small_f32_rows, LANES),
    ]
    ra, rb, rc, rd, re, small_parts = exchange_multi(parts, name="exchange_grads")

    def wmv(n, shape2d):
        return [a[pre + n].reshape(shape2d) for pre in ("", "m_", "v_")]

    res = {}
    for i, n in enumerate(GROUP_A):
        res[n] = adamw_rows(ra, i * DEPTH * D_MODEL, *wmv(n, (DEPTH * D_MODEL, f_ff)), name=f"adamw_{n}")
    for n in GROUP_B:
        res[n] = adamw_rows(rb, off_b[n], *wmv(n, (rows_b[n], D_MODEL)), name=f"adamw_{n}", tr=64)
    res["ab_w_in"] = adamw_rows(rc, 0, *wmv("ab_w_in", (D_MODEL, AB_PROJ // N_DEV)), name="adamw_ab_w_in")
    res["c_w_in"] = adamw_rows(rd, 0, *wmv("c_w_in", (D_MODEL, shapes["c_w_in"][2])), name="adamw_c_w_in")
    res["ple_wp"] = adamw_rows(re, 0, *wmv("ple_wp", (PLE_WP_ROWS, LANES)), name="adamw_ple_wp")
    res_small = adamw_rows(small_parts, 0, *[_flat_pad([a[pre + n] for n in small_all], F32, SMALL_F32).reshape(
        small_f32_rows, LANES) for pre in ("", "m_", "v_")], name="adamw_small", tr=576)
    kinds = []
    for k in range(4):
        kd = {n: res[n][k].reshape(shapes[n]) for n in res}
        kd.update(_take(res_small[k].reshape(-1), small_all, shapes))
        kinds.append(kd)
    loss = lax.psum(loss_part[0, 0], ("x", "y", "c"))
    return (loss, grad_x, *[kinds[0][n] for n in WEIGHTS], *[kinds[1][n] for n in WEIGHTS],
            *[kinds[2][n] for n in WEIGHTS], *[kinds[3][n] for n in WEIGHTS])


BIG_ROWS = 5632
SMALL_F32 = 73728


def _flat_pad(arrs, dtype, total):
    flat = jnp.concatenate([z.astype(dtype).reshape(-1) for z in arrs])
    return jnp.pad(flat, (0, total - flat.shape[0]))


def _flat8_pad(arrs, dtype, total):
    flat = jnp.concatenate([z.astype(dtype).reshape(N_DEV, -1) for z in arrs], axis=1)
    return jnp.pad(flat, ((0, 0), (0, total - flat.shape[1])))


def _bits(z):
    return z if MM == F32 else _as_bf16_bits(z)


def _unbits(z):
    return z if MM == F32 else _from_bf16_bits(z)


def _take(flat, names, shapes):
    out, off = {}, 0
    for n in names:
        sz = int(np.prod(shapes[n]))
        out[n] = flat[..., off:off + sz].reshape(flat.shape[:-1] + tuple(shapes[n]))
        off += sz
    return out


def _step(a):
    x, p = a["x"], a["p"]
    bsz, s_len, d = x.shape
    t = bsz * s_len
    x2 = x.reshape(t, d)
    tgt = a["loss_target"].reshape(t, d)
    p2 = p.reshape(DEPTH, t, D_PLE)
    shapes = {n: a[n].shape for n in WEIGHTS}
    big_names = [n for n, _ in BIG]
    small_names = [n for n, _ in SMALL]
    n_small = sum(int(np.prod(shapes[n])) for n in small_names)
    bits_per = 1 if MM == F32 else 2
    small_rows = -(-(n_small * bits_per) // PACK_ALIGN) * (PACK_ALIGN // PACK_COLS)

    send = jnp.concatenate([
        _flat_pad([a[n] for n in big_names], MM, BIG_ROWS * PACK_COLS).reshape(BIG_ROWS, PACK_COLS),
        _bits(_flat_pad([a[n] for n in small_names], F32, small_rows * PACK_COLS // bits_per)).reshape(small_rows, PACK_COLS),
    ], axis=0)
    gathered = all_gather(send, name="gather_weights")
    wb = _take(gathered[:, :BIG_ROWS].reshape(N_DEV, -1), big_names, shapes)
    ws = _take(_unbits(gathered[:, BIG_ROWS:].reshape(N_DEV, -1)), small_names, shapes)
    w = {n: _join(wb[n], ax) for n, ax in BIG}
    w.update({n: _join(ws[n], ax) for n, ax in SMALL})
    ln_g, ln_b = w["ln_g"], w["ln_b"]
    c_in_main = w["c_w_in"][0][:, :4 * C_WIDTH]
    c_in_gate = jnp.pad(w["c_w_in"][0][:, 4 * C_WIDTH:], ((0, 0), (0, 128 - 2 * C_HEADS)))
    wa_d, wx_d = _dense_blocks(a["b_wa"][0]), _dense_blocks(a["b_wx"][0])
    lru_w = (w["b_conv_w"][0], a["b_conv_b"][0], wa_d, a["b_ba"][0], wx_d, a["b_bx"][0], a["b_lam"][0])
    gdc_w = (a["c_a_log"][0], a["c_dt_bias"][0], a["c_norm_g"][0])

    h = x2
    saved = []
    for i in range(DEPTH):
        s = {"x0": h}
        s["y1"], s["z1"], s["hg1"], s["hu1"] = ffn_fwd(h, w["ffn1_wg"][i], w["ffn1_wu"][i], w["ffn1_wd"][i], ln_g[i, 0], ln_b[i, 0],
                                   name=f"ffn1_fwd_{i}")
        if i == 0:
            s["proj"] = matmul(s["y1"], w["ab_w_in"][0], mode="nn", name="ab_in_fwd")
            ya = attn_fwd(s["proj"], a["a_sinks"][0], bsz, name="attn_fwd")
            yb = lru_fwd(s["proj"], *lru_w, bsz, name="lru_fwd")
            s["mix"] = jnp.concatenate([ya, yb], axis=1)
            w_out = w["ab_w_out"][0]
        else:
            s["proj"] = matmul(s["y1"], c_in_main, mode="nn", name="c_in_fwd")
            s["gates"] = matmul(s["y1"], c_in_gate, mode="nn", name="c_gate_fwd")
            s["qkv"] = gdc_pre_fwd(s["proj"], w["c_conv_w"][0], bsz, name="gdc_pre_fwd")
            s["mix"], s["o_pre"], s["vnew"], s["tmat"], s["states"] = gdc_fwd(
                s["qkv"], s["proj"], s["gates"], *gdc_w, bsz, name="gdc_fwd")
            w_out = w["c_w_out"][0]
        s["y2"], s["z2"] = mm_ln_fwd(s["mix"], w_out, s["y1"], ln_g[i, 1], ln_b[i, 1], name=f"mix_out_fwd_{i}")
        s["y3"], s["z3"], s["hg2"], s["hu2"] = ffn_fwd(s["y2"], w["ffn2_wg"][i], w["ffn2_wu"][i], w["ffn2_wd"][i], ln_g[i, 2], ln_b[i, 2],
                                   name=f"ffn2_fwd_{i}")
        h = ple_fwd(s["y3"], p2[i], w["ple_wg"][i], a["ple_bg"][i], w["ple_wp"][i], name=f"ple_fwd_{i}")
        saved.append(s)
    loss_part, dh = loss_fwd_bwd(h, tgt, name="loss")

    g = {n: [None] * shapes[n][0] for n in ("ffn1_wg", "ffn1_wu", "ffn1_wd", "ffn2_wg", "ffn2_wu", "ffn2_wd", "ln_g",
                                             "ln_b", "ple_wg", "ple_bg", "ple_wp")}
    wide = dict(tm=1024, tn=1408, tk=512)
    tall = dict(tm=1408, tn=1024, tk=512)
    for i in reversed(range(DEPTH)):
        s = saved[i]
        dy3, dt, de, dbg = ple_bwd(dh, s["y3"], p2[i], w["ple_wg"][i], a["ple_bg"][i], w["ple_wp"][i], name=f"ple_bwd_{i}")
        g["ple_wg"][i] = matmul(s["y3"], dt, mode="tn", name=f"ple_wg_grad_{i}")
        g["ple_wp"][i] = matmul(p2[i], de, mode="tn", name=f"ple_wp_grad_{i}")
        g["ple_bg"][i] = dbg[0]
        dz3, dg2, db2 = ln_bwd(dy3, s["z3"], ln_g[i, 2], name=f"ln2_bwd_{i}")
        dy2, act, dhg, dhu = ffn_bwd(dz3, s["hg2"], s["hu2"], w["ffn2_wg"][i], w["ffn2_wu"][i], w["ffn2_wd"][i], name=f"ffn2_bwd_{i}", tm=FFN_BWD_TM)
        g["ffn2_wg"][i] = matmul(s["y2"], dhg, mode="tn", name=f"ffn2_wg_grad_{i}", **wide)
        g["ffn2_wu"][i] = matmul(s["y2"], dhu, mode="tn", name=f"ffn2_wu_grad_{i}", **wide)
        g["ffn2_wd"][i] = matmul(act, dz3, mode="tn", scale=0.5, name=f"ffn2_wd_grad_{i}", **tall)
        dz2, dg1, db1 = ln_bwd(dy2, s["z2"], ln_g[i, 1], name=f"ln1_bwd_{i}")
        if i == 0:
            dmix = matmul(dz2, w["ab_w_out"][0], mode="nt", name="ab_out_bwd")
            g["ab_w_out"] = matmul(s["mix"], dz2, mode="tn", name="ab_out_grad")
            dq, dk, dv, dsk = attn_bwd(s["proj"], a["a_sinks"][0], dmix, bsz, name="attn_bwd")
            dbx, dbgate, dcw, dcb, dwa, dba, dwx, dbxb, dlam = lru_bwd(s["proj"], *lru_w, dmix, bsz, name="lru_bwd")
            dproj = jnp.concatenate([dq, dk, dv, dbx, dbgate], axis=1).astype(MM)
            dy1 = matmul(dproj, w["ab_w_in"][0], mode="nt", add=dz2, add_scale=DN_ALPHA, name="ab_in_bwd")
            g["ab_w_in"] = matmul(s["y1"], dproj, mode="tn", name="ab_in_grad")
        else:
            dmix = matmul(dz2, w["c_w_out"][0], mode="nt", name="c_out_bwd")
            g["c_w_out"] = matmul(s["mix"], dz2, mode="tn", name="c_out_grad")
            dqkv, dzc, dgates, dsm = gdc_bwd(s["qkv"], s["proj"], s["gates"], *gdc_w, s["o_pre"], s["vnew"], s["tmat"],
                                             s["states"], dmix, bsz, name="gdc_bwd")
            draw, dccw = gdc_pre_bwd(s["proj"], w["c_conv_w"][0], dqkv, bsz, name="gdc_pre_bwd")
            dproj = jnp.concatenate([draw, dzc], axis=1).astype(MM)
            dgb = dgates.astype(MM)
            dy1 = matmul(dproj, c_in_main, mode="nt", add=dz2, add_scale=DN_ALPHA, name="c_in_bwd")
            dy1 = matmul(dgb, c_in_gate, mode="nt", add=dy1, name="c_gate_bwd")
            g["c_w_in"] = jnp.concatenate([matmul(s["y1"], dproj, mode="tn", name="c_in_grad"),
                                           matmul(s["y1"], dgb, mode="tn", name="c_gate_grad")[:, :2 * C_HEADS]], axis=1)
        dz1, dg0, db0 = ln_bwd(dy1, s["z1"], ln_g[i, 0], name=f"ln0_bwd_{i}")
        dh, act, dhg, dhu = ffn_bwd(dz1, s["hg1"], s["hu1"], w["ffn1_wg"][i], w["ffn1_wu"][i], w["ffn1_wd"][i], name=f"ffn1_bwd_{i}", tm=FFN_BWD_TM)
        g["ffn1_wg"][i] = matmul(s["x0"], dhg, mode="tn", name=f"ffn1_wg_grad_{i}", **wide)
        g["ffn1_wu"][i] = matmul(s["x0"], dhu, mode="tn", name=f"ffn1_wu_grad_{i}", **wide)
        g["ffn1_wd"][i] = matmul(act, dz1, mode="tn", scale=0.5, name=f"ffn1_wd_grad_{i}", **tall)
        g["ln_g"][i] = jnp.concatenate([dg0, dg1, dg2], axis=0)
        g["ln_b"][i] = jnp.concatenate([db0, db1, db2], axis=0)
    grad_x = dh.reshape(bsz, s_len, d)
    full = {n: jnp.stack(v) if isinstance(v, list) else v[None] for n, v in g.items()}
    full["b_conv_w"] = dcw[None]
    full["c_conv_w"] = jnp.sum(dccw, axis=0)[None]
    dsm_sum = jnp.sum(dsm, axis=0)
    full.update(a_sinks=jnp.sum(dsk, axis=0)[:, :A_HEADS], b_conv_b=dcb, b_wa=_diag_blocks(dwa)[None], b_ba=dba,
                b_wx=_diag_blocks(dwx)[None], b_bx=dbxb, b_lam=dlam, c_a_log=dsm_sum[None, :C_HEADS, 0],
                c_dt_bias=dsm_sum[None, :C_HEADS, 1], c_norm_g=jnp.sum(dsm_sum[C_HEADS:], axis=0)[None])

    small_cols = SMALL_F32 * bits_per // PACK_COLS
    repl_flat = _flat_pad([full[n] for n in REPL], F32, SMALL_F32 - n_small)
    small8 = jnp.concatenate([_flat8_pad([_split(full[n], ax) for n, ax in SMALL], F32, n_small),
                              jnp.broadcast_to(repl_flat, (N_DEV,) + repl_flat.shape)], axis=1)
    parts = jnp.concatenate([
        _flat8_pad([_split(full[n], ax) for n, ax in BIG], MM, BIG_ROWS * PACK_COLS).reshape(N_DEV, BIG_ROWS, PACK_COLS),
        _bits(small8).reshape(N_DEV, small_cols, PACK_COLS)], axis=1)
    recv = all_to_all(parts, name="exchange_grads")

    def mine(prefix, names, dtype_total):
        return _flat_pad([a[prefix + n] for n in names], F32, dtype_total)

    outs = {}
    big_total = BIG_ROWS * PACK_COLS
    res_big = adamw_sum(recv, *[mine(pre, big_names, big_total).reshape(BIG_ROWS, PACK_COLS) for pre in ("", "m_", "v_")],
                        name="adamw_big")
    small_all = small_names + REPL
    cols_f32 = PACK_COLS // bits_per
    res_small = adamw_sum(_unbits(recv[:, BIG_ROWS:]).reshape(N_DEV, small_cols, cols_f32),
                          *[mine(pre, small_all, SMALL_F32).reshape(small_cols, cols_f32) for pre in ("", "m_", "v_")],
                          name="adamw_small")
    kinds = []
    for rb, rs in zip(res_big, res_small):
        k = _take(rb.reshape(-1), big_names, shapes)
        k.update(_take(rs.reshape(-1), small_all, shapes))
        kinds.append(k)
    loss = lax.psum(loss_part[0, 0], ("x", "y", "c"))
    return (loss, grad_x, *[kinds[0][n] for n in WEIGHTS], *[kinds[1][n] for n in WEIGHTS],
            *[kinds[2][n] for n in WEIGHTS], *[kinds[3][n] for n in WEIGHTS])
```

```python
import functools
import math

import numpy as np
import jax
import jax.numpy as jnp
from jax import lax
from jax.experimental import pallas as pl
from jax.experimental.pallas import tpu as pltpu

F32 = jnp.float32
MM = jnp.bfloat16
HI = lax.Precision.HIGHEST

D_MODEL = 1024
D_FF = 2816
D_PLE = 256
DEPTH = 2
CHUNK = 64
A_HEADS = 8
A_KV_HEADS = 2
A_GROUP = 4
A_HEAD_DIM = 64
A_WIDTH = 512
A_KV_WIDTH = 128
B_WIDTH = 512
B_BLOCK = 64
RG_C = 8.0
AB_PROJ = 1792
C_HEADS = 8
C_HEAD_DIM = 128
C_WIDTH = 1024
DN_ALPHA = (2.0 * DEPTH) ** 0.25
LN_EPS = 1e-5
NORM_EPS = 1e-6
NEG = -1e30
ADAM_LR = 0.001
ADAM_B1 = 0.9
ADAM_B2 = 0.999
ADAM_EPS = 1e-08
ADAM_WD = 0.01
ADAM_STEP = 10
N_DEV = 8
VMEM_LIMIT = 56 * 1024 * 1024

NN = ((1,), (0,))
NT = ((1,), (1,))
TN = ((0,), (0,))


def _pcall(body, **kw):
    return pl.pallas_call(body, **kw)


def _cp(*sem):
    return pltpu.CompilerParams(dimension_semantics=sem, vmem_limit_bytes=VMEM_LIMIT)


MESH_ID = pl.DeviceIdType.MESH
_FLIPS = [(0, 0, 1), (1, 0, 0), (0, 1, 0), (1, 1, 0), (1, 0, 1), (0, 1, 1), (1, 1, 1)]


def _me():
    return lax.axis_index("x"), lax.axis_index("y"), lax.axis_index("c")


def _flip(coord, d):
    return 1 - coord if d else coord


def _side_copies(kind, x_refs, o_refs, send_sems, recv_sems, local_sems, start):
    x, y, c = _me()
    mine = 4 * x + 2 * y + c
    for gi, (x_ref, o_ref) in enumerate(zip(x_refs, o_refs)):
        src_own = x_ref if kind == "gather" else x_ref.at[mine]
        own = pltpu.make_async_copy(src_own, o_ref.at[mine], local_sems.at[gi])
        own.start() if start else own.wait()
        for k, (dx, dy, dc) in enumerate(_FLIPS):
            px, py, pc = _flip(x, dx), _flip(y, dy), _flip(c, dc)
            src = x_ref if kind == "gather" else x_ref.at[4 * px + 2 * py + pc]
            cp = pltpu.make_async_remote_copy(
                src_ref=src, dst_ref=o_ref.at[mine], send_sem=send_sems.at[7 * gi + k], recv_sem=recv_sems.at[7 * gi + k],
                device_id=(px, py, pc), device_id_type=MESH_ID)
            cp.start() if start else cp.wait()


def _call(body, args, side, grid, **kw):
    if side is None:
        return _pcall(body, grid=grid, **kw)(*args)
    kind, arrs = side
    ns, n_in, n_out = len(arrs), len(args), len(kw["out_specs"])
    scratch = list(kw.get("scratch_shapes", []))
    n_scr = len(scratch)

    def edge(at_end):
        conds = [pl.program_id(ax) == (n - 1 if at_end else 0) for ax, n in enumerate(grid)]
        return functools.reduce(jnp.logical_and, conds)

    def wrapped(*refs):
        ins, sx = refs[:n_in], refs[n_in:n_in + ns]
        outs, so = refs[n_in + ns:n_in + ns + n_out], refs[n_in + ns + n_out:n_in + 2 * ns + n_out]
        rest = refs[n_in + 2 * ns + n_out:]
        scr, sems = rest[:n_scr], rest[n_scr:]

        @pl.when(edge(False))
        def _():
            _side_copies(kind, sx, so, *sems, start=True)

        body(*ins, *outs, *scr)

        @pl.when(edge(True))
        def _():
            _side_copies(kind, sx, so, *sems, start=False)

    hbm = pl.BlockSpec(memory_space=pl.ANY)
    side_shapes = [jax.ShapeDtypeStruct(((N_DEV,) if kind == "gather" else ()) + z.shape, z.dtype) for z in arrs]
    kw = dict(kw)
    kw["in_specs"] = list(kw["in_specs"]) + [hbm] * ns
    kw["out_specs"] = list(kw["out_specs"]) + [hbm] * ns
    kw["out_shape"] = list(kw["out_shape"]) + side_shapes
    kw["scratch_shapes"] = scratch + [pltpu.SemaphoreType.DMA((7 * ns,)), pltpu.SemaphoreType.DMA((7 * ns,)),
                                      pltpu.SemaphoreType.DMA((ns,))]
    kw["compiler_params"] = _cp(*["arbitrary"] * len(grid))
    res = _pcall(wrapped, grid=grid, **kw)(*args, *arrs)
    return list(res[:n_out]), list(res[n_out:])


def _dot(a, b, dims=NN, precision=None):
    return lax.dot_general(a, b, (dims, ((), ())), preferred_element_type=F32, precision=precision)


def _mdot(a, b, dims=NN):
    return _dot(a.astype(MM), b.astype(MM), dims)


def _tile(n, pref):
    if n <= pref:
        return n
    for c in range(pref - pref % 128, 0, -128):
        if n % c == 0:
            return c
    return n


def _sigmoid(x):
    return 1.0 / (1.0 + jnp.exp(-x))


def _softplus(x):
    return jnp.maximum(x, 0.0) + jnp.log(1.0 + jnp.exp(-jnp.abs(x)))


def _ln_stats(z):
    mu = jnp.mean(z, axis=-1, keepdims=True)
    zc = z - mu
    var = jnp.mean(zc * zc, axis=-1, keepdims=True)
    return zc, lax.rsqrt(var + LN_EPS)


def matmul(a, b, *, mode, name, tm=512, tn=512, tk=512, out_dtype=F32, scale=None, add=None, add_scale=1.0, side=None):
    if mode == "nn":
        (m, kk), (_, n) = a.shape, b.shape
        dims = NN
    elif mode == "nt":
        (m, kk), (n, _) = a.shape, b.shape
        dims = NT
    else:
        (kk, m), (_, n) = a.shape, b.shape
        dims = TN
    tm, tn, tk = _tile(m, tm), _tile(n, tn), _tile(kk, tk)
    if mode == "nn":
        a_spec = pl.BlockSpec((tm, tk), lambda i, j, k: (i, k))
        b_spec = pl.BlockSpec((tk, tn), lambda i, j, k: (k, j))
    elif mode == "nt":
        a_spec = pl.BlockSpec((tm, tk), lambda i, j, k: (i, k))
        b_spec = pl.BlockSpec((tn, tk), lambda i, j, k: (j, k))
    else:
        a_spec = pl.BlockSpec((tk, tm), lambda i, j, k: (k, i))
        b_spec = pl.BlockSpec((tk, tn), lambda i, j, k: (k, j))
    nk = kk // tk
    o_spec = pl.BlockSpec((tm, tn), lambda i, j, k: (i, j))
    has_add = add is not None

    def body(*refs):
        if has_add:
            a_ref, b_ref, add_ref, o_ref, acc_ref = refs
        else:
            a_ref, b_ref, o_ref, acc_ref = refs
        k = pl.program_id(2)

        @pl.when(k == 0)
        def _():
            acc_ref[...] = jnp.zeros_like(acc_ref)

        acc_ref[...] += _mdot(a_ref[...], b_ref[...], dims)

        @pl.when(k == nk - 1)
        def _():
            r = acc_ref[...]
            if scale is not None:
                r = r * scale
            if has_add:
                r = r + add_scale * add_ref[...].astype(F32)
            o_ref[...] = r.astype(out_dtype)

    ins = [a, b] + ([add] if has_add else [])
    in_specs = [a_spec, b_spec] + ([o_spec] if has_add else [])
    res = _call(
        body, ins, side, (m // tm, n // tn, nk), name=name, in_specs=in_specs, out_specs=[o_spec],
        out_shape=[jax.ShapeDtypeStruct((m, n), out_dtype)], scratch_shapes=[pltpu.VMEM((tm, tn), F32)],
        compiler_params=_cp("parallel", "parallel", "arbitrary"),
    )
    return res[0] if side is None else (res[0][0], res[1])


def ffn_fwd(x, wg, wu, wd, g, b, *, name, tm=512, tf=256, side=None):
    t, d = x.shape
    f = wg.shape[1]
    tm = min(tm, t)
    nj = f // tf

    def body(x_ref, wg_ref, wu_ref, wd_ref, g_ref, b_ref, y_ref, z_ref, hg_ref, hu_ref, xb_ref, acc_ref):
        j = pl.program_id(1)

        @pl.when(j == 0)
        def _():
            xb_ref[...] = x_ref[...].astype(MM)
            acc_ref[...] = jnp.zeros_like(acc_ref)

        xb = xb_ref[...]
        hg = _dot(xb, wg_ref[...])
        hu = _dot(xb, wu_ref[...])
        hg_ref[...] = hg.astype(MM)
        hu_ref[...] = hu.astype(MM)
        act = (hg * _sigmoid(hg) * hu).astype(MM)
        acc_ref[...] += _dot(act, wd_ref[...])

        @pl.when(j == nj - 1)
        def _():
            z = DN_ALPHA * x_ref[...] + 0.5 * acc_ref[...]
            z_ref[...] = z
            zc, rstd = _ln_stats(z)
            y_ref[...] = zc * rstd * g_ref[...] + b_ref[...]

    row = pl.BlockSpec((tm, d), lambda i, j: (i, 0))
    hid = pl.BlockSpec((tm, tf), lambda i, j: (i, j))
    vec = pl.BlockSpec((1, d), lambda i, j: (0, 0))
    return _call(
        body, (x, wg, wu, wd, g.reshape(1, d), b.reshape(1, d)), side, (t // tm, nj), name=name,
        in_specs=[row, pl.BlockSpec((d, tf), lambda i, j: (0, j)), pl.BlockSpec((d, tf), lambda i, j: (0, j)),
                  pl.BlockSpec((tf, d), lambda i, j: (j, 0)), vec, vec],
        out_specs=[row, row, hid, hid],
        out_shape=[jax.ShapeDtypeStruct((t, d), F32), jax.ShapeDtypeStruct((t, d), F32),
                   jax.ShapeDtypeStruct((t, f), MM), jax.ShapeDtypeStruct((t, f), MM)],
        scratch_shapes=[pltpu.VMEM((tm, d), MM), pltpu.VMEM((tm, d), F32)],
        compiler_params=_cp("parallel", "arbitrary"),
    )


def ffn_bwd(dz, hg, hu, wg, wu, wd, *, name, tm=512, tf=256, side=None):
    t, d = dz.shape
    f = wg.shape[1]
    tm = min(tm, t)
    nj = f // tf

    def body(dz_ref, hg_ref, hu_ref, wg_ref, wu_ref, wd_ref, dx_ref, act_ref, dhg_ref, dhu_ref, dfb_ref, acc_ref):
        j = pl.program_id(1)

        @pl.when(j == 0)
        def _():
            dfb_ref[...] = (0.5 * dz_ref[...]).astype(MM)
            acc_ref[...] = jnp.zeros_like(acc_ref)

        hg = hg_ref[...].astype(F32)
        hu = hu_ref[...].astype(F32)
        s = _sigmoid(hg)
        dact = _dot(dfb_ref[...], wd_ref[...], NT)
        sg = hg * s
        act_ref[...] = (sg * hu).astype(MM)
        dhu = (dact * sg).astype(MM)
        dhg = (dact * hu * (s + sg * (1.0 - s))).astype(MM)
        dhu_ref[...] = dhu
        dhg_ref[...] = dhg
        acc_ref[...] += _dot(dhg, wg_ref[...], NT) + _dot(dhu, wu_ref[...], NT)

        @pl.when(j == nj - 1)
        def _():
            dx_ref[...] = DN_ALPHA * dz_ref[...] + acc_ref[...]

    row = pl.BlockSpec((tm, d), lambda i, j: (i, 0))
    hid = pl.BlockSpec((tm, tf), lambda i, j: (i, j))
    return _call(
        body, (dz, hg, hu, wg, wu, wd), side, (t // tm, nj), name=name,
        in_specs=[row, hid, hid, pl.BlockSpec((d, tf), lambda i, j: (0, j)), pl.BlockSpec((d, tf), lambda i, j: (0, j)),
                  pl.BlockSpec((tf, d), lambda i, j: (j, 0))],
        out_specs=[row, hid, hid, hid],
        out_shape=[jax.ShapeDtypeStruct((t, d), F32)] + [jax.ShapeDtypeStruct((t, f), MM)] * 3,
        scratch_shapes=[pltpu.VMEM((tm, d), MM), pltpu.VMEM((tm, d), F32)],
        compiler_params=_cp("parallel", "arbitrary"),
    )


def ln_bwd(dy, z, g, *, name, tm=512):
    t, d = z.shape
    tm = min(tm, t)

    def body(dy_ref, z_ref, g_ref, dz_ref, dg_ref, db_ref):
        i = pl.program_id(0)

        @pl.when(i == 0)
        def _():
            dg_ref[...] = jnp.zeros_like(dg_ref)
            db_ref[...] = jnp.zeros_like(db_ref)

        dy = dy_ref[...]
        zc, rstd = _ln_stats(z_ref[...])
        xh = zc * rstd
        dg_ref[...] += jnp.sum(dy * xh, axis=0, keepdims=True)
        db_ref[...] += jnp.sum(dy, axis=0, keepdims=True)
        dxh = dy * g_ref[...]
        m1 = jnp.mean(dxh, axis=-1, keepdims=True)
        m2 = jnp.mean(dxh * xh, axis=-1, keepdims=True)
        dz_ref[...] = rstd * (dxh - m1 - xh * m2)

    row = pl.BlockSpec((tm, d), lambda i: (i, 0))
    vec = pl.BlockSpec((1, d), lambda i: (0, 0))
    return _pcall(
        body, name=name, grid=(t // tm,), in_specs=[row, row, vec], out_specs=[row, vec, vec],
        out_shape=[jax.ShapeDtypeStruct((t, d), F32), jax.ShapeDtypeStruct((1, d), F32), jax.ShapeDtypeStruct((1, d), F32)],
        compiler_params=_cp("arbitrary"),
    )(dy, z, g.reshape(1, d))


def mm_ln_fwd(a, w, res, g, b, *, name, tm=512):
    t, kk = a.shape
    d = w.shape[1]
    tm = min(tm, t)

    def body(a_ref, w_ref, res_ref, g_ref, b_ref, y_ref, z_ref):
        z = DN_ALPHA * res_ref[...] + _mdot(a_ref[...], w_ref[...])
        z_ref[...] = z
        zc, rstd = _ln_stats(z)
        y_ref[...] = zc * rstd * g_ref[...] + b_ref[...]

    row = pl.BlockSpec((tm, d), lambda i: (i, 0))
    vec = pl.BlockSpec((1, d), lambda i: (0, 0))
    return _pcall(
        body, name=name, grid=(t // tm,),
        in_specs=[pl.BlockSpec((tm, kk), lambda i: (i, 0)), pl.BlockSpec((kk, d), lambda i: (0, 0)), row, vec, vec],
        out_specs=[row, row],
        out_shape=[jax.ShapeDtypeStruct((t, d), F32), jax.ShapeDtypeStruct((t, d), F32)],
        compiler_params=_cp("parallel"),
    )(a, w, res, g.reshape(1, d), b.reshape(1, d))


def ple_fwd(y, p, wg, bg, wp, *, name, tm=512):
    t, d = y.shape
    dp = p.shape[1]
    tm = min(tm, t)

    def body(y_ref, p_ref, wg_ref, bg_ref, wp_ref, o_ref):
        yv = y_ref[...]
        gate = _sigmoid(_mdot(yv, wg_ref[...]) + bg_ref[...])
        o_ref[...] = yv + gate * _mdot(p_ref[...], wp_ref[...])

    row = pl.BlockSpec((tm, d), lambda i: (i, 0))
    return _pcall(
        body, name=name, grid=(t // tm,),
        in_specs=[row, pl.BlockSpec((tm, dp), lambda i: (i, 0)), pl.BlockSpec((d, d), lambda i: (0, 0)),
                  pl.BlockSpec((1, d), lambda i: (0, 0)), pl.BlockSpec((dp, d), lambda i: (0, 0))],
        out_specs=row, out_shape=jax.ShapeDtypeStruct((t, d), F32), compiler_params=_cp("parallel"),
    )(y, p, wg, bg.reshape(1, d), wp)


def ple_bwd(do, y, p, wg, bg, wp, *, name, tm=512):
    t, d = y.shape
    dp = p.shape[1]
    tm = min(tm, t)

    def body(do_ref, y_ref, p_ref, wg_ref, bg_ref, wp_ref, dy_ref, dt_ref, de_ref, dbg_ref):
        i = pl.program_id(0)

        @pl.when(i == 0)
        def _():
            dbg_ref[...] = jnp.zeros_like(dbg_ref)

        dov = do_ref[...]
        gate = _sigmoid(_mdot(y_ref[...], wg_ref[...]) + bg_ref[...])
        emb = _mdot(p_ref[...], wp_ref[...])
        dt = dov * emb * gate * (1.0 - gate)
        dbg_ref[...] += jnp.sum(dt, axis=0, keepdims=True)
        dtb = dt.astype(MM)
        dt_ref[...] = dtb
        de_ref[...] = (dov * gate).astype(MM)
        dy_ref[...] = dov + _dot(dtb, wg_ref[...], NT)

    row = pl.BlockSpec((tm, d), lambda i: (i, 0))
    vec = pl.BlockSpec((1, d), lambda i: (0, 0))
    return _pcall(
        body, name=name, grid=(t // tm,),
        in_specs=[row, row, pl.BlockSpec((tm, dp), lambda i: (i, 0)), pl.BlockSpec((d, d), lambda i: (0, 0)),
                  vec, pl.BlockSpec((dp, d), lambda i: (0, 0))],
        out_specs=[row, row, row, vec],
        out_shape=[jax.ShapeDtypeStruct((t, d), F32), jax.ShapeDtypeStruct((t, d), MM),
                   jax.ShapeDtypeStruct((t, d), MM), jax.ShapeDtypeStruct((1, d), F32)],
        compiler_params=_cp("arbitrary"),
    )(do, y, p, wg, bg.reshape(1, d), wp)


def loss_fwd_bwd(y, tgt, *, name, tm=512):
    t, d = y.shape
    tm = min(tm, t)

    def body(y_ref, t_ref, l_ref, dy_ref):
        i = pl.program_id(0)

        @pl.when(i == 0)
        def _():
            l_ref[...] = jnp.zeros_like(l_ref)

        err = y_ref[...] - t_ref[...]
        dy_ref[...] = err * (1.0 / d)
        l_ref[...] += (0.5 / d) * jnp.sum(jnp.sum(err * err, axis=1, keepdims=True), axis=0, keepdims=True)

    row = pl.BlockSpec((tm, d), lambda i: (i, 0))
    return _pcall(
        body, name=name, grid=(t // tm,), in_specs=[row, row],
        out_specs=[pl.BlockSpec((1, 128), lambda i: (0, 0)), row],
        out_shape=[jax.ShapeDtypeStruct((1, 128), F32), jax.ShapeDtypeStruct((t, d), F32)],
        compiler_params=_cp("arbitrary"),
    )(y, tgt)


def _shift_dn(x, s, row):
    return x if s == 0 else jnp.where(row >= s, pltpu.roll(x, s, 0), 0.0)


def _shift_up(x, s, row):
    n = x.shape[0]
    return x if s == 0 else jnp.where(row < n - s, pltpu.roll(x, n - s, 0), 0.0)


def _conv_fwd(x, w, row):
    kk = w.shape[0]
    y = w[kk - 1:kk, :] * x
    for j in range(kk - 1):
        y = y + w[j:j + 1, :] * _shift_dn(x, kk - 1 - j, row)
    return y


def _conv_bwd(x, w, dy, row):
    kk = w.shape[0]
    dx = w[kk - 1:kk, :] * dy
    dws = []
    for j in range(kk - 1):
        dx = dx + w[j:j + 1, :] * _shift_up(dy, kk - 1 - j, row)
        dws.append(jnp.sum(dy * _shift_dn(x, kk - 1 - j, row), axis=0, keepdims=True))
    dws.append(jnp.sum(dy * x, axis=0, keepdims=True))
    return dx, jnp.concatenate(dws, axis=0)


def _gelu(x):
    c = math.sqrt(2.0 / math.pi)
    th = jnp.tanh(c * (x + 0.044715 * x * x * x))
    return 0.5 * x * (1.0 + th), th


def _gelu_grad(x, th):
    c = math.sqrt(2.0 / math.pi)
    return 0.5 * (1.0 + th) + 0.5 * x * (1.0 - th * th) * c * (1.0 + 3.0 * 0.044715 * x * x)


def _neg_expm1(y):
    ser = -(y * (1.0 + y * (0.5 + y * (1.0 / 6.0 + y * (1.0 / 24.0 + y * (1.0 / 120.0))))))
    return jnp.where(y > -0.05, ser, 1.0 - jnp.exp(y))


def _attn_head(qh, kk, vv, bias, valid, sink):
    s = _mdot(qh, kk, NT) * (A_HEAD_DIM ** -0.5) - bias
    s = jnp.where(valid, s, NEG)
    m = jnp.maximum(jnp.max(s, axis=-1, keepdims=True), sink)
    pr = jnp.exp(s - m)
    den = jnp.sum(pr, axis=-1, keepdims=True) + jnp.exp(sink - m)
    return pr / den, jnp.exp(sink - m) / den


def _attn_valid(n):
    ji = lax.broadcasted_iota(jnp.int32, (1, 3 * CHUNK), 1)
    return (n * CHUNK + ji - 2 * CHUNK) >= 0


def _attn_group_consts(kh, sk_ref):
    rows = A_GROUP * CHUNK
    ri = lax.broadcasted_iota(jnp.int32, (rows, 3 * CHUNK), 0)
    ji = lax.broadcasted_iota(jnp.int32, (rows, 3 * CHUNK), 1)
    dist = jnp.abs((ri & (CHUNK - 1)) + 2 * CHUNK - ji).astype(F32)
    rcol = lax.broadcasted_iota(jnp.int32, (rows, 1), 0)
    slope = jnp.zeros((rows, 1), F32)
    sink = jnp.zeros((rows, 1), F32)
    for gi in range(A_GROUP):
        h = kh * A_GROUP + gi
        inblk = (rcol >= gi * CHUNK) & (rcol < (gi + 1) * CHUNK)
        slope = jnp.where(inblk, 2.0 ** -(h + 1), slope)
        sink = jnp.where(inblk, sk_ref[h], sink)
    return slope * dist, sink


def _stack_heads(x, kh):
    return jnp.concatenate([x[:, (kh * A_GROUP + gi) * 64:(kh * A_GROUP + gi + 1) * 64] for gi in range(A_GROUP)], axis=0)


def _attn_masks(n):
    ci = lax.broadcasted_iota(jnp.int32, (CHUNK, 3 * CHUNK), 0)
    ji = lax.broadcasted_iota(jnp.int32, (CHUNK, 3 * CHUNK), 1)
    dist = jnp.abs(ci + 2 * CHUNK - ji).astype(F32)
    valid = (n * CHUNK + ji - 2 * CHUNK) >= 0
    return dist, valid


def attn_fwd(proj, sinks, bsz, *, name):
    t = proj.shape[0]
    s_len = t // bsz
    nc = s_len // CHUNK
    pad = 2 * CHUNK

    def body(q_ref, k_ref, v_ref, sk_ref, o_ref, kp_ref, vp_ref):
        kp_ref[0:pad, :] = jnp.zeros((pad, A_KV_WIDTH), F32)
        vp_ref[0:pad, :] = jnp.zeros((pad, A_KV_WIDTH), F32)
        kp_ref[pad:, :] = k_ref[...]
        vp_ref[pad:, :] = v_ref[...]

        consts = [_attn_group_consts(kh, sk_ref) for kh in range(A_KV_HEADS)]

        def chunk(n, carry):
            st = pl.multiple_of(n * CHUNK, CHUNK)
            q = q_ref[pl.ds(st, CHUNK), :]
            kb = kp_ref[pl.ds(st, 3 * CHUNK), :]
            vb = vp_ref[pl.ds(st, 3 * CHUNK), :]
            valid = _attn_valid(n)
            outs = []
            for kh in range(A_KV_HEADS):
                bias, sink = consts[kh]
                pn, _ = _attn_head(_stack_heads(q, kh), kb[:, kh * 64:(kh + 1) * 64], None, bias, valid, sink)
                o = _mdot(pn, vb[:, kh * 64:(kh + 1) * 64])
                outs += [o[gi * CHUNK:(gi + 1) * CHUNK] for gi in range(A_GROUP)]
            o_ref[pl.ds(st, CHUNK), :] = jnp.concatenate(outs, axis=-1)
            return carry

        lax.fori_loop(0, nc, chunk, 0)

    return _pcall(
        body, name=name, grid=(bsz,),
        in_specs=[pl.BlockSpec((s_len, A_WIDTH), lambda b: (b, 0)), pl.BlockSpec((s_len, 128), lambda b: (b, 4)),
                  pl.BlockSpec((s_len, 128), lambda b: (b, 5)), pl.BlockSpec(memory_space=pltpu.SMEM)],
        out_specs=pl.BlockSpec((s_len, A_WIDTH), lambda b: (b, 0)),
        out_shape=jax.ShapeDtypeStruct((t, A_WIDTH), F32),
        scratch_shapes=[pltpu.VMEM((s_len + pad, A_KV_WIDTH), F32), pltpu.VMEM((s_len + pad, A_KV_WIDTH), F32)],
        compiler_params=_cp("parallel"),
    )(proj, proj, proj, sinks)


def attn_bwd(proj, sinks, dcat, bsz, *, name, side=None):
    t = proj.shape[0]
    s_len = t // bsz
    nc = s_len // CHUNK
    pad = 2 * CHUNK

    def body(q_ref, k_ref, v_ref, do_ref, sk_ref, dq_ref, dk_ref, dv_ref, dsk_ref, kp_ref, vp_ref, dkp_ref, dvp_ref):
        kp_ref[0:pad, :] = jnp.zeros((pad, A_KV_WIDTH), F32)
        vp_ref[0:pad, :] = jnp.zeros((pad, A_KV_WIDTH), F32)
        kp_ref[pad:, :] = k_ref[...]
        vp_ref[pad:, :] = v_ref[...]
        dkp_ref[...] = jnp.zeros_like(dkp_ref)
        dvp_ref[...] = jnp.zeros_like(dvp_ref)
        lane = lax.broadcasted_iota(jnp.int32, (1, 128), 1)

        consts = [_attn_group_consts(kh, sk_ref) for kh in range(A_KV_HEADS)]

        def chunk(n, dsk):
            st = pl.multiple_of(n * CHUNK, CHUNK)
            q = q_ref[pl.ds(st, CHUNK), :]
            do = do_ref[pl.ds(st, CHUNK), :]
            kb = kp_ref[pl.ds(st, 3 * CHUNK), :]
            vb = vp_ref[pl.ds(st, 3 * CHUNK), :]
            valid = _attn_valid(n)
            dqs, dks, dvs = [], [], []
            for kh in range(A_KV_HEADS):
                kk = kb[:, kh * 64:(kh + 1) * 64]
                vv = vb[:, kh * 64:(kh + 1) * 64]
                bias, sink = consts[kh]
                qs = _stack_heads(q, kh)
                dos = _stack_heads(do, kh)
                pn, psink = _attn_head(qs, kk, None, bias, valid, sink)
                dp = _mdot(dos, vv, NT)
                rowdot = jnp.sum(pn * dp, axis=-1, keepdims=True)
                ds = pn * (dp - rowdot)
                sink_part = psink * rowdot
                for gi in range(A_GROUP):
                    part = jnp.sum(sink_part[gi * CHUNK:(gi + 1) * CHUNK], axis=0, keepdims=True)
                    dsk = dsk + jnp.where(lane == kh * A_GROUP + gi, -part, 0.0)
                dq = _mdot(ds, kk) * (A_HEAD_DIM ** -0.5)
                dqs += [dq[gi * CHUNK:(gi + 1) * CHUNK] for gi in range(A_GROUP)]
                dks.append(_mdot(ds, qs, TN) * (A_HEAD_DIM ** -0.5))
                dvs.append(_mdot(pn, dos, TN))
            dq_ref[pl.ds(st, CHUNK), :] = jnp.concatenate(dqs, axis=-1)
            dkp_ref[pl.ds(st, 3 * CHUNK), :] += jnp.concatenate(dks, axis=-1)
            dvp_ref[pl.ds(st, 3 * CHUNK), :] += jnp.concatenate(dvs, axis=-1)
            return dsk

        dsk = lax.fori_loop(0, nc, chunk, jnp.zeros((1, 128), F32))
        dsk_ref[0] = dsk
        dk_ref[...] = dkp_ref[pad:, :]
        dv_ref[...] = dvp_ref[pad:, :]

    kv = jax.ShapeDtypeStruct((t, A_KV_WIDTH), F32)
    return _call(
        body, (proj, proj, proj, dcat, sinks), side, (bsz,), name=name,
        in_specs=[pl.BlockSpec((s_len, A_WIDTH), lambda b: (b, 0)), pl.BlockSpec((s_len, 128), lambda b: (b, 4)),
                  pl.BlockSpec((s_len, 128), lambda b: (b, 5)), pl.BlockSpec((s_len, A_WIDTH), lambda b: (b, 0)),
                  pl.BlockSpec(memory_space=pltpu.SMEM)],
        out_specs=[pl.BlockSpec((s_len, A_WIDTH), lambda b: (b, 0)), pl.BlockSpec((s_len, 128), lambda b: (b, 0)),
                   pl.BlockSpec((s_len, 128), lambda b: (b, 0)), pl.BlockSpec((1, 1, 128), lambda b: (b, 0, 0))],
        out_shape=[jax.ShapeDtypeStruct((t, A_WIDTH), F32), kv, kv, jax.ShapeDtypeStruct((bsz, 1, 128), F32)],
        scratch_shapes=[pltpu.VMEM((s_len + pad, A_KV_WIDTH), F32)] * 4,
        compiler_params=_cp("parallel"),
    )


def _lru_gates(x, cw, cb, wa, ba, wx, bx, lam, row):
    xc = _conv_fwd(x, cw, row) + cb
    r = _sigmoid(_mdot(xc, wa) + ba)
    i = _sigmoid(_mdot(xc, wx) + bx)
    sp = _softplus(-lam)
    log_a = -RG_C * r * sp
    a = jnp.exp(log_a)
    mult = jnp.sqrt(_neg_expm1(2.0 * log_a))
    return xc, r, i, sp, a, mult


def _lru_scan(a, u, row):
    n = a.shape[0]
    d = 1
    while d < n:
        a_sh = jnp.where(row >= d, pltpu.roll(a, d, 0), 1.0)
        u_sh = jnp.where(row >= d, pltpu.roll(u, d, 0), 0.0)
        u = a * u_sh + u
        a = a * a_sh
        d *= 2
    return u


def _lru_scan_rev(a, u, row):
    n = a.shape[0]
    d = 1
    while d < n:
        a_sh = jnp.where(row < n - d, pltpu.roll(a, n - d, 0), 1.0)
        u_sh = jnp.where(row < n - d, pltpu.roll(u, n - d, 0), 0.0)
        u = a * u_sh + u
        a = a * a_sh
        d *= 2
    return u


def _lru_specs(s_len, order):
    def at(f):
        return lambda *g: f(*order(*g))
    return [pl.BlockSpec((s_len, 128), at(lambda b, cb: (b, 6 + cb))), pl.BlockSpec((s_len, 128), at(lambda b, cb: (b, 10 + cb))),
            pl.BlockSpec((4, 128), at(lambda b, cb: (0, cb))), pl.BlockSpec((1, 128), at(lambda b, cb: (0, cb))),
            pl.BlockSpec((1, 128, 128), at(lambda b, cb: (cb, 0, 0))), pl.BlockSpec((1, 128), at(lambda b, cb: (0, cb))),
            pl.BlockSpec((1, 128, 128), at(lambda b, cb: (cb, 0, 0))), pl.BlockSpec((1, 128), at(lambda b, cb: (0, cb))),
            pl.BlockSpec((1, 128), at(lambda b, cb: (0, cb)))]


def lru_fwd(proj, cw, cb, wa, ba, wx, bxb, lam, bsz, *, name):
    t = proj.shape[0]
    s_len = t // bsz

    def body(x_ref, g_ref, cw_ref, cb_ref, wa_ref, ba_ref, wx_ref, bx_ref, lam_ref, y_ref):
        row = lax.broadcasted_iota(jnp.int32, (s_len, 128), 0)
        xc, r, i, sp, a, mult = _lru_gates(x_ref[...], cw_ref[...], cb_ref[...], wa_ref[0], ba_ref[...], wx_ref[0],
                                           bx_ref[...], lam_ref[...], row)
        h = _lru_scan(a, mult * (i * xc), row)
        y_ref[...] = h * _gelu(g_ref[...])[0]

    return _pcall(
        body, name=name, grid=(bsz, 4), in_specs=_lru_specs(s_len, lambda b, cb: (b, cb)),
        out_specs=pl.BlockSpec((s_len, 128), lambda b, cb: (b, cb)),
        out_shape=jax.ShapeDtypeStruct((t, B_WIDTH), F32), compiler_params=_cp("parallel", "parallel"),
    )(proj, proj, cw, cb.reshape(1, -1), wa, ba.reshape(1, -1), wx, bxb.reshape(1, -1), lam.reshape(1, -1))


def lru_bwd(proj, cw, cb, wa, ba, wx, bxb, lam, dcat, bsz, *, name, side=None):
    t = proj.shape[0]
    s_len = t // bsz

    def body(x_ref, g_ref, cw_ref, cb_ref, wa_ref, ba_ref, wx_ref, bx_ref, lam_ref, dy_ref,
             dx_ref, dg_ref, dcw_ref, dcb_ref, dwa_ref, dba_ref, dwx_ref, dbx_ref, dlam_ref):
        b = pl.program_id(1)
        row = lax.broadcasted_iota(jnp.int32, (s_len, 128), 0)
        x = x_ref[...]
        lam = lam_ref[...]
        xc, r, i, sp, a, mult = _lru_gates(x, cw_ref[...], cb_ref[...], wa_ref[0], ba_ref[...], wx_ref[0], bx_ref[...],
                                           lam, row)
        ixc = i * xc
        h = _lru_scan(a, mult * ixc, row)
        gv = g_ref[...]
        gl, th = _gelu(gv)
        dy = dy_ref[...]
        dg_ref[...] = dy * h * _gelu_grad(gv, th)
        gr = _lru_scan_rev(_shift_up(a, 1, row), dy * gl, row)
        da = gr * _shift_dn(h, 1, row)
        dmult = gr * ixc
        di = gr * mult * xc
        dxc = gr * mult * i
        dlog_a = da * a - dmult * (a * a) / mult
        dr = dlog_a * (-RG_C * sp)
        dlam = jnp.sum(dlog_a * r, axis=0, keepdims=True) * (RG_C * _sigmoid(-lam))
        dpa = dr * r * (1.0 - r)
        dpx = di * i * (1.0 - i)
        dxc = dxc + _mdot(dpa, wa_ref[0], NT) + _mdot(dpx, wx_ref[0], NT)
        dx, dcw = _conv_bwd(x, cw_ref[...], dxc, row)
        dx_ref[...] = dx

        @pl.when(b == 0)
        def _():
            for ref in (dcw_ref, dcb_ref, dwa_ref, dba_ref, dwx_ref, dbx_ref, dlam_ref):
                ref[...] = jnp.zeros_like(ref)

        dcw_ref[...] += dcw
        dcb_ref[...] += jnp.sum(dxc, axis=0, keepdims=True)
        dwa_ref[0] += _mdot(xc, dpa, TN)
        dwx_ref[0] += _mdot(xc, dpx, TN)
        dba_ref[...] += jnp.sum(dpa, axis=0, keepdims=True)
        dbx_ref[...] += jnp.sum(dpx, axis=0, keepdims=True)
        dlam_ref[...] += dlam

    order = lambda cb, b: (b, cb)
    act = pl.BlockSpec((s_len, 128), lambda cb, b: (b, cb))
    vec = pl.BlockSpec((1, 128), lambda cb, b: (0, cb))
    mat = pl.BlockSpec((1, 128, 128), lambda cb, b: (cb, 0, 0))
    vshape = jax.ShapeDtypeStruct((1, B_WIDTH), F32)
    mshape = jax.ShapeDtypeStruct((4, 128, 128), F32)
    return _call(
        body, (proj, proj, cw, cb.reshape(1, -1), wa, ba.reshape(1, -1), wx, bxb.reshape(1, -1), lam.reshape(1, -1), dcat),
        side, (4, bsz), name=name,
        in_specs=_lru_specs(s_len, order) + [pl.BlockSpec((s_len, 128), lambda cb, b: (b, 4 + cb))],
        out_specs=[act, act, pl.BlockSpec((4, 128), lambda cb, b: (0, cb)), vec, mat, vec, mat, vec, vec],
        out_shape=[jax.ShapeDtypeStruct((t, B_WIDTH), F32), jax.ShapeDtypeStruct((t, B_WIDTH), F32),
                   jax.ShapeDtypeStruct((4, B_WIDTH), F32), vshape, mshape, vshape, mshape, vshape, vshape],
        compiler_params=_cp("parallel", "arbitrary"),
    )


_BDIMS = {"nn": ((2,), (1,)), "nt": ((2,), (2,)), "tn": ((1,), (1,))}
C_QSCALE = C_HEAD_DIM ** -0.5


def _bmm(a, b, mode, exact=False):
    dims = (_BDIMS[mode], ((0,), (0,)))
    if exact:
        return lax.dot_general(a, b, dims, preferred_element_type=F32, precision=lax.Precision.HIGH)
    return lax.dot_general(a.astype(MM), b.astype(MM), dims, preferred_element_type=F32)


def _col(x, idx, lane):
    return jnp.broadcast_to(jnp.sum(jnp.where(lane == idx, x, 0.0), axis=-1, keepdims=True), x.shape)


def _seg_cumsum(g, row):
    pos = row & (CHUNK - 1)
    d = 1
    while d < CHUNK:
        g = g + jnp.where(pos >= d, pltpu.roll(g, d, 0), 0.0)
        d *= 2
    return g


def _seg_cumsum_rev(g, row):
    pos = row & (CHUNK - 1)
    n = g.shape[0]
    d = 1
    while d < CHUNK:
        g = g + jnp.where(pos < CHUNK - d, pltpu.roll(g, n - d, 0), 0.0)
        d *= 2
    return g


def _gdn_prep(qr, kr, vr, gates, cwq, cwk, cwv, a_log, dtb, h):
    s_len = qr.shape[0]
    nc = s_len // CHUNK
    row = lax.broadcasted_iota(jnp.int32, (s_len, 128), 0)
    lane = lax.broadcasted_iota(jnp.int32, (s_len, 128), 1)
    r = {"row": row, "lane": lane}
    for nm, x, w in (("q", qr, cwq), ("k", kr, cwk), ("v", vr, cwv)):
        c = _conv_fwd(x, w, row)
        sg = _sigmoid(c)
        r["c" + nm], r["s" + nm], r[nm + "c"] = c, sg, c * sg
    r["rq"] = lax.rsqrt(jnp.sum(r["qc"] * r["qc"], axis=-1, keepdims=True) + NORM_EPS)
    r["rk"] = lax.rsqrt(jnp.sum(r["kc"] * r["kc"], axis=-1, keepdims=True) + NORM_EPS)
    r["qn"] = r["qc"] * r["rq"]
    r["kn"] = r["kc"] * r["rk"]
    r["beta"] = _sigmoid(_col(gates, h, lane))
    r["A"] = jnp.exp(a_log)
    r["pre"] = _col(gates, 8 + h, lane) + dtb
    r["sp"] = _softplus(r["pre"])
    gc = _seg_cumsum(-r["A"] * r["sp"], row)
    sh = (nc, CHUNK, 128)
    q3 = (r["qn"] * C_QSCALE).reshape(sh)
    k3 = r["kn"].reshape(sh)
    v3 = r["vc"].reshape(sh)
    beta3 = r["beta"].reshape(sh)
    gc3 = gc.reshape(sh)
    gcl3 = gc3[:, CHUNK - 1:CHUNK, :]
    eg = jnp.exp(gc3)
    ekd = jnp.exp(gcl3 - gc3)
    col64 = gc3[:, :, :CHUNK]
    row64 = jnp.swapaxes(gc3, 1, 2)[:, :CHUNK, :]
    ii = lax.broadcasted_iota(jnp.int32, (nc, CHUNK, CHUNK), 1)
    jj = lax.broadcasted_iota(jnp.int32, (nc, CHUNK, CHUNK), 2)
    tril = ii >= jj
    strict = ii > jj
    dm = jnp.where(tril, jnp.exp(jnp.where(tril, col64 - row64, 0.0)), 0.0)
    kb = k3 * beta3
    lmat = jnp.where(strict, _bmm(kb, k3, "nt") * dm, 0.0)
    attn = _bmm(q3, k3, "nt") * dm
    r.update(q3=q3, k3=k3, v3=v3, beta3=beta3, gc3=gc3, eg=eg, ekd=ekd, gl=jnp.exp(gcl3), dm=dm, kb=kb, lmat=lmat,
             attn=attn, strict=strict, tril=tril, qg=q3 * eg, kdec=k3 * ekd)
    return r


def _neumann_inverse(lmat):
    ii = lax.broadcasted_iota(jnp.int32, lmat.shape, 1)
    jj = lax.broadcasted_iota(jnp.int32, lmat.shape, 2)
    x = -lmat
    tm = jnp.where(ii == jj, 1.0, 0.0) + x
    pw = x
    for _ in range(5):
        pw = _bmm(pw, pw, "nn", exact=True)
        tm = tm + _bmm(tm, pw, "nn", exact=True)
    return tm


def _gdn_specs(s_len):
    act = lambda off: pl.BlockSpec((s_len, 128), lambda b, h: (b, off + h))
    cw = lambda off: pl.BlockSpec((4, 128), lambda b, h: (0, off + h))
    smem = pl.BlockSpec(memory_space=pltpu.SMEM)
    return [act(0), act(8), act(16), act(24), pl.BlockSpec((s_len, 128), lambda b, h: (b, 0)), cw(0), cw(8), cw(16),
            smem, smem, pl.BlockSpec((1, 128), lambda b, h: (0, 0))]


def gdn_fwd(proj, gates, cw, a_log, dtb, ng, bsz, *, name):
    t = proj.shape[0]
    s_len = t // bsz
    nc = s_len // CHUNK

    def body(q_ref, k_ref, v_ref, z_ref, gt_ref, cwq_ref, cwk_ref, cwv_ref, al_ref, dt_ref, ng_ref,
             out_ref, o_ref, vn_ref, tm_ref, st_ref, u_s, w_s, qg_s, kd_s, at_s, gl_s):
        h = pl.program_id(1)
        r = _gdn_prep(q_ref[...], k_ref[...], v_ref[...], gt_ref[...], cwq_ref[...], cwk_ref[...], cwv_ref[...],
                      al_ref[h], dt_ref[h], h)
        tm = _neumann_inverse(r["lmat"])
        tm_ref[0, 0] = tm
        u_s[...] = _bmm(tm, r["v3"] * r["beta3"], "nn", exact=True)
        w_s[...] = _bmm(tm, r["kb"] * r["eg"], "nn", exact=True)
        qg_s[...] = r["qg"]
        kd_s[...] = r["kdec"]
        at_s[...] = r["attn"]
        gl_s[...] = r["gl"]

        def chunk(n, state):
            st = pl.multiple_of(n * CHUNK, CHUNK)
            st_ref[0, 0, n] = state
            v_new = u_s[n] - _mdot(w_s[n], state)
            o_ref[pl.ds(st, CHUNK), :] = _mdot(qg_s[n], state) + _mdot(at_s[n], v_new)
            vn_ref[pl.ds(st, CHUNK), :] = v_new
            return state * gl_s[n] + _mdot(kd_s[n], v_new, TN)

        lax.fori_loop(0, nc, chunk, jnp.zeros((128, 128), F32))
        o = o_ref[...]
        rms = lax.rsqrt(jnp.mean(o * o, axis=-1, keepdims=True) + NORM_EPS)
        z = z_ref[...]
        out_ref[...] = o * rms * ng_ref[...] * (z * _sigmoid(z))

    blk = pl.BlockSpec((s_len, 128), lambda b, h: (b, h))
    full = jax.ShapeDtypeStruct((t, C_WIDTH), F32)
    return _pcall(
        body, name=name, grid=(bsz, C_HEADS), in_specs=_gdn_specs(s_len),
        out_specs=[blk, blk, blk, pl.BlockSpec((1, 1, nc, CHUNK, CHUNK), lambda b, h: (b, h, 0, 0, 0)),
                   pl.BlockSpec((1, 1, nc, 128, 128), lambda b, h: (b, h, 0, 0, 0))],
        out_shape=[full, full, full, jax.ShapeDtypeStruct((bsz, C_HEADS, nc, CHUNK, CHUNK), F32),
                   jax.ShapeDtypeStruct((bsz, C_HEADS, nc, 128, 128), F32)],
        scratch_shapes=[pltpu.VMEM((nc, CHUNK, 128), F32)] * 4 + [pltpu.VMEM((nc, CHUNK, CHUNK), F32),
                                                                   pltpu.VMEM((nc, 1, 128), F32)],
        compiler_params=_cp("parallel", "parallel"),
    )(proj, proj, proj, proj, gates, cw, cw, cw, a_log, dtb, ng.reshape(1, 128))


def gdn_bwd(proj, gates, cw, a_log, dtb, ng, o_pre, vnew, tmat, states, dout, bsz, *, name):
    t = proj.shape[0]
    s_len = t // bsz
    nc = s_len // CHUNK

    def body(q_ref, k_ref, v_ref, z_ref, gt_ref, cwq_ref, cwk_ref, cwv_ref, al_ref, dt_ref, ng_ref,
             o_ref, vn_ref, tm_ref, st_ref, do_ref,
             dq_ref, dk_ref, dv_ref, dz_ref, dgt_ref, dcq_ref, dck_ref, dcv_ref, dsm_ref,
             w_s, qg_s, kd_s, at_s, gl_s, dop_s, du_s, dw_s, dat_s, dqg_s, dkd_s, dgl_s):
        h = pl.program_id(1)
        qr, kr, vr = q_ref[...], k_ref[...], v_ref[...]
        r = _gdn_prep(qr, kr, vr, gt_ref[...], cwq_ref[...], cwk_ref[...], cwv_ref[...], al_ref[h], dt_ref[h], h)
        row, lane = r["row"], r["lane"]
        tm = tm_ref[0, 0]
        q3, k3, v3, beta3, eg, kb, dm = r["q3"], r["k3"], r["v3"], r["beta3"], r["eg"], r["kb"], r["dm"]
        u3 = _bmm(tm, v3 * beta3, "nn", exact=True)
        w3 = _bmm(tm, kb * eg, "nn", exact=True)
        w_s[...] = w3
        qg_s[...] = r["qg"]
        kd_s[...] = r["kdec"]
        at_s[...] = r["attn"]
        gl_s[...] = r["gl"]

        z = z_ref[...]
        sz = _sigmoid(z)
        o = o_ref[...]
        rms = lax.rsqrt(jnp.mean(o * o, axis=-1, keepdims=True) + NORM_EPS)
        on = o * rms
        dout_v = do_ref[...]
        ngv = ng_ref[...]
        dz_ref[...] = dout_v * on * ngv * (sz * (1.0 + z * (1.0 - sz)))
        dos = dout_v * (z * sz)
        dng = jnp.sum(dos * on, axis=0, keepdims=True)
        don = dos * ngv
        dop_s[...] = (rms * (don - on * jnp.mean(don * on, axis=-1, keepdims=True))).reshape(nc, CHUNK, 128)

        def chunk(i, dstate):
            n = nc - 1 - i
            st = pl.multiple_of(n * CHUNK, CHUNK)
            state = st_ref[0, 0, n]
            vn = vn_ref[pl.ds(st, CHUNK), :]
            do_n = dop_s[n]
            dvn = _mdot(at_s[n], do_n, TN) + _mdot(kd_s[n], dstate)
            du_s[n] = dvn
            dat_s[n] = _mdot(do_n, vn, NT)
            dqg_s[n] = _mdot(do_n, state, NT)
            dkd_s[n] = _mdot(vn, dstate, NT)
            dgl_s[n] = jnp.broadcast_to(jnp.sum(jnp.sum(state * dstate, axis=1, keepdims=True), axis=0, keepdims=True), (1, 128))
            dw_s[n] = -_mdot(dvn, state, NT)
            return dstate * gl_s[n] + _mdot(qg_s[n], do_n, TN) - _mdot(w_s[n], dvn, TN)

        lax.fori_loop(0, nc, chunk, jnp.zeros((128, 128), F32))

        du, dw, dqg, dkd = du_s[...], dw_s[...], dqg_s[...], dkd_s[...]
        dat = jnp.where(r["tril"], dat_s[...], 0.0)
        dvb = _bmm(tm, du, "tn", exact=True)
        dkbg = _bmm(tm, dw, "tn", exact=True)
        dl = -jnp.where(r["strict"], _bmm(dvb, u3, "nt") + _bmm(dkbg, w3, "nt"), 0.0)
        dml = dl * dm
        dn = dat * dm
        dkb = _bmm(dml, k3, "nn") + dkbg * eg
        dk3 = _bmm(dml, kb, "tn") + _bmm(dn, q3, "tn") + dkd * r["ekd"] + dkb * beta3
        dq3 = dqg * eg + _bmm(dn, k3, "nn")
        e = dl * r["lmat"] + dat * r["attn"]
        ones = jnp.ones((nc, CHUNK, 128), F32)
        colsum = lax.dot_general(e, ones, (_BDIMS["tn"], ((0,), (0,))), preferred_element_type=F32, precision=HI)
        dgc = jnp.sum(e, axis=-1, keepdims=True) - colsum
        dgc = dgc + eg * (jnp.sum(dqg * q3, axis=-1, keepdims=True) + jnp.sum(dkbg * kb, axis=-1, keepdims=True))
        skd = jnp.sum(dkd * r["kdec"], axis=-1, keepdims=True)
        dgcl = jnp.sum(skd, axis=1, keepdims=True) + dgl_s[...] * r["gl"]
        pos3 = lax.broadcasted_iota(jnp.int32, (nc, CHUNK, 128), 1)
        dgc = dgc - skd + jnp.where(pos3 == CHUNK - 1, dgcl, 0.0)
        dbeta = jnp.sum(dkb * k3, axis=-1, keepdims=True) + jnp.sum(dvb * v3, axis=-1, keepdims=True)
        dv3 = dvb * beta3

        dg = _seg_cumsum_rev(dgc.reshape(s_len, 128), row)
        beta = r["beta"]
        dbl = jnp.broadcast_to(dbeta, (nc, CHUNK, 128)).reshape(s_len, 128) * beta * (1.0 - beta)
        dai = dg * (-r["A"]) * _sigmoid(r["pre"])
        d_dtb = jnp.sum(dai, axis=0, keepdims=True)
        d_alog = jnp.sum(dg * (-r["sp"]), axis=0, keepdims=True) * r["A"]

        @pl.when(h == 0)
        def _():
            dgt_ref[...] = jnp.zeros_like(dgt_ref)
            dsm_ref[...] = jnp.zeros_like(dsm_ref)

        dgt_ref[...] += jnp.where(lane == h, dbl, 0.0) + jnp.where(lane == 8 + h, dai, 0.0)
        r16 = lax.broadcasted_iota(jnp.int32, (16, 128), 0)
        l16 = lax.broadcasted_iota(jnp.int32, (16, 128), 1)
        small = jnp.where((r16 == h) & (l16 == 0), d_alog, 0.0) + jnp.where((r16 == h) & (l16 == 1), d_dtb, 0.0)
        dsm_ref[0] += small + jnp.where(r16 == 8 + h, dng, 0.0)

        dqn = dq3.reshape(s_len, 128) * C_QSCALE
        dkn = dk3.reshape(s_len, 128)
        dqc = r["rq"] * (dqn - r["qn"] * jnp.sum(dqn * r["qn"], axis=-1, keepdims=True))
        dkc = r["rk"] * (dkn - r["kn"] * jnp.sum(dkn * r["kn"], axis=-1, keepdims=True))
        dvc = dv3.reshape(s_len, 128)
        for nm, x, w_ref, dxc, dx_ref, dc_ref in (("q", qr, cwq_ref, dqc, dq_ref, dcq_ref), ("k", kr, cwk_ref, dkc, dk_ref, dck_ref),
                                                 ("v", vr, cwv_ref, dvc, dv_ref, dcv_ref)):
            c, sg = r["c" + nm], r["s" + nm]
            dc = dxc * (sg * (1.0 + c * (1.0 - sg)))
            dx, dwc = _conv_bwd(x, w_ref[...], dc, row)
            dx_ref[...] = dx
            dc_ref[0] = dwc

    blk = pl.BlockSpec((s_len, 128), lambda b, h: (b, h))
    full = jax.ShapeDtypeStruct((t, C_WIDTH), F32)
    cwo = pl.BlockSpec((1, 4, 128), lambda b, h: (b, 0, h))
    cws = jax.ShapeDtypeStruct((bsz, 4, C_WIDTH), F32)
    c128 = pltpu.VMEM((nc, CHUNK, 128), F32)
    outs = _pcall(
        body, name=name, grid=(bsz, C_HEADS),
        in_specs=_gdn_specs(s_len) + [blk, blk, pl.BlockSpec((1, 1, nc, CHUNK, CHUNK), lambda b, h: (b, h, 0, 0, 0)),
                                      pl.BlockSpec((1, 1, nc, 128, 128), lambda b, h: (b, h, 0, 0, 0)), blk],
        out_specs=[blk, blk, blk, blk, pl.BlockSpec((s_len, 128), lambda b, h: (b, 0)), cwo, cwo, cwo,
                   pl.BlockSpec((1, 16, 128), lambda b, h: (b, 0, 0))],
        out_shape=[full, full, full, full, jax.ShapeDtypeStruct((t, 128), F32), cws, cws, cws,
                   jax.ShapeDtypeStruct((bsz, 16, 128), F32)],
        scratch_shapes=[c128, c128, c128, pltpu.VMEM((nc, CHUNK, CHUNK), F32), pltpu.VMEM((nc, 1, 128), F32), c128,
                        c128, c128, pltpu.VMEM((nc, CHUNK, CHUNK), F32), c128, c128, pltpu.VMEM((nc, 1, 128), F32)],
        compiler_params=_cp("parallel", "arbitrary"),
    )(proj, proj, proj, proj, gates, cw, cw, cw, a_log, dtb, ng.reshape(1, 128), o_pre, vnew, tmat, states, dout)
    dq, dk, dv, dz, dgates, dcq, dck, dcv, dsm = outs
    return dq, dk, dv, dz, dgates, jnp.concatenate([dcq, dck, dcv], axis=-1), dsm


def gdc_pre_fwd(proj, cw, bsz, *, name):
    t = proj.shape[0]
    s_len = t // bsz

    def body(x_ref, w_ref, y_ref):
        row = lax.broadcasted_iota(jnp.int32, (s_len, 128), 0)
        c = _conv_fwd(x_ref[...], w_ref[...], row)
        xc = c * _sigmoid(c)
        rn = lax.rsqrt(jnp.sum(xc * xc, axis=-1, keepdims=True) + NORM_EPS)
        y_ref[...] = jnp.where(pl.program_id(1) < 2 * C_HEADS, xc * rn, xc)

    blk = pl.BlockSpec((s_len, 128), lambda b, j: (b, j))
    return _pcall(
        body, name=name, grid=(bsz, 3 * C_HEADS), in_specs=[blk, pl.BlockSpec((4, 128), lambda b, j: (0, j))],
        out_specs=blk, out_shape=jax.ShapeDtypeStruct((t, 3 * C_WIDTH), F32), compiler_params=_cp("parallel", "parallel"),
    )(proj, cw)


def gdc_pre_bwd(proj, cw, dy, bsz, *, name):
    t = proj.shape[0]
    s_len = t // bsz

    def body(x_ref, w_ref, dy_ref, dx_ref, dw_ref):
        row = lax.broadcasted_iota(jnp.int32, (s_len, 128), 0)
        x = x_ref[...]
        c = _conv_fwd(x, w_ref[...], row)
        sg = _sigmoid(c)
        xc = c * sg
        rn = lax.rsqrt(jnp.sum(xc * xc, axis=-1, keepdims=True) + NORM_EPS)
        dyv = dy_ref[...]
        xn = xc * rn
        dxc = jnp.where(pl.program_id(1) < 2 * C_HEADS, rn * (dyv - xn * jnp.sum(dyv * xn, axis=-1, keepdims=True)), dyv)
        dc = dxc * (sg * (1.0 + c * (1.0 - sg)))
        dx, dw = _conv_bwd(x, w_ref[...], dc, row)
        dx_ref[...] = dx
        dw_ref[0] = dw

    blk = pl.BlockSpec((s_len, 128), lambda b, j: (b, j))
    return _pcall(
        body, name=name, grid=(bsz, 3 * C_HEADS), in_specs=[blk, pl.BlockSpec((4, 128), lambda b, j: (0, j)), blk],
        out_specs=[blk, pl.BlockSpec((1, 4, 128), lambda b, j: (b, 0, j))],
        out_shape=[jax.ShapeDtypeStruct((t, 3 * C_WIDTH), F32), jax.ShapeDtypeStruct((bsz, 4, 3 * C_WIDTH), F32)],
        compiler_params=_cp("parallel", "parallel"),
    )(proj, cw, dy)


GDC_GROUP = 8


def _gdc_local(qn, kn, vc, gates, a_log, dtb, h):
    rows = qn.shape[0]
    nc = rows // CHUNK
    row = lax.broadcasted_iota(jnp.int32, (rows, 128), 0)
    lane = lax.broadcasted_iota(jnp.int32, (rows, 128), 1)
    r = {"row": row, "lane": lane}
    r["beta"] = _sigmoid(_col(gates, h, lane))
    r["A"] = jnp.exp(a_log)
    r["pre"] = _col(gates, 8 + h, lane) + dtb
    r["sp"] = _softplus(r["pre"])
    gc = _seg_cumsum(-r["A"] * r["sp"], row)
    sh = (nc, CHUNK, 128)
    q3 = (qn * C_QSCALE).reshape(sh)
    k3 = kn.reshape(sh)
    v3 = vc.reshape(sh)
    beta3 = r["beta"].reshape(sh)
    gc3 = gc.reshape(sh)
    gcl3 = gc3[:, CHUNK - 1:CHUNK, :]
    eg = jnp.exp(gc3)
    ekd = jnp.exp(gcl3 - gc3)
    col64 = gc3[:, :, :CHUNK]
    row64 = jnp.swapaxes(gc3, 1, 2)[:, :CHUNK, :]
    ii = lax.broadcasted_iota(jnp.int32, (nc, CHUNK, CHUNK), 1)
    jj = lax.broadcasted_iota(jnp.int32, (nc, CHUNK, CHUNK), 2)
    tril = ii >= jj
    strict = ii > jj
    dm = jnp.where(tril, jnp.exp(jnp.where(tril, col64 - row64, 0.0)), 0.0)
    kb = k3 * beta3
    lmat = jnp.where(strict, _bmm(kb, k3, "nt") * dm, 0.0)
    attn = _bmm(q3, k3, "nt") * dm
    r.update(q3=q3, k3=k3, v3=v3, beta3=beta3, eg=eg, ekd=ekd, gl=jnp.exp(gcl3), dm=dm, kb=kb, lmat=lmat,
             attn=attn, strict=strict, tril=tril, qg=q3 * eg, kdec=k3 * ekd)
    return r


def _gdc_specs(s_len):
    act = lambda off: pl.BlockSpec((s_len, 128), lambda b, h: (b, off + h))
    smem = pl.BlockSpec(memory_space=pltpu.SMEM)
    return [act(0), act(8), act(16), act(24), pl.BlockSpec((s_len, 128), lambda b, h: (b, 0)), smem, smem,
            pl.BlockSpec((1, 128), lambda b, h: (0, 0))]


def gdc_fwd(qkv, proj, gates, a_log, dtb, ng, bsz, *, name, side=None):
    t = proj.shape[0]
    s_len = t // bsz
    nc = s_len // CHUNK
    grp = min(GDC_GROUP, nc)
    gr = grp * CHUNK

    def body(q_ref, k_ref, v_ref, z_ref, gt_ref, al_ref, dt_ref, ng_ref,
             out_ref, o_ref, vn_ref, tm_ref, st_ref, u_s, w_s, qg_s, kd_s, at_s, gl_s):
        h = pl.program_id(1)

        def local(gi, carry):
            rs = pl.ds(pl.multiple_of(gi * gr, gr), gr)
            cs = pl.ds(gi * grp, grp)
            r = _gdc_local(q_ref[rs, :], k_ref[rs, :], v_ref[rs, :], gt_ref[rs, :], al_ref[h], dt_ref[h], h)
            tm = _neumann_inverse(r["lmat"])
            tm_ref[0, 0, cs] = tm
            u_s[cs] = _bmm(tm, r["v3"] * r["beta3"], "nn", exact=True)
            w_s[cs] = _bmm(tm, r["kb"] * r["eg"], "nn", exact=True)
            qg_s[cs] = r["qg"]
            kd_s[cs] = r["kdec"]
            at_s[cs] = r["attn"]
            gl_s[cs] = r["gl"]
            return carry

        lax.fori_loop(0, nc // grp, local, 0)

        def chunk(n, state):
            st = pl.multiple_of(n * CHUNK, CHUNK)
            st_ref[0, 0, n] = state
            v_new = u_s[n] - _mdot(w_s[n], state)
            o_ref[pl.ds(st, CHUNK), :] = _mdot(qg_s[n], state) + _mdot(at_s[n], v_new)
            vn_ref[pl.ds(st, CHUNK), :] = v_new
            return state * gl_s[n] + _mdot(kd_s[n], v_new, TN)

        lax.fori_loop(0, nc, chunk, jnp.zeros((128, 128), F32))
        o = o_ref[...]
        rms = lax.rsqrt(jnp.mean(o * o, axis=-1, keepdims=True) + NORM_EPS)
        z = z_ref[...]
        out_ref[...] = o * rms * ng_ref[...] * (z * _sigmoid(z))

    blk = pl.BlockSpec((s_len, 128), lambda b, h: (b, h))
    full = jax.ShapeDtypeStruct((t, C_WIDTH), F32)
    return _call(
        body, (qkv, qkv, qkv, proj, gates, a_log, dtb, ng.reshape(1, 128)), side, (bsz, C_HEADS), name=name,
        in_specs=_gdc_specs(s_len),
        out_specs=[blk, blk, blk, pl.BlockSpec((1, 1, nc, CHUNK, CHUNK), lambda b, h: (b, h, 0, 0, 0)),
                   pl.BlockSpec((1, 1, nc, 128, 128), lambda b, h: (b, h, 0, 0, 0))],
        out_shape=[full, full, full, jax.ShapeDtypeStruct((bsz, C_HEADS, nc, CHUNK, CHUNK), F32),
                   jax.ShapeDtypeStruct((bsz, C_HEADS, nc, 128, 128), F32)],
        scratch_shapes=[pltpu.VMEM((nc, CHUNK, 128), F32)] * 4 + [pltpu.VMEM((nc, CHUNK, CHUNK), F32),
                                                                   pltpu.VMEM((nc, 1, 128), F32)],
        compiler_params=_cp("parallel", "parallel"),
    )


def gdc_bwd(qkv, proj, gates, a_log, dtb, ng, o_pre, vnew, tmat, states, dout, bsz, *, name, side=None):
    t = proj.shape[0]
    s_len = t // bsz
    nc = s_len // CHUNK
    grp = min(GDC_GROUP, nc)
    gr = grp * CHUNK

    def body(q_ref, k_ref, v_ref, z_ref, gt_ref, al_ref, dt_ref, ng_ref, o_ref, vn_ref, tm_ref, st_ref, do_ref,
             dq_ref, dk_ref, dv_ref, dz_ref, dgt_ref, dsm_ref,
             w_s, qg_s, kd_s, at_s, gl_s, dop_s, du_s, dw_s, dat_s, dqg_s, dkd_s, dgl_s):
        h = pl.program_id(1)
        a_log_h, dtb_h = al_ref[h], dt_ref[h]

        z = z_ref[...]
        sz = _sigmoid(z)
        o = o_ref[...]
        rms = lax.rsqrt(jnp.mean(o * o, axis=-1, keepdims=True) + NORM_EPS)
        on = o * rms
        dout_v = do_ref[...]
        ngv = ng_ref[...]
        dz_ref[...] = dout_v * on * ngv * (sz * (1.0 + z * (1.0 - sz)))
        dos = dout_v * (z * sz)
        dng = jnp.sum(dos * on, axis=0, keepdims=True)
        don = dos * ngv
        dop_s[...] = (rms * (don - on * jnp.mean(don * on, axis=-1, keepdims=True))).reshape(nc, CHUNK, 128)

        def local(gi, carry):
            rs = pl.ds(pl.multiple_of(gi * gr, gr), gr)
            cs = pl.ds(gi * grp, grp)
            r = _gdc_local(q_ref[rs, :], k_ref[rs, :], v_ref[rs, :], gt_ref[rs, :], a_log_h, dtb_h, h)
            w_s[cs] = _bmm(tm_ref[0, 0, cs], r["kb"] * r["eg"], "nn", exact=True)
            qg_s[cs] = r["qg"]
            kd_s[cs] = r["kdec"]
            at_s[cs] = r["attn"]
            gl_s[cs] = r["gl"]
            return carry

        lax.fori_loop(0, nc // grp, local, 0)

        def chunk(i, dstate):
            n = nc - 1 - i
            st = pl.multiple_of(n * CHUNK, CHUNK)
            state = st_ref[0, 0, n]
            vn = vn_ref[pl.ds(st, CHUNK), :]
            do_n = dop_s[n]
            dvn = _mdot(at_s[n], do_n, TN) + _mdot(kd_s[n], dstate)
            du_s[n] = dvn
            dat_s[n] = _mdot(do_n, vn, NT)
            dqg_s[n] = _mdot(do_n, state, NT)
            dkd_s[n] = _mdot(vn, dstate, NT)
            dgl_s[n] = jnp.broadcast_to(jnp.sum(jnp.sum(state * dstate, axis=1, keepdims=True), axis=0, keepdims=True), (1, 128))
            dw_s[n] = -_mdot(dvn, state, NT)
            return dstate * gl_s[n] + _mdot(qg_s[n], do_n, TN) - _mdot(w_s[n], dvn, TN)

        lax.fori_loop(0, nc, chunk, jnp.zeros((128, 128), F32))

        @pl.when(h == 0)
        def _():
            dgt_ref[...] = jnp.zeros_like(dgt_ref)
            dsm_ref[...] = jnp.zeros_like(dsm_ref)

        def local_bwd(gi, carry):
            d_alog, d_dtb = carry
            rs = pl.ds(pl.multiple_of(gi * gr, gr), gr)
            cs = pl.ds(gi * grp, grp)
            r = _gdc_local(q_ref[rs, :], k_ref[rs, :], v_ref[rs, :], gt_ref[rs, :], a_log_h, dtb_h, h)
            row, lane = r["row"], r["lane"]
            q3, k3, v3, beta3, eg, kb, dm = r["q3"], r["k3"], r["v3"], r["beta3"], r["eg"], r["kb"], r["dm"]
            tm = tm_ref[0, 0, cs]
            u3 = _bmm(tm, v3 * beta3, "nn", exact=True)
            w3 = w_s[cs]
            du, dw, dqg, dkd = du_s[cs], dw_s[cs], dqg_s[cs], dkd_s[cs]
            dat = jnp.where(r["tril"], dat_s[cs], 0.0)
            dvb = _bmm(tm, du, "tn", exact=True)
            dkbg = _bmm(tm, dw, "tn", exact=True)
            dl = -jnp.where(r["strict"], _bmm(dvb, u3, "nt") + _bmm(dkbg, w3, "nt"), 0.0)
            dml = dl * dm
            dn = dat * dm
            dkb = _bmm(dml, k3, "nn") + dkbg * eg
            dk3 = _bmm(dml, kb, "tn") + _bmm(dn, q3, "tn") + dkd * r["ekd"] + dkb * beta3
            dq3 = dqg * eg + _bmm(dn, k3, "nn")
            e = dl * r["lmat"] + dat * r["attn"]
            ones = jnp.ones((grp, CHUNK, 128), F32)
            colsum = lax.dot_general(e, ones, (_BDIMS["tn"], ((0,), (0,))), preferred_element_type=F32, precision=HI)
            dgc = jnp.sum(e, axis=-1, keepdims=True) - colsum
            dgc = dgc + eg * (jnp.sum(dqg * q3, axis=-1, keepdims=True) + jnp.sum(dkbg * kb, axis=-1, keepdims=True))
            skd = jnp.sum(dkd * r["kdec"], axis=-1, keepdims=True)
            dgcl = jnp.sum(skd, axis=1, keepdims=True) + dgl_s[cs] * r["gl"]
            pos3 = lax.broadcasted_iota(jnp.int32, (grp, CHUNK, 128), 1)
            dgc = dgc - skd + jnp.where(pos3 == CHUNK - 1, dgcl, 0.0)
            dbeta = jnp.sum(dkb * k3, axis=-1, keepdims=True) + jnp.sum(dvb * v3, axis=-1, keepdims=True)
            dg = _seg_cumsum_rev(dgc.reshape(gr, 128), row)
            beta = r["beta"]
            dbl = jnp.broadcast_to(dbeta, (grp, CHUNK, 128)).reshape(gr, 128) * beta * (1.0 - beta)
            dai = dg * (-r["A"]) * _sigmoid(r["pre"])
            dgt_ref[rs, :] += jnp.where(lane == h, dbl, 0.0) + jnp.where(lane == 8 + h, dai, 0.0)
            dq_ref[rs, :] = dq3.reshape(gr, 128) * C_QSCALE
            dk_ref[rs, :] = dk3.reshape(gr, 128)
            dv_ref[rs, :] = (dvb * beta3).reshape(gr, 128)
            return (d_alog + jnp.sum(dg * (-r["sp"]), axis=0, keepdims=True) * r["A"],
                    d_dtb + jnp.sum(dai, axis=0, keepdims=True))

        zero = jnp.zeros((1, 128), F32)
        d_alog, d_dtb = lax.fori_loop(0, nc // grp, local_bwd, (zero, zero))
        r16 = lax.broadcasted_iota(jnp.int32, (16, 128), 0)
        l16 = lax.broadcasted_iota(jnp.int32, (16, 128), 1)
        small = jnp.where((r16 == h) & (l16 == 0), d_alog, 0.0) + jnp.where((r16 == h) & (l16 == 1), d_dtb, 0.0)
        dsm_ref[0] += small + jnp.where(r16 == 8 + h, dng, 0.0)

    blk = pl.BlockSpec((s_len, 128), lambda b, h: (b, h))
    blk3 = lambda off: pl.BlockSpec((s_len, 128), lambda b, h: (b, off + h))
    full = jax.ShapeDtypeStruct((t, C_WIDTH), F32)
    c128 = pltpu.VMEM((nc, CHUNK, 128), F32)
    res = _call(
        body, (qkv, qkv, qkv, proj, gates, a_log, dtb, ng.reshape(1, 128), o_pre, vnew, tmat, states, dout), side,
        (bsz, C_HEADS), name=name,
        in_specs=_gdc_specs(s_len) + [blk, blk, pl.BlockSpec((1, 1, nc, CHUNK, CHUNK), lambda b, h: (b, h, 0, 0, 0)),
                                      pl.BlockSpec((1, 1, nc, 128, 128), lambda b, h: (b, h, 0, 0, 0)), blk],
        out_specs=[blk, blk, blk, blk, pl.BlockSpec((s_len, 128), lambda b, h: (b, 0)),
                   pl.BlockSpec((1, 16, 128), lambda b, h: (b, 0, 0))],
        out_shape=[full, full, full, full, jax.ShapeDtypeStruct((t, 128), F32), jax.ShapeDtypeStruct((bsz, 16, 128), F32)],
        scratch_shapes=[c128, c128, c128, pltpu.VMEM((nc, CHUNK, CHUNK), F32), pltpu.VMEM((nc, 1, 128), F32), c128,
                        c128, c128, pltpu.VMEM((nc, CHUNK, CHUNK), F32), c128, c128, pltpu.VMEM((nc, 1, 128), F32)],
        compiler_params=_cp("parallel", "arbitrary"),
    )
    (dq, dk, dv, dz, dgates, dsm), extra = res if side is not None else (res, None)
    out = (jnp.concatenate([dq, dk, dv], axis=-1), dz, dgates, dsm)
    return out if side is None else (out, extra)


MESH_ID = pl.DeviceIdType.MESH
_FLIPS = [(0, 0, 1), (1, 0, 0), (0, 1, 0), (1, 1, 0), (1, 0, 1), (0, 1, 1), (1, 1, 1)]


def _me():
    return lax.axis_index("x"), lax.axis_index("y"), lax.axis_index("c")


def _flip(coord, d):
    return 1 - coord if d else coord


def all_gather(shard, *, name):
    def body(x_ref, o_ref, send_sems, recv_sems, local_sem):
        x, y, c = _me()
        mine = 4 * x + 2 * y + c
        own = pltpu.make_async_copy(x_ref, o_ref.at[mine], local_sem)
        own.start()
        copies = []
        for k, (dx, dy, dc) in enumerate(_FLIPS):
            cp = pltpu.make_async_remote_copy(
                src_ref=x_ref, dst_ref=o_ref.at[mine], send_sem=send_sems.at[k], recv_sem=recv_sems.at[k],
                device_id=(_flip(x, dx), _flip(y, dy), _flip(c, dc)), device_id_type=MESH_ID)
            cp.start()
            copies.append(cp)
        for cp in copies:
            cp.wait()
        own.wait()

    hbm = pl.BlockSpec(memory_space=pl.ANY)
    return _pcall(
        body, name=name, in_specs=[hbm], out_specs=hbm,
        out_shape=jax.ShapeDtypeStruct((N_DEV,) + shard.shape, shard.dtype),
        scratch_shapes=[pltpu.SemaphoreType.DMA((7,)), pltpu.SemaphoreType.DMA((7,)), pltpu.SemaphoreType.DMA(())],
    )(shard)


def all_to_all(parts, *, name):
    def body(x_ref, o_ref, send_sems, recv_sems, local_sem):
        x, y, c = _me()
        mine = 4 * x + 2 * y + c
        own = pltpu.make_async_copy(x_ref.at[mine], o_ref.at[mine], local_sem)
        own.start()
        copies = []
        for k, (dx, dy, dc) in enumerate(_FLIPS):
            px, py, pc = _flip(x, dx), _flip(y, dy), _flip(c, dc)
            cp = pltpu.make_async_remote_copy(
                src_ref=x_ref.at[4 * px + 2 * py + pc], dst_ref=o_ref.at[mine], send_sem=send_sems.at[k],
                recv_sem=recv_sems.at[k], device_id=(px, py, pc), device_id_type=MESH_ID)
            cp.start()
            copies.append(cp)
        for cp in copies:
            cp.wait()
        own.wait()

    hbm = pl.BlockSpec(memory_space=pl.ANY)
    return _pcall(
        body, name=name, in_specs=[hbm], out_specs=hbm, out_shape=jax.ShapeDtypeStruct(parts.shape, parts.dtype),
        scratch_shapes=[pltpu.SemaphoreType.DMA((7,)), pltpu.SemaphoreType.DMA((7,)), pltpu.SemaphoreType.DMA(())],
    )(parts)


def adamw_sum(parts, w, m, v, *, name, tr=256):
    r, cdim = w.shape
    tr = _tile8(r, tr)

    def body(p_ref, w_ref, m_ref, v_ref, g_ref, d_ref, mo_ref, vo_ref):
        g = p_ref[0].astype(F32)
        for j in range(1, N_DEV):
            g = g + p_ref[j].astype(F32)
        g_ref[...] = g
        mn = ADAM_B1 * m_ref[...] + (1.0 - ADAM_B1) * g
        vn = ADAM_B2 * v_ref[...] + (1.0 - ADAM_B2) * (g * g)
        mo_ref[...] = mn
        vo_ref[...] = vn
        m_hat = mn / (1.0 - ADAM_B1 ** ADAM_STEP)
        v_hat = vn / (1.0 - ADAM_B2 ** ADAM_STEP)
        d_ref[...] = -ADAM_LR * (m_hat / (jnp.sqrt(v_hat) + ADAM_EPS) + ADAM_WD * w_ref[...])

    blk = pl.BlockSpec((tr, cdim), lambda i: (i, 0))
    shp = jax.ShapeDtypeStruct((r, cdim), F32)
    return _pcall(
        body, name=name, grid=(r // tr,), in_specs=[pl.BlockSpec((N_DEV, tr, cdim), lambda i: (0, i, 0)), blk, blk, blk],
        out_specs=[blk, blk, blk, blk], out_shape=[shp, shp, shp, shp], compiler_params=_cp("parallel"),
    )(parts, w, m, v)


def _tile8(n, pref):
    for c in range(min(pref, n) - min(pref, n) % 16, 0, -16):
        if n % c == 0:
            return c
    return n


BIG = [("ffn1_wg", 2), ("ffn1_wu", 2), ("ffn1_wd", 1), ("ffn2_wg", 2), ("ffn2_wu", 2), ("ffn2_wd", 1), ("ple_wg", 1),
       ("ple_wp", 2), ("ab_w_in", 2), ("ab_w_out", 1), ("c_w_in", 2), ("c_w_out", 1)]
SMALL = [("ln_g", 2), ("ln_b", 2), ("b_conv_w", 2), ("c_conv_w", 2)]
REPL = ["ple_bg", "a_sinks", "b_conv_b", "b_wa", "b_ba", "b_wx", "b_bx", "b_lam", "c_a_log", "c_dt_bias", "c_norm_g"]
WEIGHTS = ["ffn1_wg", "ffn1_wu", "ffn1_wd", "ffn2_wg", "ffn2_wu", "ffn2_wd", "ln_g", "ln_b", "ple_wg", "ple_bg", "ple_wp",
           "ab_w_in", "a_sinks", "b_conv_w", "b_conv_b", "b_wa", "b_ba", "b_wx", "b_bx", "b_lam", "ab_w_out", "c_w_in",
           "c_conv_w", "c_a_log", "c_dt_bias", "c_norm_g", "c_w_out"]
PACK_COLS = 1024
PACK_ALIGN = 16 * PACK_COLS


def _as_bf16_bits(a):
    return lax.bitcast_convert_type(a, jnp.bfloat16).reshape(a.shape[:-1] + (2 * a.shape[-1],))


def _from_bf16_bits(a):
    return lax.bitcast_convert_type(a.reshape(a.shape[:-1] + (a.shape[-1] // 2, 2)), F32)


def _pad_rows(flat, align=PACK_ALIGN):
    n = flat.shape[-1]
    total = -(-n // align) * align
    flat = jnp.pad(flat, [(0, 0)] * (flat.ndim - 1) + [(0, total - n)])
    return flat.reshape(flat.shape[:-1] + (total // PACK_COLS, PACK_COLS))


def _join(blocks, axis):
    moved = jnp.moveaxis(blocks, 0, axis)
    shp = list(moved.shape)
    return moved.reshape(shp[:axis] + [shp[axis] * shp[axis + 1]] + shp[axis + 2:])


def _split(full, axis):
    shp = list(full.shape)
    return jnp.moveaxis(full.reshape(shp[:axis] + [N_DEV, shp[axis] // N_DEV] + shp[axis + 1:]), axis, 0)


def _dense_blocks(w):
    z = jnp.zeros((4, 2, 64, 2, 64), w.dtype)
    w4 = w.reshape(4, 2, 64, 64)
    z = z.at[:, 0, :, 0, :].set(w4[:, 0]).at[:, 1, :, 1, :].set(w4[:, 1])
    return z.reshape(4, 128, 128)


def _diag_blocks(d):
    d5 = d.reshape(4, 2, 64, 2, 64)
    return jnp.stack([d5[:, 0, :, 0, :], d5[:, 1, :, 1, :]], axis=1).reshape(8, 64, 64)


def kernel(x, p, ffn1_wg, ffn1_wu, ffn1_wd, ffn2_wg, ffn2_wu, ffn2_wd, ln_g, ln_b, ple_wg, ple_bg, ple_wp, ab_w_in, a_sinks, b_conv_w, b_conv_b, b_wa, b_ba, b_wx, b_bx, b_lam, ab_w_out, c_w_in, c_conv_w, c_a_log, c_dt_bias, c_norm_g, c_w_out, loss_target, m_ffn1_wg, m_ffn1_wu, m_ffn1_wd, m_ffn2_wg, m_ffn2_wu, m_ffn2_wd, m_ln_g, m_ln_b, m_ple_wg, m_ple_bg, m_ple_wp, m_ab_w_in, m_a_sinks, m_b_conv_w, m_b_conv_b, m_b_wa, m_b_ba, m_b_wx, m_b_bx, m_b_lam, m_ab_w_out, m_c_w_in, m_c_conv_w, m_c_a_log, m_c_dt_bias, m_c_norm_g, m_c_w_out, v_ffn1_wg, v_ffn1_wu, v_ffn1_wd, v_ffn2_wg, v_ffn2_wu, v_ffn2_wd, v_ln_g, v_ln_b, v_ple_wg, v_ple_bg, v_ple_wp, v_ab_w_in, v_a_sinks, v_b_conv_w, v_b_conv_b, v_b_wa, v_b_ba, v_b_wx, v_b_bx, v_b_lam, v_ab_w_out, v_c_w_in, v_c_conv_w, v_c_a_log, v_c_dt_bias, v_c_norm_g, v_c_w_out):
    a = dict(locals())
    return _step3(a)


def _step3(a):
    x, p = a["x"], a["p"]
    bsz, s_len, d = x.shape
    t = bsz * s_len
    x2 = x.reshape(t, d)
    tgt = a["loss_target"].reshape(t, d)
    p2 = p.reshape(DEPTH, t, D_PLE)
    shapes = {n: a[n].shape for n in WEIGHTS}
    n_small = sum(int(np.prod(shapes[n])) for n in SMALL_NAMES)
    small_all = SMALL_NAMES + REPL
    f_ff = shapes["ffn1_wg"][2]
    c_cols = shapes["c_w_in"][2]
    wide = dict(tm=1024, tn=1408, tk=1024)
    tall = dict(tm=1408, tn=1024, tk=1024)

    def cast(z):
        return z.astype(MM)

    def ffn_shards(which, l):
        return [cast(a[which + "_wg"][l]), cast(a[which + "_wu"][l]), cast(a[which + "_wd"][l])]

    def ffn_weights(gat, tag):
        return (join_cols(gat[0], name=f"join_{tag}_wg"), join_cols(gat[1], name=f"join_{tag}_wu"),
                gat[2].reshape(N_DEV * gat[2].shape[1], D_MODEL))

    def rows_full(gat):
        return gat.reshape(N_DEV * gat.shape[1], D_MODEL)

    small_send = _flat_pad([a[n] for n in SMALL_NAMES], F32, 32 * LANES).reshape(32, LANES)
    g0 = gather_multi(ffn_shards("ffn1", 0) + [small_send], name="gather_first")
    ws = _take(g0[3].reshape(N_DEV, -1), SMALL_NAMES, shapes)
    small = {n: _join(ws[n], 2) for n in SMALL_NAMES}
    ln_g, ln_b = small["ln_g"], small["ln_b"]
    wa_d, wx_d = _dense_blocks(a["b_wa"][0]), _dense_blocks(a["b_wx"][0])
    lru_w = (small["b_conv_w"][0], a["b_conv_b"][0], wa_d, a["b_ba"][0], wx_d, a["b_bx"][0], a["b_lam"][0])
    gdc_w = (a["c_a_log"][0], a["c_dt_bias"][0], a["c_norm_g"][0])
    wf = {("ffn1", 0): ffn_weights(g0[:3], "ffn1_0")}

    s0 = {"x0": x2}
    side = ("gather", ffn_shards("ffn2", 0) + [cast(a["ab_w_in"][0]), cast(a["ab_w_out"][0])])
    (s0["y1"], s0["z1"], s0["hg1"], s0["hu1"]), got = ffn_fwd(x2, *wf["ffn1", 0], ln_g[0, 0], ln_b[0, 0],
                                                             name="ffn1_fwd_0", tm=FFN_TM, tf=FFN_TF, side=side)
    wf["ffn2", 0] = ffn_weights(got[:3], "ffn2_0")
    ab_w_in, ab_w_out = join_cols(got[3], name="join_ab_in"), rows_full(got[4])
    s0["proj"] = matmul(s0["y1"], ab_w_in, mode="nn", name="ab_in_fwd", tn=896, tk=1024)
    ya = attn_fwd(s0["proj"], a["a_sinks"][0], bsz, name="attn_fwd")
    yb = lru_fwd(s0["proj"], *lru_w, bsz, name="lru_fwd")
    s0["mix"] = jnp.concatenate([ya, yb], axis=1)
    s0["y2"], s0["z2"] = mm_ln_fwd(s0["mix"], ab_w_out, s0["y1"], ln_g[0, 1], ln_b[0, 1], name="mix_out_fwd_0")
    side = ("gather", ffn_shards("ffn1", 1) + [cast(a["ple_wg"][0]), cast(a["ple_wp"][0])])
    (s0["y3"], s0["z3"], s0["hg2"], s0["hu2"]), got = ffn_fwd(s0["y2"], *wf["ffn2", 0], ln_g[0, 2], ln_b[0, 2],
                                                             name="ffn2_fwd_0", tm=FFN_TM, tf=FFN_TF, side=side)
    wf["ffn1", 1] = ffn_weights(got[:3], "ffn1_1")
    ple_wg = [rows_full(got[3]), None]
    ple_wp = [_join(got[4], 1), None]
    h1 = ple_fwd(s0["y3"], p2[0], ple_wg[0], a["ple_bg"][0], ple_wp[0], name="ple_fwd_0")

    s1 = {"x0": h1}
    side = ("gather", [cast(a["c_w_in"][0]), cast(a["c_w_out"][0])])
    (s1["y1"], s1["z1"], s1["hg1"], s1["hu1"]), got = ffn_fwd(h1, *wf["ffn1", 1], ln_g[1, 0], ln_b[1, 0],
                                                             name="ffn1_fwd_1", tm=FFN_TM, tf=FFN_TF, side=side)
    c_in_main, c_in_gate = join_cols(got[0], name="join_c_in", outs=[(0, 4 * C_WIDTH, 4 * C_WIDTH),
                                                                      (4 * C_WIDTH, 4 * C_WIDTH + 2 * C_HEADS, LANES)])
    c_w_out = rows_full(got[1])
    s1["proj"] = matmul(s1["y1"], c_in_main, mode="nn", name="c_in_fwd", tm=1024, tn=2048, tk=1024)
    s1["gates"] = matmul(s1["y1"], c_in_gate, mode="nn", name="c_gate_fwd", tk=1024)
    s1["qkv"] = gdc_pre_fwd(s1["proj"], small["c_conv_w"][0], bsz, name="gdc_pre_fwd")
    side = ("gather", ffn_shards("ffn2", 1) + [cast(a["ple_wg"][1]), cast(a["ple_wp"][1])])
    (s1["mix"], s1["o_pre"], s1["vnew"], s1["tmat"], s1["states"]), got = gdc_fwd(
        s1["qkv"], s1["proj"], s1["gates"], *gdc_w, bsz, name="gdc_fwd", side=side)
    wf["ffn2", 1] = ffn_weights(got[:3], "ffn2_1")
    ple_wg[1], ple_wp[1] = rows_full(got[3]), _join(got[4], 1)
    s1["y2"], s1["z2"] = mm_ln_fwd(s1["mix"], c_w_out, s1["y1"], ln_g[1, 1], ln_b[1, 1], name="mix_out_fwd_1")
    s1["y3"], s1["z3"], s1["hg2"], s1["hu2"] = ffn_fwd(s1["y2"], *wf["ffn2", 1], ln_g[1, 2], ln_b[1, 2], name="ffn2_fwd_1",
                                                           tm=FFN_TM, tf=FFN_TF)
    h2 = ple_fwd(s1["y3"], p2[1], ple_wg[1], a["ple_bg"][1], ple_wp[1], name="ple_fwd_1")
    loss_part, dh = loss_fwd_bwd(h2, tgt, name="loss")

    def ffn_parts(xin, act, dhg, dhu, dz, tag):
        dwg = matmul(xin, dhg, mode="tn", name=f"{tag}_wg_grad", **wide)
        dwu = matmul(xin, dhu, mode="tn", name=f"{tag}_wu_grad", **wide)
        dwd = matmul(act, dz, mode="tn", scale=0.5, out_dtype=MM, name=f"{tag}_wd_grad", **tall)
        return [split_cols([(dwg, N_DEV * f_ff)], f_ff, name=f"split_{tag}_wg"),
                split_cols([(dwu, N_DEV * f_ff)], f_ff, name=f"split_{tag}_wu"), dwd.reshape(N_DEV, f_ff, D_MODEL)]

    def ple_parts(i, s, dt, de):
        gwg = matmul(s["y3"], dt, mode="tn", out_dtype=MM, name=f"ple_wg_grad_{i}", tm=1024, tn=1024)
        gwp = matmul(p2[i], de, mode="tn", out_dtype=MM, name=f"ple_wp_grad_{i}", tn=1024)
        return [gwg.reshape(N_DEV, D_MODEL // N_DEV, D_MODEL), _split(gwp, 1)]

    gln = {"ln_g": [None, None], "ln_b": [None, None]}
    gple_bg = [None, None]

    dy3, dt, de, dbg = ple_bwd(dh, s1["y3"], p2[1], ple_wg[1], a["ple_bg"][1], ple_wp[1], name="ple_bwd_1")
    gple_bg[1] = dbg[0]
    parts_ple1 = ple_parts(1, s1, dt, de)
    dz3, dg2, db2 = ln_bwd(dy3, s1["z3"], ln_g[1, 2], name="ln2_bwd_1")
    dy2, act, dhg, dhu = ffn_bwd(dz3, s1["hg2"], s1["hu2"], *wf["ffn2", 1], name="ffn2_bwd_1", tm=FFN_TM, tf=FFN_TF)
    parts_ffn2_1 = ffn_parts(s1["y2"], act, dhg, dhu, dz3, "ffn2_1")
    dz2, dg1, db1 = ln_bwd(dy2, s1["z2"], ln_g[1, 1], name="ln1_bwd_1")
    dmix = matmul(dz2, c_w_out, mode="nt", name="c_out_bwd", tn=1024, tk=1024)
    parts_c_out = matmul(s1["mix"], dz2, mode="tn", out_dtype=MM, name="c_out_grad", tm=1024, tn=1024).reshape(
        N_DEV, D_MODEL // N_DEV, D_MODEL)
    (dqkv, dzc, dgates, dsm), recv1 = gdc_bwd(s1["qkv"], s1["proj"], s1["gates"], *gdc_w, s1["o_pre"], s1["vnew"],
                                              s1["tmat"], s1["states"], dmix, bsz, name="gdc_bwd",
                                              side=("exchange", parts_ffn2_1 + parts_ple1))
    draw, dccw = gdc_pre_bwd(s1["proj"], small["c_conv_w"][0], dqkv, bsz, name="gdc_pre_bwd")
    dproj = jnp.concatenate([draw, dzc], axis=1).astype(MM)
    dgb = dgates.astype(MM)
    dy1 = matmul(dproj, c_in_main, mode="nt", add=dz2, add_scale=DN_ALPHA, name="c_in_bwd", tn=1024, tk=4096)
    dy1 = matmul(dgb, c_in_gate, mode="nt", add=dy1, name="c_gate_bwd", tn=1024)
    g_c_main = matmul(s1["y1"], dproj, mode="tn", name="c_in_grad", tm=1024, tn=1024, tk=1024)
    g_c_gate = matmul(s1["y1"], dgb, mode="tn", name="c_gate_grad", tm=1024)
    parts_c_in = split_cols([(g_c_main, 4 * C_WIDTH), (g_c_gate, 2 * C_HEADS)], c_cols, name="split_c_in")
    dz1, dg0, db0 = ln_bwd(dy1, s1["z1"], ln_g[1, 0], name="ln0_bwd_1")
    (dh, act, dhg, dhu), recv_c = ffn_bwd(dz1, s1["hg1"], s1["hu1"], *wf["ffn1", 1], name="ffn1_bwd_1", tm=FFN_TM,
                                          tf=FFN_TF, side=("exchange", [parts_c_in, parts_c_out]))
    parts_ffn1_1 = ffn_parts(s1["x0"], act, dhg, dhu, dz1, "ffn1_1")
    gln["ln_g"][1] = jnp.concatenate([dg0, dg1, dg2], axis=0)
    gln["ln_b"][1] = jnp.concatenate([db0, db1, db2], axis=0)

    dy3, dt, de, dbg = ple_bwd(dh, s0["y3"], p2[0], ple_wg[0], a["ple_bg"][0], ple_wp[0], name="ple_bwd_0")
    gple_bg[0] = dbg[0]
    parts_ple0 = ple_parts(0, s0, dt, de)
    dz3, dg2, db2 = ln_bwd(dy3, s0["z3"], ln_g[0, 2], name="ln2_bwd_0")
    (dy2, act, dhg, dhu), recv2 = ffn_bwd(dz3, s0["hg2"], s0["hu2"], *wf["ffn2", 0], name="ffn2_bwd_0", tm=FFN_TM, tf=FFN_TF,
                                          side=("exchange", parts_ffn1_1))
    parts_ffn2_0 = ffn_parts(s0["y2"], act, dhg, dhu, dz3, "ffn2_0")
    dz2, dg1, db1 = ln_bwd(dy2, s0["z2"], ln_g[0, 1], name="ln1_bwd_0")
    dmix = matmul(dz2, ab_w_out, mode="nt", name="ab_out_bwd", tn=1024, tk=1024)
    parts_ab_out = matmul(s0["mix"], dz2, mode="tn", out_dtype=MM, name="ab_out_grad", tm=1024, tn=1024).reshape(
        N_DEV, D_MODEL // N_DEV, D_MODEL)
    (dq, dk, dv, dsk), recv3a = attn_bwd(s0["proj"], a["a_sinks"][0], dmix, bsz, name="attn_bwd",
                                         side=("exchange", parts_ffn2_0[:2]))
    (dbx, dbgate, dcw, dcb, dwa, dba, dwx, dbxb, dlam), recv3b = lru_bwd(
        s0["proj"], *lru_w, dmix, bsz, name="lru_bwd", side=("exchange", [parts_ffn2_0[2]] + parts_ple0 + [parts_ab_out]))
    dproj = jnp.concatenate([dq, dk, dv, dbx, dbgate], axis=1).astype(MM)
    dy1 = matmul(dproj, ab_w_in, mode="nt", add=dz2, add_scale=DN_ALPHA, name="ab_in_bwd", tn=1024, tk=1792)
    g_ab_in = matmul(s0["y1"], dproj, mode="tn", name="ab_in_grad", tm=1024, tn=896)
    parts_ab_in = split_cols([(g_ab_in, AB_PROJ)], AB_PROJ // N_DEV, name="split_ab_in")
    dz1, dg0, db0 = ln_bwd(dy1, s0["z1"], ln_g[0, 0], name="ln0_bwd_0")
    gln["ln_g"][0] = jnp.concatenate([dg0, dg1, dg2], axis=0)
    gln["ln_b"][0] = jnp.concatenate([db0, db1, db2], axis=0)

    dsm_sum = jnp.sum(dsm, axis=0)
    full = dict(ln_g=jnp.stack(gln["ln_g"]), ln_b=jnp.stack(gln["ln_b"]), b_conv_w=dcw[None],
                c_conv_w=jnp.sum(dccw, axis=0)[None], ple_bg=jnp.stack(gple_bg),
                a_sinks=jnp.sum(dsk, axis=0)[:, :A_HEADS], b_conv_b=dcb, b_wa=_diag_blocks(dwa)[None], b_ba=dba,
                b_wx=_diag_blocks(dwx)[None], b_bx=dbxb, b_lam=dlam, c_a_log=dsm_sum[None, :C_HEADS, 0],
                c_dt_bias=dsm_sum[None, :C_HEADS, 1], c_norm_g=jnp.sum(dsm_sum[C_HEADS:], axis=0)[None])
    small_rows = SMALL_F32 // LANES
    repl_flat = _flat_pad([full[n] for n in REPL], F32, SMALL_F32 - n_small)
    small8 = jnp.concatenate([_flat8_pad([_split(full[n], 2) for n in SMALL_NAMES], F32, n_small),
                              jnp.broadcast_to(repl_flat, (N_DEV,) + repl_flat.shape)], axis=1)
    (dh, act, dhg, dhu), recv3c = ffn_bwd(dz1, s0["hg1"], s0["hu1"], *wf["ffn1", 0], name="ffn1_bwd_0", tm=FFN_TM,
                                          tf=FFN_TF, side=("exchange", [parts_ab_in, small8.reshape(N_DEV, small_rows, LANES)]))
    grad_x = dh.reshape(bsz, s_len, d)

    dwg = matmul(s0["x0"], dhg, mode="tn", name="ffn1_0_wg_grad", **wide)
    parts_wg = split_cols([(dwg, N_DEV * f_ff)], f_ff, name="split_ffn1_0_wg")
    dwu, recv4a = matmul(s0["x0"], dhu, mode="tn", name="ffn1_0_wu_grad", side=("exchange", [parts_wg]), **wide)
    parts_wu = split_cols([(dwu, N_DEV * f_ff)], f_ff, name="split_ffn1_0_wu")
    dwd, recv4b = matmul(act, dz1, mode="tn", scale=0.5, out_dtype=MM, name="ffn1_0_wd_grad",
                         side=("exchange", [parts_wu]), **tall)
    recv4c = exchange_multi([dwd.reshape(N_DEV, f_ff, D_MODEL)], name="exchange_last")

    def upd(parts, n, l, shape2d, **kw):
        wmv = [a[pre + n][l].reshape(shape2d) for pre in ("", "m_", "v_")]
        return adamw_rows(parts, 0, *wmv, name=f"adamw_{n}_{l}", **kw)

    def upd_ffn(parts, which, l):
        return {(which + "_wg", l): upd(parts[0], which + "_wg", l, (D_MODEL, f_ff)),
                (which + "_wu", l): upd(parts[1], which + "_wu", l, (D_MODEL, f_ff)),
                (which + "_wd", l): upd(parts[2], which + "_wd", l, (f_ff, D_MODEL), tr=176)}

    rows8 = D_MODEL // N_DEV
    res = {}
    res.update(upd_ffn(recv1[:3], "ffn2", 1))
    res["ple_wg", 1] = upd(recv1[3], "ple_wg", 1, (rows8, D_MODEL), tr=128)
    res["ple_wp", 1] = upd(recv1[4], "ple_wp", 1, (D_PLE, LANES))
    res.update(upd_ffn(recv2, "ffn1", 1))
    res["c_w_in", 0] = upd(recv_c[0], "c_w_in", 0, (D_MODEL, c_cols))
    res["c_w_out", 0] = upd(recv_c[1], "c_w_out", 0, (rows8, D_MODEL), tr=128)
    res.update(upd_ffn(recv3a + recv3b[:1], "ffn2", 0))
    res["ple_wg", 0] = upd(recv3b[1], "ple_wg", 0, (rows8, D_MODEL), tr=128)
    res["ple_wp", 0] = upd(recv3b[2], "ple_wp", 0, (D_PLE, LANES))
    res["ab_w_out", 0] = upd(recv3b[3], "ab_w_out", 0, (rows8, D_MODEL), tr=128)
    res.update(upd_ffn(recv4a + recv4b + list(recv4c), "ffn1", 0))
    res["ab_w_in", 0] = upd(recv3c[0], "ab_w_in", 0, (D_MODEL, AB_PROJ // N_DEV))
    res_small = adamw_rows(recv3c[1], 0, *[_flat_pad([a[pre + n] for n in small_all], F32, SMALL_F32).reshape(
        small_rows, LANES) for pre in ("", "m_", "v_")], name="adamw_small", tr=small_rows)
    kinds = []
    for k in range(4):
        kd = _take(res_small[k].reshape(-1), small_all, shapes)
        for n in WEIGHTS:
            if n not in kd:
                kd[n] = jnp.stack([res[n, l][k] for l in range(shapes[n][0])]).reshape(shapes[n])
        kinds.append(kd)
    loss = lax.psum(loss_part[0, 0], ("x", "y", "c"))
    return (loss, grad_x, *[kinds[0][n] for n in WEIGHTS], *[kinds[1][n] for n in WEIGHTS],
            *[kinds[2][n] for n in WEIGHTS], *[kinds[3][n] for n in WEIGHTS])


def join_cols(x, *, name, outs=None, tk=256):
    _, kk, n = x.shape
    tk = _tile8(kk, tk)
    outs = outs or [(0, N_DEV * n, N_DEV * n)]

    def body(x_ref, *o_refs):
        full = jnp.concatenate([x_ref[k] for k in range(N_DEV)], axis=-1)
        for (lo, hi, wd), o_ref in zip(outs, o_refs):
            piece = full[:, lo:hi]
            if wd > hi - lo:
                piece = jnp.concatenate([piece, jnp.zeros((tk, wd - (hi - lo)), piece.dtype)], axis=-1)
            o_ref[...] = piece

    res = _pcall(
        body, name=name, grid=(kk // tk,), in_specs=[pl.BlockSpec((N_DEV, tk, n), lambda i: (0, i, 0))],
        out_specs=[pl.BlockSpec((tk, wd), lambda i: (i, 0)) for _, _, wd in outs],
        out_shape=[jax.ShapeDtypeStruct((kk, wd), x.dtype) for _, _, wd in outs], compiler_params=_cp("parallel"),
    )(x)
    return res if len(outs) > 1 else res[0]


def split_cols(pieces, n, *, name, tk=256):
    kk = pieces[0][0].shape[0]
    tk = _tile8(kk, tk)

    def body(*refs):
        o_ref = refs[-1]
        vals = [r[...][:, :used] for r, (_, used) in zip(refs[:-1], pieces)]
        full = vals[0] if len(vals) == 1 else jnp.concatenate(vals, axis=-1)
        for k in range(N_DEV):
            o_ref[k] = full[:, k * n:(k + 1) * n].astype(MM)

    return _pcall(
        body, name=name, grid=(kk // tk,),
        in_specs=[pl.BlockSpec((tk, arr.shape[1]), lambda i: (i, 0)) for arr, _ in pieces],
        out_specs=pl.BlockSpec((N_DEV, tk, n), lambda i: (0, i, 0)),
        out_shape=jax.ShapeDtypeStruct((N_DEV, kk, n), MM), compiler_params=_cp("parallel"),
    )(*[arr for arr, _ in pieces])


def gather_multi(shards, *, name):
    ng = len(shards)

    def body(*refs):
        x_refs, o_refs = refs[:ng], refs[ng:2 * ng]
        send_sems, recv_sems, local_sems = refs[2 * ng:]
        x, y, c = _me()
        sibling = (x, y, 1 - c)
        chips = [(1 - x, y), (x, 1 - y), (1 - x, 1 - y)]

        def slot(px, py, pc):
            return 4 * px + 2 * py + pc

        def copy(gi, k, block, to, src=None):
            dst = o_refs[gi].at[slot(*block)]
            return pltpu.make_async_remote_copy(
                src_ref=dst if src is None else src, dst_ref=dst, send_sem=send_sems.at[7 * gi + k],
                recv_sem=recv_sems.at[7 * gi + k], device_id=to, device_id_type=MESH_ID)

        own = [pltpu.make_async_copy(x_refs[gi], o_refs[gi].at[slot(x, y, c)], local_sems.at[gi]) for gi in range(ng)]
        for cp in own:
            cp.start()
        first = []
        for gi in range(ng):
            first.append(copy(gi, 0, (x, y, c), sibling, src=x_refs[gi]))
            first += [copy(gi, 1 + j, (x, y, c), (*chip, c), src=x_refs[gi]) for j, chip in enumerate(chips)]
        for cp in first:
            cp.start()
        passed = []
        for j, chip in enumerate(chips):
            for gi in range(ng):
                copy(gi, 1 + j, (*chip, c), (x, y, c)).wait_recv()
                fwd = copy(gi, 4 + j, (*chip, c), sibling)
                fwd.start()
                passed.append(fwd)
        for gi in range(ng):
            copy(gi, 0, sibling, (x, y, c)).wait_recv()
            for j, chip in enumerate(chips):
                copy(gi, 4 + j, (*chip, 1 - c), (x, y, c)).wait_recv()
        for cp in first + passed:
            cp.wait_send()
        for cp in own:
            cp.wait()

    hbm = pl.BlockSpec(memory_space=pl.ANY)
    return _pcall(
        body, name=name, in_specs=[hbm] * ng, out_specs=[hbm] * ng,
        out_shape=[jax.ShapeDtypeStruct((N_DEV,) + s.shape, s.dtype) for s in shards],
        scratch_shapes=[pltpu.SemaphoreType.DMA((7 * ng,)), pltpu.SemaphoreType.DMA((7 * ng,)),
                        pltpu.SemaphoreType.DMA((ng,))],
    )(*shards)


def exchange_multi(parts, *, name):
    ng = len(parts)

    def body(*refs):
        x_refs, o_refs = refs[:ng], refs[ng:2 * ng]
        send_sems, recv_sems, local_sems = refs[2 * ng:]
        x, y, c = _me()
        mine = 4 * x + 2 * y + c
        own = [pltpu.make_async_copy(x_refs[gi].at[mine], o_refs[gi].at[mine], local_sems.at[gi]) for gi in range(ng)]
        for cp in own:
            cp.start()
        copies = []
        for k, (dx, dy, dc) in enumerate(_FLIPS):
            px, py, pc = _flip(x, dx), _flip(y, dy), _flip(c, dc)
            for gi in range(ng):
                cp = pltpu.make_async_remote_copy(
                    src_ref=x_refs[gi].at[4 * px + 2 * py + pc], dst_ref=o_refs[gi].at[mine],
                    send_sem=send_sems.at[7 * gi + k], recv_sem=recv_sems.at[7 * gi + k], device_id=(px, py, pc),
                    device_id_type=MESH_ID)
                cp.start()
                copies.append(cp)
        for cp in copies:
            cp.wait()
        for cp in own:
            cp.wait()

    hbm = pl.BlockSpec(memory_space=pl.ANY)
    return _pcall(
        body, name=name, in_specs=[hbm] * ng, out_specs=[hbm] * ng,
        out_shape=[jax.ShapeDtypeStruct(s.shape, s.dtype) for s in parts],
        scratch_shapes=[pltpu.SemaphoreType.DMA((7 * ng,)), pltpu.SemaphoreType.DMA((7 * ng,)),
                        pltpu.SemaphoreType.DMA((ng,))],
    )(*parts)


def adamw_rows(parts, row0, w, m, v, *, name, tr=256):
    r, cdim = w.shape
    tr = _tile8(math.gcd(r, row0) if row0 else r, tr)
    blk0 = row0 // tr

    def body(p_ref, w_ref, m_ref, v_ref, g_ref, d_ref, mo_ref, vo_ref):
        g = p_ref[0].astype(F32)
        for j in range(1, N_DEV):
            g = g + p_ref[j].astype(F32)
        g_ref[...] = g
        mn = ADAM_B1 * m_ref[...] + (1.0 - ADAM_B1) * g
        vn = ADAM_B2 * v_ref[...] + (1.0 - ADAM_B2) * (g * g)
        mo_ref[...] = mn
        vo_ref[...] = vn
        m_hat = mn / (1.0 - ADAM_B1 ** ADAM_STEP)
        v_hat = vn / (1.0 - ADAM_B2 ** ADAM_STEP)
        d_ref[...] = -ADAM_LR * (m_hat / (jnp.sqrt(v_hat) + ADAM_EPS) + ADAM_WD * w_ref[...])

    blk = pl.BlockSpec((tr, cdim), lambda i: (i, 0))
    shp = jax.ShapeDtypeStruct((r, cdim), F32)
    return _pcall(
        body, name=name, grid=(r // tr,),
        in_specs=[pl.BlockSpec((N_DEV, tr, cdim), lambda i: (0, blk0 + i, 0)), blk, blk, blk],
        out_specs=[blk, blk, blk, blk], out_shape=[shp, shp, shp, shp], compiler_params=_cp("parallel"),
    )(parts, w, m, v)


GROUP_A = ["ffn1_wg", "ffn1_wu", "ffn2_wg", "ffn2_wu"]
GROUP_B = ["ffn1_wd", "ffn2_wd", "ple_wg", "ab_w_out", "c_w_out"]
SMALL_NAMES = ["ln_g", "ln_b", "b_conv_w", "c_conv_w"]
LANES = 128
FFN_TM = 512
FFN_TF = 1408
SMALL_F32 = 73728
PLE_WP_ROWS = DEPTH * D_PLE


def _step2(a):
    x, p = a["x"], a["p"]
    bsz, s_len, d = x.shape
    t = bsz * s_len
    x2 = x.reshape(t, d)
    tgt = a["loss_target"].reshape(t, d)
    p2 = p.reshape(DEPTH, t, D_PLE)
    shapes = {n: a[n].shape for n in WEIGHTS}
    bits_per = 1 if MM == F32 else 2
    n_small = sum(int(np.prod(shapes[n])) for n in SMALL_NAMES)
    small_all = SMALL_NAMES + REPL
    f_ff = shapes["ffn1_wg"][2]
    rows_b = {n: shapes[n][0] * shapes[n][1] for n in GROUP_B}
    off_b = dict(zip(GROUP_B, np.cumsum([0] + [rows_b[n] for n in GROUP_B])[:-1].tolist()))

    send = [
        jnp.concatenate([a[n].astype(MM).reshape(-1, f_ff) for n in GROUP_A], axis=0),
        jnp.concatenate([a[n].astype(MM).reshape(-1, D_MODEL) for n in GROUP_B], axis=0),
        a["ab_w_in"][0].astype(MM),
        a["c_w_in"][0].astype(MM),
        a["ple_wp"].astype(MM).reshape(PLE_WP_ROWS, LANES),
        _flat_pad([a[n] for n in SMALL_NAMES], F32, 32 * LANES).reshape(32, LANES),
    ]
    ga, gb, gc, gd, ge, gf = gather_multi(send, name="gather_weights")
    wa_full = join_cols(ga, name="join_ffn").reshape(len(GROUP_A), DEPTH, D_MODEL, N_DEV * f_ff)
    w = {n: wa_full[i] for i, n in enumerate(GROUP_A)}
    for n in GROUP_B:
        lyr, rws = shapes[n][0], shapes[n][1]
        blk = gb[:, off_b[n]:off_b[n] + rows_b[n]].reshape(N_DEV, lyr, rws, D_MODEL)
        w[n] = jnp.swapaxes(blk, 0, 1).reshape(lyr, N_DEV * rws, D_MODEL)
    w["ab_w_in"] = join_cols(gc, name="join_ab_in")
    c_in_main, c_in_gate = join_cols(gd, name="join_c_in", outs=[(0, 4 * C_WIDTH, 4 * C_WIDTH),
                                                                  (4 * C_WIDTH, 4 * C_WIDTH + 2 * C_HEADS, LANES)])
    w["ple_wp"] = _join(ge.reshape(N_DEV, DEPTH, D_PLE, LANES), 2)
    ws = _take(gf.reshape(N_DEV, -1), SMALL_NAMES, shapes)
    w.update({n: _join(ws[n], 2) for n in SMALL_NAMES})
    ln_g, ln_b = w["ln_g"], w["ln_b"]
    wa_d, wx_d = _dense_blocks(a["b_wa"][0]), _dense_blocks(a["b_wx"][0])
    lru_w = (w["b_conv_w"][0], a["b_conv_b"][0], wa_d, a["b_ba"][0], wx_d, a["b_bx"][0], a["b_lam"][0])
    gdc_w = (a["c_a_log"][0], a["c_dt_bias"][0], a["c_norm_g"][0])

    h = x2
    saved = []
    for i in range(DEPTH):
        s = {"x0": h}
        s["y1"], s["z1"], s["hg1"], s["hu1"] = ffn_fwd(h, w["ffn1_wg"][i], w["ffn1_wu"][i], w["ffn1_wd"][i], ln_g[i, 0], ln_b[i, 0],
                                   name=f"ffn1_fwd_{i}")
        if i == 0:
            s["proj"] = matmul(s["y1"], w["ab_w_in"], mode="nn", name="ab_in_fwd", tn=896, tk=1024)
            ya = attn_fwd(s["proj"], a["a_sinks"][0], bsz, name="attn_fwd")
            yb = lru_fwd(s["proj"], *lru_w, bsz, name="lru_fwd")
            s["mix"] = jnp.concatenate([ya, yb], axis=1)
            w_out = w["ab_w_out"][0]
        else:
            s["proj"] = matmul(s["y1"], c_in_main, mode="nn", name="c_in_fwd", tm=1024, tn=2048, tk=1024)
            s["gates"] = matmul(s["y1"], c_in_gate, mode="nn", name="c_gate_fwd", tk=1024)
            s["qkv"] = gdc_pre_fwd(s["proj"], w["c_conv_w"][0], bsz, name="gdc_pre_fwd")
            s["mix"], s["o_pre"], s["vnew"], s["tmat"], s["states"] = gdc_fwd(
                s["qkv"], s["proj"], s["gates"], *gdc_w, bsz, name="gdc_fwd")
            w_out = w["c_w_out"][0]
        s["y2"], s["z2"] = mm_ln_fwd(s["mix"], w_out, s["y1"], ln_g[i, 1], ln_b[i, 1], name=f"mix_out_fwd_{i}")
        s["y3"], s["z3"], s["hg2"], s["hu2"] = ffn_fwd(s["y2"], w["ffn2_wg"][i], w["ffn2_wu"][i], w["ffn2_wd"][i], ln_g[i, 2], ln_b[i, 2],
                                   name=f"ffn2_fwd_{i}")
        h = ple_fwd(s["y3"], p2[i], w["ple_wg"][i], a["ple_bg"][i], w["ple_wp"][i], name=f"ple_fwd_{i}")
        saved.append(s)
    loss_part, dh = loss_fwd_bwd(h, tgt, name="loss")

    g = {n: [None] * shapes[n][0] for n in ("ffn1_wg", "ffn1_wu", "ffn1_wd", "ffn2_wg", "ffn2_wu", "ffn2_wd", "ln_g",
                                             "ln_b", "ple_wg", "ple_bg", "ple_wp")}
    wide = dict(tm=1024, tn=1408, tk=1024)
    tall = dict(tm=1408, tn=1024, tk=1024)
    for i in reversed(range(DEPTH)):
        s = saved[i]
        dy3, dt, de, dbg = ple_bwd(dh, s["y3"], p2[i], w["ple_wg"][i], a["ple_bg"][i], w["ple_wp"][i], name=f"ple_bwd_{i}")
        g["ple_wg"][i] = matmul(s["y3"], dt, mode="tn", name=f"ple_wg_grad_{i}", tm=1024, tn=1024)
        g["ple_wp"][i] = matmul(p2[i], de, mode="tn", name=f"ple_wp_grad_{i}", tn=1024)
        g["ple_bg"][i] = dbg[0]
        dz3, dg2, db2 = ln_bwd(dy3, s["z3"], ln_g[i, 2], name=f"ln2_bwd_{i}")
        dy2, act, dhg, dhu = ffn_bwd(dz3, s["hg2"], s["hu2"], w["ffn2_wg"][i], w["ffn2_wu"][i], w["ffn2_wd"][i], name=f"ffn2_bwd_{i}", tm=FFN_TM, tf=FFN_TF)
        g["ffn2_wg"][i] = matmul(s["y2"], dhg, mode="tn", name=f"ffn2_wg_grad_{i}", **wide)
        g["ffn2_wu"][i] = matmul(s["y2"], dhu, mode="tn", name=f"ffn2_wu_grad_{i}", **wide)
        g["ffn2_wd"][i] = matmul(act, dz3, mode="tn", scale=0.5, name=f"ffn2_wd_grad_{i}", **tall)
        dz2, dg1, db1 = ln_bwd(dy2, s["z2"], ln_g[i, 1], name=f"ln1_bwd_{i}")
        if i == 0:
            dmix = matmul(dz2, w["ab_w_out"][0], mode="nt", name="ab_out_bwd", tn=1024, tk=1024)
            g["ab_w_out"] = matmul(s["mix"], dz2, mode="tn", name="ab_out_grad", tm=1024, tn=1024)
            dq, dk, dv, dsk = attn_bwd(s["proj"], a["a_sinks"][0], dmix, bsz, name="attn_bwd")
            dbx, dbgate, dcw, dcb, dwa, dba, dwx, dbxb, dlam = lru_bwd(s["proj"], *lru_w, dmix, bsz, name="lru_bwd")
            dproj = jnp.concatenate([dq, dk, dv, dbx, dbgate], axis=1).astype(MM)
            dy1 = matmul(dproj, w["ab_w_in"], mode="nt", add=dz2, add_scale=DN_ALPHA, name="ab_in_bwd", tn=1024, tk=1792)
            g_ab_in = matmul(s["y1"], dproj, mode="tn", name="ab_in_grad", tm=1024, tn=896)
        else:
            dmix = matmul(dz2, w["c_w_out"][0], mode="nt", name="c_out_bwd", tn=1024, tk=1024)
            g["c_w_out"] = matmul(s["mix"], dz2, mode="tn", name="c_out_grad", tm=1024, tn=1024)
            dqkv, dzc, dgates, dsm = gdc_bwd(s["qkv"], s["proj"], s["gates"], *gdc_w, s["o_pre"], s["vnew"], s["tmat"],
                                             s["states"], dmix, bsz, name="gdc_bwd")
            draw, dccw = gdc_pre_bwd(s["proj"], w["c_conv_w"][0], dqkv, bsz, name="gdc_pre_bwd")
            dproj = jnp.concatenate([draw, dzc], axis=1).astype(MM)
            dgb = dgates.astype(MM)
            dy1 = matmul(dproj, c_in_main, mode="nt", add=dz2, add_scale=DN_ALPHA, name="c_in_bwd", tn=1024, tk=4096)
            dy1 = matmul(dgb, c_in_gate, mode="nt", add=dy1, name="c_gate_bwd", tn=1024)
            g_c_main = matmul(s["y1"], dproj, mode="tn", name="c_in_grad", tm=1024, tn=1024, tk=1024)
            g_c_gate = matmul(s["y1"], dgb, mode="tn", name="c_gate_grad", tm=1024)
        dz1, dg0, db0 = ln_bwd(dy1, s["z1"], ln_g[i, 0], name=f"ln0_bwd_{i}")
        dh, act, dhg, dhu = ffn_bwd(dz1, s["hg1"], s["hu1"], w["ffn1_wg"][i], w["ffn1_wu"][i], w["ffn1_wd"][i], name=f"ffn1_bwd_{i}", tm=FFN_TM, tf=FFN_TF)
        g["ffn1_wg"][i] = matmul(s["x0"], dhg, mode="tn", name=f"ffn1_wg_grad_{i}", **wide)
        g["ffn1_wu"][i] = matmul(s["x0"], dhu, mode="tn", name=f"ffn1_wu_grad_{i}", **wide)
        g["ffn1_wd"][i] = matmul(act, dz1, mode="tn", scale=0.5, name=f"ffn1_wd_grad_{i}", **tall)
        g["ln_g"][i] = jnp.concatenate([dg0, dg1, dg2], axis=0)
        g["ln_b"][i] = jnp.concatenate([db0, db1, db2], axis=0)
    grad_x = dh.reshape(bsz, s_len, d)
    full = {n: jnp.stack(v) if isinstance(v, list) else v[None] for n, v in g.items()}
    full["b_conv_w"] = dcw[None]
    full["c_conv_w"] = jnp.sum(dccw, axis=0)[None]
    dsm_sum = jnp.sum(dsm, axis=0)
    full.update(a_sinks=jnp.sum(dsk, axis=0)[:, :A_HEADS], b_conv_b=dcb, b_wa=_diag_blocks(dwa)[None], b_ba=dba,
                b_wx=_diag_blocks(dwx)[None], b_bx=dbxb, b_lam=dlam, c_a_log=dsm_sum[None, :C_HEADS, 0],
                c_dt_bias=dsm_sum[None, :C_HEADS, 1], c_norm_g=jnp.sum(dsm_sum[C_HEADS:], axis=0)[None])

    small_f32_rows = SMALL_F32 // LANES
    repl_flat = _flat_pad([full[n] for n in REPL], F32, SMALL_F32 - n_small)
    small8 = jnp.concatenate([_flat8_pad([_split(full[n], 2) for n in SMALL_NAMES], F32, n_small),
                              jnp.broadcast_to(repl_flat, (N_DEV,) + repl_flat.shape)], axis=1)
    parts = [
        split_cols([(jnp.concatenate([full[n].reshape(-1, N_DEV * f_ff) for n in GROUP_A], axis=0), N_DEV * f_ff)], f_ff,
                   name="split_ffn"),
        jnp.concatenate([_split(full[n], 1).astype(MM).reshape(N_DEV, -1, D_MODEL) for n in GROUP_B], axis=1),
        split_cols([(g_ab_in, AB_PROJ)], AB_PROJ // N_DEV, name="split_ab_in"),
        split_cols([(g_c_main, 4 * C_WIDTH), (g_c_gate, 2 * C_HEADS)], (4 * C_WIDTH + 2 * C_HEADS) // N_DEV,
                   name="split_c_in"),
        _split(full["ple_wp"], 2).astype(MM).reshape(N_DEV, PLE_WP_ROWS, LANES),
        small8.reshape(N_DEV, small_f32_rows, LANES),
    ]
    ra, rb, rc, rd, re, small_parts = exchange_multi(parts, name="exchange_grads")

    def wmv(n, shape2d):
        return [a[pre + n].reshape(shape2d) for pre in ("", "m_", "v_")]

    res = {}
    for i, n in enumerate(GROUP_A):
        res[n] = adamw_rows(ra, i * DEPTH * D_MODEL, *wmv(n, (DEPTH * D_MODEL, f_ff)), name=f"adamw_{n}")
    for n in GROUP_B:
        res[n] = adamw_rows(rb, off_b[n], *wmv(n, (rows_b[n], D_MODEL)), name=f"adamw_{n}", tr=64)
    res["ab_w_in"] = adamw_rows(rc, 0, *wmv("ab_w_in", (D_MODEL, AB_PROJ // N_DEV)), name="adamw_ab_w_in")
    res["c_w_in"] = adamw_rows(rd, 0, *wmv("c_w_in", (D_MODEL, shapes["c_w_in"][2])), name="adamw_c_w_in")
    res["ple_wp"] = adamw_rows(re, 0, *wmv("ple_wp", (PLE_WP_ROWS, LANES)), name="adamw_ple_wp")
    res_small = adamw_rows(small_parts, 0, *[_flat_pad([a[pre + n] for n in small_all], F32, SMALL_F32).reshape(
        small_f32_rows, LANES) for pre in ("", "m_", "v_")], name="adamw_small", tr=576)
    kinds = []
    for k in range(4):
        kd = {n: res[n][k].reshape(shapes[n]) for n in res}
        kd.update(_take(res_small[k].reshape(-1), small_all, shapes))
        kinds.append(kd)
    loss = lax.psum(loss_part[0, 0], ("x", "y", "c"))
    return (loss, grad_x, *[kinds[0][n] for n in WEIGHTS], *[kinds[1][n] for n in WEIGHTS],
            *[kinds[2][n] for n in WEIGHTS], *[kinds[3][n] for n in WEIGHTS])


BIG_ROWS = 5632
SMALL_F32 = 73728


def _flat_pad(arrs, dtype, total):
    flat = jnp.concatenate([z.astype(dtype).reshape(-1) for z in arrs])
    return jnp.pad(flat, (0, total - flat.shape[0]))


def _flat8_pad(arrs, dtype, total):
    flat = jnp.concatenate([z.astype(dtype).reshape(N_DEV, -1) for z in arrs], axis=1)
    return jnp.pad(flat, ((0, 0), (0, total - flat.shape[1])))


def _bits(z):
    return z if MM == F32 else _as_bf16_bits(z)


def _unbits(z):
    return z if MM == F32 else _from_bf16_bits(z)


def _take(flat, names, shapes):
    out, off = {}, 0
    for n in names:
        sz = int(np.prod(shapes[n]))
        out[n] = flat[..., off:off + sz].reshape(flat.shape[:-1] + tuple(shapes[n]))
        off += sz
    return out


def _step(a):
    x, p = a["x"], a["p"]
    bsz, s_len, d = x.shape
    t = bsz * s_len
    x2 = x.reshape(t, d)
    tgt = a["loss_target"].reshape(t, d)
    p2 = p.reshape(DEPTH, t, D_PLE)
    shapes = {n: a[n].shape for n in WEIGHTS}
    big_names = [n for n, _ in BIG]
    small_names = [n for n, _ in SMALL]
    n_small = sum(int(np.prod(shapes[n])) for n in small_names)
    bits_per = 1 if MM == F32 else 2
    small_rows = -(-(n_small * bits_per) // PACK_ALIGN) * (PACK_ALIGN // PACK_COLS)

    send = jnp.concatenate([
        _flat_pad([a[n] for n in big_names], MM, BIG_ROWS * PACK_COLS).reshape(BIG_ROWS, PACK_COLS),
        _bits(_flat_pad([a[n] for n in small_names], F32, small_rows * PACK_COLS // bits_per)).reshape(small_rows, PACK_COLS),
    ], axis=0)
    gathered = all_gather(send, name="gather_weights")
    wb = _take(gathered[:, :BIG_ROWS].reshape(N_DEV, -1), big_names, shapes)
    ws = _take(_unbits(gathered[:, BIG_ROWS:].reshape(N_DEV, -1)), small_names, shapes)
    w = {n: _join(wb[n], ax) for n, ax in BIG}
    w.update({n: _join(ws[n], ax) for n, ax in SMALL})
    ln_g, ln_b = w["ln_g"], w["ln_b"]
    c_in_main = w["c_w_in"][0][:, :4 * C_WIDTH]
    c_in_gate = jnp.pad(w["c_w_in"][0][:, 4 * C_WIDTH:], ((0, 0), (0, 128 - 2 * C_HEADS)))
    wa_d, wx_d = _dense_blocks(a["b_wa"][0]), _dense_blocks(a["b_wx"][0])
    lru_w = (w["b_conv_w"][0], a["b_conv_b"][0], wa_d, a["b_ba"][0], wx_d, a["b_bx"][0], a["b_lam"][0])
    gdc_w = (a["c_a_log"][0], a["c_dt_bias"][0], a["c_norm_g"][0])

    h = x2
    saved = []
    for i in range(DEPTH):
        s = {"x0": h}
        s["y1"], s["z1"], s["hg1"], s["hu1"] = ffn_fwd(h, w["ffn1_wg"][i], w["ffn1_wu"][i], w["ffn1_wd"][i], ln_g[i, 0], ln_b[i, 0],
                                   name=f"ffn1_fwd_{i}")
        if i == 0:
            s["proj"] = matmul(s["y1"], w["ab_w_in"][0], mode="nn", name="ab_in_fwd")
            ya = attn_fwd(s["proj"], a["a_sinks"][0], bsz, name="attn_fwd")
            yb = lru_fwd(s["proj"], *lru_w, bsz, name="lru_fwd")
            s["mix"] = jnp.concatenate([ya, yb], axis=1)
            w_out = w["ab_w_out"][0]
        else:
            s["proj"] = matmul(s["y1"], c_in_main, mode="nn", name="c_in_fwd")
            s["gates"] = matmul(s["y1"], c_in_gate, mode="nn", name="c_gate_fwd")
            s["qkv"] = gdc_pre_fwd(s["proj"], w["c_conv_w"][0], bsz, name="gdc_pre_fwd")
            s["mix"], s["o_pre"], s["vnew"], s["tmat"], s["states"] = gdc_fwd(
                s["qkv"], s["proj"], s["gates"], *gdc_w, bsz, name="gdc_fwd")
            w_out = w["c_w_out"][0]
        s["y2"], s["z2"] = mm_ln_fwd(s["mix"], w_out, s["y1"], ln_g[i, 1], ln_b[i, 1], name=f"mix_out_fwd_{i}")
        s["y3"], s["z3"], s["hg2"], s["hu2"] = ffn_fwd(s["y2"], w["ffn2_wg"][i], w["ffn2_wu"][i], w["ffn2_wd"][i], ln_g[i, 2], ln_b[i, 2],
                                   name=f"ffn2_fwd_{i}")
        h = ple_fwd(s["y3"], p2[i], w["ple_wg"][i], a["ple_bg"][i], w["ple_wp"][i], name=f"ple_fwd_{i}")
        saved.append(s)
    loss_part, dh = loss_fwd_bwd(h, tgt, name="loss")

    g = {n: [None] * shapes[n][0] for n in ("ffn1_wg", "ffn1_wu", "ffn1_wd", "ffn2_wg", "ffn2_wu", "ffn2_wd", "ln_g",
                                             "ln_b", "ple_wg", "ple_bg", "ple_wp")}
    wide = dict(tm=1024, tn=1408, tk=1024)
    tall = dict(tm=1408, tn=1024, tk=1024)
    for i in reversed(range(DEPTH)):
        s = saved[i]
        dy3, dt, de, dbg = ple_bwd(dh, s["y3"], p2[i], w["ple_wg"][i], a["ple_bg"][i], w["ple_wp"][i], name=f"ple_bwd_{i}")
        g["ple_wg"][i] = matmul(s["y3"], dt, mode="tn", name=f"ple_wg_grad_{i}")
        g["ple_wp"][i] = matmul(p2[i], de, mode="tn", name=f"ple_wp_grad_{i}")
        g["ple_bg"][i] = dbg[0]
        dz3, dg2, db2 = ln_bwd(dy3, s["z3"], ln_g[i, 2], name=f"ln2_bwd_{i}")
        dy2, act, dhg, dhu = ffn_bwd(dz3, s["hg2"], s["hu2"], w["ffn2_wg"][i], w["ffn2_wu"][i], w["ffn2_wd"][i], name=f"ffn2_bwd_{i}", tm=FFN_TM, tf=FFN_TF)
        g["ffn2_wg"][i] = matmul(s["y2"], dhg, mode="tn", name=f"ffn2_wg_grad_{i}", **wide)
        g["ffn2_wu"][i] = matmul(s["y2"], dhu, mode="tn", name=f"ffn2_wu_grad_{i}", **wide)
        g["ffn2_wd"][i] = matmul(act, dz3, mode="tn", scale=0.5, name=f"ffn2_wd_grad_{i}", **tall)
        dz2, dg1, db1 = ln_bwd(dy2, s["z2"], ln_g[i, 1], name=f"ln1_bwd_{i}")
        if i == 0:
            dmix = matmul(dz2, w["ab_w_out"][0], mode="nt", name="ab_out_bwd")
            g["ab_w_out"] = matmul(s["mix"], dz2, mode="tn", name="ab_out_grad")
            dq, dk, dv, dsk = attn_bwd(s["proj"], a["a_sinks"][0], dmix, bsz, name="attn_bwd")
            dbx, dbgate, dcw, dcb, dwa, dba, dwx, dbxb, dlam = lru_bwd(s["proj"], *lru_w, dmix, bsz, name="lru_bwd")
            dproj = jnp.concatenate([dq, dk, dv, dbx, dbgate], axis=1).astype(MM)
            dy1 = matmul(dproj, w["ab_w_in"][0], mode="nt", add=dz2, add_scale=DN_ALPHA, name="ab_in_bwd")
            g["ab_w_in"] = matmul(s["y1"], dproj, mode="tn", name="ab_in_grad")
        else:
            dmix = matmul(dz2, w["c_w_out"][0], mode="nt", name="c_out_bwd")
            g["c_w_out"] = matmul(s["mix"], dz2, mode="tn", name="c_out_grad")
            dqkv, dzc, dgates, dsm = gdc_bwd(s["qkv"], s["proj"], s["gates"], *gdc_w, s["o_pre"], s["vnew"], s["tmat"],
                                             s["states"], dmix, bsz, name="gdc_bwd")
            draw, dccw = gdc_pre_bwd(s["proj"], w["c_conv_w"][0], dqkv, bsz, name="gdc_pre_bwd")
            dproj = jnp.concatenate([draw, dzc], axis=1).astype(MM)
            dgb = dgates.astype(MM)
            dy1 = matmul(dproj, c_in_main, mode="nt", add=dz2, add_scale=DN_ALPHA, name="c_in_bwd")
            dy1 = matmul(dgb, c_in_gate, mode="nt", add=dy1, name="c_gate_bwd")
            g["c_w_in"] = jnp.concatenate([matmul(s["y1"], dproj, mode="tn", name="c_in_grad"),
                                           matmul(s["y1"], dgb, mode="tn", name="c_gate_grad")[:, :2 * C_HEADS]], axis=1)
        dz1, dg0, db0 = ln_bwd(dy1, s["z1"], ln_g[i, 0], name=f"ln0_bwd_{i}")
        dh, act, dhg, dhu = ffn_bwd(dz1, s["hg1"], s["hu1"], w["ffn1_wg"][i], w["ffn1_wu"][i], w["ffn1_wd"][i], name=f"ffn1_bwd_{i}", tm=FFN_TM, tf=FFN_TF)
        g["ffn1_wg"][i] = matmul(s["x0"], dhg, mode="tn", name=f"ffn1_wg_grad_{i}", **wide)
        g["ffn1_wu"][i] = matmul(s["x0"], dhu, mode="tn", name=f"ffn1_wu_grad_{i}", **wide)
        g["ffn1_wd"][i] = matmul(act, dz1, mode="tn", scale=0.5, name=f"ffn1_wd_grad_{i}", **tall)
        g["ln_g"][i] = jnp.concatenate([dg0, dg1, dg2], axis=0)
        g["ln_b"][i] = jnp.concatenate([db0, db1, db2], axis=0)
    grad_x = dh.reshape(bsz, s_len, d)
    full = {n: jnp.stack(v) if isinstance(v, list) else v[None] for n, v in g.items()}
    full["b_conv_w"] = dcw[None]
    full["c_conv_w"] = jnp.sum(dccw, axis=0)[None]
    dsm_sum = jnp.sum(dsm, axis=0)
    full.update(a_sinks=jnp.sum(dsk, axis=0)[:, :A_HEADS], b_conv_b=dcb, b_wa=_diag_blocks(dwa)[None], b_ba=dba,
                b_wx=_diag_blocks(dwx)[None], b_bx=dbxb, b_lam=dlam, c_a_log=dsm_sum[None, :C_HEADS, 0],
                c_dt_bias=dsm_sum[None, :C_HEADS, 1], c_norm_g=jnp.sum(dsm_sum[C_HEADS:], axis=0)[None])

    small_cols = SMALL_F32 * bits_per // PACK_COLS
    repl_flat = _flat_pad([full[n] for n in REPL], F32, SMALL_F32 - n_small)
    small8 = jnp.concatenate([_flat8_pad([_split(full[n], ax) for n, ax in SMALL], F32, n_small),
                              jnp.broadcast_to(repl_flat, (N_DEV,) + repl_flat.shape)], axis=1)
    parts = jnp.concatenate([
        _flat8_pad([_split(full[n], ax) for n, ax in BIG], MM, BIG_ROWS * PACK_COLS).reshape(N_DEV, BIG_ROWS, PACK_COLS),
        _bits(small8).reshape(N_DEV, small_cols, PACK_COLS)], axis=1)
    recv = all_to_all(parts, name="exchange_grads")

    def mine(prefix, names, dtype_total):
        return _flat_pad([a[prefix + n] for n in names], F32, dtype_total)

    outs = {}
    big_total = BIG_ROWS * PACK_COLS
    res_big = adamw_sum(recv, *[mine(pre, big_names, big_total).reshape(BIG_ROWS, PACK_COLS) for pre in ("", "m_", "v_")],
                        name="adamw_big")
    small_all = small_names + REPL
    cols_f32 = PACK_COLS // bits_per
    res_small = adamw_sum(_unbits(recv[:, BIG_ROWS:]).reshape(N_DEV, small_cols, cols_f32),
                          *[mine(pre, small_all, SMALL_F32).reshape(small_cols, cols_f32) for pre in ("", "m_", "v_")],
                          name="adamw_small")
    kinds = []
    for rb, rs in zip(res_big, res_small):
        k = _take(rb.reshape(-1), big_names, shapes)
        k.update(_take(rs.reshape(-1), small_all, shapes))
        kinds.append(k)
    loss = lax.psum(loss_part[0, 0], ("x", "y", "c"))
    return (loss, grad_x, *[kinds[0][n] for n in WEIGHTS], *[kinds[1][n] for n in WEIGHTS],
            *[kinds[2][n] for n in WEIGHTS], *[kinds[3][n] for n in WEIGHTS])
```

```python
import functools
import math

import numpy as np
import jax
import jax.numpy as jnp
from jax import lax
from jax.experimental import pallas as pl
from jax.experimental.pallas import tpu as pltpu

F32 = jnp.float32
MM = jnp.bfloat16
HI = lax.Precision.HIGHEST

D_MODEL = 1024
D_FF = 2816
D_PLE = 256
DEPTH = 2
CHUNK = 64
A_HEADS = 8
A_KV_HEADS = 2
A_GROUP = 4
A_HEAD_DIM = 64
A_WIDTH = 512
A_KV_WIDTH = 128
B_WIDTH = 512
B_BLOCK = 64
RG_C = 8.0
AB_PROJ = 1792
C_HEADS = 8
C_HEAD_DIM = 128
C_WIDTH = 1024
DN_ALPHA = (2.0 * DEPTH) ** 0.25
LN_EPS = 1e-5
NORM_EPS = 1e-6
NEG = -1e30
ADAM_LR = 0.001
ADAM_B1 = 0.9
ADAM_B2 = 0.999
ADAM_EPS = 1e-08
ADAM_WD = 0.01
ADAM_STEP = 10
N_DEV = 8
VMEM_LIMIT = 56 * 1024 * 1024

NN = ((1,), (0,))
NT = ((1,), (1,))
TN = ((0,), (0,))


def _pcall(body, **kw):
    return pl.pallas_call(body, **kw)


def _cp(*sem):
    return pltpu.CompilerParams(dimension_semantics=sem, vmem_limit_bytes=VMEM_LIMIT)


MESH_ID = pl.DeviceIdType.MESH
_FLIPS = [(0, 0, 1), (1, 0, 0), (0, 1, 0), (1, 1, 0), (1, 0, 1), (0, 1, 1), (1, 1, 1)]


def _me():
    return lax.axis_index("x"), lax.axis_index("y"), lax.axis_index("c")


def _flip(coord, d):
    return 1 - coord if d else coord


def _side_copies(kind, x_refs, o_refs, send_sems, recv_sems, local_sems, start):
    x, y, c = _me()
    mine = 4 * x + 2 * y + c
    for gi, (x_ref, o_ref) in enumerate(zip(x_refs, o_refs)):
        src_own = x_ref if kind == "gather" else x_ref.at[mine]
        own = pltpu.make_async_copy(src_own, o_ref.at[mine], local_sems.at[gi])
        own.start() if start else own.wait()
        for k, (dx, dy, dc) in enumerate(_FLIPS):
            px, py, pc = _flip(x, dx), _flip(y, dy), _flip(c, dc)
            src = x_ref if kind == "gather" else x_ref.at[4 * px + 2 * py + pc]
            cp = pltpu.make_async_remote_copy(
                src_ref=src, dst_ref=o_ref.at[mine], send_sem=send_sems.at[7 * gi + k], recv_sem=recv_sems.at[7 * gi + k],
                device_id=(px, py, pc), device_id_type=MESH_ID)
            cp.start() if start else cp.wait()


def _call(body, args, side, grid, **kw):
    if side is None:
        return _pcall(body, grid=grid, **kw)(*args)
    kind, arrs = side
    ns, n_in, n_out = len(arrs), len(args), len(kw["out_specs"])
    scratch = list(kw.get("scratch_shapes", []))
    n_scr = len(scratch)

    def edge(at_end):
        conds = [pl.program_id(ax) == (n - 1 if at_end else 0) for ax, n in enumerate(grid)]
        return functools.reduce(jnp.logical_and, conds)

    def wrapped(*refs):
        ins, sx = refs[:n_in], refs[n_in:n_in + ns]
        outs, so = refs[n_in + ns:n_in + ns + n_out], refs[n_in + ns + n_out:n_in + 2 * ns + n_out]
        rest = refs[n_in + 2 * ns + n_out:]
        scr, sems = rest[:n_scr], rest[n_scr:]

        @pl.when(edge(False))
        def _():
            _side_copies(kind, sx, so, *sems, start=True)

        body(*ins, *outs, *scr)

        @pl.when(edge(True))
        def _():
            _side_copies(kind, sx, so, *sems, start=False)

    hbm = pl.BlockSpec(memory_space=pl.ANY)
    side_shapes = [jax.ShapeDtypeStruct(((N_DEV,) if kind == "gather" else ()) + z.shape, z.dtype) for z in arrs]
    kw = dict(kw)
    kw["in_specs"] = list(kw["in_specs"]) + [hbm] * ns
    kw["out_specs"] = list(kw["out_specs"]) + [hbm] * ns
    kw["out_shape"] = list(kw["out_shape"]) + side_shapes
    kw["scratch_shapes"] = scratch + [pltpu.SemaphoreType.DMA((7 * ns,)), pltpu.SemaphoreType.DMA((7 * ns,)),
                                      pltpu.SemaphoreType.DMA((ns,))]
    kw["compiler_params"] = _cp(*["arbitrary"] * len(grid))
    res = _pcall(wrapped, grid=grid, **kw)(*args, *arrs)
    return list(res[:n_out]), list(res[n_out:])


def _dot(a, b, dims=NN, precision=None):
    return lax.dot_general(a, b, (dims, ((), ())), preferred_element_type=F32, precision=precision)


def _mdot(a, b, dims=NN):
    return _dot(a.astype(MM), b.astype(MM), dims)


def _tile(n, pref):
    if n <= pref:
        return n
    for c in range(pref - pref % 128, 0, -128):
        if n % c == 0:
            return c
    return n


def _sigmoid(x):
    return 1.0 / (1.0 + jnp.exp(-x))


def _softplus(x):
    return jnp.maximum(x, 0.0) + jnp.log(1.0 + jnp.exp(-jnp.abs(x)))


def _ln_stats(z):
    mu = jnp.mean(z, axis=-1, keepdims=True)
    zc = z - mu
    var = jnp.mean(zc * zc, axis=-1, keepdims=True)
    return zc, lax.rsqrt(var + LN_EPS)


def matmul(a, b, *, mode, name, tm=512, tn=512, tk=512, out_dtype=F32, scale=None, add=None, add_scale=1.0, side=None):
    if mode == "nn":
        (m, kk), (_, n) = a.shape, b.shape
        dims = NN
    elif mode == "nt":
        (m, kk), (n, _) = a.shape, b.shape
        dims = NT
    else:
        (kk, m), (_, n) = a.shape, b.shape
        dims = TN
    tm, tn, tk = _tile(m, tm), _tile(n, tn), _tile(kk, tk)
    if mode == "nn":
        a_spec = pl.BlockSpec((tm, tk), lambda i, j, k: (i, k))
        b_spec = pl.BlockSpec((tk, tn), lambda i, j, k: (k, j))
    elif mode == "nt":
        a_spec = pl.BlockSpec((tm, tk), lambda i, j, k: (i, k))
        b_spec = pl.BlockSpec((tn, tk), lambda i, j, k: (j, k))
    else:
        a_spec = pl.BlockSpec((tk, tm), lambda i, j, k: (k, i))
        b_spec = pl.BlockSpec((tk, tn), lambda i, j, k: (k, j))
    nk = kk // tk
    o_spec = pl.BlockSpec((tm, tn), lambda i, j, k: (i, j))
    has_add = add is not None

    def body(*refs):
        if has_add:
            a_ref, b_ref, add_ref, o_ref, acc_ref = refs
        else:
            a_ref, b_ref, o_ref, acc_ref = refs
        k = pl.program_id(2)

        @pl.when(k == 0)
        def _():
            acc_ref[...] = jnp.zeros_like(acc_ref)

        acc_ref[...] += _mdot(a_ref[...], b_ref[...], dims)

        @pl.when(k == nk - 1)
        def _():
            r = acc_ref[...]
            if scale is not None:
                r = r * scale
            if has_add:
                r = r + add_scale * add_ref[...].astype(F32)
            o_ref[...] = r.astype(out_dtype)

    ins = [a, b] + ([add] if has_add else [])
    in_specs = [a_spec, b_spec] + ([o_spec] if has_add else [])
    res = _call(
        body, ins, side, (m // tm, n // tn, nk), name=name, in_specs=in_specs, out_specs=[o_spec],
        out_shape=[jax.ShapeDtypeStruct((m, n), out_dtype)], scratch_shapes=[pltpu.VMEM((tm, tn), F32)],
        compiler_params=_cp("parallel", "parallel", "arbitrary"),
    )
    return res[0] if side is None else (res[0][0], res[1])


def ffn_fwd(x, wg, wu, wd, g, b, *, name, tm=512, tf=256, side=None):
    t, d = x.shape
    f = wg.shape[1]
    tm = min(tm, t)
    nj = f // tf

    def body(x_ref, wg_ref, wu_ref, wd_ref, g_ref, b_ref, y_ref, z_ref, hg_ref, hu_ref, xb_ref, acc_ref):
        j = pl.program_id(1)

        @pl.when(j == 0)
        def _():
            xb_ref[...] = x_ref[...].astype(MM)
            acc_ref[...] = jnp.zeros_like(acc_ref)

        xb = xb_ref[...]
        hg = _dot(xb, wg_ref[...])
        hu = _dot(xb, wu_ref[...])
        hg_ref[...] = hg.astype(MM)
        hu_ref[...] = hu.astype(MM)
        act = (hg * _sigmoid(hg) * hu).astype(MM)
        acc_ref[...] += _dot(act, wd_ref[...])

        @pl.when(j == nj - 1)
        def _():
            z = DN_ALPHA * x_ref[...] + 0.5 * acc_ref[...]
            z_ref[...] = z
            zc, rstd = _ln_stats(z)
            y_ref[...] = zc * rstd * g_ref[...] + b_ref[...]

    row = pl.BlockSpec((tm, d), lambda i, j: (i, 0))
    hid = pl.BlockSpec((tm, tf), lambda i, j: (i, j))
    vec = pl.BlockSpec((1, d), lambda i, j: (0, 0))
    return _call(
        body, (x, wg, wu, wd, g.reshape(1, d), b.reshape(1, d)), side, (t // tm, nj), name=name,
        in_specs=[row, pl.BlockSpec((d, tf), lambda i, j: (0, j)), pl.BlockSpec((d, tf), lambda i, j: (0, j)),
                  pl.BlockSpec((tf, d), lambda i, j: (j, 0)), vec, vec],
        out_specs=[row, row, hid, hid],
        out_shape=[jax.ShapeDtypeStruct((t, d), F32), jax.ShapeDtypeStruct((t, d), F32),
                   jax.ShapeDtypeStruct((t, f), MM), jax.ShapeDtypeStruct((t, f), MM)],
        scratch_shapes=[pltpu.VMEM((tm, d), MM), pltpu.VMEM((tm, d), F32)],
        compiler_params=_cp("parallel", "arbitrary"),
    )


def ffn_bwd(dz, hg, hu, wg, wu, wd, *, name, tm=512, tf=256, side=None):
    t, d = dz.shape
    f = wg.shape[1]
    tm = min(tm, t)
    nj = f // tf

    def body(dz_ref, hg_ref, hu_ref, wg_ref, wu_ref, wd_ref, dx_ref, act_ref, dhg_ref, dhu_ref, dfb_ref, acc_ref):
        j = pl.program_id(1)

        @pl.when(j == 0)
        def _():
            dfb_ref[...] = (0.5 * dz_ref[...]).astype(MM)
            acc_ref[...] = jnp.zeros_like(acc_ref)

        hg = hg_ref[...].astype(F32)
        hu = hu_ref[...].astype(F32)
        s = _sigmoid(hg)
        dact = _dot(dfb_ref[...], wd_ref[...], NT)
        sg = hg * s
        act_ref[...] = (sg * hu).astype(MM)
        dhu = (dact * sg).astype(MM)
        dhg = (dact * hu * (s + sg * (1.0 - s))).astype(MM)
        dhu_ref[...] = dhu
        dhg_ref[...] = dhg
        acc_ref[...] += _dot(dhg, wg_ref[...], NT) + _dot(dhu, wu_ref[...], NT)

        @pl.when(j == nj - 1)
        def _():
            dx_ref[...] = DN_ALPHA * dz_ref[...] + acc_ref[...]

    row = pl.BlockSpec((tm, d), lambda i, j: (i, 0))
    hid = pl.BlockSpec((tm, tf), lambda i, j: (i, j))
    return _call(
        body, (dz, hg, hu, wg, wu, wd), side, (t // tm, nj), name=name,
        in_specs=[row, hid, hid, pl.BlockSpec((d, tf), lambda i, j: (0, j)), pl.BlockSpec((d, tf), lambda i, j: (0, j)),
                  pl.BlockSpec((tf, d), lambda i, j: (j, 0))],
        out_specs=[row, hid, hid, hid],
        out_shape=[jax.ShapeDtypeStruct((t, d), F32)] + [jax.ShapeDtypeStruct((t, f), MM)] * 3,
        scratch_shapes=[pltpu.VMEM((tm, d), MM), pltpu.VMEM((tm, d), F32)],
        compiler_params=_cp("parallel", "arbitrary"),
    )


def ln_bwd(dy, z, g, *, name, tm=512):
    t, d = z.shape
    tm = min(tm, t)

    def body(dy_ref, z_ref, g_ref, dz_ref, dg_ref, db_ref):
        i = pl.program_id(0)

        @pl.when(i == 0)
        def _():
            dg_ref[...] = jnp.zeros_like(dg_ref)
            db_ref[...] = jnp.zeros_like(db_ref)

        dy = dy_ref[...]
        zc, rstd = _ln_stats(z_ref[...])
        xh = zc * rstd
        dg_ref[...] += jnp.sum(dy * xh, axis=0, keepdims=True)
        db_ref[...] += jnp.sum(dy, axis=0, keepdims=True)
        dxh = dy * g_ref[...]
        m1 = jnp.mean(dxh, axis=-1, keepdims=True)
        m2 = jnp.mean(dxh * xh, axis=-1, keepdims=True)
        dz_ref[...] = rstd * (dxh - m1 - xh * m2)

    row = pl.BlockSpec((tm, d), lambda i: (i, 0))
    vec = pl.BlockSpec((1, d), lambda i: (0, 0))
    return _pcall(
        body, name=name, grid=(t // tm,), in_specs=[row, row, vec], out_specs=[row, vec, vec],
        out_shape=[jax.ShapeDtypeStruct((t, d), F32), jax.ShapeDtypeStruct((1, d), F32), jax.ShapeDtypeStruct((1, d), F32)],
        compiler_params=_cp("arbitrary"),
    )(dy, z, g.reshape(1, d))


def mm_ln_fwd(a, w, res, g, b, *, name, tm=512):
    t, kk = a.shape
    d = w.shape[1]
    tm = min(tm, t)

    def body(a_ref, w_ref, res_ref, g_ref, b_ref, y_ref, z_ref):
        z = DN_ALPHA * res_ref[...] + _mdot(a_ref[...], w_ref[...])
        z_ref[...] = z
        zc, rstd = _ln_stats(z)
        y_ref[...] = zc * rstd * g_ref[...] + b_ref[...]

    row = pl.BlockSpec((tm, d), lambda i: (i, 0))
    vec = pl.BlockSpec((1, d), lambda i: (0, 0))
    return _pcall(
        body, name=name, grid=(t // tm,),
        in_specs=[pl.BlockSpec((tm, kk), lambda i: (i, 0)), pl.BlockSpec((kk, d), lambda i: (0, 0)), row, vec, vec],
        out_specs=[row, row],
        out_shape=[jax.ShapeDtypeStruct((t, d), F32), jax.ShapeDtypeStruct((t, d), F32)],
        compiler_params=_cp("parallel"),
    )(a, w, res, g.reshape(1, d), b.reshape(1, d))


def ple_fwd(y, p, wg, bg, wp, *, name, tm=512):
    t, d = y.shape
    dp = p.shape[1]
    tm = min(tm, t)

    def body(y_ref, p_ref, wg_ref, bg_ref, wp_ref, o_ref):
        yv = y_ref[...]
        gate = _sigmoid(_mdot(yv, wg_ref[...]) + bg_ref[...])
        o_ref[...] = yv + gate * _mdot(p_ref[...], wp_ref[...])

    row = pl.BlockSpec((tm, d), lambda i: (i, 0))
    return _pcall(
        body, name=name, grid=(t // tm,),
        in_specs=[row, pl.BlockSpec((tm, dp), lambda i: (i, 0)), pl.BlockSpec((d, d), lambda i: (0, 0)),
                  pl.BlockSpec((1, d), lambda i: (0, 0)), pl.BlockSpec((dp, d), lambda i: (0, 0))],
        out_specs=row, out_shape=jax.ShapeDtypeStruct((t, d), F32), compiler_params=_cp("parallel"),
    )(y, p, wg, bg.reshape(1, d), wp)


def ple_bwd(do, y, p, wg, bg, wp, *, name, tm=512):
    t, d = y.shape
    dp = p.shape[1]
    tm = min(tm, t)

    def body(do_ref, y_ref, p_ref, wg_ref, bg_ref, wp_ref, dy_ref, dt_ref, de_ref, dbg_ref):
        i = pl.program_id(0)

        @pl.when(i == 0)
        def _():
            dbg_ref[...] = jnp.zeros_like(dbg_ref)

        dov = do_ref[...]
        gate = _sigmoid(_mdot(y_ref[...], wg_ref[...]) + bg_ref[...])
        emb = _mdot(p_ref[...], wp_ref[...])
        dt = dov * emb * gate * (1.0 - gate)
        dbg_ref[...] += jnp.sum(dt, axis=0, keepdims=True)
        dtb = dt.astype(MM)
        dt_ref[...] = dtb
        de_ref[...] = (dov * gate).astype(MM)
        dy_ref[...] = dov + _dot(dtb, wg_ref[...], NT)

    row = pl.BlockSpec((tm, d), lambda i: (i, 0))
    vec = pl.BlockSpec((1, d), lambda i: (0, 0))
    return _pcall(
        body, name=name, grid=(t // tm,),
        in_specs=[row, row, pl.BlockSpec((tm, dp), lambda i: (i, 0)), pl.BlockSpec((d, d), lambda i: (0, 0)),
                  vec, pl.BlockSpec((dp, d), lambda i: (0, 0))],
        out_specs=[row, row, row, vec],
        out_shape=[jax.ShapeDtypeStruct((t, d), F32), jax.ShapeDtypeStruct((t, d), MM),
                   jax.ShapeDtypeStruct((t, d), MM), jax.ShapeDtypeStruct((1, d), F32)],
        compiler_params=_cp("arbitrary"),
    )(do, y, p, wg, bg.reshape(1, d), wp)


def loss_fwd_bwd(y, tgt, *, name, tm=512):
    t, d = y.shape
    tm = min(tm, t)

    def body(y_ref, t_ref, l_ref, dy_ref):
        i = pl.program_id(0)

        @pl.when(i == 0)
        def _():
            l_ref[...] = jnp.zeros_like(l_ref)

        err = y_ref[...] - t_ref[...]
        dy_ref[...] = err * (1.0 / d)
        l_ref[...] += (0.5 / d) * jnp.sum(jnp.sum(err * err, axis=1, keepdims=True), axis=0, keepdims=True)

    row = pl.BlockSpec((tm, d), lambda i: (i, 0))
    return _pcall(
        body, name=name, grid=(t // tm,), in_specs=[row, row],
        out_specs=[pl.BlockSpec((1, 128), lambda i: (0, 0)), row],
        out_shape=[jax.ShapeDtypeStruct((1, 128), F32), jax.ShapeDtypeStruct((t, d), F32)],
        compiler_params=_cp("arbitrary"),
    )(y, tgt)


def _shift_dn(x, s, row):
    return x if s == 0 else jnp.where(row >= s, pltpu.roll(x, s, 0), 0.0)


def _shift_up(x, s, row):
    n = x.shape[0]
    return x if s == 0 else jnp.where(row < n - s, pltpu.roll(x, n - s, 0), 0.0)


def _conv_fwd(x, w, row):
    kk = w.shape[0]
    y = w[kk - 1:kk, :] * x
    for j in range(kk - 1):
        y = y + w[j:j + 1, :] * _shift_dn(x, kk - 1 - j, row)
    return y


def _conv_bwd(x, w, dy, row):
    kk = w.shape[0]
    dx = w[kk - 1:kk, :] * dy
    dws = []
    for j in range(kk - 1):
        dx = dx + w[j:j + 1, :] * _shift_up(dy, kk - 1 - j, row)
        dws.append(jnp.sum(dy * _shift_dn(x, kk - 1 - j, row), axis=0, keepdims=True))
    dws.append(jnp.sum(dy * x, axis=0, keepdims=True))
    return dx, jnp.concatenate(dws, axis=0)


def _gelu(x):
    c = math.sqrt(2.0 / math.pi)
    th = jnp.tanh(c * (x + 0.044715 * x * x * x))
    return 0.5 * x * (1.0 + th), th


def _gelu_grad(x, th):
    c = math.sqrt(2.0 / math.pi)
    return 0.5 * (1.0 + th) + 0.5 * x * (1.0 - th * th) * c * (1.0 + 3.0 * 0.044715 * x * x)


def _neg_expm1(y):
    ser = -(y * (1.0 + y * (0.5 + y * (1.0 / 6.0 + y * (1.0 / 24.0 + y * (1.0 / 120.0))))))
    return jnp.where(y > -0.05, ser, 1.0 - jnp.exp(y))


def _attn_head(qh, kk, vv, bias, valid, sink):
    s = _mdot(qh, kk, NT) * (A_HEAD_DIM ** -0.5) - bias
    s = jnp.where(valid, s, NEG)
    m = jnp.maximum(jnp.max(s, axis=-1, keepdims=True), sink)
    pr = jnp.exp(s - m)
    den = jnp.sum(pr, axis=-1, keepdims=True) + jnp.exp(sink - m)
    return pr / den, jnp.exp(sink - m) / den


def _attn_valid(n):
    ji = lax.broadcasted_iota(jnp.int32, (1, 3 * CHUNK), 1)
    return (n * CHUNK + ji - 2 * CHUNK) >= 0


def _attn_group_consts(kh, sk_ref):
    rows = A_GROUP * CHUNK
    ri = lax.broadcasted_iota(jnp.int32, (rows, 3 * CHUNK), 0)
    ji = lax.broadcasted_iota(jnp.int32, (rows, 3 * CHUNK), 1)
    dist = jnp.abs((ri & (CHUNK - 1)) + 2 * CHUNK - ji).astype(F32)
    rcol = lax.broadcasted_iota(jnp.int32, (rows, 1), 0)
    slope = jnp.zeros((rows, 1), F32)
    sink = jnp.zeros((rows, 1), F32)
    for gi in range(A_GROUP):
        h = kh * A_GROUP + gi
        inblk = (rcol >= gi * CHUNK) & (rcol < (gi + 1) * CHUNK)
        slope = jnp.where(inblk, 2.0 ** -(h + 1), slope)
        sink = jnp.where(inblk, sk_ref[h], sink)
    return slope * dist, sink


def _stack_heads(x, kh):
    return jnp.concatenate([x[:, (kh * A_GROUP + gi) * 64:(kh * A_GROUP + gi + 1) * 64] for gi in range(A_GROUP)], axis=0)


def _attn_masks(n):
    ci = lax.broadcasted_iota(jnp.int32, (CHUNK, 3 * CHUNK), 0)
    ji = lax.broadcasted_iota(jnp.int32, (CHUNK, 3 * CHUNK), 1)
    dist = jnp.abs(ci + 2 * CHUNK - ji).astype(F32)
    valid = (n * CHUNK + ji - 2 * CHUNK) >= 0
    return dist, valid


def attn_fwd(proj, sinks, bsz, *, name, side=None):
    t = proj.shape[0]
    s_len = t // bsz
    nc = s_len // CHUNK
    pad = 2 * CHUNK

    def body(q_ref, k_ref, v_ref, sk_ref, o_ref, kp_ref, vp_ref):
        kp_ref[0:pad, :] = jnp.zeros((pad, A_KV_WIDTH), F32)
        vp_ref[0:pad, :] = jnp.zeros((pad, A_KV_WIDTH), F32)
        kp_ref[pad:, :] = k_ref[...]
        vp_ref[pad:, :] = v_ref[...]

        consts = [_attn_group_consts(kh, sk_ref) for kh in range(A_KV_HEADS)]

        def chunk(n, carry):
            st = pl.multiple_of(n * CHUNK, CHUNK)
            q = q_ref[pl.ds(st, CHUNK), :]
            kb = kp_ref[pl.ds(st, 3 * CHUNK), :]
            vb = vp_ref[pl.ds(st, 3 * CHUNK), :]
            valid = _attn_valid(n)
            outs = []
            for kh in range(A_KV_HEADS):
                bias, sink = consts[kh]
                pn, _ = _attn_head(_stack_heads(q, kh), kb[:, kh * 64:(kh + 1) * 64], None, bias, valid, sink)
                o = _mdot(pn, vb[:, kh * 64:(kh + 1) * 64])
                outs += [o[gi * CHUNK:(gi + 1) * CHUNK] for gi in range(A_GROUP)]
            o_ref[pl.ds(st, CHUNK), :] = jnp.concatenate(outs, axis=-1)
            return carry

        lax.fori_loop(0, nc, chunk, 0)

    res = _call(
        body, (proj, proj, proj, sinks), side, (bsz,), name=name,
        in_specs=[pl.BlockSpec((s_len, A_WIDTH), lambda b: (b, 0)), pl.BlockSpec((s_len, 128), lambda b: (b, 4)),
                  pl.BlockSpec((s_len, 128), lambda b: (b, 5)), pl.BlockSpec(memory_space=pltpu.SMEM)],
        out_specs=[pl.BlockSpec((s_len, A_WIDTH), lambda b: (b, 0))],
        out_shape=[jax.ShapeDtypeStruct((t, A_WIDTH), F32)],
        scratch_shapes=[pltpu.VMEM((s_len + pad, A_KV_WIDTH), F32), pltpu.VMEM((s_len + pad, A_KV_WIDTH), F32)],
        compiler_params=_cp("parallel"),
    )
    return res[0] if side is None else (res[0][0], res[1])


def attn_bwd(proj, sinks, dcat, bsz, *, name, side=None):
    t = proj.shape[0]
    s_len = t // bsz
    nc = s_len // CHUNK
    pad = 2 * CHUNK

    def body(q_ref, k_ref, v_ref, do_ref, sk_ref, dq_ref, dk_ref, dv_ref, dsk_ref, kp_ref, vp_ref, dkp_ref, dvp_ref):
        kp_ref[0:pad, :] = jnp.zeros((pad, A_KV_WIDTH), F32)
        vp_ref[0:pad, :] = jnp.zeros((pad, A_KV_WIDTH), F32)
        kp_ref[pad:, :] = k_ref[...]
        vp_ref[pad:, :] = v_ref[...]
        dkp_ref[...] = jnp.zeros_like(dkp_ref)
        dvp_ref[...] = jnp.zeros_like(dvp_ref)
        lane = lax.broadcasted_iota(jnp.int32, (1, 128), 1)

        consts = [_attn_group_consts(kh, sk_ref) for kh in range(A_KV_HEADS)]

        def chunk(n, dsk):
            st = pl.multiple_of(n * CHUNK, CHUNK)
            q = q_ref[pl.ds(st, CHUNK), :]
            do = do_ref[pl.ds(st, CHUNK), :]
            kb = kp_ref[pl.ds(st, 3 * CHUNK), :]
            vb = vp_ref[pl.ds(st, 3 * CHUNK), :]
            valid = _attn_valid(n)
            dqs, dks, dvs = [], [], []
            for kh in range(A_KV_HEADS):
                kk = kb[:, kh * 64:(kh + 1) * 64]
                vv = vb[:, kh * 64:(kh + 1) * 64]
                bias, sink = consts[kh]
                qs = _stack_heads(q, kh)
                dos = _stack_heads(do, kh)
                pn, psink = _attn_head(qs, kk, None, bias, valid, sink)
                dp = _mdot(dos, vv, NT)
                rowdot = jnp.sum(pn * dp, axis=-1, keepdims=True)
                ds = pn * (dp - rowdot)
                sink_part = psink * rowdot
                for gi in range(A_GROUP):
                    part = jnp.sum(sink_part[gi * CHUNK:(gi + 1) * CHUNK], axis=0, keepdims=True)
                    dsk = dsk + jnp.where(lane == kh * A_GROUP + gi, -part, 0.0)
                dq = _mdot(ds, kk) * (A_HEAD_DIM ** -0.5)
                dqs += [dq[gi * CHUNK:(gi + 1) * CHUNK] for gi in range(A_GROUP)]
                dks.append(_mdot(ds, qs, TN) * (A_HEAD_DIM ** -0.5))
                dvs.append(_mdot(pn, dos, TN))
            dq_ref[pl.ds(st, CHUNK), :] = jnp.concatenate(dqs, axis=-1)
            dkp_ref[pl.ds(st, 3 * CHUNK), :] += jnp.concatenate(dks, axis=-1)
            dvp_ref[pl.ds(st, 3 * CHUNK), :] += jnp.concatenate(dvs, axis=-1)
            return dsk

        dsk = lax.fori_loop(0, nc, chunk, jnp.zeros((1, 128), F32))
        dsk_ref[0] = dsk
        dk_ref[...] = dkp_ref[pad:, :]
        dv_ref[...] = dvp_ref[pad:, :]

    kv = jax.ShapeDtypeStruct((t, A_KV_WIDTH), F32)
    return _call(
        body, (proj, proj, proj, dcat, sinks), side, (bsz,), name=name,
        in_specs=[pl.BlockSpec((s_len, A_WIDTH), lambda b: (b, 0)), pl.BlockSpec((s_len, 128), lambda b: (b, 4)),
                  pl.BlockSpec((s_len, 128), lambda b: (b, 5)), pl.BlockSpec((s_len, A_WIDTH), lambda b: (b, 0)),
                  pl.BlockSpec(memory_space=pltpu.SMEM)],
        out_specs=[pl.BlockSpec((s_len, A_WIDTH), lambda b: (b, 0)), pl.BlockSpec((s_len, 128), lambda b: (b, 0)),
                   pl.BlockSpec((s_len, 128), lambda b: (b, 0)), pl.BlockSpec((1, 1, 128), lambda b: (b, 0, 0))],
        out_shape=[jax.ShapeDtypeStruct((t, A_WIDTH), F32), kv, kv, jax.ShapeDtypeStruct((bsz, 1, 128), F32)],
        scratch_shapes=[pltpu.VMEM((s_len + pad, A_KV_WIDTH), F32)] * 4,
        compiler_params=_cp("parallel"),
    )


def _lru_gates(x, cw, cb, wa, ba, wx, bx, lam, row):
    xc = _conv_fwd(x, cw, row) + cb
    r = _sigmoid(_mdot(xc, wa) + ba)
    i = _sigmoid(_mdot(xc, wx) + bx)
    sp = _softplus(-lam)
    log_a = -RG_C * r * sp
    a = jnp.exp(log_a)
    mult = jnp.sqrt(_neg_expm1(2.0 * log_a))
    return xc, r, i, sp, a, mult


def _lru_scan(a, u, row):
    n = a.shape[0]
    d = 1
    while d < n:
        a_sh = jnp.where(row >= d, pltpu.roll(a, d, 0), 1.0)
        u_sh = jnp.where(row >= d, pltpu.roll(u, d, 0), 0.0)
        u = a * u_sh + u
        a = a * a_sh
        d *= 2
    return u


def _lru_scan_rev(a, u, row):
    n = a.shape[0]
    d = 1
    while d < n:
        a_sh = jnp.where(row < n - d, pltpu.roll(a, n - d, 0), 1.0)
        u_sh = jnp.where(row < n - d, pltpu.roll(u, n - d, 0), 0.0)
        u = a * u_sh + u
        a = a * a_sh
        d *= 2
    return u


def _lru_specs(s_len, order):
    def at(f):
        return lambda *g: f(*order(*g))
    return [pl.BlockSpec((s_len, 128), at(lambda b, cb: (b, 6 + cb))), pl.BlockSpec((s_len, 128), at(lambda b, cb: (b, 10 + cb))),
            pl.BlockSpec((4, 128), at(lambda b, cb: (0, cb))), pl.BlockSpec((1, 128), at(lambda b, cb: (0, cb))),
            pl.BlockSpec((1, 128, 128), at(lambda b, cb: (cb, 0, 0))), pl.BlockSpec((1, 128), at(lambda b, cb: (0, cb))),
            pl.BlockSpec((1, 128, 128), at(lambda b, cb: (cb, 0, 0))), pl.BlockSpec((1, 128), at(lambda b, cb: (0, cb))),
            pl.BlockSpec((1, 128), at(lambda b, cb: (0, cb)))]


def lru_fwd(proj, cw, cb, wa, ba, wx, bxb, lam, bsz, *, name):
    t = proj.shape[0]
    s_len = t // bsz

    def body(x_ref, g_ref, cw_ref, cb_ref, wa_ref, ba_ref, wx_ref, bx_ref, lam_ref, y_ref):
        row = lax.broadcasted_iota(jnp.int32, (s_len, 128), 0)
        xc, r, i, sp, a, mult = _lru_gates(x_ref[...], cw_ref[...], cb_ref[...], wa_ref[0], ba_ref[...], wx_ref[0],
                                           bx_ref[...], lam_ref[...], row)
        h = _lru_scan(a, mult * (i * xc), row)
        y_ref[...] = h * _gelu(g_ref[...])[0]

    return _pcall(
        body, name=name, grid=(bsz, 4), in_specs=_lru_specs(s_len, lambda b, cb: (b, cb)),
        out_specs=pl.BlockSpec((s_len, 128), lambda b, cb: (b, cb)),
        out_shape=jax.ShapeDtypeStruct((t, B_WIDTH), F32), compiler_params=_cp("parallel", "parallel"),
    )(proj, proj, cw, cb.reshape(1, -1), wa, ba.reshape(1, -1), wx, bxb.reshape(1, -1), lam.reshape(1, -1))


def lru_bwd(proj, cw, cb, wa, ba, wx, bxb, lam, dcat, bsz, *, name, side=None):
    t = proj.shape[0]
    s_len = t // bsz

    def body(x_ref, g_ref, cw_ref, cb_ref, wa_ref, ba_ref, wx_ref, bx_ref, lam_ref, dy_ref,
             dx_ref, dg_ref, dcw_ref, dcb_ref, dwa_ref, dba_ref, dwx_ref, dbx_ref, dlam_ref):
        b = pl.program_id(1)
        row = lax.broadcasted_iota(jnp.int32, (s_len, 128), 0)
        x = x_ref[...]
        lam = lam_ref[...]
        xc, r, i, sp, a, mult = _lru_gates(x, cw_ref[...], cb_ref[...], wa_ref[0], ba_ref[...], wx_ref[0], bx_ref[...],
                                           lam, row)
        ixc = i * xc
        h = _lru_scan(a, mult * ixc, row)
        gv = g_ref[...]
        gl, th = _gelu(gv)
        dy = dy_ref[...]
        dg_ref[...] = dy * h * _gelu_grad(gv, th)
        gr = _lru_scan_rev(_shift_up(a, 1, row), dy * gl, row)
        da = gr * _shift_dn(h, 1, row)
        dmult = gr * ixc
        di = gr * mult * xc
        dxc = gr * mult * i
        dlog_a = da * a - dmult * (a * a) / mult
        dr = dlog_a * (-RG_C * sp)
        dlam = jnp.sum(dlog_a * r, axis=0, keepdims=True) * (RG_C * _sigmoid(-lam))
        dpa = dr * r * (1.0 - r)
        dpx = di * i * (1.0 - i)
        dxc = dxc + _mdot(dpa, wa_ref[0], NT) + _mdot(dpx, wx_ref[0], NT)
        dx, dcw = _conv_bwd(x, cw_ref[...], dxc, row)
        dx_ref[...] = dx

        @pl.when(b == 0)
        def _():
            for ref in (dcw_ref, dcb_ref, dwa_ref, dba_ref, dwx_ref, dbx_ref, dlam_ref):
                ref[...] = jnp.zeros_like(ref)

        dcw_ref[...] += dcw
        dcb_ref[...] += jnp.sum(dxc, axis=0, keepdims=True)
        dwa_ref[0] += _mdot(xc, dpa, TN)
        dwx_ref[0] += _mdot(xc, dpx, TN)
        dba_ref[...] += jnp.sum(dpa, axis=0, keepdims=True)
        dbx_ref[...] += jnp.sum(dpx, axis=0, keepdims=True)
        dlam_ref[...] += dlam

    order = lambda cb, b: (b, cb)
    act = pl.BlockSpec((s_len, 128), lambda cb, b: (b, cb))
    vec = pl.BlockSpec((1, 128), lambda cb, b: (0, cb))
    mat = pl.BlockSpec((1, 128, 128), lambda cb, b: (cb, 0, 0))
    vshape = jax.ShapeDtypeStruct((1, B_WIDTH), F32)
    mshape = jax.ShapeDtypeStruct((4, 128, 128), F32)
    return _call(
        body, (proj, proj, cw, cb.reshape(1, -1), wa, ba.reshape(1, -1), wx, bxb.reshape(1, -1), lam.reshape(1, -1), dcat),
        side, (4, bsz), name=name,
        in_specs=_lru_specs(s_len, order) + [pl.BlockSpec((s_len, 128), lambda cb, b: (b, 4 + cb))],
        out_specs=[act, act, pl.BlockSpec((4, 128), lambda cb, b: (0, cb)), vec, mat, vec, mat, vec, vec],
        out_shape=[jax.ShapeDtypeStruct((t, B_WIDTH), F32), jax.ShapeDtypeStruct((t, B_WIDTH), F32),
                   jax.ShapeDtypeStruct((4, B_WIDTH), F32), vshape, mshape, vshape, mshape, vshape, vshape],
        compiler_params=_cp("parallel", "arbitrary"),
    )


_BDIMS = {"nn": ((2,), (1,)), "nt": ((2,), (2,)), "tn": ((1,), (1,))}
C_QSCALE = C_HEAD_DIM ** -0.5


def _bmm(a, b, mode, exact=False):
    dims = (_BDIMS[mode], ((0,), (0,)))
    if exact:
        return lax.dot_general(a, b, dims, preferred_element_type=F32, precision=lax.Precision.HIGH)
    return lax.dot_general(a.astype(MM), b.astype(MM), dims, preferred_element_type=F32)


def _col(x, idx, lane):
    return jnp.broadcast_to(jnp.sum(jnp.where(lane == idx, x, 0.0), axis=-1, keepdims=True), x.shape)


def _seg_cumsum(g, row):
    pos = row & (CHUNK - 1)
    d = 1
    while d < CHUNK:
        g = g + jnp.where(pos >= d, pltpu.roll(g, d, 0), 0.0)
        d *= 2
    return g


def _seg_cumsum_rev(g, row):
    pos = row & (CHUNK - 1)
    n = g.shape[0]
    d = 1
    while d < CHUNK:
        g = g + jnp.where(pos < CHUNK - d, pltpu.roll(g, n - d, 0), 0.0)
        d *= 2
    return g


def _gdn_prep(qr, kr, vr, gates, cwq, cwk, cwv, a_log, dtb, h):
    s_len = qr.shape[0]
    nc = s_len // CHUNK
    row = lax.broadcasted_iota(jnp.int32, (s_len, 128), 0)
    lane = lax.broadcasted_iota(jnp.int32, (s_len, 128), 1)
    r = {"row": row, "lane": lane}
    for nm, x, w in (("q", qr, cwq), ("k", kr, cwk), ("v", vr, cwv)):
        c = _conv_fwd(x, w, row)
        sg = _sigmoid(c)
        r["c" + nm], r["s" + nm], r[nm + "c"] = c, sg, c * sg
    r["rq"] = lax.rsqrt(jnp.sum(r["qc"] * r["qc"], axis=-1, keepdims=True) + NORM_EPS)
    r["rk"] = lax.rsqrt(jnp.sum(r["kc"] * r["kc"], axis=-1, keepdims=True) + NORM_EPS)
    r["qn"] = r["qc"] * r["rq"]
    r["kn"] = r["kc"] * r["rk"]
    r["beta"] = _sigmoid(_col(gates, h, lane))
    r["A"] = jnp.exp(a_log)
    r["pre"] = _col(gates, 8 + h, lane) + dtb
    r["sp"] = _softplus(r["pre"])
    gc = _seg_cumsum(-r["A"] * r["sp"], row)
    sh = (nc, CHUNK, 128)
    q3 = (r["qn"] * C_QSCALE).reshape(sh)
    k3 = r["kn"].reshape(sh)
    v3 = r["vc"].reshape(sh)
    beta3 = r["beta"].reshape(sh)
    gc3 = gc.reshape(sh)
    gcl3 = gc3[:, CHUNK - 1:CHUNK, :]
    eg = jnp.exp(gc3)
    ekd = jnp.exp(gcl3 - gc3)
    col64 = gc3[:, :, :CHUNK]
    row64 = jnp.swapaxes(gc3, 1, 2)[:, :CHUNK, :]
    ii = lax.broadcasted_iota(jnp.int32, (nc, CHUNK, CHUNK), 1)
    jj = lax.broadcasted_iota(jnp.int32, (nc, CHUNK, CHUNK), 2)
    tril = ii >= jj
    strict = ii > jj
    dm = jnp.where(tril, jnp.exp(jnp.where(tril, col64 - row64, 0.0)), 0.0)
    kb = k3 * beta3
    lmat = jnp.where(strict, _bmm(kb, k3, "nt") * dm, 0.0)
    attn = _bmm(q3, k3, "nt") * dm
    r.update(q3=q3, k3=k3, v3=v3, beta3=beta3, gc3=gc3, eg=eg, ekd=ekd, gl=jnp.exp(gcl3), dm=dm, kb=kb, lmat=lmat,
             attn=attn, strict=strict, tril=tril, qg=q3 * eg, kdec=k3 * ekd)
    return r


def _neumann_inverse(lmat):
    ii = lax.broadcasted_iota(jnp.int32, lmat.shape, 1)
    jj = lax.broadcasted_iota(jnp.int32, lmat.shape, 2)
    x = -lmat
    tm = jnp.where(ii == jj, 1.0, 0.0) + x
    pw = x
    for _ in range(5):
        pw = _bmm(pw, pw, "nn", exact=True)
        tm = tm + _bmm(tm, pw, "nn", exact=True)
    return tm


def _gdn_specs(s_len):
    act = lambda off: pl.BlockSpec((s_len, 128), lambda b, h: (b, off + h))
    cw = lambda off: pl.BlockSpec((4, 128), lambda b, h: (0, off + h))
    smem = pl.BlockSpec(memory_space=pltpu.SMEM)
    return [act(0), act(8), act(16), act(24), pl.BlockSpec((s_len, 128), lambda b, h: (b, 0)), cw(0), cw(8), cw(16),
            smem, smem, pl.BlockSpec((1, 128), lambda b, h: (0, 0))]


def gdn_fwd(proj, gates, cw, a_log, dtb, ng, bsz, *, name):
    t = proj.shape[0]
    s_len = t // bsz
    nc = s_len // CHUNK

    def body(q_ref, k_ref, v_ref, z_ref, gt_ref, cwq_ref, cwk_ref, cwv_ref, al_ref, dt_ref, ng_ref,
             out_ref, o_ref, vn_ref, tm_ref, st_ref, u_s, w_s, qg_s, kd_s, at_s, gl_s):
        h = pl.program_id(1)
        r = _gdn_prep(q_ref[...], k_ref[...], v_ref[...], gt_ref[...], cwq_ref[...], cwk_ref[...], cwv_ref[...],
                      al_ref[h], dt_ref[h], h)
        tm = _neumann_inverse(r["lmat"])
        tm_ref[0, 0] = tm
        u_s[...] = _bmm(tm, r["v3"] * r["beta3"], "nn", exact=True)
        w_s[...] = _bmm(tm, r["kb"] * r["eg"], "nn", exact=True)
        qg_s[...] = r["qg"]
        kd_s[...] = r["kdec"]
        at_s[...] = r["attn"]
        gl_s[...] = r["gl"]

        def chunk(n, state):
            st = pl.multiple_of(n * CHUNK, CHUNK)
            st_ref[0, 0, n] = state
            v_new = u_s[n] - _mdot(w_s[n], state)
            o_ref[pl.ds(st, CHUNK), :] = _mdot(qg_s[n], state) + _mdot(at_s[n], v_new)
            vn_ref[pl.ds(st, CHUNK), :] = v_new
            return state * gl_s[n] + _mdot(kd_s[n], v_new, TN)

        lax.fori_loop(0, nc, chunk, jnp.zeros((128, 128), F32))
        o = o_ref[...]
        rms = lax.rsqrt(jnp.mean(o * o, axis=-1, keepdims=True) + NORM_EPS)
        z = z_ref[...]
        out_ref[...] = o * rms * ng_ref[...] * (z * _sigmoid(z))

    blk = pl.BlockSpec((s_len, 128), lambda b, h: (b, h))
    full = jax.ShapeDtypeStruct((t, C_WIDTH), F32)
    return _pcall(
        body, name=name, grid=(bsz, C_HEADS), in_specs=_gdn_specs(s_len),
        out_specs=[blk, blk, blk, pl.BlockSpec((1, 1, nc, CHUNK, CHUNK), lambda b, h: (b, h, 0, 0, 0)),
                   pl.BlockSpec((1, 1, nc, 128, 128), lambda b, h: (b, h, 0, 0, 0))],
        out_shape=[full, full, full, jax.ShapeDtypeStruct((bsz, C_HEADS, nc, CHUNK, CHUNK), F32),
                   jax.ShapeDtypeStruct((bsz, C_HEADS, nc, 128, 128), F32)],
        scratch_shapes=[pltpu.VMEM((nc, CHUNK, 128), F32)] * 4 + [pltpu.VMEM((nc, CHUNK, CHUNK), F32),
                                                                   pltpu.VMEM((nc, 1, 128), F32)],
        compiler_params=_cp("parallel", "parallel"),
    )(proj, proj, proj, proj, gates, cw, cw, cw, a_log, dtb, ng.reshape(1, 128))


def gdn_bwd(proj, gates, cw, a_log, dtb, ng, o_pre, vnew, tmat, states, dout, bsz, *, name):
    t = proj.shape[0]
    s_len = t // bsz
    nc = s_len // CHUNK

    def body(q_ref, k_ref, v_ref, z_ref, gt_ref, cwq_ref, cwk_ref, cwv_ref, al_ref, dt_ref, ng_ref,
             o_ref, vn_ref, tm_ref, st_ref, do_ref,
             dq_ref, dk_ref, dv_ref, dz_ref, dgt_ref, dcq_ref, dck_ref, dcv_ref, dsm_ref,
             w_s, qg_s, kd_s, at_s, gl_s, dop_s, du_s, dw_s, dat_s, dqg_s, dkd_s, dgl_s):
        h = pl.program_id(1)
        qr, kr, vr = q_ref[...], k_ref[...], v_ref[...]
        r = _gdn_prep(qr, kr, vr, gt_ref[...], cwq_ref[...], cwk_ref[...], cwv_ref[...], al_ref[h], dt_ref[h], h)
        row, lane = r["row"], r["lane"]
        tm = tm_ref[0, 0]
        q3, k3, v3, beta3, eg, kb, dm = r["q3"], r["k3"], r["v3"], r["beta3"], r["eg"], r["kb"], r["dm"]
        u3 = _bmm(tm, v3 * beta3, "nn", exact=True)
        w3 = _bmm(tm, kb * eg, "nn", exact=True)
        w_s[...] = w3
        qg_s[...] = r["qg"]
        kd_s[...] = r["kdec"]
        at_s[...] = r["attn"]
        gl_s[...] = r["gl"]

        z = z_ref[...]
        sz = _sigmoid(z)
        o = o_ref[...]
        rms = lax.rsqrt(jnp.mean(o * o, axis=-1, keepdims=True) + NORM_EPS)
        on = o * rms
        dout_v = do_ref[...]
        ngv = ng_ref[...]
        dz_ref[...] = dout_v * on * ngv * (sz * (1.0 + z * (1.0 - sz)))
        dos = dout_v * (z * sz)
        dng = jnp.sum(dos * on, axis=0, keepdims=True)
        don = dos * ngv
        dop_s[...] = (rms * (don - on * jnp.mean(don * on, axis=-1, keepdims=True))).reshape(nc, CHUNK, 128)

        def chunk(i, dstate):
            n = nc - 1 - i
            st = pl.multiple_of(n * CHUNK, CHUNK)
            state = st_ref[0, 0, n]
            vn = vn_ref[pl.ds(st, CHUNK), :]
            do_n = dop_s[n]
            dvn = _mdot(at_s[n], do_n, TN) + _mdot(kd_s[n], dstate)
            du_s[n] = dvn
            dat_s[n] = _mdot(do_n, vn, NT)
            dqg_s[n] = _mdot(do_n, state, NT)
            dkd_s[n] = _mdot(vn, dstate, NT)
            dgl_s[n] = jnp.broadcast_to(jnp.sum(jnp.sum(state * dstate, axis=1, keepdims=True), axis=0, keepdims=True), (1, 128))
            dw_s[n] = -_mdot(dvn, state, NT)
            return dstate * gl_s[n] + _mdot(qg_s[n], do_n, TN) - _mdot(w_s[n], dvn, TN)

        lax.fori_loop(0, nc, chunk, jnp.zeros((128, 128), F32))

        du, dw, dqg, dkd = du_s[...], dw_s[...], dqg_s[...], dkd_s[...]
        dat = jnp.where(r["tril"], dat_s[...], 0.0)
        dvb = _bmm(tm, du, "tn", exact=True)
        dkbg = _bmm(tm, dw, "tn", exact=True)
        dl = -jnp.where(r["strict"], _bmm(dvb, u3, "nt") + _bmm(dkbg, w3, "nt"), 0.0)
        dml = dl * dm
        dn = dat * dm
        dkb = _bmm(dml, k3, "nn") + dkbg * eg
        dk3 = _bmm(dml, kb, "tn") + _bmm(dn, q3, "tn") + dkd * r["ekd"] + dkb * beta3
        dq3 = dqg * eg + _bmm(dn, k3, "nn")
        e = dl * r["lmat"] + dat * r["attn"]
        ones = jnp.ones((nc, CHUNK, 128), F32)
        colsum = lax.dot_general(e, ones, (_BDIMS["tn"], ((0,), (0,))), preferred_element_type=F32, precision=HI)
        dgc = jnp.sum(e, axis=-1, keepdims=True) - colsum
        dgc = dgc + eg * (jnp.sum(dqg * q3, axis=-1, keepdims=True) + jnp.sum(dkbg * kb, axis=-1, keepdims=True))
        skd = jnp.sum(dkd * r["kdec"], axis=-1, keepdims=True)
        dgcl = jnp.sum(skd, axis=1, keepdims=True) + dgl_s[...] * r["gl"]
        pos3 = lax.broadcasted_iota(jnp.int32, (nc, CHUNK, 128), 1)
        dgc = dgc - skd + jnp.where(pos3 == CHUNK - 1, dgcl, 0.0)
        dbeta = jnp.sum(dkb * k3, axis=-1, keepdims=True) + jnp.sum(dvb * v3, axis=-1, keepdims=True)
        dv3 = dvb * beta3

        dg = _seg_cumsum_rev(dgc.reshape(s_len, 128), row)
        beta = r["beta"]
        dbl = jnp.broadcast_to(dbeta, (nc, CHUNK, 128)).reshape(s_len, 128) * beta * (1.0 - beta)
        dai = dg * (-r["A"]) * _sigmoid(r["pre"])
        d_dtb = jnp.sum(dai, axis=0, keepdims=True)
        d_alog = jnp.sum(dg * (-r["sp"]), axis=0, keepdims=True) * r["A"]

        @pl.when(h == 0)
        def _():
            dgt_ref[...] = jnp.zeros_like(dgt_ref)
            dsm_ref[...] = jnp.zeros_like(dsm_ref)

        dgt_ref[...] += jnp.where(lane == h, dbl, 0.0) + jnp.where(lane == 8 + h, dai, 0.0)
        r16 = lax.broadcasted_iota(jnp.int32, (16, 128), 0)
        l16 = lax.broadcasted_iota(jnp.int32, (16, 128), 1)
        small = jnp.where((r16 == h) & (l16 == 0), d_alog, 0.0) + jnp.where((r16 == h) & (l16 == 1), d_dtb, 0.0)
        dsm_ref[0] += small + jnp.where(r16 == 8 + h, dng, 0.0)

        dqn = dq3.reshape(s_len, 128) * C_QSCALE
        dkn = dk3.reshape(s_len, 128)
        dqc = r["rq"] * (dqn - r["qn"] * jnp.sum(dqn * r["qn"], axis=-1, keepdims=True))
        dkc = r["rk"] * (dkn - r["kn"] * jnp.sum(dkn * r["kn"], axis=-1, keepdims=True))
        dvc = dv3.reshape(s_len, 128)
        for nm, x, w_ref, dxc, dx_ref, dc_ref in (("q", qr, cwq_ref, dqc, dq_ref, dcq_ref), ("k", kr, cwk_ref, dkc, dk_ref, dck_ref),
                                                 ("v", vr, cwv_ref, dvc, dv_ref, dcv_ref)):
            c, sg = r["c" + nm], r["s" + nm]
            dc = dxc * (sg * (1.0 + c * (1.0 - sg)))
            dx, dwc = _conv_bwd(x, w_ref[...], dc, row)
            dx_ref[...] = dx
            dc_ref[0] = dwc

    blk = pl.BlockSpec((s_len, 128), lambda b, h: (b, h))
    full = jax.ShapeDtypeStruct((t, C_WIDTH), F32)
    cwo = pl.BlockSpec((1, 4, 128), lambda b, h: (b, 0, h))
    cws = jax.ShapeDtypeStruct((bsz, 4, C_WIDTH), F32)
    c128 = pltpu.VMEM((nc, CHUNK, 128), F32)
    outs = _pcall(
        body, name=name, grid=(bsz, C_HEADS),
        in_specs=_gdn_specs(s_len) + [blk, blk, pl.BlockSpec((1, 1, nc, CHUNK, CHUNK), lambda b, h: (b, h, 0, 0, 0)),
                                      pl.BlockSpec((1, 1, nc, 128, 128), lambda b, h: (b, h, 0, 0, 0)), blk],
        out_specs=[blk, blk, blk, blk, pl.BlockSpec((s_len, 128), lambda b, h: (b, 0)), cwo, cwo, cwo,
                   pl.BlockSpec((1, 16, 128), lambda b, h: (b, 0, 0))],
        out_shape=[full, full, full, full, jax.ShapeDtypeStruct((t, 128), F32), cws, cws, cws,
                   jax.ShapeDtypeStruct((bsz, 16, 128), F32)],
        scratch_shapes=[c128, c128, c128, pltpu.VMEM((nc, CHUNK, CHUNK), F32), pltpu.VMEM((nc, 1, 128), F32), c128,
                        c128, c128, pltpu.VMEM((nc, CHUNK, CHUNK), F32), c128, c128, pltpu.VMEM((nc, 1, 128), F32)],
        compiler_params=_cp("parallel", "arbitrary"),
    )(proj, proj, proj, proj, gates, cw, cw, cw, a_log, dtb, ng.reshape(1, 128), o_pre, vnew, tmat, states, dout)
    dq, dk, dv, dz, dgates, dcq, dck, dcv, dsm = outs
    return dq, dk, dv, dz, dgates, jnp.concatenate([dcq, dck, dcv], axis=-1), dsm


def gdc_pre_fwd(proj, cw, bsz, *, name):
    t = proj.shape[0]
    s_len = t // bsz

    def body(x_ref, w_ref, y_ref):
        row = lax.broadcasted_iota(jnp.int32, (s_len, 128), 0)
        c = _conv_fwd(x_ref[...], w_ref[...], row)
        xc = c * _sigmoid(c)
        rn = lax.rsqrt(jnp.sum(xc * xc, axis=-1, keepdims=True) + NORM_EPS)
        y_ref[...] = jnp.where(pl.program_id(1) < 2 * C_HEADS, xc * rn, xc)

    blk = pl.BlockSpec((s_len, 128), lambda b, j: (b, j))
    return _pcall(
        body, name=name, grid=(bsz, 3 * C_HEADS), in_specs=[blk, pl.BlockSpec((4, 128), lambda b, j: (0, j))],
        out_specs=blk, out_shape=jax.ShapeDtypeStruct((t, 3 * C_WIDTH), F32), compiler_params=_cp("parallel", "parallel"),
    )(proj, cw)


def gdc_pre_bwd(proj, cw, dy, dproj, bsz, *, name):
    t = proj.shape[0]
    s_len = t // bsz

    def body(x_ref, w_ref, dy_ref, _, dx_ref, dw_ref):
        row = lax.broadcasted_iota(jnp.int32, (s_len, 128), 0)
        x = x_ref[...]
        c = _conv_fwd(x, w_ref[...], row)
        sg = _sigmoid(c)
        xc = c * sg
        rn = lax.rsqrt(jnp.sum(xc * xc, axis=-1, keepdims=True) + NORM_EPS)
        dyv = dy_ref[...]
        xn = xc * rn
        dxc = jnp.where(pl.program_id(1) < 2 * C_HEADS, rn * (dyv - xn * jnp.sum(dyv * xn, axis=-1, keepdims=True)), dyv)
        dc = dxc * (sg * (1.0 + c * (1.0 - sg)))
        dx, dw = _conv_bwd(x, w_ref[...], dc, row)
        dx_ref[...] = dx.astype(MM)
        dw_ref[0] = dw

    blk = pl.BlockSpec((s_len, 128), lambda b, j: (b, j))
    return _pcall(
        body, name=name, grid=(bsz, 3 * C_HEADS),
        in_specs=[blk, pl.BlockSpec((4, 128), lambda b, j: (0, j)), blk, pl.BlockSpec(memory_space=pl.ANY)],
        out_specs=[blk, pl.BlockSpec((1, 4, 128), lambda b, j: (b, 0, j))],
        out_shape=[jax.ShapeDtypeStruct((t, 4 * C_WIDTH), MM), jax.ShapeDtypeStruct((bsz, 4, 3 * C_WIDTH), F32)],
        input_output_aliases={3: 0}, compiler_params=_cp("parallel", "parallel"),
    )(proj, cw, dy, dproj)


GDC_GROUP = 16


def _gdc_local(qn, kn, vc, gates, a_log, dtb, h):
    rows = qn.shape[0]
    nc = rows // CHUNK
    row = lax.broadcasted_iota(jnp.int32, (rows, 128), 0)
    lane = lax.broadcasted_iota(jnp.int32, (rows, 128), 1)
    r = {"row": row, "lane": lane}
    r["beta"] = _sigmoid(_col(gates, h, lane))
    r["A"] = jnp.exp(a_log)
    r["pre"] = _col(gates, 8 + h, lane) + dtb
    r["sp"] = _softplus(r["pre"])
    gc = _seg_cumsum(-r["A"] * r["sp"], row)
    sh = (nc, CHUNK, 128)
    q3 = (qn * C_QSCALE).reshape(sh)
    k3 = kn.reshape(sh)
    v3 = vc.reshape(sh)
    beta3 = r["beta"].reshape(sh)
    gc3 = gc.reshape(sh)
    gcl3 = gc3[:, CHUNK - 1:CHUNK, :]
    eg = jnp.exp(gc3)
    ekd = jnp.exp(gcl3 - gc3)
    col64 = gc3[:, :, :CHUNK]
    row64 = jnp.swapaxes(gc3, 1, 2)[:, :CHUNK, :]
    ii = lax.broadcasted_iota(jnp.int32, (nc, CHUNK, CHUNK), 1)
    jj = lax.broadcasted_iota(jnp.int32, (nc, CHUNK, CHUNK), 2)
    tril = ii >= jj
    strict = ii > jj
    dm = jnp.where(tril, jnp.exp(jnp.where(tril, col64 - row64, 0.0)), 0.0)
    kb = k3 * beta3
    lmat = jnp.where(strict, _bmm(kb, k3, "nt") * dm, 0.0)
    attn = _bmm(q3, k3, "nt") * dm
    r.update(q3=q3, k3=k3, v3=v3, beta3=beta3, eg=eg, ekd=ekd, gl=jnp.exp(gcl3), dm=dm, kb=kb, lmat=lmat,
             attn=attn, strict=strict, tril=tril, qg=q3 * eg, kdec=k3 * ekd)
    return r


def _gdc_specs(s_len):
    act = lambda off: pl.BlockSpec((s_len, 128), lambda b, h: (b, off + h))
    smem = pl.BlockSpec(memory_space=pltpu.SMEM)
    return [act(0), act(8), act(16), act(24), pl.BlockSpec((s_len, 128), lambda b, h: (b, 0)), smem, smem,
            pl.BlockSpec((1, 128), lambda b, h: (0, 0))]


def gdc_fwd(qkv, proj, gates, a_log, dtb, ng, bsz, *, name, side=None):
    t = proj.shape[0]
    s_len = t // bsz
    nc = s_len // CHUNK
    grp = min(GDC_GROUP, nc)
    gr = grp * CHUNK

    def body(q_ref, k_ref, v_ref, z_ref, gt_ref, al_ref, dt_ref, ng_ref,
             out_ref, o_ref, tm_ref, st_ref, c_s, b_s, qp_s, op_s, gl_s):
        h = pl.program_id(1)

        def local(gi, carry):
            rs = pl.ds(pl.multiple_of(gi * gr, gr), gr)
            cs = pl.ds(gi * grp, grp)
            r = _gdc_local(q_ref[rs, :], k_ref[rs, :], v_ref[rs, :], gt_ref[rs, :], al_ref[h], dt_ref[h], h)
            tm = _neumann_inverse(r["lmat"])
            tm_ref[0, 0, cs] = tm
            u = _bmm(tm, r["v3"] * r["beta3"], "nn", exact=True)
            w = _bmm(tm, r["kb"] * r["eg"], "nn", exact=True)
            c_s[cs] = -_bmm(r["kdec"], w, "tn")
            b_s[cs] = _bmm(r["kdec"], u, "tn")
            qp_s[cs] = r["qg"] - _bmm(r["attn"], w, "nn")
            op_s[cs] = _bmm(r["attn"], u, "nn")
            gl_s[cs] = r["gl"]
            return carry

        lax.fori_loop(0, nc // grp, local, 0)

        def chunk(n, state):
            st = pl.multiple_of(n * CHUNK, CHUNK)
            st_ref[0, 0, n] = state
            o_ref[pl.ds(st, CHUNK), :] = _mdot(qp_s[n], state) + op_s[n]
            return state * gl_s[n] + _mdot(c_s[n], state) + b_s[n]

        lax.fori_loop(0, nc, chunk, jnp.zeros((128, 128), F32))
        o = o_ref[...]
        rms = lax.rsqrt(jnp.mean(o * o, axis=-1, keepdims=True) + NORM_EPS)
        z = z_ref[...]
        out_ref[...] = o * rms * ng_ref[...] * (z * _sigmoid(z))

    blk = pl.BlockSpec((s_len, 128), lambda b, h: (b, h))
    full = jax.ShapeDtypeStruct((t, C_WIDTH), F32)
    return _call(
        body, (qkv, qkv, qkv, proj, gates, a_log, dtb, ng.reshape(1, 128)), side, (bsz, C_HEADS), name=name,
        in_specs=_gdc_specs(s_len),
        out_specs=[blk, blk, pl.BlockSpec((1, 1, nc, CHUNK, CHUNK), lambda b, h: (b, h, 0, 0, 0)),
                   pl.BlockSpec((1, 1, nc, 128, 128), lambda b, h: (b, h, 0, 0, 0))],
        out_shape=[full, full, jax.ShapeDtypeStruct((bsz, C_HEADS, nc, CHUNK, CHUNK), F32),
                   jax.ShapeDtypeStruct((bsz, C_HEADS, nc, 128, 128), F32)],
        scratch_shapes=[pltpu.VMEM((nc, 128, 128), F32)] * 2 + [pltpu.VMEM((nc, CHUNK, 128), F32)] * 2 +
                       [pltpu.VMEM((nc, 1, 128), F32)],
        compiler_params=_cp("parallel", "parallel"),
    )


def gdc_bwd(qkv, proj, gates, a_log, dtb, ng, o_pre, tmat, states, dout, bsz, *, name, side=None):
    t = proj.shape[0]
    s_len = t // bsz
    nc = s_len // CHUNK
    grp = min(GDC_GROUP, nc)
    gr = grp * CHUNK

    def body(q_ref, k_ref, v_ref, z_ref, gt_ref, al_ref, dt_ref, ng_ref, o_ref, tm_ref, st_ref, do_ref,
             dq_ref, dk_ref, dv_ref, dz_ref, dgt_ref, dsm_ref, c_s, e_s, dsp_s, gl_s, dop_s):
        h = pl.program_id(1)
        a_log_h, dtb_h = al_ref[h], dt_ref[h]

        z = z_ref[...]
        sz = _sigmoid(z)
        o = o_ref[...]
        rms = lax.rsqrt(jnp.mean(o * o, axis=-1, keepdims=True) + NORM_EPS)
        on = o * rms
        dout_v = do_ref[...]
        ngv = ng_ref[...]
        dz_ref[...] = (dout_v * on * ngv * (sz * (1.0 + z * (1.0 - sz)))).astype(MM)
        dos = dout_v * (z * sz)
        dng = jnp.sum(dos * on, axis=0, keepdims=True)
        don = dos * ngv
        dop_s[...] = (rms * (don - on * jnp.mean(don * on, axis=-1, keepdims=True))).reshape(nc, CHUNK, 128)

        def local(gi, carry):
            rs = pl.ds(pl.multiple_of(gi * gr, gr), gr)
            cs = pl.ds(gi * grp, grp)
            r = _gdc_local(q_ref[rs, :], k_ref[rs, :], v_ref[rs, :], gt_ref[rs, :], a_log_h, dtb_h, h)
            w = _bmm(tm_ref[0, 0, cs], r["kb"] * r["eg"], "nn", exact=True)
            c_s[cs] = -_bmm(w, r["kdec"], "tn")
            e_s[cs] = _bmm(r["qg"] - _bmm(r["attn"], w, "nn"), dop_s[cs], "tn")
            gl_s[cs] = r["gl"]
            return carry

        lax.fori_loop(0, nc // grp, local, 0)

        def chunk(i, dstate):
            n = nc - 1 - i
            dsp_s[n] = dstate
            return dstate * gl_s[n] + _mdot(c_s[n], dstate) + e_s[n]

        lax.fori_loop(0, nc, chunk, jnp.zeros((128, 128), F32))

        @pl.when(h == 0)
        def _():
            dgt_ref[...] = jnp.zeros_like(dgt_ref)
            dsm_ref[...] = jnp.zeros_like(dsm_ref)

        def local_bwd(gi, carry):
            d_alog, d_dtb = carry
            rs = pl.ds(pl.multiple_of(gi * gr, gr), gr)
            cs = pl.ds(gi * grp, grp)
            r = _gdc_local(q_ref[rs, :], k_ref[rs, :], v_ref[rs, :], gt_ref[rs, :], a_log_h, dtb_h, h)
            row, lane = r["row"], r["lane"]
            q3, k3, v3, beta3, eg, kb, dm = r["q3"], r["k3"], r["v3"], r["beta3"], r["eg"], r["kb"], r["dm"]
            tm = tm_ref[0, 0, cs]
            u3 = _bmm(tm, v3 * beta3, "nn", exact=True)
            w3 = _bmm(tm, kb * eg, "nn", exact=True)
            state, dsp, do3 = st_ref[0, 0, cs], dsp_s[cs], dop_s[cs]
            vn = u3 - _bmm(w3, state, "nn")
            du = _bmm(r["attn"], do3, "tn") + _bmm(r["kdec"], dsp, "nn")
            dat = jnp.where(r["tril"], _bmm(do3, vn, "nt"), 0.0)
            dqg = _bmm(do3, state, "nt")
            dkd = _bmm(vn, dsp, "nt")
            dgl = jnp.sum(jnp.sum(state * dsp, axis=2, keepdims=True), axis=1, keepdims=True)
            dw = -_bmm(du, state, "nt")
            dvb = _bmm(tm, du, "tn", exact=True)
            dkbg = _bmm(tm, dw, "tn", exact=True)
            dl = -jnp.where(r["strict"], _bmm(dvb, u3, "nt") + _bmm(dkbg, w3, "nt"), 0.0)
            dml = dl * dm
            dn = dat * dm
            dkb = _bmm(dml, k3, "nn") + dkbg * eg
            dk3 = _bmm(dml, kb, "tn") + _bmm(dn, q3, "tn") + dkd * r["ekd"] + dkb * beta3
            dq3 = dqg * eg + _bmm(dn, k3, "nn")
            e = dl * r["lmat"] + dat * r["attn"]
            ones = jnp.ones((grp, CHUNK, 128), F32)
            colsum = lax.dot_general(e, ones, (_BDIMS["tn"], ((0,), (0,))), preferred_element_type=F32, precision=HI)
            dgc = jnp.sum(e, axis=-1, keepdims=True) - colsum
            dgc = dgc + eg * (jnp.sum(dqg * q3, axis=-1, keepdims=True) + jnp.sum(dkbg * kb, axis=-1, keepdims=True))
            skd = jnp.sum(dkd * r["kdec"], axis=-1, keepdims=True)
            dgcl = jnp.sum(skd, axis=1, keepdims=True) + dgl * r["gl"]
            pos3 = lax.broadcasted_iota(jnp.int32, (grp, CHUNK, 128), 1)
            dgc = dgc - skd + jnp.where(pos3 == CHUNK - 1, dgcl, 0.0)
            dbeta = jnp.sum(dkb * k3, axis=-1, keepdims=True) + jnp.sum(dvb * v3, axis=-1, keepdims=True)
            dg = _seg_cumsum_rev(dgc.reshape(gr, 128), row)
            beta = r["beta"]
            dbl = jnp.broadcast_to(dbeta, (grp, CHUNK, 128)).reshape(gr, 128) * beta * (1.0 - beta)
            dai = dg * (-r["A"]) * _sigmoid(r["pre"])
            dgt_ref[rs, :] += jnp.where(lane == h, dbl, 0.0) + jnp.where(lane == 8 + h, dai, 0.0)
            dq_ref[rs, :] = dq3.reshape(gr, 128) * C_QSCALE
            dk_ref[rs, :] = dk3.reshape(gr, 128)
            dv_ref[rs, :] = (dvb * beta3).reshape(gr, 128)
            return (d_alog + jnp.sum(dg * (-r["sp"]), axis=0, keepdims=True) * r["A"],
                    d_dtb + jnp.sum(dai, axis=0, keepdims=True))

        zero = jnp.zeros((1, 128), F32)
        d_alog, d_dtb = lax.fori_loop(0, nc // grp, local_bwd, (zero, zero))
        r16 = lax.broadcasted_iota(jnp.int32, (16, 128), 0)
        l16 = lax.broadcasted_iota(jnp.int32, (16, 128), 1)
        small = jnp.where((r16 == h) & (l16 == 0), d_alog, 0.0) + jnp.where((r16 == h) & (l16 == 1), d_dtb, 0.0)
        dsm_ref[0] += small + jnp.where(r16 == 8 + h, dng, 0.0)

    blk = pl.BlockSpec((s_len, 128), lambda b, h: (b, h))
    blk3 = lambda off: pl.BlockSpec((s_len, 128), lambda b, h: (b, off + h))
    full = jax.ShapeDtypeStruct((t, C_WIDTH), F32)
    c128 = pltpu.VMEM((nc, CHUNK, 128), F32)
    sq = pltpu.VMEM((nc, 128, 128), F32)
    res = _call(
        body, (qkv, qkv, qkv, proj, gates, a_log, dtb, ng.reshape(1, 128), o_pre, tmat, states, dout), side,
        (bsz, C_HEADS), name=name,
        in_specs=_gdc_specs(s_len) + [blk, pl.BlockSpec((1, 1, nc, CHUNK, CHUNK), lambda b, h: (b, h, 0, 0, 0)),
                                      pl.BlockSpec((1, 1, nc, 128, 128), lambda b, h: (b, h, 0, 0, 0)), blk],
        out_specs=[blk, blk, blk, pl.BlockSpec((s_len, 128), lambda b, h: (b, 3 * C_HEADS + h)),
                   pl.BlockSpec((s_len, 128), lambda b, h: (b, 0)), pl.BlockSpec((1, 16, 128), lambda b, h: (b, 0, 0))],
        out_shape=[full, full, full, jax.ShapeDtypeStruct((t, 4 * C_WIDTH), MM), jax.ShapeDtypeStruct((t, 128), F32),
                   jax.ShapeDtypeStruct((bsz, 16, 128), F32)],
        scratch_shapes=[sq, sq, sq, pltpu.VMEM((nc, 1, 128), F32), c128],
        compiler_params=_cp("parallel", "arbitrary"),
    )
    (dq, dk, dv, dz, dgates, dsm), extra = res if side is not None else (res, None)
    out = (jnp.concatenate([dq, dk, dv], axis=-1), dz, dgates, dsm)
    return out if side is None else (out, extra)


MESH_ID = pl.DeviceIdType.MESH
_FLIPS = [(0, 0, 1), (1, 0, 0), (0, 1, 0), (1, 1, 0), (1, 0, 1), (0, 1, 1), (1, 1, 1)]


def _me():
    return lax.axis_index("x"), lax.axis_index("y"), lax.axis_index("c")


def _flip(coord, d):
    return 1 - coord if d else coord


def all_gather(shard, *, name):
    def body(x_ref, o_ref, send_sems, recv_sems, local_sem):
        x, y, c = _me()
        mine = 4 * x + 2 * y + c
        own = pltpu.make_async_copy(x_ref, o_ref.at[mine], local_sem)
        own.start()
        copies = []
        for k, (dx, dy, dc) in enumerate(_FLIPS):
            cp = pltpu.make_async_remote_copy(
                src_ref=x_ref, dst_ref=o_ref.at[mine], send_sem=send_sems.at[k], recv_sem=recv_sems.at[k],
                device_id=(_flip(x, dx), _flip(y, dy), _flip(c, dc)), device_id_type=MESH_ID)
            cp.start()
            copies.append(cp)
        for cp in copies:
            cp.wait()
        own.wait()

    hbm = pl.BlockSpec(memory_space=pl.ANY)
    return _pcall(
        body, name=name, in_specs=[hbm], out_specs=hbm,
        out_shape=jax.ShapeDtypeStruct((N_DEV,) + shard.shape, shard.dtype),
        scratch_shapes=[pltpu.SemaphoreType.DMA((7,)), pltpu.SemaphoreType.DMA((7,)), pltpu.SemaphoreType.DMA(())],
    )(shard)


def all_to_all(parts, *, name):
    def body(x_ref, o_ref, send_sems, recv_sems, local_sem):
        x, y, c = _me()
        mine = 4 * x + 2 * y + c
        own = pltpu.make_async_copy(x_ref.at[mine], o_ref.at[mine], local_sem)
        own.start()
        copies = []
        for k, (dx, dy, dc) in enumerate(_FLIPS):
            px, py, pc = _flip(x, dx), _flip(y, dy), _flip(c, dc)
            cp = pltpu.make_async_remote_copy(
                src_ref=x_ref.at[4 * px + 2 * py + pc], dst_ref=o_ref.at[mine], send_sem=send_sems.at[k],
                recv_sem=recv_sems.at[k], device_id=(px, py, pc), device_id_type=MESH_ID)
            cp.start()
            copies.append(cp)
        for cp in copies:
            cp.wait()
        own.wait()

    hbm = pl.BlockSpec(memory_space=pl.ANY)
    return _pcall(
        body, name=name, in_specs=[hbm], out_specs=hbm, out_shape=jax.ShapeDtypeStruct(parts.shape, parts.dtype),
        scratch_shapes=[pltpu.SemaphoreType.DMA((7,)), pltpu.SemaphoreType.DMA((7,)), pltpu.SemaphoreType.DMA(())],
    )(parts)


def adamw_sum(parts, w, m, v, *, name, tr=256):
    r, cdim = w.shape
    tr = _tile8(r, tr)

    def body(p_ref, w_ref, m_ref, v_ref, g_ref, d_ref, mo_ref, vo_ref):
        g = p_ref[0].astype(F32)
        for j in range(1, N_DEV):
            g = g + p_ref[j].astype(F32)
        g_ref[...] = g
        mn = ADAM_B1 * m_ref[...] + (1.0 - ADAM_B1) * g
        vn = ADAM_B2 * v_ref[...] + (1.0 - ADAM_B2) * (g * g)
        mo_ref[...] = mn
        vo_ref[...] = vn
        m_hat = mn / (1.0 - ADAM_B1 ** ADAM_STEP)
        v_hat = vn / (1.0 - ADAM_B2 ** ADAM_STEP)
        d_ref[...] = -ADAM_LR * (m_hat / (jnp.sqrt(v_hat) + ADAM_EPS) + ADAM_WD * w_ref[...])

    blk = pl.BlockSpec((tr, cdim), lambda i: (i, 0))
    shp = jax.ShapeDtypeStruct((r, cdim), F32)
    return _pcall(
        body, name=name, grid=(r // tr,), in_specs=[pl.BlockSpec((N_DEV, tr, cdim), lambda i: (0, i, 0)), blk, blk, blk],
        out_specs=[blk, blk, blk, blk], out_shape=[shp, shp, shp, shp], compiler_params=_cp("parallel"),
    )(parts, w, m, v)


def _tile8(n, pref):
    for c in range(min(pref, n) - min(pref, n) % 16, 0, -16):
        if n % c == 0:
            return c
    return n


BIG = [("ffn1_wg", 2), ("ffn1_wu", 2), ("ffn1_wd", 1), ("ffn2_wg", 2), ("ffn2_wu", 2), ("ffn2_wd", 1), ("ple_wg", 1),
       ("ple_wp", 2), ("ab_w_in", 2), ("ab_w_out", 1), ("c_w_in", 2), ("c_w_out", 1)]
SMALL = [("ln_g", 2), ("ln_b", 2), ("b_conv_w", 2), ("c_conv_w", 2)]
REPL = ["ple_bg", "a_sinks", "b_conv_b", "b_wa", "b_ba", "b_wx", "b_bx", "b_lam", "c_a_log", "c_dt_bias", "c_norm_g"]
WEIGHTS = ["ffn1_wg", "ffn1_wu", "ffn1_wd", "ffn2_wg", "ffn2_wu", "ffn2_wd", "ln_g", "ln_b", "ple_wg", "ple_bg", "ple_wp",
           "ab_w_in", "a_sinks", "b_conv_w", "b_conv_b", "b_wa", "b_ba", "b_wx", "b_bx", "b_lam", "ab_w_out", "c_w_in",
           "c_conv_w", "c_a_log", "c_dt_bias", "c_norm_g", "c_w_out"]
PACK_COLS = 1024
PACK_ALIGN = 16 * PACK_COLS


def _as_bf16_bits(a):
    return lax.bitcast_convert_type(a, jnp.bfloat16).reshape(a.shape[:-1] + (2 * a.shape[-1],))


def _from_bf16_bits(a):
    return lax.bitcast_convert_type(a.reshape(a.shape[:-1] + (a.shape[-1] // 2, 2)), F32)


def _pad_rows(flat, align=PACK_ALIGN):
    n = flat.shape[-1]
    total = -(-n // align) * align
    flat = jnp.pad(flat, [(0, 0)] * (flat.ndim - 1) + [(0, total - n)])
    return flat.reshape(flat.shape[:-1] + (total // PACK_COLS, PACK_COLS))


def _join(blocks, axis):
    moved = jnp.moveaxis(blocks, 0, axis)
    shp = list(moved.shape)
    return moved.reshape(shp[:axis] + [shp[axis] * shp[axis + 1]] + shp[axis + 2:])


def _split(full, axis):
    shp = list(full.shape)
    return jnp.moveaxis(full.reshape(shp[:axis] + [N_DEV, shp[axis] // N_DEV] + shp[axis + 1:]), axis, 0)


def _dense_blocks(w):
    z = jnp.zeros((4, 2, 64, 2, 64), w.dtype)
    w4 = w.reshape(4, 2, 64, 64)
    z = z.at[:, 0, :, 0, :].set(w4[:, 0]).at[:, 1, :, 1, :].set(w4[:, 1])
    return z.reshape(4, 128, 128)


def _diag_blocks(d):
    d5 = d.reshape(4, 2, 64, 2, 64)
    return jnp.stack([d5[:, 0, :, 0, :], d5[:, 1, :, 1, :]], axis=1).reshape(8, 64, 64)


def kernel(x, p, ffn1_wg, ffn1_wu, ffn1_wd, ffn2_wg, ffn2_wu, ffn2_wd, ln_g, ln_b, ple_wg, ple_bg, ple_wp, ab_w_in, a_sinks, b_conv_w, b_conv_b, b_wa, b_ba, b_wx, b_bx, b_lam, ab_w_out, c_w_in, c_conv_w, c_a_log, c_dt_bias, c_norm_g, c_w_out, loss_target, m_ffn1_wg, m_ffn1_wu, m_ffn1_wd, m_ffn2_wg, m_ffn2_wu, m_ffn2_wd, m_ln_g, m_ln_b, m_ple_wg, m_ple_bg, m_ple_wp, m_ab_w_in, m_a_sinks, m_b_conv_w, m_b_conv_b, m_b_wa, m_b_ba, m_b_wx, m_b_bx, m_b_lam, m_ab_w_out, m_c_w_in, m_c_conv_w, m_c_a_log, m_c_dt_bias, m_c_norm_g, m_c_w_out, v_ffn1_wg, v_ffn1_wu, v_ffn1_wd, v_ffn2_wg, v_ffn2_wu, v_ffn2_wd, v_ln_g, v_ln_b, v_ple_wg, v_ple_bg, v_ple_wp, v_ab_w_in, v_a_sinks, v_b_conv_w, v_b_conv_b, v_b_wa, v_b_ba, v_b_wx, v_b_bx, v_b_lam, v_ab_w_out, v_c_w_in, v_c_conv_w, v_c_a_log, v_c_dt_bias, v_c_norm_g, v_c_w_out):
    a = dict(locals())
    return _step3(a)


def _step3(a):
    x, p = a["x"], a["p"]
    bsz, s_len, d = x.shape
    t = bsz * s_len
    x2 = x.reshape(t, d)
    tgt = a["loss_target"].reshape(t, d)
    p2 = p.reshape(DEPTH, t, D_PLE)
    shapes = {n: a[n].shape for n in WEIGHTS}
    n_small = sum(int(np.prod(shapes[n])) for n in SMALL_NAMES)
    small_all = SMALL_NAMES + REPL
    f_ff = shapes["ffn1_wg"][2]
    c_cols = shapes["c_w_in"][2]
    wide = dict(tm=1024, tn=1408, tk=1024)
    tall = dict(tm=1408, tn=1024, tk=1024)

    def cast(z):
        return z.astype(MM)

    def ffn_shards(which, l):
        return [cast(a[which + "_wg"][l]), cast(a[which + "_wu"][l]), cast(a[which + "_wd"][l])]

    def ffn_weights(gat, tag):
        return (join_cols(gat[0], name=f"join_{tag}_wg"), join_cols(gat[1], name=f"join_{tag}_wu"),
                gat[2].reshape(N_DEV * gat[2].shape[1], D_MODEL))

    def rows_full(gat):
        return gat.reshape(N_DEV * gat.shape[1], D_MODEL)

    small_send = _flat_pad([a[n] for n in SMALL_NAMES], F32, 32 * LANES).reshape(32, LANES)
    g0 = gather_multi(ffn_shards("ffn1", 0) + [small_send, cast(a["ab_w_in"][0]), cast(a["ab_w_out"][0])],
                      name="gather_first")
    ws = _take(g0[3].reshape(N_DEV, -1), SMALL_NAMES, shapes)
    small = {n: _join(ws[n], 2) for n in SMALL_NAMES}
    ln_g, ln_b = small["ln_g"], small["ln_b"]
    wa_d, wx_d = _dense_blocks(a["b_wa"][0]), _dense_blocks(a["b_wx"][0])
    lru_w = (small["b_conv_w"][0], a["b_conv_b"][0], wa_d, a["b_ba"][0], wx_d, a["b_bx"][0], a["b_lam"][0])
    gdc_w = (a["c_a_log"][0], a["c_dt_bias"][0], a["c_norm_g"][0])
    wf = {("ffn1", 0): ffn_weights(g0[:3], "ffn1_0")}

    s0 = {"x0": x2}
    side = ("gather", ffn_shards("ffn2", 0))
    (s0["y1"], s0["z1"], s0["hg1"], s0["hu1"]), got = ffn_fwd(x2, *wf["ffn1", 0], ln_g[0, 0], ln_b[0, 0],
                                                             name="ffn1_fwd_0", tm=FFN_TM, tf=FFN_TF, side=side)
    wf["ffn2", 0] = ffn_weights(got, "ffn2_0")
    ab_w_in, ab_w_out = join_cols(g0[4], name="join_ab_in"), rows_full(g0[5])
    s0["proj"] = matmul(s0["y1"], ab_w_in, mode="nn", name="ab_in_fwd", tn=896, tk=1024)
    ya, got_ple = attn_fwd(s0["proj"], a["a_sinks"][0], bsz, name="attn_fwd",
                           side=("gather", [cast(a["ple_wg"][0]), cast(a["ple_wp"][0])]))
    yb = lru_fwd(s0["proj"], *lru_w, bsz, name="lru_fwd")
    s0["mix"] = jnp.concatenate([ya, yb], axis=1)
    s0["y2"], s0["z2"] = mm_ln_fwd(s0["mix"], ab_w_out, s0["y1"], ln_g[0, 1], ln_b[0, 1], name="mix_out_fwd_0")
    side = ("gather", ffn_shards("ffn1", 1))
    (s0["y3"], s0["z3"], s0["hg2"], s0["hu2"]), got = ffn_fwd(s0["y2"], *wf["ffn2", 0], ln_g[0, 2], ln_b[0, 2],
                                                             name="ffn2_fwd_0", tm=FFN_TM, tf=FFN_TF, side=side)
    wf["ffn1", 1] = ffn_weights(got, "ffn1_1")
    ple_wg = [rows_full(got_ple[0]), None]
    ple_wp = [_join(got_ple[1], 1), None]
    h1 = ple_fwd(s0["y3"], p2[0], ple_wg[0], a["ple_bg"][0], ple_wp[0], name="ple_fwd_0")

    s1 = {"x0": h1}
    side = ("gather", [cast(a["c_w_in"][0]), cast(a["c_w_out"][0])])
    (s1["y1"], s1["z1"], s1["hg1"], s1["hu1"]), got = ffn_fwd(h1, *wf["ffn1", 1], ln_g[1, 0], ln_b[1, 0],
                                                             name="ffn1_fwd_1", tm=FFN_TM, tf=FFN_TF, side=side)
    c_in_main, c_in_gate = join_cols(got[0], name="join_c_in", outs=[(0, 4 * C_WIDTH, 4 * C_WIDTH),
                                                                      (4 * C_WIDTH, 4 * C_WIDTH + 2 * C_HEADS, LANES)])
    c_w_out = rows_full(got[1])
    s1["proj"] = matmul(s1["y1"], c_in_main, mode="nn", name="c_in_fwd", tm=1024, tn=2048, tk=1024)
    s1["gates"] = matmul(s1["y1"], c_in_gate, mode="nn", name="c_gate_fwd", tk=1024)
    s1["qkv"] = gdc_pre_fwd(s1["proj"], small["c_conv_w"][0], bsz, name="gdc_pre_fwd")
    side = ("gather", ffn_shards("ffn2", 1) + [cast(a["ple_wg"][1]), cast(a["ple_wp"][1])])
    (s1["mix"], s1["o_pre"], s1["tmat"], s1["states"]), got = gdc_fwd(
        s1["qkv"], s1["proj"], s1["gates"], *gdc_w, bsz, name="gdc_fwd", side=side)
    wf["ffn2", 1] = ffn_weights(got[:3], "ffn2_1")
    ple_wg[1], ple_wp[1] = rows_full(got[3]), _join(got[4], 1)
    s1["y2"], s1["z2"] = mm_ln_fwd(s1["mix"], c_w_out, s1["y1"], ln_g[1, 1], ln_b[1, 1], name="mix_out_fwd_1")
    s1["y3"], s1["z3"], s1["hg2"], s1["hu2"] = ffn_fwd(s1["y2"], *wf["ffn2", 1], ln_g[1, 2], ln_b[1, 2], name="ffn2_fwd_1",
                                                           tm=FFN_TM, tf=FFN_TF)
    h2 = ple_fwd(s1["y3"], p2[1], ple_wg[1], a["ple_bg"][1], ple_wp[1], name="ple_fwd_1")
    loss_part, dh = loss_fwd_bwd(h2, tgt, name="loss")

    def ffn_parts(xin, act, dhg, dhu, dz, tag):
        dwg = matmul(xin, dhg, mode="tn", name=f"{tag}_wg_grad", **wide)
        dwu = matmul(xin, dhu, mode="tn", name=f"{tag}_wu_grad", **wide)
        dwd = matmul(act, dz, mode="tn", scale=0.5, out_dtype=MM, name=f"{tag}_wd_grad", **tall)
        return [split_cols([(dwg, N_DEV * f_ff)], f_ff, name=f"split_{tag}_wg"),
                split_cols([(dwu, N_DEV * f_ff)], f_ff, name=f"split_{tag}_wu"), dwd.reshape(N_DEV, f_ff, D_MODEL)]

    def ple_parts(i, s, dt, de):
        gwg = matmul(s["y3"], dt, mode="tn", out_dtype=MM, name=f"ple_wg_grad_{i}", tm=1024, tn=1024)
        gwp = matmul(p2[i], de, mode="tn", out_dtype=MM, name=f"ple_wp_grad_{i}", tn=1024)
        return [gwg.reshape(N_DEV, D_MODEL // N_DEV, D_MODEL), _split(gwp, 1)]

    gln = {"ln_g": [None, None], "ln_b": [None, None]}
    gple_bg = [None, None]

    dy3, dt, de, dbg = ple_bwd(dh, s1["y3"], p2[1], ple_wg[1], a["ple_bg"][1], ple_wp[1], name="ple_bwd_1")
    gple_bg[1] = dbg[0]
    parts_ple1 = ple_parts(1, s1, dt, de)
    dz3, dg2, db2 = ln_bwd(dy3, s1["z3"], ln_g[1, 2], name="ln2_bwd_1")
    dy2, act, dhg, dhu = ffn_bwd(dz3, s1["hg2"], s1["hu2"], *wf["ffn2", 1], name="ffn2_bwd_1", tm=FFN_TM, tf=FFN_TF)
    parts_ffn2_1 = ffn_parts(s1["y2"], act, dhg, dhu, dz3, "ffn2_1")
    dz2, dg1, db1 = ln_bwd(dy2, s1["z2"], ln_g[1, 1], name="ln1_bwd_1")
    dmix = matmul(dz2, c_w_out, mode="nt", name="c_out_bwd", tn=1024, tk=1024)
    parts_c_out = matmul(s1["mix"], dz2, mode="tn", out_dtype=MM, name="c_out_grad", tm=1024, tn=1024).reshape(
        N_DEV, D_MODEL // N_DEV, D_MODEL)
    (dqkv, dzc, dgates, dsm), recv1 = gdc_bwd(s1["qkv"], s1["proj"], s1["gates"], *gdc_w, s1["o_pre"],
                                              s1["tmat"], s1["states"], dmix, bsz, name="gdc_bwd",
                                              side=("exchange", parts_ffn2_1 + parts_ple1))
    dproj, dccw = gdc_pre_bwd(s1["proj"], small["c_conv_w"][0], dqkv, dzc, bsz, name="gdc_pre_bwd")
    dgb = dgates.astype(MM)
    dy1 = matmul(dproj, c_in_main, mode="nt", add=dz2, add_scale=DN_ALPHA, name="c_in_bwd", tn=1024, tk=4096)
    dy1 = matmul(dgb, c_in_gate, mode="nt", add=dy1, name="c_gate_bwd", tn=1024)
    g_c_main = matmul(s1["y1"], dproj, mode="tn", name="c_in_grad", tm=1024, tn=1024, tk=1024)
    g_c_gate = matmul(s1["y1"], dgb, mode="tn", name="c_gate_grad", tm=1024)
    parts_c_in = split_cols([(g_c_main, 4 * C_WIDTH), (g_c_gate, 2 * C_HEADS)], c_cols, name="split_c_in")
    dz1, dg0, db0 = ln_bwd(dy1, s1["z1"], ln_g[1, 0], name="ln0_bwd_1")
    (dh, act, dhg, dhu), recv_c = ffn_bwd(dz1, s1["hg1"], s1["hu1"], *wf["ffn1", 1], name="ffn1_bwd_1", tm=FFN_TM,
                                          tf=FFN_TF, side=("exchange", [parts_c_in, parts_c_out]))
    parts_ffn1_1 = ffn_parts(s1["x0"], act, dhg, dhu, dz1, "ffn1_1")
    gln["ln_g"][1] = jnp.concatenate([dg0, dg1, dg2], axis=0)
    gln["ln_b"][1] = jnp.concatenate([db0, db1, db2], axis=0)

    dy3, dt, de, dbg = ple_bwd(dh, s0["y3"], p2[0], ple_wg[0], a["ple_bg"][0], ple_wp[0], name="ple_bwd_0")
    gple_bg[0] = dbg[0]
    parts_ple0 = ple_parts(0, s0, dt, de)
    dz3, dg2, db2 = ln_bwd(dy3, s0["z3"], ln_g[0, 2], name="ln2_bwd_0")
    (dy2, act, dhg, dhu), recv2 = ffn_bwd(dz3, s0["hg2"], s0["hu2"], *wf["ffn2", 0], name="ffn2_bwd_0", tm=FFN_TM, tf=FFN_TF,
                                          side=("exchange", parts_ffn1_1))
    parts_ffn2_0 = ffn_parts(s0["y2"], act, dhg, dhu, dz3, "ffn2_0")
    dz2, dg1, db1 = ln_bwd(dy2, s0["z2"], ln_g[0, 1], name="ln1_bwd_0")
    dmix = matmul(dz2, ab_w_out, mode="nt", name="ab_out_bwd", tn=1024, tk=1024)
    parts_ab_out = matmul(s0["mix"], dz2, mode="tn", out_dtype=MM, name="ab_out_grad", tm=1024, tn=1024).reshape(
        N_DEV, D_MODEL // N_DEV, D_MODEL)
    (dq, dk, dv, dsk), recv3a = attn_bwd(s0["proj"], a["a_sinks"][0], dmix, bsz, name="attn_bwd",
                                         side=("exchange", parts_ffn2_0[:2]))
    (dbx, dbgate, dcw, dcb, dwa, dba, dwx, dbxb, dlam), recv3b = lru_bwd(
        s0["proj"], *lru_w, dmix, bsz, name="lru_bwd", side=("exchange", [parts_ffn2_0[2]] + parts_ple0 + [parts_ab_out]))
    dproj = jnp.concatenate([dq, dk, dv, dbx, dbgate], axis=1).astype(MM)
    dy1 = matmul(dproj, ab_w_in, mode="nt", add=dz2, add_scale=DN_ALPHA, name="ab_in_bwd", tn=1024, tk=1792)
    g_ab_in = matmul(s0["y1"], dproj, mode="tn", name="ab_in_grad", tm=1024, tn=896)
    parts_ab_in = split_cols([(g_ab_in, AB_PROJ)], AB_PROJ // N_DEV, name="split_ab_in")
    dz1, dg0, db0 = ln_bwd(dy1, s0["z1"], ln_g[0, 0], name="ln0_bwd_0")
    gln["ln_g"][0] = jnp.concatenate([dg0, dg1, dg2], axis=0)
    gln["ln_b"][0] = jnp.concatenate([db0, db1, db2], axis=0)

    dsm_sum = jnp.sum(dsm, axis=0)
    full = dict(ln_g=jnp.stack(gln["ln_g"]), ln_b=jnp.stack(gln["ln_b"]), b_conv_w=dcw[None],
                c_conv_w=jnp.sum(dccw, axis=0)[None], ple_bg=jnp.stack(gple_bg),
                a_sinks=jnp.sum(dsk, axis=0)[:, :A_HEADS], b_conv_b=dcb, b_wa=_diag_blocks(dwa)[None], b_ba=dba,
                b_wx=_diag_blocks(dwx)[None], b_bx=dbxb, b_lam=dlam, c_a_log=dsm_sum[None, :C_HEADS, 0],
                c_dt_bias=dsm_sum[None, :C_HEADS, 1], c_norm_g=jnp.sum(dsm_sum[C_HEADS:], axis=0)[None])
    small_rows = SMALL_F32 // LANES
    repl_flat = _flat_pad([full[n] for n in REPL], F32, SMALL_F32 - n_small)
    small8 = jnp.concatenate([_flat8_pad([_split(full[n], 2) for n in SMALL_NAMES], F32, n_small),
                              jnp.broadcast_to(repl_flat, (N_DEV,) + repl_flat.shape)], axis=1)
    (dh, act, dhg, dhu), recv3c = ffn_bwd(dz1, s0["hg1"], s0["hu1"], *wf["ffn1", 0], name="ffn1_bwd_0", tm=FFN_TM,
                                          tf=FFN_TF, side=("exchange", [parts_ab_in, small8.reshape(N_DEV, small_rows, LANES)]))
    grad_x = dh.reshape(bsz, s_len, d)

    dwg = matmul(s0["x0"], dhg, mode="tn", name="ffn1_0_wg_grad", **wide)
    parts_wg = split_cols([(dwg, N_DEV * f_ff)], f_ff, name="split_ffn1_0_wg")
    dwu, recv4a = matmul(s0["x0"], dhu, mode="tn", name="ffn1_0_wu_grad", side=("exchange", [parts_wg]), **wide)
    parts_wu = split_cols([(dwu, N_DEV * f_ff)], f_ff, name="split_ffn1_0_wu")
    dwd, recv4b = matmul(act, dz1, mode="tn", scale=0.5, out_dtype=MM, name="ffn1_0_wd_grad",
                         side=("exchange", [parts_wu]), **tall)
    recv4c = exchange_multi([dwd.reshape(N_DEV, f_ff, D_MODEL)], name="exchange_last")

    def upd(parts, n, l, shape2d, **kw):
        wmv = [a[pre + n][l].reshape(shape2d) for pre in ("", "m_", "v_")]
        return adamw_rows(parts, 0, *wmv, name=f"adamw_{n}_{l}", **kw)

    def upd_ffn(parts, which, l):
        return {(which + "_wg", l): upd(parts[0], which + "_wg", l, (D_MODEL, f_ff)),
                (which + "_wu", l): upd(parts[1], which + "_wu", l, (D_MODEL, f_ff)),
                (which + "_wd", l): upd(parts[2], which + "_wd", l, (f_ff, D_MODEL), tr=176)}

    rows8 = D_MODEL // N_DEV
    res = {}
    res.update(upd_ffn(recv1[:3], "ffn2", 1))
    res["ple_wg", 1] = upd(recv1[3], "ple_wg", 1, (rows8, D_MODEL), tr=128)
    res["ple_wp", 1] = upd(recv1[4], "ple_wp", 1, (D_PLE, LANES))
    res.update(upd_ffn(recv2, "ffn1", 1))
    res["c_w_in", 0] = upd(recv_c[0], "c_w_in", 0, (D_MODEL, c_cols))
    res["c_w_out", 0] = upd(recv_c[1], "c_w_out", 0, (rows8, D_MODEL), tr=128)
    res.update(upd_ffn(recv3a + recv3b[:1], "ffn2", 0))
    res["ple_wg", 0] = upd(recv3b[1], "ple_wg", 0, (rows8, D_MODEL), tr=128)
    res["ple_wp", 0] = upd(recv3b[2], "ple_wp", 0, (D_PLE, LANES))
    res["ab_w_out", 0] = upd(recv3b[3], "ab_w_out", 0, (rows8, D_MODEL), tr=128)
    res.update(upd_ffn(recv4a + recv4b + list(recv4c), "ffn1", 0))
    res["ab_w_in", 0] = upd(recv3c[0], "ab_w_in", 0, (D_MODEL, AB_PROJ // N_DEV))
    res_small = adamw_rows(recv3c[1], 0, *[_flat_pad([a[pre + n] for n in small_all], F32, SMALL_F32).reshape(
        small_rows, LANES) for pre in ("", "m_", "v_")], name="adamw_small", tr=small_rows)
    kinds = []
    for k in range(4):
        kd = _take(res_small[k].reshape(-1), small_all, shapes)
        for n in WEIGHTS:
            if n not in kd:
                kd[n] = jnp.stack([res[n, l][k] for l in range(shapes[n][0])]).reshape(shapes[n])
        kinds.append(kd)
    loss = lax.psum(loss_part[0, 0], ("x", "y", "c"))
    return (loss, grad_x, *[kinds[0][n] for n in WEIGHTS], *[kinds[1][n] for n in WEIGHTS],
            *[kinds[2][n] for n in WEIGHTS], *[kinds[3][n] for n in WEIGHTS])


def join_cols(x, *, name, outs=None, tk=256):
    _, kk, n = x.shape
    tk = _tile8(kk, tk)
    outs = outs or [(0, N_DEV * n, N_DEV * n)]

    def body(x_ref, *o_refs):
        full = jnp.concatenate([x_ref[k] for k in range(N_DEV)], axis=-1)
        for (lo, hi, wd), o_ref in zip(outs, o_refs):
            piece = full[:, lo:hi]
            if wd > hi - lo:
                piece = jnp.concatenate([piece, jnp.zeros((tk, wd - (hi - lo)), piece.dtype)], axis=-1)
            o_ref[...] = piece

    res = _pcall(
        body, name=name, grid=(kk // tk,), in_specs=[pl.BlockSpec((N_DEV, tk, n), lambda i: (0, i, 0))],
        out_specs=[pl.BlockSpec((tk, wd), lambda i: (i, 0)) for _, _, wd in outs],
        out_shape=[jax.ShapeDtypeStruct((kk, wd), x.dtype) for _, _, wd in outs], compiler_params=_cp("parallel"),
    )(x)
    return res if len(outs) > 1 else res[0]


def split_cols(pieces, n, *, name, tk=256):
    kk = pieces[0][0].shape[0]
    tk = _tile8(kk, tk)

    def body(*refs):
        o_ref = refs[-1]
        vals = [r[...][:, :used] for r, (_, used) in zip(refs[:-1], pieces)]
        full = vals[0] if len(vals) == 1 else jnp.concatenate(vals, axis=-1)
        for k in range(N_DEV):
            o_ref[k] = full[:, k * n:(k + 1) * n].astype(MM)

    return _pcall(
        body, name=name, grid=(kk // tk,),
        in_specs=[pl.BlockSpec((tk, arr.shape[1]), lambda i: (i, 0)) for arr, _ in pieces],
        out_specs=pl.BlockSpec((N_DEV, tk, n), lambda i: (0, i, 0)),
        out_shape=jax.ShapeDtypeStruct((N_DEV, kk, n), MM), compiler_params=_cp("parallel"),
    )(*[arr for arr, _ in pieces])


def gather_multi(shards, *, name):
    ng = len(shards)

    def body(*refs):
        x_refs, o_refs = refs[:ng], refs[ng:2 * ng]
        send_sems, recv_sems, local_sems = refs[2 * ng:]
        x, y, c = _me()
        sibling = (x, y, 1 - c)
        chips = [(1 - x, y), (x, 1 - y), (1 - x, 1 - y)]

        def slot(px, py, pc):
            return 4 * px + 2 * py + pc

        def copy(gi, k, block, to, src=None):
            dst = o_refs[gi].at[slot(*block)]
            return pltpu.make_async_remote_copy(
                src_ref=dst if src is None else src, dst_ref=dst, send_sem=send_sems.at[7 * gi + k],
                recv_sem=recv_sems.at[7 * gi + k], device_id=to, device_id_type=MESH_ID)

        own = [pltpu.make_async_copy(x_refs[gi], o_refs[gi].at[slot(x, y, c)], local_sems.at[gi]) for gi in range(ng)]
        for cp in own:
            cp.start()
        first = []
        for gi in range(ng):
            first.append(copy(gi, 0, (x, y, c), sibling, src=x_refs[gi]))
            first += [copy(gi, 1 + j, (x, y, c), (*chip, c), src=x_refs[gi]) for j, chip in enumerate(chips)]
        for cp in first:
            cp.start()
        passed = []
        for j, chip in enumerate(chips):
            for gi in range(ng):
                copy(gi, 1 + j, (*chip, c), (x, y, c)).wait_recv()
                fwd = copy(gi, 4 + j, (*chip, c), sibling)
                fwd.start()
                passed.append(fwd)
        for gi in range(ng):
            copy(gi, 0, sibling, (x, y, c)).wait_recv()
            for j, chip in enumerate(chips):
                copy(gi, 4 + j, (*chip, 1 - c), (x, y, c)).wait_recv()
        for cp in first + passed:
            cp.wait_send()
        for cp in own:
            cp.wait()

    hbm = pl.BlockSpec(memory_space=pl.ANY)
    return _pcall(
        body, name=name, in_specs=[hbm] * ng, out_specs=[hbm] * ng,
        out_shape=[jax.ShapeDtypeStruct((N_DEV,) + s.shape, s.dtype) for s in shards],
        scratch_shapes=[pltpu.SemaphoreType.DMA((7 * ng,)), pltpu.SemaphoreType.DMA((7 * ng,)),
                        pltpu.SemaphoreType.DMA((ng,))],
    )(*shards)


def exchange_multi(parts, *, name):
    ng = len(parts)

    def body(*refs):
        x_refs, o_refs = refs[:ng], refs[ng:2 * ng]
        send_sems, recv_sems, local_sems = refs[2 * ng:]
        x, y, c = _me()
        mine = 4 * x + 2 * y + c
        own = [pltpu.make_async_copy(x_refs[gi].at[mine], o_refs[gi].at[mine], local_sems.at[gi]) for gi in range(ng)]
        for cp in own:
            cp.start()
        copies = []
        for k, (dx, dy, dc) in enumerate(_FLIPS):
            px, py, pc = _flip(x, dx), _flip(y, dy), _flip(c, dc)
            for gi in range(ng):
                cp = pltpu.make_async_remote_copy(
                    src_ref=x_refs[gi].at[4 * px + 2 * py + pc], dst_ref=o_refs[gi].at[mine],
                    send_sem=send_sems.at[7 * gi + k], recv_sem=recv_sems.at[7 * gi + k], device_id=(px, py, pc),
                    device_id_type=MESH_ID)
                cp.start()
                copies.append(cp)
        for cp in copies:
            cp.wait()
        for cp in own:
            cp.wait()

    hbm = pl.BlockSpec(memory_space=pl.ANY)
    return _pcall(
        body, name=name, in_specs=[hbm] * ng, out_specs=[hbm] * ng,
        out_shape=[jax.ShapeDtypeStruct(s.shape, s.dtype) for s in parts],
        scratch_shapes=[pltpu.SemaphoreType.DMA((7 * ng,)), pltpu.SemaphoreType.DMA((7 * ng,)),
                        pltpu.SemaphoreType.DMA((ng,))],
    )(*parts)


def adamw_rows(parts, row0, w, m, v, *, name, tr=256):
    r, cdim = w.shape
    tr = _tile8(math.gcd(r, row0) if row0 else r, tr)
    blk0 = row0 // tr

    def body(p_ref, w_ref, m_ref, v_ref, g_ref, d_ref, mo_ref, vo_ref):
        g = p_ref[0].astype(F32)
        for j in range(1, N_DEV):
            g = g + p_ref[j].astype(F32)
        g_ref[...] = g
        mn = ADAM_B1 * m_ref[...] + (1.0 - ADAM_B1) * g
        vn = ADAM_B2 * v_ref[...] + (1.0 - ADAM_B2) * (g * g)
        mo_ref[...] = mn
        vo_ref[...] = vn
        m_hat = mn / (1.0 - ADAM_B1 ** ADAM_STEP)
        v_hat = vn / (1.0 - ADAM_B2 ** ADAM_STEP)
        d_ref[...] = -ADAM_LR * (m_hat / (jnp.sqrt(v_hat) + ADAM_EPS) + ADAM_WD * w_ref[...])

    blk = pl.BlockSpec((tr, cdim), lambda i: (i, 0))
    shp = jax.ShapeDtypeStruct((r, cdim), F32)
    return _pcall(
        body, name=name, grid=(r // tr,),
        in_specs=[pl.BlockSpec((N_DEV, tr, cdim), lambda i: (0, blk0 + i, 0)), blk, blk, blk],
        out_specs=[blk, blk, blk, blk], out_shape=[shp, shp, shp, shp], compiler_params=_cp("parallel"),
    )(parts, w, m, v)


GROUP_A = ["ffn1_wg", "ffn1_wu", "ffn2_wg", "ffn2_wu"]
GROUP_B = ["ffn1_wd", "ffn2_wd", "ple_wg", "ab_w_out", "c_w_out"]
SMALL_NAMES = ["ln_g", "ln_b", "b_conv_w", "c_conv_w"]
LANES = 128
FFN_TM = 512
FFN_TF = 1408
SMALL_F32 = 73728
PLE_WP_ROWS = DEPTH * D_PLE


def _step2(a):
    x, p = a["x"], a["p"]
    bsz, s_len, d = x.shape
    t = bsz * s_len
    x2 = x.reshape(t, d)
    tgt = a["loss_target"].reshape(t, d)
    p2 = p.reshape(DEPTH, t, D_PLE)
    shapes = {n: a[n].shape for n in WEIGHTS}
    bits_per = 1 if MM == F32 else 2
    n_small = sum(int(np.prod(shapes[n])) for n in SMALL_NAMES)
    small_all = SMALL_NAMES + REPL
    f_ff = shapes["ffn1_wg"][2]
    rows_b = {n: shapes[n][0] * shapes[n][1] for n in GROUP_B}
    off_b = dict(zip(GROUP_B, np.cumsum([0] + [rows_b[n] for n in GROUP_B])[:-1].tolist()))

    send = [
        jnp.concatenate([a[n].astype(MM).reshape(-1, f_ff) for n in GROUP_A], axis=0),
        jnp.concatenate([a[n].astype(MM).reshape(-1, D_MODEL) for n in GROUP_B], axis=0),
        a["ab_w_in"][0].astype(MM),
        a["c_w_in"][0].astype(MM),
        a["ple_wp"].astype(MM).reshape(PLE_WP_ROWS, LANES),
        _flat_pad([a[n] for n in SMALL_NAMES], F32, 32 * LANES).reshape(32, LANES),
    ]
    ga, gb, gc, gd, ge, gf = gather_multi(send, name="gather_weights")
    wa_full = join_cols(ga, name="join_ffn").reshape(len(GROUP_A), DEPTH, D_MODEL, N_DEV * f_ff)
    w = {n: wa_full[i] for i, n in enumerate(GROUP_A)}
    for n in GROUP_B:
        lyr, rws = shapes[n][0], shapes[n][1]
        blk = gb[:, off_b[n]:off_b[n] + rows_b[n]].reshape(N_DEV, lyr, rws, D_MODEL)
        w[n] = jnp.swapaxes(blk, 0, 1).reshape(lyr, N_DEV * rws, D_MODEL)
    w["ab_w_in"] = join_cols(gc, name="join_ab_in")
    c_in_main, c_in_gate = join_cols(gd, name="join_c_in", outs=[(0, 4 * C_WIDTH, 4 * C_WIDTH),
                                                                  (4 * C_WIDTH, 4 * C_WIDTH + 2 * C_HEADS, LANES)])
    w["ple_wp"] = _join(ge.reshape(N_DEV, DEPTH, D_PLE, LANES), 2)
    ws = _take(gf.reshape(N_DEV, -1), SMALL_NAMES, shapes)
    w.update({n: _join(ws[n], 2) for n in SMALL_NAMES})
    ln_g, ln_b = w["ln_g"], w["ln_b"]
    wa_d, wx_d = _dense_blocks(a["b_wa"][0]), _dense_blocks(a["b_wx"][0])
    lru_w = (w["b_conv_w"][0], a["b_conv_b"][0], wa_d, a["b_ba"][0], wx_d, a["b_bx"][0], a["b_lam"][0])
    gdc_w = (a["c_a_log"][0], a["c_dt_bias"][0], a["c_norm_g"][0])

    h = x2
    saved = []
    for i in range(DEPTH):
        s = {"x0": h}
        s["y1"], s["z1"], s["hg1"], s["hu1"] = ffn_fwd(h, w["ffn1_wg"][i], w["ffn1_wu"][i], w["ffn1_wd"][i], ln_g[i, 0], ln_b[i, 0],
                                   name=f"ffn1_fwd_{i}")
        if i == 0:
            s["proj"] = matmul(s["y1"], w["ab_w_in"], mode="nn", name="ab_in_fwd", tn=896, tk=1024)
            ya = attn_fwd(s["proj"], a["a_sinks"][0], bsz, name="attn_fwd")
            yb = lru_fwd(s["proj"], *lru_w, bsz, name="lru_fwd")
            s["mix"] = jnp.concatenate([ya, yb], axis=1)
            w_out = w["ab_w_out"][0]
        else:
            s["proj"] = matmul(s["y1"], c_in_main, mode="nn", name="c_in_fwd", tm=1024, tn=2048, tk=1024)
            s["gates"] = matmul(s["y1"], c_in_gate, mode="nn", name="c_gate_fwd", tk=1024)
            s["qkv"] = gdc_pre_fwd(s["proj"], w["c_conv_w"][0], bsz, name="gdc_pre_fwd")
            s["mix"], s["o_pre"], s["vnew"], s["tmat"], s["states"] = gdc_fwd(
                s["qkv"], s["proj"], s["gates"], *gdc_w, bsz, name="gdc_fwd")
            w_out = w["c_w_out"][0]
        s["y2"], s["z2"] = mm_ln_fwd(s["mix"], w_out, s["y1"], ln_g[i, 1], ln_b[i, 1], name=f"mix_out_fwd_{i}")
        s["y3"], s["z3"], s["hg2"], s["hu2"] = ffn_fwd(s["y2"], w["ffn2_wg"][i], w["ffn2_wu"][i], w["ffn2_wd"][i], ln_g[i, 2], ln_b[i, 2],
                                   name=f"ffn2_fwd_{i}")
        h = ple_fwd(s["y3"], p2[i], w["ple_wg"][i], a["ple_bg"][i], w["ple_wp"][i], name=f"ple_fwd_{i}")
        saved.append(s)
    loss_part, dh = loss_fwd_bwd(h, tgt, name="loss")

    g = {n: [None] * shapes[n][0] for n in ("ffn1_wg", "ffn1_wu", "ffn1_wd", "ffn2_wg", "ffn2_wu", "ffn2_wd", "ln_g",
                                             "ln_b", "ple_wg", "ple_bg", "ple_wp")}
    wide = dict(tm=1024, tn=1408, tk=1024)
    tall = dict(tm=1408, tn=1024, tk=1024)
    for i in reversed(range(DEPTH)):
        s = saved[i]
        dy3, dt, de, dbg = ple_bwd(dh, s["y3"], p2[i], w["ple_wg"][i], a["ple_bg"][i], w["ple_wp"][i], name=f"ple_bwd_{i}")
        g["ple_wg"][i] = matmul(s["y3"], dt, mode="tn", name=f"ple_wg_grad_{i}", tm=1024, tn=1024)
        g["ple_wp"][i] = matmul(p2[i], de, mode="tn", name=f"ple_wp_grad_{i}", tn=1024)
        g["ple_bg"][i] = dbg[0]
        dz3, dg2, db2 = ln_bwd(dy3, s["z3"], ln_g[i, 2], name=f"ln2_bwd_{i}")
        dy2, act, dhg, dhu = ffn_bwd(dz3, s["hg2"], s["hu2"], w["ffn2_wg"][i], w["ffn2_wu"][i], w["ffn2_wd"][i], name=f"ffn2_bwd_{i}", tm=FFN_TM, tf=FFN_TF)
        g["ffn2_wg"][i] = matmul(s["y2"], dhg, mode="tn", name=f"ffn2_wg_grad_{i}", **wide)
        g["ffn2_wu"][i] = matmul(s["y2"], dhu, mode="tn", name=f"ffn2_wu_grad_{i}", **wide)
        g["ffn2_wd"][i] = matmul(act, dz3, mode="tn", scale=0.5, name=f"ffn2_wd_grad_{i}", **tall)
        dz2, dg1, db1 = ln_bwd(dy2, s["z2"], ln_g[i, 1], name=f"ln1_bwd_{i}")
        if i == 0:
            dmix = matmul(dz2, w["ab_w_out"][0], mode="nt", name="ab_out_bwd", tn=1024, tk=1024)
            g["ab_w_out"] = matmul(s["mix"], dz2, mode="tn", name="ab_out_grad", tm=1024, tn=1024)
            dq, dk, dv, dsk = attn_bwd(s["proj"], a["a_sinks"][0], dmix, bsz, name="attn_bwd")
            dbx, dbgate, dcw, dcb, dwa, dba, dwx, dbxb, dlam = lru_bwd(s["proj"], *lru_w, dmix, bsz, name="lru_bwd")
            dproj = jnp.concatenate([dq, dk, dv, dbx, dbgate], axis=1).astype(MM)
            dy1 = matmul(dproj, w["ab_w_in"], mode="nt", add=dz2, add_scale=DN_ALPHA, name="ab_in_bwd", tn=1024, tk=1792)
            g_ab_in = matmul(s["y1"], dproj, mode="tn", name="ab_in_grad", tm=1024, tn=896)
        else:
            dmix = matmul(dz2, w["c_w_out"][0], mode="nt", name="c_out_bwd", tn=1024, tk=1024)
            g["c_w_out"] = matmul(s["mix"], dz2, mode="tn", name="c_out_grad", tm=1024, tn=1024)
            dqkv, dzc, dgates, dsm = gdc_bwd(s["qkv"], s["proj"], s["gates"], *gdc_w, s["o_pre"], s["vnew"], s["tmat"],
                                             s["states"], dmix, bsz, name="gdc_bwd")
            draw, dccw = gdc_pre_bwd(s["proj"], w["c_conv_w"][0], dqkv, bsz, name="gdc_pre_bwd")
            dproj = jnp.concatenate([draw, dzc], axis=1).astype(MM)
            dgb = dgates.astype(MM)
            dy1 = matmul(dproj, c_in_main, mode="nt", add=dz2, add_scale=DN_ALPHA, name="c_in_bwd", tn=1024, tk=4096)
            dy1 = matmul(dgb, c_in_gate, mode="nt", add=dy1, name="c_gate_bwd", tn=1024)
            g_c_main = matmul(s["y1"], dproj, mode="tn", name="c_in_grad", tm=1024, tn=1024, tk=1024)
            g_c_gate = matmul(s["y1"], dgb, mode="tn", name="c_gate_grad", tm=1024)
        dz1, dg0, db0 = ln_bwd(dy1, s["z1"], ln_g[i, 0], name=f"ln0_bwd_{i}")
        dh, act, dhg, dhu = ffn_bwd(dz1, s["hg1"], s["hu1"], w["ffn1_wg"][i], w["ffn1_wu"][i], w["ffn1_wd"][i], name=f"ffn1_bwd_{i}", tm=FFN_TM, tf=FFN_TF)
        g["ffn1_wg"][i] = matmul(s["x0"], dhg, mode="tn", name=f"ffn1_wg_grad_{i}", **wide)
        g["ffn1_wu"][i] = matmul(s["x0"], dhu, mode="tn", name=f"ffn1_wu_grad_{i}", **wide)
        g["ffn1_wd"][i] = matmul(act, dz1, mode="tn", scale=0.5, name=f"ffn1_wd_grad_{i}", **tall)
        g["ln_g"][i] = jnp.concatenate([dg0, dg1, dg2], axis=0)
        g["ln_b"][i] = jnp.concatenate([db0, db1, db2], axis=0)
    grad_x = dh.reshape(bsz, s_len, d)
    full = {n: jnp.stack(v) if isinstance(v, list) else v[None] for n, v in g.items()}
    full["b_conv_w"] = dcw[None]
    full["c_conv_w"] = jnp.sum(dccw, axis=0)[None]
    dsm_sum = jnp.sum(dsm, axis=0)
    full.update(a_sinks=jnp.sum(dsk, axis=0)[:, :A_HEADS], b_conv_b=dcb, b_wa=_diag_blocks(dwa)[None], b_ba=dba,
                b_wx=_diag_blocks(dwx)[None], b_bx=dbxb, b_lam=dlam, c_a_log=dsm_sum[None, :C_HEADS, 0],
                c_dt_bias=dsm_sum[None, :C_HEADS, 1], c_norm_g=jnp.sum(dsm_sum[C_HEADS:], axis=0)[None])

    small_f32_rows = SMALL_F32 // LANES
    repl_flat = _flat_pad([full[n] for n in REPL], F32, SMALL_F32 - n_small)
    small8 = jnp.concatenate([_flat8_pad([_split(full[n], 2) for n in SMALL_NAMES], F32, n_small),
                              jnp.broadcast_to(repl_flat, (N_DEV,) + repl_flat.shape)], axis=1)
    parts = [
        split_cols([(jnp.concatenate([full[n].reshape(-1, N_DEV * f_ff) for n in GROUP_A], axis=0), N_DEV * f_ff)], f_ff,
                   name="split_ffn"),
        jnp.concatenate([_split(full[n], 1).astype(MM).reshape(N_DEV, -1, D_MODEL) for n in GROUP_B], axis=1),
        split_cols([(g_ab_in, AB_PROJ)], AB_PROJ // N_DEV, name="split_ab_in"),
        split_cols([(g_c_main, 4 * C_WIDTH), (g_c_gate, 2 * C_HEADS)], (4 * C_WIDTH + 2 * C_HEADS) // N_DEV,
                   name="split_c_in"),
        _split(full["ple_wp"], 2).astype(MM).reshape(N_DEV, PLE_WP_ROWS, LANES),
        small8.reshape(N_DEV, small_f32_rows, LANES),
    ]
    ra, rb, rc, rd, re, small_parts = exchange_multi(parts, name="exchange_grads")

    def wmv(n, shape2d):
        return [a[pre + n].reshape(shape2d) for pre in ("", "m_", "v_")]

    res = {}
    for i, n in enumerate(GROUP_A):
        res[n] = adamw_rows(ra, i * DEPTH * D_MODEL, *wmv(n, (DEPTH * D_MODEL, f_ff)), name=f"adamw_{n}")
    for n in GROUP_B:
        res[n] = adamw_rows(rb, off_b[n], *wmv(n, (rows_b[n], D_MODEL)), name=f"adamw_{n}", tr=64)
    res["ab_w_in"] = adamw_rows(rc, 0, *wmv("ab_w_in", (D_MODEL, AB_PROJ // N_DEV)), name="adamw_ab_w_in")
    res["c_w_in"] = adamw_rows(rd, 0, *wmv("c_w_in", (D_MODEL, shapes["c_w_in"][2])), name="adamw_c_w_in")
    res["ple_wp"] = adamw_rows(re, 0, *wmv("ple_wp", (PLE_WP_ROWS, LANES)), name="adamw_ple_wp")
    res_small = adamw_rows(small_parts, 0, *[_flat_pad([a[pre + n] for n in small_all], F32, SMALL_F32).reshape(
        small_f32_rows, LANES) for pre in ("", "m_", "v_")], name="adamw_small", tr=576)
    kinds = []
    for k in range(4):
        kd = {n: res[n][k].reshape(shapes[n]) for n in res}
        kd.update(_take(res_small[k].reshape(-1), small_all, shapes))
        kinds.append(kd)
    loss = lax.psum(loss_part[0, 0], ("x", "y", "c"))
    return (loss, grad_x, *[kinds[0][n] for n in WEIGHTS], *[kinds[1][n] for n in WEIGHTS],
            *[kinds[2][n] for n in WEIGHTS], *[kinds[3][n] for n in WEIGHTS])


BIG_ROWS = 5632
SMALL_F32 = 73728


def _flat_pad(arrs, dtype, total):
    flat = jnp.concatenate([z.astype(dtype).reshape(-1) for z in arrs])
    return jnp.pad(flat, (0, total - flat.shape[0]))


def _flat8_pad(arrs, dtype, total):
    flat = jnp.concatenate([z.astype(dtype).reshape(N_DEV, -1) for z in arrs], axis=1)
    return jnp.pad(flat, ((0, 0), (0, total - flat.shape[1])))


def _bits(z):
    return z if MM == F32 else _as_bf16_bits(z)


def _unbits(z):
    return z if MM == F32 else _from_bf16_bits(z)


def _take(flat, names, shapes):
    out, off = {}, 0
    for n in names:
        sz = int(np.prod(shapes[n]))
        out[n] = flat[..., off:off + sz].reshape(flat.shape[:-1] + tuple(shapes[n]))
        off += sz
    return out


def _step(a):
    x, p = a["x"], a["p"]
    bsz, s_len, d = x.shape
    t = bsz * s_len
    x2 = x.reshape(t, d)
    tgt = a["loss_target"].reshape(t, d)
    p2 = p.reshape(DEPTH, t, D_PLE)
    shapes = {n: a[n].shape for n in WEIGHTS}
    big_names = [n for n, _ in BIG]
    small_names = [n for n, _ in SMALL]
    n_small = sum(int(np.prod(shapes[n])) for n in small_names)
    bits_per = 1 if MM == F32 else 2
    small_rows = -(-(n_small * bits_per) // PACK_ALIGN) * (PACK_ALIGN // PACK_COLS)

    send = jnp.concatenate([
        _flat_pad([a[n] for n in big_names], MM, BIG_ROWS * PACK_COLS).reshape(BIG_ROWS, PACK_COLS),
        _bits(_flat_pad([a[n] for n in small_names], F32, small_rows * PACK_COLS // bits_per)).reshape(small_rows, PACK_COLS),
    ], axis=0)
    gathered = all_gather(send, name="gather_weights")
    wb = _take(gathered[:, :BIG_ROWS].reshape(N_DEV, -1), big_names, shapes)
    ws = _take(_unbits(gathered[:, BIG_ROWS:].reshape(N_DEV, -1)), small_names, shapes)
    w = {n: _join(wb[n], ax) for n, ax in BIG}
    w.update({n: _join(ws[n], ax) for n, ax in SMALL})
    ln_g, ln_b = w["ln_g"], w["ln_b"]
    c_in_main = w["c_w_in"][0][:, :4 * C_WIDTH]
    c_in_gate = jnp.pad(w["c_w_in"][0][:, 4 * C_WIDTH:], ((0, 0), (0, 128 - 2 * C_HEADS)))
    wa_d, wx_d = _dense_blocks(a["b_wa"][0]), _dense_blocks(a["b_wx"][0])
    lru_w = (w["b_conv_w"][0], a["b_conv_b"][0], wa_d, a["b_ba"][0], wx_d, a["b_bx"][0], a["b_lam"][0])
    gdc_w = (a["c_a_log"][0], a["c_dt_bias"][0], a["c_norm_g"][0])

    h = x2
    saved = []
    for i in range(DEPTH):
        s = {"x0": h}
        s["y1"], s["z1"], s["hg1"], s["hu1"] = ffn_fwd(h, w["ffn1_wg"][i], w["ffn1_wu"][i], w["ffn1_wd"][i], ln_g[i, 0], ln_b[i, 0],
                                   name=f"ffn1_fwd_{i}")
        if i == 0:
            s["proj"] = matmul(s["y1"], w["ab_w_in"][0], mode="nn", name="ab_in_fwd")
            ya = attn_fwd(s["proj"], a["a_sinks"][0], bsz, name="attn_fwd")
            yb = lru_fwd(s["proj"], *lru_w, bsz, name="lru_fwd")
            s["mix"] = jnp.concatenate([ya, yb], axis=1)
            w_out = w["ab_w_out"][0]
        else:
            s["proj"] = matmul(s["y1"], c_in_main, mode="nn", name="c_in_fwd")
            s["gates"] = matmul(s["y1"], c_in_gate, mode="nn", name="c_gate_fwd")
            s["qkv"] = gdc_pre_fwd(s["proj"], w["c_conv_w"][0], bsz, name="gdc_pre_fwd")
            s["mix"], s["o_pre"], s["vnew"], s["tmat"], s["states"] = gdc_fwd(
                s["qkv"], s["proj"], s["gates"], *gdc_w, bsz, name="gdc_fwd")
            w_out = w["c_w_out"][0]
        s["y2"], s["z2"] = mm_ln_fwd(s["mix"], w_out, s["y1"], ln_g[i, 1], ln_b[i, 1], name=f"mix_out_fwd_{i}")
        s["y3"], s["z3"], s["hg2"], s["hu2"] = ffn_fwd(s["y2"], w["ffn2_wg"][i], w["ffn2_wu"][i], w["ffn2_wd"][i], ln_g[i, 2], ln_b[i, 2],
                                   name=f"ffn2_fwd_{i}")
        h = ple_fwd(s["y3"], p2[i], w["ple_wg"][i], a["ple_bg"][i], w["ple_wp"][i], name=f"ple_fwd_{i}")
        saved.append(s)
    loss_part, dh = loss_fwd_bwd(h, tgt, name="loss")

    g = {n: [None] * shapes[n][0] for n in ("ffn1_wg", "ffn1_wu", "ffn1_wd", "ffn2_wg", "ffn2_wu", "ffn2_wd", "ln_g",
                                             "ln_b", "ple_wg", "ple_bg", "ple_wp")}
    wide = dict(tm=1024, tn=1408, tk=1024)
    tall = dict(tm=1408, tn=1024, tk=1024)
    for i in reversed(range(DEPTH)):
        s = saved[i]
        dy3, dt, de, dbg = ple_bwd(dh, s["y3"], p2[i], w["ple_wg"][i], a["ple_bg"][i], w["ple_wp"][i], name=f"ple_bwd_{i}")
        g["ple_wg"][i] = matmul(s["y3"], dt, mode="tn", name=f"ple_wg_grad_{i}")
        g["ple_wp"][i] = matmul(p2[i], de, mode="tn", name=f"ple_wp_grad_{i}")
        g["ple_bg"][i] = dbg[0]
        dz3, dg2, db2 = ln_bwd(dy3, s["z3"], ln_g[i, 2], name=f"ln2_bwd_{i}")
        dy2, act, dhg, dhu = ffn_bwd(dz3, s["hg2"], s["hu2"], w["ffn2_wg"][i], w["ffn2_wu"][i], w["ffn2_wd"][i], name=f"ffn2_bwd_{i}", tm=FFN_TM, tf=FFN_TF)
        g["ffn2_wg"][i] = matmul(s["y2"], dhg, mode="tn", name=f"ffn2_wg_grad_{i}", **wide)
        g["ffn2_wu"][i] = matmul(s["y2"], dhu, mode="tn", name=f"ffn2_wu_grad_{i}", **wide)
        g["ffn2_wd"][i] = matmul(act, dz3, mode="tn", scale=0.5, name=f"ffn2_wd_grad_{i}", **tall)
        dz2, dg1, db1 = ln_bwd(dy2, s["z2"], ln_g[i, 1], name=f"ln1_bwd_{i}")
        if i == 0:
            dmix = matmul(dz2, w["ab_w_out"][0], mode="nt", name="ab_out_bwd")
            g["ab_w_out"] = matmul(s["mix"], dz2, mode="tn", name="ab_out_grad")
            dq, dk, dv, dsk = attn_bwd(s["proj"], a["a_sinks"][0], dmix, bsz, name="attn_bwd")
            dbx, dbgate, dcw, dcb, dwa, dba, dwx, dbxb, dlam = lru_bwd(s["proj"], *lru_w, dmix, bsz, name="lru_bwd")
            dproj = jnp.concatenate([dq, dk, dv, dbx, dbgate], axis=1).astype(MM)
            dy1 = matmul(dproj, w["ab_w_in"][0], mode="nt", add=dz2, add_scale=DN_ALPHA, name="ab_in_bwd")
            g["ab_w_in"] = matmul(s["y1"], dproj, mode="tn", name="ab_in_grad")
        else:
            dmix = matmul(dz2, w["c_w_out"][0], mode="nt", name="c_out_bwd")
            g["c_w_out"] = matmul(s["mix"], dz2, mode="tn", name="c_out_grad")
            dqkv, dzc, dgates, dsm = gdc_bwd(s["qkv"], s["proj"], s["gates"], *gdc_w, s["o_pre"], s["vnew"], s["tmat"],
                                             s["states"], dmix, bsz, name="gdc_bwd")
            draw, dccw = gdc_pre_bwd(s["proj"], w["c_conv_w"][0], dqkv, bsz, name="gdc_pre_bwd")
            dproj = jnp.concatenate([draw, dzc], axis=1).astype(MM)
            dgb = dgates.astype(MM)
            dy1 = matmul(dproj, c_in_main, mode="nt", add=dz2, add_scale=DN_ALPHA, name="c_in_bwd")
            dy1 = matmul(dgb, c_in_gate, mode="nt", add=dy1, name="c_gate_bwd")
            g["c_w_in"] = jnp.concatenate([matmul(s["y1"], dproj, mode="tn", name="c_in_grad"),
                                           matmul(s["y1"], dgb, mode="tn", name="c_gate_grad")[:, :2 * C_HEADS]], axis=1)
        dz1, dg0, db0 = ln_bwd(dy1, s["z1"], ln_g[i, 0], name=f"ln0_bwd_{i}")
        dh, act, dhg, dhu = ffn_bwd(dz1, s["hg1"], s["hu1"], w["ffn1_wg"][i], w["ffn1_wu"][i], w["ffn1_wd"][i], name=f"ffn1_bwd_{i}", tm=FFN_TM, tf=FFN_TF)
        g["ffn1_wg"][i] = matmul(s["x0"], dhg, mode="tn", name=f"ffn1_wg_grad_{i}", **wide)
        g["ffn1_wu"][i] = matmul(s["x0"], dhu, mode="tn", name=f"ffn1_wu_grad_{i}", **wide)
        g["ffn1_wd"][i] = matmul(act, dz1, mode="tn", scale=0.5, name=f"ffn1_wd_grad_{i}", **tall)
        g["ln_g"][i] = jnp.concatenate([dg0, dg1, dg2], axis=0)
        g["ln_b"][i] = jnp.concatenate([db0, db1, db2], axis=0)
    grad_x = dh.reshape(bsz, s_len, d)
    full = {n: jnp.stack(v) if isinstance(v, list) else v[None] for n, v in g.items()}
    full["b_conv_w"] = dcw[None]
    full["c_conv_w"] = jnp.sum(dccw, axis=0)[None]
    dsm_sum = jnp.sum(dsm, axis=0)
    full.update(a_sinks=jnp.sum(dsk, axis=0)[:, :A_HEADS], b_conv_b=dcb, b_wa=_diag_blocks(dwa)[None], b_ba=dba,
                b_wx=_diag_blocks(dwx)[None], b_bx=dbxb, b_lam=dlam, c_a_log=dsm_sum[None, :C_HEADS, 0],
                c_dt_bias=dsm_sum[None, :C_HEADS, 1], c_norm_g=jnp.sum(dsm_sum[C_HEADS:], axis=0)[None])

    small_cols = SMALL_F32 * bits_per // PACK_COLS
    repl_flat = _flat_pad([full[n] for n in REPL], F32, SMALL_F32 - n_small)
    small8 = jnp.concatenate([_flat8_pad([_split(full[n], ax) for n, ax in SMALL], F32, n_small),
                              jnp.broadcast_to(repl_flat, (N_DEV,) + repl_flat.shape)], axis=1)
    parts = jnp.concatenate([
        _flat8_pad([_split(full[n], ax) for n, ax in BIG], MM, BIG_ROWS * PACK_COLS).reshape(N_DEV, BIG_ROWS, PACK_COLS),
        _bits(small8).reshape(N_DEV, small_cols, PACK_COLS)], axis=1)
    recv = all_to_all(parts, name="exchange_grads")

    def mine(prefix, names, dtype_total):
        return _flat_pad([a[prefix + n] for n in names], F32, dtype_total)

    outs = {}
    big_total = BIG_ROWS * PACK_COLS
    res_big = adamw_sum(recv, *[mine(pre, big_names, big_total).reshape(BIG_ROWS, PACK_COLS) for pre in ("", "m_", "v_")],
                        name="adamw_big")
    small_all = small_names + REPL
    cols_f32 = PACK_COLS // bits_per
    res_small = adamw_sum(_unbits(recv[:, BIG_ROWS:]).reshape(N_DEV, small_cols, cols_f32),
                          *[mine(pre, small_all, SMALL_F32).reshape(small_cols, cols_f32) for pre in ("", "m_", "v_")],
                          name="adamw_small")
    kinds = []
    for rb, rs in zip(res_big, res_small):
        k = _take(rb.reshape(-1), big_names, shapes)
        k.update(_take(rs.reshape(-1), small_all, shapes))
        kinds.append(k)
    loss = lax.psum(loss_part[0, 0], ("x", "y", "c"))
    return (loss, grad_x, *[kinds[0][n] for n in WEIGHTS], *[kinds[1][n] for n in WEIGHTS],
            *[kinds[2][n] for n in WEIGHTS], *[kinds[3][n] for n in WEIGHTS])
```

```python
import functools
import math

import numpy as np
import jax
import jax.numpy as jnp
from jax import lax
from jax.experimental import pallas as pl
from jax.experimental.pallas import tpu as pltpu

F32 = jnp.float32
MM = jnp.bfloat16
HI = lax.Precision.HIGHEST

D_MODEL = 1024
D_FF = 2816
D_PLE = 256
DEPTH = 2
CHUNK = 64
A_HEADS = 8
A_KV_HEADS = 2
A_GROUP = 4
A_HEAD_DIM = 64
A_WIDTH = 512
A_KV_WIDTH = 128
B_WIDTH = 512
B_BLOCK = 64
RG_C = 8.0
AB_PROJ = 1792
C_HEADS = 8
C_HEAD_DIM = 128
C_WIDTH = 1024
DN_ALPHA = (2.0 * DEPTH) ** 0.25
LN_EPS = 1e-5
NORM_EPS = 1e-6
NEG = -1e30
ADAM_LR = 0.001
ADAM_B1 = 0.9
ADAM_B2 = 0.999
ADAM_EPS = 1e-08
ADAM_WD = 0.01
ADAM_STEP = 10
N_DEV = 8
VMEM_LIMIT = 56 * 1024 * 1024

NN = ((1,), (0,))
NT = ((1,), (1,))
TN = ((0,), (0,))


def _pcall(body, **kw):
    return pl.pallas_call(body, **kw)


def _cp(*sem):
    return pltpu.CompilerParams(dimension_semantics=sem, vmem_limit_bytes=VMEM_LIMIT)


MESH_ID = pl.DeviceIdType.MESH
_FLIPS = [(0, 0, 1), (1, 0, 0), (0, 1, 0), (1, 1, 0), (1, 0, 1), (0, 1, 1), (1, 1, 1)]


def _me():
    return lax.axis_index("x"), lax.axis_index("y"), lax.axis_index("c")


def _flip(coord, d):
    return 1 - coord if d else coord


def _side_copies(kind, x_refs, o_refs, send_sems, recv_sems, local_sems, start):
    x, y, c = _me()
    mine = 4 * x + 2 * y + c
    for gi, (x_ref, o_ref) in enumerate(zip(x_refs, o_refs)):
        src_own = x_ref if kind == "gather" else x_ref.at[mine]
        own = pltpu.make_async_copy(src_own, o_ref.at[mine], local_sems.at[gi])
        own.start() if start else own.wait()
        for k, (dx, dy, dc) in enumerate(_FLIPS):
            px, py, pc = _flip(x, dx), _flip(y, dy), _flip(c, dc)
            src = x_ref if kind == "gather" else x_ref.at[4 * px + 2 * py + pc]
            cp = pltpu.make_async_remote_copy(
                src_ref=src, dst_ref=o_ref.at[mine], send_sem=send_sems.at[7 * gi + k], recv_sem=recv_sems.at[7 * gi + k],
                device_id=(px, py, pc), device_id_type=MESH_ID)
            cp.start() if start else cp.wait()


def _call(body, args, side, grid, **kw):
    if side is None:
        return _pcall(body, grid=grid, **kw)(*args)
    kind, arrs = side
    ns, n_in, n_out = len(arrs), len(args), len(kw["out_specs"])
    scratch = list(kw.get("scratch_shapes", []))
    n_scr = len(scratch)

    def edge(at_end):
        conds = [pl.program_id(ax) == (n - 1 if at_end else 0) for ax, n in enumerate(grid)]
        return functools.reduce(jnp.logical_and, conds)

    def wrapped(*refs):
        ins, sx = refs[:n_in], refs[n_in:n_in + ns]
        outs, so = refs[n_in + ns:n_in + ns + n_out], refs[n_in + ns + n_out:n_in + 2 * ns + n_out]
        rest = refs[n_in + 2 * ns + n_out:]
        scr, sems = rest[:n_scr], rest[n_scr:]

        @pl.when(edge(False))
        def _():
            _side_copies(kind, sx, so, *sems, start=True)

        body(*ins, *outs, *scr)

        @pl.when(edge(True))
        def _():
            _side_copies(kind, sx, so, *sems, start=False)

    hbm = pl.BlockSpec(memory_space=pl.ANY)
    side_shapes = [jax.ShapeDtypeStruct(((N_DEV,) if kind == "gather" else ()) + z.shape, z.dtype) for z in arrs]
    kw = dict(kw)
    kw["in_specs"] = list(kw["in_specs"]) + [hbm] * ns
    kw["out_specs"] = list(kw["out_specs"]) + [hbm] * ns
    kw["out_shape"] = list(kw["out_shape"]) + side_shapes
    kw["scratch_shapes"] = scratch + [pltpu.SemaphoreType.DMA((7 * ns,)), pltpu.SemaphoreType.DMA((7 * ns,)),
                                      pltpu.SemaphoreType.DMA((ns,))]
    kw["compiler_params"] = _cp(*["arbitrary"] * len(grid))
    res = _pcall(wrapped, grid=grid, **kw)(*args, *arrs)
    return list(res[:n_out]), list(res[n_out:])


def _dot(a, b, dims=NN, precision=None):
    return lax.dot_general(a, b, (dims, ((), ())), preferred_element_type=F32, precision=precision)


def _mdot(a, b, dims=NN):
    return _dot(a.astype(MM), b.astype(MM), dims)


def _tile(n, pref):
    if n <= pref:
        return n
    for c in range(pref - pref % 128, 0, -128):
        if n % c == 0:
            return c
    return n


def _sigmoid(x):
    return 1.0 / (1.0 + jnp.exp(-x))


def _softplus(x):
    return jnp.maximum(x, 0.0) + jnp.log(1.0 + jnp.exp(-jnp.abs(x)))


def _ln_stats(z):
    mu = jnp.mean(z, axis=-1, keepdims=True)
    zc = z - mu
    var = jnp.mean(zc * zc, axis=-1, keepdims=True)
    return zc, lax.rsqrt(var + LN_EPS)


def matmul(a, b, *, mode, name, tm=512, tn=512, tk=512, out_dtype=F32, scale=None, add=None, add_scale=1.0, side=None):
    if mode == "nn":
        (m, kk), (_, n) = a.shape, b.shape
        dims = NN
    elif mode == "nt":
        (m, kk), (n, _) = a.shape, b.shape
        dims = NT
    else:
        (kk, m), (_, n) = a.shape, b.shape
        dims = TN
    tm, tn, tk = _tile(m, tm), _tile(n, tn), _tile(kk, tk)
    if mode == "nn":
        a_spec = pl.BlockSpec((tm, tk), lambda i, j, k: (i, k))
        b_spec = pl.BlockSpec((tk, tn), lambda i, j, k: (k, j))
    elif mode == "nt":
        a_spec = pl.BlockSpec((tm, tk), lambda i, j, k: (i, k))
        b_spec = pl.BlockSpec((tn, tk), lambda i, j, k: (j, k))
    else:
        a_spec = pl.BlockSpec((tk, tm), lambda i, j, k: (k, i))
        b_spec = pl.BlockSpec((tk, tn), lambda i, j, k: (k, j))
    nk = kk // tk
    o_spec = pl.BlockSpec((tm, tn), lambda i, j, k: (i, j))
    has_add = add is not None

    def body(*refs):
        if has_add:
            a_ref, b_ref, add_ref, o_ref, acc_ref = refs
        else:
            a_ref, b_ref, o_ref, acc_ref = refs
        k = pl.program_id(2)

        @pl.when(k == 0)
        def _():
            acc_ref[...] = jnp.zeros_like(acc_ref)

        acc_ref[...] += _mdot(a_ref[...], b_ref[...], dims)

        @pl.when(k == nk - 1)
        def _():
            r = acc_ref[...]
            if scale is not None:
                r = r * scale
            if has_add:
                r = r + add_scale * add_ref[...].astype(F32)
            o_ref[...] = r.astype(out_dtype)

    ins = [a, b] + ([add] if has_add else [])
    in_specs = [a_spec, b_spec] + ([o_spec] if has_add else [])
    res = _call(
        body, ins, side, (m // tm, n // tn, nk), name=name, in_specs=in_specs, out_specs=[o_spec],
        out_shape=[jax.ShapeDtypeStruct((m, n), out_dtype)], scratch_shapes=[pltpu.VMEM((tm, tn), F32)],
        compiler_params=_cp("parallel", "parallel", "arbitrary"),
    )
    return res[0] if side is None else (res[0][0], res[1])


def ffn_fwd(x, wg, wu, wd, g, b, *, name, tm=512, tf=256, side=None):
    t, d = x.shape
    f = wg.shape[1]
    tm = min(tm, t)
    nj = f // tf

    def body(x_ref, wg_ref, wu_ref, wd_ref, g_ref, b_ref, y_ref, z_ref, hg_ref, hu_ref, xb_ref, acc_ref):
        j = pl.program_id(1)

        @pl.when(j == 0)
        def _():
            xb_ref[...] = x_ref[...].astype(MM)
            acc_ref[...] = jnp.zeros_like(acc_ref)

        xb = xb_ref[...]
        hg = _dot(xb, wg_ref[...])
        hu = _dot(xb, wu_ref[...])
        hg_ref[...] = hg.astype(MM)
        hu_ref[...] = hu.astype(MM)
        act = (hg * _sigmoid(hg) * hu).astype(MM)
        acc_ref[...] += _dot(act, wd_ref[...])

        @pl.when(j == nj - 1)
        def _():
            z = DN_ALPHA * x_ref[...] + 0.5 * acc_ref[...]
            z_ref[...] = z
            zc, rstd = _ln_stats(z)
            y_ref[...] = zc * rstd * g_ref[...] + b_ref[...]

    row = pl.BlockSpec((tm, d), lambda i, j: (i, 0))
    hid = pl.BlockSpec((tm, tf), lambda i, j: (i, j))
    vec = pl.BlockSpec((1, d), lambda i, j: (0, 0))
    return _call(
        body, (x, wg, wu, wd, g.reshape(1, d), b.reshape(1, d)), side, (t // tm, nj), name=name,
        in_specs=[row, pl.BlockSpec((d, tf), lambda i, j: (0, j)), pl.BlockSpec((d, tf), lambda i, j: (0, j)),
                  pl.BlockSpec((tf, d), lambda i, j: (j, 0)), vec, vec],
        out_specs=[row, row, hid, hid],
        out_shape=[jax.ShapeDtypeStruct((t, d), F32), jax.ShapeDtypeStruct((t, d), F32),
                   jax.ShapeDtypeStruct((t, f), MM), jax.ShapeDtypeStruct((t, f), MM)],
        scratch_shapes=[pltpu.VMEM((tm, d), MM), pltpu.VMEM((tm, d), F32)],
        compiler_params=_cp("parallel", "arbitrary"),
    )


def _ln_bwd_tile(dy, z, g):
    zc, rstd = _ln_stats(z)
    xh = zc * rstd
    dxh = dy * g
    m1 = jnp.mean(dxh, axis=-1, keepdims=True)
    m2 = jnp.mean(dxh * xh, axis=-1, keepdims=True)
    return rstd * (dxh - m1 - xh * m2), jnp.sum(dy * xh, axis=0, keepdims=True), jnp.sum(dy, axis=0, keepdims=True)


def ffn_bwd(dz, hg, hu, wg, wu, wd, *, name, tm=512, tf=256, side=None, ln=None):
    t, d = dz.shape
    f = wg.shape[1]
    tm = min(tm, t)
    nj = f // tf
    n_in = 6 + (2 if ln else 0)

    def body(*refs):
        dz_ref, hg_ref, hu_ref, wg_ref, wu_ref, wd_ref = refs[:6]
        dx_ref, act_ref, dhg_ref, dhu_ref = refs[n_in:n_in + 4]
        dfb_ref, acc_ref = refs[-2:]
        i, j = pl.program_id(0), pl.program_id(1)

        @pl.when(j == 0)
        def _():
            dfb_ref[...] = (0.5 * dz_ref[...]).astype(MM)
            acc_ref[...] = jnp.zeros_like(acc_ref)

        hg = hg_ref[...].astype(F32)
        hu = hu_ref[...].astype(F32)
        s = _sigmoid(hg)
        dact = _dot(dfb_ref[...], wd_ref[...], NT)
        sg = hg * s
        act_ref[...] = (sg * hu).astype(MM)
        dhu = (dact * sg).astype(MM)
        dhg = (dact * hu * (s + sg * (1.0 - s))).astype(MM)
        dhu_ref[...] = dhu
        dhg_ref[...] = dhg
        acc_ref[...] += _dot(dhg, wg_ref[...], NT) + _dot(dhu, wu_ref[...], NT)

        if ln:
            z_ref, g_ref = refs[6:8]
            dg_ref, db_ref = refs[n_in + 4:n_in + 6]

            @pl.when((i == 0) & (j == 0))
            def _():
                dg_ref[...] = jnp.zeros_like(dg_ref)
                db_ref[...] = jnp.zeros_like(db_ref)

        @pl.when(j == nj - 1)
        def _():
            dx = DN_ALPHA * dz_ref[...] + acc_ref[...]
            if ln:
                dx, dg, db = _ln_bwd_tile(dx, z_ref[...], g_ref[...])
                dg_ref[...] += dg
                db_ref[...] += db
            dx_ref[...] = dx

    row = pl.BlockSpec((tm, d), lambda i, j: (i, 0))
    hid = pl.BlockSpec((tm, tf), lambda i, j: (i, j))
    vec = pl.BlockSpec((1, d), lambda i, j: (0, 0))
    vshape = jax.ShapeDtypeStruct((1, d), F32)
    return _call(
        body, (dz, hg, hu, wg, wu, wd) + ((ln[0], ln[1].reshape(1, d)) if ln else ()), side, (t // tm, nj), name=name,
        in_specs=[row, hid, hid, pl.BlockSpec((d, tf), lambda i, j: (0, j)), pl.BlockSpec((d, tf), lambda i, j: (0, j)),
                  pl.BlockSpec((tf, d), lambda i, j: (j, 0))] + ([row, vec] if ln else []),
        out_specs=[row, hid, hid, hid] + ([vec, vec] if ln else []),
        out_shape=[jax.ShapeDtypeStruct((t, d), F32)] + [jax.ShapeDtypeStruct((t, f), MM)] * 3 + ([vshape, vshape] if ln else []),
        scratch_shapes=[pltpu.VMEM((tm, d), MM), pltpu.VMEM((tm, d), F32)],
        compiler_params=_cp("arbitrary" if ln else "parallel", "arbitrary"),
    )


def ln_bwd(dy, z, g, *, name, tm=512):
    t, d = z.shape
    tm = min(tm, t)

    def body(dy_ref, z_ref, g_ref, dz_ref, dg_ref, db_ref):
        i = pl.program_id(0)

        @pl.when(i == 0)
        def _():
            dg_ref[...] = jnp.zeros_like(dg_ref)
            db_ref[...] = jnp.zeros_like(db_ref)

        dy = dy_ref[...]
        zc, rstd = _ln_stats(z_ref[...])
        xh = zc * rstd
        dg_ref[...] += jnp.sum(dy * xh, axis=0, keepdims=True)
        db_ref[...] += jnp.sum(dy, axis=0, keepdims=True)
        dxh = dy * g_ref[...]
        m1 = jnp.mean(dxh, axis=-1, keepdims=True)
        m2 = jnp.mean(dxh * xh, axis=-1, keepdims=True)
        dz_ref[...] = rstd * (dxh - m1 - xh * m2)

    row = pl.BlockSpec((tm, d), lambda i: (i, 0))
    vec = pl.BlockSpec((1, d), lambda i: (0, 0))
    return _pcall(
        body, name=name, grid=(t // tm,), in_specs=[row, row, vec], out_specs=[row, vec, vec],
        out_shape=[jax.ShapeDtypeStruct((t, d), F32), jax.ShapeDtypeStruct((1, d), F32), jax.ShapeDtypeStruct((1, d), F32)],
        compiler_params=_cp("arbitrary"),
    )(dy, z, g.reshape(1, d))


def mm_ln_fwd(a, w, res, g, b, *, name, tm=512):
    t, kk = a.shape
    d = w.shape[1]
    tm = min(tm, t)

    def body(a_ref, w_ref, res_ref, g_ref, b_ref, y_ref, z_ref):
        z = DN_ALPHA * res_ref[...] + _mdot(a_ref[...], w_ref[...])
        z_ref[...] = z
        zc, rstd = _ln_stats(z)
        y_ref[...] = zc * rstd * g_ref[...] + b_ref[...]

    row = pl.BlockSpec((tm, d), lambda i: (i, 0))
    vec = pl.BlockSpec((1, d), lambda i: (0, 0))
    return _pcall(
        body, name=name, grid=(t // tm,),
        in_specs=[pl.BlockSpec((tm, kk), lambda i: (i, 0)), pl.BlockSpec((kk, d), lambda i: (0, 0)), row, vec, vec],
        out_specs=[row, row],
        out_shape=[jax.ShapeDtypeStruct((t, d), F32), jax.ShapeDtypeStruct((t, d), F32)],
        compiler_params=_cp("parallel"),
    )(a, w, res, g.reshape(1, d), b.reshape(1, d))


def ple_fwd(y, p, wg, bg, wp, *, name, tm=512):
    t, d = y.shape
    dp = p.shape[1]
    tm = min(tm, t)

    def body(y_ref, p_ref, wg_ref, bg_ref, wp_ref, o_ref):
        yv = y_ref[...]
        gate = _sigmoid(_mdot(yv, wg_ref[...]) + bg_ref[...])
        o_ref[...] = yv + gate * _mdot(p_ref[...], wp_ref[...])

    row = pl.BlockSpec((tm, d), lambda i: (i, 0))
    return _pcall(
        body, name=name, grid=(t // tm,),
        in_specs=[row, pl.BlockSpec((tm, dp), lambda i: (i, 0)), pl.BlockSpec((d, d), lambda i: (0, 0)),
                  pl.BlockSpec((1, d), lambda i: (0, 0)), pl.BlockSpec((dp, d), lambda i: (0, 0))],
        out_specs=row, out_shape=jax.ShapeDtypeStruct((t, d), F32), compiler_params=_cp("parallel"),
    )(y, p, wg, bg.reshape(1, d), wp)


def ple_bwd(do, y, p, wg, bg, wp, z, g, *, name, tm=512):
    t, d = y.shape
    dp = p.shape[1]
    tm = min(tm, t)

    def body(do_ref, y_ref, p_ref, wg_ref, bg_ref, wp_ref, z_ref, g_ref, dz_ref, dt_ref, de_ref, dbg_ref, dg_ref, db_ref):
        i = pl.program_id(0)

        @pl.when(i == 0)
        def _():
            for ref in (dbg_ref, dg_ref, db_ref):
                ref[...] = jnp.zeros_like(ref)

        dov = do_ref[...]
        gate = _sigmoid(_mdot(y_ref[...], wg_ref[...]) + bg_ref[...])
        emb = _mdot(p_ref[...], wp_ref[...])
        dt = dov * emb * gate * (1.0 - gate)
        dbg_ref[...] += jnp.sum(dt, axis=0, keepdims=True)
        dtb = dt.astype(MM)
        dt_ref[...] = dtb
        de_ref[...] = (dov * gate).astype(MM)
        dz, dg, db = _ln_bwd_tile(dov + _dot(dtb, wg_ref[...], NT), z_ref[...], g_ref[...])
        dz_ref[...] = dz
        dg_ref[...] += dg
        db_ref[...] += db

    row = pl.BlockSpec((tm, d), lambda i: (i, 0))
    vec = pl.BlockSpec((1, d), lambda i: (0, 0))
    vshape = jax.ShapeDtypeStruct((1, d), F32)
    return _pcall(
        body, name=name, grid=(t // tm,),
        in_specs=[row, row, pl.BlockSpec((tm, dp), lambda i: (i, 0)), pl.BlockSpec((d, d), lambda i: (0, 0)),
                  vec, pl.BlockSpec((dp, d), lambda i: (0, 0)), row, vec],
        out_specs=[row, row, row, vec, vec, vec],
        out_shape=[jax.ShapeDtypeStruct((t, d), F32), jax.ShapeDtypeStruct((t, d), MM),
                   jax.ShapeDtypeStruct((t, d), MM), vshape, vshape, vshape],
        compiler_params=_cp("arbitrary"),
    )(do, y, p, wg, bg.reshape(1, d), wp, z, g.reshape(1, d))


def loss_fwd_bwd(y, tgt, *, name, tm=512):
    t, d = y.shape
    tm = min(tm, t)

    def body(y_ref, t_ref, l_ref, dy_ref):
        i = pl.program_id(0)

        @pl.when(i == 0)
        def _():
            l_ref[...] = jnp.zeros_like(l_ref)

        err = y_ref[...] - t_ref[...]
        dy_ref[...] = err * (1.0 / d)
        l_ref[...] += (0.5 / d) * jnp.sum(jnp.sum(err * err, axis=1, keepdims=True), axis=0, keepdims=True)

    row = pl.BlockSpec((tm, d), lambda i: (i, 0))
    return _pcall(
        body, name=name, grid=(t // tm,), in_specs=[row, row],
        out_specs=[pl.BlockSpec((1, 128), lambda i: (0, 0)), row],
        out_shape=[jax.ShapeDtypeStruct((1, 128), F32), jax.ShapeDtypeStruct((t, d), F32)],
        compiler_params=_cp("arbitrary"),
    )(y, tgt)


def _shift_dn(x, s, row):
    return x if s == 0 else jnp.where(row >= s, pltpu.roll(x, s, 0), 0.0)


def _shift_up(x, s, row):
    n = x.shape[0]
    return x if s == 0 else jnp.where(row < n - s, pltpu.roll(x, n - s, 0), 0.0)


def _conv_fwd(x, w, row):
    kk = w.shape[0]
    y = w[kk - 1:kk, :] * x
    for j in range(kk - 1):
        y = y + w[j:j + 1, :] * _shift_dn(x, kk - 1 - j, row)
    return y


def _conv_bwd(x, w, dy, row):
    kk = w.shape[0]
    dx = w[kk - 1:kk, :] * dy
    dws = []
    for j in range(kk - 1):
        dx = dx + w[j:j + 1, :] * _shift_up(dy, kk - 1 - j, row)
        dws.append(jnp.sum(dy * _shift_dn(x, kk - 1 - j, row), axis=0, keepdims=True))
    dws.append(jnp.sum(dy * x, axis=0, keepdims=True))
    return dx, jnp.concatenate(dws, axis=0)


def _gelu(x):
    c = math.sqrt(2.0 / math.pi)
    th = jnp.tanh(c * (x + 0.044715 * x * x * x))
    return 0.5 * x * (1.0 + th), th


def _gelu_grad(x, th):
    c = math.sqrt(2.0 / math.pi)
    return 0.5 * (1.0 + th) + 0.5 * x * (1.0 - th * th) * c * (1.0 + 3.0 * 0.044715 * x * x)


def _neg_expm1(y):
    ser = -(y * (1.0 + y * (0.5 + y * (1.0 / 6.0 + y * (1.0 / 24.0 + y * (1.0 / 120.0))))))
    return jnp.where(y > -0.05, ser, 1.0 - jnp.exp(y))


def _attn_head(qh, kk, vv, bias, valid, sink):
    s = _mdot(qh, kk, NT) * (A_HEAD_DIM ** -0.5) - bias
    s = jnp.where(valid, s, NEG)
    m = jnp.maximum(jnp.max(s, axis=-1, keepdims=True), sink)
    pr = jnp.exp(s - m)
    den = jnp.sum(pr, axis=-1, keepdims=True) + jnp.exp(sink - m)
    return pr / den, jnp.exp(sink - m) / den


def _attn_valid(n):
    ji = lax.broadcasted_iota(jnp.int32, (1, 3 * CHUNK), 1)
    return (n * CHUNK + ji - 2 * CHUNK) >= 0


def _attn_group_consts(kh, sk_ref):
    rows = A_GROUP * CHUNK
    ri = lax.broadcasted_iota(jnp.int32, (rows, 3 * CHUNK), 0)
    ji = lax.broadcasted_iota(jnp.int32, (rows, 3 * CHUNK), 1)
    dist = jnp.abs((ri & (CHUNK - 1)) + 2 * CHUNK - ji).astype(F32)
    rcol = lax.broadcasted_iota(jnp.int32, (rows, 1), 0)
    slope = jnp.zeros((rows, 1), F32)
    sink = jnp.zeros((rows, 1), F32)
    for gi in range(A_GROUP):
        h = kh * A_GROUP + gi
        inblk = (rcol >= gi * CHUNK) & (rcol < (gi + 1) * CHUNK)
        slope = jnp.where(inblk, 2.0 ** -(h + 1), slope)
        sink = jnp.where(inblk, sk_ref[h], sink)
    return slope * dist, sink


def _stack_heads(x, kh):
    return jnp.concatenate([x[:, (kh * A_GROUP + gi) * 64:(kh * A_GROUP + gi + 1) * 64] for gi in range(A_GROUP)], axis=0)


def _attn_masks(n):
    ci = lax.broadcasted_iota(jnp.int32, (CHUNK, 3 * CHUNK), 0)
    ji = lax.broadcasted_iota(jnp.int32, (CHUNK, 3 * CHUNK), 1)
    dist = jnp.abs(ci + 2 * CHUNK - ji).astype(F32)
    valid = (n * CHUNK + ji - 2 * CHUNK) >= 0
    return dist, valid


def attn_fwd(proj, sinks, bsz, *, name, side=None):
    t = proj.shape[0]
    s_len = t // bsz
    nc = s_len // CHUNK
    pad = 2 * CHUNK

    def body(q_ref, k_ref, v_ref, sk_ref, o_ref, kp_ref, vp_ref):
        kp_ref[0:pad, :] = jnp.zeros((pad, A_KV_WIDTH), F32)
        vp_ref[0:pad, :] = jnp.zeros((pad, A_KV_WIDTH), F32)
        kp_ref[pad:, :] = k_ref[...]
        vp_ref[pad:, :] = v_ref[...]

        consts = [_attn_group_consts(kh, sk_ref) for kh in range(A_KV_HEADS)]

        def chunk(n, carry):
            st = pl.multiple_of(n * CHUNK, CHUNK)
            q = q_ref[pl.ds(st, CHUNK), :]
            kb = kp_ref[pl.ds(st, 3 * CHUNK), :]
            vb = vp_ref[pl.ds(st, 3 * CHUNK), :]
            valid = _attn_valid(n)
            outs = []
            for kh in range(A_KV_HEADS):
                bias, sink = consts[kh]
                pn, _ = _attn_head(_stack_heads(q, kh), kb[:, kh * 64:(kh + 1) * 64], None, bias, valid, sink)
                o = _mdot(pn, vb[:, kh * 64:(kh + 1) * 64])
                outs += [o[gi * CHUNK:(gi + 1) * CHUNK] for gi in range(A_GROUP)]
            o_ref[pl.ds(st, CHUNK), :] = jnp.concatenate(outs, axis=-1)
            return carry

        lax.fori_loop(0, nc, chunk, 0)

    res = _call(
        body, (proj, proj, proj, sinks), side, (bsz,), name=name,
        in_specs=[pl.BlockSpec((s_len, A_WIDTH), lambda b: (b, 0)), pl.BlockSpec((s_len, 128), lambda b: (b, 4)),
                  pl.BlockSpec((s_len, 128), lambda b: (b, 5)), pl.BlockSpec(memory_space=pltpu.SMEM)],
        out_specs=[pl.BlockSpec((s_len, A_WIDTH), lambda b: (b, 0))],
        out_shape=[jax.ShapeDtypeStruct((t, A_WIDTH), F32)],
        scratch_shapes=[pltpu.VMEM((s_len + pad, A_KV_WIDTH), F32), pltpu.VMEM((s_len + pad, A_KV_WIDTH), F32)],
        compiler_params=_cp("parallel"),
    )
    return res[0] if side is None else (res[0][0], res[1])


def attn_bwd(proj, sinks, dcat, bsz, *, name, side=None):
    t = proj.shape[0]
    s_len = t // bsz
    nc = s_len // CHUNK
    pad = 2 * CHUNK

    def body(q_ref, k_ref, v_ref, do_ref, sk_ref, dq_ref, dk_ref, dv_ref, dsk_ref, kp_ref, vp_ref, dkp_ref, dvp_ref):
        kp_ref[0:pad, :] = jnp.zeros((pad, A_KV_WIDTH), F32)
        vp_ref[0:pad, :] = jnp.zeros((pad, A_KV_WIDTH), F32)
        kp_ref[pad:, :] = k_ref[...]
        vp_ref[pad:, :] = v_ref[...]
        dkp_ref[...] = jnp.zeros_like(dkp_ref)
        dvp_ref[...] = jnp.zeros_like(dvp_ref)
        lane = lax.broadcasted_iota(jnp.int32, (1, 128), 1)

        consts = [_attn_group_consts(kh, sk_ref) for kh in range(A_KV_HEADS)]

        def chunk(n, dsk):
            st = pl.multiple_of(n * CHUNK, CHUNK)
            q = q_ref[pl.ds(st, CHUNK), :]
            do = do_ref[pl.ds(st, CHUNK), :]
            kb = kp_ref[pl.ds(st, 3 * CHUNK), :]
            vb = vp_ref[pl.ds(st, 3 * CHUNK), :]
            valid = _attn_valid(n)
            dqs, dks, dvs = [], [], []
            for kh in range(A_KV_HEADS):
                kk = kb[:, kh * 64:(kh + 1) * 64]
                vv = vb[:, kh * 64:(kh + 1) * 64]
                bias, sink = consts[kh]
                qs = _stack_heads(q, kh)
                dos = _stack_heads(do, kh)
                pn, psink = _attn_head(qs, kk, None, bias, valid, sink)
                dp = _mdot(dos, vv, NT)
                rowdot = jnp.sum(pn * dp, axis=-1, keepdims=True)
                ds = pn * (dp - rowdot)
                sink_part = psink * rowdot
                for gi in range(A_GROUP):
                    part = jnp.sum(sink_part[gi * CHUNK:(gi + 1) * CHUNK], axis=0, keepdims=True)
                    dsk = dsk + jnp.where(lane == kh * A_GROUP + gi, -part, 0.0)
                dq = _mdot(ds, kk) * (A_HEAD_DIM ** -0.5)
                dqs += [dq[gi * CHUNK:(gi + 1) * CHUNK] for gi in range(A_GROUP)]
                dks.append(_mdot(ds, qs, TN) * (A_HEAD_DIM ** -0.5))
                dvs.append(_mdot(pn, dos, TN))
            dq_ref[pl.ds(st, CHUNK), :] = jnp.concatenate(dqs, axis=-1)
            dkp_ref[pl.ds(st, 3 * CHUNK), :] += jnp.concatenate(dks, axis=-1)
            dvp_ref[pl.ds(st, 3 * CHUNK), :] += jnp.concatenate(dvs, axis=-1)
            return dsk

        dsk = lax.fori_loop(0, nc, chunk, jnp.zeros((1, 128), F32))
        dsk_ref[0] = dsk
        dk_ref[...] = dkp_ref[pad:, :]
        dv_ref[...] = dvp_ref[pad:, :]

    kv = jax.ShapeDtypeStruct((t, A_KV_WIDTH), F32)
    return _call(
        body, (proj, proj, proj, dcat, sinks), side, (bsz,), name=name,
        in_specs=[pl.BlockSpec((s_len, A_WIDTH), lambda b: (b, 0)), pl.BlockSpec((s_len, 128), lambda b: (b, 4)),
                  pl.BlockSpec((s_len, 128), lambda b: (b, 5)), pl.BlockSpec((s_len, A_WIDTH), lambda b: (b, 0)),
                  pl.BlockSpec(memory_space=pltpu.SMEM)],
        out_specs=[pl.BlockSpec((s_len, A_WIDTH), lambda b: (b, 0)), pl.BlockSpec((s_len, 128), lambda b: (b, 0)),
                   pl.BlockSpec((s_len, 128), lambda b: (b, 0)), pl.BlockSpec((1, 1, 128), lambda b: (b, 0, 0))],
        out_shape=[jax.ShapeDtypeStruct((t, A_WIDTH), F32), kv, kv, jax.ShapeDtypeStruct((bsz, 1, 128), F32)],
        scratch_shapes=[pltpu.VMEM((s_len + pad, A_KV_WIDTH), F32)] * 4,
        compiler_params=_cp("parallel"),
    )


def _lru_gates(x, cw, cb, wa, ba, wx, bx, lam, row):
    xc = _conv_fwd(x, cw, row) + cb
    r = _sigmoid(_mdot(xc, wa) + ba)
    i = _sigmoid(_mdot(xc, wx) + bx)
    sp = _softplus(-lam)
    log_a = -RG_C * r * sp
    a = jnp.exp(log_a)
    mult = jnp.sqrt(_neg_expm1(2.0 * log_a))
    return xc, r, i, sp, a, mult


def _lru_scan(a, u, row):
    n = a.shape[0]
    d = 1
    while d < n:
        a_sh = jnp.where(row >= d, pltpu.roll(a, d, 0), 1.0)
        u_sh = jnp.where(row >= d, pltpu.roll(u, d, 0), 0.0)
        u = a * u_sh + u
        a = a * a_sh
        d *= 2
    return u


def _lru_scan_rev(a, u, row):
    n = a.shape[0]
    d = 1
    while d < n:
        a_sh = jnp.where(row < n - d, pltpu.roll(a, n - d, 0), 1.0)
        u_sh = jnp.where(row < n - d, pltpu.roll(u, n - d, 0), 0.0)
        u = a * u_sh + u
        a = a * a_sh
        d *= 2
    return u


LRU_BLOCK = 128


def _lru_scan_refs(a_ref, u_ref, h_ref, reverse=False):
    nb = a_ref.shape[0] // LRU_BLOCK
    row = lax.broadcasted_iota(jnp.int32, (LRU_BLOCK, 128), 0)

    def block(i, carry):
        bi = nb - 1 - i if reverse else i
        rs = pl.ds(pl.multiple_of(bi * LRU_BLOCK, LRU_BLOCK), LRU_BLOCK)
        a, u = a_ref[rs, :], u_ref[rs, :]
        d = 1
        while d < LRU_BLOCK:
            keep = row < LRU_BLOCK - d if reverse else row >= d
            sh = LRU_BLOCK - d if reverse else d
            a_sh = jnp.where(keep, pltpu.roll(a, sh, 0), 1.0)
            u_sh = jnp.where(keep, pltpu.roll(u, sh, 0), 0.0)
            u = a * u_sh + u
            a = a * a_sh
            d *= 2
        h = u + a * carry
        h_ref[rs, :] = h
        return h[0:1, :] if reverse else h[LRU_BLOCK - 1:LRU_BLOCK, :]

    lax.fori_loop(0, nb, block, jnp.zeros((1, 128), F32))


def _lru_specs(s_len, order):
    def at(f):
        return lambda *g: f(*order(*g))
    return [pl.BlockSpec((s_len, 128), at(lambda b, cb: (b, 6 + cb))), pl.BlockSpec((s_len, 128), at(lambda b, cb: (b, 10 + cb))),
            pl.BlockSpec((4, 128), at(lambda b, cb: (0, cb))), pl.BlockSpec((1, 128), at(lambda b, cb: (0, cb))),
            pl.BlockSpec((1, 128, 128), at(lambda b, cb: (cb, 0, 0))), pl.BlockSpec((1, 128), at(lambda b, cb: (0, cb))),
            pl.BlockSpec((1, 128, 128), at(lambda b, cb: (cb, 0, 0))), pl.BlockSpec((1, 128), at(lambda b, cb: (0, cb))),
            pl.BlockSpec((1, 128), at(lambda b, cb: (0, cb)))]


def lru_fwd(proj, cw, cb, wa, ba, wx, bxb, lam, bsz, *, name):
    t = proj.shape[0]
    s_len = t // bsz

    def body(x_ref, g_ref, cw_ref, cb_ref, wa_ref, ba_ref, wx_ref, bx_ref, lam_ref, y_ref, a_s, u_s):
        row = lax.broadcasted_iota(jnp.int32, (s_len, 128), 0)
        xc, r, i, sp, a, mult = _lru_gates(x_ref[...], cw_ref[...], cb_ref[...], wa_ref[0], ba_ref[...], wx_ref[0],
                                           bx_ref[...], lam_ref[...], row)
        a_s[...] = a
        u_s[...] = mult * (i * xc)
        _lru_scan_refs(a_s, u_s, y_ref)
        y_ref[...] = y_ref[...] * _gelu(g_ref[...])[0]

    return _pcall(
        body, name=name, grid=(bsz, 4), in_specs=_lru_specs(s_len, lambda b, cb: (b, cb)),
        out_specs=pl.BlockSpec((s_len, 128), lambda b, cb: (b, cb)),
        out_shape=jax.ShapeDtypeStruct((t, B_WIDTH), F32), scratch_shapes=[pltpu.VMEM((s_len, 128), F32)] * 2,
        compiler_params=_cp("parallel", "parallel"),
    )(proj, proj, cw, cb.reshape(1, -1), wa, ba.reshape(1, -1), wx, bxb.reshape(1, -1), lam.reshape(1, -1))


def lru_bwd(proj, cw, cb, wa, ba, wx, bxb, lam, dcat, bsz, *, name, side=None):
    t = proj.shape[0]
    s_len = t // bsz

    def body(x_ref, g_ref, cw_ref, cb_ref, wa_ref, ba_ref, wx_ref, bx_ref, lam_ref, dy_ref,
             dx_ref, dg_ref, dcw_ref, dcb_ref, dwa_ref, dba_ref, dwx_ref, dbx_ref, dlam_ref, a_s, u_s, h_s, g_s):
        b = pl.program_id(1)
        row = lax.broadcasted_iota(jnp.int32, (s_len, 128), 0)
        x = x_ref[...]
        lam = lam_ref[...]
        xc, r, i, sp, a, mult = _lru_gates(x, cw_ref[...], cb_ref[...], wa_ref[0], ba_ref[...], wx_ref[0], bx_ref[...],
                                           lam, row)
        ixc = i * xc
        a_s[...] = a
        u_s[...] = mult * ixc
        _lru_scan_refs(a_s, u_s, h_s)
        h = h_s[...]
        gv = g_ref[...]
        gl, th = _gelu(gv)
        dy = dy_ref[...]
        dg_ref[...] = dy * h * _gelu_grad(gv, th)
        a_s[...] = _shift_up(a, 1, row)
        u_s[...] = dy * gl
        _lru_scan_refs(a_s, u_s, g_s, reverse=True)
        gr = g_s[...]
        da = gr * _shift_dn(h, 1, row)
        dmult = gr * ixc
        di = gr * mult * xc
        dxc = gr * mult * i
        dlog_a = da * a - dmult * (a * a) / mult
        dr = dlog_a * (-RG_C * sp)
        dlam = jnp.sum(dlog_a * r, axis=0, keepdims=True) * (RG_C * _sigmoid(-lam))
        dpa = dr * r * (1.0 - r)
        dpx = di * i * (1.0 - i)
        dxc = dxc + _mdot(dpa, wa_ref[0], NT) + _mdot(dpx, wx_ref[0], NT)
        dx, dcw = _conv_bwd(x, cw_ref[...], dxc, row)
        dx_ref[...] = dx

        @pl.when(b == 0)
        def _():
            for ref in (dcw_ref, dcb_ref, dwa_ref, dba_ref, dwx_ref, dbx_ref, dlam_ref):
                ref[...] = jnp.zeros_like(ref)

        dcw_ref[...] += dcw
        dcb_ref[...] += jnp.sum(dxc, axis=0, keepdims=True)
        dwa_ref[0] += _mdot(xc, dpa, TN)
        dwx_ref[0] += _mdot(xc, dpx, TN)
        dba_ref[...] += jnp.sum(dpa, axis=0, keepdims=True)
        dbx_ref[...] += jnp.sum(dpx, axis=0, keepdims=True)
        dlam_ref[...] += dlam

    order = lambda cb, b: (b, cb)
    act = pl.BlockSpec((s_len, 128), lambda cb, b: (b, cb))
    vec = pl.BlockSpec((1, 128), lambda cb, b: (0, cb))
    mat = pl.BlockSpec((1, 128, 128), lambda cb, b: (cb, 0, 0))
    vshape = jax.ShapeDtypeStruct((1, B_WIDTH), F32)
    mshape = jax.ShapeDtypeStruct((4, 128, 128), F32)
    return _call(
        body, (proj, proj, cw, cb.reshape(1, -1), wa, ba.reshape(1, -1), wx, bxb.reshape(1, -1), lam.reshape(1, -1), dcat),
        side, (4, bsz), name=name,
        in_specs=_lru_specs(s_len, order) + [pl.BlockSpec((s_len, 128), lambda cb, b: (b, 4 + cb))],
        out_specs=[act, act, pl.BlockSpec((4, 128), lambda cb, b: (0, cb)), vec, mat, vec, mat, vec, vec],
        out_shape=[jax.ShapeDtypeStruct((t, B_WIDTH), F32), jax.ShapeDtypeStruct((t, B_WIDTH), F32),
                   jax.ShapeDtypeStruct((4, B_WIDTH), F32), vshape, mshape, vshape, mshape, vshape, vshape],
        scratch_shapes=[pltpu.VMEM((s_len, 128), F32)] * 4,
        compiler_params=_cp("parallel", "arbitrary"),
    )


_BDIMS = {"nn": ((2,), (1,)), "nt": ((2,), (2,)), "tn": ((1,), (1,))}
C_QSCALE = C_HEAD_DIM ** -0.5


def _bmm(a, b, mode, exact=False):
    dims = (_BDIMS[mode], ((0,), (0,)))
    if exact:
        return lax.dot_general(a, b, dims, preferred_element_type=F32, precision=lax.Precision.HIGH)
    return lax.dot_general(a.astype(MM), b.astype(MM), dims, preferred_element_type=F32)


def _col(x, idx, lane):
    return jnp.broadcast_to(jnp.sum(jnp.where(lane == idx, x, 0.0), axis=-1, keepdims=True), x.shape)


def _seg_cumsum(g, row):
    pos = row & (CHUNK - 1)
    d = 1
    while d < CHUNK:
        g = g + jnp.where(pos >= d, pltpu.roll(g, d, 0), 0.0)
        d *= 2
    return g


def _seg_cumsum_rev(g, row):
    pos = row & (CHUNK - 1)
    n = g.shape[0]
    d = 1
    while d < CHUNK:
        g = g + jnp.where(pos < CHUNK - d, pltpu.roll(g, n - d, 0), 0.0)
        d *= 2
    return g


def _gdn_prep(qr, kr, vr, gates, cwq, cwk, cwv, a_log, dtb, h):
    s_len = qr.shape[0]
    nc = s_len // CHUNK
    row = lax.broadcasted_iota(jnp.int32, (s_len, 128), 0)
    lane = lax.broadcasted_iota(jnp.int32, (s_len, 128), 1)
    r = {"row": row, "lane": lane}
    for nm, x, w in (("q", qr, cwq), ("k", kr, cwk), ("v", vr, cwv)):
        c = _conv_fwd(x, w, row)
        sg = _sigmoid(c)
        r["c" + nm], r["s" + nm], r[nm + "c"] = c, sg, c * sg
    r["rq"] = lax.rsqrt(jnp.sum(r["qc"] * r["qc"], axis=-1, keepdims=True) + NORM_EPS)
    r["rk"] = lax.rsqrt(jnp.sum(r["kc"] * r["kc"], axis=-1, keepdims=True) + NORM_EPS)
    r["qn"] = r["qc"] * r["rq"]
    r["kn"] = r["kc"] * r["rk"]
    r["beta"] = _sigmoid(_col(gates, h, lane))
    r["A"] = jnp.exp(a_log)
    r["pre"] = _col(gates, 8 + h, lane) + dtb
    r["sp"] = _softplus(r["pre"])
    gc = _seg_cumsum(-r["A"] * r["sp"], row)
    sh = (nc, CHUNK, 128)
    q3 = (r["qn"] * C_QSCALE).reshape(sh)
    k3 = r["kn"].reshape(sh)
    v3 = r["vc"].reshape(sh)
    beta3 = r["beta"].reshape(sh)
    gc3 = gc.reshape(sh)
    gcl3 = gc3[:, CHUNK - 1:CHUNK, :]
    eg = jnp.exp(gc3)
    ekd = jnp.exp(gcl3 - gc3)
    col64 = gc3[:, :, :CHUNK]
    row64 = jnp.swapaxes(gc3, 1, 2)[:, :CHUNK, :]
    ii = lax.broadcasted_iota(jnp.int32, (nc, CHUNK, CHUNK), 1)
    jj = lax.broadcasted_iota(jnp.int32, (nc, CHUNK, CHUNK), 2)
    tril = ii >= jj
    strict = ii > jj
    dm = jnp.where(tril, jnp.exp(jnp.where(tril, col64 - row64, 0.0)), 0.0)
    kb = k3 * beta3
    lmat = jnp.where(strict, _bmm(kb, k3, "nt") * dm, 0.0)
    attn = _bmm(q3, k3, "nt") * dm
    r.update(q3=q3, k3=k3, v3=v3, beta3=beta3, gc3=gc3, eg=eg, ekd=ekd, gl=jnp.exp(gcl3), dm=dm, kb=kb, lmat=lmat,
             attn=attn, strict=strict, tril=tril, qg=q3 * eg, kdec=k3 * ekd)
    return r


def _neumann_inverse(lmat):
    ii = lax.broadcasted_iota(jnp.int32, lmat.shape, 1)
    jj = lax.broadcasted_iota(jnp.int32, lmat.shape, 2)
    x = -lmat
    tm = jnp.where(ii == jj, 1.0, 0.0) + x
    pw = x
    for _ in range(5):
        pw = _bmm(pw, pw, "nn", exact=True)
        tm = tm + _bmm(tm, pw, "nn", exact=True)
    return tm


def _gdn_specs(s_len):
    act = lambda off: pl.BlockSpec((s_len, 128), lambda b, h: (b, off + h))
    cw = lambda off: pl.BlockSpec((4, 128), lambda b, h: (0, off + h))
    smem = pl.BlockSpec(memory_space=pltpu.SMEM)
    return [act(0), act(8), act(16), act(24), pl.BlockSpec((s_len, 128), lambda b, h: (b, 0)), cw(0), cw(8), cw(16),
            smem, smem, pl.BlockSpec((1, 128), lambda b, h: (0, 0))]


def gdn_fwd(proj, gates, cw, a_log, dtb, ng, bsz, *, name):
    t = proj.shape[0]
    s_len = t // bsz
    nc = s_len // CHUNK

    def body(q_ref, k_ref, v_ref, z_ref, gt_ref, cwq_ref, cwk_ref, cwv_ref, al_ref, dt_ref, ng_ref,
             out_ref, o_ref, vn_ref, tm_ref, st_ref, u_s, w_s, qg_s, kd_s, at_s, gl_s):
        h = pl.program_id(1)
        r = _gdn_prep(q_ref[...], k_ref[...], v_ref[...], gt_ref[...], cwq_ref[...], cwk_ref[...], cwv_ref[...],
                      al_ref[h], dt_ref[h], h)
        tm = _neumann_inverse(r["lmat"])
        tm_ref[0, 0] = tm
        u_s[...] = _bmm(tm, r["v3"] * r["beta3"], "nn", exact=True)
        w_s[...] = _bmm(tm, r["kb"] * r["eg"], "nn", exact=True)
        qg_s[...] = r["qg"]
        kd_s[...] = r["kdec"]
        at_s[...] = r["attn"]
        gl_s[...] = r["gl"]

        def chunk(n, state):
            st = pl.multiple_of(n * CHUNK, CHUNK)
            st_ref[0, 0, n] = state
            v_new = u_s[n] - _mdot(w_s[n], state)
            o_ref[pl.ds(st, CHUNK), :] = _mdot(qg_s[n], state) + _mdot(at_s[n], v_new)
            vn_ref[pl.ds(st, CHUNK), :] = v_new
            return state * gl_s[n] + _mdot(kd_s[n], v_new, TN)

        lax.fori_loop(0, nc, chunk, jnp.zeros((128, 128), F32))
        o = o_ref[...]
        rms = lax.rsqrt(jnp.mean(o * o, axis=-1, keepdims=True) + NORM_EPS)
        z = z_ref[...]
        out_ref[...] = o * rms * ng_ref[...] * (z * _sigmoid(z))

    blk = pl.BlockSpec((s_len, 128), lambda b, h: (b, h))
    full = jax.ShapeDtypeStruct((t, C_WIDTH), F32)
    return _pcall(
        body, name=name, grid=(bsz, C_HEADS), in_specs=_gdn_specs(s_len),
        out_specs=[blk, blk, blk, pl.BlockSpec((1, 1, nc, CHUNK, CHUNK), lambda b, h: (b, h, 0, 0, 0)),
                   pl.BlockSpec((1, 1, nc, 128, 128), lambda b, h: (b, h, 0, 0, 0))],
        out_shape=[full, full, full, jax.ShapeDtypeStruct((bsz, C_HEADS, nc, CHUNK, CHUNK), F32),
                   jax.ShapeDtypeStruct((bsz, C_HEADS, nc, 128, 128), F32)],
        scratch_shapes=[pltpu.VMEM((nc, CHUNK, 128), F32)] * 4 + [pltpu.VMEM((nc, CHUNK, CHUNK), F32),
                                                                   pltpu.VMEM((nc, 1, 128), F32)],
        compiler_params=_cp("parallel", "parallel"),
    )(proj, proj, proj, proj, gates, cw, cw, cw, a_log, dtb, ng.reshape(1, 128))


def gdn_bwd(proj, gates, cw, a_log, dtb, ng, o_pre, vnew, tmat, states, dout, bsz, *, name):
    t = proj.shape[0]
    s_len = t // bsz
    nc = s_len // CHUNK

    def body(q_ref, k_ref, v_ref, z_ref, gt_ref, cwq_ref, cwk_ref, cwv_ref, al_ref, dt_ref, ng_ref,
             o_ref, vn_ref, tm_ref, st_ref, do_ref,
             dq_ref, dk_ref, dv_ref, dz_ref, dgt_ref, dcq_ref, dck_ref, dcv_ref, dsm_ref,
             w_s, qg_s, kd_s, at_s, gl_s, dop_s, du_s, dw_s, dat_s, dqg_s, dkd_s, dgl_s):
        h = pl.program_id(1)
        qr, kr, vr = q_ref[...], k_ref[...], v_ref[...]
        r = _gdn_prep(qr, kr, vr, gt_ref[...], cwq_ref[...], cwk_ref[...], cwv_ref[...], al_ref[h], dt_ref[h], h)
        row, lane = r["row"], r["lane"]
        tm = tm_ref[0, 0]
        q3, k3, v3, beta3, eg, kb, dm = r["q3"], r["k3"], r["v3"], r["beta3"], r["eg"], r["kb"], r["dm"]
        u3 = _bmm(tm, v3 * beta3, "nn", exact=True)
        w3 = _bmm(tm, kb * eg, "nn", exact=True)
        w_s[...] = w3
        qg_s[...] = r["qg"]
        kd_s[...] = r["kdec"]
        at_s[...] = r["attn"]
        gl_s[...] = r["gl"]

        z = z_ref[...]
        sz = _sigmoid(z)
        o = o_ref[...]
        rms = lax.rsqrt(jnp.mean(o * o, axis=-1, keepdims=True) + NORM_EPS)
        on = o * rms
        dout_v = do_ref[...]
        ngv = ng_ref[...]
        dz_ref[...] = dout_v * on * ngv * (sz * (1.0 + z * (1.0 - sz)))
        dos = dout_v * (z * sz)
        dng = jnp.sum(dos * on, axis=0, keepdims=True)
        don = dos * ngv
        dop_s[...] = (rms * (don - on * jnp.mean(don * on, axis=-1, keepdims=True))).reshape(nc, CHUNK, 128)

        def chunk(i, dstate):
            n = nc - 1 - i
            st = pl.multiple_of(n * CHUNK, CHUNK)
            state = st_ref[0, 0, n]
            vn = vn_ref[pl.ds(st, CHUNK), :]
            do_n = dop_s[n]
            dvn = _mdot(at_s[n], do_n, TN) + _mdot(kd_s[n], dstate)
            du_s[n] = dvn
            dat_s[n] = _mdot(do_n, vn, NT)
            dqg_s[n] = _mdot(do_n, state, NT)
            dkd_s[n] = _mdot(vn, dstate, NT)
            dgl_s[n] = jnp.broadcast_to(jnp.sum(jnp.sum(state * dstate, axis=1, keepdims=True), axis=0, keepdims=True), (1, 128))
            dw_s[n] = -_mdot(dvn, state, NT)
            return dstate * gl_s[n] + _mdot(qg_s[n], do_n, TN) - _mdot(w_s[n], dvn, TN)

        lax.fori_loop(0, nc, chunk, jnp.zeros((128, 128), F32))

        du, dw, dqg, dkd = du_s[...], dw_s[...], dqg_s[...], dkd_s[...]
        dat = jnp.where(r["tril"], dat_s[...], 0.0)
        dvb = _bmm(tm, du, "tn", exact=True)
        dkbg = _bmm(tm, dw, "tn", exact=True)
        dl = -jnp.where(r["strict"], _bmm(dvb, u3, "nt") + _bmm(dkbg, w3, "nt"), 0.0)
        dml = dl * dm
        dn = dat * dm
        dkb = _bmm(dml, k3, "nn") + dkbg * eg
        dk3 = _bmm(dml, kb, "tn") + _bmm(dn, q3, "tn") + dkd * r["ekd"] + dkb * beta3
        dq3 = dqg * eg + _bmm(dn, k3, "nn")
        e = dl * r["lmat"] + dat * r["attn"]
        ones = jnp.ones((nc, CHUNK, 128), F32)
        colsum = lax.dot_general(e, ones, (_BDIMS["tn"], ((0,), (0,))), preferred_element_type=F32, precision=HI)
        dgc = jnp.sum(e, axis=-1, keepdims=True) - colsum
        dgc = dgc + eg * (jnp.sum(dqg * q3, axis=-1, keepdims=True) + jnp.sum(dkbg * kb, axis=-1, keepdims=True))
        skd = jnp.sum(dkd * r["kdec"], axis=-1, keepdims=True)
        dgcl = jnp.sum(skd, axis=1, keepdims=True) + dgl_s[...] * r["gl"]
        pos3 = lax.broadcasted_iota(jnp.int32, (nc, CHUNK, 128), 1)
        dgc = dgc - skd + jnp.where(pos3 == CHUNK - 1, dgcl, 0.0)
        dbeta = jnp.sum(dkb * k3, axis=-1, keepdims=True) + jnp.sum(dvb * v3, axis=-1, keepdims=True)
        dv3 = dvb * beta3

        dg = _seg_cumsum_rev(dgc.reshape(s_len, 128), row)
        beta = r["beta"]
        dbl = jnp.broadcast_to(dbeta, (nc, CHUNK, 128)).reshape(s_len, 128) * beta * (1.0 - beta)
        dai = dg * (-r["A"]) * _sigmoid(r["pre"])
        d_dtb = jnp.sum(dai, axis=0, keepdims=True)
        d_alog = jnp.sum(dg * (-r["sp"]), axis=0, keepdims=True) * r["A"]

        @pl.when(h == 0)
        def _():
            dgt_ref[...] = jnp.zeros_like(dgt_ref)
            dsm_ref[...] = jnp.zeros_like(dsm_ref)

        dgt_ref[...] += jnp.where(lane == h, dbl, 0.0) + jnp.where(lane == 8 + h, dai, 0.0)
        r16 = lax.broadcasted_iota(jnp.int32, (16, 128), 0)
        l16 = lax.broadcasted_iota(jnp.int32, (16, 128), 1)
        small = jnp.where((r16 == h) & (l16 == 0), d_alog, 0.0) + jnp.where((r16 == h) & (l16 == 1), d_dtb, 0.0)
        dsm_ref[0] += small + jnp.where(r16 == 8 + h, dng, 0.0)

        dqn = dq3.reshape(s_len, 128) * C_QSCALE
        dkn = dk3.reshape(s_len, 128)
        dqc = r["rq"] * (dqn - r["qn"] * jnp.sum(dqn * r["qn"], axis=-1, keepdims=True))
        dkc = r["rk"] * (dkn - r["kn"] * jnp.sum(dkn * r["kn"], axis=-1, keepdims=True))
        dvc = dv3.reshape(s_len, 128)
        for nm, x, w_ref, dxc, dx_ref, dc_ref in (("q", qr, cwq_ref, dqc, dq_ref, dcq_ref), ("k", kr, cwk_ref, dkc, dk_ref, dck_ref),
                                                 ("v", vr, cwv_ref, dvc, dv_ref, dcv_ref)):
            c, sg = r["c" + nm], r["s" + nm]
            dc = dxc * (sg * (1.0 + c * (1.0 - sg)))
            dx, dwc = _conv_bwd(x, w_ref[...], dc, row)
            dx_ref[...] = dx
            dc_ref[0] = dwc

    blk = pl.BlockSpec((s_len, 128), lambda b, h: (b, h))
    full = jax.ShapeDtypeStruct((t, C_WIDTH), F32)
    cwo = pl.BlockSpec((1, 4, 128), lambda b, h: (b, 0, h))
    cws = jax.ShapeDtypeStruct((bsz, 4, C_WIDTH), F32)
    c128 = pltpu.VMEM((nc, CHUNK, 128), F32)
    outs = _pcall(
        body, name=name, grid=(bsz, C_HEADS),
        in_specs=_gdn_specs(s_len) + [blk, blk, pl.BlockSpec((1, 1, nc, CHUNK, CHUNK), lambda b, h: (b, h, 0, 0, 0)),
                                      pl.BlockSpec((1, 1, nc, 128, 128), lambda b, h: (b, h, 0, 0, 0)), blk],
        out_specs=[blk, blk, blk, blk, pl.BlockSpec((s_len, 128), lambda b, h: (b, 0)), cwo, cwo, cwo,
                   pl.BlockSpec((1, 16, 128), lambda b, h: (b, 0, 0))],
        out_shape=[full, full, full, full, jax.ShapeDtypeStruct((t, 128), F32), cws, cws, cws,
                   jax.ShapeDtypeStruct((bsz, 16, 128), F32)],
        scratch_shapes=[c128, c128, c128, pltpu.VMEM((nc, CHUNK, CHUNK), F32), pltpu.VMEM((nc, 1, 128), F32), c128,
                        c128, c128, pltpu.VMEM((nc, CHUNK, CHUNK), F32), c128, c128, pltpu.VMEM((nc, 1, 128), F32)],
        compiler_params=_cp("parallel", "arbitrary"),
    )(proj, proj, proj, proj, gates, cw, cw, cw, a_log, dtb, ng.reshape(1, 128), o_pre, vnew, tmat, states, dout)
    dq, dk, dv, dz, dgates, dcq, dck, dcv, dsm = outs
    return dq, dk, dv, dz, dgates, jnp.concatenate([dcq, dck, dcv], axis=-1), dsm


def gdc_pre_fwd(proj, cw, bsz, *, name):
    t = proj.shape[0]
    s_len = t // bsz

    def body(x_ref, w_ref, y_ref):
        row = lax.broadcasted_iota(jnp.int32, (s_len, 128), 0)
        c = _conv_fwd(x_ref[...], w_ref[...], row)
        xc = c * _sigmoid(c)
        rn = lax.rsqrt(jnp.sum(xc * xc, axis=-1, keepdims=True) + NORM_EPS)
        y_ref[...] = jnp.where(pl.program_id(1) < 2 * C_HEADS, xc * rn, xc)

    blk = pl.BlockSpec((s_len, 128), lambda b, j: (b, j))
    return _pcall(
        body, name=name, grid=(bsz, 3 * C_HEADS), in_specs=[blk, pl.BlockSpec((4, 128), lambda b, j: (0, j))],
        out_specs=blk, out_shape=jax.ShapeDtypeStruct((t, 3 * C_WIDTH), F32), compiler_params=_cp("parallel", "parallel"),
    )(proj, cw)


def gdc_pre_bwd(proj, cw, dy, dproj, bsz, *, name):
    t = proj.shape[0]
    s_len = t // bsz

    def body(x_ref, w_ref, dq_ref, dk_ref, dv_ref, _, dx_ref, dw_ref):
        row = lax.broadcasted_iota(jnp.int32, (s_len, 128), 0)
        x = x_ref[...]
        c = _conv_fwd(x, w_ref[...], row)
        sg = _sigmoid(c)
        xc = c * sg
        rn = lax.rsqrt(jnp.sum(xc * xc, axis=-1, keepdims=True) + NORM_EPS)
        part = pl.program_id(1) // C_HEADS
        dyv = jnp.where(part == 0, dq_ref[...], jnp.where(part == 1, dk_ref[...], dv_ref[...]))
        xn = xc * rn
        dxc = jnp.where(pl.program_id(1) < 2 * C_HEADS, rn * (dyv - xn * jnp.sum(dyv * xn, axis=-1, keepdims=True)), dyv)
        dc = dxc * (sg * (1.0 + c * (1.0 - sg)))
        dx, dw = _conv_bwd(x, w_ref[...], dc, row)
        dx_ref[...] = dx.astype(MM)
        dw_ref[0] = dw

    blk = pl.BlockSpec((s_len, 128), lambda b, j: (b, j))

    def dy_spec(part):
        return pl.BlockSpec((s_len, 128), lambda b, j: (b, jnp.clip(j - part * C_HEADS, 0, C_HEADS - 1)))

    return _pcall(
        body, name=name, grid=(bsz, 3 * C_HEADS),
        in_specs=[blk, pl.BlockSpec((4, 128), lambda b, j: (0, j)), dy_spec(0), dy_spec(1), dy_spec(2),
                  pl.BlockSpec(memory_space=pl.ANY)],
        out_specs=[blk, pl.BlockSpec((1, 4, 128), lambda b, j: (b, 0, j))],
        out_shape=[jax.ShapeDtypeStruct((t, 4 * C_WIDTH), MM), jax.ShapeDtypeStruct((bsz, 4, 3 * C_WIDTH), F32)],
        input_output_aliases={5: 0}, compiler_params=_cp("parallel", "parallel"),
    )(proj, cw, *dy, dproj)


GDC_GROUP = 16


def _gdc_local(qn, kn, vc, gates, a_log, dtb, h):
    rows = qn.shape[0]
    nc = rows // CHUNK
    row = lax.broadcasted_iota(jnp.int32, (rows, 128), 0)
    lane = lax.broadcasted_iota(jnp.int32, (rows, 128), 1)
    r = {"row": row, "lane": lane}
    r["beta"] = _sigmoid(_col(gates, h, lane))
    r["A"] = jnp.exp(a_log)
    r["pre"] = _col(gates, 8 + h, lane) + dtb
    r["sp"] = _softplus(r["pre"])
    gc = _seg_cumsum(-r["A"] * r["sp"], row)
    sh = (nc, CHUNK, 128)
    q3 = (qn * C_QSCALE).reshape(sh)
    k3 = kn.reshape(sh)
    v3 = vc.reshape(sh)
    beta3 = r["beta"].reshape(sh)
    gc3 = gc.reshape(sh)
    gcl3 = gc3[:, CHUNK - 1:CHUNK, :]
    eg = jnp.exp(gc3)
    ekd = jnp.exp(gcl3 - gc3)
    col64 = gc3[:, :, :CHUNK]
    row64 = jnp.swapaxes(gc3, 1, 2)[:, :CHUNK, :]
    ii = lax.broadcasted_iota(jnp.int32, (nc, CHUNK, CHUNK), 1)
    jj = lax.broadcasted_iota(jnp.int32, (nc, CHUNK, CHUNK), 2)
    tril = ii >= jj
    strict = ii > jj
    dm = jnp.where(tril, jnp.exp(jnp.where(tril, col64 - row64, 0.0)), 0.0)
    kb = k3 * beta3
    lmat = jnp.where(strict, _bmm(kb, k3, "nt") * dm, 0.0)
    attn = _bmm(q3, k3, "nt") * dm
    r.update(q3=q3, k3=k3, v3=v3, beta3=beta3, eg=eg, ekd=ekd, gl=jnp.exp(gcl3), dm=dm, kb=kb, lmat=lmat,
             attn=attn, strict=strict, tril=tril, qg=q3 * eg, kdec=k3 * ekd)
    return r


def _gdc_specs(s_len):
    act = lambda off: pl.BlockSpec((s_len, 128), lambda b, h: (b, off + h))
    smem = pl.BlockSpec(memory_space=pltpu.SMEM)
    return [act(0), act(8), act(16), act(24), pl.BlockSpec((s_len, 128), lambda b, h: (b, 0)), smem, smem,
            pl.BlockSpec((1, 128), lambda b, h: (0, 0))]


def gdc_fwd(qkv, proj, gates, a_log, dtb, ng, bsz, *, name, side=None):
    t = proj.shape[0]
    s_len = t // bsz
    nc = s_len // CHUNK
    grp = min(GDC_GROUP, nc)
    gr = grp * CHUNK

    def body(q_ref, k_ref, v_ref, z_ref, gt_ref, al_ref, dt_ref, ng_ref,
             out_ref, o_ref, tm_ref, st_ref, c_s, b_s, qp_s, op_s, gl_s):
        h = pl.program_id(1)

        def local(gi, carry):
            rs = pl.ds(pl.multiple_of(gi * gr, gr), gr)
            cs = pl.ds(gi * grp, grp)
            r = _gdc_local(q_ref[rs, :], k_ref[rs, :], v_ref[rs, :], gt_ref[rs, :], al_ref[h], dt_ref[h], h)
            tm = _neumann_inverse(r["lmat"])
            tm_ref[0, 0, cs] = tm
            u = _bmm(tm, r["v3"] * r["beta3"], "nn", exact=True)
            w = _bmm(tm, r["kb"] * r["eg"], "nn", exact=True)
            c_s[cs] = -_bmm(r["kdec"], w, "tn")
            b_s[cs] = _bmm(r["kdec"], u, "tn")
            qp_s[cs] = r["qg"] - _bmm(r["attn"], w, "nn")
            op_s[cs] = _bmm(r["attn"], u, "nn")
            gl_s[cs] = r["gl"]
            return carry

        lax.fori_loop(0, nc // grp, local, 0)

        def chunk(n, state):
            st = pl.multiple_of(n * CHUNK, CHUNK)
            st_ref[0, 0, n] = state
            o_ref[pl.ds(st, CHUNK), :] = _mdot(qp_s[n], state) + op_s[n]
            return state * gl_s[n] + _mdot(c_s[n], state) + b_s[n]

        lax.fori_loop(0, nc, chunk, jnp.zeros((128, 128), F32))
        o = o_ref[...]
        rms = lax.rsqrt(jnp.mean(o * o, axis=-1, keepdims=True) + NORM_EPS)
        z = z_ref[...]
        out_ref[...] = o * rms * ng_ref[...] * (z * _sigmoid(z))

    blk = pl.BlockSpec((s_len, 128), lambda b, h: (b, h))
    full = jax.ShapeDtypeStruct((t, C_WIDTH), F32)
    return _call(
        body, (qkv, qkv, qkv, proj, gates, a_log, dtb, ng.reshape(1, 128)), side, (bsz, C_HEADS), name=name,
        in_specs=_gdc_specs(s_len),
        out_specs=[blk, blk, pl.BlockSpec((1, 1, nc, CHUNK, CHUNK), lambda b, h: (b, h, 0, 0, 0)),
                   pl.BlockSpec((1, 1, nc, 128, 128), lambda b, h: (b, h, 0, 0, 0))],
        out_shape=[full, full, jax.ShapeDtypeStruct((bsz, C_HEADS, nc, CHUNK, CHUNK), F32),
                   jax.ShapeDtypeStruct((bsz, C_HEADS, nc, 128, 128), F32)],
        scratch_shapes=[pltpu.VMEM((nc, 128, 128), F32)] * 2 + [pltpu.VMEM((nc, CHUNK, 128), F32)] * 2 +
                       [pltpu.VMEM((nc, 1, 128), F32)],
        compiler_params=_cp("parallel", "parallel"),
    )


def gdc_bwd(qkv, proj, gates, a_log, dtb, ng, o_pre, tmat, states, dout, bsz, *, name, side=None):
    t = proj.shape[0]
    s_len = t // bsz
    nc = s_len // CHUNK
    grp = min(GDC_GROUP, nc)
    gr = grp * CHUNK

    def body(q_ref, k_ref, v_ref, z_ref, gt_ref, al_ref, dt_ref, ng_ref, o_ref, tm_ref, st_ref, do_ref,
             dq_ref, dk_ref, dv_ref, dz_ref, dgt_ref, dsm_ref, c_s, e_s, dsp_s, gl_s, dop_s):
        h = pl.program_id(1)
        a_log_h, dtb_h = al_ref[h], dt_ref[h]

        z = z_ref[...]
        sz = _sigmoid(z)
        o = o_ref[...]
        rms = lax.rsqrt(jnp.mean(o * o, axis=-1, keepdims=True) + NORM_EPS)
        on = o * rms
        dout_v = do_ref[...]
        ngv = ng_ref[...]
        dz_ref[...] = (dout_v * on * ngv * (sz * (1.0 + z * (1.0 - sz)))).astype(MM)
        dos = dout_v * (z * sz)
        dng = jnp.sum(dos * on, axis=0, keepdims=True)
        don = dos * ngv
        dop_s[...] = (rms * (don - on * jnp.mean(don * on, axis=-1, keepdims=True))).reshape(nc, CHUNK, 128)

        def local(gi, carry):
            rs = pl.ds(pl.multiple_of(gi * gr, gr), gr)
            cs = pl.ds(gi * grp, grp)
            r = _gdc_local(q_ref[rs, :], k_ref[rs, :], v_ref[rs, :], gt_ref[rs, :], a_log_h, dtb_h, h)
            w = _bmm(tm_ref[0, 0, cs], r["kb"] * r["eg"], "nn", exact=True)
            c_s[cs] = -_bmm(w, r["kdec"], "tn")
            e_s[cs] = _bmm(r["qg"] - _bmm(r["attn"], w, "nn"), dop_s[cs], "tn")
            gl_s[cs] = r["gl"]
            return carry

        lax.fori_loop(0, nc // grp, local, 0)

        def chunk(i, dstate):
            n = nc - 1 - i
            dsp_s[n] = dstate
            return dstate * gl_s[n] + _mdot(c_s[n], dstate) + e_s[n]

        lax.fori_loop(0, nc, chunk, jnp.zeros((128, 128), F32))

        @pl.when(h == 0)
        def _():
            dgt_ref[...] = jnp.zeros_like(dgt_ref)
            dsm_ref[...] = jnp.zeros_like(dsm_ref)

        def local_bwd(gi, carry):
            d_alog, d_dtb = carry
            rs = pl.ds(pl.multiple_of(gi * gr, gr), gr)
            cs = pl.ds(gi * grp, grp)
            r = _gdc_local(q_ref[rs, :], k_ref[rs, :], v_ref[rs, :], gt_ref[rs, :], a_log_h, dtb_h, h)
            row, lane = r["row"], r["lane"]
            q3, k3, v3, beta3, eg, kb, dm = r["q3"], r["k3"], r["v3"], r["beta3"], r["eg"], r["kb"], r["dm"]
            tm = tm_ref[0, 0, cs]
            u3 = _bmm(tm, v3 * beta3, "nn", exact=True)
            w3 = _bmm(tm, kb * eg, "nn", exact=True)
            state, dsp, do3 = st_ref[0, 0, cs], dsp_s[cs], dop_s[cs]
            vn = u3 - _bmm(w3, state, "nn")
            du = _bmm(r["attn"], do3, "tn") + _bmm(r["kdec"], dsp, "nn")
            dat = jnp.where(r["tril"], _bmm(do3, vn, "nt"), 0.0)
            dqg = _bmm(do3, state, "nt")
            dkd = _bmm(vn, dsp, "nt")
            dgl = jnp.sum(jnp.sum(state * dsp, axis=2, keepdims=True), axis=1, keepdims=True)
            dw = -_bmm(du, state, "nt")
            dvb = _bmm(tm, du, "tn", exact=True)
            dkbg = _bmm(tm, dw, "tn", exact=True)
            dl = -jnp.where(r["strict"], _bmm(dvb, u3, "nt") + _bmm(dkbg, w3, "nt"), 0.0)
            dml = dl * dm
            dn = dat * dm
            dkb = _bmm(dml, k3, "nn") + dkbg * eg
            dk3 = _bmm(dml, kb, "tn") + _bmm(dn, q3, "tn") + dkd * r["ekd"] + dkb * beta3
            dq3 = dqg * eg + _bmm(dn, k3, "nn")
            e = dl * r["lmat"] + dat * r["attn"]
            ones = jnp.ones((grp, CHUNK, 128), F32)
            colsum = lax.dot_general(e, ones, (_BDIMS["tn"], ((0,), (0,))), preferred_element_type=F32, precision=HI)
            dgc = jnp.sum(e, axis=-1, keepdims=True) - colsum
            dgc = dgc + eg * (jnp.sum(dqg * q3, axis=-1, keepdims=True) + jnp.sum(dkbg * kb, axis=-1, keepdims=True))
            skd = jnp.sum(dkd * r["kdec"], axis=-1, keepdims=True)
            dgcl = jnp.sum(skd, axis=1, keepdims=True) + dgl * r["gl"]
            pos3 = lax.broadcasted_iota(jnp.int32, (grp, CHUNK, 128), 1)
            dgc = dgc - skd + jnp.where(pos3 == CHUNK - 1, dgcl, 0.0)
            dbeta = jnp.sum(dkb * k3, axis=-1, keepdims=True) + jnp.sum(dvb * v3, axis=-1, keepdims=True)
            dg = _seg_cumsum_rev(dgc.reshape(gr, 128), row)
            beta = r["beta"]
            dbl = jnp.broadcast_to(dbeta, (grp, CHUNK, 128)).reshape(gr, 128) * beta * (1.0 - beta)
            dai = dg * (-r["A"]) * _sigmoid(r["pre"])
            dgt_ref[rs, :] += jnp.where(lane == h, dbl, 0.0) + jnp.where(lane == 8 + h, dai, 0.0)
            dq_ref[rs, :] = dq3.reshape(gr, 128) * C_QSCALE
            dk_ref[rs, :] = dk3.reshape(gr, 128)
            dv_ref[rs, :] = (dvb * beta3).reshape(gr, 128)
            return (d_alog + jnp.sum(dg * (-r["sp"]), axis=0, keepdims=True) * r["A"],
                    d_dtb + jnp.sum(dai, axis=0, keepdims=True))

        zero = jnp.zeros((1, 128), F32)
        d_alog, d_dtb = lax.fori_loop(0, nc // grp, local_bwd, (zero, zero))
        r16 = lax.broadcasted_iota(jnp.int32, (16, 128), 0)
        l16 = lax.broadcasted_iota(jnp.int32, (16, 128), 1)
        small = jnp.where((r16 == h) & (l16 == 0), d_alog, 0.0) + jnp.where((r16 == h) & (l16 == 1), d_dtb, 0.0)
        dsm_ref[0] += small + jnp.where(r16 == 8 + h, dng, 0.0)

    blk = pl.BlockSpec((s_len, 128), lambda b, h: (b, h))
    blk3 = lambda off: pl.BlockSpec((s_len, 128), lambda b, h: (b, off + h))
    full = jax.ShapeDtypeStruct((t, C_WIDTH), F32)
    c128 = pltpu.VMEM((nc, CHUNK, 128), F32)
    sq = pltpu.VMEM((nc, 128, 128), F32)
    res = _call(
        body, (qkv, qkv, qkv, proj, gates, a_log, dtb, ng.reshape(1, 128), o_pre, tmat, states, dout), side,
        (bsz, C_HEADS), name=name,
        in_specs=_gdc_specs(s_len) + [blk, pl.BlockSpec((1, 1, nc, CHUNK, CHUNK), lambda b, h: (b, h, 0, 0, 0)),
                                      pl.BlockSpec((1, 1, nc, 128, 128), lambda b, h: (b, h, 0, 0, 0)), blk],
        out_specs=[blk, blk, blk, pl.BlockSpec((s_len, 128), lambda b, h: (b, 3 * C_HEADS + h)),
                   pl.BlockSpec((s_len, 128), lambda b, h: (b, 0)), pl.BlockSpec((1, 16, 128), lambda b, h: (b, 0, 0))],
        out_shape=[full, full, full, jax.ShapeDtypeStruct((t, 4 * C_WIDTH), MM), jax.ShapeDtypeStruct((t, 128), F32),
                   jax.ShapeDtypeStruct((bsz, 16, 128), F32)],
        scratch_shapes=[sq, sq, sq, pltpu.VMEM((nc, 1, 128), F32), c128],
        compiler_params=_cp("parallel", "arbitrary"),
    )
    (dq, dk, dv, dz, dgates, dsm), extra = res if side is not None else (res, None)
    out = ((dq, dk, dv), dz, dgates, dsm)
    return out if side is None else (out, extra)


MESH_ID = pl.DeviceIdType.MESH
_FLIPS = [(0, 0, 1), (1, 0, 0), (0, 1, 0), (1, 1, 0), (1, 0, 1), (0, 1, 1), (1, 1, 1)]


def _me():
    return lax.axis_index("x"), lax.axis_index("y"), lax.axis_index("c")


def _flip(coord, d):
    return 1 - coord if d else coord


def all_gather(shard, *, name):
    def body(x_ref, o_ref, send_sems, recv_sems, local_sem):
        x, y, c = _me()
        mine = 4 * x + 2 * y + c
        own = pltpu.make_async_copy(x_ref, o_ref.at[mine], local_sem)
        own.start()
        copies = []
        for k, (dx, dy, dc) in enumerate(_FLIPS):
            cp = pltpu.make_async_remote_copy(
                src_ref=x_ref, dst_ref=o_ref.at[mine], send_sem=send_sems.at[k], recv_sem=recv_sems.at[k],
                device_id=(_flip(x, dx), _flip(y, dy), _flip(c, dc)), device_id_type=MESH_ID)
            cp.start()
            copies.append(cp)
        for cp in copies:
            cp.wait()
        own.wait()

    hbm = pl.BlockSpec(memory_space=pl.ANY)
    return _pcall(
        body, name=name, in_specs=[hbm], out_specs=hbm,
        out_shape=jax.ShapeDtypeStruct((N_DEV,) + shard.shape, shard.dtype),
        scratch_shapes=[pltpu.SemaphoreType.DMA((7,)), pltpu.SemaphoreType.DMA((7,)), pltpu.SemaphoreType.DMA(())],
    )(shard)


def all_to_all(parts, *, name):
    def body(x_ref, o_ref, send_sems, recv_sems, local_sem):
        x, y, c = _me()
        mine = 4 * x + 2 * y + c
        own = pltpu.make_async_copy(x_ref.at[mine], o_ref.at[mine], local_sem)
        own.start()
        copies = []
        for k, (dx, dy, dc) in enumerate(_FLIPS):
            px, py, pc = _flip(x, dx), _flip(y, dy), _flip(c, dc)
            cp = pltpu.make_async_remote_copy(
                src_ref=x_ref.at[4 * px + 2 * py + pc], dst_ref=o_ref.at[mine], send_sem=send_sems.at[k],
                recv_sem=recv_sems.at[k], device_id=(px, py, pc), device_id_type=MESH_ID)
            cp.start()
            copies.append(cp)
        for cp in copies:
            cp.wait()
        own.wait()

    hbm = pl.BlockSpec(memory_space=pl.ANY)
    return _pcall(
        body, name=name, in_specs=[hbm], out_specs=hbm, out_shape=jax.ShapeDtypeStruct(parts.shape, parts.dtype),
        scratch_shapes=[pltpu.SemaphoreType.DMA((7,)), pltpu.SemaphoreType.DMA((7,)), pltpu.SemaphoreType.DMA(())],
    )(parts)


def adamw_sum(parts, w, m, v, *, name, tr=256):
    r, cdim = w.shape
    tr = _tile8(r, tr)

    def body(p_ref, w_ref, m_ref, v_ref, g_ref, d_ref, mo_ref, vo_ref):
        g = p_ref[0].astype(F32)
        for j in range(1, N_DEV):
            g = g + p_ref[j].astype(F32)
        g_ref[...] = g
        mn = ADAM_B1 * m_ref[...] + (1.0 - ADAM_B1) * g
        vn = ADAM_B2 * v_ref[...] + (1.0 - ADAM_B2) * (g * g)
        mo_ref[...] = mn
        vo_ref[...] = vn
        m_hat = mn / (1.0 - ADAM_B1 ** ADAM_STEP)
        v_hat = vn / (1.0 - ADAM_B2 ** ADAM_STEP)
        d_ref[...] = -ADAM_LR * (m_hat / (jnp.sqrt(v_hat) + ADAM_EPS) + ADAM_WD * w_ref[...])

    blk = pl.BlockSpec((tr, cdim), lambda i: (i, 0))
    shp = jax.ShapeDtypeStruct((r, cdim), F32)
    return _pcall(
        body, name=name, grid=(r // tr,), in_specs=[pl.BlockSpec((N_DEV, tr, cdim), lambda i: (0, i, 0)), blk, blk, blk],
        out_specs=[blk, blk, blk, blk], out_shape=[shp, shp, shp, shp], compiler_params=_cp("parallel"),
    )(parts, w, m, v)


def _tile8(n, pref):
    for c in range(min(pref, n) - min(pref, n) % 16, 0, -16):
        if n % c == 0:
            return c
    return n


BIG = [("ffn1_wg", 2), ("ffn1_wu", 2), ("ffn1_wd", 1), ("ffn2_wg", 2), ("ffn2_wu", 2), ("ffn2_wd", 1), ("ple_wg", 1),
       ("ple_wp", 2), ("ab_w_in", 2), ("ab_w_out", 1), ("c_w_in", 2), ("c_w_out", 1)]
SMALL = [("ln_g", 2), ("ln_b", 2), ("b_conv_w", 2), ("c_conv_w", 2)]
REPL = ["ple_bg", "a_sinks", "b_conv_b", "b_wa", "b_ba", "b_wx", "b_bx", "b_lam", "c_a_log", "c_dt_bias", "c_norm_g"]
WEIGHTS = ["ffn1_wg", "ffn1_wu", "ffn1_wd", "ffn2_wg", "ffn2_wu", "ffn2_wd", "ln_g", "ln_b", "ple_wg", "ple_bg", "ple_wp",
           "ab_w_in", "a_sinks", "b_conv_w", "b_conv_b", "b_wa", "b_ba", "b_wx", "b_bx", "b_lam", "ab_w_out", "c_w_in",
           "c_conv_w", "c_a_log", "c_dt_bias", "c_norm_g", "c_w_out"]
PACK_COLS = 1024
PACK_ALIGN = 16 * PACK_COLS


def _as_bf16_bits(a):
    return lax.bitcast_convert_type(a, jnp.bfloat16).reshape(a.shape[:-1] + (2 * a.shape[-1],))


def _from_bf16_bits(a):
    return lax.bitcast_convert_type(a.reshape(a.shape[:-1] + (a.shape[-1] // 2, 2)), F32)


def _pad_rows(flat, align=PACK_ALIGN):
    n = flat.shape[-1]
    total = -(-n // align) * align
    flat = jnp.pad(flat, [(0, 0)] * (flat.ndim - 1) + [(0, total - n)])
    return flat.reshape(flat.shape[:-1] + (total // PACK_COLS, PACK_COLS))


def _join(blocks, axis):
    moved = jnp.moveaxis(blocks, 0, axis)
    shp = list(moved.shape)
    return moved.reshape(shp[:axis] + [shp[axis] * shp[axis + 1]] + shp[axis + 2:])


def _split(full, axis):
    shp = list(full.shape)
    return jnp.moveaxis(full.reshape(shp[:axis] + [N_DEV, shp[axis] // N_DEV] + shp[axis + 1:]), axis, 0)


def _dense_blocks(w):
    z = jnp.zeros((4, 2, 64, 2, 64), w.dtype)
    w4 = w.reshape(4, 2, 64, 64)
    z = z.at[:, 0, :, 0, :].set(w4[:, 0]).at[:, 1, :, 1, :].set(w4[:, 1])
    return z.reshape(4, 128, 128)


def _diag_blocks(d):
    d5 = d.reshape(4, 2, 64, 2, 64)
    return jnp.stack([d5[:, 0, :, 0, :], d5[:, 1, :, 1, :]], axis=1).reshape(8, 64, 64)


def kernel(x, p, ffn1_wg, ffn1_wu, ffn1_wd, ffn2_wg, ffn2_wu, ffn2_wd, ln_g, ln_b, ple_wg, ple_bg, ple_wp, ab_w_in, a_sinks, b_conv_w, b_conv_b, b_wa, b_ba, b_wx, b_bx, b_lam, ab_w_out, c_w_in, c_conv_w, c_a_log, c_dt_bias, c_norm_g, c_w_out, loss_target, m_ffn1_wg, m_ffn1_wu, m_ffn1_wd, m_ffn2_wg, m_ffn2_wu, m_ffn2_wd, m_ln_g, m_ln_b, m_ple_wg, m_ple_bg, m_ple_wp, m_ab_w_in, m_a_sinks, m_b_conv_w, m_b_conv_b, m_b_wa, m_b_ba, m_b_wx, m_b_bx, m_b_lam, m_ab_w_out, m_c_w_in, m_c_conv_w, m_c_a_log, m_c_dt_bias, m_c_norm_g, m_c_w_out, v_ffn1_wg, v_ffn1_wu, v_ffn1_wd, v_ffn2_wg, v_ffn2_wu, v_ffn2_wd, v_ln_g, v_ln_b, v_ple_wg, v_ple_bg, v_ple_wp, v_ab_w_in, v_a_sinks, v_b_conv_w, v_b_conv_b, v_b_wa, v_b_ba, v_b_wx, v_b_bx, v_b_lam, v_ab_w_out, v_c_w_in, v_c_conv_w, v_c_a_log, v_c_dt_bias, v_c_norm_g, v_c_w_out):
    a = dict(locals())
    return _step3(a)


def _step3(a):
    x, p = a["x"], a["p"]
    bsz, s_len, d = x.shape
    t = bsz * s_len
    x2 = x.reshape(t, d)
    tgt = a["loss_target"].reshape(t, d)
    p2 = p.reshape(DEPTH, t, D_PLE)
    shapes = {n: a[n].shape for n in WEIGHTS}
    n_small = sum(int(np.prod(shapes[n])) for n in SMALL_NAMES)
    small_all = SMALL_NAMES + REPL
    f_ff = shapes["ffn1_wg"][2]
    c_cols = shapes["c_w_in"][2]
    wide = dict(tm=1024, tn=1408, tk=1024)
    tall = dict(tm=1408, tn=1024, tk=1024)

    def cast(z):
        return z.astype(MM)

    def ffn_shards(which, l):
        return [cast(a[which + "_wg"][l]), cast(a[which + "_wu"][l]), cast(a[which + "_wd"][l])]

    def ffn_weights(gat, tag):
        return (join_cols(gat[0], name=f"join_{tag}_wg"), join_cols(gat[1], name=f"join_{tag}_wu"),
                gat[2].reshape(N_DEV * gat[2].shape[1], D_MODEL))

    def rows_full(gat):
        return gat.reshape(N_DEV * gat.shape[1], D_MODEL)

    small_send = _flat_pad([a[n] for n in SMALL_NAMES], F32, 32 * LANES).reshape(32, LANES)
    g0 = gather_multi(ffn_shards("ffn1", 0) + [small_send, cast(a["ab_w_in"][0]), cast(a["ab_w_out"][0])],
                      name="gather_first")
    ws = _take(g0[3].reshape(N_DEV, -1), SMALL_NAMES, shapes)
    small = {n: _join(ws[n], 2) for n in SMALL_NAMES}
    ln_g, ln_b = small["ln_g"], small["ln_b"]
    wa_d, wx_d = _dense_blocks(a["b_wa"][0]), _dense_blocks(a["b_wx"][0])
    lru_w = (small["b_conv_w"][0], a["b_conv_b"][0], wa_d, a["b_ba"][0], wx_d, a["b_bx"][0], a["b_lam"][0])
    gdc_w = (a["c_a_log"][0], a["c_dt_bias"][0], a["c_norm_g"][0])
    wf = {("ffn1", 0): ffn_weights(g0[:3], "ffn1_0")}

    s0 = {"x0": x2}
    side = ("gather", ffn_shards("ffn2", 0))
    (s0["y1"], s0["z1"], s0["hg1"], s0["hu1"]), got = ffn_fwd(x2, *wf["ffn1", 0], ln_g[0, 0], ln_b[0, 0],
                                                             name="ffn1_fwd_0", tm=FFN_TM, tf=FFN_TF, side=side)
    wf["ffn2", 0] = ffn_weights(got, "ffn2_0")
    ab_w_in, ab_w_out = join_cols(g0[4], name="join_ab_in"), rows_full(g0[5])
    s0["proj"] = matmul(s0["y1"], ab_w_in, mode="nn", name="ab_in_fwd", tn=896, tk=1024)
    ya, got_ple = attn_fwd(s0["proj"], a["a_sinks"][0], bsz, name="attn_fwd",
                           side=("gather", [cast(a["ple_wg"][0]), cast(a["ple_wp"][0])]))
    yb = lru_fwd(s0["proj"], *lru_w, bsz, name="lru_fwd")
    s0["mix"] = jnp.concatenate([ya, yb], axis=1)
    s0["y2"], s0["z2"] = mm_ln_fwd(s0["mix"], ab_w_out, s0["y1"], ln_g[0, 1], ln_b[0, 1], name="mix_out_fwd_0")
    side = ("gather", ffn_shards("ffn1", 1))
    (s0["y3"], s0["z3"], s0["hg2"], s0["hu2"]), got = ffn_fwd(s0["y2"], *wf["ffn2", 0], ln_g[0, 2], ln_b[0, 2],
                                                             name="ffn2_fwd_0", tm=FFN_TM, tf=FFN_TF, side=side)
    wf["ffn1", 1] = ffn_weights(got, "ffn1_1")
    ple_wg = [rows_full(got_ple[0]), None]
    ple_wp = [_join(got_ple[1], 1), None]
    h1 = ple_fwd(s0["y3"], p2[0], ple_wg[0], a["ple_bg"][0], ple_wp[0], name="ple_fwd_0")

    s1 = {"x0": h1}
    side = ("gather", [cast(a["c_w_in"][0]), cast(a["c_w_out"][0])])
    (s1["y1"], s1["z1"], s1["hg1"], s1["hu1"]), got = ffn_fwd(h1, *wf["ffn1", 1], ln_g[1, 0], ln_b[1, 0],
                                                             name="ffn1_fwd_1", tm=FFN_TM, tf=FFN_TF, side=side)
    c_in_main, c_in_gate = join_cols(got[0], name="join_c_in", outs=[(0, 4 * C_WIDTH, 4 * C_WIDTH),
                                                                      (4 * C_WIDTH, 4 * C_WIDTH + 2 * C_HEADS, LANES)])
    c_w_out = rows_full(got[1])
    s1["proj"] = matmul(s1["y1"], c_in_main, mode="nn", name="c_in_fwd", tm=1024, tn=2048, tk=1024)
    s1["gates"] = matmul(s1["y1"], c_in_gate, mode="nn", name="c_gate_fwd", tk=1024)
    s1["qkv"] = gdc_pre_fwd(s1["proj"], small["c_conv_w"][0], bsz, name="gdc_pre_fwd")
    side = ("gather", ffn_shards("ffn2", 1) + [cast(a["ple_wg"][1]), cast(a["ple_wp"][1])])
    (s1["mix"], s1["o_pre"], s1["tmat"], s1["states"]), got = gdc_fwd(
        s1["qkv"], s1["proj"], s1["gates"], *gdc_w, bsz, name="gdc_fwd", side=side)
    wf["ffn2", 1] = ffn_weights(got[:3], "ffn2_1")
    ple_wg[1], ple_wp[1] = rows_full(got[3]), _join(got[4], 1)
    s1["y2"], s1["z2"] = mm_ln_fwd(s1["mix"], c_w_out, s1["y1"], ln_g[1, 1], ln_b[1, 1], name="mix_out_fwd_1")
    s1["y3"], s1["z3"], s1["hg2"], s1["hu2"] = ffn_fwd(s1["y2"], *wf["ffn2", 1], ln_g[1, 2], ln_b[1, 2], name="ffn2_fwd_1",
                                                           tm=FFN_TM, tf=FFN_TF)
    h2 = ple_fwd(s1["y3"], p2[1], ple_wg[1], a["ple_bg"][1], ple_wp[1], name="ple_fwd_1")
    loss_part, dh = loss_fwd_bwd(h2, tgt, name="loss")

    def ffn_parts(xin, act, dhg, dhu, dz, tag):
        dwg = matmul(xin, dhg, mode="tn", name=f"{tag}_wg_grad", **wide)
        dwu = matmul(xin, dhu, mode="tn", name=f"{tag}_wu_grad", **wide)
        dwd = matmul(act, dz, mode="tn", scale=0.5, out_dtype=MM, name=f"{tag}_wd_grad", **tall)
        return [split_cols([(dwg, N_DEV * f_ff)], f_ff, name=f"split_{tag}_wg"),
                split_cols([(dwu, N_DEV * f_ff)], f_ff, name=f"split_{tag}_wu"), dwd.reshape(N_DEV, f_ff, D_MODEL)]

    def ple_parts(i, s, dt, de):
        gwg = matmul(s["y3"], dt, mode="tn", out_dtype=MM, name=f"ple_wg_grad_{i}", tm=1024, tn=1024)
        gwp = matmul(p2[i], de, mode="tn", out_dtype=MM, name=f"ple_wp_grad_{i}", tn=1024)
        return [gwg.reshape(N_DEV, D_MODEL // N_DEV, D_MODEL), _split(gwp, 1)]

    gln = {"ln_g": [None, None], "ln_b": [None, None]}
    gple_bg = [None, None]

    dz3, dt, de, dbg, dg2, db2 = ple_bwd(dh, s1["y3"], p2[1], ple_wg[1], a["ple_bg"][1], ple_wp[1], s1["z3"], ln_g[1, 2],
                                         name="ple_bwd_1")
    gple_bg[1] = dbg[0]
    parts_ple1 = ple_parts(1, s1, dt, de)
    dy2, act, dhg, dhu = ffn_bwd(dz3, s1["hg2"], s1["hu2"], *wf["ffn2", 1], name="ffn2_bwd_1", tm=FFN_TM, tf=FFN_TF)
    parts_ffn2_1 = ffn_parts(s1["y2"], act, dhg, dhu, dz3, "ffn2_1")
    dz2, dg1, db1 = ln_bwd(dy2, s1["z2"], ln_g[1, 1], name="ln1_bwd_1")
    dmix = matmul(dz2, c_w_out, mode="nt", name="c_out_bwd", tn=1024, tk=1024)
    parts_c_out = matmul(s1["mix"], dz2, mode="tn", out_dtype=MM, name="c_out_grad", tm=1024, tn=1024).reshape(
        N_DEV, D_MODEL // N_DEV, D_MODEL)
    (dqkv, dzc, dgates, dsm), recv1 = gdc_bwd(s1["qkv"], s1["proj"], s1["gates"], *gdc_w, s1["o_pre"],
                                              s1["tmat"], s1["states"], dmix, bsz, name="gdc_bwd",
                                              side=("exchange", parts_ffn2_1 + parts_ple1))
    dproj, dccw = gdc_pre_bwd(s1["proj"], small["c_conv_w"][0], dqkv, dzc, bsz, name="gdc_pre_bwd")
    dgb = dgates.astype(MM)
    dy1 = matmul(dproj, c_in_main, mode="nt", add=dz2, add_scale=DN_ALPHA, name="c_in_bwd", tn=1024, tk=4096)
    dy1 = matmul(dgb, c_in_gate, mode="nt", add=dy1, name="c_gate_bwd", tn=1024)
    g_c_main = matmul(s1["y1"], dproj, mode="tn", name="c_in_grad", tm=1024, tn=1024, tk=1024)
    g_c_gate = matmul(s1["y1"], dgb, mode="tn", name="c_gate_grad", tm=1024)
    parts_c_in = split_cols([(g_c_main, 4 * C_WIDTH), (g_c_gate, 2 * C_HEADS)], c_cols, name="split_c_in")
    dz1, dg0, db0 = ln_bwd(dy1, s1["z1"], ln_g[1, 0], name="ln0_bwd_1")
    (dh, act, dhg, dhu), recv_c = ffn_bwd(dz1, s1["hg1"], s1["hu1"], *wf["ffn1", 1], name="ffn1_bwd_1", tm=FFN_TM,
                                          tf=FFN_TF, side=("exchange", [parts_c_in, parts_c_out]))
    parts_ffn1_1 = ffn_parts(s1["x0"], act, dhg, dhu, dz1, "ffn1_1")
    gln["ln_g"][1] = jnp.concatenate([dg0, dg1, dg2], axis=0)
    gln["ln_b"][1] = jnp.concatenate([db0, db1, db2], axis=0)

    dz3, dt, de, dbg, dg2, db2 = ple_bwd(dh, s0["y3"], p2[0], ple_wg[0], a["ple_bg"][0], ple_wp[0], s0["z3"], ln_g[0, 2],
                                         name="ple_bwd_0")
    gple_bg[0] = dbg[0]
    parts_ple0 = ple_parts(0, s0, dt, de)
    (dy2, act, dhg, dhu), recv2 = ffn_bwd(dz3, s0["hg2"], s0["hu2"], *wf["ffn2", 0], name="ffn2_bwd_0", tm=FFN_TM,
                                          tf=FFN_TF, side=("exchange", parts_ffn1_1))
    parts_ffn2_0 = ffn_parts(s0["y2"], act, dhg, dhu, dz3, "ffn2_0")
    dz2, dg1, db1 = ln_bwd(dy2, s0["z2"], ln_g[0, 1], name="ln1_bwd_0")
    dmix = matmul(dz2, ab_w_out, mode="nt", name="ab_out_bwd", tn=1024, tk=1024)
    parts_ab_out = matmul(s0["mix"], dz2, mode="tn", out_dtype=MM, name="ab_out_grad", tm=1024, tn=1024).reshape(
        N_DEV, D_MODEL // N_DEV, D_MODEL)
    (dq, dk, dv, dsk), recv3a = attn_bwd(s0["proj"], a["a_sinks"][0], dmix, bsz, name="attn_bwd",
                                         side=("exchange", parts_ffn2_0[:2]))
    (dbx, dbgate, dcw, dcb, dwa, dba, dwx, dbxb, dlam), recv3b = lru_bwd(
        s0["proj"], *lru_w, dmix, bsz, name="lru_bwd", side=("exchange", [parts_ffn2_0[2]] + parts_ple0 + [parts_ab_out]))
    dproj = jnp.concatenate([dq, dk, dv, dbx, dbgate], axis=1).astype(MM)
    dy1 = matmul(dproj, ab_w_in, mode="nt", add=dz2, add_scale=DN_ALPHA, name="ab_in_bwd", tn=1024, tk=1792)
    g_ab_in = matmul(s0["y1"], dproj, mode="tn", name="ab_in_grad", tm=1024, tn=896)
    parts_ab_in = split_cols([(g_ab_in, AB_PROJ)], AB_PROJ // N_DEV, name="split_ab_in")
    dz1, dg0, db0 = ln_bwd(dy1, s0["z1"], ln_g[0, 0], name="ln0_bwd_0")
    gln["ln_g"][0] = jnp.concatenate([dg0, dg1, dg2], axis=0)
    gln["ln_b"][0] = jnp.concatenate([db0, db1, db2], axis=0)

    dsm_sum = jnp.sum(dsm, axis=0)
    full = dict(ln_g=jnp.stack(gln["ln_g"]), ln_b=jnp.stack(gln["ln_b"]), b_conv_w=dcw[None],
                c_conv_w=jnp.sum(dccw, axis=0)[None], ple_bg=jnp.stack(gple_bg),
                a_sinks=jnp.sum(dsk, axis=0)[:, :A_HEADS], b_conv_b=dcb, b_wa=_diag_blocks(dwa)[None], b_ba=dba,
                b_wx=_diag_blocks(dwx)[None], b_bx=dbxb, b_lam=dlam, c_a_log=dsm_sum[None, :C_HEADS, 0],
                c_dt_bias=dsm_sum[None, :C_HEADS, 1], c_norm_g=jnp.sum(dsm_sum[C_HEADS:], axis=0)[None])
    small_rows = SMALL_F32 // LANES
    repl_flat = _flat_pad([full[n] for n in REPL], F32, SMALL_F32 - n_small)
    small8 = jnp.concatenate([_flat8_pad([_split(full[n], 2) for n in SMALL_NAMES], F32, n_small),
                              jnp.broadcast_to(repl_flat, (N_DEV,) + repl_flat.shape)], axis=1)
    (dh, act, dhg, dhu), recv3c = ffn_bwd(dz1, s0["hg1"], s0["hu1"], *wf["ffn1", 0], name="ffn1_bwd_0", tm=FFN_TM,
                                          tf=FFN_TF, side=("exchange", [parts_ab_in, small8.reshape(N_DEV, small_rows, LANES)]))
    grad_x = dh.reshape(bsz, s_len, d)

    dwg = matmul(s0["x0"], dhg, mode="tn", name="ffn1_0_wg_grad", **wide)
    parts_wg = split_cols([(dwg, N_DEV * f_ff)], f_ff, name="split_ffn1_0_wg")
    dwu, recv4a = matmul(s0["x0"], dhu, mode="tn", name="ffn1_0_wu_grad", side=("exchange", [parts_wg]), **wide)
    parts_wu = split_cols([(dwu, N_DEV * f_ff)], f_ff, name="split_ffn1_0_wu")
    dwd, recv4b = matmul(act, dz1, mode="tn", scale=0.5, out_dtype=MM, name="ffn1_0_wd_grad",
                         side=("exchange", [parts_wu]), **tall)
    recv4c = exchange_multi([dwd.reshape(N_DEV, f_ff, D_MODEL)], name="exchange_last")

    def upd(parts, n, l, shape2d, **kw):
        wmv = [a[pre + n][l].reshape(shape2d) for pre in ("", "m_", "v_")]
        return adamw_rows(parts, 0, *wmv, name=f"adamw_{n}_{l}", **kw)

    def upd_ffn(parts, which, l):
        return {(which + "_wg", l): upd(parts[0], which + "_wg", l, (D_MODEL, f_ff)),
                (which + "_wu", l): upd(parts[1], which + "_wu", l, (D_MODEL, f_ff)),
                (which + "_wd", l): upd(parts[2], which + "_wd", l, (f_ff, D_MODEL), tr=176)}

    rows8 = D_MODEL // N_DEV
    res = {}
    res.update(upd_ffn(recv1[:3], "ffn2", 1))
    res["ple_wg", 1] = upd(recv1[3], "ple_wg", 1, (rows8, D_MODEL), tr=128)
    res["ple_wp", 1] = upd(recv1[4], "ple_wp", 1, (D_PLE, LANES))
    res.update(upd_ffn(recv2, "ffn1", 1))
    res["c_w_in", 0] = upd(recv_c[0], "c_w_in", 0, (D_MODEL, c_cols))
    res["c_w_out", 0] = upd(recv_c[1], "c_w_out", 0, (rows8, D_MODEL), tr=128)
    res.update(upd_ffn(recv3a + recv3b[:1], "ffn2", 0))
    res["ple_wg", 0] = upd(recv3b[1], "ple_wg", 0, (rows8, D_MODEL), tr=128)
    res["ple_wp", 0] = upd(recv3b[2], "ple_wp", 0, (D_PLE, LANES))
    res["ab_w_out", 0] = upd(recv3b[3], "ab_w_out", 0, (rows8, D_MODEL), tr=128)
    res.update(upd_ffn(recv4a + recv4b + list(recv4c), "ffn1", 0))
    res["ab_w_in", 0] = upd(recv3c[0], "ab_w_in", 0, (D_MODEL, AB_PROJ // N_DEV))
    res_small = adamw_rows(recv3c[1], 0, *[_flat_pad([a[pre + n] for n in small_all], F32, SMALL_F32).reshape(
        small_rows, LANES) for pre in ("", "m_", "v_")], name="adamw_small", tr=small_rows)
    kinds = []
    for k in range(4):
        kd = _take(res_small[k].reshape(-1), small_all, shapes)
        for n in WEIGHTS:
            if n not in kd:
                kd[n] = jnp.stack([res[n, l][k] for l in range(shapes[n][0])]).reshape(shapes[n])
        kinds.append(kd)
    loss = lax.psum(loss_part[0, 0], ("x", "y", "c"))
    return (loss, grad_x, *[kinds[0][n] for n in WEIGHTS], *[kinds[1][n] for n in WEIGHTS],
            *[kinds[2][n] for n in WEIGHTS], *[kinds[3][n] for n in WEIGHTS])


def join_cols(x, *, name, outs=None, tk=256):
    _, kk, n = x.shape
    tk = _tile8(kk, tk)
    outs = outs or [(0, N_DEV * n, N_DEV * n)]

    def body(x_ref, *o_refs):
        full = jnp.concatenate([x_ref[k] for k in range(N_DEV)], axis=-1)
        for (lo, hi, wd), o_ref in zip(outs, o_refs):
            piece = full[:, lo:hi]
            if wd > hi - lo:
                piece = jnp.concatenate([piece, jnp.zeros((tk, wd - (hi - lo)), piece.dtype)], axis=-1)
            o_ref[...] = piece

    res = _pcall(
        body, name=name, grid=(kk // tk,), in_specs=[pl.BlockSpec((N_DEV, tk, n), lambda i: (0, i, 0))],
        out_specs=[pl.BlockSpec((tk, wd), lambda i: (i, 0)) for _, _, wd in outs],
        out_shape=[jax.ShapeDtypeStruct((kk, wd), x.dtype) for _, _, wd in outs], compiler_params=_cp("parallel"),
    )(x)
    return res if len(outs) > 1 else res[0]


def split_cols(pieces, n, *, name, tk=256):
    kk = pieces[0][0].shape[0]
    tk = _tile8(kk, tk)

    def body(*refs):
        o_ref = refs[-1]
        vals = [r[...][:, :used] for r, (_, used) in zip(refs[:-1], pieces)]
        full = vals[0] if len(vals) == 1 else jnp.concatenate(vals, axis=-1)
        for k in range(N_DEV):
            o_ref[k] = full[:, k * n:(k + 1) * n].astype(MM)

    return _pcall(
        body, name=name, grid=(kk // tk,),
        in_specs=[pl.BlockSpec((tk, arr.shape[1]), lambda i: (i, 0)) for arr, _ in pieces],
        out_specs=pl.BlockSpec((N_DEV, tk, n), lambda i: (0, i, 0)),
        out_shape=jax.ShapeDtypeStruct((N_DEV, kk, n), MM), compiler_params=_cp("parallel"),
    )(*[arr for arr, _ in pieces])


def gather_multi(shards, *, name):
    ng = len(shards)

    def body(*refs):
        x_refs, o_refs = refs[:ng], refs[ng:2 * ng]
        send_sems, recv_sems, local_sems = refs[2 * ng:]
        x, y, c = _me()
        sibling = (x, y, 1 - c)
        chips = [(1 - x, y), (x, 1 - y), (1 - x, 1 - y)]

        def slot(px, py, pc):
            return 4 * px + 2 * py + pc

        def copy(gi, k, block, to, src=None):
            dst = o_refs[gi].at[slot(*block)]
            return pltpu.make_async_remote_copy(
                src_ref=dst if src is None else src, dst_ref=dst, send_sem=send_sems.at[7 * gi + k],
                recv_sem=recv_sems.at[7 * gi + k], device_id=to, device_id_type=MESH_ID)

        own = [pltpu.make_async_copy(x_refs[gi], o_refs[gi].at[slot(x, y, c)], local_sems.at[gi]) for gi in range(ng)]
        for cp in own:
            cp.start()
        first = []
        for gi in range(ng):
            first.append(copy(gi, 0, (x, y, c), sibling, src=x_refs[gi]))
            first += [copy(gi, 1 + j, (x, y, c), (*chip, c), src=x_refs[gi]) for j, chip in enumerate(chips)]
        for cp in first:
            cp.start()
        passed = []
        for j, chip in enumerate(chips):
            for gi in range(ng):
                copy(gi, 1 + j, (*chip, c), (x, y, c)).wait_recv()
                fwd = copy(gi, 4 + j, (*chip, c), sibling)
                fwd.start()
                passed.append(fwd)
        for gi in range(ng):
            copy(gi, 0, sibling, (x, y, c)).wait_recv()
            for j, chip in enumerate(chips):
                copy(gi, 4 + j, (*chip, 1 - c), (x, y, c)).wait_recv()
        for cp in first + passed:
            cp.wait_send()
        for cp in own:
            cp.wait()

    hbm = pl.BlockSpec(memory_space=pl.ANY)
    return _pcall(
        body, name=name, in_specs=[hbm] * ng, out_specs=[hbm] * ng,
        out_shape=[jax.ShapeDtypeStruct((N_DEV,) + s.shape, s.dtype) for s in shards],
        scratch_shapes=[pltpu.SemaphoreType.DMA((7 * ng,)), pltpu.SemaphoreType.DMA((7 * ng,)),
                        pltpu.SemaphoreType.DMA((ng,))],
    )(*shards)


def exchange_multi(parts, *, name):
    ng = len(parts)

    def body(*refs):
        x_refs, o_refs = refs[:ng], refs[ng:2 * ng]
        send_sems, recv_sems, local_sems = refs[2 * ng:]
        x, y, c = _me()
        mine = 4 * x + 2 * y + c
        own = [pltpu.make_async_copy(x_refs[gi].at[mine], o_refs[gi].at[mine], local_sems.at[gi]) for gi in range(ng)]
        for cp in own:
            cp.start()
        copies = []
        for k, (dx, dy, dc) in enumerate(_FLIPS):
            px, py, pc = _flip(x, dx), _flip(y, dy), _flip(c, dc)
            for gi in range(ng):
                cp = pltpu.make_async_remote_copy(
                    src_ref=x_refs[gi].at[4 * px + 2 * py + pc], dst_ref=o_refs[gi].at[mine],
                    send_sem=send_sems.at[7 * gi + k], recv_sem=recv_sems.at[7 * gi + k], device_id=(px, py, pc),
                    device_id_type=MESH_ID)
                cp.start()
                copies.append(cp)
        for cp in copies:
            cp.wait()
        for cp in own:
            cp.wait()

    hbm = pl.BlockSpec(memory_space=pl.ANY)
    return _pcall(
        body, name=name, in_specs=[hbm] * ng, out_specs=[hbm] * ng,
        out_shape=[jax.ShapeDtypeStruct(s.shape, s.dtype) for s in parts],
        scratch_shapes=[pltpu.SemaphoreType.DMA((7 * ng,)), pltpu.SemaphoreType.DMA((7 * ng,)),
                        pltpu.SemaphoreType.DMA((ng,))],
    )(*parts)


def adamw_rows(parts, row0, w, m, v, *, name, tr=256):
    r, cdim = w.shape
    tr = _tile8(math.gcd(r, row0) if row0 else r, tr)
    blk0 = row0 // tr

    def body(p_ref, w_ref, m_ref, v_ref, g_ref, d_ref, mo_ref, vo_ref):
        g = p_ref[0].astype(F32)
        for j in range(1, N_DEV):
            g = g + p_ref[j].astype(F32)
        g_ref[...] = g
        mn = ADAM_B1 * m_ref[...] + (1.0 - ADAM_B1) * g
        vn = ADAM_B2 * v_ref[...] + (1.0 - ADAM_B2) * (g * g)
        mo_ref[...] = mn
        vo_ref[...] = vn
        m_hat = mn / (1.0 - ADAM_B1 ** ADAM_STEP)
        v_hat = vn / (1.0 - ADAM_B2 ** ADAM_STEP)
        d_ref[...] = -ADAM_LR * (m_hat / (jnp.sqrt(v_hat) + ADAM_EPS) + ADAM_WD * w_ref[...])

    blk = pl.BlockSpec((tr, cdim), lambda i: (i, 0))
    shp = jax.ShapeDtypeStruct((r, cdim), F32)
    return _pcall(
        body, name=name, grid=(r // tr,),
        in_specs=[pl.BlockSpec((N_DEV, tr, cdim), lambda i: (0, blk0 + i, 0)), blk, blk, blk],
        out_specs=[blk, blk, blk, blk], out_shape=[shp, shp, shp, shp], compiler_params=_cp("parallel"),
    )(parts, w, m, v)


GROUP_A = ["ffn1_wg", "ffn1_wu", "ffn2_wg", "ffn2_wu"]
GROUP_B = ["ffn1_wd", "ffn2_wd", "ple_wg", "ab_w_out", "c_w_out"]
SMALL_NAMES = ["ln_g", "ln_b", "b_conv_w", "c_conv_w"]
LANES = 128
FFN_TM = 512
FFN_TF = 1408
SMALL_F32 = 73728
PLE_WP_ROWS = DEPTH * D_PLE


def _step2(a):
    x, p = a["x"], a["p"]
    bsz, s_len, d = x.shape
    t = bsz * s_len
    x2 = x.reshape(t, d)
    tgt = a["loss_target"].reshape(t, d)
    p2 = p.reshape(DEPTH, t, D_PLE)
    shapes = {n: a[n].shape for n in WEIGHTS}
    bits_per = 1 if MM == F32 else 2
    n_small = sum(int(np.prod(shapes[n])) for n in SMALL_NAMES)
    small_all = SMALL_NAMES + REPL
    f_ff = shapes["ffn1_wg"][2]
    rows_b = {n: shapes[n][0] * shapes[n][1] for n in GROUP_B}
    off_b = dict(zip(GROUP_B, np.cumsum([0] + [rows_b[n] for n in GROUP_B])[:-1].tolist()))

    send = [
        jnp.concatenate([a[n].astype(MM).reshape(-1, f_ff) for n in GROUP_A], axis=0),
        jnp.concatenate([a[n].astype(MM).reshape(-1, D_MODEL) for n in GROUP_B], axis=0),
        a["ab_w_in"][0].astype(MM),
        a["c_w_in"][0].astype(MM),
        a["ple_wp"].astype(MM).reshape(PLE_WP_ROWS, LANES),
        _flat_pad([a[n] for n in SMALL_NAMES], F32, 32 * LANES).reshape(32, LANES),
    ]
    ga, gb, gc, gd, ge, gf = gather_multi(send, name="gather_weights")
    wa_full = join_cols(ga, name="join_ffn").reshape(len(GROUP_A), DEPTH, D_MODEL, N_DEV * f_ff)
    w = {n: wa_full[i] for i, n in enumerate(GROUP_A)}
    for n in GROUP_B:
        lyr, rws = shapes[n][0], shapes[n][1]
        blk = gb[:, off_b[n]:off_b[n] + rows_b[n]].reshape(N_DEV, lyr, rws, D_MODEL)
        w[n] = jnp.swapaxes(blk, 0, 1).reshape(lyr, N_DEV * rws, D_MODEL)
    w["ab_w_in"] = join_cols(gc, name="join_ab_in")
    c_in_main, c_in_gate = join_cols(gd, name="join_c_in", outs=[(0, 4 * C_WIDTH, 4 * C_WIDTH),
                                                                  (4 * C_WIDTH, 4 * C_WIDTH + 2 * C_HEADS, LANES)])
    w["ple_wp"] = _join(ge.reshape(N_DEV, DEPTH, D_PLE, LANES), 2)
    ws = _take(gf.reshape(N_DEV, -1), SMALL_NAMES, shapes)
    w.update({n: _join(ws[n], 2) for n in SMALL_NAMES})
    ln_g, ln_b = w["ln_g"], w["ln_b"]
    wa_d, wx_d = _dense_blocks(a["b_wa"][0]), _dense_blocks(a["b_wx"][0])
    lru_w = (w["b_conv_w"][0], a["b_conv_b"][0], wa_d, a["b_ba"][0], wx_d, a["b_bx"][0], a["b_lam"][0])
    gdc_w = (a["c_a_log"][0], a["c_dt_bias"][0], a["c_norm_g"][0])

    h = x2
    saved = []
    for i in range(DEPTH):
        s = {"x0": h}
        s["y1"], s["z1"], s["hg1"], s["hu1"] = ffn_fwd(h, w["ffn1_wg"][i], w["ffn1_wu"][i], w["ffn1_wd"][i], ln_g[i, 0], ln_b[i, 0],
                                   name=f"ffn1_fwd_{i}")
        if i == 0:
            s["proj"] = matmul(s["y1"], w["ab_w_in"], mode="nn", name="ab_in_fwd", tn=896, tk=1024)
            ya = attn_fwd(s["proj"], a["a_sinks"][0], bsz, name="attn_fwd")
            yb = lru_fwd(s["proj"], *lru_w, bsz, name="lru_fwd")
            s["mix"] = jnp.concatenate([ya, yb], axis=1)
            w_out = w["ab_w_out"][0]
        else:
            s["proj"] = matmul(s["y1"], c_in_main, mode="nn", name="c_in_fwd", tm=1024, tn=2048, tk=1024)
            s["gates"] = matmul(s["y1"], c_in_gate, mode="nn", name="c_gate_fwd", tk=1024)
            s["qkv"] = gdc_pre_fwd(s["proj"], w["c_conv_w"][0], bsz, name="gdc_pre_fwd")
            s["mix"], s["o_pre"], s["vnew"], s["tmat"], s["states"] = gdc_fwd(
                s["qkv"], s["proj"], s["gates"], *gdc_w, bsz, name="gdc_fwd")
            w_out = w["c_w_out"][0]
        s["y2"], s["z2"] = mm_ln_fwd(s["mix"], w_out, s["y1"], ln_g[i, 1], ln_b[i, 1], name=f"mix_out_fwd_{i}")
        s["y3"], s["z3"], s["hg2"], s["hu2"] = ffn_fwd(s["y2"], w["ffn2_wg"][i], w["ffn2_wu"][i], w["ffn2_wd"][i], ln_g[i, 2], ln_b[i, 2],
                                   name=f"ffn2_fwd_{i}")
        h = ple_fwd(s["y3"], p2[i], w["ple_wg"][i], a["ple_bg"][i], w["ple_wp"][i], name=f"ple_fwd_{i}")
        saved.append(s)
    loss_part, dh = loss_fwd_bwd(h, tgt, name="loss")

    g = {n: [None] * shapes[n][0] for n in ("ffn1_wg", "ffn1_wu", "ffn1_wd", "ffn2_wg", "ffn2_wu", "ffn2_wd", "ln_g",
                                             "ln_b", "ple_wg", "ple_bg", "ple_wp")}
    wide = dict(tm=1024, tn=1408, tk=1024)
    tall = dict(tm=1408, tn=1024, tk=1024)
    for i in reversed(range(DEPTH)):
        s = saved[i]
        dy3, dt, de, dbg = ple_bwd(dh, s["y3"], p2[i], w["ple_wg"][i], a["ple_bg"][i], w["ple_wp"][i], name=f"ple_bwd_{i}")
        g["ple_wg"][i] = matmul(s["y3"], dt, mode="tn", name=f"ple_wg_grad_{i}", tm=1024, tn=1024)
        g["ple_wp"][i] = matmul(p2[i], de, mode="tn", name=f"ple_wp_grad_{i}", tn=1024)
        g["ple_bg"][i] = dbg[0]
        dz3, dg2, db2 = ln_bwd(dy3, s["z3"], ln_g[i, 2], name=f"ln2_bwd_{i}")
        dy2, act, dhg, dhu = ffn_bwd(dz3, s["hg2"], s["hu2"], w["ffn2_wg"][i], w["ffn2_wu"][i], w["ffn2_wd"][i], name=f"ffn2_bwd_{i}", tm=FFN_TM, tf=FFN_TF)
        g["ffn2_wg"][i] = matmul(s["y2"], dhg, mode="tn", name=f"ffn2_wg_grad_{i}", **wide)
        g["ffn2_wu"][i] = matmul(s["y2"], dhu, mode="tn", name=f"ffn2_wu_grad_{i}", **wide)
        g["ffn2_wd"][i] = matmul(act, dz3, mode="tn", scale=0.5, name=f"ffn2_wd_grad_{i}", **tall)
        dz2, dg1, db1 = ln_bwd(dy2, s["z2"], ln_g[i, 1], name=f"ln1_bwd_{i}")
        if i == 0:
            dmix = matmul(dz2, w["ab_w_out"][0], mode="nt", name="ab_out_bwd", tn=1024, tk=1024)
            g["ab_w_out"] = matmul(s["mix"], dz2, mode="tn", name="ab_out_grad", tm=1024, tn=1024)
            dq, dk, dv, dsk = attn_bwd(s["proj"], a["a_sinks"][0], dmix, bsz, name="attn_bwd")
            dbx, dbgate, dcw, dcb, dwa, dba, dwx, dbxb, dlam = lru_bwd(s["proj"], *lru_w, dmix, bsz, name="lru_bwd")
            dproj = jnp.concatenate([dq, dk, dv, dbx, dbgate], axis=1).astype(MM)
            dy1 = matmul(dproj, w["ab_w_in"], mode="nt", add=dz2, add_scale=DN_ALPHA, name="ab_in_bwd", tn=1024, tk=1792)
            g_ab_in = matmul(s["y1"], dproj, mode="tn", name="ab_in_grad", tm=1024, tn=896)
        else:
            dmix = matmul(dz2, w["c_w_out"][0], mode="nt", name="c_out_bwd", tn=1024, tk=1024)
            g["c_w_out"] = matmul(s["mix"], dz2, mode="tn", name="c_out_grad", tm=1024, tn=1024)
            dqkv, dzc, dgates, dsm = gdc_bwd(s["qkv"], s["proj"], s["gates"], *gdc_w, s["o_pre"], s["vnew"], s["tmat"],
                                             s["states"], dmix, bsz, name="gdc_bwd")
            draw, dccw = gdc_pre_bwd(s["proj"], w["c_conv_w"][0], dqkv, bsz, name="gdc_pre_bwd")
            dproj = jnp.concatenate([draw, dzc], axis=1).astype(MM)
            dgb = dgates.astype(MM)
            dy1 = matmul(dproj, c_in_main, mode="nt", add=dz2, add_scale=DN_ALPHA, name="c_in_bwd", tn=1024, tk=4096)
            dy1 = matmul(dgb, c_in_gate, mode="nt", add=dy1, name="c_gate_bwd", tn=1024)
            g_c_main = matmul(s["y1"], dproj, mode="tn", name="c_in_grad", tm=1024, tn=1024, tk=1024)
            g_c_gate = matmul(s["y1"], dgb, mode="tn", name="c_gate_grad", tm=1024)
        dz1, dg0, db0 = ln_bwd(dy1, s["z1"], ln_g[i, 0], name=f"ln0_bwd_{i}")
        dh, act, dhg, dhu = ffn_bwd(dz1, s["hg1"], s["hu1"], w["ffn1_wg"][i], w["ffn1_wu"][i], w["ffn1_wd"][i], name=f"ffn1_bwd_{i}", tm=FFN_TM, tf=FFN_TF)
        g["ffn1_wg"][i] = matmul(s["x0"], dhg, mode="tn", name=f"ffn1_wg_grad_{i}", **wide)
        g["ffn1_wu"][i] = matmul(s["x0"], dhu, mode="tn", name=f"ffn1_wu_grad_{i}", **wide)
        g["ffn1_wd"][i] = matmul(act, dz1, mode="tn", scale=0.5, name=f"ffn1_wd_grad_{i}", **tall)
        g["ln_g"][i] = jnp.concatenate([dg0, dg1, dg2], axis=0)
        g["ln_b"][i] = jnp.concatenate([db0, db1, db2], axis=0)
    grad_x = dh.reshape(bsz, s_len, d)
    full = {n: jnp.stack(v) if isinstance(v, list) else v[None] for n, v in g.items()}
    full["b_conv_w"] = dcw[None]
    full["c_conv_w"] = jnp.sum(dccw, axis=0)[None]
    dsm_sum = jnp.sum(dsm, axis=0)
    full.update(a_sinks=jnp.sum(dsk, axis=0)[:, :A_HEADS], b_conv_b=dcb, b_wa=_diag_blocks(dwa)[None], b_ba=dba,
                b_wx=_diag_blocks(dwx)[None], b_bx=dbxb, b_lam=dlam, c_a_log=dsm_sum[None, :C_HEADS, 0],
                c_dt_bias=dsm_sum[None, :C_HEADS, 1], c_norm_g=jnp.sum(dsm_sum[C_HEADS:], axis=0)[None])

    small_f32_rows = SMALL_F32 // LANES
    repl_flat = _flat_pad([full[n] for n in REPL], F32, SMALL_F32 - n_small)
    small8 = jnp.concatenate([_flat8_pad([_split(full[n], 2) for n in SMALL_NAMES], F32, n_small),
                              jnp.broadcast_to(repl_flat, (N_DEV,) + repl_flat.shape)], axis=1)
    parts = [
        split_cols([(jnp.concatenate([full[n].reshape(-1, N_DEV * f_ff) for n in GROUP_A], axis=0), N_DEV * f_ff)], f_ff,
                   name="split_ffn"),
        jnp.concatenate([_split(full[n], 1).astype(MM).reshape(N_DEV, -1, D_MODEL) for n in GROUP_B], axis=1),
        split_cols([(g_ab_in, AB_PROJ)], AB_PROJ // N_DEV, name="split_ab_in"),
        split_cols([(g_c_main, 4 * C_WIDTH), (g_c_gate, 2 * C_HEADS)], (4 * C_WIDTH + 2 * C_HEADS) // N_DEV,
                   name="split_c_in"),
        _split(full["ple_wp"], 2).astype(MM).reshape(N_DEV, PLE_WP_ROWS, LANES),
        small8.reshape(N_DEV, small_f32_rows, LANES),
    ]
    ra, rb, rc, rd, re, small_parts = exchange_multi(parts, name="exchange_grads")

    def wmv(n, shape2d):
        return [a[pre + n].reshape(shape2d) for pre in ("", "m_", "v_")]

    res = {}
    for i, n in enumerate(GROUP_A):
        res[n] = adamw_rows(ra, i * DEPTH * D_MODEL, *wmv(n, (DEPTH * D_MODEL, f_ff)), name=f"adamw_{n}")
    for n in GROUP_B:
        res[n] = adamw_rows(rb, off_b[n], *wmv(n, (rows_b[n], D_MODEL)), name=f"adamw_{n}", tr=64)
    res["ab_w_in"] = adamw_rows(rc, 0, *wmv("ab_w_in", (D_MODEL, AB_PROJ // N_DEV)), name="adamw_ab_w_in")
    res["c_w_in"] = adamw_rows(rd, 0, *wmv("c_w_in", (D_MODEL, shapes["c_w_in"][2])), name="adamw_c_w_in")
    res["ple_wp"] = adamw_rows(re, 0, *wmv("ple_wp", (PLE_WP_ROWS, LANES)), name="adamw_ple_wp")
    res_small = adamw_rows(small_parts, 0, *[_flat_pad([a[pre + n] for n in small_all], F32, SMALL_F32).reshape(
        small_f32_rows, LANES) for pre in ("", "m_", "v_")], name="adamw_small", tr=576)
    kinds = []
    for k in range(4):
        kd = {n: res[n][k].reshape(shapes[n]) for n in res}
        kd.update(_take(res_small[k].reshape(-1), small_all, shapes))
        kinds.append(kd)
    loss = lax.psum(loss_part[0, 0], ("x", "y", "c"))
    return (loss, grad_x, *[kinds[0][n] for n in WEIGHTS], *[kinds[1][n] for n in WEIGHTS],
            *[kinds[2][n] for n in WEIGHTS], *[kinds[3][n] for n in WEIGHTS])


BIG_ROWS = 5632
SMALL_F32 = 73728


def _flat_pad(arrs, dtype, total):
    flat = jnp.concatenate([z.astype(dtype).reshape(-1) for z in arrs])
    return jnp.pad(flat, (0, total - flat.shape[0]))


def _flat8_pad(arrs, dtype, total):
    flat = jnp.concatenate([z.astype(dtype).reshape(N_DEV, -1) for z in arrs], axis=1)
    return jnp.pad(flat, ((0, 0), (0, total - flat.shape[1])))


def _bits(z):
    return z if MM == F32 else _as_bf16_bits(z)


def _unbits(z):
    return z if MM == F32 else _from_bf16_bits(z)


def _take(flat, names, shapes):
    out, off = {}, 0
    for n in names:
        sz = int(np.prod(shapes[n]))
        out[n] = flat[..., off:off + sz].reshape(flat.shape[:-1] + tuple(shapes[n]))
        off += sz
    return out


def _step(a):
    x, p = a["x"], a["p"]
    bsz, s_len, d = x.shape
    t = bsz * s_len
    x2 = x.reshape(t, d)
    tgt = a["loss_target"].reshape(t, d)
    p2 = p.reshape(DEPTH, t, D_PLE)
    shapes = {n: a[n].shape for n in WEIGHTS}
    big_names = [n for n, _ in BIG]
    small_names = [n for n, _ in SMALL]
    n_small = sum(int(np.prod(shapes[n])) for n in small_names)
    bits_per = 1 if MM == F32 else 2
    small_rows = -(-(n_small * bits_per) // PACK_ALIGN) * (PACK_ALIGN // PACK_COLS)

    send = jnp.concatenate([
        _flat_pad([a[n] for n in big_names], MM, BIG_ROWS * PACK_COLS).reshape(BIG_ROWS, PACK_COLS),
        _bits(_flat_pad([a[n] for n in small_names], F32, small_rows * PACK_COLS // bits_per)).reshape(small_rows, PACK_COLS),
    ], axis=0)
    gathered = all_gather(send, name="gather_weights")
    wb = _take(gathered[:, :BIG_ROWS].reshape(N_DEV, -1), big_names, shapes)
    ws = _take(_unbits(gathered[:, BIG_ROWS:].reshape(N_DEV, -1)), small_names, shapes)
    w = {n: _join(wb[n], ax) for n, ax in BIG}
    w.update({n: _join(ws[n], ax) for n, ax in SMALL})
    ln_g, ln_b = w["ln_g"], w["ln_b"]
    c_in_main = w["c_w_in"][0][:, :4 * C_WIDTH]
    c_in_gate = jnp.pad(w["c_w_in"][0][:, 4 * C_WIDTH:], ((0, 0), (0, 128 - 2 * C_HEADS)))
    wa_d, wx_d = _dense_blocks(a["b_wa"][0]), _dense_blocks(a["b_wx"][0])
    lru_w = (w["b_conv_w"][0], a["b_conv_b"][0], wa_d, a["b_ba"][0], wx_d, a["b_bx"][0], a["b_lam"][0])
    gdc_w = (a["c_a_log"][0], a["c_dt_bias"][0], a["c_norm_g"][0])

    h = x2
    saved = []
    for i in range(DEPTH):
        s = {"x0": h}
        s["y1"], s["z1"], s["hg1"], s["hu1"] = ffn_fwd(h, w["ffn1_wg"][i], w["ffn1_wu"][i], w["ffn1_wd"][i], ln_g[i, 0], ln_b[i, 0],
                                   name=f"ffn1_fwd_{i}")
        if i == 0:
            s["proj"] = matmul(s["y1"], w["ab_w_in"][0], mode="nn", name="ab_in_fwd")
            ya = attn_fwd(s["proj"], a["a_sinks"][0], bsz, name="attn_fwd")
            yb = lru_fwd(s["proj"], *lru_w, bsz, name="lru_fwd")
            s["mix"] = jnp.concatenate([ya, yb], axis=1)
            w_out = w["ab_w_out"][0]
        else:
            s["proj"] = matmul(s["y1"], c_in_main, mode="nn", name="c_in_fwd")
            s["gates"] = matmul(s["y1"], c_in_gate, mode="nn", name="c_gate_fwd")
            s["qkv"] = gdc_pre_fwd(s["proj"], w["c_conv_w"][0], bsz, name="gdc_pre_fwd")
            s["mix"], s["o_pre"], s["vnew"], s["tmat"], s["states"] = gdc_fwd(
                s["qkv"], s["proj"], s["gates"], *gdc_w, bsz, name="gdc_fwd")
            w_out = w["c_w_out"][0]
        s["y2"], s["z2"] = mm_ln_fwd(s["mix"], w_out, s["y1"], ln_g[i, 1], ln_b[i, 1], name=f"mix_out_fwd_{i}")
        s["y3"], s["z3"], s["hg2"], s["hu2"] = ffn_fwd(s["y2"], w["ffn2_wg"][i], w["ffn2_wu"][i], w["ffn2_wd"][i], ln_g[i, 2], ln_b[i, 2],
                                   name=f"ffn2_fwd_{i}")
        h = ple_fwd(s["y3"], p2[i], w["ple_wg"][i], a["ple_bg"][i], w["ple_wp"][i], name=f"ple_fwd_{i}")
        saved.append(s)
    loss_part, dh = loss_fwd_bwd(h, tgt, name="loss")

    g = {n: [None] * shapes[n][0] for n in ("ffn1_wg", "ffn1_wu", "ffn1_wd", "ffn2_wg", "ffn2_wu", "ffn2_wd", "ln_g",
                                             "ln_b", "ple_wg", "ple_bg", "ple_wp")}
    wide = dict(tm=1024, tn=1408, tk=1024)
    tall = dict(tm=1408, tn=1024, tk=1024)
    for i in reversed(range(DEPTH)):
        s = saved[i]
        dy3, dt, de, dbg = ple_bwd(dh, s["y3"], p2[i], w["ple_wg"][i], a["ple_bg"][i], w["ple_wp"][i], name=f"ple_bwd_{i}")
        g["ple_wg"][i] = matmul(s["y3"], dt, mode="tn", name=f"ple_wg_grad_{i}")
        g["ple_wp"][i] = matmul(p2[i], de, mode="tn", name=f"ple_wp_grad_{i}")
        g["ple_bg"][i] = dbg[0]
        dz3, dg2, db2 = ln_bwd(dy3, s["z3"], ln_g[i, 2], name=f"ln2_bwd_{i}")
        dy2, act, dhg, dhu = ffn_bwd(dz3, s["hg2"], s["hu2"], w["ffn2_wg"][i], w["ffn2_wu"][i], w["ffn2_wd"][i], name=f"ffn2_bwd_{i}", tm=FFN_TM, tf=FFN_TF)
        g["ffn2_wg"][i] = matmul(s["y2"], dhg, mode="tn", name=f"ffn2_wg_grad_{i}", **wide)
        g["ffn2_wu"][i] = matmul(s["y2"], dhu, mode="tn", name=f"ffn2_wu_grad_{i}", **wide)
        g["ffn2_wd"][i] = matmul(act, dz3, mode="tn", scale=0.5, name=f"ffn2_wd_grad_{i}", **tall)
        dz2, dg1, db1 = ln_bwd(dy2, s["z2"], ln_g[i, 1], name=f"ln1_bwd_{i}")
        if i == 0:
            dmix = matmul(dz2, w["ab_w_out"][0], mode="nt", name="ab_out_bwd")
            g["ab_w_out"] = matmul(s["mix"], dz2, mode="tn", name="ab_out_grad")
            dq, dk, dv, dsk = attn_bwd(s["proj"], a["a_sinks"][0], dmix, bsz, name="attn_bwd")
            dbx, dbgate, dcw, dcb, dwa, dba, dwx, dbxb, dlam = lru_bwd(s["proj"], *lru_w, dmix, bsz, name="lru_bwd")
            dproj = jnp.concatenate([dq, dk, dv, dbx, dbgate], axis=1).astype(MM)
            dy1 = matmul(dproj, w["ab_w_in"][0], mode="nt", add=dz2, add_scale=DN_ALPHA, name="ab_in_bwd")
            g["ab_w_in"] = matmul(s["y1"], dproj, mode="tn", name="ab_in_grad")
        else:
            dmix = matmul(dz2, w["c_w_out"][0], mode="nt", name="c_out_bwd")
            g["c_w_out"] = matmul(s["mix"], dz2, mode="tn", name="c_out_grad")
            dqkv, dzc, dgates, dsm = gdc_bwd(s["qkv"], s["proj"], s["gates"], *gdc_w, s["o_pre"], s["vnew"], s["tmat"],
                                             s["states"], dmix, bsz, name="gdc_bwd")
            draw, dccw = gdc_pre_bwd(s["proj"], w["c_conv_w"][0], dqkv, bsz, name="gdc_pre_bwd")
            dproj = jnp.concatenate([draw, dzc], axis=1).astype(MM)
            dgb = dgates.astype(MM)
            dy1 = matmul(dproj, c_in_main, mode="nt", add=dz2, add_scale=DN_ALPHA, name="c_in_bwd")
            dy1 = matmul(dgb, c_in_gate, mode="nt", add=dy1, name="c_gate_bwd")
            g["c_w_in"] = jnp.concatenate([matmul(s["y1"], dproj, mode="tn", name="c_in_grad"),
                                           matmul(s["y1"], dgb, mode="tn", name="c_gate_grad")[:, :2 * C_HEADS]], axis=1)
        dz1, dg0, db0 = ln_bwd(dy1, s["z1"], ln_g[i, 0], name=f"ln0_bwd_{i}")
        dh, act, dhg, dhu = ffn_bwd(dz1, s["hg1"], s["hu1"], w["ffn1_wg"][i], w["ffn1_wu"][i], w["ffn1_wd"][i], name=f"ffn1_bwd_{i}", tm=FFN_TM, tf=FFN_TF)
        g["ffn1_wg"][i] = matmul(s["x0"], dhg, mode="tn", name=f"ffn1_wg_grad_{i}", **wide)
        g["ffn1_wu"][i] = matmul(s["x0"], dhu, mode="tn", name=f"ffn1_wu_grad_{i}", **wide)
        g["ffn1_wd"][i] = matmul(act, dz1, mode="tn", scale=0.5, name=f"ffn1_wd_grad_{i}", **tall)
        g["ln_g"][i] = jnp.concatenate([dg0, dg1, dg2], axis=0)
        g["ln_b"][i] = jnp.concatenate([db0, db1, db2], axis=0)
    grad_x = dh.reshape(bsz, s_len, d)
    full = {n: jnp.stack(v) if isinstance(v, list) else v[None] for n, v in g.items()}
    full["b_conv_w"] = dcw[None]
    full["c_conv_w"] = jnp.sum(dccw, axis=0)[None]
    dsm_sum = jnp.sum(dsm, axis=0)
    full.update(a_sinks=jnp.sum(dsk, axis=0)[:, :A_HEADS], b_conv_b=dcb, b_wa=_diag_blocks(dwa)[None], b_ba=dba,
                b_wx=_diag_blocks(dwx)[None], b_bx=dbxb, b_lam=dlam, c_a_log=dsm_sum[None, :C_HEADS, 0],
                c_dt_bias=dsm_sum[None, :C_HEADS, 1], c_norm_g=jnp.sum(dsm_sum[C_HEADS:], axis=0)[None])

    small_cols = SMALL_F32 * bits_per // PACK_COLS
    repl_flat = _flat_pad([full[n] for n in REPL], F32, SMALL_F32 - n_small)
    small8 = jnp.concatenate([_flat8_pad([_split(full[n], ax) for n, ax in SMALL], F32, n_small),
                              jnp.broadcast_to(repl_flat, (N_DEV,) + repl_flat.shape)], axis=1)
    parts = jnp.concatenate([
        _flat8_pad([_split(full[n], ax) for n, ax in BIG], MM, BIG_ROWS * PACK_COLS).reshape(N_DEV, BIG_ROWS, PACK_COLS),
        _bits(small8).reshape(N_DEV, small_cols, PACK_COLS)], axis=1)
    recv = all_to_all(parts, name="exchange_grads")

    def mine(prefix, names, dtype_total):
        return _flat_pad([a[prefix + n] for n in names], F32, dtype_total)

    outs = {}
    big_total = BIG_ROWS * PACK_COLS
    res_big = adamw_sum(recv, *[mine(pre, big_names, big_total).reshape(BIG_ROWS, PACK_COLS) for pre in ("", "m_", "v_")],
                        name="adamw_big")
    small_all = small_names + REPL
    cols_f32 = PACK_COLS // bits_per
    res_small = adamw_sum(_unbits(recv[:, BIG_ROWS:]).reshape(N_DEV, small_cols, cols_f32),
                          *[mine(pre, small_all, SMALL_F32).reshape(small_cols, cols_f32) for pre in ("", "m_", "v_")],
                          name="adamw_small")
    kinds = []
    for rb, rs in zip(res_big, res_small):
        k = _take(rb.reshape(-1), big_names, shapes)
        k.update(_take(rs.reshape(-1), small_all, shapes))
        kinds.append(k)
    loss = lax.psum(loss_part[0, 0], ("x", "y", "c"))
    return (loss, grad_x, *[kinds[0][n] for n in WEIGHTS], *[kinds[1][n] for n in WEIGHTS],
            *[kinds[2][n] for n in WEIGHTS], *[kinds[3][n] for n in WEIGHTS])
```

```python
import functools
import math

import numpy as np
import jax
import jax.numpy as jnp
from jax import lax
from jax.experimental import pallas as pl
from jax.experimental.pallas import tpu as pltpu

F32 = jnp.float32
MM = jnp.bfloat16
HI = lax.Precision.HIGHEST

D_MODEL = 1024
D_FF = 2816
D_PLE = 256
DEPTH = 2
CHUNK = 64
A_HEADS = 8
A_KV_HEADS = 2
A_GROUP = 4
A_HEAD_DIM = 64
A_WIDTH = 512
A_KV_WIDTH = 128
B_WIDTH = 512
B_BLOCK = 64
RG_C = 8.0
AB_PROJ = 1792
C_HEADS = 8
C_HEAD_DIM = 128
C_WIDTH = 1024
DN_ALPHA = (2.0 * DEPTH) ** 0.25
LN_EPS = 1e-5
NORM_EPS = 1e-6
NEG = -1e30
ADAM_LR = 0.001
ADAM_B1 = 0.9
ADAM_B2 = 0.999
ADAM_EPS = 1e-08
ADAM_WD = 0.01
ADAM_STEP = 10
N_DEV = 8
VMEM_LIMIT = 56 * 1024 * 1024

NN = ((1,), (0,))
NT = ((1,), (1,))
TN = ((0,), (0,))


def _pcall(body, **kw):
    return pl.pallas_call(body, **kw)


def _cp(*sem):
    return pltpu.CompilerParams(dimension_semantics=sem, vmem_limit_bytes=VMEM_LIMIT)


MESH_ID = pl.DeviceIdType.MESH
_FLIPS = [(0, 0, 1), (1, 0, 0), (0, 1, 0), (1, 1, 0), (1, 0, 1), (0, 1, 1), (1, 1, 1)]


def _me():
    return lax.axis_index("x"), lax.axis_index("y"), lax.axis_index("c")


def _flip(coord, d):
    return 1 - coord if d else coord


def _side_copies(kind, x_refs, o_refs, send_sems, recv_sems, local_sems, start):
    x, y, c = _me()
    mine = 4 * x + 2 * y + c
    for gi, (x_ref, o_ref) in enumerate(zip(x_refs, o_refs)):
        src_own = x_ref if kind == "gather" else x_ref.at[mine]
        own = pltpu.make_async_copy(src_own, o_ref.at[mine], local_sems.at[gi])
        own.start() if start else own.wait()
        for k, (dx, dy, dc) in enumerate(_FLIPS):
            px, py, pc = _flip(x, dx), _flip(y, dy), _flip(c, dc)
            src = x_ref if kind == "gather" else x_ref.at[4 * px + 2 * py + pc]
            cp = pltpu.make_async_remote_copy(
                src_ref=src, dst_ref=o_ref.at[mine], send_sem=send_sems.at[7 * gi + k], recv_sem=recv_sems.at[7 * gi + k],
                device_id=(px, py, pc), device_id_type=MESH_ID)
            cp.start() if start else cp.wait()


def _call(body, args, side, grid, **kw):
    if side is None:
        return _pcall(body, grid=grid, **kw)(*args)
    kind, arrs = side
    ns, n_in, n_out = len(arrs), len(args), len(kw["out_specs"])
    scratch = list(kw.get("scratch_shapes", []))
    n_scr = len(scratch)

    def edge(at_end):
        conds = [pl.program_id(ax) == (n - 1 if at_end else 0) for ax, n in enumerate(grid)]
        return functools.reduce(jnp.logical_and, conds)

    def wrapped(*refs):
        ins, sx = refs[:n_in], refs[n_in:n_in + ns]
        outs, so = refs[n_in + ns:n_in + ns + n_out], refs[n_in + ns + n_out:n_in + 2 * ns + n_out]
        rest = refs[n_in + 2 * ns + n_out:]
        scr, sems = rest[:n_scr], rest[n_scr:]

        @pl.when(edge(False))
        def _():
            _side_copies(kind, sx, so, *sems, start=True)

        body(*ins, *outs, *scr)

        @pl.when(edge(True))
        def _():
            _side_copies(kind, sx, so, *sems, start=False)

    hbm = pl.BlockSpec(memory_space=pl.ANY)
    side_shapes = [jax.ShapeDtypeStruct(((N_DEV,) if kind == "gather" else ()) + z.shape, z.dtype) for z in arrs]
    kw = dict(kw)
    kw["in_specs"] = list(kw["in_specs"]) + [hbm] * ns
    kw["out_specs"] = list(kw["out_specs"]) + [hbm] * ns
    kw["out_shape"] = list(kw["out_shape"]) + side_shapes
    kw["scratch_shapes"] = scratch + [pltpu.SemaphoreType.DMA((7 * ns,)), pltpu.SemaphoreType.DMA((7 * ns,)),
                                      pltpu.SemaphoreType.DMA((ns,))]
    kw["compiler_params"] = _cp(*["arbitrary"] * len(grid))
    res = _pcall(wrapped, grid=grid, **kw)(*args, *arrs)
    return list(res[:n_out]), list(res[n_out:])


def _dot(a, b, dims=NN, precision=None):
    return lax.dot_general(a, b, (dims, ((), ())), preferred_element_type=F32, precision=precision)


def _mdot(a, b, dims=NN):
    return _dot(a.astype(MM), b.astype(MM), dims)


def _tile(n, pref):
    if n <= pref:
        return n
    for c in range(pref - pref % 128, 0, -128):
        if n % c == 0:
            return c
    return n


def _sigmoid(x):
    return 1.0 / (1.0 + jnp.exp(-x))


def _softplus(x):
    return jnp.maximum(x, 0.0) + jnp.log(1.0 + jnp.exp(-jnp.abs(x)))


def _ln_stats(z):
    mu = jnp.mean(z, axis=-1, keepdims=True)
    zc = z - mu
    var = jnp.mean(zc * zc, axis=-1, keepdims=True)
    return zc, lax.rsqrt(var + LN_EPS)


def matmul(a, b, *, mode, name, tm=512, tn=512, tk=512, out_dtype=F32, scale=None, add=None, add_scale=1.0, side=None):
    if mode == "nn":
        (m, kk), (_, n) = a.shape, b.shape
        dims = NN
    elif mode == "nt":
        (m, kk), (n, _) = a.shape, b.shape
        dims = NT
    else:
        (kk, m), (_, n) = a.shape, b.shape
        dims = TN
    tm, tn, tk = _tile(m, tm), _tile(n, tn), _tile(kk, tk)
    if mode == "nn":
        a_spec = pl.BlockSpec((tm, tk), lambda i, j, k: (i, k))
        b_spec = pl.BlockSpec((tk, tn), lambda i, j, k: (k, j))
    elif mode == "nt":
        a_spec = pl.BlockSpec((tm, tk), lambda i, j, k: (i, k))
        b_spec = pl.BlockSpec((tn, tk), lambda i, j, k: (j, k))
    else:
        a_spec = pl.BlockSpec((tk, tm), lambda i, j, k: (k, i))
        b_spec = pl.BlockSpec((tk, tn), lambda i, j, k: (k, j))
    nk = kk // tk
    o_spec = pl.BlockSpec((tm, tn), lambda i, j, k: (i, j))
    has_add = add is not None

    def body(*refs):
        if has_add:
            a_ref, b_ref, add_ref, o_ref, acc_ref = refs
        else:
            a_ref, b_ref, o_ref, acc_ref = refs
        k = pl.program_id(2)

        @pl.when(k == 0)
        def _():
            acc_ref[...] = jnp.zeros_like(acc_ref)

        acc_ref[...] += _mdot(a_ref[...], b_ref[...], dims)

        @pl.when(k == nk - 1)
        def _():
            r = acc_ref[...]
            if scale is not None:
                r = r * scale
            if has_add:
                r = r + add_scale * add_ref[...].astype(F32)
            o_ref[...] = r.astype(out_dtype)

    ins = [a, b] + ([add] if has_add else [])
    in_specs = [a_spec, b_spec] + ([o_spec] if has_add else [])
    res = _call(
        body, ins, side, (m // tm, n // tn, nk), name=name, in_specs=in_specs, out_specs=[o_spec],
        out_shape=[jax.ShapeDtypeStruct((m, n), out_dtype)], scratch_shapes=[pltpu.VMEM((tm, tn), F32)],
        compiler_params=_cp("parallel", "parallel", "arbitrary"),
    )
    return res[0] if side is None else (res[0][0], res[1])


def ffn_fwd(x, wg, wu, wd, g, b, *, name, tm=512, tf=256, side=None):
    t, d = x.shape
    f = wg.shape[1]
    tm = min(tm, t)
    nj = f // tf

    def body(x_ref, wg_ref, wu_ref, wd_ref, g_ref, b_ref, y_ref, z_ref, hg_ref, hu_ref, xb_ref, acc_ref):
        j = pl.program_id(1)

        @pl.when(j == 0)
        def _():
            xb_ref[...] = x_ref[...].astype(MM)
            acc_ref[...] = jnp.zeros_like(acc_ref)

        xb = xb_ref[...]
        hg = _dot(xb, wg_ref[...])
        hu = _dot(xb, wu_ref[...])
        hg_ref[...] = hg.astype(MM)
        hu_ref[...] = hu.astype(MM)
        act = (hg * _sigmoid(hg) * hu).astype(MM)
        acc_ref[...] += _dot(act, wd_ref[...])

        @pl.when(j == nj - 1)
        def _():
            z = DN_ALPHA * x_ref[...] + 0.5 * acc_ref[...]
            z_ref[...] = z
            zc, rstd = _ln_stats(z)
            y_ref[...] = zc * rstd * g_ref[...] + b_ref[...]

    row = pl.BlockSpec((tm, d), lambda i, j: (i, 0))
    hid = pl.BlockSpec((tm, tf), lambda i, j: (i, j))
    vec = pl.BlockSpec((1, d), lambda i, j: (0, 0))
    return _call(
        body, (x, wg, wu, wd, g.reshape(1, d), b.reshape(1, d)), side, (t // tm, nj), name=name,
        in_specs=[row, pl.BlockSpec((d, tf), lambda i, j: (0, j)), pl.BlockSpec((d, tf), lambda i, j: (0, j)),
                  pl.BlockSpec((tf, d), lambda i, j: (j, 0)), vec, vec],
        out_specs=[row, row, hid, hid],
        out_shape=[jax.ShapeDtypeStruct((t, d), F32), jax.ShapeDtypeStruct((t, d), F32),
                   jax.ShapeDtypeStruct((t, f), MM), jax.ShapeDtypeStruct((t, f), MM)],
        scratch_shapes=[pltpu.VMEM((tm, d), MM), pltpu.VMEM((tm, d), F32)],
        compiler_params=_cp("parallel", "arbitrary"),
    )


def _ln_bwd_tile(dy, z, g):
    zc, rstd = _ln_stats(z)
    xh = zc * rstd
    dxh = dy * g
    m1 = jnp.mean(dxh, axis=-1, keepdims=True)
    m2 = jnp.mean(dxh * xh, axis=-1, keepdims=True)
    return rstd * (dxh - m1 - xh * m2), jnp.sum(dy * xh, axis=0, keepdims=True), jnp.sum(dy, axis=0, keepdims=True)


def ffn_bwd(dz, hg, hu, wg, wu, wd, *, name, tm=512, tf=256, side=None, ln=None):
    t, d = dz.shape
    f = wg.shape[1]
    tm = min(tm, t)
    nj = f // tf
    n_in = 6 + (2 if ln else 0)

    def body(*refs):
        dz_ref, hg_ref, hu_ref, wg_ref, wu_ref, wd_ref = refs[:6]
        dx_ref, act_ref, dhg_ref, dhu_ref = refs[n_in:n_in + 4]
        dfb_ref, acc_ref = refs[-2:]
        i, j = pl.program_id(0), pl.program_id(1)

        @pl.when(j == 0)
        def _():
            dfb_ref[...] = (0.5 * dz_ref[...]).astype(MM)
            acc_ref[...] = jnp.zeros_like(acc_ref)

        hg = hg_ref[...].astype(F32)
        hu = hu_ref[...].astype(F32)
        s = _sigmoid(hg)
        dact = _dot(dfb_ref[...], wd_ref[...], NT)
        sg = hg * s
        act_ref[...] = (sg * hu).astype(MM)
        dhu = (dact * sg).astype(MM)
        dhg = (dact * hu * (s + sg * (1.0 - s))).astype(MM)
        dhu_ref[...] = dhu
        dhg_ref[...] = dhg
        acc_ref[...] += _dot(dhg, wg_ref[...], NT) + _dot(dhu, wu_ref[...], NT)

        if ln:
            z_ref, g_ref = refs[6:8]
            dg_ref, db_ref = refs[n_in + 4:n_in + 6]

            @pl.when((i == 0) & (j == 0))
            def _():
                dg_ref[...] = jnp.zeros_like(dg_ref)
                db_ref[...] = jnp.zeros_like(db_ref)

        @pl.when(j == nj - 1)
        def _():
            dx = DN_ALPHA * dz_ref[...] + acc_ref[...]
            if ln:
                dx, dg, db = _ln_bwd_tile(dx, z_ref[...], g_ref[...])
                dg_ref[...] += dg
                db_ref[...] += db
            dx_ref[...] = dx

    row = pl.BlockSpec((tm, d), lambda i, j: (i, 0))
    hid = pl.BlockSpec((tm, tf), lambda i, j: (i, j))
    vec = pl.BlockSpec((1, d), lambda i, j: (0, 0))
    vshape = jax.ShapeDtypeStruct((1, d), F32)
    return _call(
        body, (dz, hg, hu, wg, wu, wd) + ((ln[0], ln[1].reshape(1, d)) if ln else ()), side, (t // tm, nj), name=name,
        in_specs=[row, hid, hid, pl.BlockSpec((d, tf), lambda i, j: (0, j)), pl.BlockSpec((d, tf), lambda i, j: (0, j)),
                  pl.BlockSpec((tf, d), lambda i, j: (j, 0))] + ([row, vec] if ln else []),
        out_specs=[row, hid, hid, hid] + ([vec, vec] if ln else []),
        out_shape=[jax.ShapeDtypeStruct((t, d), F32)] + [jax.ShapeDtypeStruct((t, f), MM)] * 3 + ([vshape, vshape] if ln else []),
        scratch_shapes=[pltpu.VMEM((tm, d), MM), pltpu.VMEM((tm, d), F32)],
        compiler_params=_cp("arbitrary" if ln else "parallel", "arbitrary"),
    )


def ln_bwd(dy, z, g, *, name, tm=512):
    t, d = z.shape
    tm = min(tm, t)

    def body(dy_ref, z_ref, g_ref, dz_ref, dg_ref, db_ref):
        i = pl.program_id(0)

        @pl.when(i == 0)
        def _():
            dg_ref[...] = jnp.zeros_like(dg_ref)
            db_ref[...] = jnp.zeros_like(db_ref)

        dy = dy_ref[...]
        zc, rstd = _ln_stats(z_ref[...])
        xh = zc * rstd
        dg_ref[...] += jnp.sum(dy * xh, axis=0, keepdims=True)
        db_ref[...] += jnp.sum(dy, axis=0, keepdims=True)
        dxh = dy * g_ref[...]
        m1 = jnp.mean(dxh, axis=-1, keepdims=True)
        m2 = jnp.mean(dxh * xh, axis=-1, keepdims=True)
        dz_ref[...] = rstd * (dxh - m1 - xh * m2)

    row = pl.BlockSpec((tm, d), lambda i: (i, 0))
    vec = pl.BlockSpec((1, d), lambda i: (0, 0))
    return _pcall(
        body, name=name, grid=(t // tm,), in_specs=[row, row, vec], out_specs=[row, vec, vec],
        out_shape=[jax.ShapeDtypeStruct((t, d), F32), jax.ShapeDtypeStruct((1, d), F32), jax.ShapeDtypeStruct((1, d), F32)],
        compiler_params=_cp("arbitrary"),
    )(dy, z, g.reshape(1, d))


def mm_ln_fwd(a, w, res, g, b, *, name, tm=512):
    t, kk = a.shape
    d = w.shape[1]
    tm = min(tm, t)

    def body(a_ref, w_ref, res_ref, g_ref, b_ref, y_ref, z_ref):
        z = DN_ALPHA * res_ref[...] + _mdot(a_ref[...], w_ref[...])
        z_ref[...] = z
        zc, rstd = _ln_stats(z)
        y_ref[...] = zc * rstd * g_ref[...] + b_ref[...]

    row = pl.BlockSpec((tm, d), lambda i: (i, 0))
    vec = pl.BlockSpec((1, d), lambda i: (0, 0))
    return _pcall(
        body, name=name, grid=(t // tm,),
        in_specs=[pl.BlockSpec((tm, kk), lambda i: (i, 0)), pl.BlockSpec((kk, d), lambda i: (0, 0)), row, vec, vec],
        out_specs=[row, row],
        out_shape=[jax.ShapeDtypeStruct((t, d), F32), jax.ShapeDtypeStruct((t, d), F32)],
        compiler_params=_cp("parallel"),
    )(a, w, res, g.reshape(1, d), b.reshape(1, d))


def ple_fwd(y, p, wg, bg, wp, *, name, tm=512):
    t, d = y.shape
    dp = p.shape[1]
    tm = min(tm, t)

    def body(y_ref, p_ref, wg_ref, bg_ref, wp_ref, o_ref):
        yv = y_ref[...]
        gate = _sigmoid(_mdot(yv, wg_ref[...]) + bg_ref[...])
        o_ref[...] = yv + gate * _mdot(p_ref[...], wp_ref[...])

    row = pl.BlockSpec((tm, d), lambda i: (i, 0))
    return _pcall(
        body, name=name, grid=(t // tm,),
        in_specs=[row, pl.BlockSpec((tm, dp), lambda i: (i, 0)), pl.BlockSpec((d, d), lambda i: (0, 0)),
                  pl.BlockSpec((1, d), lambda i: (0, 0)), pl.BlockSpec((dp, d), lambda i: (0, 0))],
        out_specs=row, out_shape=jax.ShapeDtypeStruct((t, d), F32), compiler_params=_cp("parallel"),
    )(y, p, wg, bg.reshape(1, d), wp)


def ple_bwd(do, y, p, wg, bg, wp, z, g, *, name, tm=512):
    t, d = y.shape
    dp = p.shape[1]
    tm = min(tm, t)

    def body(do_ref, y_ref, p_ref, wg_ref, bg_ref, wp_ref, z_ref, g_ref, dz_ref, dt_ref, de_ref, dbg_ref, dg_ref, db_ref):
        i = pl.program_id(0)

        @pl.when(i == 0)
        def _():
            for ref in (dbg_ref, dg_ref, db_ref):
                ref[...] = jnp.zeros_like(ref)

        dov = do_ref[...]
        gate = _sigmoid(_mdot(y_ref[...], wg_ref[...]) + bg_ref[...])
        emb = _mdot(p_ref[...], wp_ref[...])
        dt = dov * emb * gate * (1.0 - gate)
        dbg_ref[...] += jnp.sum(dt, axis=0, keepdims=True)
        dtb = dt.astype(MM)
        dt_ref[...] = dtb
        de_ref[...] = (dov * gate).astype(MM)
        dz, dg, db = _ln_bwd_tile(dov + _dot(dtb, wg_ref[...], NT), z_ref[...], g_ref[...])
        dz_ref[...] = dz
        dg_ref[...] += dg
        db_ref[...] += db

    row = pl.BlockSpec((tm, d), lambda i: (i, 0))
    vec = pl.BlockSpec((1, d), lambda i: (0, 0))
    vshape = jax.ShapeDtypeStruct((1, d), F32)
    return _pcall(
        body, name=name, grid=(t // tm,),
        in_specs=[row, row, pl.BlockSpec((tm, dp), lambda i: (i, 0)), pl.BlockSpec((d, d), lambda i: (0, 0)),
                  vec, pl.BlockSpec((dp, d), lambda i: (0, 0)), row, vec],
        out_specs=[row, row, row, vec, vec, vec],
        out_shape=[jax.ShapeDtypeStruct((t, d), F32), jax.ShapeDtypeStruct((t, d), MM),
                   jax.ShapeDtypeStruct((t, d), MM), vshape, vshape, vshape],
        compiler_params=_cp("arbitrary"),
    )(do, y, p, wg, bg.reshape(1, d), wp, z, g.reshape(1, d))


def loss_fwd_bwd(y, tgt, *, name, tm=512):
    t, d = y.shape
    tm = min(tm, t)

    def body(y_ref, t_ref, l_ref, dy_ref):
        i = pl.program_id(0)

        @pl.when(i == 0)
        def _():
            l_ref[...] = jnp.zeros_like(l_ref)

        err = y_ref[...] - t_ref[...]
        dy_ref[...] = err * (1.0 / d)
        l_ref[...] += (0.5 / d) * jnp.sum(jnp.sum(err * err, axis=1, keepdims=True), axis=0, keepdims=True)

    row = pl.BlockSpec((tm, d), lambda i: (i, 0))
    return _pcall(
        body, name=name, grid=(t // tm,), in_specs=[row, row],
        out_specs=[pl.BlockSpec((1, 128), lambda i: (0, 0)), row],
        out_shape=[jax.ShapeDtypeStruct((1, 128), F32), jax.ShapeDtypeStruct((t, d), F32)],
        compiler_params=_cp("arbitrary"),
    )(y, tgt)


def _shift_dn(x, s, row):
    return x if s == 0 else jnp.where(row >= s, pltpu.roll(x, s, 0), 0.0)


def _shift_up(x, s, row):
    n = x.shape[0]
    return x if s == 0 else jnp.where(row < n - s, pltpu.roll(x, n - s, 0), 0.0)


def _conv_fwd(x, w, row):
    kk = w.shape[0]
    y = w[kk - 1:kk, :] * x
    for j in range(kk - 1):
        y = y + w[j:j + 1, :] * _shift_dn(x, kk - 1 - j, row)
    return y


def _conv_bwd(x, w, dy, row):
    kk = w.shape[0]
    dx = w[kk - 1:kk, :] * dy
    dws = []
    for j in range(kk - 1):
        dx = dx + w[j:j + 1, :] * _shift_up(dy, kk - 1 - j, row)
        dws.append(jnp.sum(dy * _shift_dn(x, kk - 1 - j, row), axis=0, keepdims=True))
    dws.append(jnp.sum(dy * x, axis=0, keepdims=True))
    return dx, jnp.concatenate(dws, axis=0)


def _gelu(x):
    c = math.sqrt(2.0 / math.pi)
    th = jnp.tanh(c * (x + 0.044715 * x * x * x))
    return 0.5 * x * (1.0 + th), th


def _gelu_grad(x, th):
    c = math.sqrt(2.0 / math.pi)
    return 0.5 * (1.0 + th) + 0.5 * x * (1.0 - th * th) * c * (1.0 + 3.0 * 0.044715 * x * x)


def _neg_expm1(y):
    ser = -(y * (1.0 + y * (0.5 + y * (1.0 / 6.0 + y * (1.0 / 24.0 + y * (1.0 / 120.0))))))
    return jnp.where(y > -0.05, ser, 1.0 - jnp.exp(y))


def _attn_head(qh, kk, vv, bias, valid, sink):
    s = _mdot(qh, kk, NT) * (A_HEAD_DIM ** -0.5) - bias
    s = jnp.where(valid, s, NEG)
    m = jnp.maximum(jnp.max(s, axis=-1, keepdims=True), sink)
    pr = jnp.exp(s - m)
    den = jnp.sum(pr, axis=-1, keepdims=True) + jnp.exp(sink - m)
    return pr / den, jnp.exp(sink - m) / den


def _attn_valid(n):
    ji = lax.broadcasted_iota(jnp.int32, (1, 3 * CHUNK), 1)
    return (n * CHUNK + ji - 2 * CHUNK) >= 0


def _attn_group_consts(kh, sk_ref):
    rows = A_GROUP * CHUNK
    ri = lax.broadcasted_iota(jnp.int32, (rows, 3 * CHUNK), 0)
    ji = lax.broadcasted_iota(jnp.int32, (rows, 3 * CHUNK), 1)
    dist = jnp.abs((ri & (CHUNK - 1)) + 2 * CHUNK - ji).astype(F32)
    rcol = lax.broadcasted_iota(jnp.int32, (rows, 1), 0)
    slope = jnp.zeros((rows, 1), F32)
    sink = jnp.zeros((rows, 1), F32)
    for gi in range(A_GROUP):
        h = kh * A_GROUP + gi
        inblk = (rcol >= gi * CHUNK) & (rcol < (gi + 1) * CHUNK)
        slope = jnp.where(inblk, 2.0 ** -(h + 1), slope)
        sink = jnp.where(inblk, sk_ref[h], sink)
    return slope * dist, sink


def _stack_heads(x, kh):
    return jnp.concatenate([x[:, (kh * A_GROUP + gi) * 64:(kh * A_GROUP + gi + 1) * 64] for gi in range(A_GROUP)], axis=0)


def _attn_masks(n):
    ci = lax.broadcasted_iota(jnp.int32, (CHUNK, 3 * CHUNK), 0)
    ji = lax.broadcasted_iota(jnp.int32, (CHUNK, 3 * CHUNK), 1)
    dist = jnp.abs(ci + 2 * CHUNK - ji).astype(F32)
    valid = (n * CHUNK + ji - 2 * CHUNK) >= 0
    return dist, valid


def attn_fwd(proj, sinks, bsz, *, name, side=None):
    t = proj.shape[0]
    s_len = t // bsz
    nc = s_len // CHUNK
    pad = 2 * CHUNK

    def body(q_ref, k_ref, v_ref, sk_ref, o_ref, kp_ref, vp_ref):
        kp_ref[0:pad, :] = jnp.zeros((pad, A_KV_WIDTH), F32)
        vp_ref[0:pad, :] = jnp.zeros((pad, A_KV_WIDTH), F32)
        kp_ref[pad:, :] = k_ref[...].astype(F32)
        vp_ref[pad:, :] = v_ref[...].astype(F32)

        consts = [_attn_group_consts(kh, sk_ref) for kh in range(A_KV_HEADS)]

        def chunk(n, carry):
            st = pl.multiple_of(n * CHUNK, CHUNK)
            q = q_ref[pl.ds(st, CHUNK), :].astype(F32)
            kb = kp_ref[pl.ds(st, 3 * CHUNK), :]
            vb = vp_ref[pl.ds(st, 3 * CHUNK), :]
            valid = _attn_valid(n)
            outs = []
            for kh in range(A_KV_HEADS):
                bias, sink = consts[kh]
                pn, _ = _attn_head(_stack_heads(q, kh), kb[:, kh * 64:(kh + 1) * 64], None, bias, valid, sink)
                o = _mdot(pn, vb[:, kh * 64:(kh + 1) * 64])
                outs += [o[gi * CHUNK:(gi + 1) * CHUNK] for gi in range(A_GROUP)]
            o_ref[pl.ds(st, CHUNK), :] = jnp.concatenate(outs, axis=-1)
            return carry

        lax.fori_loop(0, nc, chunk, 0)

    res = _call(
        body, (proj, proj, proj, sinks), side, (bsz,), name=name,
        in_specs=[pl.BlockSpec((s_len, A_WIDTH), lambda b: (b, 0)), pl.BlockSpec((s_len, 128), lambda b: (b, 4)),
                  pl.BlockSpec((s_len, 128), lambda b: (b, 5)), pl.BlockSpec(memory_space=pltpu.SMEM)],
        out_specs=[pl.BlockSpec((s_len, A_WIDTH), lambda b: (b, 0))],
        out_shape=[jax.ShapeDtypeStruct((t, A_WIDTH), F32)],
        scratch_shapes=[pltpu.VMEM((s_len + pad, A_KV_WIDTH), F32), pltpu.VMEM((s_len + pad, A_KV_WIDTH), F32)],
        compiler_params=_cp("parallel"),
    )
    return res[0] if side is None else (res[0][0], res[1])


def attn_bwd(proj, sinks, dcat, bsz, *, name, side=None):
    t = proj.shape[0]
    s_len = t // bsz
    nc = s_len // CHUNK
    pad = 2 * CHUNK

    def body(q_ref, k_ref, v_ref, do_ref, sk_ref, dq_ref, dk_ref, dv_ref, dsk_ref, kp_ref, vp_ref, dkp_ref, dvp_ref):
        kp_ref[0:pad, :] = jnp.zeros((pad, A_KV_WIDTH), F32)
        vp_ref[0:pad, :] = jnp.zeros((pad, A_KV_WIDTH), F32)
        kp_ref[pad:, :] = k_ref[...].astype(F32)
        vp_ref[pad:, :] = v_ref[...].astype(F32)
        dkp_ref[...] = jnp.zeros_like(dkp_ref)
        dvp_ref[...] = jnp.zeros_like(dvp_ref)
        lane = lax.broadcasted_iota(jnp.int32, (1, 128), 1)

        consts = [_attn_group_consts(kh, sk_ref) for kh in range(A_KV_HEADS)]

        def chunk(n, dsk):
            st = pl.multiple_of(n * CHUNK, CHUNK)
            q = q_ref[pl.ds(st, CHUNK), :].astype(F32)
            do = do_ref[pl.ds(st, CHUNK), :]
            kb = kp_ref[pl.ds(st, 3 * CHUNK), :]
            vb = vp_ref[pl.ds(st, 3 * CHUNK), :]
            valid = _attn_valid(n)
            dqs, dks, dvs = [], [], []
            for kh in range(A_KV_HEADS):
                kk = kb[:, kh * 64:(kh + 1) * 64]
                vv = vb[:, kh * 64:(kh + 1) * 64]
                bias, sink = consts[kh]
                qs = _stack_heads(q, kh)
                dos = _stack_heads(do, kh)
                pn, psink = _attn_head(qs, kk, None, bias, valid, sink)
                dp = _mdot(dos, vv, NT)
                rowdot = jnp.sum(pn * dp, axis=-1, keepdims=True)
                ds = pn * (dp - rowdot)
                sink_part = psink * rowdot
                for gi in range(A_GROUP):
                    part = jnp.sum(sink_part[gi * CHUNK:(gi + 1) * CHUNK], axis=0, keepdims=True)
                    dsk = dsk + jnp.where(lane == kh * A_GROUP + gi, -part, 0.0)
                dq = _mdot(ds, kk) * (A_HEAD_DIM ** -0.5)
                dqs += [dq[gi * CHUNK:(gi + 1) * CHUNK] for gi in range(A_GROUP)]
                dks.append(_mdot(ds, qs, TN) * (A_HEAD_DIM ** -0.5))
                dvs.append(_mdot(pn, dos, TN))
            dq_ref[pl.ds(st, CHUNK), :] = jnp.concatenate(dqs, axis=-1)
            dkp_ref[pl.ds(st, 3 * CHUNK), :] += jnp.concatenate(dks, axis=-1)
            dvp_ref[pl.ds(st, 3 * CHUNK), :] += jnp.concatenate(dvs, axis=-1)
            return dsk

        dsk = lax.fori_loop(0, nc, chunk, jnp.zeros((1, 128), F32))
        dsk_ref[0] = dsk
        dk_ref[...] = dkp_ref[pad:, :]
        dv_ref[...] = dvp_ref[pad:, :]

    kv = jax.ShapeDtypeStruct((t, A_KV_WIDTH), F32)
    return _call(
        body, (proj, proj, proj, dcat, sinks), side, (bsz,), name=name,
        in_specs=[pl.BlockSpec((s_len, A_WIDTH), lambda b: (b, 0)), pl.BlockSpec((s_len, 128), lambda b: (b, 4)),
                  pl.BlockSpec((s_len, 128), lambda b: (b, 5)), pl.BlockSpec((s_len, A_WIDTH), lambda b: (b, 0)),
                  pl.BlockSpec(memory_space=pltpu.SMEM)],
        out_specs=[pl.BlockSpec((s_len, A_WIDTH), lambda b: (b, 0)), pl.BlockSpec((s_len, 128), lambda b: (b, 0)),
                   pl.BlockSpec((s_len, 128), lambda b: (b, 0)), pl.BlockSpec((1, 1, 128), lambda b: (b, 0, 0))],
        out_shape=[jax.ShapeDtypeStruct((t, A_WIDTH), F32), kv, kv, jax.ShapeDtypeStruct((bsz, 1, 128), F32)],
        scratch_shapes=[pltpu.VMEM((s_len + pad, A_KV_WIDTH), F32)] * 4,
        compiler_params=_cp("parallel"),
    )


def _lru_gates(x, cw, cb, wa, ba, wx, bx, lam, row):
    xc = _conv_fwd(x, cw, row) + cb
    r = _sigmoid(_mdot(xc, wa) + ba)
    i = _sigmoid(_mdot(xc, wx) + bx)
    sp = _softplus(-lam)
    log_a = -RG_C * r * sp
    a = jnp.exp(log_a)
    mult = jnp.sqrt(_neg_expm1(2.0 * log_a))
    return xc, r, i, sp, a, mult


def _lru_scan(a, u, row):
    n = a.shape[0]
    d = 1
    while d < n:
        a_sh = jnp.where(row >= d, pltpu.roll(a, d, 0), 1.0)
        u_sh = jnp.where(row >= d, pltpu.roll(u, d, 0), 0.0)
        u = a * u_sh + u
        a = a * a_sh
        d *= 2
    return u


def _lru_scan_rev(a, u, row):
    n = a.shape[0]
    d = 1
    while d < n:
        a_sh = jnp.where(row < n - d, pltpu.roll(a, n - d, 0), 1.0)
        u_sh = jnp.where(row < n - d, pltpu.roll(u, n - d, 0), 0.0)
        u = a * u_sh + u
        a = a * a_sh
        d *= 2
    return u


LRU_BLOCK = 128


def _lru_scan_refs(a_ref, u_ref, h_ref, reverse=False):
    nb = a_ref.shape[0] // LRU_BLOCK
    row = lax.broadcasted_iota(jnp.int32, (LRU_BLOCK, 128), 0)

    def block(i, carry):
        bi = nb - 1 - i if reverse else i
        rs = pl.ds(pl.multiple_of(bi * LRU_BLOCK, LRU_BLOCK), LRU_BLOCK)
        a, u = a_ref[rs, :], u_ref[rs, :]
        d = 1
        while d < LRU_BLOCK:
            keep = row < LRU_BLOCK - d if reverse else row >= d
            sh = LRU_BLOCK - d if reverse else d
            a_sh = jnp.where(keep, pltpu.roll(a, sh, 0), 1.0)
            u_sh = jnp.where(keep, pltpu.roll(u, sh, 0), 0.0)
            u = a * u_sh + u
            a = a * a_sh
            d *= 2
        h = u + a * carry
        h_ref[rs, :] = h
        return h[0:1, :] if reverse else h[LRU_BLOCK - 1:LRU_BLOCK, :]

    lax.fori_loop(0, nb, block, jnp.zeros((1, 128), F32))


def _lru_specs(s_len, order):
    def at(f):
        return lambda *g: f(*order(*g))
    return [pl.BlockSpec((s_len, 128), at(lambda b, cb: (b, 6 + cb))), pl.BlockSpec((s_len, 128), at(lambda b, cb: (b, 10 + cb))),
            pl.BlockSpec((4, 128), at(lambda b, cb: (0, cb))), pl.BlockSpec((1, 128), at(lambda b, cb: (0, cb))),
            pl.BlockSpec((1, 128, 128), at(lambda b, cb: (cb, 0, 0))), pl.BlockSpec((1, 128), at(lambda b, cb: (0, cb))),
            pl.BlockSpec((1, 128, 128), at(lambda b, cb: (cb, 0, 0))), pl.BlockSpec((1, 128), at(lambda b, cb: (0, cb))),
            pl.BlockSpec((1, 128), at(lambda b, cb: (0, cb)))]


def lru_fwd(proj, cw, cb, wa, ba, wx, bxb, lam, bsz, *, name):
    t = proj.shape[0]
    s_len = t // bsz

    def body(x_ref, g_ref, cw_ref, cb_ref, wa_ref, ba_ref, wx_ref, bx_ref, lam_ref, y_ref, a_s, u_s):
        row = lax.broadcasted_iota(jnp.int32, (s_len, 128), 0)
        xc, r, i, sp, a, mult = _lru_gates(x_ref[...].astype(F32), cw_ref[...], cb_ref[...], wa_ref[0], ba_ref[...],
                                           wx_ref[0], bx_ref[...], lam_ref[...], row)
        a_s[...] = a
        u_s[...] = mult * (i * xc)
        _lru_scan_refs(a_s, u_s, y_ref)
        y_ref[...] = y_ref[...] * _gelu(g_ref[...].astype(F32))[0]

    return _pcall(
        body, name=name, grid=(bsz, 4), in_specs=_lru_specs(s_len, lambda b, cb: (b, cb)),
        out_specs=pl.BlockSpec((s_len, 128), lambda b, cb: (b, cb)),
        out_shape=jax.ShapeDtypeStruct((t, B_WIDTH), F32), scratch_shapes=[pltpu.VMEM((s_len, 128), F32)] * 2,
        compiler_params=_cp("parallel", "parallel"),
    )(proj, proj, cw, cb.reshape(1, -1), wa, ba.reshape(1, -1), wx, bxb.reshape(1, -1), lam.reshape(1, -1))


def lru_bwd(proj, cw, cb, wa, ba, wx, bxb, lam, dcat, bsz, *, name, side=None):
    t = proj.shape[0]
    s_len = t // bsz

    def body(x_ref, g_ref, cw_ref, cb_ref, wa_ref, ba_ref, wx_ref, bx_ref, lam_ref, dy_ref,
             dx_ref, dg_ref, dcw_ref, dcb_ref, dwa_ref, dba_ref, dwx_ref, dbx_ref, dlam_ref, a_s, u_s, h_s, g_s):
        b = pl.program_id(1)
        row = lax.broadcasted_iota(jnp.int32, (s_len, 128), 0)
        x = x_ref[...].astype(F32)
        lam = lam_ref[...]
        xc, r, i, sp, a, mult = _lru_gates(x, cw_ref[...], cb_ref[...], wa_ref[0], ba_ref[...], wx_ref[0], bx_ref[...],
                                           lam, row)
        ixc = i * xc
        a_s[...] = a
        u_s[...] = mult * ixc
        _lru_scan_refs(a_s, u_s, h_s)
        h = h_s[...]
        gv = g_ref[...].astype(F32)
        gl, th = _gelu(gv)
        dy = dy_ref[...]
        dg_ref[...] = dy * h * _gelu_grad(gv, th)
        a_s[...] = _shift_up(a, 1, row)
        u_s[...] = dy * gl
        _lru_scan_refs(a_s, u_s, g_s, reverse=True)
        gr = g_s[...]
        da = gr * _shift_dn(h, 1, row)
        dmult = gr * ixc
        di = gr * mult * xc
        dxc = gr * mult * i
        dlog_a = da * a - dmult * (a * a) / mult
        dr = dlog_a * (-RG_C * sp)
        dlam = jnp.sum(dlog_a * r, axis=0, keepdims=True) * (RG_C * _sigmoid(-lam))
        dpa = dr * r * (1.0 - r)
        dpx = di * i * (1.0 - i)
        dxc = dxc + _mdot(dpa, wa_ref[0], NT) + _mdot(dpx, wx_ref[0], NT)
        dx, dcw = _conv_bwd(x, cw_ref[...], dxc, row)
        dx_ref[...] = dx

        @pl.when(b == 0)
        def _():
            for ref in (dcw_ref, dcb_ref, dwa_ref, dba_ref, dwx_ref, dbx_ref, dlam_ref):
                ref[...] = jnp.zeros_like(ref)

        dcw_ref[...] += dcw
        dcb_ref[...] += jnp.sum(dxc, axis=0, keepdims=True)
        dwa_ref[0] += _mdot(xc, dpa, TN)
        dwx_ref[0] += _mdot(xc, dpx, TN)
        dba_ref[...] += jnp.sum(dpa, axis=0, keepdims=True)
        dbx_ref[...] += jnp.sum(dpx, axis=0, keepdims=True)
        dlam_ref[...] += dlam

    order = lambda cb, b: (b, cb)
    act = pl.BlockSpec((s_len, 128), lambda cb, b: (b, cb))
    vec = pl.BlockSpec((1, 128), lambda cb, b: (0, cb))
    mat = pl.BlockSpec((1, 128, 128), lambda cb, b: (cb, 0, 0))
    vshape = jax.ShapeDtypeStruct((1, B_WIDTH), F32)
    mshape = jax.ShapeDtypeStruct((4, 128, 128), F32)
    return _call(
        body, (proj, proj, cw, cb.reshape(1, -1), wa, ba.reshape(1, -1), wx, bxb.reshape(1, -1), lam.reshape(1, -1), dcat),
        side, (4, bsz), name=name,
        in_specs=_lru_specs(s_len, order) + [pl.BlockSpec((s_len, 128), lambda cb, b: (b, 4 + cb))],
        out_specs=[act, act, pl.BlockSpec((4, 128), lambda cb, b: (0, cb)), vec, mat, vec, mat, vec, vec],
        out_shape=[jax.ShapeDtypeStruct((t, B_WIDTH), F32), jax.ShapeDtypeStruct((t, B_WIDTH), F32),
                   jax.ShapeDtypeStruct((4, B_WIDTH), F32), vshape, mshape, vshape, mshape, vshape, vshape],
        scratch_shapes=[pltpu.VMEM((s_len, 128), F32)] * 4,
        compiler_params=_cp("parallel", "arbitrary"),
    )


_BDIMS = {"nn": ((2,), (1,)), "nt": ((2,), (2,)), "tn": ((1,), (1,))}
C_QSCALE = C_HEAD_DIM ** -0.5


def _bmm(a, b, mode, exact=False):
    dims = (_BDIMS[mode], ((0,), (0,)))
    if exact:
        return lax.dot_general(a, b, dims, preferred_element_type=F32, precision=lax.Precision.HIGH)
    return lax.dot_general(a.astype(MM), b.astype(MM), dims, preferred_element_type=F32)


def _col(x, idx, lane):
    return jnp.broadcast_to(jnp.sum(jnp.where(lane == idx, x, 0.0), axis=-1, keepdims=True), x.shape)


def _seg_cumsum(g, row):
    pos = row & (CHUNK - 1)
    d = 1
    while d < CHUNK:
        g = g + jnp.where(pos >= d, pltpu.roll(g, d, 0), 0.0)
        d *= 2
    return g


def _seg_cumsum_rev(g, row):
    pos = row & (CHUNK - 1)
    n = g.shape[0]
    d = 1
    while d < CHUNK:
        g = g + jnp.where(pos < CHUNK - d, pltpu.roll(g, n - d, 0), 0.0)
        d *= 2
    return g


def _gdn_prep(qr, kr, vr, gates, cwq, cwk, cwv, a_log, dtb, h):
    s_len = qr.shape[0]
    nc = s_len // CHUNK
    row = lax.broadcasted_iota(jnp.int32, (s_len, 128), 0)
    lane = lax.broadcasted_iota(jnp.int32, (s_len, 128), 1)
    r = {"row": row, "lane": lane}
    for nm, x, w in (("q", qr, cwq), ("k", kr, cwk), ("v", vr, cwv)):
        c = _conv_fwd(x, w, row)
        sg = _sigmoid(c)
        r["c" + nm], r["s" + nm], r[nm + "c"] = c, sg, c * sg
    r["rq"] = lax.rsqrt(jnp.sum(r["qc"] * r["qc"], axis=-1, keepdims=True) + NORM_EPS)
    r["rk"] = lax.rsqrt(jnp.sum(r["kc"] * r["kc"], axis=-1, keepdims=True) + NORM_EPS)
    r["qn"] = r["qc"] * r["rq"]
    r["kn"] = r["kc"] * r["rk"]
    r["beta"] = _sigmoid(_col(gates, h, lane))
    r["A"] = jnp.exp(a_log)
    r["pre"] = _col(gates, 8 + h, lane) + dtb
    r["sp"] = _softplus(r["pre"])
    gc = _seg_cumsum(-r["A"] * r["sp"], row)
    sh = (nc, CHUNK, 128)
    q3 = (r["qn"] * C_QSCALE).reshape(sh)
    k3 = r["kn"].reshape(sh)
    v3 = r["vc"].reshape(sh)
    beta3 = r["beta"].reshape(sh)
    gc3 = gc.reshape(sh)
    gcl3 = gc3[:, CHUNK - 1:CHUNK, :]
    eg = jnp.exp(gc3)
    ekd = jnp.exp(gcl3 - gc3)
    col64 = gc3[:, :, :CHUNK]
    row64 = jnp.swapaxes(gc3, 1, 2)[:, :CHUNK, :]
    ii = lax.broadcasted_iota(jnp.int32, (nc, CHUNK, CHUNK), 1)
    jj = lax.broadcasted_iota(jnp.int32, (nc, CHUNK, CHUNK), 2)
    tril = ii >= jj
    strict = ii > jj
    dm = jnp.where(tril, jnp.exp(jnp.where(tril, col64 - row64, 0.0)), 0.0)
    kb = k3 * beta3
    lmat = jnp.where(strict, _bmm(kb, k3, "nt") * dm, 0.0)
    attn = _bmm(q3, k3, "nt") * dm
    r.update(q3=q3, k3=k3, v3=v3, beta3=beta3, gc3=gc3, eg=eg, ekd=ekd, gl=jnp.exp(gcl3), dm=dm, kb=kb, lmat=lmat,
             attn=attn, strict=strict, tril=tril, qg=q3 * eg, kdec=k3 * ekd)
    return r


def _neumann_inverse(lmat):
    ii = lax.broadcasted_iota(jnp.int32, lmat.shape, 1)
    jj = lax.broadcasted_iota(jnp.int32, lmat.shape, 2)
    x = -lmat
    tm = jnp.where(ii == jj, 1.0, 0.0) + x
    pw = x
    for _ in range(5):
        pw = _bmm(pw, pw, "nn", exact=True)
        tm = tm + _bmm(tm, pw, "nn", exact=True)
    return tm


def _gdn_specs(s_len):
    act = lambda off: pl.BlockSpec((s_len, 128), lambda b, h: (b, off + h))
    cw = lambda off: pl.BlockSpec((4, 128), lambda b, h: (0, off + h))
    smem = pl.BlockSpec(memory_space=pltpu.SMEM)
    return [act(0), act(8), act(16), act(24), pl.BlockSpec((s_len, 128), lambda b, h: (b, 0)), cw(0), cw(8), cw(16),
            smem, smem, pl.BlockSpec((1, 128), lambda b, h: (0, 0))]


def gdn_fwd(proj, gates, cw, a_log, dtb, ng, bsz, *, name):
    t = proj.shape[0]
    s_len = t // bsz
    nc = s_len // CHUNK

    def body(q_ref, k_ref, v_ref, z_ref, gt_ref, cwq_ref, cwk_ref, cwv_ref, al_ref, dt_ref, ng_ref,
             out_ref, o_ref, vn_ref, tm_ref, st_ref, u_s, w_s, qg_s, kd_s, at_s, gl_s):
        h = pl.program_id(1)
        r = _gdn_prep(q_ref[...], k_ref[...], v_ref[...], gt_ref[...], cwq_ref[...], cwk_ref[...], cwv_ref[...],
                      al_ref[h], dt_ref[h], h)
        tm = _neumann_inverse(r["lmat"])
        tm_ref[0, 0] = tm
        u_s[...] = _bmm(tm, r["v3"] * r["beta3"], "nn", exact=True)
        w_s[...] = _bmm(tm, r["kb"] * r["eg"], "nn", exact=True)
        qg_s[...] = r["qg"]
        kd_s[...] = r["kdec"]
        at_s[...] = r["attn"]
        gl_s[...] = r["gl"]

        def chunk(n, state):
            st = pl.multiple_of(n * CHUNK, CHUNK)
            st_ref[0, 0, n] = state
            v_new = u_s[n] - _mdot(w_s[n], state)
            o_ref[pl.ds(st, CHUNK), :] = _mdot(qg_s[n], state) + _mdot(at_s[n], v_new)
            vn_ref[pl.ds(st, CHUNK), :] = v_new
            return state * gl_s[n] + _mdot(kd_s[n], v_new, TN)

        lax.fori_loop(0, nc, chunk, jnp.zeros((128, 128), F32))
        o = o_ref[...]
        rms = lax.rsqrt(jnp.mean(o * o, axis=-1, keepdims=True) + NORM_EPS)
        z = z_ref[...]
        out_ref[...] = o * rms * ng_ref[...] * (z * _sigmoid(z))

    blk = pl.BlockSpec((s_len, 128), lambda b, h: (b, h))
    full = jax.ShapeDtypeStruct((t, C_WIDTH), F32)
    return _pcall(
        body, name=name, grid=(bsz, C_HEADS), in_specs=_gdn_specs(s_len),
        out_specs=[blk, blk, blk, pl.BlockSpec((1, 1, nc, CHUNK, CHUNK), lambda b, h: (b, h, 0, 0, 0)),
                   pl.BlockSpec((1, 1, nc, 128, 128), lambda b, h: (b, h, 0, 0, 0))],
        out_shape=[full, full, full, jax.ShapeDtypeStruct((bsz, C_HEADS, nc, CHUNK, CHUNK), F32),
                   jax.ShapeDtypeStruct((bsz, C_HEADS, nc, 128, 128), F32)],
        scratch_shapes=[pltpu.VMEM((nc, CHUNK, 128), F32)] * 4 + [pltpu.VMEM((nc, CHUNK, CHUNK), F32),
                                                                   pltpu.VMEM((nc, 1, 128), F32)],
        compiler_params=_cp("parallel", "parallel"),
    )(proj, proj, proj, proj, gates, cw, cw, cw, a_log, dtb, ng.reshape(1, 128))


def gdn_bwd(proj, gates, cw, a_log, dtb, ng, o_pre, vnew, tmat, states, dout, bsz, *, name):
    t = proj.shape[0]
    s_len = t // bsz
    nc = s_len // CHUNK

    def body(q_ref, k_ref, v_ref, z_ref, gt_ref, cwq_ref, cwk_ref, cwv_ref, al_ref, dt_ref, ng_ref,
             o_ref, vn_ref, tm_ref, st_ref, do_ref,
             dq_ref, dk_ref, dv_ref, dz_ref, dgt_ref, dcq_ref, dck_ref, dcv_ref, dsm_ref,
             w_s, qg_s, kd_s, at_s, gl_s, dop_s, du_s, dw_s, dat_s, dqg_s, dkd_s, dgl_s):
        h = pl.program_id(1)
        qr, kr, vr = q_ref[...], k_ref[...], v_ref[...]
        r = _gdn_prep(qr, kr, vr, gt_ref[...], cwq_ref[...], cwk_ref[...], cwv_ref[...], al_ref[h], dt_ref[h], h)
        row, lane = r["row"], r["lane"]
        tm = tm_ref[0, 0]
        q3, k3, v3, beta3, eg, kb, dm = r["q3"], r["k3"], r["v3"], r["beta3"], r["eg"], r["kb"], r["dm"]
        u3 = _bmm(tm, v3 * beta3, "nn", exact=True)
        w3 = _bmm(tm, kb * eg, "nn", exact=True)
        w_s[...] = w3
        qg_s[...] = r["qg"]
        kd_s[...] = r["kdec"]
        at_s[...] = r["attn"]
        gl_s[...] = r["gl"]

        z = z_ref[...]
        sz = _sigmoid(z)
        o = o_ref[...]
        rms = lax.rsqrt(jnp.mean(o * o, axis=-1, keepdims=True) + NORM_EPS)
        on = o * rms
        dout_v = do_ref[...]
        ngv = ng_ref[...]
        dz_ref[...] = dout_v * on * ngv * (sz * (1.0 + z * (1.0 - sz)))
        dos = dout_v * (z * sz)
        dng = jnp.sum(dos * on, axis=0, keepdims=True)
        don = dos * ngv
        dop_s[...] = (rms * (don - on * jnp.mean(don * on, axis=-1, keepdims=True))).reshape(nc, CHUNK, 128)

        def chunk(i, dstate):
            n = nc - 1 - i
            st = pl.multiple_of(n * CHUNK, CHUNK)
            state = st_ref[0, 0, n]
            vn = vn_ref[pl.ds(st, CHUNK), :]
            do_n = dop_s[n]
            dvn = _mdot(at_s[n], do_n, TN) + _mdot(kd_s[n], dstate)
            du_s[n] = dvn
            dat_s[n] = _mdot(do_n, vn, NT)
            dqg_s[n] = _mdot(do_n, state, NT)
            dkd_s[n] = _mdot(vn, dstate, NT)
            dgl_s[n] = jnp.broadcast_to(jnp.sum(jnp.sum(state * dstate, axis=1, keepdims=True), axis=0, keepdims=True), (1, 128))
            dw_s[n] = -_mdot(dvn, state, NT)
            return dstate * gl_s[n] + _mdot(qg_s[n], do_n, TN) - _mdot(w_s[n], dvn, TN)

        lax.fori_loop(0, nc, chunk, jnp.zeros((128, 128), F32))

        du, dw, dqg, dkd = du_s[...], dw_s[...], dqg_s[...], dkd_s[...]
        dat = jnp.where(r["tril"], dat_s[...], 0.0)
        dvb = _bmm(tm, du, "tn", exact=True)
        dkbg = _bmm(tm, dw, "tn", exact=True)
        dl = -jnp.where(r["strict"], _bmm(dvb, u3, "nt") + _bmm(dkbg, w3, "nt"), 0.0)
        dml = dl * dm
        dn = dat * dm
        dkb = _bmm(dml, k3, "nn") + dkbg * eg
        dk3 = _bmm(dml, kb, "tn") + _bmm(dn, q3, "tn") + dkd * r["ekd"] + dkb * beta3
        dq3 = dqg * eg + _bmm(dn, k3, "nn")
        e = dl * r["lmat"] + dat * r["attn"]
        ones = jnp.ones((nc, CHUNK, 128), F32)
        colsum = lax.dot_general(e, ones, (_BDIMS["tn"], ((0,), (0,))), preferred_element_type=F32, precision=lax.Precision.HIGH)
        dgc = jnp.sum(e, axis=-1, keepdims=True) - colsum
        dgc = dgc + eg * (jnp.sum(dqg * q3, axis=-1, keepdims=True) + jnp.sum(dkbg * kb, axis=-1, keepdims=True))
        skd = jnp.sum(dkd * r["kdec"], axis=-1, keepdims=True)
        dgcl = jnp.sum(skd, axis=1, keepdims=True) + dgl_s[...] * r["gl"]
        pos3 = lax.broadcasted_iota(jnp.int32, (nc, CHUNK, 128), 1)
        dgc = dgc - skd + jnp.where(pos3 == CHUNK - 1, dgcl, 0.0)
        dbeta = jnp.sum(dkb * k3, axis=-1, keepdims=True) + jnp.sum(dvb * v3, axis=-1, keepdims=True)
        dv3 = dvb * beta3

        dg = _seg_cumsum_rev(dgc.reshape(s_len, 128), row)
        beta = r["beta"]
        dbl = jnp.broadcast_to(dbeta, (nc, CHUNK, 128)).reshape(s_len, 128) * beta * (1.0 - beta)
        dai = dg * (-r["A"]) * _sigmoid(r["pre"])
        d_dtb = jnp.sum(dai, axis=0, keepdims=True)
        d_alog = jnp.sum(dg * (-r["sp"]), axis=0, keepdims=True) * r["A"]

        @pl.when(h == 0)
        def _():
            dgt_ref[...] = jnp.zeros_like(dgt_ref)
            dsm_ref[...] = jnp.zeros_like(dsm_ref)

        dgt_ref[...] += jnp.where(lane == h, dbl, 0.0) + jnp.where(lane == 8 + h, dai, 0.0)
        r16 = lax.broadcasted_iota(jnp.int32, (16, 128), 0)
        l16 = lax.broadcasted_iota(jnp.int32, (16, 128), 1)
        small = jnp.where((r16 == h) & (l16 == 0), d_alog, 0.0) + jnp.where((r16 == h) & (l16 == 1), d_dtb, 0.0)
        dsm_ref[0] += small + jnp.where(r16 == 8 + h, dng, 0.0)

        dqn = dq3.reshape(s_len, 128) * C_QSCALE
        dkn = dk3.reshape(s_len, 128)
        dqc = r["rq"] * (dqn - r["qn"] * jnp.sum(dqn * r["qn"], axis=-1, keepdims=True))
        dkc = r["rk"] * (dkn - r["kn"] * jnp.sum(dkn * r["kn"], axis=-1, keepdims=True))
        dvc = dv3.reshape(s_len, 128)
        for nm, x, w_ref, dxc, dx_ref, dc_ref in (("q", qr, cwq_ref, dqc, dq_ref, dcq_ref), ("k", kr, cwk_ref, dkc, dk_ref, dck_ref),
                                                 ("v", vr, cwv_ref, dvc, dv_ref, dcv_ref)):
            c, sg = r["c" + nm], r["s" + nm]
            dc = dxc * (sg * (1.0 + c * (1.0 - sg)))
            dx, dwc = _conv_bwd(x, w_ref[...], dc, row)
            dx_ref[...] = dx
            dc_ref[0] = dwc

    blk = pl.BlockSpec((s_len, 128), lambda b, h: (b, h))
    full = jax.ShapeDtypeStruct((t, C_WIDTH), F32)
    cwo = pl.BlockSpec((1, 4, 128), lambda b, h: (b, 0, h))
    cws = jax.ShapeDtypeStruct((bsz, 4, C_WIDTH), F32)
    c128 = pltpu.VMEM((nc, CHUNK, 128), F32)
    outs = _pcall(
        body, name=name, grid=(bsz, C_HEADS),
        in_specs=_gdn_specs(s_len) + [blk, blk, pl.BlockSpec((1, 1, nc, CHUNK, CHUNK), lambda b, h: (b, h, 0, 0, 0)),
                                      pl.BlockSpec((1, 1, nc, 128, 128), lambda b, h: (b, h, 0, 0, 0)), blk],
        out_specs=[blk, blk, blk, blk, pl.BlockSpec((s_len, 128), lambda b, h: (b, 0)), cwo, cwo, cwo,
                   pl.BlockSpec((1, 16, 128), lambda b, h: (b, 0, 0))],
        out_shape=[full, full, full, full, jax.ShapeDtypeStruct((t, 128), F32), cws, cws, cws,
                   jax.ShapeDtypeStruct((bsz, 16, 128), F32)],
        scratch_shapes=[c128, c128, c128, pltpu.VMEM((nc, CHUNK, CHUNK), F32), pltpu.VMEM((nc, 1, 128), F32), c128,
                        c128, c128, pltpu.VMEM((nc, CHUNK, CHUNK), F32), c128, c128, pltpu.VMEM((nc, 1, 128), F32)],
        compiler_params=_cp("parallel", "arbitrary"),
    )(proj, proj, proj, proj, gates, cw, cw, cw, a_log, dtb, ng.reshape(1, 128), o_pre, vnew, tmat, states, dout)
    dq, dk, dv, dz, dgates, dcq, dck, dcv, dsm = outs
    return dq, dk, dv, dz, dgates, jnp.concatenate([dcq, dck, dcv], axis=-1), dsm


def gdc_pre_fwd(proj, cw, bsz, *, name):
    t = proj.shape[0]
    s_len = t // bsz

    def body(x_ref, w_ref, y_ref):
        row = lax.broadcasted_iota(jnp.int32, (s_len, 128), 0)
        c = _conv_fwd(x_ref[...].astype(F32), w_ref[...], row)
        xc = c * _sigmoid(c)
        rn = lax.rsqrt(jnp.sum(xc * xc, axis=-1, keepdims=True) + NORM_EPS)
        y_ref[...] = jnp.where(pl.program_id(1) < 2 * C_HEADS, xc * rn, xc)

    blk = pl.BlockSpec((s_len, 128), lambda b, j: (b, j))
    return _pcall(
        body, name=name, grid=(bsz, 3 * C_HEADS), in_specs=[blk, pl.BlockSpec((4, 128), lambda b, j: (0, j))],
        out_specs=blk, out_shape=jax.ShapeDtypeStruct((t, 3 * C_WIDTH), F32), compiler_params=_cp("parallel", "parallel"),
    )(proj, cw)


def gdc_pre_bwd(proj, cw, dy, dproj, bsz, *, name):
    t = proj.shape[0]
    s_len = t // bsz

    def body(x_ref, w_ref, dq_ref, dk_ref, dv_ref, _, dx_ref, dw_ref):
        row = lax.broadcasted_iota(jnp.int32, (s_len, 128), 0)
        x = x_ref[...].astype(F32)
        c = _conv_fwd(x, w_ref[...], row)
        sg = _sigmoid(c)
        xc = c * sg
        rn = lax.rsqrt(jnp.sum(xc * xc, axis=-1, keepdims=True) + NORM_EPS)
        part = pl.program_id(1) // C_HEADS
        dyv = jnp.where(part == 0, dq_ref[...], jnp.where(part == 1, dk_ref[...], dv_ref[...]))
        xn = xc * rn
        dxc = jnp.where(pl.program_id(1) < 2 * C_HEADS, rn * (dyv - xn * jnp.sum(dyv * xn, axis=-1, keepdims=True)), dyv)
        dc = dxc * (sg * (1.0 + c * (1.0 - sg)))
        dx, dw = _conv_bwd(x, w_ref[...], dc, row)
        dx_ref[...] = dx.astype(MM)
        dw_ref[0] = dw

    blk = pl.BlockSpec((s_len, 128), lambda b, j: (b, j))

    def dy_spec(part):
        return pl.BlockSpec((s_len, 128), lambda b, j: (b, jnp.clip(j - part * C_HEADS, 0, C_HEADS - 1)))

    return _pcall(
        body, name=name, grid=(bsz, 3 * C_HEADS),
        in_specs=[blk, pl.BlockSpec((4, 128), lambda b, j: (0, j)), dy_spec(0), dy_spec(1), dy_spec(2),
                  pl.BlockSpec(memory_space=pl.ANY)],
        out_specs=[blk, pl.BlockSpec((1, 4, 128), lambda b, j: (b, 0, j))],
        out_shape=[jax.ShapeDtypeStruct((t, 4 * C_WIDTH), MM), jax.ShapeDtypeStruct((bsz, 4, 3 * C_WIDTH), F32)],
        input_output_aliases={5: 0}, compiler_params=_cp("parallel", "parallel"),
    )(proj, cw, *dy, dproj)


GDC_GROUP = 16


def _gdc_local(qn, kn, vc, gates, a_log, dtb, h):
    rows = qn.shape[0]
    nc = rows // CHUNK
    row = lax.broadcasted_iota(jnp.int32, (rows, 128), 0)
    lane = lax.broadcasted_iota(jnp.int32, (rows, 128), 1)
    r = {"row": row, "lane": lane}
    r["beta"] = _sigmoid(_col(gates, h, lane))
    r["A"] = jnp.exp(a_log)
    r["pre"] = _col(gates, 8 + h, lane) + dtb
    r["sp"] = _softplus(r["pre"])
    gc = _seg_cumsum(-r["A"] * r["sp"], row)
    sh = (nc, CHUNK, 128)
    q3 = (qn * C_QSCALE).reshape(sh)
    k3 = kn.reshape(sh)
    v3 = vc.reshape(sh)
    beta3 = r["beta"].reshape(sh)
    gc3 = gc.reshape(sh)
    gcl3 = gc3[:, CHUNK - 1:CHUNK, :]
    eg = jnp.exp(gc3)
    ekd = jnp.exp(gcl3 - gc3)
    col64 = gc3[:, :, :CHUNK]
    row64 = jnp.swapaxes(gc3, 1, 2)[:, :CHUNK, :]
    ii = lax.broadcasted_iota(jnp.int32, (nc, CHUNK, CHUNK), 1)
    jj = lax.broadcasted_iota(jnp.int32, (nc, CHUNK, CHUNK), 2)
    tril = ii >= jj
    strict = ii > jj
    dm = jnp.where(tril, jnp.exp(jnp.where(tril, col64 - row64, 0.0)), 0.0)
    kb = k3 * beta3
    lmat = jnp.where(strict, _bmm(kb, k3, "nt") * dm, 0.0)
    attn = _bmm(q3, k3, "nt") * dm
    r.update(q3=q3, k3=k3, v3=v3, beta3=beta3, eg=eg, ekd=ekd, gl=jnp.exp(gcl3), dm=dm, kb=kb, lmat=lmat,
             attn=attn, strict=strict, tril=tril, qg=q3 * eg, kdec=k3 * ekd)
    return r


def _gdc_specs(s_len):
    act = lambda off: pl.BlockSpec((s_len, 128), lambda b, h: (b, off + h))
    smem = pl.BlockSpec(memory_space=pltpu.SMEM)
    return [act(0), act(8), act(16), act(24), pl.BlockSpec((s_len, 128), lambda b, h: (b, 0)), smem, smem,
            pl.BlockSpec((1, 128), lambda b, h: (0, 0))]


def gdc_fwd(qkv, proj, gates, a_log, dtb, ng, bsz, *, name, side=None):
    t = proj.shape[0]
    s_len = t // bsz
    nc = s_len // CHUNK
    grp = min(GDC_GROUP, nc)
    gr = grp * CHUNK

    def body(q_ref, k_ref, v_ref, z_ref, gt_ref, al_ref, dt_ref, ng_ref,
             out_ref, o_ref, tm_ref, st_ref, c_s, b_s, qp_s, op_s, gl_s):
        h = pl.program_id(1)

        def local(gi, carry):
            rs = pl.ds(pl.multiple_of(gi * gr, gr), gr)
            cs = pl.ds(gi * grp, grp)
            r = _gdc_local(q_ref[rs, :], k_ref[rs, :], v_ref[rs, :], gt_ref[rs, :], al_ref[h], dt_ref[h], h)
            tm = _neumann_inverse(r["lmat"])
            tm_ref[0, 0, cs] = tm
            u = _bmm(tm, r["v3"] * r["beta3"], "nn")
            w = _bmm(tm, r["kb"] * r["eg"], "nn")
            c_s[cs] = -_bmm(r["kdec"], w, "tn")
            b_s[cs] = _bmm(r["kdec"], u, "tn")
            qp_s[cs] = r["qg"] - _bmm(r["attn"], w, "nn")
            op_s[cs] = _bmm(r["attn"], u, "nn")
            gl_s[cs] = r["gl"]
            return carry

        lax.fori_loop(0, nc // grp, local, 0)

        def chunk(n, state):
            st = pl.multiple_of(n * CHUNK, CHUNK)
            st_ref[0, 0, n] = state
            o_ref[pl.ds(st, CHUNK), :] = _mdot(qp_s[n], state) + op_s[n]
            return state * gl_s[n] + _mdot(c_s[n], state) + b_s[n]

        lax.fori_loop(0, nc, chunk, jnp.zeros((128, 128), F32))
        o = o_ref[...]
        rms = lax.rsqrt(jnp.mean(o * o, axis=-1, keepdims=True) + NORM_EPS)
        z = z_ref[...].astype(F32)
        out_ref[...] = o * rms * ng_ref[...] * (z * _sigmoid(z))

    blk = pl.BlockSpec((s_len, 128), lambda b, h: (b, h))
    full = jax.ShapeDtypeStruct((t, C_WIDTH), F32)
    return _call(
        body, (qkv, qkv, qkv, proj, gates, a_log, dtb, ng.reshape(1, 128)), side, (bsz, C_HEADS), name=name,
        in_specs=_gdc_specs(s_len),
        out_specs=[blk, blk, pl.BlockSpec((1, 1, nc, CHUNK, CHUNK), lambda b, h: (b, h, 0, 0, 0)),
                   pl.BlockSpec((1, 1, nc, 128, 128), lambda b, h: (b, h, 0, 0, 0))],
        out_shape=[full, full, jax.ShapeDtypeStruct((bsz, C_HEADS, nc, CHUNK, CHUNK), F32),
                   jax.ShapeDtypeStruct((bsz, C_HEADS, nc, 128, 128), F32)],
        scratch_shapes=[pltpu.VMEM((nc, 128, 128), F32)] * 2 + [pltpu.VMEM((nc, CHUNK, 128), F32)] * 2 +
                       [pltpu.VMEM((nc, 1, 128), F32)],
        compiler_params=_cp("parallel", "parallel"),
    )


def gdc_bwd(qkv, proj, gates, a_log, dtb, ng, o_pre, tmat, states, dout, bsz, *, name, side=None):
    t = proj.shape[0]
    s_len = t // bsz
    nc = s_len // CHUNK
    grp = min(GDC_GROUP, nc)
    gr = grp * CHUNK

    def body(q_ref, k_ref, v_ref, z_ref, gt_ref, al_ref, dt_ref, ng_ref, o_ref, tm_ref, st_ref, do_ref,
             dq_ref, dk_ref, dv_ref, dz_ref, dgt_ref, dsm_ref, c_s, e_s, dsp_s, gl_s, dop_s):
        h = pl.program_id(1)
        a_log_h, dtb_h = al_ref[h], dt_ref[h]

        z = z_ref[...].astype(F32)
        sz = _sigmoid(z)
        o = o_ref[...]
        rms = lax.rsqrt(jnp.mean(o * o, axis=-1, keepdims=True) + NORM_EPS)
        on = o * rms
        dout_v = do_ref[...]
        ngv = ng_ref[...]
        dz_ref[...] = (dout_v * on * ngv * (sz * (1.0 + z * (1.0 - sz)))).astype(MM)
        dos = dout_v * (z * sz)
        dng = jnp.sum(dos * on, axis=0, keepdims=True)
        don = dos * ngv
        dop_s[...] = (rms * (don - on * jnp.mean(don * on, axis=-1, keepdims=True))).reshape(nc, CHUNK, 128)

        def local(gi, carry):
            rs = pl.ds(pl.multiple_of(gi * gr, gr), gr)
            cs = pl.ds(gi * grp, grp)
            r = _gdc_local(q_ref[rs, :], k_ref[rs, :], v_ref[rs, :], gt_ref[rs, :], a_log_h, dtb_h, h)
            w = _bmm(tm_ref[0, 0, cs], r["kb"] * r["eg"], "nn")
            c_s[cs] = -_bmm(w, r["kdec"], "tn")
            e_s[cs] = _bmm(r["qg"] - _bmm(r["attn"], w, "nn"), dop_s[cs], "tn")
            gl_s[cs] = r["gl"]
            return carry

        lax.fori_loop(0, nc // grp, local, 0)

        def chunk(i, dstate):
            n = nc - 1 - i
            dsp_s[n] = dstate
            return dstate * gl_s[n] + _mdot(c_s[n], dstate) + e_s[n]

        lax.fori_loop(0, nc, chunk, jnp.zeros((128, 128), F32))

        @pl.when(h == 0)
        def _():
            dgt_ref[...] = jnp.zeros_like(dgt_ref)
            dsm_ref[...] = jnp.zeros_like(dsm_ref)

        def local_bwd(gi, carry):
            d_alog, d_dtb = carry
            rs = pl.ds(pl.multiple_of(gi * gr, gr), gr)
            cs = pl.ds(gi * grp, grp)
            r = _gdc_local(q_ref[rs, :], k_ref[rs, :], v_ref[rs, :], gt_ref[rs, :], a_log_h, dtb_h, h)
            row, lane = r["row"], r["lane"]
            q3, k3, v3, beta3, eg, kb, dm = r["q3"], r["k3"], r["v3"], r["beta3"], r["eg"], r["kb"], r["dm"]
            tm = tm_ref[0, 0, cs]
            u3 = _bmm(tm, v3 * beta3, "nn")
            w3 = _bmm(tm, kb * eg, "nn")
            state, dsp, do3 = st_ref[0, 0, cs], dsp_s[cs], dop_s[cs]
            vn = u3 - _bmm(w3, state, "nn")
            du = _bmm(r["attn"], do3, "tn") + _bmm(r["kdec"], dsp, "nn")
            dat = jnp.where(r["tril"], _bmm(do3, vn, "nt"), 0.0)
            dqg = _bmm(do3, state, "nt")
            dkd = _bmm(vn, dsp, "nt")
            dgl = jnp.sum(jnp.sum(state * dsp, axis=2, keepdims=True), axis=1, keepdims=True)
            dw = -_bmm(du, state, "nt")
            dvb = _bmm(tm, du, "tn")
            dkbg = _bmm(tm, dw, "tn")
            dl = -jnp.where(r["strict"], _bmm(dvb, u3, "nt") + _bmm(dkbg, w3, "nt"), 0.0)
            dml = dl * dm
            dn = dat * dm
            dkb = _bmm(dml, k3, "nn") + dkbg * eg
            dk3 = _bmm(dml, kb, "tn") + _bmm(dn, q3, "tn") + dkd * r["ekd"] + dkb * beta3
            dq3 = dqg * eg + _bmm(dn, k3, "nn")
            e = dl * r["lmat"] + dat * r["attn"]
            ones = jnp.ones((grp, CHUNK, 128), F32)
            colsum = lax.dot_general(e, ones, (_BDIMS["tn"], ((0,), (0,))), preferred_element_type=F32, precision=lax.Precision.HIGH)
            dgc = jnp.sum(e, axis=-1, keepdims=True) - colsum
            dgc = dgc + eg * (jnp.sum(dqg * q3, axis=-1, keepdims=True) + jnp.sum(dkbg * kb, axis=-1, keepdims=True))
            skd = jnp.sum(dkd * r["kdec"], axis=-1, keepdims=True)
            dgcl = jnp.sum(skd, axis=1, keepdims=True) + dgl * r["gl"]
            pos3 = lax.broadcasted_iota(jnp.int32, (grp, CHUNK, 128), 1)
            dgc = dgc - skd + jnp.where(pos3 == CHUNK - 1, dgcl, 0.0)
            dbeta = jnp.sum(dkb * k3, axis=-1, keepdims=True) + jnp.sum(dvb * v3, axis=-1, keepdims=True)
            dg = _seg_cumsum_rev(dgc.reshape(gr, 128), row)
            beta = r["beta"]
            dbl = jnp.broadcast_to(dbeta, (grp, CHUNK, 128)).reshape(gr, 128) * beta * (1.0 - beta)
            dai = dg * (-r["A"]) * _sigmoid(r["pre"])
            dgt_ref[rs, :] += jnp.where(lane == h, dbl, 0.0) + jnp.where(lane == 8 + h, dai, 0.0)
            dq_ref[rs, :] = dq3.reshape(gr, 128) * C_QSCALE
            dk_ref[rs, :] = dk3.reshape(gr, 128)
            dv_ref[rs, :] = (dvb * beta3).reshape(gr, 128)
            return (d_alog + jnp.sum(dg * (-r["sp"]), axis=0, keepdims=True) * r["A"],
                    d_dtb + jnp.sum(dai, axis=0, keepdims=True))

        zero = jnp.zeros((1, 128), F32)
        d_alog, d_dtb = lax.fori_loop(0, nc // grp, local_bwd, (zero, zero))
        r16 = lax.broadcasted_iota(jnp.int32, (16, 128), 0)
        l16 = lax.broadcasted_iota(jnp.int32, (16, 128), 1)
        small = jnp.where((r16 == h) & (l16 == 0), d_alog, 0.0) + jnp.where((r16 == h) & (l16 == 1), d_dtb, 0.0)
        dsm_ref[0] += small + jnp.where(r16 == 8 + h, dng, 0.0)

    blk = pl.BlockSpec((s_len, 128), lambda b, h: (b, h))
    blk3 = lambda off: pl.BlockSpec((s_len, 128), lambda b, h: (b, off + h))
    full = jax.ShapeDtypeStruct((t, C_WIDTH), F32)
    c128 = pltpu.VMEM((nc, CHUNK, 128), F32)
    sq = pltpu.VMEM((nc, 128, 128), F32)
    res = _call(
        body, (qkv, qkv, qkv, proj, gates, a_log, dtb, ng.reshape(1, 128), o_pre, tmat, states, dout), side,
        (bsz, C_HEADS), name=name,
        in_specs=_gdc_specs(s_len) + [blk, pl.BlockSpec((1, 1, nc, CHUNK, CHUNK), lambda b, h: (b, h, 0, 0, 0)),
                                      pl.BlockSpec((1, 1, nc, 128, 128), lambda b, h: (b, h, 0, 0, 0)), blk],
        out_specs=[blk, blk, blk, pl.BlockSpec((s_len, 128), lambda b, h: (b, 3 * C_HEADS + h)),
                   pl.BlockSpec((s_len, 128), lambda b, h: (b, 0)), pl.BlockSpec((1, 16, 128), lambda b, h: (b, 0, 0))],
        out_shape=[full, full, full, jax.ShapeDtypeStruct((t, 4 * C_WIDTH), MM), jax.ShapeDtypeStruct((t, 128), F32),
                   jax.ShapeDtypeStruct((bsz, 16, 128), F32)],
        scratch_shapes=[sq, sq, sq, pltpu.VMEM((nc, 1, 128), F32), c128],
        compiler_params=_cp("parallel", "arbitrary"),
    )
    (dq, dk, dv, dz, dgates, dsm), extra = res if side is not None else (res, None)
    out = ((dq, dk, dv), dz, dgates, dsm)
    return out if side is None else (out, extra)


MESH_ID = pl.DeviceIdType.MESH
_FLIPS = [(0, 0, 1), (1, 0, 0), (0, 1, 0), (1, 1, 0), (1, 0, 1), (0, 1, 1), (1, 1, 1)]


def _me():
    return lax.axis_index("x"), lax.axis_index("y"), lax.axis_index("c")


def _flip(coord, d):
    return 1 - coord if d else coord


def all_gather(shard, *, name):
    def body(x_ref, o_ref, send_sems, recv_sems, local_sem):
        x, y, c = _me()
        mine = 4 * x + 2 * y + c
        own = pltpu.make_async_copy(x_ref, o_ref.at[mine], local_sem)
        own.start()
        copies = []
        for k, (dx, dy, dc) in enumerate(_FLIPS):
            cp = pltpu.make_async_remote_copy(
                src_ref=x_ref, dst_ref=o_ref.at[mine], send_sem=send_sems.at[k], recv_sem=recv_sems.at[k],
                device_id=(_flip(x, dx), _flip(y, dy), _flip(c, dc)), device_id_type=MESH_ID)
            cp.start()
            copies.append(cp)
        for cp in copies:
            cp.wait()
        own.wait()

    hbm = pl.BlockSpec(memory_space=pl.ANY)
    return _pcall(
        body, name=name, in_specs=[hbm], out_specs=hbm,
        out_shape=jax.ShapeDtypeStruct((N_DEV,) + shard.shape, shard.dtype),
        scratch_shapes=[pltpu.SemaphoreType.DMA((7,)), pltpu.SemaphoreType.DMA((7,)), pltpu.SemaphoreType.DMA(())],
    )(shard)


def all_to_all(parts, *, name):
    def body(x_ref, o_ref, send_sems, recv_sems, local_sem):
        x, y, c = _me()
        mine = 4 * x + 2 * y + c
        own = pltpu.make_async_copy(x_ref.at[mine], o_ref.at[mine], local_sem)
        own.start()
        copies = []
        for k, (dx, dy, dc) in enumerate(_FLIPS):
            px, py, pc = _flip(x, dx), _flip(y, dy), _flip(c, dc)
            cp = pltpu.make_async_remote_copy(
                src_ref=x_ref.at[4 * px + 2 * py + pc], dst_ref=o_ref.at[mine], send_sem=send_sems.at[k],
                recv_sem=recv_sems.at[k], device_id=(px, py, pc), device_id_type=MESH_ID)
            cp.start()
            copies.append(cp)
        for cp in copies:
            cp.wait()
        own.wait()

    hbm = pl.BlockSpec(memory_space=pl.ANY)
    return _pcall(
        body, name=name, in_specs=[hbm], out_specs=hbm, out_shape=jax.ShapeDtypeStruct(parts.shape, parts.dtype),
        scratch_shapes=[pltpu.SemaphoreType.DMA((7,)), pltpu.SemaphoreType.DMA((7,)), pltpu.SemaphoreType.DMA(())],
    )(parts)


def adamw_sum(parts, w, m, v, *, name, tr=256):
    r, cdim = w.shape
    tr = _tile8(r, tr)

    def body(p_ref, w_ref, m_ref, v_ref, g_ref, d_ref, mo_ref, vo_ref):
        g = p_ref[0].astype(F32)
        for j in range(1, N_DEV):
            g = g + p_ref[j].astype(F32)
        g_ref[...] = g
        mn = ADAM_B1 * m_ref[...] + (1.0 - ADAM_B1) * g
        vn = ADAM_B2 * v_ref[...] + (1.0 - ADAM_B2) * (g * g)
        mo_ref[...] = mn
        vo_ref[...] = vn
        m_hat = mn / (1.0 - ADAM_B1 ** ADAM_STEP)
        v_hat = vn / (1.0 - ADAM_B2 ** ADAM_STEP)
        d_ref[...] = -ADAM_LR * (m_hat / (jnp.sqrt(v_hat) + ADAM_EPS) + ADAM_WD * w_ref[...])

    blk = pl.BlockSpec((tr, cdim), lambda i: (i, 0))
    shp = jax.ShapeDtypeStruct((r, cdim), F32)
    return _pcall(
        body, name=name, grid=(r // tr,), in_specs=[pl.BlockSpec((N_DEV, tr, cdim), lambda i: (0, i, 0)), blk, blk, blk],
        out_specs=[blk, blk, blk, blk], out_shape=[shp, shp, shp, shp], compiler_params=_cp("parallel"),
    )(parts, w, m, v)


def _tile8(n, pref):
    for c in range(min(pref, n) - min(pref, n) % 16, 0, -16):
        if n % c == 0:
            return c
    return n


BIG = [("ffn1_wg", 2), ("ffn1_wu", 2), ("ffn1_wd", 1), ("ffn2_wg", 2), ("ffn2_wu", 2), ("ffn2_wd", 1), ("ple_wg", 1),
       ("ple_wp", 2), ("ab_w_in", 2), ("ab_w_out", 1), ("c_w_in", 2), ("c_w_out", 1)]
SMALL = [("ln_g", 2), ("ln_b", 2), ("b_conv_w", 2), ("c_conv_w", 2)]
REPL = ["ple_bg", "a_sinks", "b_conv_b", "b_wa", "b_ba", "b_wx", "b_bx", "b_lam", "c_a_log", "c_dt_bias", "c_norm_g"]
WEIGHTS = ["ffn1_wg", "ffn1_wu", "ffn1_wd", "ffn2_wg", "ffn2_wu", "ffn2_wd", "ln_g", "ln_b", "ple_wg", "ple_bg", "ple_wp",
           "ab_w_in", "a_sinks", "b_conv_w", "b_conv_b", "b_wa", "b_ba", "b_wx", "b_bx", "b_lam", "ab_w_out", "c_w_in",
           "c_conv_w", "c_a_log", "c_dt_bias", "c_norm_g", "c_w_out"]
PACK_COLS = 1024
PACK_ALIGN = 16 * PACK_COLS


def _as_bf16_bits(a):
    return lax.bitcast_convert_type(a, jnp.bfloat16).reshape(a.shape[:-1] + (2 * a.shape[-1],))


def _from_bf16_bits(a):
    return lax.bitcast_convert_type(a.reshape(a.shape[:-1] + (a.shape[-1] // 2, 2)), F32)


def _pad_rows(flat, align=PACK_ALIGN):
    n = flat.shape[-1]
    total = -(-n // align) * align
    flat = jnp.pad(flat, [(0, 0)] * (flat.ndim - 1) + [(0, total - n)])
    return flat.reshape(flat.shape[:-1] + (total // PACK_COLS, PACK_COLS))


def _join(blocks, axis):
    moved = jnp.moveaxis(blocks, 0, axis)
    shp = list(moved.shape)
    return moved.reshape(shp[:axis] + [shp[axis] * shp[axis + 1]] + shp[axis + 2:])


def _split(full, axis):
    shp = list(full.shape)
    return jnp.moveaxis(full.reshape(shp[:axis] + [N_DEV, shp[axis] // N_DEV] + shp[axis + 1:]), axis, 0)


def _dense_blocks(w):
    z = jnp.zeros((4, 2, 64, 2, 64), w.dtype)
    w4 = w.reshape(4, 2, 64, 64)
    z = z.at[:, 0, :, 0, :].set(w4[:, 0]).at[:, 1, :, 1, :].set(w4[:, 1])
    return z.reshape(4, 128, 128)


def _diag_blocks(d):
    d5 = d.reshape(4, 2, 64, 2, 64)
    return jnp.stack([d5[:, 0, :, 0, :], d5[:, 1, :, 1, :]], axis=1).reshape(8, 64, 64)


def kernel(x, p, ffn1_wg, ffn1_wu, ffn1_wd, ffn2_wg, ffn2_wu, ffn2_wd, ln_g, ln_b, ple_wg, ple_bg, ple_wp, ab_w_in, a_sinks, b_conv_w, b_conv_b, b_wa, b_ba, b_wx, b_bx, b_lam, ab_w_out, c_w_in, c_conv_w, c_a_log, c_dt_bias, c_norm_g, c_w_out, loss_target, m_ffn1_wg, m_ffn1_wu, m_ffn1_wd, m_ffn2_wg, m_ffn2_wu, m_ffn2_wd, m_ln_g, m_ln_b, m_ple_wg, m_ple_bg, m_ple_wp, m_ab_w_in, m_a_sinks, m_b_conv_w, m_b_conv_b, m_b_wa, m_b_ba, m_b_wx, m_b_bx, m_b_lam, m_ab_w_out, m_c_w_in, m_c_conv_w, m_c_a_log, m_c_dt_bias, m_c_norm_g, m_c_w_out, v_ffn1_wg, v_ffn1_wu, v_ffn1_wd, v_ffn2_wg, v_ffn2_wu, v_ffn2_wd, v_ln_g, v_ln_b, v_ple_wg, v_ple_bg, v_ple_wp, v_ab_w_in, v_a_sinks, v_b_conv_w, v_b_conv_b, v_b_wa, v_b_ba, v_b_wx, v_b_bx, v_b_lam, v_ab_w_out, v_c_w_in, v_c_conv_w, v_c_a_log, v_c_dt_bias, v_c_norm_g, v_c_w_out):
    a = dict(locals())
    return _step3(a)


def _step3(a):
    x, p = a["x"], a["p"]
    bsz, s_len, d = x.shape
    t = bsz * s_len
    x2 = x.reshape(t, d)
    tgt = a["loss_target"].reshape(t, d)
    p2 = p.reshape(DEPTH, t, D_PLE)
    shapes = {n: a[n].shape for n in WEIGHTS}
    n_small = sum(int(np.prod(shapes[n])) for n in SMALL_NAMES)
    small_all = SMALL_NAMES + REPL
    f_ff = shapes["ffn1_wg"][2]
    c_cols = shapes["c_w_in"][2]
    wide = dict(tm=1024, tn=1408, tk=1024)
    tall = dict(tm=1408, tn=1024, tk=1024)

    def cast(z):
        return z.astype(MM)

    def ffn_shards(which, l):
        return [cast(a[which + "_wg"][l]), cast(a[which + "_wu"][l]), cast(a[which + "_wd"][l])]

    def ffn_weights(gat, tag):
        return (join_cols(gat[0], name=f"join_{tag}_wg"), join_cols(gat[1], name=f"join_{tag}_wu"),
                gat[2].reshape(N_DEV * gat[2].shape[1], D_MODEL))

    def rows_full(gat):
        return gat.reshape(N_DEV * gat.shape[1], D_MODEL)

    small_send = _flat_pad([a[n] for n in SMALL_NAMES], F32, 32 * LANES).reshape(32, LANES)
    g0 = gather_multi(ffn_shards("ffn1", 0) + [small_send], name="gather_first")
    ws = _take(g0[3].reshape(N_DEV, -1), SMALL_NAMES, shapes)
    small = {n: _join(ws[n], 2) for n in SMALL_NAMES}
    ln_g, ln_b = small["ln_g"], small["ln_b"]
    wa_d, wx_d = _dense_blocks(a["b_wa"][0]), _dense_blocks(a["b_wx"][0])
    lru_w = (small["b_conv_w"][0], a["b_conv_b"][0], wa_d, a["b_ba"][0], wx_d, a["b_bx"][0], a["b_lam"][0])
    gdc_w = (a["c_a_log"][0], a["c_dt_bias"][0], a["c_norm_g"][0])
    wf = {("ffn1", 0): ffn_weights(g0[:3], "ffn1_0")}

    s0 = {"x0": x2}
    u1 = ffn_shards("ffn2", 0)
    side = ("gather", u1[:2] + [cast(a["ab_w_in"][0]), cast(a["ab_w_out"][0])])
    (s0["y1"], s0["z1"], s0["hg1"], s0["hu1"]), got = ffn_fwd(x2, *wf["ffn1", 0], ln_g[0, 0], ln_b[0, 0],
                                                             name="ffn1_fwd_0", tm=FFN_TM, tf=FFN_TF, side=side)
    ab_w_in, ab_w_out = join_cols(got[2], name="join_ab_in"), rows_full(got[3])
    s0["proj"] = matmul(s0["y1"], ab_w_in, mode="nn", out_dtype=MM, name="ab_in_fwd", tn=896, tk=1024)
    ya, got_ple = attn_fwd(s0["proj"], a["a_sinks"][0], bsz, name="attn_fwd",
                           side=("gather", [u1[2], cast(a["ple_wg"][0]), cast(a["ple_wp"][0])]))
    wf["ffn2", 0] = ffn_weights(got[:2] + got_ple[:1], "ffn2_0")
    got_ple = got_ple[1:]
    yb = lru_fwd(s0["proj"], *lru_w, bsz, name="lru_fwd")
    s0["mix"] = jnp.concatenate([ya, yb], axis=1)
    s0["y2"], s0["z2"] = mm_ln_fwd(s0["mix"], ab_w_out, s0["y1"], ln_g[0, 1], ln_b[0, 1], name="mix_out_fwd_0")
    side = ("gather", ffn_shards("ffn1", 1))
    (s0["y3"], s0["z3"], s0["hg2"], s0["hu2"]), got = ffn_fwd(s0["y2"], *wf["ffn2", 0], ln_g[0, 2], ln_b[0, 2],
                                                             name="ffn2_fwd_0", tm=FFN_TM, tf=FFN_TF, side=side)
    wf["ffn1", 1] = ffn_weights(got, "ffn1_1")
    ple_wg = [rows_full(got_ple[0]), None]
    ple_wp = [_join(got_ple[1], 1), None]
    h1 = ple_fwd(s0["y3"], p2[0], ple_wg[0], a["ple_bg"][0], ple_wp[0], name="ple_fwd_0")

    s1 = {"x0": h1}
    side = ("gather", [cast(a["c_w_in"][0]), cast(a["c_w_out"][0])])
    (s1["y1"], s1["z1"], s1["hg1"], s1["hu1"]), got = ffn_fwd(h1, *wf["ffn1", 1], ln_g[1, 0], ln_b[1, 0],
                                                             name="ffn1_fwd_1", tm=FFN_TM, tf=FFN_TF, side=side)
    c_in_main, c_in_gate = join_cols(got[0], name="join_c_in", outs=[(0, 4 * C_WIDTH, 4 * C_WIDTH),
                                                                      (4 * C_WIDTH, 4 * C_WIDTH + 2 * C_HEADS, LANES)])
    c_w_out = rows_full(got[1])
    s1["proj"] = matmul(s1["y1"], c_in_main, mode="nn", out_dtype=MM, name="c_in_fwd", tm=1024, tn=2048, tk=1024)
    s1["gates"] = matmul(s1["y1"], c_in_gate, mode="nn", name="c_gate_fwd", tk=1024)
    s1["qkv"] = gdc_pre_fwd(s1["proj"], small["c_conv_w"][0], bsz, name="gdc_pre_fwd")
    side = ("gather", ffn_shards("ffn2", 1) + [cast(a["ple_wg"][1]), cast(a["ple_wp"][1])])
    (s1["mix"], s1["o_pre"], s1["tmat"], s1["states"]), got = gdc_fwd(
        s1["qkv"], s1["proj"], s1["gates"], *gdc_w, bsz, name="gdc_fwd", side=side)
    wf["ffn2", 1] = ffn_weights(got[:3], "ffn2_1")
    ple_wg[1], ple_wp[1] = rows_full(got[3]), _join(got[4], 1)
    s1["y2"], s1["z2"] = mm_ln_fwd(s1["mix"], c_w_out, s1["y1"], ln_g[1, 1], ln_b[1, 1], name="mix_out_fwd_1")
    s1["y3"], s1["z3"], s1["hg2"], s1["hu2"] = ffn_fwd(s1["y2"], *wf["ffn2", 1], ln_g[1, 2], ln_b[1, 2], name="ffn2_fwd_1",
                                                           tm=FFN_TM, tf=FFN_TF)
    h2 = ple_fwd(s1["y3"], p2[1], ple_wg[1], a["ple_bg"][1], ple_wp[1], name="ple_fwd_1")
    loss_part, dh = loss_fwd_bwd(h2, tgt, name="loss")

    def ffn_parts(xin, act, dhg, dhu, dz, tag):
        dwg = matmul(xin, dhg, mode="tn", name=f"{tag}_wg_grad", **wide)
        dwu = matmul(xin, dhu, mode="tn", name=f"{tag}_wu_grad", **wide)
        dwd = matmul(act, dz, mode="tn", scale=0.5, out_dtype=MM, name=f"{tag}_wd_grad", **tall)
        return [split_cols([(dwg, N_DEV * f_ff)], f_ff, name=f"split_{tag}_wg"),
                split_cols([(dwu, N_DEV * f_ff)], f_ff, name=f"split_{tag}_wu"), dwd.reshape(N_DEV, f_ff, D_MODEL)]

    def ple_parts(i, s, dt, de):
        gwg = matmul(s["y3"], dt, mode="tn", out_dtype=MM, name=f"ple_wg_grad_{i}", tm=1024, tn=1024)
        gwp = matmul(p2[i], de, mode="tn", out_dtype=MM, name=f"ple_wp_grad_{i}", tn=1024)
        return [gwg.reshape(N_DEV, D_MODEL // N_DEV, D_MODEL), _split(gwp, 1)]

    gln = {"ln_g": [None, None], "ln_b": [None, None]}
    gple_bg = [None, None]

    dz3, dt, de, dbg, dg2, db2 = ple_bwd(dh, s1["y3"], p2[1], ple_wg[1], a["ple_bg"][1], ple_wp[1], s1["z3"], ln_g[1, 2],
                                         name="ple_bwd_1")
    gple_bg[1] = dbg[0]
    parts_ple1 = ple_parts(1, s1, dt, de)
    dy2, act, dhg, dhu = ffn_bwd(dz3, s1["hg2"], s1["hu2"], *wf["ffn2", 1], name="ffn2_bwd_1", tm=FFN_TM, tf=FFN_TF)
    parts_ffn2_1 = ffn_parts(s1["y2"], act, dhg, dhu, dz3, "ffn2_1")
    dz2, dg1, db1 = ln_bwd(dy2, s1["z2"], ln_g[1, 1], name="ln1_bwd_1")
    dmix = matmul(dz2, c_w_out, mode="nt", name="c_out_bwd", tn=1024, tk=1024)
    parts_c_out = matmul(s1["mix"], dz2, mode="tn", out_dtype=MM, name="c_out_grad", tm=1024, tn=1024).reshape(
        N_DEV, D_MODEL // N_DEV, D_MODEL)
    (dqkv, dzc, dgates, dsm), recv1 = gdc_bwd(s1["qkv"], s1["proj"], s1["gates"], *gdc_w, s1["o_pre"],
                                              s1["tmat"], s1["states"], dmix, bsz, name="gdc_bwd",
                                              side=("exchange", parts_ffn2_1 + parts_ple1))
    dproj, dccw = gdc_pre_bwd(s1["proj"], small["c_conv_w"][0], dqkv, dzc, bsz, name="gdc_pre_bwd")
    dgb = dgates.astype(MM)
    dy1 = matmul(dproj, c_in_main, mode="nt", add=dz2, add_scale=DN_ALPHA, name="c_in_bwd", tn=1024, tk=4096)
    dy1 = matmul(dgb, c_in_gate, mode="nt", add=dy1, name="c_gate_bwd", tn=1024)
    g_c_main = matmul(s1["y1"], dproj, mode="tn", name="c_in_grad", tm=1024, tn=1024, tk=1024)
    g_c_gate = matmul(s1["y1"], dgb, mode="tn", name="c_gate_grad", tm=1024)
    parts_c_in = split_cols([(g_c_main, 4 * C_WIDTH), (g_c_gate, 2 * C_HEADS)], c_cols, name="split_c_in")
    dz1, dg0, db0 = ln_bwd(dy1, s1["z1"], ln_g[1, 0], name="ln0_bwd_1")
    (dh, act, dhg, dhu), recv_c = ffn_bwd(dz1, s1["hg1"], s1["hu1"], *wf["ffn1", 1], name="ffn1_bwd_1", tm=FFN_TM,
                                          tf=FFN_TF, side=("exchange", [parts_c_in, parts_c_out]))
    parts_ffn1_1 = ffn_parts(s1["x0"], act, dhg, dhu, dz1, "ffn1_1")
    gln["ln_g"][1] = jnp.concatenate([dg0, dg1, dg2], axis=0)
    gln["ln_b"][1] = jnp.concatenate([db0, db1, db2], axis=0)

    dz3, dt, de, dbg, dg2, db2 = ple_bwd(dh, s0["y3"], p2[0], ple_wg[0], a["ple_bg"][0], ple_wp[0], s0["z3"], ln_g[0, 2],
                                         name="ple_bwd_0")
    gple_bg[0] = dbg[0]
    parts_ple0 = ple_parts(0, s0, dt, de)
    (dy2, act, dhg, dhu), recv2 = ffn_bwd(dz3, s0["hg2"], s0["hu2"], *wf["ffn2", 0], name="ffn2_bwd_0", tm=FFN_TM,
                                          tf=FFN_TF, side=("exchange", parts_ffn1_1))
    parts_ffn2_0 = ffn_parts(s0["y2"], act, dhg, dhu, dz3, "ffn2_0")
    dz2, dg1, db1 = ln_bwd(dy2, s0["z2"], ln_g[0, 1], name="ln1_bwd_0")
    dmix = matmul(dz2, ab_w_out, mode="nt", name="ab_out_bwd", tn=1024, tk=1024)
    parts_ab_out = matmul(s0["mix"], dz2, mode="tn", out_dtype=MM, name="ab_out_grad", tm=1024, tn=1024).reshape(
        N_DEV, D_MODEL // N_DEV, D_MODEL)
    (dq, dk, dv, dsk), recv3a = attn_bwd(s0["proj"], a["a_sinks"][0], dmix, bsz, name="attn_bwd",
                                         side=("exchange", parts_ffn2_0[:2]))
    (dbx, dbgate, dcw, dcb, dwa, dba, dwx, dbxb, dlam), recv3b = lru_bwd(
        s0["proj"], *lru_w, dmix, bsz, name="lru_bwd", side=("exchange", [parts_ffn2_0[2]] + parts_ple0 + [parts_ab_out]))
    dproj = jnp.concatenate([dq, dk, dv, dbx, dbgate], axis=1).astype(MM)
    dy1 = matmul(dproj, ab_w_in, mode="nt", add=dz2, add_scale=DN_ALPHA, name="ab_in_bwd", tn=1024, tk=1792)
    g_ab_in = matmul(s0["y1"], dproj, mode="tn", name="ab_in_grad", tm=1024, tn=896)
    parts_ab_in = split_cols([(g_ab_in, AB_PROJ)], AB_PROJ // N_DEV, name="split_ab_in")
    dz1, dg0, db0 = ln_bwd(dy1, s0["z1"], ln_g[0, 0], name="ln0_bwd_0")
    gln["ln_g"][0] = jnp.concatenate([dg0, dg1, dg2], axis=0)
    gln["ln_b"][0] = jnp.concatenate([db0, db1, db2], axis=0)

    dsm_sum = jnp.sum(dsm, axis=0)
    full = dict(ln_g=jnp.stack(gln["ln_g"]), ln_b=jnp.stack(gln["ln_b"]), b_conv_w=dcw[None],
                c_conv_w=jnp.sum(dccw, axis=0)[None], ple_bg=jnp.stack(gple_bg),
                a_sinks=jnp.sum(dsk, axis=0)[:, :A_HEADS], b_conv_b=dcb, b_wa=_diag_blocks(dwa)[None], b_ba=dba,
                b_wx=_diag_blocks(dwx)[None], b_bx=dbxb, b_lam=dlam, c_a_log=dsm_sum[None, :C_HEADS, 0],
                c_dt_bias=dsm_sum[None, :C_HEADS, 1], c_norm_g=jnp.sum(dsm_sum[C_HEADS:], axis=0)[None])
    small_rows = SMALL_F32 // LANES
    repl_flat = _flat_pad([full[n] for n in REPL], F32, SMALL_F32 - n_small)
    small8 = jnp.concatenate([_flat8_pad([_split(full[n], 2) for n in SMALL_NAMES], F32, n_small),
                              jnp.broadcast_to(repl_flat, (N_DEV,) + repl_flat.shape)], axis=1)
    (dh, act, dhg, dhu), recv3c = ffn_bwd(dz1, s0["hg1"], s0["hu1"], *wf["ffn1", 0], name="ffn1_bwd_0", tm=FFN_TM,
                                          tf=FFN_TF, side=("exchange", [parts_ab_in, small8.reshape(N_DEV, small_rows, LANES)]))
    grad_x = dh.reshape(bsz, s_len, d)

    dwg = matmul(s0["x0"], dhg, mode="tn", name="ffn1_0_wg_grad", **wide)
    parts_wg = split_cols([(dwg, N_DEV * f_ff)], f_ff, name="split_ffn1_0_wg")
    dwu, recv4a = matmul(s0["x0"], dhu, mode="tn", name="ffn1_0_wu_grad", side=("exchange", [parts_wg]), **wide)
    parts_wu = split_cols([(dwu, N_DEV * f_ff)], f_ff, name="split_ffn1_0_wu")
    dwd, recv4b = matmul(act, dz1, mode="tn", scale=0.5, out_dtype=MM, name="ffn1_0_wd_grad",
                         side=("exchange", [parts_wu]), **tall)
    recv4c = exchange_multi([dwd.reshape(N_DEV, f_ff, D_MODEL)], name="exchange_last")

    def upd(parts, n, l, shape2d, **kw):
        wmv = [a[pre + n][l].reshape(shape2d) for pre in ("", "m_", "v_")]
        return adamw_rows(parts, 0, *wmv, name=f"adamw_{n}_{l}", **kw)

    def upd_ffn(parts, which, l):
        return {(which + "_wg", l): upd(parts[0], which + "_wg", l, (D_MODEL, f_ff)),
                (which + "_wu", l): upd(parts[1], which + "_wu", l, (D_MODEL, f_ff)),
                (which + "_wd", l): upd(parts[2], which + "_wd", l, (f_ff, D_MODEL), tr=176)}

    rows8 = D_MODEL // N_DEV
    res = {}
    res.update(upd_ffn(recv1[:3], "ffn2", 1))
    res["ple_wg", 1] = upd(recv1[3], "ple_wg", 1, (rows8, D_MODEL), tr=128)
    res["ple_wp", 1] = upd(recv1[4], "ple_wp", 1, (D_PLE, LANES))
    res.update(upd_ffn(recv2, "ffn1", 1))
    res["c_w_in", 0] = upd(recv_c[0], "c_w_in", 0, (D_MODEL, c_cols))
    res["c_w_out", 0] = upd(recv_c[1], "c_w_out", 0, (rows8, D_MODEL), tr=128)
    res.update(upd_ffn(recv3a + recv3b[:1], "ffn2", 0))
    res["ple_wg", 0] = upd(recv3b[1], "ple_wg", 0, (rows8, D_MODEL), tr=128)
    res["ple_wp", 0] = upd(recv3b[2], "ple_wp", 0, (D_PLE, LANES))
    res["ab_w_out", 0] = upd(recv3b[3], "ab_w_out", 0, (rows8, D_MODEL), tr=128)
    res.update(upd_ffn(recv4a + recv4b + list(recv4c), "ffn1", 0))
    res["ab_w_in", 0] = upd(recv3c[0], "ab_w_in", 0, (D_MODEL, AB_PROJ // N_DEV))
    res_small = adamw_rows(recv3c[1], 0, *[_flat_pad([a[pre + n] for n in small_all], F32, SMALL_F32).reshape(
        small_rows, LANES) for pre in ("", "m_", "v_")], name="adamw_small", tr=small_rows)
    kinds = []
    for k in range(4):
        kd = _take(res_small[k].reshape(-1), small_all, shapes)
        for n in WEIGHTS:
            if n not in kd:
                kd[n] = jnp.stack([res[n, l][k] for l in range(shapes[n][0])]).reshape(shapes[n])
        kinds.append(kd)
    loss = lax.psum(loss_part[0, 0], ("x", "y", "c"))
    return (loss, grad_x, *[kinds[0][n] for n in WEIGHTS], *[kinds[1][n] for n in WEIGHTS],
            *[kinds[2][n] for n in WEIGHTS], *[kinds[3][n] for n in WEIGHTS])


def join_cols(x, *, name, outs=None, tk=256):
    _, kk, n = x.shape
    tk = _tile8(kk, tk)
    outs = outs or [(0, N_DEV * n, N_DEV * n)]

    def body(x_ref, *o_refs):
        full = jnp.concatenate([x_ref[k] for k in range(N_DEV)], axis=-1)
        for (lo, hi, wd), o_ref in zip(outs, o_refs):
            piece = full[:, lo:hi]
            if wd > hi - lo:
                piece = jnp.concatenate([piece, jnp.zeros((tk, wd - (hi - lo)), piece.dtype)], axis=-1)
            o_ref[...] = piece

    res = _pcall(
        body, name=name, grid=(kk // tk,), in_specs=[pl.BlockSpec((N_DEV, tk, n), lambda i: (0, i, 0))],
        out_specs=[pl.BlockSpec((tk, wd), lambda i: (i, 0)) for _, _, wd in outs],
        out_shape=[jax.ShapeDtypeStruct((kk, wd), x.dtype) for _, _, wd in outs], compiler_params=_cp("parallel"),
    )(x)
    return res if len(outs) > 1 else res[0]


def split_cols(pieces, n, *, name, tk=256):
    kk = pieces[0][0].shape[0]
    tk = _tile8(kk, tk)

    def body(*refs):
        o_ref = refs[-1]
        vals = [r[...][:, :used] for r, (_, used) in zip(refs[:-1], pieces)]
        full = vals[0] if len(vals) == 1 else jnp.concatenate(vals, axis=-1)
        for k in range(N_DEV):
            o_ref[k] = full[:, k * n:(k + 1) * n].astype(MM)

    return _pcall(
        body, name=name, grid=(kk // tk,),
        in_specs=[pl.BlockSpec((tk, arr.shape[1]), lambda i: (i, 0)) for arr, _ in pieces],
        out_specs=pl.BlockSpec((N_DEV, tk, n), lambda i: (0, i, 0)),
        out_shape=jax.ShapeDtypeStruct((N_DEV, kk, n), MM), compiler_params=_cp("parallel"),
    )(*[arr for arr, _ in pieces])


def gather_multi(shards, *, name):
    ng = len(shards)

    def body(*refs):
        x_refs, o_refs = refs[:ng], refs[ng:2 * ng]
        send_sems, recv_sems, local_sems = refs[2 * ng:]
        x, y, c = _me()
        sibling = (x, y, 1 - c)
        chips = [(1 - x, y), (x, 1 - y), (1 - x, 1 - y)]

        def slot(px, py, pc):
            return 4 * px + 2 * py + pc

        def copy(gi, k, block, to, src=None):
            dst = o_refs[gi].at[slot(*block)]
            return pltpu.make_async_remote_copy(
                src_ref=dst if src is None else src, dst_ref=dst, send_sem=send_sems.at[7 * gi + k],
                recv_sem=recv_sems.at[7 * gi + k], device_id=to, device_id_type=MESH_ID)

        own = [pltpu.make_async_copy(x_refs[gi], o_refs[gi].at[slot(x, y, c)], local_sems.at[gi]) for gi in range(ng)]
        for cp in own:
            cp.start()
        first = []
        for gi in range(ng):
            first.append(copy(gi, 0, (x, y, c), sibling, src=x_refs[gi]))
            first += [copy(gi, 1 + j, (x, y, c), (*chip, c), src=x_refs[gi]) for j, chip in enumerate(chips)]
        for cp in first:
            cp.start()
        passed = []
        for j, chip in enumerate(chips):
            for gi in range(ng):
                copy(gi, 1 + j, (*chip, c), (x, y, c)).wait_recv()
                fwd = copy(gi, 4 + j, (*chip, c), sibling)
                fwd.start()
                passed.append(fwd)
        for gi in range(ng):
            copy(gi, 0, sibling, (x, y, c)).wait_recv()
            for j, chip in enumerate(chips):
                copy(gi, 4 + j, (*chip, 1 - c), (x, y, c)).wait_recv()
        for cp in first + passed:
            cp.wait_send()
        for cp in own:
            cp.wait()

    hbm = pl.BlockSpec(memory_space=pl.ANY)
    return _pcall(
        body, name=name, in_specs=[hbm] * ng, out_specs=[hbm] * ng,
        out_shape=[jax.ShapeDtypeStruct((N_DEV,) + s.shape, s.dtype) for s in shards],
        scratch_shapes=[pltpu.SemaphoreType.DMA((7 * ng,)), pltpu.SemaphoreType.DMA((7 * ng,)),
                        pltpu.SemaphoreType.DMA((ng,))],
    )(*shards)


def exchange_multi(parts, *, name):
    ng = len(parts)

    def body(*refs):
        x_refs, o_refs = refs[:ng], refs[ng:2 * ng]
        send_sems, recv_sems, local_sems = refs[2 * ng:]
        x, y, c = _me()
        mine = 4 * x + 2 * y + c
        own = [pltpu.make_async_copy(x_refs[gi].at[mine], o_refs[gi].at[mine], local_sems.at[gi]) for gi in range(ng)]
        for cp in own:
            cp.start()
        copies = []
        for k, (dx, dy, dc) in enumerate(_FLIPS):
            px, py, pc = _flip(x, dx), _flip(y, dy), _flip(c, dc)
            for gi in range(ng):
                cp = pltpu.make_async_remote_copy(
                    src_ref=x_refs[gi].at[4 * px + 2 * py + pc], dst_ref=o_refs[gi].at[mine],
                    send_sem=send_sems.at[7 * gi + k], recv_sem=recv_sems.at[7 * gi + k], device_id=(px, py, pc),
                    device_id_type=MESH_ID)
                cp.start()
                copies.append(cp)
        for cp in copies:
            cp.wait()
        for cp in own:
            cp.wait()

    hbm = pl.BlockSpec(memory_space=pl.ANY)
    return _pcall(
        body, name=name, in_specs=[hbm] * ng, out_specs=[hbm] * ng,
        out_shape=[jax.ShapeDtypeStruct(s.shape, s.dtype) for s in parts],
        scratch_shapes=[pltpu.SemaphoreType.DMA((7 * ng,)), pltpu.SemaphoreType.DMA((7 * ng,)),
                        pltpu.SemaphoreType.DMA((ng,))],
    )(*parts)


def adamw_rows(parts, row0, w, m, v, *, name, tr=256):
    r, cdim = w.shape
    tr = _tile8(math.gcd(r, row0) if row0 else r, tr)
    blk0 = row0 // tr

    def body(p_ref, w_ref, m_ref, v_ref, g_ref, d_ref, mo_ref, vo_ref):
        g = p_ref[0].astype(F32)
        for j in range(1, N_DEV):
            g = g + p_ref[j].astype(F32)
        g_ref[...] = g
        mn = ADAM_B1 * m_ref[...] + (1.0 - ADAM_B1) * g
        vn = ADAM_B2 * v_ref[...] + (1.0 - ADAM_B2) * (g * g)
        mo_ref[...] = mn
        vo_ref[...] = vn
        m_hat = mn / (1.0 - ADAM_B1 ** ADAM_STEP)
        v_hat = vn / (1.0 - ADAM_B2 ** ADAM_STEP)
        d_ref[...] = -ADAM_LR * (m_hat / (jnp.sqrt(v_hat) + ADAM_EPS) + ADAM_WD * w_ref[...])

    blk = pl.BlockSpec((tr, cdim), lambda i: (i, 0))
    shp = jax.ShapeDtypeStruct((r, cdim), F32)
    return _pcall(
        body, name=name, grid=(r // tr,),
        in_specs=[pl.BlockSpec((N_DEV, tr, cdim), lambda i: (0, blk0 + i, 0)), blk, blk, blk],
        out_specs=[blk, blk, blk, blk], out_shape=[shp, shp, shp, shp], compiler_params=_cp("parallel"),
    )(parts, w, m, v)


GROUP_A = ["ffn1_wg", "ffn1_wu", "ffn2_wg", "ffn2_wu"]
GROUP_B = ["ffn1_wd", "ffn2_wd", "ple_wg", "ab_w_out", "c_w_out"]
SMALL_NAMES = ["ln_g", "ln_b", "b_conv_w", "c_conv_w"]
LANES = 128
FFN_TM = 512
FFN_TF = 1408
SMALL_F32 = 73728
PLE_WP_ROWS = DEPTH * D_PLE


def _step2(a):
    x, p = a["x"], a["p"]
    bsz, s_len, d = x.shape
    t = bsz * s_len
    x2 = x.reshape(t, d)
    tgt = a["loss_target"].reshape(t, d)
    p2 = p.reshape(DEPTH, t, D_PLE)
    shapes = {n: a[n].shape for n in WEIGHTS}
    bits_per = 1 if MM == F32 else 2
    n_small = sum(int(np.prod(shapes[n])) for n in SMALL_NAMES)
    small_all = SMALL_NAMES + REPL
    f_ff = shapes["ffn1_wg"][2]
    rows_b = {n: shapes[n][0] * shapes[n][1] for n in GROUP_B}
    off_b = dict(zip(GROUP_B, np.cumsum([0] + [rows_b[n] for n in GROUP_B])[:-1].tolist()))

    send = [
        jnp.concatenate([a[n].astype(MM).reshape(-1, f_ff) for n in GROUP_A], axis=0),
        jnp.concatenate([a[n].astype(MM).reshape(-1, D_MODEL) for n in GROUP_B], axis=0),
        a["ab_w_in"][0].astype(MM),
        a["c_w_in"][0].astype(MM),
        a["ple_wp"].astype(MM).reshape(PLE_WP_ROWS, LANES),
        _flat_pad([a[n] for n in SMALL_NAMES], F32, 32 * LANES).reshape(32, LANES),
    ]
    ga, gb, gc, gd, ge, gf = gather_multi(send, name="gather_weights")
    wa_full = join_cols(ga, name="join_ffn").reshape(len(GROUP_A), DEPTH, D_MODEL, N_DEV * f_ff)
    w = {n: wa_full[i] for i, n in enumerate(GROUP_A)}
    for n in GROUP_B:
        lyr, rws = shapes[n][0], shapes[n][1]
        blk = gb[:, off_b[n]:off_b[n] + rows_b[n]].reshape(N_DEV, lyr, rws, D_MODEL)
        w[n] = jnp.swapaxes(blk, 0, 1).reshape(lyr, N_DEV * rws, D_MODEL)
    w["ab_w_in"] = join_cols(gc, name="join_ab_in")
    c_in_main, c_in_gate = join_cols(gd, name="join_c_in", outs=[(0, 4 * C_WIDTH, 4 * C_WIDTH),
                                                                  (4 * C_WIDTH, 4 * C_WIDTH + 2 * C_HEADS, LANES)])
    w["ple_wp"] = _join(ge.reshape(N_DEV, DEPTH, D_PLE, LANES), 2)
    ws = _take(gf.reshape(N_DEV, -1), SMALL_NAMES, shapes)
    w.update({n: _join(ws[n], 2) for n in SMALL_NAMES})
    ln_g, ln_b = w["ln_g"], w["ln_b"]
    wa_d, wx_d = _dense_blocks(a["b_wa"][0]), _dense_blocks(a["b_wx"][0])
    lru_w = (w["b_conv_w"][0], a["b_conv_b"][0], wa_d, a["b_ba"][0], wx_d, a["b_bx"][0], a["b_lam"][0])
    gdc_w = (a["c_a_log"][0], a["c_dt_bias"][0], a["c_norm_g"][0])

    h = x2
    saved = []
    for i in range(DEPTH):
        s = {"x0": h}
        s["y1"], s["z1"], s["hg1"], s["hu1"] = ffn_fwd(h, w["ffn1_wg"][i], w["ffn1_wu"][i], w["ffn1_wd"][i], ln_g[i, 0], ln_b[i, 0],
                                   name=f"ffn1_fwd_{i}")
        if i == 0:
            s["proj"] = matmul(s["y1"], w["ab_w_in"], mode="nn", name="ab_in_fwd", tn=896, tk=1024)
            ya = attn_fwd(s["proj"], a["a_sinks"][0], bsz, name="attn_fwd")
            yb = lru_fwd(s["proj"], *lru_w, bsz, name="lru_fwd")
            s["mix"] = jnp.concatenate([ya, yb], axis=1)
            w_out = w["ab_w_out"][0]
        else:
            s["proj"] = matmul(s["y1"], c_in_main, mode="nn", name="c_in_fwd", tm=1024, tn=2048, tk=1024)
            s["gates"] = matmul(s["y1"], c_in_gate, mode="nn", name="c_gate_fwd", tk=1024)
            s["qkv"] = gdc_pre_fwd(s["proj"], w["c_conv_w"][0], bsz, name="gdc_pre_fwd")
            s["mix"], s["o_pre"], s["vnew"], s["tmat"], s["states"] = gdc_fwd(
                s["qkv"], s["proj"], s["gates"], *gdc_w, bsz, name="gdc_fwd")
            w_out = w["c_w_out"][0]
        s["y2"], s["z2"] = mm_ln_fwd(s["mix"], w_out, s["y1"], ln_g[i, 1], ln_b[i, 1], name=f"mix_out_fwd_{i}")
        s["y3"], s["z3"], s["hg2"], s["hu2"] = ffn_fwd(s["y2"], w["ffn2_wg"][i], w["ffn2_wu"][i], w["ffn2_wd"][i], ln_g[i, 2], ln_b[i, 2],
                                   name=f"ffn2_fwd_{i}")
        h = ple_fwd(s["y3"], p2[i], w["ple_wg"][i], a["ple_bg"][i], w["ple_wp"][i], name=f"ple_fwd_{i}")
        saved.append(s)
    loss_part, dh = loss_fwd_bwd(h, tgt, name="loss")

    g = {n: [None] * shapes[n][0] for n in ("ffn1_wg", "ffn1_wu", "ffn1_wd", "ffn2_wg", "ffn2_wu", "ffn2_wd", "ln_g",
                                             "ln_b", "ple_wg", "ple_bg", "ple_wp")}
    wide = dict(tm=1024, tn=1408, tk=1024)
    tall = dict(tm=1408, tn=1024, tk=1024)
    for i in reversed(range(DEPTH)):
        s = saved[i]
        dy3, dt, de, dbg = ple_bwd(dh, s["y3"], p2[i], w["ple_wg"][i], a["ple_bg"][i], w["ple_wp"][i], name=f"ple_bwd_{i}")
        g["ple_wg"][i] = matmul(s["y3"], dt, mode="tn", name=f"ple_wg_grad_{i}", tm=1024, tn=1024)
        g["ple_wp"][i] = matmul(p2[i], de, mode="tn", name=f"ple_wp_grad_{i}", tn=1024)
        g["ple_bg"][i] = dbg[0]
        dz3, dg2, db2 = ln_bwd(dy3, s["z3"], ln_g[i, 2], name=f"ln2_bwd_{i}")
        dy2, act, dhg, dhu = ffn_bwd(dz3, s["hg2"], s["hu2"], w["ffn2_wg"][i], w["ffn2_wu"][i], w["ffn2_wd"][i], name=f"ffn2_bwd_{i}", tm=FFN_TM, tf=FFN_TF)
        g["ffn2_wg"][i] = matmul(s["y2"], dhg, mode="tn", name=f"ffn2_wg_grad_{i}", **wide)
        g["ffn2_wu"][i] = matmul(s["y2"], dhu, mode="tn", name=f"ffn2_wu_grad_{i}", **wide)
        g["ffn2_wd"][i] = matmul(act, dz3, mode="tn", scale=0.5, name=f"ffn2_wd_grad_{i}", **tall)
        dz2, dg1, db1 = ln_bwd(dy2, s["z2"], ln_g[i, 1], name=f"ln1_bwd_{i}")
        if i == 0:
            dmix = matmul(dz2, w["ab_w_out"][0], mode="nt", name="ab_out_bwd", tn=1024, tk=1024)
            g["ab_w_out"] = matmul(s["mix"], dz2, mode="tn", name="ab_out_grad", tm=1024, tn=1024)
            dq, dk, dv, dsk = attn_bwd(s["proj"], a["a_sinks"][0], dmix, bsz, name="attn_bwd")
            dbx, dbgate, dcw, dcb, dwa, dba, dwx, dbxb, dlam = lru_bwd(s["proj"], *lru_w, dmix, bsz, name="lru_bwd")
            dproj = jnp.concatenate([dq, dk, dv, dbx, dbgate], axis=1).astype(MM)
            dy1 = matmul(dproj, w["ab_w_in"], mode="nt", add=dz2, add_scale=DN_ALPHA, name="ab_in_bwd", tn=1024, tk=1792)
            g_ab_in = matmul(s["y1"], dproj, mode="tn", name="ab_in_grad", tm=1024, tn=896)
        else:
            dmix = matmul(dz2, w["c_w_out"][0], mode="nt", name="c_out_bwd", tn=1024, tk=1024)
            g["c_w_out"] = matmul(s["mix"], dz2, mode="tn", name="c_out_grad", tm=1024, tn=1024)
            dqkv, dzc, dgates, dsm = gdc_bwd(s["qkv"], s["proj"], s["gates"], *gdc_w, s["o_pre"], s["vnew"], s["tmat"],
                                             s["states"], dmix, bsz, name="gdc_bwd")
            draw, dccw = gdc_pre_bwd(s["proj"], w["c_conv_w"][0], dqkv, bsz, name="gdc_pre_bwd")
            dproj = jnp.concatenate([draw, dzc], axis=1).astype(MM)
            dgb = dgates.astype(MM)
            dy1 = matmul(dproj, c_in_main, mode="nt", add=dz2, add_scale=DN_ALPHA, name="c_in_bwd", tn=1024, tk=4096)
            dy1 = matmul(dgb, c_in_gate, mode="nt", add=dy1, name="c_gate_bwd", tn=1024)
            g_c_main = matmul(s["y1"], dproj, mode="tn", name="c_in_grad", tm=1024, tn=1024, tk=1024)
            g_c_gate = matmul(s["y1"], dgb, mode="tn", name="c_gate_grad", tm=1024)
        dz1, dg0, db0 = ln_bwd(dy1, s["z1"], ln_g[i, 0], name=f"ln0_bwd_{i}")
        dh, act, dhg, dhu = ffn_bwd(dz1, s["hg1"], s["hu1"], w["ffn1_wg"][i], w["ffn1_wu"][i], w["ffn1_wd"][i], name=f"ffn1_bwd_{i}", tm=FFN_TM, tf=FFN_TF)
        g["ffn1_wg"][i] = matmul(s["x0"], dhg, mode="tn", name=f"ffn1_wg_grad_{i}", **wide)
        g["ffn1_wu"][i] = matmul(s["x0"], dhu, mode="tn", name=f"ffn1_wu_grad_{i}", **wide)
        g["ffn1_wd"][i] = matmul(act, dz1, mode="tn", scale=0.5, name=f"ffn1_wd_grad_{i}", **tall)
        g["ln_g"][i] = jnp.concatenate([dg0, dg1, dg2], axis=0)
        g["ln_b"][i] = jnp.concatenate([db0, db1, db2], axis=0)
    grad_x = dh.reshape(bsz, s_len, d)
    full = {n: jnp.stack(v) if isinstance(v, list) else v[None] for n, v in g.items()}
    full["b_conv_w"] = dcw[None]
    full["c_conv_w"] = jnp.sum(dccw, axis=0)[None]
    dsm_sum = jnp.sum(dsm, axis=0)
    full.update(a_sinks=jnp.sum(dsk, axis=0)[:, :A_HEADS], b_conv_b=dcb, b_wa=_diag_blocks(dwa)[None], b_ba=dba,
                b_wx=_diag_blocks(dwx)[None], b_bx=dbxb, b_lam=dlam, c_a_log=dsm_sum[None, :C_HEADS, 0],
                c_dt_bias=dsm_sum[None, :C_HEADS, 1], c_norm_g=jnp.sum(dsm_sum[C_HEADS:], axis=0)[None])

    small_f32_rows = SMALL_F32 // LANES
    repl_flat = _flat_pad([full[n] for n in REPL], F32, SMALL_F32 - n_small)
    small8 = jnp.concatenate([_flat8_pad([_split(full[n], 2) for n in SMALL_NAMES], F32, n_small),
                              jnp.broadcast_to(repl_flat, (N_DEV,) + repl_flat.shape)], axis=1)
    parts = [
        split_cols([(jnp.concatenate([full[n].reshape(-1, N_DEV * f_ff) for n in GROUP_A], axis=0), N_DEV * f_ff)], f_ff,
                   name="split_ffn"),
        jnp.concatenate([_split(full[n], 1).astype(MM).reshape(N_DEV, -1, D_MODEL) for n in GROUP_B], axis=1),
        split_cols([(g_ab_in, AB_PROJ)], AB_PROJ // N_DEV, name="split_ab_in"),
        split_cols([(g_c_main, 4 * C_WIDTH), (g_c_gate, 2 * C_HEADS)], (4 * C_WIDTH + 2 * C_HEADS) // N_DEV,
                   name="split_c_in"),
        _split(full["ple_wp"], 2).astype(MM).reshape(N_DEV, PLE_WP_ROWS, LANES),
        small8.reshape(N_DEV, small_f32_rows, LANES),
    ]
    ra, rb, rc, rd, re, small_parts = exchange_multi(parts, name="exchange_grads")

    def wmv(n, shape2d):
        return [a[pre + n].reshape(shape2d) for pre in ("", "m_", "v_")]

    res = {}
    for i, n in enumerate(GROUP_A):
        res[n] = adamw_rows(ra, i * DEPTH * D_MODEL, *wmv(n, (DEPTH * D_MODEL, f_ff)), name=f"adamw_{n}")
    for n in GROUP_B:
        res[n] = adamw_rows(rb, off_b[n], *wmv(n, (rows_b[n], D_MODEL)), name=f"adamw_{n}", tr=64)
    res["ab_w_in"] = adamw_rows(rc, 0, *wmv("ab_w_in", (D_MODEL, AB_PROJ // N_DEV)), name="adamw_ab_w_in")
    res["c_w_in"] = adamw_rows(rd, 0, *wmv("c_w_in", (D_MODEL, shapes["c_w_in"][2])), name="adamw_c_w_in")
    res["ple_wp"] = adamw_rows(re, 0, *wmv("ple_wp", (PLE_WP_ROWS, LANES)), name="adamw_ple_wp")
    res_small = adamw_rows(small_parts, 0, *[_flat_pad([a[pre + n] for n in small_all], F32, SMALL_F32).reshape(
        small_f32_rows, LANES) for pre in ("", "m_", "v_")], name="adamw_small", tr=576)
    kinds = []
    for k in range(4):
        kd = {n: res[n][k].reshape(shapes[n]) for n in res}
        kd.update(_take(res_small[k].reshape(-1), small_all, shapes))
        kinds.append(kd)
    loss = lax.psum(loss_part[0, 0], ("x", "y", "c"))
    return (loss, grad_x, *[kinds[0][n] for n in WEIGHTS], *[kinds[1][n] for n in WEIGHTS],
            *[kinds[2][n] for n in WEIGHTS], *[kinds[3][n] for n in WEIGHTS])


BIG_ROWS = 5632
SMALL_F32 = 73728


def _flat_pad(arrs, dtype, total):
    flat = jnp.concatenate([z.astype(dtype).reshape(-1) for z in arrs])
    return jnp.pad(flat, (0, total - flat.shape[0]))


def _flat8_pad(arrs, dtype, total):
    flat = jnp.concatenate([z.astype(dtype).reshape(N_DEV, -1) for z in arrs], axis=1)
    return jnp.pad(flat, ((0, 0), (0, total - flat.shape[1])))


def _bits(z):
    return z if MM == F32 else _as_bf16_bits(z)


def _unbits(z):
    return z if MM == F32 else _from_bf16_bits(z)


def _take(flat, names, shapes):
    out, off = {}, 0
    for n in names:
        sz = int(np.prod(shapes[n]))
        out[n] = flat[..., off:off + sz].reshape(flat.shape[:-1] + tuple(shapes[n]))
        off += sz
    return out


def _step(a):
    x, p = a["x"], a["p"]
    bsz, s_len, d = x.shape
    t = bsz * s_len
    x2 = x.reshape(t, d)
    tgt = a["loss_target"].reshape(t, d)
    p2 = p.reshape(DEPTH, t, D_PLE)
    shapes = {n: a[n].shape for n in WEIGHTS}
    big_names = [n for n, _ in BIG]
    small_names = [n for n, _ in SMALL]
    n_small = sum(int(np.prod(shapes[n])) for n in small_names)
    bits_per = 1 if MM == F32 else 2
    small_rows = -(-(n_small * bits_per) // PACK_ALIGN) * (PACK_ALIGN // PACK_COLS)

    send = jnp.concatenate([
        _flat_pad([a[n] for n in big_names], MM, BIG_ROWS * PACK_COLS).reshape(BIG_ROWS, PACK_COLS),
        _bits(_flat_pad([a[n] for n in small_names], F32, small_rows * PACK_COLS // bits_per)).reshape(small_rows, PACK_COLS),
    ], axis=0)
    gathered = all_gather(send, name="gather_weights")
    wb = _take(gathered[:, :BIG_ROWS].reshape(N_DEV, -1), big_names, shapes)
    ws = _take(_unbits(gathered[:, BIG_ROWS:].reshape(N_DEV, -1)), small_names, shapes)
    w = {n: _join(wb[n], ax) for n, ax in BIG}
    w.update({n: _join(ws[n], ax) for n, ax in SMALL})
    ln_g, ln_b = w["ln_g"], w["ln_b"]
    c_in_main = w["c_w_in"][0][:, :4 * C_WIDTH]
    c_in_gate = jnp.pad(w["c_w_in"][0][:, 4 * C_WIDTH:], ((0, 0), (0, 128 - 2 * C_HEADS)))
    wa_d, wx_d = _dense_blocks(a["b_wa"][0]), _dense_blocks(a["b_wx"][0])
    lru_w = (w["b_conv_w"][0], a["b_conv_b"][0], wa_d, a["b_ba"][0], wx_d, a["b_bx"][0], a["b_lam"][0])
    gdc_w = (a["c_a_log"][0], a["c_dt_bias"][0], a["c_norm_g"][0])

    h = x2
    saved = []
    for i in range(DEPTH):
        s = {"x0": h}
        s["y1"], s["z1"], s["hg1"], s["hu1"] = ffn_fwd(h, w["ffn1_wg"][i], w["ffn1_wu"][i], w["ffn1_wd"][i], ln_g[i, 0], ln_b[i, 0],
                                   name=f"ffn1_fwd_{i}")
        if i == 0:
            s["proj"] = matmul(s["y1"], w["ab_w_in"][0], mode="nn", name="ab_in_fwd")
            ya = attn_fwd(s["proj"], a["a_sinks"][0], bsz, name="attn_fwd")
            yb = lru_fwd(s["proj"], *lru_w, bsz, name="lru_fwd")
            s["mix"] = jnp.concatenate([ya, yb], axis=1)
            w_out = w["ab_w_out"][0]
        else:
            s["proj"] = matmul(s["y1"], c_in_main, mode="nn", name="c_in_fwd")
            s["gates"] = matmul(s["y1"], c_in_gate, mode="nn", name="c_gate_fwd")
            s["qkv"] = gdc_pre_fwd(s["proj"], w["c_conv_w"][0], bsz, name="gdc_pre_fwd")
            s["mix"], s["o_pre"], s["vnew"], s["tmat"], s["states"] = gdc_fwd(
                s["qkv"], s["proj"], s["gates"], *gdc_w, bsz, name="gdc_fwd")
            w_out = w["c_w_out"][0]
        s["y2"], s["z2"] = mm_ln_fwd(s["mix"], w_out, s["y1"], ln_g[i, 1], ln_b[i, 1], name=f"mix_out_fwd_{i}")
        s["y3"], s["z3"], s["hg2"], s["hu2"] = ffn_fwd(s["y2"], w["ffn2_wg"][i], w["ffn2_wu"][i], w["ffn2_wd"][i], ln_g[i, 2], ln_b[i, 2],
                                   name=f"ffn2_fwd_{i}")
        h = ple_fwd(s["y3"], p2[i], w["ple_wg"][i], a["ple_bg"][i], w["ple_wp"][i], name=f"ple_fwd_{i}")
        saved.append(s)
    loss_part, dh = loss_fwd_bwd(h, tgt, name="loss")

    g = {n: [None] * shapes[n][0] for n in ("ffn1_wg", "ffn1_wu", "ffn1_wd", "ffn2_wg", "ffn2_wu", "ffn2_wd", "ln_g",
                                             "ln_b", "ple_wg", "ple_bg", "ple_wp")}
    wide = dict(tm=1024, tn=1408, tk=1024)
    tall = dict(tm=1408, tn=1024, tk=1024)
    for i in reversed(range(DEPTH)):
        s = saved[i]
        dy3, dt, de, dbg = ple_bwd(dh, s["y3"], p2[i], w["ple_wg"][i], a["ple_bg"][i], w["ple_wp"][i], name=f"ple_bwd_{i}")
        g["ple_wg"][i] = matmul(s["y3"], dt, mode="tn", name=f"ple_wg_grad_{i}")
        g["ple_wp"][i] = matmul(p2[i], de, mode="tn", name=f"ple_wp_grad_{i}")
        g["ple_bg"][i] = dbg[0]
        dz3, dg2, db2 = ln_bwd(dy3, s["z3"], ln_g[i, 2], name=f"ln2_bwd_{i}")
        dy2, act, dhg, dhu = ffn_bwd(dz3, s["hg2"], s["hu2"], w["ffn2_wg"][i], w["ffn2_wu"][i], w["ffn2_wd"][i], name=f"ffn2_bwd_{i}", tm=FFN_TM, tf=FFN_TF)
        g["ffn2_wg"][i] = matmul(s["y2"], dhg, mode="tn", name=f"ffn2_wg_grad_{i}", **wide)
        g["ffn2_wu"][i] = matmul(s["y2"], dhu, mode="tn", name=f"ffn2_wu_grad_{i}", **wide)
        g["ffn2_wd"][i] = matmul(act, dz3, mode="tn", scale=0.5, name=f"ffn2_wd_grad_{i}", **tall)
        dz2, dg1, db1 = ln_bwd(dy2, s["z2"], ln_g[i, 1], name=f"ln1_bwd_{i}")
        if i == 0:
            dmix = matmul(dz2, w["ab_w_out"][0], mode="nt", name="ab_out_bwd")
            g["ab_w_out"] = matmul(s["mix"], dz2, mode="tn", name="ab_out_grad")
            dq, dk, dv, dsk = attn_bwd(s["proj"], a["a_sinks"][0], dmix, bsz, name="attn_bwd")
            dbx, dbgate, dcw, dcb, dwa, dba, dwx, dbxb, dlam = lru_bwd(s["proj"], *lru_w, dmix, bsz, name="lru_bwd")
            dproj = jnp.concatenate([dq, dk, dv, dbx, dbgate], axis=1).astype(MM)
            dy1 = matmul(dproj, w["ab_w_in"][0], mode="nt", add=dz2, add_scale=DN_ALPHA, name="ab_in_bwd")
            g["ab_w_in"] = matmul(s["y1"], dproj, mode="tn", name="ab_in_grad")
        else:
            dmix = matmul(dz2, w["c_w_out"][0], mode="nt", name="c_out_bwd")
            g["c_w_out"] = matmul(s["mix"], dz2, mode="tn", name="c_out_grad")
            dqkv, dzc, dgates, dsm = gdc_bwd(s["qkv"], s["proj"], s["gates"], *gdc_w, s["o_pre"], s["vnew"], s["tmat"],
                                             s["states"], dmix, bsz, name="gdc_bwd")
            draw, dccw = gdc_pre_bwd(s["proj"], w["c_conv_w"][0], dqkv, bsz, name="gdc_pre_bwd")
            dproj = jnp.concatenate([draw, dzc], axis=1).astype(MM)
            dgb = dgates.astype(MM)
            dy1 = matmul(dproj, c_in_main, mode="nt", add=dz2, add_scale=DN_ALPHA, name="c_in_bwd")
            dy1 = matmul(dgb, c_in_gate, mode="nt", add=dy1, name="c_gate_bwd")
            g["c_w_in"] = jnp.concatenate([matmul(s["y1"], dproj, mode="tn", name="c_in_grad"),
                                           matmul(s["y1"], dgb, mode="tn", name="c_gate_grad")[:, :2 * C_HEADS]], axis=1)
        dz1, dg0, db0 = ln_bwd(dy1, s["z1"], ln_g[i, 0], name=f"ln0_bwd_{i}")
        dh, act, dhg, dhu = ffn_bwd(dz1, s["hg1"], s["hu1"], w["ffn1_wg"][i], w["ffn1_wu"][i], w["ffn1_wd"][i], name=f"ffn1_bwd_{i}", tm=FFN_TM, tf=FFN_TF)
        g["ffn1_wg"][i] = matmul(s["x0"], dhg, mode="tn", name=f"ffn1_wg_grad_{i}", **wide)
        g["ffn1_wu"][i] = matmul(s["x0"], dhu, mode="tn", name=f"ffn1_wu_grad_{i}", **wide)
        g["ffn1_wd"][i] = matmul(act, dz1, mode="tn", scale=0.5, name=f"ffn1_wd_grad_{i}", **tall)
        g["ln_g"][i] = jnp.concatenate([dg0, dg1, dg2], axis=0)
        g["ln_b"][i] = jnp.concatenate([db0, db1, db2], axis=0)
    grad_x = dh.reshape(bsz, s_len, d)
    full = {n: jnp.stack(v) if isinstance(v, list) else v[None] for n, v in g.items()}
    full["b_conv_w"] = dcw[None]
    full["c_conv_w"] = jnp.sum(dccw, axis=0)[None]
    dsm_sum = jnp.sum(dsm, axis=0)
    full.update(a_sinks=jnp.sum(dsk, axis=0)[:, :A_HEADS], b_conv_b=dcb, b_wa=_diag_blocks(dwa)[None], b_ba=dba,
                b_wx=_diag_blocks(dwx)[None], b_bx=dbxb, b_lam=dlam, c_a_log=dsm_sum[None, :C_HEADS, 0],
                c_dt_bias=dsm_sum[None, :C_HEADS, 1], c_norm_g=jnp.sum(dsm_sum[C_HEADS:], axis=0)[None])

    small_cols = SMALL_F32 * bits_per // PACK_COLS
    repl_flat = _flat_pad([full[n] for n in REPL], F32, SMALL_F32 - n_small)
    small8 = jnp.concatenate([_flat8_pad([_split(full[n], ax) for n, ax in SMALL], F32, n_small),
                              jnp.broadcast_to(repl_flat, (N_DEV,) + repl_flat.shape)], axis=1)
    parts = jnp.concatenate([
        _flat8_pad([_split(full[n], ax) for n, ax in BIG], MM, BIG_ROWS * PACK_COLS).reshape(N_DEV, BIG_ROWS, PACK_COLS),
        _bits(small8).reshape(N_DEV, small_cols, PACK_COLS)], axis=1)
    recv = all_to_all(parts, name="exchange_grads")

    def mine(prefix, names, dtype_total):
        return _flat_pad([a[prefix + n] for n in names], F32, dtype_total)

    outs = {}
    big_total = BIG_ROWS * PACK_COLS
    res_big = adamw_sum(recv, *[mine(pre, big_names, big_total).reshape(BIG_ROWS, PACK_COLS) for pre in ("", "m_", "v_")],
                        name="adamw_big")
    small_all = small_names + REPL
    cols_f32 = PACK_COLS // bits_per
    res_small = adamw_sum(_unbits(recv[:, BIG_ROWS:]).reshape(N_DEV, small_cols, cols_f32),
                          *[mine(pre, small_all, SMALL_F32).reshape(small_cols, cols_f32) for pre in ("", "m_", "v_")],
                          name="adamw_small")
    kinds = []
    for rb, rs in zip(res_big, res_small):
        k = _take(rb.reshape(-1), big_names, shapes)
        k.update(_take(rs.reshape(-1), small_all, shapes))
        kinds.append(k)
    loss = lax.psum(loss_part[0, 0], ("x", "y", "c"))
    return (loss, grad_x, *[kinds[0][n] for n in WEIGHTS], *[kinds[1][n] for n in WEIGHTS],
            *[kinds[2][n] for n in WEIGHTS], *[kinds[3][n] for n in WEIGHTS])
```

```python
import functools
import math

import numpy as np
import jax
import jax.numpy as jnp
from jax import lax
from jax.experimental import pallas as pl
from jax.experimental.pallas import tpu as pltpu

F32 = jnp.float32
MM = jnp.bfloat16
HI = lax.Precision.HIGHEST

D_MODEL = 1024
D_FF = 2816
D_PLE = 256
DEPTH = 2
CHUNK = 64
A_HEADS = 8
A_KV_HEADS = 2
A_GROUP = 4
A_HEAD_DIM = 64
A_WIDTH = 512
A_KV_WIDTH = 128
B_WIDTH = 512
B_BLOCK = 64
RG_C = 8.0
AB_PROJ = 1792
C_HEADS = 8
C_HEAD_DIM = 128
C_WIDTH = 1024
DN_ALPHA = (2.0 * DEPTH) ** 0.25
LN_EPS = 1e-5
NORM_EPS = 1e-6
NEG = -1e30
ADAM_LR = 0.001
ADAM_B1 = 0.9
ADAM_B2 = 0.999
ADAM_EPS = 1e-08
ADAM_WD = 0.01
ADAM_STEP = 10
N_DEV = 8
VMEM_LIMIT = 56 * 1024 * 1024

NN = ((1,), (0,))
NT = ((1,), (1,))
TN = ((0,), (0,))


def _pcall(body, **kw):
    return pl.pallas_call(body, **kw)


def _cp(*sem):
    return pltpu.CompilerParams(dimension_semantics=sem, vmem_limit_bytes=VMEM_LIMIT)


MESH_ID = pl.DeviceIdType.MESH
_FLIPS = [(0, 0, 1), (1, 0, 0), (0, 1, 0), (1, 1, 0), (1, 0, 1), (0, 1, 1), (1, 1, 1)]


def _me():
    return lax.axis_index("x"), lax.axis_index("y"), lax.axis_index("c")


def _flip(coord, d):
    return 1 - coord if d else coord


def _side_copies(kind, x_refs, o_refs, send_sems, recv_sems, local_sems, start):
    x, y, c = _me()
    mine = 4 * x + 2 * y + c
    for gi, (x_ref, o_ref) in enumerate(zip(x_refs, o_refs)):
        src_own = x_ref if kind == "gather" else x_ref.at[mine]
        own = pltpu.make_async_copy(src_own, o_ref.at[mine], local_sems.at[gi])
        own.start() if start else own.wait()
        for k, (dx, dy, dc) in enumerate(_FLIPS):
            px, py, pc = _flip(x, dx), _flip(y, dy), _flip(c, dc)
            src = x_ref if kind == "gather" else x_ref.at[4 * px + 2 * py + pc]
            cp = pltpu.make_async_remote_copy(
                src_ref=src, dst_ref=o_ref.at[mine], send_sem=send_sems.at[7 * gi + k], recv_sem=recv_sems.at[7 * gi + k],
                device_id=(px, py, pc), device_id_type=MESH_ID)
            cp.start() if start else cp.wait()


def _call(body, args, side, grid, **kw):
    if side is None:
        return _pcall(body, grid=grid, **kw)(*args)
    kind, arrs = side
    ns, n_in, n_out = len(arrs), len(args), len(kw["out_specs"])
    scratch = list(kw.get("scratch_shapes", []))
    n_scr = len(scratch)

    def edge(at_end):
        conds = [pl.program_id(ax) == (n - 1 if at_end else 0) for ax, n in enumerate(grid)]
        return functools.reduce(jnp.logical_and, conds)

    def wrapped(*refs):
        ins, sx = refs[:n_in], refs[n_in:n_in + ns]
        outs, so = refs[n_in + ns:n_in + ns + n_out], refs[n_in + ns + n_out:n_in + 2 * ns + n_out]
        rest = refs[n_in + 2 * ns + n_out:]
        scr, sems = rest[:n_scr], rest[n_scr:]

        @pl.when(edge(False))
        def _():
            _side_copies(kind, sx, so, *sems, start=True)

        body(*ins, *outs, *scr)

        @pl.when(edge(True))
        def _():
            _side_copies(kind, sx, so, *sems, start=False)

    hbm = pl.BlockSpec(memory_space=pl.ANY)
    side_shapes = [jax.ShapeDtypeStruct(((N_DEV,) if kind == "gather" else ()) + z.shape, z.dtype) for z in arrs]
    kw = dict(kw)
    kw["in_specs"] = list(kw["in_specs"]) + [hbm] * ns
    kw["out_specs"] = list(kw["out_specs"]) + [hbm] * ns
    kw["out_shape"] = list(kw["out_shape"]) + side_shapes
    kw["scratch_shapes"] = scratch + [pltpu.SemaphoreType.DMA((7 * ns,)), pltpu.SemaphoreType.DMA((7 * ns,)),
                                      pltpu.SemaphoreType.DMA((ns,))]
    kw["compiler_params"] = _cp(*["arbitrary"] * len(grid))
    res = _pcall(wrapped, grid=grid, **kw)(*args, *arrs)
    return list(res[:n_out]), list(res[n_out:])


def _dot(a, b, dims=NN, precision=None):
    return lax.dot_general(a, b, (dims, ((), ())), preferred_element_type=F32, precision=precision)


def _mdot(a, b, dims=NN):
    return _dot(a.astype(MM), b.astype(MM), dims)


def _tile(n, pref):
    if n <= pref:
        return n
    for c in range(pref - pref % 128, 0, -128):
        if n % c == 0:
            return c
    return n


def _sigmoid(x):
    return 1.0 / (1.0 + jnp.exp(-x))


def _softplus(x):
    return jnp.maximum(x, 0.0) + jnp.log(1.0 + jnp.exp(-jnp.abs(x)))


def _ln_stats(z):
    mu = jnp.mean(z, axis=-1, keepdims=True)
    zc = z - mu
    var = jnp.mean(zc * zc, axis=-1, keepdims=True)
    return zc, lax.rsqrt(var + LN_EPS)


def matmul(a, b, *, mode, name, tm=512, tn=512, tk=512, out_dtype=F32, scale=None, add=None, add_scale=1.0, side=None,
           split_n=None, ln=None):
    if mode == "nn":
        (m, kk), (_, n) = a.shape, b.shape
        dims = NN
    elif mode == "nt":
        (m, kk), (n, _) = a.shape, b.shape
        dims = NT
    else:
        (kk, m), (_, n) = a.shape, b.shape
        dims = TN
    tm, tn, tk = _tile(m, tm), _tile(n, tn), _tile(kk, tk)
    if mode == "nn":
        a_spec = pl.BlockSpec((tm, tk), lambda i, j, k: (i, k))
        b_spec = pl.BlockSpec((tk, tn), lambda i, j, k: (k, j))
    elif mode == "nt":
        a_spec = pl.BlockSpec((tm, tk), lambda i, j, k: (i, k))
        b_spec = pl.BlockSpec((tn, tk), lambda i, j, k: (j, k))
    else:
        a_spec = pl.BlockSpec((tk, tm), lambda i, j, k: (k, i))
        b_spec = pl.BlockSpec((tk, tn), lambda i, j, k: (k, j))
    nk = kk // tk
    o_spec = pl.BlockSpec((tm, tn), lambda i, j, k: (i, j))
    has_add = add is not None
    n_in = 2 + has_add + (2 if ln else 0)
    assert not (ln and split_n) and (not ln or tn == n) and (not split_n or tn % split_n == 0)

    def body(*refs):
        a_ref, b_ref = refs[:2]
        o_ref = refs[n_in]
        acc_ref = refs[-1]
        i, j, k = pl.program_id(0), pl.program_id(1), pl.program_id(2)

        @pl.when(k == 0)
        def _():
            acc_ref[...] = jnp.zeros_like(acc_ref)

        acc_ref[...] += _mdot(a_ref[...], b_ref[...], dims)
        if ln:
            z_ref, g_ref = refs[n_in - 2:n_in]
            dg_ref, db_ref = refs[n_in + 1:n_in + 3]

            @pl.when((i == 0) & (k == 0))
            def _():
                dg_ref[...] = jnp.zeros_like(dg_ref)
                db_ref[...] = jnp.zeros_like(db_ref)

        @pl.when(k == nk - 1)
        def _():
            r = acc_ref[...]
            if scale is not None:
                r = r * scale
            if has_add:
                r = r + add_scale * refs[2][...].astype(F32)
            if ln:
                r, dg, db = _ln_bwd_tile(r, z_ref[...], g_ref[...])
                dg_ref[...] += dg
                db_ref[...] += db
            if split_n:
                for q in range(tn // split_n):
                    o_ref[q] = r[:, q * split_n:(q + 1) * split_n].astype(out_dtype)
            else:
                o_ref[...] = r.astype(out_dtype)

    vec = pl.BlockSpec((1, n), lambda i, j, k: (0, 0))
    ins = [a, b] + ([add] if has_add else []) + ([ln[0], ln[1].reshape(1, n)] if ln else [])
    in_specs = [a_spec, b_spec] + ([o_spec] if has_add else []) + ([o_spec, vec] if ln else [])
    out_specs, out_shape = [o_spec], [jax.ShapeDtypeStruct((m, n), out_dtype)]
    if split_n:
        out_specs = [pl.BlockSpec((tn // split_n, tm, split_n), lambda i, j, k: (j, i, 0))]
        out_shape = [jax.ShapeDtypeStruct((n // split_n, m, split_n), out_dtype)]
    if ln:
        out_specs += [vec, vec]
        out_shape += [jax.ShapeDtypeStruct((1, n), F32)] * 2
    res = _call(
        body, ins, side, (m // tm, n // tn, nk), name=name, in_specs=in_specs, out_specs=out_specs, out_shape=out_shape,
        scratch_shapes=[pltpu.VMEM((tm, tn), F32)],
        compiler_params=_cp("arbitrary" if ln else "parallel", "parallel", "arbitrary"),
    )
    outs, extra = (res, None) if side is None else res
    outs = outs[0] if len(outs) == 1 else tuple(outs)
    return outs if side is None else (outs, extra)


def ffn_fwd(x, wg, wu, wd, g, b, *, name, tm=512, tf=256, side=None):
    t, d = x.shape
    f = wg.shape[1]
    tm = min(tm, t)
    nj = f // tf

    def body(x_ref, wg_ref, wu_ref, wd_ref, g_ref, b_ref, y_ref, z_ref, hg_ref, hu_ref, xb_ref, acc_ref):
        j = pl.program_id(1)

        @pl.when(j == 0)
        def _():
            xb_ref[...] = x_ref[...].astype(MM)
            acc_ref[...] = jnp.zeros_like(acc_ref)

        xb = xb_ref[...]
        hg = _dot(xb, wg_ref[...])
        hu = _dot(xb, wu_ref[...])
        hg_ref[...] = hg.astype(MM)
        hu_ref[...] = hu.astype(MM)
        act = (hg * _sigmoid(hg) * hu).astype(MM)
        acc_ref[...] += _dot(act, wd_ref[...])

        @pl.when(j == nj - 1)
        def _():
            z = DN_ALPHA * x_ref[...] + 0.5 * acc_ref[...]
            z_ref[...] = z
            zc, rstd = _ln_stats(z)
            y_ref[...] = zc * rstd * g_ref[...] + b_ref[...]

    row = pl.BlockSpec((tm, d), lambda i, j: (i, 0))
    hid = pl.BlockSpec((tm, tf), lambda i, j: (i, j))
    vec = pl.BlockSpec((1, d), lambda i, j: (0, 0))
    return _call(
        body, (x, wg, wu, wd, g.reshape(1, d), b.reshape(1, d)), side, (t // tm, nj), name=name,
        in_specs=[row, pl.BlockSpec((d, tf), lambda i, j: (0, j)), pl.BlockSpec((d, tf), lambda i, j: (0, j)),
                  pl.BlockSpec((tf, d), lambda i, j: (j, 0)), vec, vec],
        out_specs=[row, row, hid, hid],
        out_shape=[jax.ShapeDtypeStruct((t, d), F32), jax.ShapeDtypeStruct((t, d), F32),
                   jax.ShapeDtypeStruct((t, f), MM), jax.ShapeDtypeStruct((t, f), MM)],
        scratch_shapes=[pltpu.VMEM((tm, d), MM), pltpu.VMEM((tm, d), F32)],
        compiler_params=_cp("parallel", "arbitrary"),
    )


def _ln_bwd_tile(dy, z, g):
    zc, rstd = _ln_stats(z)
    xh = zc * rstd
    dxh = dy * g
    m1 = jnp.mean(dxh, axis=-1, keepdims=True)
    m2 = jnp.mean(dxh * xh, axis=-1, keepdims=True)
    return rstd * (dxh - m1 - xh * m2), jnp.sum(dy * xh, axis=0, keepdims=True), jnp.sum(dy, axis=0, keepdims=True)


def ffn_bwd(dz, hg, hu, wg, wu, wd, *, name, tm=512, tf=256, side=None, ln=None):
    t, d = dz.shape
    f = wg.shape[1]
    tm = min(tm, t)
    nj = f // tf
    n_in = 6 + (2 if ln else 0)

    def body(*refs):
        dz_ref, hg_ref, hu_ref, wg_ref, wu_ref, wd_ref = refs[:6]
        dx_ref, act_ref, dhg_ref, dhu_ref = refs[n_in:n_in + 4]
        dfb_ref, acc_ref = refs[-2:]
        i, j = pl.program_id(0), pl.program_id(1)

        @pl.when(j == 0)
        def _():
            dfb_ref[...] = (0.5 * dz_ref[...]).astype(MM)
            acc_ref[...] = jnp.zeros_like(acc_ref)

        hg = hg_ref[...].astype(F32)
        hu = hu_ref[...].astype(F32)
        s = _sigmoid(hg)
        dact = _dot(dfb_ref[...], wd_ref[...], NT)
        sg = hg * s
        act_ref[...] = (sg * hu).astype(MM)
        dhu = (dact * sg).astype(MM)
        dhg = (dact * hu * (s + sg * (1.0 - s))).astype(MM)
        dhu_ref[...] = dhu
        dhg_ref[...] = dhg
        acc_ref[...] += _dot(dhg, wg_ref[...], NT) + _dot(dhu, wu_ref[...], NT)

        if ln:
            z_ref, g_ref = refs[6:8]
            dg_ref, db_ref = refs[n_in + 4:n_in + 6]

            @pl.when((i == 0) & (j == 0))
            def _():
                dg_ref[...] = jnp.zeros_like(dg_ref)
                db_ref[...] = jnp.zeros_like(db_ref)

        @pl.when(j == nj - 1)
        def _():
            dx = DN_ALPHA * dz_ref[...] + acc_ref[...]
            if ln:
                dx, dg, db = _ln_bwd_tile(dx, z_ref[...], g_ref[...])
                dg_ref[...] += dg
                db_ref[...] += db
            dx_ref[...] = dx

    row = pl.BlockSpec((tm, d), lambda i, j: (i, 0))
    hid = pl.BlockSpec((tm, tf), lambda i, j: (i, j))
    vec = pl.BlockSpec((1, d), lambda i, j: (0, 0))
    vshape = jax.ShapeDtypeStruct((1, d), F32)
    return _call(
        body, (dz, hg, hu, wg, wu, wd) + ((ln[0], ln[1].reshape(1, d)) if ln else ()), side, (t // tm, nj), name=name,
        in_specs=[row, hid, hid, pl.BlockSpec((d, tf), lambda i, j: (0, j)), pl.BlockSpec((d, tf), lambda i, j: (0, j)),
                  pl.BlockSpec((tf, d), lambda i, j: (j, 0))] + ([row, vec] if ln else []),
        out_specs=[row, hid, hid, hid] + ([vec, vec] if ln else []),
        out_shape=[jax.ShapeDtypeStruct((t, d), F32)] + [jax.ShapeDtypeStruct((t, f), MM)] * 3 + ([vshape, vshape] if ln else []),
        scratch_shapes=[pltpu.VMEM((tm, d), MM), pltpu.VMEM((tm, d), F32)],
        compiler_params=_cp("arbitrary" if ln else "parallel", "arbitrary"),
    )


def ln_bwd(dy, z, g, *, name, tm=512):
    t, d = z.shape
    tm = min(tm, t)

    def body(dy_ref, z_ref, g_ref, dz_ref, dg_ref, db_ref):
        i = pl.program_id(0)

        @pl.when(i == 0)
        def _():
            dg_ref[...] = jnp.zeros_like(dg_ref)
            db_ref[...] = jnp.zeros_like(db_ref)

        dy = dy_ref[...]
        zc, rstd = _ln_stats(z_ref[...])
        xh = zc * rstd
        dg_ref[...] += jnp.sum(dy * xh, axis=0, keepdims=True)
        db_ref[...] += jnp.sum(dy, axis=0, keepdims=True)
        dxh = dy * g_ref[...]
        m1 = jnp.mean(dxh, axis=-1, keepdims=True)
        m2 = jnp.mean(dxh * xh, axis=-1, keepdims=True)
        dz_ref[...] = rstd * (dxh - m1 - xh * m2)

    row = pl.BlockSpec((tm, d), lambda i: (i, 0))
    vec = pl.BlockSpec((1, d), lambda i: (0, 0))
    return _pcall(
        body, name=name, grid=(t // tm,), in_specs=[row, row, vec], out_specs=[row, vec, vec],
        out_shape=[jax.ShapeDtypeStruct((t, d), F32), jax.ShapeDtypeStruct((1, d), F32), jax.ShapeDtypeStruct((1, d), F32)],
        compiler_params=_cp("arbitrary"),
    )(dy, z, g.reshape(1, d))


def mm_ln_fwd(a, w, res, g, b, *, name, tm=512):
    t, kk = a.shape
    d = w.shape[1]
    tm = min(tm, t)

    def body(a_ref, w_ref, res_ref, g_ref, b_ref, y_ref, z_ref):
        z = DN_ALPHA * res_ref[...] + _mdot(a_ref[...], w_ref[...])
        z_ref[...] = z
        zc, rstd = _ln_stats(z)
        y_ref[...] = zc * rstd * g_ref[...] + b_ref[...]

    row = pl.BlockSpec((tm, d), lambda i: (i, 0))
    vec = pl.BlockSpec((1, d), lambda i: (0, 0))
    return _pcall(
        body, name=name, grid=(t // tm,),
        in_specs=[pl.BlockSpec((tm, kk), lambda i: (i, 0)), pl.BlockSpec((kk, d), lambda i: (0, 0)), row, vec, vec],
        out_specs=[row, row],
        out_shape=[jax.ShapeDtypeStruct((t, d), F32), jax.ShapeDtypeStruct((t, d), F32)],
        compiler_params=_cp("parallel"),
    )(a, w, res, g.reshape(1, d), b.reshape(1, d))


def ple_fwd(y, p, wg, bg, wp, *, name, tm=512):
    t, d = y.shape
    dp = p.shape[1]
    tm = min(tm, t)

    def body(y_ref, p_ref, wg_ref, bg_ref, wp_ref, o_ref):
        yv = y_ref[...]
        gate = _sigmoid(_mdot(yv, wg_ref[...]) + bg_ref[...])
        o_ref[...] = yv + gate * _mdot(p_ref[...], wp_ref[...])

    row = pl.BlockSpec((tm, d), lambda i: (i, 0))
    return _pcall(
        body, name=name, grid=(t // tm,),
        in_specs=[row, pl.BlockSpec((tm, dp), lambda i: (i, 0)), pl.BlockSpec((d, d), lambda i: (0, 0)),
                  pl.BlockSpec((1, d), lambda i: (0, 0)), pl.BlockSpec((dp, d), lambda i: (0, 0))],
        out_specs=row, out_shape=jax.ShapeDtypeStruct((t, d), F32), compiler_params=_cp("parallel"),
    )(y, p, wg, bg.reshape(1, d), wp)


def ple_bwd(do, y, p, wg, bg, wp, z, g, *, name, tm=512):
    t, d = y.shape
    dp = p.shape[1]
    tm = min(tm, t)

    def body(do_ref, y_ref, p_ref, wg_ref, bg_ref, wp_ref, z_ref, g_ref, dz_ref, dt_ref, de_ref, dbg_ref, dg_ref, db_ref):
        i = pl.program_id(0)

        @pl.when(i == 0)
        def _():
            for ref in (dbg_ref, dg_ref, db_ref):
                ref[...] = jnp.zeros_like(ref)

        dov = do_ref[...]
        gate = _sigmoid(_mdot(y_ref[...], wg_ref[...]) + bg_ref[...])
        emb = _mdot(p_ref[...], wp_ref[...])
        dt = dov * emb * gate * (1.0 - gate)
        dbg_ref[...] += jnp.sum(dt, axis=0, keepdims=True)
        dtb = dt.astype(MM)
        dt_ref[...] = dtb
        de_ref[...] = (dov * gate).astype(MM)
        dz, dg, db = _ln_bwd_tile(dov + _dot(dtb, wg_ref[...], NT), z_ref[...], g_ref[...])
        dz_ref[...] = dz
        dg_ref[...] += dg
        db_ref[...] += db

    row = pl.BlockSpec((tm, d), lambda i: (i, 0))
    vec = pl.BlockSpec((1, d), lambda i: (0, 0))
    vshape = jax.ShapeDtypeStruct((1, d), F32)
    return _pcall(
        body, name=name, grid=(t // tm,),
        in_specs=[row, row, pl.BlockSpec((tm, dp), lambda i: (i, 0)), pl.BlockSpec((d, d), lambda i: (0, 0)),
                  vec, pl.BlockSpec((dp, d), lambda i: (0, 0)), row, vec],
        out_specs=[row, row, row, vec, vec, vec],
        out_shape=[jax.ShapeDtypeStruct((t, d), F32), jax.ShapeDtypeStruct((t, d), MM),
                   jax.ShapeDtypeStruct((t, d), MM), vshape, vshape, vshape],
        compiler_params=_cp("arbitrary"),
    )(do, y, p, wg, bg.reshape(1, d), wp, z, g.reshape(1, d))


def loss_fwd_bwd(y, tgt, *, name, tm=512):
    t, d = y.shape
    tm = min(tm, t)

    def body(y_ref, t_ref, l_ref, dy_ref):
        i = pl.program_id(0)

        @pl.when(i == 0)
        def _():
            l_ref[...] = jnp.zeros_like(l_ref)

        err = y_ref[...] - t_ref[...]
        dy_ref[...] = err * (1.0 / d)
        l_ref[...] += (0.5 / d) * jnp.sum(jnp.sum(err * err, axis=1, keepdims=True), axis=0, keepdims=True)

    row = pl.BlockSpec((tm, d), lambda i: (i, 0))
    return _pcall(
        body, name=name, grid=(t // tm,), in_specs=[row, row],
        out_specs=[pl.BlockSpec((1, 128), lambda i: (0, 0)), row],
        out_shape=[jax.ShapeDtypeStruct((1, 128), F32), jax.ShapeDtypeStruct((t, d), F32)],
        compiler_params=_cp("arbitrary"),
    )(y, tgt)


def _shift_dn(x, s, row):
    return x if s == 0 else jnp.where(row >= s, pltpu.roll(x, s, 0), 0.0)


def _shift_up(x, s, row):
    n = x.shape[0]
    return x if s == 0 else jnp.where(row < n - s, pltpu.roll(x, n - s, 0), 0.0)


def _conv_fwd(x, w, row):
    kk = w.shape[0]
    y = w[kk - 1:kk, :] * x
    for j in range(kk - 1):
        y = y + w[j:j + 1, :] * _shift_dn(x, kk - 1 - j, row)
    return y


def _conv_bwd(x, w, dy, row):
    kk = w.shape[0]
    dx = w[kk - 1:kk, :] * dy
    dws = []
    for j in range(kk - 1):
        dx = dx + w[j:j + 1, :] * _shift_up(dy, kk - 1 - j, row)
        dws.append(jnp.sum(dy * _shift_dn(x, kk - 1 - j, row), axis=0, keepdims=True))
    dws.append(jnp.sum(dy * x, axis=0, keepdims=True))
    return dx, jnp.concatenate(dws, axis=0)


def _gelu(x):
    c = math.sqrt(2.0 / math.pi)
    th = jnp.tanh(c * (x + 0.044715 * x * x * x))
    return 0.5 * x * (1.0 + th), th


def _gelu_grad(x, th):
    c = math.sqrt(2.0 / math.pi)
    return 0.5 * (1.0 + th) + 0.5 * x * (1.0 - th * th) * c * (1.0 + 3.0 * 0.044715 * x * x)


def _neg_expm1(y):
    ser = -(y * (1.0 + y * (0.5 + y * (1.0 / 6.0 + y * (1.0 / 24.0 + y * (1.0 / 120.0))))))
    return jnp.where(y > -0.05, ser, 1.0 - jnp.exp(y))


def _attn_head(qh, kk, vv, bias, valid, sink):
    s = _mdot(qh, kk, NT) * (A_HEAD_DIM ** -0.5) - bias
    s = jnp.where(valid, s, NEG)
    m = jnp.maximum(jnp.max(s, axis=-1, keepdims=True), sink)
    pr = jnp.exp(s - m)
    den = jnp.sum(pr, axis=-1, keepdims=True) + jnp.exp(sink - m)
    return pr / den, jnp.exp(sink - m) / den


def _attn_valid(n):
    ji = lax.broadcasted_iota(jnp.int32, (1, 3 * CHUNK), 1)
    return (n * CHUNK + ji - 2 * CHUNK) >= 0


def _attn_group_consts(kh, sk_ref):
    rows = A_GROUP * CHUNK
    ri = lax.broadcasted_iota(jnp.int32, (rows, 3 * CHUNK), 0)
    ji = lax.broadcasted_iota(jnp.int32, (rows, 3 * CHUNK), 1)
    dist = jnp.abs((ri & (CHUNK - 1)) + 2 * CHUNK - ji).astype(F32)
    rcol = lax.broadcasted_iota(jnp.int32, (rows, 1), 0)
    slope = jnp.zeros((rows, 1), F32)
    sink = jnp.zeros((rows, 1), F32)
    for gi in range(A_GROUP):
        h = kh * A_GROUP + gi
        inblk = (rcol >= gi * CHUNK) & (rcol < (gi + 1) * CHUNK)
        slope = jnp.where(inblk, 2.0 ** -(h + 1), slope)
        sink = jnp.where(inblk, sk_ref[h], sink)
    return slope * dist, sink


def _stack_heads(x, kh):
    return jnp.concatenate([x[:, (kh * A_GROUP + gi) * 64:(kh * A_GROUP + gi + 1) * 64] for gi in range(A_GROUP)], axis=0)


def _attn_masks(n):
    ci = lax.broadcasted_iota(jnp.int32, (CHUNK, 3 * CHUNK), 0)
    ji = lax.broadcasted_iota(jnp.int32, (CHUNK, 3 * CHUNK), 1)
    dist = jnp.abs(ci + 2 * CHUNK - ji).astype(F32)
    valid = (n * CHUNK + ji - 2 * CHUNK) >= 0
    return dist, valid


def attn_fwd(proj, sinks, bsz, *, name, side=None):
    t = proj.shape[0]
    s_len = t // bsz
    nc = s_len // CHUNK
    pad = 2 * CHUNK

    def body(q_ref, k_ref, v_ref, sk_ref, o_ref, kp_ref, vp_ref):
        kp_ref[0:pad, :] = jnp.zeros((pad, A_KV_WIDTH), F32)
        vp_ref[0:pad, :] = jnp.zeros((pad, A_KV_WIDTH), F32)
        kp_ref[pad:, :] = k_ref[...].astype(F32)
        vp_ref[pad:, :] = v_ref[...].astype(F32)

        consts = [_attn_group_consts(kh, sk_ref) for kh in range(A_KV_HEADS)]

        def chunk(n, carry):
            st = pl.multiple_of(n * CHUNK, CHUNK)
            q = q_ref[pl.ds(st, CHUNK), :].astype(F32)
            kb = kp_ref[pl.ds(st, 3 * CHUNK), :]
            vb = vp_ref[pl.ds(st, 3 * CHUNK), :]
            valid = _attn_valid(n)
            outs = []
            for kh in range(A_KV_HEADS):
                bias, sink = consts[kh]
                pn, _ = _attn_head(_stack_heads(q, kh), kb[:, kh * 64:(kh + 1) * 64], None, bias, valid, sink)
                o = _mdot(pn, vb[:, kh * 64:(kh + 1) * 64])
                outs += [o[gi * CHUNK:(gi + 1) * CHUNK] for gi in range(A_GROUP)]
            o_ref[pl.ds(st, CHUNK), :] = jnp.concatenate(outs, axis=-1)
            return carry

        lax.fori_loop(0, nc, chunk, 0)

    res = _call(
        body, (proj, proj, proj, sinks), side, (bsz,), name=name,
        in_specs=[pl.BlockSpec((s_len, A_WIDTH), lambda b: (b, 0)), pl.BlockSpec((s_len, 128), lambda b: (b, 4)),
                  pl.BlockSpec((s_len, 128), lambda b: (b, 5)), pl.BlockSpec(memory_space=pltpu.SMEM)],
        out_specs=[pl.BlockSpec((s_len, A_WIDTH), lambda b: (b, 0))],
        out_shape=[jax.ShapeDtypeStruct((t, A_WIDTH), F32)],
        scratch_shapes=[pltpu.VMEM((s_len + pad, A_KV_WIDTH), F32), pltpu.VMEM((s_len + pad, A_KV_WIDTH), F32)],
        compiler_params=_cp("parallel"),
    )
    return res[0] if side is None else (res[0][0], res[1])


def attn_bwd(proj, sinks, dcat, bsz, *, name, side=None):
    t = proj.shape[0]
    s_len = t // bsz
    nc = s_len // CHUNK
    pad = 2 * CHUNK

    def body(q_ref, k_ref, v_ref, do_ref, sk_ref, dq_ref, dk_ref, dv_ref, dsk_ref, kp_ref, vp_ref, dkp_ref, dvp_ref):
        kp_ref[0:pad, :] = jnp.zeros((pad, A_KV_WIDTH), F32)
        vp_ref[0:pad, :] = jnp.zeros((pad, A_KV_WIDTH), F32)
        kp_ref[pad:, :] = k_ref[...].astype(F32)
        vp_ref[pad:, :] = v_ref[...].astype(F32)
        dkp_ref[...] = jnp.zeros_like(dkp_ref)
        dvp_ref[...] = jnp.zeros_like(dvp_ref)
        lane = lax.broadcasted_iota(jnp.int32, (1, 128), 1)

        consts = [_attn_group_consts(kh, sk_ref) for kh in range(A_KV_HEADS)]

        def chunk(n, dsk):
            st = pl.multiple_of(n * CHUNK, CHUNK)
            q = q_ref[pl.ds(st, CHUNK), :].astype(F32)
            do = do_ref[pl.ds(st, CHUNK), :]
            kb = kp_ref[pl.ds(st, 3 * CHUNK), :]
            vb = vp_ref[pl.ds(st, 3 * CHUNK), :]
            valid = _attn_valid(n)
            dqs, dks, dvs = [], [], []
            for kh in range(A_KV_HEADS):
                kk = kb[:, kh * 64:(kh + 1) * 64]
                vv = vb[:, kh * 64:(kh + 1) * 64]
                bias, sink = consts[kh]
                qs = _stack_heads(q, kh)
                dos = _stack_heads(do, kh)
                pn, psink = _attn_head(qs, kk, None, bias, valid, sink)
                dp = _mdot(dos, vv, NT)
                rowdot = jnp.sum(pn * dp, axis=-1, keepdims=True)
                ds = pn * (dp - rowdot)
                sink_part = psink * rowdot
                for gi in range(A_GROUP):
                    part = jnp.sum(sink_part[gi * CHUNK:(gi + 1) * CHUNK], axis=0, keepdims=True)
                    dsk = dsk + jnp.where(lane == kh * A_GROUP + gi, -part, 0.0)
                dq = _mdot(ds, kk) * (A_HEAD_DIM ** -0.5)
                dqs += [dq[gi * CHUNK:(gi + 1) * CHUNK] for gi in range(A_GROUP)]
                dks.append(_mdot(ds, qs, TN) * (A_HEAD_DIM ** -0.5))
                dvs.append(_mdot(pn, dos, TN))
            dq_ref[pl.ds(st, CHUNK), :] = jnp.concatenate(dqs, axis=-1)
            dkp_ref[pl.ds(st, 3 * CHUNK), :] += jnp.concatenate(dks, axis=-1)
            dvp_ref[pl.ds(st, 3 * CHUNK), :] += jnp.concatenate(dvs, axis=-1)
            return dsk

        dsk = lax.fori_loop(0, nc, chunk, jnp.zeros((1, 128), F32))
        dsk_ref[0] = dsk
        dk_ref[...] = dkp_ref[pad:, :]
        dv_ref[...] = dvp_ref[pad:, :]

    kv = jax.ShapeDtypeStruct((t, A_KV_WIDTH), F32)
    return _call(
        body, (proj, proj, proj, dcat, sinks), side, (bsz,), name=name,
        in_specs=[pl.BlockSpec((s_len, A_WIDTH), lambda b: (b, 0)), pl.BlockSpec((s_len, 128), lambda b: (b, 4)),
                  pl.BlockSpec((s_len, 128), lambda b: (b, 5)), pl.BlockSpec((s_len, A_WIDTH), lambda b: (b, 0)),
                  pl.BlockSpec(memory_space=pltpu.SMEM)],
        out_specs=[pl.BlockSpec((s_len, A_WIDTH), lambda b: (b, 0)), pl.BlockSpec((s_len, 128), lambda b: (b, 0)),
                   pl.BlockSpec((s_len, 128), lambda b: (b, 0)), pl.BlockSpec((1, 1, 128), lambda b: (b, 0, 0))],
        out_shape=[jax.ShapeDtypeStruct((t, A_WIDTH), F32), kv, kv, jax.ShapeDtypeStruct((bsz, 1, 128), F32)],
        scratch_shapes=[pltpu.VMEM((s_len + pad, A_KV_WIDTH), F32)] * 4,
        compiler_params=_cp("parallel"),
    )


def _lru_gates(x, cw, cb, wa, ba, wx, bx, lam, row):
    xc = _conv_fwd(x, cw, row) + cb
    r = _sigmoid(_mdot(xc, wa) + ba)
    i = _sigmoid(_mdot(xc, wx) + bx)
    sp = _softplus(-lam)
    log_a = -RG_C * r * sp
    a = jnp.exp(log_a)
    mult = jnp.sqrt(_neg_expm1(2.0 * log_a))
    return xc, r, i, sp, a, mult


def _lru_scan(a, u, row):
    n = a.shape[0]
    d = 1
    while d < n:
        a_sh = jnp.where(row >= d, pltpu.roll(a, d, 0), 1.0)
        u_sh = jnp.where(row >= d, pltpu.roll(u, d, 0), 0.0)
        u = a * u_sh + u
        a = a * a_sh
        d *= 2
    return u


def _lru_scan_rev(a, u, row):
    n = a.shape[0]
    d = 1
    while d < n:
        a_sh = jnp.where(row < n - d, pltpu.roll(a, n - d, 0), 1.0)
        u_sh = jnp.where(row < n - d, pltpu.roll(u, n - d, 0), 0.0)
        u = a * u_sh + u
        a = a * a_sh
        d *= 2
    return u


LRU_BLOCK = 128


def _lru_scan_refs(a_ref, u_ref, h_ref, reverse=False):
    nb = a_ref.shape[0] // LRU_BLOCK
    row = lax.broadcasted_iota(jnp.int32, (LRU_BLOCK, 128), 0)

    def block(i, carry):
        bi = nb - 1 - i if reverse else i
        rs = pl.ds(pl.multiple_of(bi * LRU_BLOCK, LRU_BLOCK), LRU_BLOCK)
        a, u = a_ref[rs, :], u_ref[rs, :]
        d = 1
        while d < LRU_BLOCK:
            keep = row < LRU_BLOCK - d if reverse else row >= d
            sh = LRU_BLOCK - d if reverse else d
            a_sh = jnp.where(keep, pltpu.roll(a, sh, 0), 1.0)
            u_sh = jnp.where(keep, pltpu.roll(u, sh, 0), 0.0)
            u = a * u_sh + u
            a = a * a_sh
            d *= 2
        h = u + a * carry
        h_ref[rs, :] = h
        return h[0:1, :] if reverse else h[LRU_BLOCK - 1:LRU_BLOCK, :]

    lax.fori_loop(0, nb, block, jnp.zeros((1, 128), F32))


def _lru_specs(s_len, order):
    def at(f):
        return lambda *g: f(*order(*g))
    return [pl.BlockSpec((s_len, 128), at(lambda b, cb: (b, 6 + cb))), pl.BlockSpec((s_len, 128), at(lambda b, cb: (b, 10 + cb))),
            pl.BlockSpec((4, 128), at(lambda b, cb: (0, cb))), pl.BlockSpec((1, 128), at(lambda b, cb: (0, cb))),
            pl.BlockSpec((1, 128, 128), at(lambda b, cb: (cb, 0, 0))), pl.BlockSpec((1, 128), at(lambda b, cb: (0, cb))),
            pl.BlockSpec((1, 128, 128), at(lambda b, cb: (cb, 0, 0))), pl.BlockSpec((1, 128), at(lambda b, cb: (0, cb))),
            pl.BlockSpec((1, 128), at(lambda b, cb: (0, cb)))]


def lru_fwd(proj, cw, cb, wa, ba, wx, bxb, lam, bsz, *, name):
    t = proj.shape[0]
    s_len = t // bsz

    def body(x_ref, g_ref, cw_ref, cb_ref, wa_ref, ba_ref, wx_ref, bx_ref, lam_ref, y_ref, a_s, u_s):
        row = lax.broadcasted_iota(jnp.int32, (s_len, 128), 0)
        xc, r, i, sp, a, mult = _lru_gates(x_ref[...].astype(F32), cw_ref[...], cb_ref[...], wa_ref[0], ba_ref[...],
                                           wx_ref[0], bx_ref[...], lam_ref[...], row)
        a_s[...] = a
        u_s[...] = mult * (i * xc)
        _lru_scan_refs(a_s, u_s, y_ref)
        y_ref[...] = y_ref[...] * _gelu(g_ref[...].astype(F32))[0]

    return _pcall(
        body, name=name, grid=(bsz, 4), in_specs=_lru_specs(s_len, lambda b, cb: (b, cb)),
        out_specs=pl.BlockSpec((s_len, 128), lambda b, cb: (b, cb)),
        out_shape=jax.ShapeDtypeStruct((t, B_WIDTH), F32), scratch_shapes=[pltpu.VMEM((s_len, 128), F32)] * 2,
        compiler_params=_cp("parallel", "parallel"),
    )(proj, proj, cw, cb.reshape(1, -1), wa, ba.reshape(1, -1), wx, bxb.reshape(1, -1), lam.reshape(1, -1))


def lru_bwd(proj, cw, cb, wa, ba, wx, bxb, lam, dcat, bsz, *, name, side=None):
    t = proj.shape[0]
    s_len = t // bsz

    def body(x_ref, g_ref, cw_ref, cb_ref, wa_ref, ba_ref, wx_ref, bx_ref, lam_ref, dy_ref,
             dx_ref, dg_ref, dcw_ref, dcb_ref, dwa_ref, dba_ref, dwx_ref, dbx_ref, dlam_ref, a_s, u_s, h_s, g_s):
        b = pl.program_id(1)
        row = lax.broadcasted_iota(jnp.int32, (s_len, 128), 0)
        x = x_ref[...].astype(F32)
        lam = lam_ref[...]
        xc, r, i, sp, a, mult = _lru_gates(x, cw_ref[...], cb_ref[...], wa_ref[0], ba_ref[...], wx_ref[0], bx_ref[...],
                                           lam, row)
        ixc = i * xc
        a_s[...] = a
        u_s[...] = mult * ixc
        _lru_scan_refs(a_s, u_s, h_s)
        h = h_s[...]
        gv = g_ref[...].astype(F32)
        gl, th = _gelu(gv)
        dy = dy_ref[...]
        dg_ref[...] = dy * h * _gelu_grad(gv, th)
        a_s[...] = _shift_up(a, 1, row)
        u_s[...] = dy * gl
        _lru_scan_refs(a_s, u_s, g_s, reverse=True)
        gr = g_s[...]
        da = gr * _shift_dn(h, 1, row)
        dmult = gr * ixc
        di = gr * mult * xc
        dxc = gr * mult * i
        dlog_a = da * a - dmult * (a * a) / mult
        dr = dlog_a * (-RG_C * sp)
        dlam = jnp.sum(dlog_a * r, axis=0, keepdims=True) * (RG_C * _sigmoid(-lam))
        dpa = dr * r * (1.0 - r)
        dpx = di * i * (1.0 - i)
        dxc = dxc + _mdot(dpa, wa_ref[0], NT) + _mdot(dpx, wx_ref[0], NT)
        dx, dcw = _conv_bwd(x, cw_ref[...], dxc, row)
        dx_ref[...] = dx

        @pl.when(b == 0)
        def _():
            for ref in (dcw_ref, dcb_ref, dwa_ref, dba_ref, dwx_ref, dbx_ref, dlam_ref):
                ref[...] = jnp.zeros_like(ref)

        dcw_ref[...] += dcw
        dcb_ref[...] += jnp.sum(dxc, axis=0, keepdims=True)
        dwa_ref[0] += _mdot(xc, dpa, TN)
        dwx_ref[0] += _mdot(xc, dpx, TN)
        dba_ref[...] += jnp.sum(dpa, axis=0, keepdims=True)
        dbx_ref[...] += jnp.sum(dpx, axis=0, keepdims=True)
        dlam_ref[...] += dlam

    order = lambda cb, b: (b, cb)
    act = pl.BlockSpec((s_len, 128), lambda cb, b: (b, cb))
    vec = pl.BlockSpec((1, 128), lambda cb, b: (0, cb))
    mat = pl.BlockSpec((1, 128, 128), lambda cb, b: (cb, 0, 0))
    vshape = jax.ShapeDtypeStruct((1, B_WIDTH), F32)
    mshape = jax.ShapeDtypeStruct((4, 128, 128), F32)
    return _call(
        body, (proj, proj, cw, cb.reshape(1, -1), wa, ba.reshape(1, -1), wx, bxb.reshape(1, -1), lam.reshape(1, -1), dcat),
        side, (4, bsz), name=name,
        in_specs=_lru_specs(s_len, order) + [pl.BlockSpec((s_len, 128), lambda cb, b: (b, 4 + cb))],
        out_specs=[act, act, pl.BlockSpec((4, 128), lambda cb, b: (0, cb)), vec, mat, vec, mat, vec, vec],
        out_shape=[jax.ShapeDtypeStruct((t, B_WIDTH), F32), jax.ShapeDtypeStruct((t, B_WIDTH), F32),
                   jax.ShapeDtypeStruct((4, B_WIDTH), F32), vshape, mshape, vshape, mshape, vshape, vshape],
        scratch_shapes=[pltpu.VMEM((s_len, 128), F32)] * 4,
        compiler_params=_cp("parallel", "arbitrary"),
    )


_BDIMS = {"nn": ((2,), (1,)), "nt": ((2,), (2,)), "tn": ((1,), (1,))}
C_QSCALE = C_HEAD_DIM ** -0.5


def _bmm(a, b, mode, exact=False):
    dims = (_BDIMS[mode], ((0,), (0,)))
    if exact:
        return lax.dot_general(a, b, dims, preferred_element_type=F32, precision=lax.Precision.HIGH)
    return lax.dot_general(a.astype(MM), b.astype(MM), dims, preferred_element_type=F32)


def _col(x, idx, lane):
    return jnp.broadcast_to(jnp.sum(jnp.where(lane == idx, x, 0.0), axis=-1, keepdims=True), x.shape)


def _seg_cumsum(g, row):
    pos = row & (CHUNK - 1)
    d = 1
    while d < CHUNK:
        g = g + jnp.where(pos >= d, pltpu.roll(g, d, 0), 0.0)
        d *= 2
    return g


def _seg_cumsum_rev(g, row):
    pos = row & (CHUNK - 1)
    n = g.shape[0]
    d = 1
    while d < CHUNK:
        g = g + jnp.where(pos < CHUNK - d, pltpu.roll(g, n - d, 0), 0.0)
        d *= 2
    return g


def _gdn_prep(qr, kr, vr, gates, cwq, cwk, cwv, a_log, dtb, h):
    s_len = qr.shape[0]
    nc = s_len // CHUNK
    row = lax.broadcasted_iota(jnp.int32, (s_len, 128), 0)
    lane = lax.broadcasted_iota(jnp.int32, (s_len, 128), 1)
    r = {"row": row, "lane": lane}
    for nm, x, w in (("q", qr, cwq), ("k", kr, cwk), ("v", vr, cwv)):
        c = _conv_fwd(x, w, row)
        sg = _sigmoid(c)
        r["c" + nm], r["s" + nm], r[nm + "c"] = c, sg, c * sg
    r["rq"] = lax.rsqrt(jnp.sum(r["qc"] * r["qc"], axis=-1, keepdims=True) + NORM_EPS)
    r["rk"] = lax.rsqrt(jnp.sum(r["kc"] * r["kc"], axis=-1, keepdims=True) + NORM_EPS)
    r["qn"] = r["qc"] * r["rq"]
    r["kn"] = r["kc"] * r["rk"]
    r["beta"] = _sigmoid(_col(gates, h, lane))
    r["A"] = jnp.exp(a_log)
    r["pre"] = _col(gates, 8 + h, lane) + dtb
    r["sp"] = _softplus(r["pre"])
    gc = _seg_cumsum(-r["A"] * r["sp"], row)
    sh = (nc, CHUNK, 128)
    q3 = (r["qn"] * C_QSCALE).reshape(sh)
    k3 = r["kn"].reshape(sh)
    v3 = r["vc"].reshape(sh)
    beta3 = r["beta"].reshape(sh)
    gc3 = gc.reshape(sh)
    gcl3 = gc3[:, CHUNK - 1:CHUNK, :]
    eg = jnp.exp(gc3)
    ekd = jnp.exp(gcl3 - gc3)
    col64 = gc3[:, :, :CHUNK]
    row64 = jnp.swapaxes(gc3, 1, 2)[:, :CHUNK, :]
    ii = lax.broadcasted_iota(jnp.int32, (nc, CHUNK, CHUNK), 1)
    jj = lax.broadcasted_iota(jnp.int32, (nc, CHUNK, CHUNK), 2)
    tril = ii >= jj
    strict = ii > jj
    dm = jnp.where(tril, jnp.exp(jnp.where(tril, col64 - row64, 0.0)), 0.0)
    kb = k3 * beta3
    lmat = jnp.where(strict, _bmm(kb, k3, "nt") * dm, 0.0)
    attn = _bmm(q3, k3, "nt") * dm
    r.update(q3=q3, k3=k3, v3=v3, beta3=beta3, gc3=gc3, eg=eg, ekd=ekd, gl=jnp.exp(gcl3), dm=dm, kb=kb, lmat=lmat,
             attn=attn, strict=strict, tril=tril, qg=q3 * eg, kdec=k3 * ekd)
    return r


def _neumann_inverse(lmat):
    ii = lax.broadcasted_iota(jnp.int32, lmat.shape, 1)
    jj = lax.broadcasted_iota(jnp.int32, lmat.shape, 2)
    x = -lmat
    tm = jnp.where(ii == jj, 1.0, 0.0) + x
    pw = x
    for _ in range(5):
        pw = _bmm(pw, pw, "nn", exact=True)
        tm = tm + _bmm(tm, pw, "nn", exact=True)
    return tm


def _gdn_specs(s_len):
    act = lambda off: pl.BlockSpec((s_len, 128), lambda b, h: (b, off + h))
    cw = lambda off: pl.BlockSpec((4, 128), lambda b, h: (0, off + h))
    smem = pl.BlockSpec(memory_space=pltpu.SMEM)
    return [act(0), act(8), act(16), act(24), pl.BlockSpec((s_len, 128), lambda b, h: (b, 0)), cw(0), cw(8), cw(16),
            smem, smem, pl.BlockSpec((1, 128), lambda b, h: (0, 0))]


def gdn_fwd(proj, gates, cw, a_log, dtb, ng, bsz, *, name):
    t = proj.shape[0]
    s_len = t // bsz
    nc = s_len // CHUNK

    def body(q_ref, k_ref, v_ref, z_ref, gt_ref, cwq_ref, cwk_ref, cwv_ref, al_ref, dt_ref, ng_ref,
             out_ref, o_ref, vn_ref, tm_ref, st_ref, u_s, w_s, qg_s, kd_s, at_s, gl_s):
        h = pl.program_id(1)
        r = _gdn_prep(q_ref[...], k_ref[...], v_ref[...], gt_ref[...], cwq_ref[...], cwk_ref[...], cwv_ref[...],
                      al_ref[h], dt_ref[h], h)
        tm = _neumann_inverse(r["lmat"])
        tm_ref[0, 0] = tm
        u_s[...] = _bmm(tm, r["v3"] * r["beta3"], "nn", exact=True)
        w_s[...] = _bmm(tm, r["kb"] * r["eg"], "nn", exact=True)
        qg_s[...] = r["qg"]
        kd_s[...] = r["kdec"]
        at_s[...] = r["attn"]
        gl_s[...] = r["gl"]

        def chunk(n, state):
            st = pl.multiple_of(n * CHUNK, CHUNK)
            st_ref[0, 0, n] = state
            v_new = u_s[n] - _mdot(w_s[n], state)
            o_ref[pl.ds(st, CHUNK), :] = _mdot(qg_s[n], state) + _mdot(at_s[n], v_new)
            vn_ref[pl.ds(st, CHUNK), :] = v_new
            return state * gl_s[n] + _mdot(kd_s[n], v_new, TN)

        lax.fori_loop(0, nc, chunk, jnp.zeros((128, 128), F32))
        o = o_ref[...]
        rms = lax.rsqrt(jnp.mean(o * o, axis=-1, keepdims=True) + NORM_EPS)
        z = z_ref[...]
        out_ref[...] = o * rms * ng_ref[...] * (z * _sigmoid(z))

    blk = pl.BlockSpec((s_len, 128), lambda b, h: (b, h))
    full = jax.ShapeDtypeStruct((t, C_WIDTH), F32)
    return _pcall(
        body, name=name, grid=(bsz, C_HEADS), in_specs=_gdn_specs(s_len),
        out_specs=[blk, blk, blk, pl.BlockSpec((1, 1, nc, CHUNK, CHUNK), lambda b, h: (b, h, 0, 0, 0)),
                   pl.BlockSpec((1, 1, nc, 128, 128), lambda b, h: (b, h, 0, 0, 0))],
        out_shape=[full, full, full, jax.ShapeDtypeStruct((bsz, C_HEADS, nc, CHUNK, CHUNK), F32),
                   jax.ShapeDtypeStruct((bsz, C_HEADS, nc, 128, 128), F32)],
        scratch_shapes=[pltpu.VMEM((nc, CHUNK, 128), F32)] * 4 + [pltpu.VMEM((nc, CHUNK, CHUNK), F32),
                                                                   pltpu.VMEM((nc, 1, 128), F32)],
        compiler_params=_cp("parallel", "parallel"),
    )(proj, proj, proj, proj, gates, cw, cw, cw, a_log, dtb, ng.reshape(1, 128))


def gdn_bwd(proj, gates, cw, a_log, dtb, ng, o_pre, vnew, tmat, states, dout, bsz, *, name):
    t = proj.shape[0]
    s_len = t // bsz
    nc = s_len // CHUNK

    def body(q_ref, k_ref, v_ref, z_ref, gt_ref, cwq_ref, cwk_ref, cwv_ref, al_ref, dt_ref, ng_ref,
             o_ref, vn_ref, tm_ref, st_ref, do_ref,
             dq_ref, dk_ref, dv_ref, dz_ref, dgt_ref, dcq_ref, dck_ref, dcv_ref, dsm_ref,
             w_s, qg_s, kd_s, at_s, gl_s, dop_s, du_s, dw_s, dat_s, dqg_s, dkd_s, dgl_s):
        h = pl.program_id(1)
        qr, kr, vr = q_ref[...], k_ref[...], v_ref[...]
        r = _gdn_prep(qr, kr, vr, gt_ref[...], cwq_ref[...], cwk_ref[...], cwv_ref[...], al_ref[h], dt_ref[h], h)
        row, lane = r["row"], r["lane"]
        tm = tm_ref[0, 0]
        q3, k3, v3, beta3, eg, kb, dm = r["q3"], r["k3"], r["v3"], r["beta3"], r["eg"], r["kb"], r["dm"]
        u3 = _bmm(tm, v3 * beta3, "nn", exact=True)
        w3 = _bmm(tm, kb * eg, "nn", exact=True)
        w_s[...] = w3
        qg_s[...] = r["qg"]
        kd_s[...] = r["kdec"]
        at_s[...] = r["attn"]
        gl_s[...] = r["gl"]

        z = z_ref[...]
        sz = _sigmoid(z)
        o = o_ref[...]
        rms = lax.rsqrt(jnp.mean(o * o, axis=-1, keepdims=True) + NORM_EPS)
        on = o * rms
        dout_v = do_ref[...]
        ngv = ng_ref[...]
        dz_ref[...] = dout_v * on * ngv * (sz * (1.0 + z * (1.0 - sz)))
        dos = dout_v * (z * sz)
        dng = jnp.sum(dos * on, axis=0, keepdims=True)
        don = dos * ngv
        dop_s[...] = (rms * (don - on * jnp.mean(don * on, axis=-1, keepdims=True))).reshape(nc, CHUNK, 128)

        def chunk(i, dstate):
            n = nc - 1 - i
            st = pl.multiple_of(n * CHUNK, CHUNK)
            state = st_ref[0, 0, n]
            vn = vn_ref[pl.ds(st, CHUNK), :]
            do_n = dop_s[n]
            dvn = _mdot(at_s[n], do_n, TN) + _mdot(kd_s[n], dstate)
            du_s[n] = dvn
            dat_s[n] = _mdot(do_n, vn, NT)
            dqg_s[n] = _mdot(do_n, state, NT)
            dkd_s[n] = _mdot(vn, dstate, NT)
            dgl_s[n] = jnp.broadcast_to(jnp.sum(jnp.sum(state * dstate, axis=1, keepdims=True), axis=0, keepdims=True), (1, 128))
            dw_s[n] = -_mdot(dvn, state, NT)
            return dstate * gl_s[n] + _mdot(qg_s[n], do_n, TN) - _mdot(w_s[n], dvn, TN)

        lax.fori_loop(0, nc, chunk, jnp.zeros((128, 128), F32))

        du, dw, dqg, dkd = du_s[...], dw_s[...], dqg_s[...], dkd_s[...]
        dat = jnp.where(r["tril"], dat_s[...], 0.0)
        dvb = _bmm(tm, du, "tn", exact=True)
        dkbg = _bmm(tm, dw, "tn", exact=True)
        dl = -jnp.where(r["strict"], _bmm(dvb, u3, "nt") + _bmm(dkbg, w3, "nt"), 0.0)
        dml = dl * dm
        dn = dat * dm
        dkb = _bmm(dml, k3, "nn") + dkbg * eg
        dk3 = _bmm(dml, kb, "tn") + _bmm(dn, q3, "tn") + dkd * r["ekd"] + dkb * beta3
        dq3 = dqg * eg + _bmm(dn, k3, "nn")
        e = dl * r["lmat"] + dat * r["attn"]
        ones = jnp.ones((nc, CHUNK, 128), F32)
        colsum = lax.dot_general(e, ones, (_BDIMS["tn"], ((0,), (0,))), preferred_element_type=F32, precision=lax.Precision.HIGH)
        dgc = jnp.sum(e, axis=-1, keepdims=True) - colsum
        dgc = dgc + eg * (jnp.sum(dqg * q3, axis=-1, keepdims=True) + jnp.sum(dkbg * kb, axis=-1, keepdims=True))
        skd = jnp.sum(dkd * r["kdec"], axis=-1, keepdims=True)
        dgcl = jnp.sum(skd, axis=1, keepdims=True) + dgl_s[...] * r["gl"]
        pos3 = lax.broadcasted_iota(jnp.int32, (nc, CHUNK, 128), 1)
        dgc = dgc - skd + jnp.where(pos3 == CHUNK - 1, dgcl, 0.0)
        dbeta = jnp.sum(dkb * k3, axis=-1, keepdims=True) + jnp.sum(dvb * v3, axis=-1, keepdims=True)
        dv3 = dvb * beta3

        dg = _seg_cumsum_rev(dgc.reshape(s_len, 128), row)
        beta = r["beta"]
        dbl = jnp.broadcast_to(dbeta, (nc, CHUNK, 128)).reshape(s_len, 128) * beta * (1.0 - beta)
        dai = dg * (-r["A"]) * _sigmoid(r["pre"])
        d_dtb = jnp.sum(dai, axis=0, keepdims=True)
        d_alog = jnp.sum(dg * (-r["sp"]), axis=0, keepdims=True) * r["A"]

        @pl.when(h == 0)
        def _():
            dgt_ref[...] = jnp.zeros_like(dgt_ref)
            dsm_ref[...] = jnp.zeros_like(dsm_ref)

        dgt_ref[...] += jnp.where(lane == h, dbl, 0.0) + jnp.where(lane == 8 + h, dai, 0.0)
        r16 = lax.broadcasted_iota(jnp.int32, (16, 128), 0)
        l16 = lax.broadcasted_iota(jnp.int32, (16, 128), 1)
        small = jnp.where((r16 == h) & (l16 == 0), d_alog, 0.0) + jnp.where((r16 == h) & (l16 == 1), d_dtb, 0.0)
        dsm_ref[0] += small + jnp.where(r16 == 8 + h, dng, 0.0)

        dqn = dq3.reshape(s_len, 128) * C_QSCALE
        dkn = dk3.reshape(s_len, 128)
        dqc = r["rq"] * (dqn - r["qn"] * jnp.sum(dqn * r["qn"], axis=-1, keepdims=True))
        dkc = r["rk"] * (dkn - r["kn"] * jnp.sum(dkn * r["kn"], axis=-1, keepdims=True))
        dvc = dv3.reshape(s_len, 128)
        for nm, x, w_ref, dxc, dx_ref, dc_ref in (("q", qr, cwq_ref, dqc, dq_ref, dcq_ref), ("k", kr, cwk_ref, dkc, dk_ref, dck_ref),
                                                 ("v", vr, cwv_ref, dvc, dv_ref, dcv_ref)):
            c, sg = r["c" + nm], r["s" + nm]
            dc = dxc * (sg * (1.0 + c * (1.0 - sg)))
            dx, dwc = _conv_bwd(x, w_ref[...], dc, row)
            dx_ref[...] = dx
            dc_ref[0] = dwc

    blk = pl.BlockSpec((s_len, 128), lambda b, h: (b, h))
    full = jax.ShapeDtypeStruct((t, C_WIDTH), F32)
    cwo = pl.BlockSpec((1, 4, 128), lambda b, h: (b, 0, h))
    cws = jax.ShapeDtypeStruct((bsz, 4, C_WIDTH), F32)
    c128 = pltpu.VMEM((nc, CHUNK, 128), F32)
    outs = _pcall(
        body, name=name, grid=(bsz, C_HEADS),
        in_specs=_gdn_specs(s_len) + [blk, blk, pl.BlockSpec((1, 1, nc, CHUNK, CHUNK), lambda b, h: (b, h, 0, 0, 0)),
                                      pl.BlockSpec((1, 1, nc, 128, 128), lambda b, h: (b, h, 0, 0, 0)), blk],
        out_specs=[blk, blk, blk, blk, pl.BlockSpec((s_len, 128), lambda b, h: (b, 0)), cwo, cwo, cwo,
                   pl.BlockSpec((1, 16, 128), lambda b, h: (b, 0, 0))],
        out_shape=[full, full, full, full, jax.ShapeDtypeStruct((t, 128), F32), cws, cws, cws,
                   jax.ShapeDtypeStruct((bsz, 16, 128), F32)],
        scratch_shapes=[c128, c128, c128, pltpu.VMEM((nc, CHUNK, CHUNK), F32), pltpu.VMEM((nc, 1, 128), F32), c128,
                        c128, c128, pltpu.VMEM((nc, CHUNK, CHUNK), F32), c128, c128, pltpu.VMEM((nc, 1, 128), F32)],
        compiler_params=_cp("parallel", "arbitrary"),
    )(proj, proj, proj, proj, gates, cw, cw, cw, a_log, dtb, ng.reshape(1, 128), o_pre, vnew, tmat, states, dout)
    dq, dk, dv, dz, dgates, dcq, dck, dcv, dsm = outs
    return dq, dk, dv, dz, dgates, jnp.concatenate([dcq, dck, dcv], axis=-1), dsm


def gdc_pre_fwd(proj, cw, bsz, *, name):
    t = proj.shape[0]
    s_len = t // bsz

    def body(x_ref, w_ref, y_ref):
        row = lax.broadcasted_iota(jnp.int32, (s_len, 128), 0)
        c = _conv_fwd(x_ref[...].astype(F32), w_ref[...], row)
        xc = c * _sigmoid(c)
        rn = lax.rsqrt(jnp.sum(xc * xc, axis=-1, keepdims=True) + NORM_EPS)
        y_ref[...] = jnp.where(pl.program_id(1) < 2 * C_HEADS, xc * rn, xc)

    blk = pl.BlockSpec((s_len, 128), lambda b, j: (b, j))
    return _pcall(
        body, name=name, grid=(bsz, 3 * C_HEADS), in_specs=[blk, pl.BlockSpec((4, 128), lambda b, j: (0, j))],
        out_specs=blk, out_shape=jax.ShapeDtypeStruct((t, 3 * C_WIDTH), F32), compiler_params=_cp("parallel", "parallel"),
    )(proj, cw)


def gdc_pre_bwd(proj, cw, dy, dproj, bsz, *, name):
    t = proj.shape[0]
    s_len = t // bsz

    def body(x_ref, w_ref, dq_ref, dk_ref, dv_ref, _, dx_ref, dw_ref):
        row = lax.broadcasted_iota(jnp.int32, (s_len, 128), 0)
        x = x_ref[...].astype(F32)
        c = _conv_fwd(x, w_ref[...], row)
        sg = _sigmoid(c)
        xc = c * sg
        rn = lax.rsqrt(jnp.sum(xc * xc, axis=-1, keepdims=True) + NORM_EPS)
        part = pl.program_id(1) // C_HEADS
        dyv = jnp.where(part == 0, dq_ref[...], jnp.where(part == 1, dk_ref[...], dv_ref[...]))
        xn = xc * rn
        dxc = jnp.where(pl.program_id(1) < 2 * C_HEADS, rn * (dyv - xn * jnp.sum(dyv * xn, axis=-1, keepdims=True)), dyv)
        dc = dxc * (sg * (1.0 + c * (1.0 - sg)))
        dx, dw = _conv_bwd(x, w_ref[...], dc, row)
        dx_ref[...] = dx.astype(MM)
        dw_ref[0] = dw

    blk = pl.BlockSpec((s_len, 128), lambda b, j: (b, j))

    def dy_spec(part):
        return pl.BlockSpec((s_len, 128), lambda b, j: (b, jnp.clip(j - part * C_HEADS, 0, C_HEADS - 1)))

    return _pcall(
        body, name=name, grid=(bsz, 3 * C_HEADS),
        in_specs=[blk, pl.BlockSpec((4, 128), lambda b, j: (0, j)), dy_spec(0), dy_spec(1), dy_spec(2),
                  pl.BlockSpec(memory_space=pl.ANY)],
        out_specs=[blk, pl.BlockSpec((1, 4, 128), lambda b, j: (b, 0, j))],
        out_shape=[jax.ShapeDtypeStruct((t, 4 * C_WIDTH), MM), jax.ShapeDtypeStruct((bsz, 4, 3 * C_WIDTH), F32)],
        input_output_aliases={5: 0}, compiler_params=_cp("parallel", "parallel"),
    )(proj, cw, *dy, dproj)


GDC_GROUP = 16


def _gdc_local(qn, kn, vc, gates, a_log, dtb, h):
    rows = qn.shape[0]
    nc = rows // CHUNK
    row = lax.broadcasted_iota(jnp.int32, (rows, 128), 0)
    lane = lax.broadcasted_iota(jnp.int32, (rows, 128), 1)
    r = {"row": row, "lane": lane}
    r["beta"] = _sigmoid(_col(gates, h, lane))
    r["A"] = jnp.exp(a_log)
    r["pre"] = _col(gates, 8 + h, lane) + dtb
    r["sp"] = _softplus(r["pre"])
    gc = _seg_cumsum(-r["A"] * r["sp"], row)
    sh = (nc, CHUNK, 128)
    q3 = (qn * C_QSCALE).reshape(sh)
    k3 = kn.reshape(sh)
    v3 = vc.reshape(sh)
    beta3 = r["beta"].reshape(sh)
    gc3 = gc.reshape(sh)
    gcl3 = gc3[:, CHUNK - 1:CHUNK, :]
    eg = jnp.exp(gc3)
    ekd = jnp.exp(gcl3 - gc3)
    col64 = gc3[:, :, :CHUNK]
    row64 = jnp.swapaxes(gc3, 1, 2)[:, :CHUNK, :]
    ii = lax.broadcasted_iota(jnp.int32, (nc, CHUNK, CHUNK), 1)
    jj = lax.broadcasted_iota(jnp.int32, (nc, CHUNK, CHUNK), 2)
    tril = ii >= jj
    strict = ii > jj
    dm = jnp.where(tril, jnp.exp(jnp.where(tril, col64 - row64, 0.0)), 0.0)
    kb = k3 * beta3
    lmat = jnp.where(strict, _bmm(kb, k3, "nt") * dm, 0.0)
    attn = _bmm(q3, k3, "nt") * dm
    r.update(q3=q3, k3=k3, v3=v3, beta3=beta3, eg=eg, ekd=ekd, gl=jnp.exp(gcl3), dm=dm, kb=kb, lmat=lmat,
             attn=attn, strict=strict, tril=tril, qg=q3 * eg, kdec=k3 * ekd)
    return r


def _gdc_specs(s_len):
    act = lambda off: pl.BlockSpec((s_len, 128), lambda b, h: (b, off + h))
    smem = pl.BlockSpec(memory_space=pltpu.SMEM)
    return [act(0), act(8), act(16), act(24), pl.BlockSpec((s_len, 128), lambda b, h: (b, 0)), smem, smem,
            pl.BlockSpec((1, 128), lambda b, h: (0, 0))]


def gdc_fwd(qkv, proj, gates, a_log, dtb, ng, bsz, *, name, side=None):
    t = proj.shape[0]
    s_len = t // bsz
    nc = s_len // CHUNK
    grp = min(GDC_GROUP, nc)
    gr = grp * CHUNK

    def body(q_ref, k_ref, v_ref, z_ref, gt_ref, al_ref, dt_ref, ng_ref,
             out_ref, o_ref, tm_ref, st_ref, c_s, b_s, qp_s, op_s, gl_s):
        h = pl.program_id(1)

        def local(gi, carry):
            rs = pl.ds(pl.multiple_of(gi * gr, gr), gr)
            cs = pl.ds(gi * grp, grp)
            r = _gdc_local(q_ref[rs, :], k_ref[rs, :], v_ref[rs, :], gt_ref[rs, :], al_ref[h], dt_ref[h], h)
            tm = _neumann_inverse(r["lmat"])
            tm_ref[0, 0, cs] = tm
            u = _bmm(tm, r["v3"] * r["beta3"], "nn")
            w = _bmm(tm, r["kb"] * r["eg"], "nn")
            c_s[cs] = -_bmm(r["kdec"], w, "tn")
            b_s[cs] = _bmm(r["kdec"], u, "tn")
            qp_s[cs] = r["qg"] - _bmm(r["attn"], w, "nn")
            op_s[cs] = _bmm(r["attn"], u, "nn")
            gl_s[cs] = r["gl"]
            return carry

        lax.fori_loop(0, nc // grp, local, 0)

        def chunk(n, state):
            st = pl.multiple_of(n * CHUNK, CHUNK)
            st_ref[0, 0, n] = state
            o_ref[pl.ds(st, CHUNK), :] = _mdot(qp_s[n], state) + op_s[n]
            return state * gl_s[n] + _mdot(c_s[n], state) + b_s[n]

        lax.fori_loop(0, nc, chunk, jnp.zeros((128, 128), F32))
        o = o_ref[...]
        rms = lax.rsqrt(jnp.mean(o * o, axis=-1, keepdims=True) + NORM_EPS)
        z = z_ref[...].astype(F32)
        out_ref[...] = o * rms * ng_ref[...] * (z * _sigmoid(z))

    blk = pl.BlockSpec((s_len, 128), lambda b, h: (b, h))
    full = jax.ShapeDtypeStruct((t, C_WIDTH), F32)
    return _call(
        body, (qkv, qkv, qkv, proj, gates, a_log, dtb, ng.reshape(1, 128)), side, (bsz, C_HEADS), name=name,
        in_specs=_gdc_specs(s_len),
        out_specs=[blk, blk, pl.BlockSpec((1, 1, nc, CHUNK, CHUNK), lambda b, h: (b, h, 0, 0, 0)),
                   pl.BlockSpec((1, 1, nc, 128, 128), lambda b, h: (b, h, 0, 0, 0))],
        out_shape=[full, full, jax.ShapeDtypeStruct((bsz, C_HEADS, nc, CHUNK, CHUNK), F32),
                   jax.ShapeDtypeStruct((bsz, C_HEADS, nc, 128, 128), F32)],
        scratch_shapes=[pltpu.VMEM((nc, 128, 128), F32)] * 2 + [pltpu.VMEM((nc, CHUNK, 128), F32)] * 2 +
                       [pltpu.VMEM((nc, 1, 128), F32)],
        compiler_params=_cp("parallel", "parallel"),
    )


def gdc_bwd(qkv, proj, gates, a_log, dtb, ng, o_pre, tmat, states, dout, bsz, *, name, side=None):
    t = proj.shape[0]
    s_len = t // bsz
    nc = s_len // CHUNK
    grp = min(GDC_GROUP, nc)
    gr = grp * CHUNK

    def body(q_ref, k_ref, v_ref, z_ref, gt_ref, al_ref, dt_ref, ng_ref, o_ref, tm_ref, st_ref, do_ref,
             dq_ref, dk_ref, dv_ref, dz_ref, dgt_ref, dsm_ref, c_s, e_s, dsp_s, gl_s, dop_s):
        h = pl.program_id(1)
        a_log_h, dtb_h = al_ref[h], dt_ref[h]

        z = z_ref[...].astype(F32)
        sz = _sigmoid(z)
        o = o_ref[...]
        rms = lax.rsqrt(jnp.mean(o * o, axis=-1, keepdims=True) + NORM_EPS)
        on = o * rms
        dout_v = do_ref[...]
        ngv = ng_ref[...]
        dz_ref[...] = (dout_v * on * ngv * (sz * (1.0 + z * (1.0 - sz)))).astype(MM)
        dos = dout_v * (z * sz)
        dng = jnp.sum(dos * on, axis=0, keepdims=True)
        don = dos * ngv
        dop_s[...] = (rms * (don - on * jnp.mean(don * on, axis=-1, keepdims=True))).reshape(nc, CHUNK, 128)

        def local(gi, carry):
            rs = pl.ds(pl.multiple_of(gi * gr, gr), gr)
            cs = pl.ds(gi * grp, grp)
            r = _gdc_local(q_ref[rs, :], k_ref[rs, :], v_ref[rs, :], gt_ref[rs, :], a_log_h, dtb_h, h)
            w = _bmm(tm_ref[0, 0, cs], r["kb"] * r["eg"], "nn")
            c_s[cs] = -_bmm(w, r["kdec"], "tn")
            e_s[cs] = _bmm(r["qg"] - _bmm(r["attn"], w, "nn"), dop_s[cs], "tn")
            gl_s[cs] = r["gl"]
            return carry

        lax.fori_loop(0, nc // grp, local, 0)

        def chunk(i, dstate):
            n = nc - 1 - i
            dsp_s[n] = dstate
            return dstate * gl_s[n] + _mdot(c_s[n], dstate) + e_s[n]

        lax.fori_loop(0, nc, chunk, jnp.zeros((128, 128), F32))

        @pl.when(h == 0)
        def _():
            dgt_ref[...] = jnp.zeros_like(dgt_ref)
            dsm_ref[...] = jnp.zeros_like(dsm_ref)

        def local_bwd(gi, carry):
            d_alog, d_dtb = carry
            rs = pl.ds(pl.multiple_of(gi * gr, gr), gr)
            cs = pl.ds(gi * grp, grp)
            r = _gdc_local(q_ref[rs, :], k_ref[rs, :], v_ref[rs, :], gt_ref[rs, :], a_log_h, dtb_h, h)
            row, lane = r["row"], r["lane"]
            q3, k3, v3, beta3, eg, kb, dm = r["q3"], r["k3"], r["v3"], r["beta3"], r["eg"], r["kb"], r["dm"]
            tm = tm_ref[0, 0, cs]
            u3 = _bmm(tm, v3 * beta3, "nn")
            w3 = _bmm(tm, kb * eg, "nn")
            state, dsp, do3 = st_ref[0, 0, cs], dsp_s[cs], dop_s[cs]
            vn = u3 - _bmm(w3, state, "nn")
            du = _bmm(r["attn"], do3, "tn") + _bmm(r["kdec"], dsp, "nn")
            dat = jnp.where(r["tril"], _bmm(do3, vn, "nt"), 0.0)
            dqg = _bmm(do3, state, "nt")
            dkd = _bmm(vn, dsp, "nt")
            dgl = jnp.sum(jnp.sum(state * dsp, axis=2, keepdims=True), axis=1, keepdims=True)
            dw = -_bmm(du, state, "nt")
            dvb = _bmm(tm, du, "tn")
            dkbg = _bmm(tm, dw, "tn")
            dl = -jnp.where(r["strict"], _bmm(dvb, u3, "nt") + _bmm(dkbg, w3, "nt"), 0.0)
            dml = dl * dm
            dn = dat * dm
            dkb = _bmm(dml, k3, "nn") + dkbg * eg
            dk3 = _bmm(dml, kb, "tn") + _bmm(dn, q3, "tn") + dkd * r["ekd"] + dkb * beta3
            dq3 = dqg * eg + _bmm(dn, k3, "nn")
            e = dl * r["lmat"] + dat * r["attn"]
            ones = jnp.ones((grp, CHUNK, 128), F32)
            colsum = lax.dot_general(e, ones, (_BDIMS["tn"], ((0,), (0,))), preferred_element_type=F32, precision=lax.Precision.HIGH)
            dgc = jnp.sum(e, axis=-1, keepdims=True) - colsum
            dgc = dgc + eg * (jnp.sum(dqg * q3, axis=-1, keepdims=True) + jnp.sum(dkbg * kb, axis=-1, keepdims=True))
            skd = jnp.sum(dkd * r["kdec"], axis=-1, keepdims=True)
            dgcl = jnp.sum(skd, axis=1, keepdims=True) + dgl * r["gl"]
            pos3 = lax.broadcasted_iota(jnp.int32, (grp, CHUNK, 128), 1)
            dgc = dgc - skd + jnp.where(pos3 == CHUNK - 1, dgcl, 0.0)
            dbeta = jnp.sum(dkb * k3, axis=-1, keepdims=True) + jnp.sum(dvb * v3, axis=-1, keepdims=True)
            dg = _seg_cumsum_rev(dgc.reshape(gr, 128), row)
            beta = r["beta"]
            dbl = jnp.broadcast_to(dbeta, (grp, CHUNK, 128)).reshape(gr, 128) * beta * (1.0 - beta)
            dai = dg * (-r["A"]) * _sigmoid(r["pre"])
            dgt_ref[rs, :] += jnp.where(lane == h, dbl, 0.0) + jnp.where(lane == 8 + h, dai, 0.0)
            dq_ref[rs, :] = dq3.reshape(gr, 128) * C_QSCALE
            dk_ref[rs, :] = dk3.reshape(gr, 128)
            dv_ref[rs, :] = (dvb * beta3).reshape(gr, 128)
            return (d_alog + jnp.sum(dg * (-r["sp"]), axis=0, keepdims=True) * r["A"],
                    d_dtb + jnp.sum(dai, axis=0, keepdims=True))

        zero = jnp.zeros((1, 128), F32)
        d_alog, d_dtb = lax.fori_loop(0, nc // grp, local_bwd, (zero, zero))
        r16 = lax.broadcasted_iota(jnp.int32, (16, 128), 0)
        l16 = lax.broadcasted_iota(jnp.int32, (16, 128), 1)
        small = jnp.where((r16 == h) & (l16 == 0), d_alog, 0.0) + jnp.where((r16 == h) & (l16 == 1), d_dtb, 0.0)
        dsm_ref[0] += small + jnp.where(r16 == 8 + h, dng, 0.0)

    blk = pl.BlockSpec((s_len, 128), lambda b, h: (b, h))
    blk3 = lambda off: pl.BlockSpec((s_len, 128), lambda b, h: (b, off + h))
    full = jax.ShapeDtypeStruct((t, C_WIDTH), F32)
    c128 = pltpu.VMEM((nc, CHUNK, 128), F32)
    sq = pltpu.VMEM((nc, 128, 128), F32)
    res = _call(
        body, (qkv, qkv, qkv, proj, gates, a_log, dtb, ng.reshape(1, 128), o_pre, tmat, states, dout), side,
        (bsz, C_HEADS), name=name,
        in_specs=_gdc_specs(s_len) + [blk, pl.BlockSpec((1, 1, nc, CHUNK, CHUNK), lambda b, h: (b, h, 0, 0, 0)),
                                      pl.BlockSpec((1, 1, nc, 128, 128), lambda b, h: (b, h, 0, 0, 0)), blk],
        out_specs=[blk, blk, blk, pl.BlockSpec((s_len, 128), lambda b, h: (b, 3 * C_HEADS + h)),
                   pl.BlockSpec((s_len, 128), lambda b, h: (b, 0)), pl.BlockSpec((1, 16, 128), lambda b, h: (b, 0, 0))],
        out_shape=[full, full, full, jax.ShapeDtypeStruct((t, 4 * C_WIDTH), MM), jax.ShapeDtypeStruct((t, 128), F32),
                   jax.ShapeDtypeStruct((bsz, 16, 128), F32)],
        scratch_shapes=[sq, sq, sq, pltpu.VMEM((nc, 1, 128), F32), c128],
        compiler_params=_cp("parallel", "arbitrary"),
    )
    (dq, dk, dv, dz, dgates, dsm), extra = res if side is not None else (res, None)
    out = ((dq, dk, dv), dz, dgates, dsm)
    return out if side is None else (out, extra)


MESH_ID = pl.DeviceIdType.MESH
_FLIPS = [(0, 0, 1), (1, 0, 0), (0, 1, 0), (1, 1, 0), (1, 0, 1), (0, 1, 1), (1, 1, 1)]


def _me():
    return lax.axis_index("x"), lax.axis_index("y"), lax.axis_index("c")


def _flip(coord, d):
    return 1 - coord if d else coord


def all_gather(shard, *, name):
    def body(x_ref, o_ref, send_sems, recv_sems, local_sem):
        x, y, c = _me()
        mine = 4 * x + 2 * y + c
        own = pltpu.make_async_copy(x_ref, o_ref.at[mine], local_sem)
        own.start()
        copies = []
        for k, (dx, dy, dc) in enumerate(_FLIPS):
            cp = pltpu.make_async_remote_copy(
                src_ref=x_ref, dst_ref=o_ref.at[mine], send_sem=send_sems.at[k], recv_sem=recv_sems.at[k],
                device_id=(_flip(x, dx), _flip(y, dy), _flip(c, dc)), device_id_type=MESH_ID)
            cp.start()
            copies.append(cp)
        for cp in copies:
            cp.wait()
        own.wait()

    hbm = pl.BlockSpec(memory_space=pl.ANY)
    return _pcall(
        body, name=name, in_specs=[hbm], out_specs=hbm,
        out_shape=jax.ShapeDtypeStruct((N_DEV,) + shard.shape, shard.dtype),
        scratch_shapes=[pltpu.SemaphoreType.DMA((7,)), pltpu.SemaphoreType.DMA((7,)), pltpu.SemaphoreType.DMA(())],
    )(shard)


def all_to_all(parts, *, name):
    def body(x_ref, o_ref, send_sems, recv_sems, local_sem):
        x, y, c = _me()
        mine = 4 * x + 2 * y + c
        own = pltpu.make_async_copy(x_ref.at[mine], o_ref.at[mine], local_sem)
        own.start()
        copies = []
        for k, (dx, dy, dc) in enumerate(_FLIPS):
            px, py, pc = _flip(x, dx), _flip(y, dy), _flip(c, dc)
            cp = pltpu.make_async_remote_copy(
                src_ref=x_ref.at[4 * px + 2 * py + pc], dst_ref=o_ref.at[mine], send_sem=send_sems.at[k],
                recv_sem=recv_sems.at[k], device_id=(px, py, pc), device_id_type=MESH_ID)
            cp.start()
            copies.append(cp)
        for cp in copies:
            cp.wait()
        own.wait()

    hbm = pl.BlockSpec(memory_space=pl.ANY)
    return _pcall(
        body, name=name, in_specs=[hbm], out_specs=hbm, out_shape=jax.ShapeDtypeStruct(parts.shape, parts.dtype),
        scratch_shapes=[pltpu.SemaphoreType.DMA((7,)), pltpu.SemaphoreType.DMA((7,)), pltpu.SemaphoreType.DMA(())],
    )(parts)


def adamw_sum(parts, w, m, v, *, name, tr=256):
    r, cdim = w.shape
    tr = _tile8(r, tr)

    def body(p_ref, w_ref, m_ref, v_ref, g_ref, d_ref, mo_ref, vo_ref):
        g = p_ref[0].astype(F32)
        for j in range(1, N_DEV):
            g = g + p_ref[j].astype(F32)
        g_ref[...] = g
        mn = ADAM_B1 * m_ref[...] + (1.0 - ADAM_B1) * g
        vn = ADAM_B2 * v_ref[...] + (1.0 - ADAM_B2) * (g * g)
        mo_ref[...] = mn
        vo_ref[...] = vn
        m_hat = mn / (1.0 - ADAM_B1 ** ADAM_STEP)
        v_hat = vn / (1.0 - ADAM_B2 ** ADAM_STEP)
        d_ref[...] = -ADAM_LR * (m_hat / (jnp.sqrt(v_hat) + ADAM_EPS) + ADAM_WD * w_ref[...])

    blk = pl.BlockSpec((tr, cdim), lambda i: (i, 0))
    shp = jax.ShapeDtypeStruct((r, cdim), F32)
    return _pcall(
        body, name=name, grid=(r // tr,), in_specs=[pl.BlockSpec((N_DEV, tr, cdim), lambda i: (0, i, 0)), blk, blk, blk],
        out_specs=[blk, blk, blk, blk], out_shape=[shp, shp, shp, shp], compiler_params=_cp("parallel"),
    )(parts, w, m, v)


def _tile8(n, pref):
    for c in range(min(pref, n) - min(pref, n) % 16, 0, -16):
        if n % c == 0:
            return c
    return n


BIG = [("ffn1_wg", 2), ("ffn1_wu", 2), ("ffn1_wd", 1), ("ffn2_wg", 2), ("ffn2_wu", 2), ("ffn2_wd", 1), ("ple_wg", 1),
       ("ple_wp", 2), ("ab_w_in", 2), ("ab_w_out", 1), ("c_w_in", 2), ("c_w_out", 1)]
SMALL = [("ln_g", 2), ("ln_b", 2), ("b_conv_w", 2), ("c_conv_w", 2)]
REPL = ["ple_bg", "a_sinks", "b_conv_b", "b_wa", "b_ba", "b_wx", "b_bx", "b_lam", "c_a_log", "c_dt_bias", "c_norm_g"]
WEIGHTS = ["ffn1_wg", "ffn1_wu", "ffn1_wd", "ffn2_wg", "ffn2_wu", "ffn2_wd", "ln_g", "ln_b", "ple_wg", "ple_bg", "ple_wp",
           "ab_w_in", "a_sinks", "b_conv_w", "b_conv_b", "b_wa", "b_ba", "b_wx", "b_bx", "b_lam", "ab_w_out", "c_w_in",
           "c_conv_w", "c_a_log", "c_dt_bias", "c_norm_g", "c_w_out"]
PACK_COLS = 1024
PACK_ALIGN = 16 * PACK_COLS


def _as_bf16_bits(a):
    return lax.bitcast_convert_type(a, jnp.bfloat16).reshape(a.shape[:-1] + (2 * a.shape[-1],))


def _from_bf16_bits(a):
    return lax.bitcast_convert_type(a.reshape(a.shape[:-1] + (a.shape[-1] // 2, 2)), F32)


def _pad_rows(flat, align=PACK_ALIGN):
    n = flat.shape[-1]
    total = -(-n // align) * align
    flat = jnp.pad(flat, [(0, 0)] * (flat.ndim - 1) + [(0, total - n)])
    return flat.reshape(flat.shape[:-1] + (total // PACK_COLS, PACK_COLS))


def _join(blocks, axis):
    moved = jnp.moveaxis(blocks, 0, axis)
    shp = list(moved.shape)
    return moved.reshape(shp[:axis] + [shp[axis] * shp[axis + 1]] + shp[axis + 2:])


def _split(full, axis):
    shp = list(full.shape)
    return jnp.moveaxis(full.reshape(shp[:axis] + [N_DEV, shp[axis] // N_DEV] + shp[axis + 1:]), axis, 0)


def _dense_blocks(w):
    z = jnp.zeros((4, 2, 64, 2, 64), w.dtype)
    w4 = w.reshape(4, 2, 64, 64)
    z = z.at[:, 0, :, 0, :].set(w4[:, 0]).at[:, 1, :, 1, :].set(w4[:, 1])
    return z.reshape(4, 128, 128)


def _diag_blocks(d):
    d5 = d.reshape(4, 2, 64, 2, 64)
    return jnp.stack([d5[:, 0, :, 0, :], d5[:, 1, :, 1, :]], axis=1).reshape(8, 64, 64)


def kernel(x, p, ffn1_wg, ffn1_wu, ffn1_wd, ffn2_wg, ffn2_wu, ffn2_wd, ln_g, ln_b, ple_wg, ple_bg, ple_wp, ab_w_in, a_sinks, b_conv_w, b_conv_b, b_wa, b_ba, b_wx, b_bx, b_lam, ab_w_out, c_w_in, c_conv_w, c_a_log, c_dt_bias, c_norm_g, c_w_out, loss_target, m_ffn1_wg, m_ffn1_wu, m_ffn1_wd, m_ffn2_wg, m_ffn2_wu, m_ffn2_wd, m_ln_g, m_ln_b, m_ple_wg, m_ple_bg, m_ple_wp, m_ab_w_in, m_a_sinks, m_b_conv_w, m_b_conv_b, m_b_wa, m_b_ba, m_b_wx, m_b_bx, m_b_lam, m_ab_w_out, m_c_w_in, m_c_conv_w, m_c_a_log, m_c_dt_bias, m_c_norm_g, m_c_w_out, v_ffn1_wg, v_ffn1_wu, v_ffn1_wd, v_ffn2_wg, v_ffn2_wu, v_ffn2_wd, v_ln_g, v_ln_b, v_ple_wg, v_ple_bg, v_ple_wp, v_ab_w_in, v_a_sinks, v_b_conv_w, v_b_conv_b, v_b_wa, v_b_ba, v_b_wx, v_b_bx, v_b_lam, v_ab_w_out, v_c_w_in, v_c_conv_w, v_c_a_log, v_c_dt_bias, v_c_norm_g, v_c_w_out):
    a = dict(locals())
    return _step3(a)


def _step3(a):
    x, p = a["x"], a["p"]
    bsz, s_len, d = x.shape
    t = bsz * s_len
    x2 = x.reshape(t, d)
    tgt = a["loss_target"].reshape(t, d)
    p2 = p.reshape(DEPTH, t, D_PLE)
    shapes = {n: a[n].shape for n in WEIGHTS}
    n_small = sum(int(np.prod(shapes[n])) for n in SMALL_NAMES)
    small_all = SMALL_NAMES + REPL
    f_ff = shapes["ffn1_wg"][2]
    c_cols = shapes["c_w_in"][2]
    wide = dict(tm=1024, tn=1408, tk=1024)
    tall = dict(tm=1408, tn=1024, tk=1024)

    def cast(z):
        return z.astype(MM)

    def ffn_shards(which, l):
        return [cast(a[which + "_wg"][l]), cast(a[which + "_wu"][l]), cast(a[which + "_wd"][l])]

    def ffn_weights(gat, tag):
        return (join_cols(gat[0], name=f"join_{tag}_wg"), join_cols(gat[1], name=f"join_{tag}_wu"),
                gat[2].reshape(N_DEV * gat[2].shape[1], D_MODEL))

    def rows_full(gat):
        return gat.reshape(N_DEV * gat.shape[1], D_MODEL)

    small_send = _flat_pad([a[n] for n in SMALL_NAMES], F32, 32 * LANES).reshape(32, LANES)
    g0 = gather_multi(ffn_shards("ffn1", 0) + [small_send], name="gather_first")
    ws = _take(g0[3].reshape(N_DEV, -1), SMALL_NAMES, shapes)
    small = {n: _join(ws[n], 2) for n in SMALL_NAMES}
    ln_g, ln_b = small["ln_g"], small["ln_b"]
    wa_d, wx_d = _dense_blocks(a["b_wa"][0]), _dense_blocks(a["b_wx"][0])
    lru_w = (small["b_conv_w"][0], a["b_conv_b"][0], wa_d, a["b_ba"][0], wx_d, a["b_bx"][0], a["b_lam"][0])
    gdc_w = (a["c_a_log"][0], a["c_dt_bias"][0], a["c_norm_g"][0])
    wf = {("ffn1", 0): ffn_weights(g0[:3], "ffn1_0")}

    s0 = {"x0": x2}
    u1 = ffn_shards("ffn2", 0)
    side = ("gather", u1[:2] + [cast(a["ab_w_in"][0]), cast(a["ab_w_out"][0])])
    (s0["y1"], s0["z1"], s0["hg1"], s0["hu1"]), got = ffn_fwd(x2, *wf["ffn1", 0], ln_g[0, 0], ln_b[0, 0],
                                                             name="ffn1_fwd_0", tm=FFN_TM, tf=FFN_TF, side=side)
    ab_w_in, ab_w_out = join_cols(got[2], name="join_ab_in"), rows_full(got[3])
    s0["proj"] = matmul(s0["y1"], ab_w_in, mode="nn", out_dtype=MM, name="ab_in_fwd", tn=896, tk=1024)
    ya, got_ple = attn_fwd(s0["proj"], a["a_sinks"][0], bsz, name="attn_fwd",
                           side=("gather", [u1[2], cast(a["ple_wg"][0]), cast(a["ple_wp"][0])]))
    wf["ffn2", 0] = ffn_weights(got[:2] + got_ple[:1], "ffn2_0")
    got_ple = got_ple[1:]
    yb = lru_fwd(s0["proj"], *lru_w, bsz, name="lru_fwd")
    s0["mix"] = jnp.concatenate([ya, yb], axis=1)
    s0["y2"], s0["z2"] = mm_ln_fwd(s0["mix"], ab_w_out, s0["y1"], ln_g[0, 1], ln_b[0, 1], name="mix_out_fwd_0")
    side = ("gather", ffn_shards("ffn1", 1))
    (s0["y3"], s0["z3"], s0["hg2"], s0["hu2"]), got = ffn_fwd(s0["y2"], *wf["ffn2", 0], ln_g[0, 2], ln_b[0, 2],
                                                             name="ffn2_fwd_0", tm=FFN_TM, tf=FFN_TF, side=side)
    wf["ffn1", 1] = ffn_weights(got, "ffn1_1")
    ple_wg = [rows_full(got_ple[0]), None]
    ple_wp = [_join(got_ple[1], 1), None]
    h1 = ple_fwd(s0["y3"], p2[0], ple_wg[0], a["ple_bg"][0], ple_wp[0], name="ple_fwd_0")

    s1 = {"x0": h1}
    side = ("gather", [cast(a["c_w_in"][0]), cast(a["c_w_out"][0])])
    (s1["y1"], s1["z1"], s1["hg1"], s1["hu1"]), got = ffn_fwd(h1, *wf["ffn1", 1], ln_g[1, 0], ln_b[1, 0],
                                                             name="ffn1_fwd_1", tm=FFN_TM, tf=FFN_TF, side=side)
    c_in_main, c_in_gate = join_cols(got[0], name="join_c_in", outs=[(0, 4 * C_WIDTH, 4 * C_WIDTH),
                                                                      (4 * C_WIDTH, 4 * C_WIDTH + 2 * C_HEADS, LANES)])
    c_w_out = rows_full(got[1])
    s1["proj"] = matmul(s1["y1"], c_in_main, mode="nn", out_dtype=MM, name="c_in_fwd", tm=1024, tn=2048, tk=1024)
    s1["gates"] = matmul(s1["y1"], c_in_gate, mode="nn", name="c_gate_fwd", tk=1024)
    s1["qkv"] = gdc_pre_fwd(s1["proj"], small["c_conv_w"][0], bsz, name="gdc_pre_fwd")
    side = ("gather", ffn_shards("ffn2", 1) + [cast(a["ple_wg"][1]), cast(a["ple_wp"][1])])
    (s1["mix"], s1["o_pre"], s1["tmat"], s1["states"]), got = gdc_fwd(
        s1["qkv"], s1["proj"], s1["gates"], *gdc_w, bsz, name="gdc_fwd", side=side)
    wf["ffn2", 1] = ffn_weights(got[:3], "ffn2_1")
    ple_wg[1], ple_wp[1] = rows_full(got[3]), _join(got[4], 1)
    s1["y2"], s1["z2"] = mm_ln_fwd(s1["mix"], c_w_out, s1["y1"], ln_g[1, 1], ln_b[1, 1], name="mix_out_fwd_1")
    s1["y3"], s1["z3"], s1["hg2"], s1["hu2"] = ffn_fwd(s1["y2"], *wf["ffn2", 1], ln_g[1, 2], ln_b[1, 2], name="ffn2_fwd_1",
                                                           tm=FFN_TM, tf=FFN_TF)
    h2 = ple_fwd(s1["y3"], p2[1], ple_wg[1], a["ple_bg"][1], ple_wp[1], name="ple_fwd_1")
    loss_part, dh = loss_fwd_bwd(h2, tgt, name="loss")

    def ffn_parts(xin, act, dhg, dhu, dz, tag):
        dwg = matmul(xin, dhg, mode="tn", out_dtype=MM, split_n=f_ff, name=f"{tag}_wg_grad", **wide)
        dwu = matmul(xin, dhu, mode="tn", out_dtype=MM, split_n=f_ff, name=f"{tag}_wu_grad", **wide)
        dwd = matmul(act, dz, mode="tn", scale=0.5, out_dtype=MM, name=f"{tag}_wd_grad", **tall)
        return [dwg, dwu, dwd.reshape(N_DEV, f_ff, D_MODEL)]

    def ple_parts(i, s, dt, de):
        gwg = matmul(s["y3"], dt, mode="tn", out_dtype=MM, name=f"ple_wg_grad_{i}", tm=1024, tn=1024)
        gwp = matmul(p2[i], de, mode="tn", out_dtype=MM, name=f"ple_wp_grad_{i}", tn=1024)
        return [gwg.reshape(N_DEV, D_MODEL // N_DEV, D_MODEL), _split(gwp, 1)]

    gln = {"ln_g": [None, None], "ln_b": [None, None]}
    gple_bg = [None, None]

    dz3, dt, de, dbg, dg2, db2 = ple_bwd(dh, s1["y3"], p2[1], ple_wg[1], a["ple_bg"][1], ple_wp[1], s1["z3"], ln_g[1, 2],
                                         name="ple_bwd_1")
    gple_bg[1] = dbg[0]
    parts_ple1 = ple_parts(1, s1, dt, de)
    dy2, act, dhg, dhu = ffn_bwd(dz3, s1["hg2"], s1["hu2"], *wf["ffn2", 1], name="ffn2_bwd_1", tm=FFN_TM, tf=FFN_TF)
    parts_ffn2_1 = ffn_parts(s1["y2"], act, dhg, dhu, dz3, "ffn2_1")
    dz2, dg1, db1 = ln_bwd(dy2, s1["z2"], ln_g[1, 1], name="ln1_bwd_1")
    dmix = matmul(dz2, c_w_out, mode="nt", name="c_out_bwd", tn=1024, tk=1024)
    parts_c_out = matmul(s1["mix"], dz2, mode="tn", out_dtype=MM, name="c_out_grad", tm=1024, tn=1024).reshape(
        N_DEV, D_MODEL // N_DEV, D_MODEL)
    (dqkv, dzc, dgates, dsm), recv1 = gdc_bwd(s1["qkv"], s1["proj"], s1["gates"], *gdc_w, s1["o_pre"],
                                              s1["tmat"], s1["states"], dmix, bsz, name="gdc_bwd",
                                              side=("exchange", parts_ffn2_1 + parts_ple1))
    dproj, dccw = gdc_pre_bwd(s1["proj"], small["c_conv_w"][0], dqkv, dzc, bsz, name="gdc_pre_bwd")
    dgb = dgates.astype(MM)
    dy1 = matmul(dproj, c_in_main, mode="nt", add=dz2, add_scale=DN_ALPHA, name="c_in_bwd", tn=1024, tk=4096)
    dz1, dg0, db0 = matmul(dgb, c_in_gate, mode="nt", add=dy1, name="c_gate_bwd", tn=1024, ln=(s1["z1"], ln_g[1, 0]))
    g_c_main = matmul(s1["y1"], dproj, mode="tn", name="c_in_grad", tm=1024, tn=1024, tk=1024)
    g_c_gate = matmul(s1["y1"], dgb, mode="tn", name="c_gate_grad", tm=1024)
    parts_c_in = split_cols([(g_c_main, 4 * C_WIDTH), (g_c_gate, 2 * C_HEADS)], c_cols, name="split_c_in")
    (dh, act, dhg, dhu), recv_c = ffn_bwd(dz1, s1["hg1"], s1["hu1"], *wf["ffn1", 1], name="ffn1_bwd_1", tm=FFN_TM,
                                          tf=FFN_TF, side=("exchange", [parts_c_in, parts_c_out]))
    parts_ffn1_1 = ffn_parts(s1["x0"], act, dhg, dhu, dz1, "ffn1_1")
    gln["ln_g"][1] = jnp.concatenate([dg0, dg1, dg2], axis=0)
    gln["ln_b"][1] = jnp.concatenate([db0, db1, db2], axis=0)

    dz3, dt, de, dbg, dg2, db2 = ple_bwd(dh, s0["y3"], p2[0], ple_wg[0], a["ple_bg"][0], ple_wp[0], s0["z3"], ln_g[0, 2],
                                         name="ple_bwd_0")
    gple_bg[0] = dbg[0]
    parts_ple0 = ple_parts(0, s0, dt, de)
    (dy2, act, dhg, dhu), recv2 = ffn_bwd(dz3, s0["hg2"], s0["hu2"], *wf["ffn2", 0], name="ffn2_bwd_0", tm=FFN_TM,
                                          tf=FFN_TF, side=("exchange", parts_ffn1_1))
    parts_ffn2_0 = ffn_parts(s0["y2"], act, dhg, dhu, dz3, "ffn2_0")
    dz2, dg1, db1 = ln_bwd(dy2, s0["z2"], ln_g[0, 1], name="ln1_bwd_0")
    dmix = matmul(dz2, ab_w_out, mode="nt", name="ab_out_bwd", tn=1024, tk=1024)
    parts_ab_out = matmul(s0["mix"], dz2, mode="tn", out_dtype=MM, name="ab_out_grad", tm=1024, tn=1024).reshape(
        N_DEV, D_MODEL // N_DEV, D_MODEL)
    (dq, dk, dv, dsk), recv3a = attn_bwd(s0["proj"], a["a_sinks"][0], dmix, bsz, name="attn_bwd",
                                         side=("exchange", parts_ffn2_0[:2]))
    (dbx, dbgate, dcw, dcb, dwa, dba, dwx, dbxb, dlam), recv3b = lru_bwd(
        s0["proj"], *lru_w, dmix, bsz, name="lru_bwd", side=("exchange", [parts_ffn2_0[2]] + parts_ple0 + [parts_ab_out]))
    dproj = jnp.concatenate([dq, dk, dv, dbx, dbgate], axis=1).astype(MM)
    dz1, dg0, db0 = matmul(dproj, ab_w_in, mode="nt", add=dz2, add_scale=DN_ALPHA, name="ab_in_bwd", tn=1024, tk=1792,
                           ln=(s0["z1"], ln_g[0, 0]))
    parts_ab_in = matmul(s0["y1"], dproj, mode="tn", out_dtype=MM, split_n=AB_PROJ // N_DEV, name="ab_in_grad",
                         tm=1024, tn=896)
    gln["ln_g"][0] = jnp.concatenate([dg0, dg1, dg2], axis=0)
    gln["ln_b"][0] = jnp.concatenate([db0, db1, db2], axis=0)

    dsm_sum = jnp.sum(dsm, axis=0)
    full = dict(ln_g=jnp.stack(gln["ln_g"]), ln_b=jnp.stack(gln["ln_b"]), b_conv_w=dcw[None],
                c_conv_w=jnp.sum(dccw, axis=0)[None], ple_bg=jnp.stack(gple_bg),
                a_sinks=jnp.sum(dsk, axis=0)[:, :A_HEADS], b_conv_b=dcb, b_wa=_diag_blocks(dwa)[None], b_ba=dba,
                b_wx=_diag_blocks(dwx)[None], b_bx=dbxb, b_lam=dlam, c_a_log=dsm_sum[None, :C_HEADS, 0],
                c_dt_bias=dsm_sum[None, :C_HEADS, 1], c_norm_g=jnp.sum(dsm_sum[C_HEADS:], axis=0)[None])
    small_rows = SMALL_F32 // LANES
    repl_flat = _flat_pad([full[n] for n in REPL], F32, SMALL_F32 - n_small)
    small8 = jnp.concatenate([_flat8_pad([_split(full[n], 2) for n in SMALL_NAMES], F32, n_small),
                              jnp.broadcast_to(repl_flat, (N_DEV,) + repl_flat.shape)], axis=1)
    (dh, act, dhg, dhu), recv3c = ffn_bwd(dz1, s0["hg1"], s0["hu1"], *wf["ffn1", 0], name="ffn1_bwd_0", tm=FFN_TM,
                                          tf=FFN_TF, side=("exchange", [parts_ab_in, small8.reshape(N_DEV, small_rows, LANES)]))
    grad_x = dh.reshape(bsz, s_len, d)

    parts_wg = matmul(s0["x0"], dhg, mode="tn", out_dtype=MM, split_n=f_ff, name="ffn1_0_wg_grad", **wide)
    parts_wu, recv4a = matmul(s0["x0"], dhu, mode="tn", out_dtype=MM, split_n=f_ff, name="ffn1_0_wu_grad",
                              side=("exchange", [parts_wg]), **wide)
    dwd, recv4b = matmul(act, dz1, mode="tn", scale=0.5, out_dtype=MM, name="ffn1_0_wd_grad",
                         side=("exchange", [parts_wu]), **tall)
    recv4c = exchange_multi([dwd.reshape(N_DEV, f_ff, D_MODEL)], name="exchange_last")

    def upd(parts, n, l, shape2d, **kw):
        wmv = [a[pre + n][l].reshape(shape2d) for pre in ("", "m_", "v_")]
        return adamw_rows(parts, 0, *wmv, name=f"adamw_{n}_{l}", **kw)

    def upd_ffn(parts, which, l):
        return {(which + "_wg", l): upd(parts[0], which + "_wg", l, (D_MODEL, f_ff)),
                (which + "_wu", l): upd(parts[1], which + "_wu", l, (D_MODEL, f_ff)),
                (which + "_wd", l): upd(parts[2], which + "_wd", l, (f_ff, D_MODEL), tr=176)}

    rows8 = D_MODEL // N_DEV
    res = {}
    res.update(upd_ffn(recv1[:3], "ffn2", 1))
    res["ple_wg", 1] = upd(recv1[3], "ple_wg", 1, (rows8, D_MODEL), tr=128)
    res["ple_wp", 1] = upd(recv1[4], "ple_wp", 1, (D_PLE, LANES))
    res.update(upd_ffn(recv2, "ffn1", 1))
    res["c_w_in", 0] = upd(recv_c[0], "c_w_in", 0, (D_MODEL, c_cols))
    res["c_w_out", 0] = upd(recv_c[1], "c_w_out", 0, (rows8, D_MODEL), tr=128)
    res.update(upd_ffn(recv3a + recv3b[:1], "ffn2", 0))
    res["ple_wg", 0] = upd(recv3b[1], "ple_wg", 0, (rows8, D_MODEL), tr=128)
    res["ple_wp", 0] = upd(recv3b[2], "ple_wp", 0, (D_PLE, LANES))
    res["ab_w_out", 0] = upd(recv3b[3], "ab_w_out", 0, (rows8, D_MODEL), tr=128)
    res.update(upd_ffn(recv4a + recv4b + list(recv4c), "ffn1", 0))
    res["ab_w_in", 0] = upd(recv3c[0], "ab_w_in", 0, (D_MODEL, AB_PROJ // N_DEV))
    res_small = adamw_rows(recv3c[1], 0, *[_flat_pad([a[pre + n] for n in small_all], F32, SMALL_F32).reshape(
        small_rows, LANES) for pre in ("", "m_", "v_")], name="adamw_small", tr=small_rows)
    kinds = []
    for k in range(4):
        kd = _take(res_small[k].reshape(-1), small_all, shapes)
        for n in WEIGHTS:
            if n not in kd:
                kd[n] = jnp.stack([res[n, l][k] for l in range(shapes[n][0])]).reshape(shapes[n])
        kinds.append(kd)
    loss = lax.psum(loss_part[0, 0], ("x", "y", "c"))
    return (loss, grad_x, *[kinds[0][n] for n in WEIGHTS], *[kinds[1][n] for n in WEIGHTS],
            *[kinds[2][n] for n in WEIGHTS], *[kinds[3][n] for n in WEIGHTS])


def join_cols(x, *, name, outs=None, tk=256):
    _, kk, n = x.shape
    tk = _tile8(kk, tk)
    outs = outs or [(0, N_DEV * n, N_DEV * n)]

    def body(x_ref, *o_refs):
        full = jnp.concatenate([x_ref[k] for k in range(N_DEV)], axis=-1)
        for (lo, hi, wd), o_ref in zip(outs, o_refs):
            piece = full[:, lo:hi]
            if wd > hi - lo:
                piece = jnp.concatenate([piece, jnp.zeros((tk, wd - (hi - lo)), piece.dtype)], axis=-1)
            o_ref[...] = piece

    res = _pcall(
        body, name=name, grid=(kk // tk,), in_specs=[pl.BlockSpec((N_DEV, tk, n), lambda i: (0, i, 0))],
        out_specs=[pl.BlockSpec((tk, wd), lambda i: (i, 0)) for _, _, wd in outs],
        out_shape=[jax.ShapeDtypeStruct((kk, wd), x.dtype) for _, _, wd in outs], compiler_params=_cp("parallel"),
    )(x)
    return res if len(outs) > 1 else res[0]


def split_cols(pieces, n, *, name, tk=256):
    kk = pieces[0][0].shape[0]
    tk = _tile8(kk, tk)

    def body(*refs):
        o_ref = refs[-1]
        vals = [r[...][:, :used] for r, (_, used) in zip(refs[:-1], pieces)]
        full = vals[0] if len(vals) == 1 else jnp.concatenate(vals, axis=-1)
        for k in range(N_DEV):
            o_ref[k] = full[:, k * n:(k + 1) * n].astype(MM)

    return _pcall(
        body, name=name, grid=(kk // tk,),
        in_specs=[pl.BlockSpec((tk, arr.shape[1]), lambda i: (i, 0)) for arr, _ in pieces],
        out_specs=pl.BlockSpec((N_DEV, tk, n), lambda i: (0, i, 0)),
        out_shape=jax.ShapeDtypeStruct((N_DEV, kk, n), MM), compiler_params=_cp("parallel"),
    )(*[arr for arr, _ in pieces])


def gather_multi(shards, *, name):
    ng = len(shards)

    def body(*refs):
        x_refs, o_refs = refs[:ng], refs[ng:2 * ng]
        send_sems, recv_sems, local_sems = refs[2 * ng:]
        x, y, c = _me()
        sibling = (x, y, 1 - c)
        chips = [(1 - x, y), (x, 1 - y), (1 - x, 1 - y)]

        def slot(px, py, pc):
            return 4 * px + 2 * py + pc

        def copy(gi, k, block, to, src=None):
            dst = o_refs[gi].at[slot(*block)]
            return pltpu.make_async_remote_copy(
                src_ref=dst if src is None else src, dst_ref=dst, send_sem=send_sems.at[7 * gi + k],
                recv_sem=recv_sems.at[7 * gi + k], device_id=to, device_id_type=MESH_ID)

        own = [pltpu.make_async_copy(x_refs[gi], o_refs[gi].at[slot(x, y, c)], local_sems.at[gi]) for gi in range(ng)]
        for cp in own:
            cp.start()
        first = []
        for gi in range(ng):
            first.append(copy(gi, 0, (x, y, c), sibling, src=x_refs[gi]))
            first += [copy(gi, 1 + j, (x, y, c), (*chip, c), src=x_refs[gi]) for j, chip in enumerate(chips)]
        for cp in first:
            cp.start()
        passed = []
        for j, chip in enumerate(chips):
            for gi in range(ng):
                copy(gi, 1 + j, (*chip, c), (x, y, c)).wait_recv()
                fwd = copy(gi, 4 + j, (*chip, c), sibling)
                fwd.start()
                passed.append(fwd)
        for gi in range(ng):
            copy(gi, 0, sibling, (x, y, c)).wait_recv()
            for j, chip in enumerate(chips):
                copy(gi, 4 + j, (*chip, 1 - c), (x, y, c)).wait_recv()
        for cp in first + passed:
            cp.wait_send()
        for cp in own:
            cp.wait()

    hbm = pl.BlockSpec(memory_space=pl.ANY)
    return _pcall(
        body, name=name, in_specs=[hbm] * ng, out_specs=[hbm] * ng,
        out_shape=[jax.ShapeDtypeStruct((N_DEV,) + s.shape, s.dtype) for s in shards],
        scratch_shapes=[pltpu.SemaphoreType.DMA((7 * ng,)), pltpu.SemaphoreType.DMA((7 * ng,)),
                        pltpu.SemaphoreType.DMA((ng,))],
    )(*shards)


def exchange_multi(parts, *, name):
    ng = len(parts)

    def body(*refs):
        x_refs, o_refs = refs[:ng], refs[ng:2 * ng]
        send_sems, recv_sems, local_sems = refs[2 * ng:]
        x, y, c = _me()
        mine = 4 * x + 2 * y + c
        own = [pltpu.make_async_copy(x_refs[gi].at[mine], o_refs[gi].at[mine], local_sems.at[gi]) for gi in range(ng)]
        for cp in own:
            cp.start()
        copies = []
        for k, (dx, dy, dc) in enumerate(_FLIPS):
            px, py, pc = _flip(x, dx), _flip(y, dy), _flip(c, dc)
            for gi in range(ng):
                cp = pltpu.make_async_remote_copy(
                    src_ref=x_refs[gi].at[4 * px + 2 * py + pc], dst_ref=o_refs[gi].at[mine],
                    send_sem=send_sems.at[7 * gi + k], recv_sem=recv_sems.at[7 * gi + k], device_id=(px, py, pc),
                    device_id_type=MESH_ID)
                cp.start()
                copies.append(cp)
        for cp in copies:
            cp.wait()
        for cp in own:
            cp.wait()

    hbm = pl.BlockSpec(memory_space=pl.ANY)
    return _pcall(
        body, name=name, in_specs=[hbm] * ng, out_specs=[hbm] * ng,
        out_shape=[jax.ShapeDtypeStruct(s.shape, s.dtype) for s in parts],
        scratch_shapes=[pltpu.SemaphoreType.DMA((7 * ng,)), pltpu.SemaphoreType.DMA((7 * ng,)),
                        pltpu.SemaphoreType.DMA((ng,))],
    )(*parts)


def adamw_rows(parts, row0, w, m, v, *, name, tr=256):
    r, cdim = w.shape
    tr = _tile8(math.gcd(r, row0) if row0 else r, tr)
    blk0 = row0 // tr

    def body(p_ref, w_ref, m_ref, v_ref, g_ref, d_ref, mo_ref, vo_ref):
        g = p_ref[0].astype(F32)
        for j in range(1, N_DEV):
            g = g + p_ref[j].astype(F32)
        g_ref[...] = g
        mn = ADAM_B1 * m_ref[...] + (1.0 - ADAM_B1) * g
        vn = ADAM_B2 * v_ref[...] + (1.0 - ADAM_B2) * (g * g)
        mo_ref[...] = mn
        vo_ref[...] = vn
        m_hat = mn / (1.0 - ADAM_B1 ** ADAM_STEP)
        v_hat = vn / (1.0 - ADAM_B2 ** ADAM_STEP)
        d_ref[...] = -ADAM_LR * (m_hat / (jnp.sqrt(v_hat) + ADAM_EPS) + ADAM_WD * w_ref[...])

    blk = pl.BlockSpec((tr, cdim), lambda i: (i, 0))
    shp = jax.ShapeDtypeStruct((r, cdim), F32)
    return _pcall(
        body, name=name, grid=(r // tr,),
        in_specs=[pl.BlockSpec((N_DEV, tr, cdim), lambda i: (0, blk0 + i, 0)), blk, blk, blk],
        out_specs=[blk, blk, blk, blk], out_shape=[shp, shp, shp, shp], compiler_params=_cp("parallel"),
    )(parts, w, m, v)


GROUP_A = ["ffn1_wg", "ffn1_wu", "ffn2_wg", "ffn2_wu"]
GROUP_B = ["ffn1_wd", "ffn2_wd", "ple_wg", "ab_w_out", "c_w_out"]
SMALL_NAMES = ["ln_g", "ln_b", "b_conv_w", "c_conv_w"]
LANES = 128
FFN_TM = 512
FFN_TF = 1408
SMALL_F32 = 73728
PLE_WP_ROWS = DEPTH * D_PLE


def _step2(a):
    x, p = a["x"], a["p"]
    bsz, s_len, d = x.shape
    t = bsz * s_len
    x2 = x.reshape(t, d)
    tgt = a["loss_target"].reshape(t, d)
    p2 = p.reshape(DEPTH, t, D_PLE)
    shapes = {n: a[n].shape for n in WEIGHTS}
    bits_per = 1 if MM == F32 else 2
    n_small = sum(int(np.prod(shapes[n])) for n in SMALL_NAMES)
    small_all = SMALL_NAMES + REPL
    f_ff = shapes["ffn1_wg"][2]
    rows_b = {n: shapes[n][0] * shapes[n][1] for n in GROUP_B}
    off_b = dict(zip(GROUP_B, np.cumsum([0] + [rows_b[n] for n in GROUP_B])[:-1].tolist()))

    send = [
        jnp.concatenate([a[n].astype(MM).reshape(-1, f_ff) for n in GROUP_A], axis=0),
        jnp.concatenate([a[n].astype(MM).reshape(-1, D_MODEL) for n in GROUP_B], axis=0),
        a["ab_w_in"][0].astype(MM),
        a["c_w_in"][0].astype(MM),
        a["ple_wp"].astype(MM).reshape(PLE_WP_ROWS, LANES),
        _flat_pad([a[n] for n in SMALL_NAMES], F32, 32 * LANES).reshape(32, LANES),
    ]
    ga, gb, gc, gd, ge, gf = gather_multi(send, name="gather_weights")
    wa_full = join_cols(ga, name="join_ffn").reshape(len(GROUP_A), DEPTH, D_MODEL, N_DEV * f_ff)
    w = {n: wa_full[i] for i, n in enumerate(GROUP_A)}
    for n in GROUP_B:
        lyr, rws = shapes[n][0], shapes[n][1]
        blk = gb[:, off_b[n]:off_b[n] + rows_b[n]].reshape(N_DEV, lyr, rws, D_MODEL)
        w[n] = jnp.swapaxes(blk, 0, 1).reshape(lyr, N_DEV * rws, D_MODEL)
    w["ab_w_in"] = join_cols(gc, name="join_ab_in")
    c_in_main, c_in_gate = join_cols(gd, name="join_c_in", outs=[(0, 4 * C_WIDTH, 4 * C_WIDTH),
                                                                  (4 * C_WIDTH, 4 * C_WIDTH + 2 * C_HEADS, LANES)])
    w["ple_wp"] = _join(ge.reshape(N_DEV, DEPTH, D_PLE, LANES), 2)
    ws = _take(gf.reshape(N_DEV, -1), SMALL_NAMES, shapes)
    w.update({n: _join(ws[n], 2) for n in SMALL_NAMES})
    ln_g, ln_b = w["ln_g"], w["ln_b"]
    wa_d, wx_d = _dense_blocks(a["b_wa"][0]), _dense_blocks(a["b_wx"][0])
    lru_w = (w["b_conv_w"][0], a["b_conv_b"][0], wa_d, a["b_ba"][0], wx_d, a["b_bx"][0], a["b_lam"][0])
    gdc_w = (a["c_a_log"][0], a["c_dt_bias"][0], a["c_norm_g"][0])

    h = x2
    saved = []
    for i in range(DEPTH):
        s = {"x0": h}
        s["y1"], s["z1"], s["hg1"], s["hu1"] = ffn_fwd(h, w["ffn1_wg"][i], w["ffn1_wu"][i], w["ffn1_wd"][i], ln_g[i, 0], ln_b[i, 0],
                                   name=f"ffn1_fwd_{i}")
        if i == 0:
            s["proj"] = matmul(s["y1"], w["ab_w_in"], mode="nn", name="ab_in_fwd", tn=896, tk=1024)
            ya = attn_fwd(s["proj"], a["a_sinks"][0], bsz, name="attn_fwd")
            yb = lru_fwd(s["proj"], *lru_w, bsz, name="lru_fwd")
            s["mix"] = jnp.concatenate([ya, yb], axis=1)
            w_out = w["ab_w_out"][0]
        else:
            s["proj"] = matmul(s["y1"], c_in_main, mode="nn", name="c_in_fwd", tm=1024, tn=2048, tk=1024)
            s["gates"] = matmul(s["y1"], c_in_gate, mode="nn", name="c_gate_fwd", tk=1024)
            s["qkv"] = gdc_pre_fwd(s["proj"], w["c_conv_w"][0], bsz, name="gdc_pre_fwd")
            s["mix"], s["o_pre"], s["vnew"], s["tmat"], s["states"] = gdc_fwd(
                s["qkv"], s["proj"], s["gates"], *gdc_w, bsz, name="gdc_fwd")
            w_out = w["c_w_out"][0]
        s["y2"], s["z2"] = mm_ln_fwd(s["mix"], w_out, s["y1"], ln_g[i, 1], ln_b[i, 1], name=f"mix_out_fwd_{i}")
        s["y3"], s["z3"], s["hg2"], s["hu2"] = ffn_fwd(s["y2"], w["ffn2_wg"][i], w["ffn2_wu"][i], w["ffn2_wd"][i], ln_g[i, 2], ln_b[i, 2],
                                   name=f"ffn2_fwd_{i}")
        h = ple_fwd(s["y3"], p2[i], w["ple_wg"][i], a["ple_bg"][i], w["ple_wp"][i], name=f"ple_fwd_{i}")
        saved.append(s)
    loss_part, dh = loss_fwd_bwd(h, tgt, name="loss")

    g = {n: [None] * shapes[n][0] for n in ("ffn1_wg", "ffn1_wu", "ffn1_wd", "ffn2_wg", "ffn2_wu", "ffn2_wd", "ln_g",
                                             "ln_b", "ple_wg", "ple_bg", "ple_wp")}
    wide = dict(tm=1024, tn=1408, tk=1024)
    tall = dict(tm=1408, tn=1024, tk=1024)
    for i in reversed(range(DEPTH)):
        s = saved[i]
        dy3, dt, de, dbg = ple_bwd(dh, s["y3"], p2[i], w["ple_wg"][i], a["ple_bg"][i], w["ple_wp"][i], name=f"ple_bwd_{i}")
        g["ple_wg"][i] = matmul(s["y3"], dt, mode="tn", name=f"ple_wg_grad_{i}", tm=1024, tn=1024)
        g["ple_wp"][i] = matmul(p2[i], de, mode="tn", name=f"ple_wp_grad_{i}", tn=1024)
        g["ple_bg"][i] = dbg[0]
        dz3, dg2, db2 = ln_bwd(dy3, s["z3"], ln_g[i, 2], name=f"ln2_bwd_{i}")
        dy2, act, dhg, dhu = ffn_bwd(dz3, s["hg2"], s["hu2"], w["ffn2_wg"][i], w["ffn2_wu"][i], w["ffn2_wd"][i], name=f"ffn2_bwd_{i}", tm=FFN_TM, tf=FFN_TF)
        g["ffn2_wg"][i] = matmul(s["y2"], dhg, mode="tn", name=f"ffn2_wg_grad_{i}", **wide)
        g["ffn2_wu"][i] = matmul(s["y2"], dhu, mode="tn", name=f"ffn2_wu_grad_{i}", **wide)
        g["ffn2_wd"][i] = matmul(act, dz3, mode="tn", scale=0.5, name=f"ffn2_wd_grad_{i}", **tall)
        dz2, dg1, db1 = ln_bwd(dy2, s["z2"], ln_g[i, 1], name=f"ln1_bwd_{i}")
        if i == 0:
            dmix = matmul(dz2, w["ab_w_out"][0], mode="nt", name="ab_out_bwd", tn=1024, tk=1024)
            g["ab_w_out"] = matmul(s["mix"], dz2, mode="tn", name="ab_out_grad", tm=1024, tn=1024)
            dq, dk, dv, dsk = attn_bwd(s["proj"], a["a_sinks"][0], dmix, bsz, name="attn_bwd")
            dbx, dbgate, dcw, dcb, dwa, dba, dwx, dbxb, dlam = lru_bwd(s["proj"], *lru_w, dmix, bsz, name="lru_bwd")
            dproj = jnp.concatenate([dq, dk, dv, dbx, dbgate], axis=1).astype(MM)
            dy1 = matmul(dproj, w["ab_w_in"], mode="nt", add=dz2, add_scale=DN_ALPHA, name="ab_in_bwd", tn=1024, tk=1792)
            g_ab_in = matmul(s["y1"], dproj, mode="tn", name="ab_in_grad", tm=1024, tn=896)
        else:
            dmix = matmul(dz2, w["c_w_out"][0], mode="nt", name="c_out_bwd", tn=1024, tk=1024)
            g["c_w_out"] = matmul(s["mix"], dz2, mode="tn", name="c_out_grad", tm=1024, tn=1024)
            dqkv, dzc, dgates, dsm = gdc_bwd(s["qkv"], s["proj"], s["gates"], *gdc_w, s["o_pre"], s["vnew"], s["tmat"],
                                             s["states"], dmix, bsz, name="gdc_bwd")
            draw, dccw = gdc_pre_bwd(s["proj"], w["c_conv_w"][0], dqkv, bsz, name="gdc_pre_bwd")
            dproj = jnp.concatenate([draw, dzc], axis=1).astype(MM)
            dgb = dgates.astype(MM)
            dy1 = matmul(dproj, c_in_main, mode="nt", add=dz2, add_scale=DN_ALPHA, name="c_in_bwd", tn=1024, tk=4096)
            dy1 = matmul(dgb, c_in_gate, mode="nt", add=dy1, name="c_gate_bwd", tn=1024)
            g_c_main = matmul(s["y1"], dproj, mode="tn", name="c_in_grad", tm=1024, tn=1024, tk=1024)
            g_c_gate = matmul(s["y1"], dgb, mode="tn", name="c_gate_grad", tm=1024)
        dz1, dg0, db0 = ln_bwd(dy1, s["z1"], ln_g[i, 0], name=f"ln0_bwd_{i}")
        dh, act, dhg, dhu = ffn_bwd(dz1, s["hg1"], s["hu1"], w["ffn1_wg"][i], w["ffn1_wu"][i], w["ffn1_wd"][i], name=f"ffn1_bwd_{i}", tm=FFN_TM, tf=FFN_TF)
        g["ffn1_wg"][i] = matmul(s["x0"], dhg, mode="tn", name=f"ffn1_wg_grad_{i}", **wide)
        g["ffn1_wu"][i] = matmul(s["x0"], dhu, mode="tn", name=f"ffn1_wu_grad_{i}", **wide)
        g["ffn1_wd"][i] = matmul(act, dz1, mode="tn", scale=0.5, name=f"ffn1_wd_grad_{i}", **tall)
        g["ln_g"][i] = jnp.concatenate([dg0, dg1, dg2], axis=0)
        g["ln_b"][i] = jnp.concatenate([db0, db1, db2], axis=0)
    grad_x = dh.reshape(bsz, s_len, d)
    full = {n: jnp.stack(v) if isinstance(v, list) else v[None] for n, v in g.items()}
    full["b_conv_w"] = dcw[None]
    full["c_conv_w"] = jnp.sum(dccw, axis=0)[None]
    dsm_sum = jnp.sum(dsm, axis=0)
    full.update(a_sinks=jnp.sum(dsk, axis=0)[:, :A_HEADS], b_conv_b=dcb, b_wa=_diag_blocks(dwa)[None], b_ba=dba,
                b_wx=_diag_blocks(dwx)[None], b_bx=dbxb, b_lam=dlam, c_a_log=dsm_sum[None, :C_HEADS, 0],
                c_dt_bias=dsm_sum[None, :C_HEADS, 1], c_norm_g=jnp.sum(dsm_sum[C_HEADS:], axis=0)[None])

    small_f32_rows = SMALL_F32 // LANES
    repl_flat = _flat_pad([full[n] for n in REPL], F32, SMALL_F32 - n_small)
    small8 = jnp.concatenate([_flat8_pad([_split(full[n], 2) for n in SMALL_NAMES], F32, n_small),
                              jnp.broadcast_to(repl_flat, (N_DEV,) + repl_flat.shape)], axis=1)
    parts = [
        split_cols([(jnp.concatenate([full[n].reshape(-1, N_DEV * f_ff) for n in GROUP_A], axis=0), N_DEV * f_ff)], f_ff,
                   name="split_ffn"),
        jnp.concatenate([_split(full[n], 1).astype(MM).reshape(N_DEV, -1, D_MODEL) for n in GROUP_B], axis=1),
        split_cols([(g_ab_in, AB_PROJ)], AB_PROJ // N_DEV, name="split_ab_in"),
        split_cols([(g_c_main, 4 * C_WIDTH), (g_c_gate, 2 * C_HEADS)], (4 * C_WIDTH + 2 * C_HEADS) // N_DEV,
                   name="split_c_in"),
        _split(full["ple_wp"], 2).astype(MM).reshape(N_DEV, PLE_WP_ROWS, LANES),
        small8.reshape(N_DEV, small_f32_rows, LANES),
    ]
    ra, rb, rc, rd, re, small_parts = exchange_multi(parts, name="exchange_grads")

    def wmv(n, shape2d):
        return [a[pre + n].reshape(shape2d) for pre in ("", "m_", "v_")]

    res = {}
    for i, n in enumerate(GROUP_A):
        res[n] = adamw_rows(ra, i * DEPTH * D_MODEL, *wmv(n, (DEPTH * D_MODEL, f_ff)), name=f"adamw_{n}")
    for n in GROUP_B:
        res[n] = adamw_rows(rb, off_b[n], *wmv(n, (rows_b[n], D_MODEL)), name=f"adamw_{n}", tr=64)
    res["ab_w_in"] = adamw_rows(rc, 0, *wmv("ab_w_in", (D_MODEL, AB_PROJ // N_DEV)), name="adamw_ab_w_in")
    res["c_w_in"] = adamw_rows(rd, 0, *wmv("c_w_in", (D_MODEL, shapes["c_w_in"][2])), name="adamw_c_w_in")
    res["ple_wp"] = adamw_rows(re, 0, *wmv("ple_wp", (PLE_WP_ROWS, LANES)), name="adamw_ple_wp")
    res_small = adamw_rows(small_parts, 0, *[_flat_pad([a[pre + n] for n in small_all], F32, SMALL_F32).reshape(
        small_f32_rows, LANES) for pre in ("", "m_", "v_")], name="adamw_small", tr=576)
    kinds = []
    for k in range(4):
        kd = {n: res[n][k].reshape(shapes[n]) for n in res}
        kd.update(_take(res_small[k].reshape(-1), small_all, shapes))
        kinds.append(kd)
    loss = lax.psum(loss_part[0, 0], ("x", "y", "c"))
    return (loss, grad_x, *[kinds[0][n] for n in WEIGHTS], *[kinds[1][n] for n in WEIGHTS],
            *[kinds[2][n] for n in WEIGHTS], *[kinds[3][n] for n in WEIGHTS])


BIG_ROWS = 5632
SMALL_F32 = 73728


def _flat_pad(arrs, dtype, total):
    flat = jnp.concatenate([z.astype(dtype).reshape(-1) for z in arrs])
    return jnp.pad(flat, (0, total - flat.shape[0]))


def _flat8_pad(arrs, dtype, total):
    flat = jnp.concatenate([z.astype(dtype).reshape(N_DEV, -1) for z in arrs], axis=1)
    return jnp.pad(flat, ((0, 0), (0, total - flat.shape[1])))


def _bits(z):
    return z if MM == F32 else _as_bf16_bits(z)


def _unbits(z):
    return z if MM == F32 else _from_bf16_bits(z)


def _take(flat, names, shapes):
    out, off = {}, 0
    for n in names:
        sz = int(np.prod(shapes[n]))
        out[n] = flat[..., off:off + sz].reshape(flat.shape[:-1] + tuple(shapes[n]))
        off += sz
    return out


def _step(a):
    x, p = a["x"], a["p"]
    bsz, s_len, d = x.shape
    t = bsz * s_len
    x2 = x.reshape(t, d)
    tgt = a["loss_target"].reshape(t, d)
    p2 = p.reshape(DEPTH, t, D_PLE)
    shapes = {n: a[n].shape for n in WEIGHTS}
    big_names = [n for n, _ in BIG]
    small_names = [n for n, _ in SMALL]
    n_small = sum(int(np.prod(shapes[n])) for n in small_names)
    bits_per = 1 if MM == F32 else 2
    small_rows = -(-(n_small * bits_per) // PACK_ALIGN) * (PACK_ALIGN // PACK_COLS)

    send = jnp.concatenate([
        _flat_pad([a[n] for n in big_names], MM, BIG_ROWS * PACK_COLS).reshape(BIG_ROWS, PACK_COLS),
        _bits(_flat_pad([a[n] for n in small_names], F32, small_rows * PACK_COLS // bits_per)).reshape(small_rows, PACK_COLS),
    ], axis=0)
    gathered = all_gather(send, name="gather_weights")
    wb = _take(gathered[:, :BIG_ROWS].reshape(N_DEV, -1), big_names, shapes)
    ws = _take(_unbits(gathered[:, BIG_ROWS:].reshape(N_DEV, -1)), small_names, shapes)
    w = {n: _join(wb[n], ax) for n, ax in BIG}
    w.update({n: _join(ws[n], ax) for n, ax in SMALL})
    ln_g, ln_b = w["ln_g"], w["ln_b"]
    c_in_main = w["c_w_in"][0][:, :4 * C_WIDTH]
    c_in_gate = jnp.pad(w["c_w_in"][0][:, 4 * C_WIDTH:], ((0, 0), (0, 128 - 2 * C_HEADS)))
    wa_d, wx_d = _dense_blocks(a["b_wa"][0]), _dense_blocks(a["b_wx"][0])
    lru_w = (w["b_conv_w"][0], a["b_conv_b"][0], wa_d, a["b_ba"][0], wx_d, a["b_bx"][0], a["b_lam"][0])
    gdc_w = (a["c_a_log"][0], a["c_dt_bias"][0], a["c_norm_g"][0])

    h = x2
    saved = []
    for i in range(DEPTH):
        s = {"x0": h}
        s["y1"], s["z1"], s["hg1"], s["hu1"] = ffn_fwd(h, w["ffn1_wg"][i], w["ffn1_wu"][i], w["ffn1_wd"][i], ln_g[i, 0], ln_b[i, 0],
                                   name=f"ffn1_fwd_{i}")
        if i == 0:
            s["proj"] = matmul(s["y1"], w["ab_w_in"][0], mode="nn", name="ab_in_fwd")
            ya = attn_fwd(s["proj"], a["a_sinks"][0], bsz, name="attn_fwd")
            yb = lru_fwd(s["proj"], *lru_w, bsz, name="lru_fwd")
            s["mix"] = jnp.concatenate([ya, yb], axis=1)
            w_out = w["ab_w_out"][0]
        else:
            s["proj"] = matmul(s["y1"], c_in_main, mode="nn", name="c_in_fwd")
            s["gates"] = matmul(s["y1"], c_in_gate, mode="nn", name="c_gate_fwd")
            s["qkv"] = gdc_pre_fwd(s["proj"], w["c_conv_w"][0], bsz, name="gdc_pre_fwd")
            s["mix"], s["o_pre"], s["vnew"], s["tmat"], s["states"] = gdc_fwd(
                s["qkv"], s["proj"], s["gates"], *gdc_w, bsz, name="gdc_fwd")
            w_out = w["c_w_out"][0]
        s["y2"], s["z2"] = mm_ln_fwd(s["mix"], w_out, s["y1"], ln_g[i, 1], ln_b[i, 1], name=f"mix_out_fwd_{i}")
        s["y3"], s["z3"], s["hg2"], s["hu2"] = ffn_fwd(s["y2"], w["ffn2_wg"][i], w["ffn2_wu"][i], w["ffn2_wd"][i], ln_g[i, 2], ln_b[i, 2],
                                   name=f"ffn2_fwd_{i}")
        h = ple_fwd(s["y3"], p2[i], w["ple_wg"][i], a["ple_bg"][i], w["ple_wp"][i], name=f"ple_fwd_{i}")
        saved.append(s)
    loss_part, dh = loss_fwd_bwd(h, tgt, name="loss")

    g = {n: [None] * shapes[n][0] for n in ("ffn1_wg", "ffn1_wu", "ffn1_wd", "ffn2_wg", "ffn2_wu", "ffn2_wd", "ln_g",
                                             "ln_b", "ple_wg", "ple_bg", "ple_wp")}
    wide = dict(tm=1024, tn=1408, tk=1024)
    tall = dict(tm=1408, tn=1024, tk=1024)
    for i in reversed(range(DEPTH)):
        s = saved[i]
        dy3, dt, de, dbg = ple_bwd(dh, s["y3"], p2[i], w["ple_wg"][i], a["ple_bg"][i], w["ple_wp"][i], name=f"ple_bwd_{i}")
        g["ple_wg"][i] = matmul(s["y3"], dt, mode="tn", name=f"ple_wg_grad_{i}")
        g["ple_wp"][i] = matmul(p2[i], de, mode="tn", name=f"ple_wp_grad_{i}")
        g["ple_bg"][i] = dbg[0]
        dz3, dg2, db2 = ln_bwd(dy3, s["z3"], ln_g[i, 2], name=f"ln2_bwd_{i}")
        dy2, act, dhg, dhu = ffn_bwd(dz3, s["hg2"], s["hu2"], w["ffn2_wg"][i], w["ffn2_wu"][i], w["ffn2_wd"][i], name=f"ffn2_bwd_{i}", tm=FFN_TM, tf=FFN_TF)
        g["ffn2_wg"][i] = matmul(s["y2"], dhg, mode="tn", name=f"ffn2_wg_grad_{i}", **wide)
        g["ffn2_wu"][i] = matmul(s["y2"], dhu, mode="tn", name=f"ffn2_wu_grad_{i}", **wide)
        g["ffn2_wd"][i] = matmul(act, dz3, mode="tn", scale=0.5, name=f"ffn2_wd_grad_{i}", **tall)
        dz2, dg1, db1 = ln_bwd(dy2, s["z2"], ln_g[i, 1], name=f"ln1_bwd_{i}")
        if i == 0:
            dmix = matmul(dz2, w["ab_w_out"][0], mode="nt", name="ab_out_bwd")
            g["ab_w_out"] = matmul(s["mix"], dz2, mode="tn", name="ab_out_grad")
            dq, dk, dv, dsk = attn_bwd(s["proj"], a["a_sinks"][0], dmix, bsz, name="attn_bwd")
            dbx, dbgate, dcw, dcb, dwa, dba, dwx, dbxb, dlam = lru_bwd(s["proj"], *lru_w, dmix, bsz, name="lru_bwd")
            dproj = jnp.concatenate([dq, dk, dv, dbx, dbgate], axis=1).astype(MM)
            dy1 = matmul(dproj, w["ab_w_in"][0], mode="nt", add=dz2, add_scale=DN_ALPHA, name="ab_in_bwd")
            g["ab_w_in"] = matmul(s["y1"], dproj, mode="tn", name="ab_in_grad")
        else:
            dmix = matmul(dz2, w["c_w_out"][0], mode="nt", name="c_out_bwd")
            g["c_w_out"] = matmul(s["mix"], dz2, mode="tn", name="c_out_grad")
            dqkv, dzc, dgates, dsm = gdc_bwd(s["qkv"], s["proj"], s["gates"], *gdc_w, s["o_pre"], s["vnew"], s["tmat"],
                                             s["states"], dmix, bsz, name="gdc_bwd")
            draw, dccw = gdc_pre_bwd(s["proj"], w["c_conv_w"][0], dqkv, bsz, name="gdc_pre_bwd")
            dproj = jnp.concatenate([draw, dzc], axis=1).astype(MM)
            dgb = dgates.astype(MM)
            dy1 = matmul(dproj, c_in_main, mode="nt", add=dz2, add_scale=DN_ALPHA, name="c_in_bwd")
            dy1 = matmul(dgb, c_in_gate, mode="nt", add=dy1, name="c_gate_bwd")
            g["c_w_in"] = jnp.concatenate([matmul(s["y1"], dproj, mode="tn", name="c_in_grad"),
                                           matmul(s["y1"], dgb, mode="tn", name="c_gate_grad")[:, :2 * C_HEADS]], axis=1)
        dz1, dg0, db0 = ln_bwd(dy1, s["z1"], ln_g[i, 0], name=f"ln0_bwd_{i}")
        dh, act, dhg, dhu = ffn_bwd(dz1, s["hg1"], s["hu1"], w["ffn1_wg"][i], w["ffn1_wu"][i], w["ffn1_wd"][i], name=f"ffn1_bwd_{i}", tm=FFN_TM, tf=FFN_TF)
        g["ffn1_wg"][i] = matmul(s["x0"], dhg, mode="tn", name=f"ffn1_wg_grad_{i}", **wide)
        g["ffn1_wu"][i] = matmul(s["x0"], dhu, mode="tn", name=f"ffn1_wu_grad_{i}", **wide)
        g["ffn1_wd"][i] = matmul(act, dz1, mode="tn", scale=0.5, name=f"ffn1_wd_grad_{i}", **tall)
        g["ln_g"][i] = jnp.concatenate([dg0, dg1, dg2], axis=0)
        g["ln_b"][i] = jnp.concatenate([db0, db1, db2], axis=0)
    grad_x = dh.reshape(bsz, s_len, d)
    full = {n: jnp.stack(v) if isinstance(v, list) else v[None] for n, v in g.items()}
    full["b_conv_w"] = dcw[None]
    full["c_conv_w"] = jnp.sum(dccw, axis=0)[None]
    dsm_sum = jnp.sum(dsm, axis=0)
    full.update(a_sinks=jnp.sum(dsk, axis=0)[:, :A_HEADS], b_conv_b=dcb, b_wa=_diag_blocks(dwa)[None], b_ba=dba,
                b_wx=_diag_blocks(dwx)[None], b_bx=dbxb, b_lam=dlam, c_a_log=dsm_sum[None, :C_HEADS, 0],
                c_dt_bias=dsm_sum[None, :C_HEADS, 1], c_norm_g=jnp.sum(dsm_sum[C_HEADS:], axis=0)[None])

    small_cols = SMALL_F32 * bits_per // PACK_COLS
    repl_flat = _flat_pad([full[n] for n in REPL], F32, SMALL_F32 - n_small)
    small8 = jnp.concatenate([_flat8_pad([_split(full[n], ax) for n, ax in SMALL], F32, n_small),
                              jnp.broadcast_to(repl_flat, (N_DEV,) + repl_flat.shape)], axis=1)
    parts = jnp.concatenate([
        _flat8_pad([_split(full[n], ax) for n, ax in BIG], MM, BIG_ROWS * PACK_COLS).reshape(N_DEV, BIG_ROWS, PACK_COLS),
        _bits(small8).reshape(N_DEV, small_cols, PACK_COLS)], axis=1)
    recv = all_to_all(parts, name="exchange_grads")

    def mine(prefix, names, dtype_total):
        return _flat_pad([a[prefix + n] for n in names], F32, dtype_total)

    outs = {}
    big_total = BIG_ROWS * PACK_COLS
    res_big = adamw_sum(recv, *[mine(pre, big_names, big_total).reshape(BIG_ROWS, PACK_COLS) for pre in ("", "m_", "v_")],
                        name="adamw_big")
    small_all = small_names + REPL
    cols_f32 = PACK_COLS // bits_per
    res_small = adamw_sum(_unbits(recv[:, BIG_ROWS:]).reshape(N_DEV, small_cols, cols_f32),
                          *[mine(pre, small_all, SMALL_F32).reshape(small_cols, cols_f32) for pre in ("", "m_", "v_")],
                          name="adamw_small")
    kinds = []
    for rb, rs in zip(res_big, res_small):
        k = _take(rb.reshape(-1), big_names, shapes)
        k.update(_take(rs.reshape(-1), small_all, shapes))
        kinds.append(k)
    loss = lax.psum(loss_part[0, 0], ("x", "y", "c"))
    return (loss, grad_x, *[kinds[0][n] for n in WEIGHTS], *[kinds[1][n] for n in WEIGHTS],
            *[kinds[2][n] for n in WEIGHTS], *[kinds[3][n] for n in WEIGHTS])
```

```python
import functools
import math

import numpy as np
import jax
import jax.numpy as jnp
from jax import lax
from jax.experimental import pallas as pl
from jax.experimental.pallas import tpu as pltpu

F32 = jnp.float32
MM = jnp.bfloat16
HI = lax.Precision.HIGHEST

D_MODEL = 1024
D_FF = 2816
D_PLE = 256
DEPTH = 2
CHUNK = 64
A_HEADS = 8
A_KV_HEADS = 2
A_GROUP = 4
A_HEAD_DIM = 64
A_WIDTH = 512
A_KV_WIDTH = 128
B_WIDTH = 512
B_BLOCK = 64
RG_C = 8.0
AB_PROJ = 1792
C_HEADS = 8
C_HEAD_DIM = 128
C_WIDTH = 1024
DN_ALPHA = (2.0 * DEPTH) ** 0.25
LN_EPS = 1e-5
NORM_EPS = 1e-6
NEG = -1e30
ADAM_LR = 0.001
ADAM_B1 = 0.9
ADAM_B2 = 0.999
ADAM_EPS = 1e-08
ADAM_WD = 0.01
ADAM_STEP = 10
N_DEV = 8
VMEM_LIMIT = 56 * 1024 * 1024

NN = ((1,), (0,))
NT = ((1,), (1,))
TN = ((0,), (0,))


def _pcall(body, **kw):
    return pl.pallas_call(body, **kw)


def _cp(*sem):
    return pltpu.CompilerParams(dimension_semantics=sem, vmem_limit_bytes=VMEM_LIMIT)


MESH_ID = pl.DeviceIdType.MESH
_FLIPS = [(0, 0, 1), (1, 0, 0), (0, 1, 0), (1, 1, 0), (1, 0, 1), (0, 1, 1), (1, 1, 1)]


def _me():
    return lax.axis_index("x"), lax.axis_index("y"), lax.axis_index("c")


def _flip(coord, d):
    return 1 - coord if d else coord


def _side_copies(kind, x_refs, o_refs, send_sems, recv_sems, local_sems, start):
    x, y, c = _me()
    mine = 4 * x + 2 * y + c
    for gi, (x_ref, o_ref) in enumerate(zip(x_refs, o_refs)):
        src_own = x_ref if kind == "gather" else x_ref.at[mine]
        own = pltpu.make_async_copy(src_own, o_ref.at[mine], local_sems.at[gi])
        own.start() if start else own.wait()
        for k, (dx, dy, dc) in enumerate(_FLIPS):
            px, py, pc = _flip(x, dx), _flip(y, dy), _flip(c, dc)
            src = x_ref if kind == "gather" else x_ref.at[4 * px + 2 * py + pc]
            cp = pltpu.make_async_remote_copy(
                src_ref=src, dst_ref=o_ref.at[mine], send_sem=send_sems.at[7 * gi + k], recv_sem=recv_sems.at[7 * gi + k],
                device_id=(px, py, pc), device_id_type=MESH_ID)
            cp.start() if start else cp.wait()


def _call(body, args, side, grid, **kw):
    if side is None:
        return _pcall(body, grid=grid, **kw)(*args)
    kind, arrs = side
    ns, n_in, n_out = len(arrs), len(args), len(kw["out_specs"])
    scratch = list(kw.get("scratch_shapes", []))
    n_scr = len(scratch)

    def edge(at_end):
        conds = [pl.program_id(ax) == (n - 1 if at_end else 0) for ax, n in enumerate(grid)]
        return functools.reduce(jnp.logical_and, conds)

    def wrapped(*refs):
        ins, sx = refs[:n_in], refs[n_in:n_in + ns]
        outs, so = refs[n_in + ns:n_in + ns + n_out], refs[n_in + ns + n_out:n_in + 2 * ns + n_out]
        rest = refs[n_in + 2 * ns + n_out:]
        scr, sems = rest[:n_scr], rest[n_scr:]

        @pl.when(edge(False))
        def _():
            _side_copies(kind, sx, so, *sems, start=True)

        body(*ins, *outs, *scr)

        @pl.when(edge(True))
        def _():
            _side_copies(kind, sx, so, *sems, start=False)

    hbm = pl.BlockSpec(memory_space=pl.ANY)
    side_shapes = [jax.ShapeDtypeStruct(((N_DEV,) if kind == "gather" else ()) + z.shape, z.dtype) for z in arrs]
    kw = dict(kw)
    kw["in_specs"] = list(kw["in_specs"]) + [hbm] * ns
    kw["out_specs"] = list(kw["out_specs"]) + [hbm] * ns
    kw["out_shape"] = list(kw["out_shape"]) + side_shapes
    kw["scratch_shapes"] = scratch + [pltpu.SemaphoreType.DMA((7 * ns,)), pltpu.SemaphoreType.DMA((7 * ns,)),
                                      pltpu.SemaphoreType.DMA((ns,))]
    kw["compiler_params"] = _cp(*["arbitrary"] * len(grid))
    res = _pcall(wrapped, grid=grid, **kw)(*args, *arrs)
    return list(res[:n_out]), list(res[n_out:])


def _dot(a, b, dims=NN, precision=None):
    return lax.dot_general(a, b, (dims, ((), ())), preferred_element_type=F32, precision=precision)


def _mdot(a, b, dims=NN):
    return _dot(a.astype(MM), b.astype(MM), dims)


def _tile(n, pref):
    if n <= pref:
        return n
    for c in range(pref - pref % 128, 0, -128):
        if n % c == 0:
            return c
    return n


def _sigmoid(x):
    return 1.0 / (1.0 + jnp.exp(-x))


def _softplus(x):
    return jnp.maximum(x, 0.0) + jnp.log(1.0 + jnp.exp(-jnp.abs(x)))


def _ln_stats(z):
    mu = jnp.mean(z, axis=-1, keepdims=True)
    zc = z - mu
    var = jnp.mean(zc * zc, axis=-1, keepdims=True)
    return zc, lax.rsqrt(var + LN_EPS)


def matmul(a, b, *, mode, name, tm=512, tn=512, tk=512, out_dtype=F32, scale=None, add=None, add_scale=1.0, side=None,
           split_n=None, ln=None):
    if mode == "nn":
        (m, kk), (_, n) = a.shape, b.shape
        dims = NN
    elif mode == "nt":
        (m, kk), (n, _) = a.shape, b.shape
        dims = NT
    else:
        (kk, m), (_, n) = a.shape, b.shape
        dims = TN
    tm, tn, tk = _tile(m, tm), _tile(n, tn), _tile(kk, tk)
    if mode == "nn":
        a_spec = pl.BlockSpec((tm, tk), lambda i, j, k: (i, k))
        b_spec = pl.BlockSpec((tk, tn), lambda i, j, k: (k, j))
    elif mode == "nt":
        a_spec = pl.BlockSpec((tm, tk), lambda i, j, k: (i, k))
        b_spec = pl.BlockSpec((tn, tk), lambda i, j, k: (j, k))
    else:
        a_spec = pl.BlockSpec((tk, tm), lambda i, j, k: (k, i))
        b_spec = pl.BlockSpec((tk, tn), lambda i, j, k: (k, j))
    nk = kk // tk
    o_spec = pl.BlockSpec((tm, tn), lambda i, j, k: (i, j))
    has_add = add is not None
    n_in = 2 + has_add + (2 if ln else 0)
    assert not (ln and split_n) and (not ln or tn == n) and (not split_n or tn % split_n == 0)

    def body(*refs):
        a_ref, b_ref = refs[:2]
        o_ref = refs[n_in]
        acc_ref = refs[-1]
        i, j, k = pl.program_id(0), pl.program_id(1), pl.program_id(2)

        @pl.when(k == 0)
        def _():
            acc_ref[...] = jnp.zeros_like(acc_ref)

        acc_ref[...] += _mdot(a_ref[...], b_ref[...], dims)
        if ln:
            z_ref, g_ref = refs[n_in - 2:n_in]
            dg_ref, db_ref = refs[n_in + 1:n_in + 3]

            @pl.when((i == 0) & (k == 0))
            def _():
                dg_ref[...] = jnp.zeros_like(dg_ref)
                db_ref[...] = jnp.zeros_like(db_ref)

        @pl.when(k == nk - 1)
        def _():
            r = acc_ref[...]
            if scale is not None:
                r = r * scale
            if has_add:
                r = r + add_scale * refs[2][...].astype(F32)
            if ln:
                r, dg, db = _ln_bwd_tile(r, z_ref[...], g_ref[...])
                dg_ref[...] += dg
                db_ref[...] += db
            if split_n:
                for q in range(tn // split_n):
                    o_ref[q] = r[:, q * split_n:(q + 1) * split_n].astype(out_dtype)
            else:
                o_ref[...] = r.astype(out_dtype)

    vec = pl.BlockSpec((1, n), lambda i, j, k: (0, 0))
    ins = [a, b] + ([add] if has_add else []) + ([ln[0], ln[1].reshape(1, n)] if ln else [])
    in_specs = [a_spec, b_spec] + ([o_spec] if has_add else []) + ([o_spec, vec] if ln else [])
    out_specs, out_shape = [o_spec], [jax.ShapeDtypeStruct((m, n), out_dtype)]
    if split_n:
        out_specs = [pl.BlockSpec((tn // split_n, tm, split_n), lambda i, j, k: (j, i, 0))]
        out_shape = [jax.ShapeDtypeStruct((n // split_n, m, split_n), out_dtype)]
    if ln:
        out_specs += [vec, vec]
        out_shape += [jax.ShapeDtypeStruct((1, n), F32)] * 2
    res = _call(
        body, ins, side, (m // tm, n // tn, nk), name=name, in_specs=in_specs, out_specs=out_specs, out_shape=out_shape,
        scratch_shapes=[pltpu.VMEM((tm, tn), F32)],
        compiler_params=_cp("arbitrary" if ln else "parallel", "parallel", "arbitrary"),
    )
    outs, extra = (res, None) if side is None else res
    outs = outs[0] if len(outs) == 1 else tuple(outs)
    return outs if side is None else (outs, extra)


def ffn_fwd(x, wg, wu, wd, g, b, *, name, tm=512, tf=256, side=None):
    t, d = x.shape
    f = wg.shape[1]
    tm = min(tm, t)
    nj = f // tf

    def body(x_ref, wg_ref, wu_ref, wd_ref, g_ref, b_ref, y_ref, z_ref, hg_ref, hu_ref, xb_ref, acc_ref):
        j = pl.program_id(1)

        @pl.when(j == 0)
        def _():
            xb_ref[...] = x_ref[...].astype(MM)
            acc_ref[...] = jnp.zeros_like(acc_ref)

        xb = xb_ref[...]
        hg = _dot(xb, wg_ref[...])
        hu = _dot(xb, wu_ref[...])
        hg_ref[...] = hg.astype(MM)
        hu_ref[...] = hu.astype(MM)
        act = (hg * _sigmoid(hg) * hu).astype(MM)
        acc_ref[...] += _dot(act, wd_ref[...])

        @pl.when(j == nj - 1)
        def _():
            z = DN_ALPHA * x_ref[...] + 0.5 * acc_ref[...]
            z_ref[...] = z
            zc, rstd = _ln_stats(z)
            y_ref[...] = zc * rstd * g_ref[...] + b_ref[...]

    row = pl.BlockSpec((tm, d), lambda i, j: (i, 0))
    hid = pl.BlockSpec((tm, tf), lambda i, j: (i, j))
    vec = pl.BlockSpec((1, d), lambda i, j: (0, 0))
    return _call(
        body, (x, wg, wu, wd, g.reshape(1, d), b.reshape(1, d)), side, (t // tm, nj), name=name,
        in_specs=[row, pl.BlockSpec((d, tf), lambda i, j: (0, j)), pl.BlockSpec((d, tf), lambda i, j: (0, j)),
                  pl.BlockSpec((tf, d), lambda i, j: (j, 0)), vec, vec],
        out_specs=[row, row, hid, hid],
        out_shape=[jax.ShapeDtypeStruct((t, d), F32), jax.ShapeDtypeStruct((t, d), F32),
                   jax.ShapeDtypeStruct((t, f), MM), jax.ShapeDtypeStruct((t, f), MM)],
        scratch_shapes=[pltpu.VMEM((tm, d), MM), pltpu.VMEM((tm, d), F32)],
        compiler_params=_cp("parallel", "arbitrary"),
    )


def _ln_bwd_tile(dy, z, g):
    zc, rstd = _ln_stats(z)
    xh = zc * rstd
    dxh = dy * g
    m1 = jnp.mean(dxh, axis=-1, keepdims=True)
    m2 = jnp.mean(dxh * xh, axis=-1, keepdims=True)
    return rstd * (dxh - m1 - xh * m2), jnp.sum(dy * xh, axis=0, keepdims=True), jnp.sum(dy, axis=0, keepdims=True)


def ffn_bwd(dz, hg, hu, wg, wu, wd, *, name, tm=512, tf=256, side=None, ln=None):
    t, d = dz.shape
    f = wg.shape[1]
    tm = min(tm, t)
    nj = f // tf
    n_in = 6 + (2 if ln else 0)

    def body(*refs):
        dz_ref, hg_ref, hu_ref, wg_ref, wu_ref, wd_ref = refs[:6]
        dx_ref, act_ref, dhg_ref, dhu_ref = refs[n_in:n_in + 4]
        dfb_ref, acc_ref = refs[-2:]
        i, j = pl.program_id(0), pl.program_id(1)

        @pl.when(j == 0)
        def _():
            dfb_ref[...] = (0.5 * dz_ref[...]).astype(MM)
            acc_ref[...] = jnp.zeros_like(acc_ref)

        hg = hg_ref[...].astype(F32)
        hu = hu_ref[...].astype(F32)
        s = _sigmoid(hg)
        dact = _dot(dfb_ref[...], wd_ref[...], NT)
        sg = hg * s
        act_ref[...] = (sg * hu).astype(MM)
        dhu = (dact * sg).astype(MM)
        dhg = (dact * hu * (s + sg * (1.0 - s))).astype(MM)
        dhu_ref[...] = dhu
        dhg_ref[...] = dhg
        acc_ref[...] += _dot(dhg, wg_ref[...], NT) + _dot(dhu, wu_ref[...], NT)

        if ln:
            z_ref, g_ref = refs[6:8]
            dg_ref, db_ref = refs[n_in + 4:n_in + 6]

            @pl.when((i == 0) & (j == 0))
            def _():
                dg_ref[...] = jnp.zeros_like(dg_ref)
                db_ref[...] = jnp.zeros_like(db_ref)

        @pl.when(j == nj - 1)
        def _():
            dx = DN_ALPHA * dz_ref[...] + acc_ref[...]
            if ln:
                dx, dg, db = _ln_bwd_tile(dx, z_ref[...], g_ref[...])
                dg_ref[...] += dg
                db_ref[...] += db
            dx_ref[...] = dx

    row = pl.BlockSpec((tm, d), lambda i, j: (i, 0))
    hid = pl.BlockSpec((tm, tf), lambda i, j: (i, j))
    vec = pl.BlockSpec((1, d), lambda i, j: (0, 0))
    vshape = jax.ShapeDtypeStruct((1, d), F32)
    return _call(
        body, (dz, hg, hu, wg, wu, wd) + ((ln[0], ln[1].reshape(1, d)) if ln else ()), side, (t // tm, nj), name=name,
        in_specs=[row, hid, hid, pl.BlockSpec((d, tf), lambda i, j: (0, j)), pl.BlockSpec((d, tf), lambda i, j: (0, j)),
                  pl.BlockSpec((tf, d), lambda i, j: (j, 0))] + ([row, vec] if ln else []),
        out_specs=[row, hid, hid, hid] + ([vec, vec] if ln else []),
        out_shape=[jax.ShapeDtypeStruct((t, d), F32)] + [jax.ShapeDtypeStruct((t, f), MM)] * 3 + ([vshape, vshape] if ln else []),
        scratch_shapes=[pltpu.VMEM((tm, d), MM), pltpu.VMEM((tm, d), F32)],
        compiler_params=_cp("arbitrary" if ln else "parallel", "arbitrary"),
    )


def ln_bwd(dy, z, g, *, name, tm=512):
    t, d = z.shape
    tm = min(tm, t)

    def body(dy_ref, z_ref, g_ref, dz_ref, dg_ref, db_ref):
        i = pl.program_id(0)

        @pl.when(i == 0)
        def _():
            dg_ref[...] = jnp.zeros_like(dg_ref)
            db_ref[...] = jnp.zeros_like(db_ref)

        dy = dy_ref[...]
        zc, rstd = _ln_stats(z_ref[...])
        xh = zc * rstd
        dg_ref[...] += jnp.sum(dy * xh, axis=0, keepdims=True)
        db_ref[...] += jnp.sum(dy, axis=0, keepdims=True)
        dxh = dy * g_ref[...]
        m1 = jnp.mean(dxh, axis=-1, keepdims=True)
        m2 = jnp.mean(dxh * xh, axis=-1, keepdims=True)
        dz_ref[...] = rstd * (dxh - m1 - xh * m2)

    row = pl.BlockSpec((tm, d), lambda i: (i, 0))
    vec = pl.BlockSpec((1, d), lambda i: (0, 0))
    return _pcall(
        body, name=name, grid=(t // tm,), in_specs=[row, row, vec], out_specs=[row, vec, vec],
        out_shape=[jax.ShapeDtypeStruct((t, d), F32), jax.ShapeDtypeStruct((1, d), F32), jax.ShapeDtypeStruct((1, d), F32)],
        compiler_params=_cp("arbitrary"),
    )(dy, z, g.reshape(1, d))


def mm_ln_fwd(a, w, res, g, b, *, name, tm=512):
    t, kk = a.shape
    d = w.shape[1]
    tm = min(tm, t)

    def body(a_ref, w_ref, res_ref, g_ref, b_ref, y_ref, z_ref):
        z = DN_ALPHA * res_ref[...] + _mdot(a_ref[...], w_ref[...])
        z_ref[...] = z
        zc, rstd = _ln_stats(z)
        y_ref[...] = zc * rstd * g_ref[...] + b_ref[...]

    row = pl.BlockSpec((tm, d), lambda i: (i, 0))
    vec = pl.BlockSpec((1, d), lambda i: (0, 0))
    return _pcall(
        body, name=name, grid=(t // tm,),
        in_specs=[pl.BlockSpec((tm, kk), lambda i: (i, 0)), pl.BlockSpec((kk, d), lambda i: (0, 0)), row, vec, vec],
        out_specs=[row, row],
        out_shape=[jax.ShapeDtypeStruct((t, d), F32), jax.ShapeDtypeStruct((t, d), F32)],
        compiler_params=_cp("parallel"),
    )(a, w, res, g.reshape(1, d), b.reshape(1, d))


def ple_fwd(y, p, wg, bg, wp, *, name, tm=512):
    t, d = y.shape
    dp = p.shape[1]
    tm = min(tm, t)

    def body(y_ref, p_ref, wg_ref, bg_ref, wp_ref, o_ref):
        yv = y_ref[...]
        gate = _sigmoid(_mdot(yv, wg_ref[...]) + bg_ref[...])
        o_ref[...] = yv + gate * _mdot(p_ref[...], wp_ref[...])

    row = pl.BlockSpec((tm, d), lambda i: (i, 0))
    return _pcall(
        body, name=name, grid=(t // tm,),
        in_specs=[row, pl.BlockSpec((tm, dp), lambda i: (i, 0)), pl.BlockSpec((d, d), lambda i: (0, 0)),
                  pl.BlockSpec((1, d), lambda i: (0, 0)), pl.BlockSpec((dp, d), lambda i: (0, 0))],
        out_specs=row, out_shape=jax.ShapeDtypeStruct((t, d), F32), compiler_params=_cp("parallel"),
    )(y, p, wg, bg.reshape(1, d), wp)


def ple_bwd(do, y, p, wg, bg, wp, z, g, *, name, tm=512):
    t, d = y.shape
    dp = p.shape[1]
    tm = min(tm, t)

    def body(do_ref, y_ref, p_ref, wg_ref, bg_ref, wp_ref, z_ref, g_ref, dz_ref, dt_ref, de_ref, dbg_ref, dg_ref, db_ref):
        i = pl.program_id(0)

        @pl.when(i == 0)
        def _():
            for ref in (dbg_ref, dg_ref, db_ref):
                ref[...] = jnp.zeros_like(ref)

        dov = do_ref[...]
        gate = _sigmoid(_mdot(y_ref[...], wg_ref[...]) + bg_ref[...])
        emb = _mdot(p_ref[...], wp_ref[...])
        dt = dov * emb * gate * (1.0 - gate)
        dbg_ref[...] += jnp.sum(dt, axis=0, keepdims=True)
        dtb = dt.astype(MM)
        dt_ref[...] = dtb
        de_ref[...] = (dov * gate).astype(MM)
        dz, dg, db = _ln_bwd_tile(dov + _dot(dtb, wg_ref[...], NT), z_ref[...], g_ref[...])
        dz_ref[...] = dz
        dg_ref[...] += dg
        db_ref[...] += db

    row = pl.BlockSpec((tm, d), lambda i: (i, 0))
    vec = pl.BlockSpec((1, d), lambda i: (0, 0))
    vshape = jax.ShapeDtypeStruct((1, d), F32)
    return _pcall(
        body, name=name, grid=(t // tm,),
        in_specs=[row, row, pl.BlockSpec((tm, dp), lambda i: (i, 0)), pl.BlockSpec((d, d), lambda i: (0, 0)),
                  vec, pl.BlockSpec((dp, d), lambda i: (0, 0)), row, vec],
        out_specs=[row, row, row, vec, vec, vec],
        out_shape=[jax.ShapeDtypeStruct((t, d), F32), jax.ShapeDtypeStruct((t, d), MM),
                   jax.ShapeDtypeStruct((t, d), MM), vshape, vshape, vshape],
        compiler_params=_cp("arbitrary"),
    )(do, y, p, wg, bg.reshape(1, d), wp, z, g.reshape(1, d))


def loss_fwd_bwd(y, tgt, *, name, tm=512):
    t, d = y.shape
    tm = min(tm, t)

    def body(y_ref, t_ref, l_ref, dy_ref):
        i = pl.program_id(0)

        @pl.when(i == 0)
        def _():
            l_ref[...] = jnp.zeros_like(l_ref)

        err = y_ref[...] - t_ref[...]
        dy_ref[...] = err * (1.0 / d)
        l_ref[...] += (0.5 / d) * jnp.sum(jnp.sum(err * err, axis=1, keepdims=True), axis=0, keepdims=True)

    row = pl.BlockSpec((tm, d), lambda i: (i, 0))
    return _pcall(
        body, name=name, grid=(t // tm,), in_specs=[row, row],
        out_specs=[pl.BlockSpec((1, 128), lambda i: (0, 0)), row],
        out_shape=[jax.ShapeDtypeStruct((1, 128), F32), jax.ShapeDtypeStruct((t, d), F32)],
        compiler_params=_cp("arbitrary"),
    )(y, tgt)


def _shift_dn(x, s, row):
    return x if s == 0 else jnp.where(row >= s, pltpu.roll(x, s, 0), 0.0)


def _shift_up(x, s, row):
    n = x.shape[0]
    return x if s == 0 else jnp.where(row < n - s, pltpu.roll(x, n - s, 0), 0.0)


def _conv_fwd(x, w, row):
    kk = w.shape[0]
    y = w[kk - 1:kk, :] * x
    for j in range(kk - 1):
        y = y + w[j:j + 1, :] * _shift_dn(x, kk - 1 - j, row)
    return y


def _conv_bwd(x, w, dy, row):
    kk = w.shape[0]
    dx = w[kk - 1:kk, :] * dy
    dws = []
    for j in range(kk - 1):
        dx = dx + w[j:j + 1, :] * _shift_up(dy, kk - 1 - j, row)
        dws.append(jnp.sum(dy * _shift_dn(x, kk - 1 - j, row), axis=0, keepdims=True))
    dws.append(jnp.sum(dy * x, axis=0, keepdims=True))
    return dx, jnp.concatenate(dws, axis=0)


def _gelu(x):
    c = math.sqrt(2.0 / math.pi)
    th = jnp.tanh(c * (x + 0.044715 * x * x * x))
    return 0.5 * x * (1.0 + th), th


def _gelu_grad(x, th):
    c = math.sqrt(2.0 / math.pi)
    return 0.5 * (1.0 + th) + 0.5 * x * (1.0 - th * th) * c * (1.0 + 3.0 * 0.044715 * x * x)


def _neg_expm1(y):
    ser = -(y * (1.0 + y * (0.5 + y * (1.0 / 6.0 + y * (1.0 / 24.0 + y * (1.0 / 120.0))))))
    return jnp.where(y > -0.05, ser, 1.0 - jnp.exp(y))


def _attn_head(qh, kk, vv, bias, valid, sink):
    s = _mdot(qh, kk, NT) * (A_HEAD_DIM ** -0.5) - bias
    s = jnp.where(valid, s, NEG)
    m = jnp.maximum(jnp.max(s, axis=-1, keepdims=True), sink)
    pr = jnp.exp(s - m)
    den = jnp.sum(pr, axis=-1, keepdims=True) + jnp.exp(sink - m)
    return pr / den, jnp.exp(sink - m) / den


def _attn_valid(n):
    ji = lax.broadcasted_iota(jnp.int32, (1, 3 * CHUNK), 1)
    return (n * CHUNK + ji - 2 * CHUNK) >= 0


def _attn_group_consts(kh, sk_ref):
    rows = A_GROUP * CHUNK
    ri = lax.broadcasted_iota(jnp.int32, (rows, 3 * CHUNK), 0)
    ji = lax.broadcasted_iota(jnp.int32, (rows, 3 * CHUNK), 1)
    dist = jnp.abs((ri & (CHUNK - 1)) + 2 * CHUNK - ji).astype(F32)
    rcol = lax.broadcasted_iota(jnp.int32, (rows, 1), 0)
    slope = jnp.zeros((rows, 1), F32)
    sink = jnp.zeros((rows, 1), F32)
    for gi in range(A_GROUP):
        h = kh * A_GROUP + gi
        inblk = (rcol >= gi * CHUNK) & (rcol < (gi + 1) * CHUNK)
        slope = jnp.where(inblk, 2.0 ** -(h + 1), slope)
        sink = jnp.where(inblk, sk_ref[h], sink)
    return slope * dist, sink


def _stack_heads(x, kh):
    return jnp.concatenate([x[:, (kh * A_GROUP + gi) * 64:(kh * A_GROUP + gi + 1) * 64] for gi in range(A_GROUP)], axis=0)


def attn_fwd(proj, sinks, bsz, *, name, side=None):
    t = proj.shape[0]
    s_len = t // bsz
    nc = s_len // CHUNK
    pad = 2 * CHUNK

    def body(q_ref, k_ref, v_ref, sk_ref, o_ref, kp_ref, vp_ref):
        kp_ref[0:pad, :] = jnp.zeros((pad, A_KV_WIDTH), F32)
        vp_ref[0:pad, :] = jnp.zeros((pad, A_KV_WIDTH), F32)
        kp_ref[pad:, :] = k_ref[...].astype(F32)
        vp_ref[pad:, :] = v_ref[...].astype(F32)

        consts = [_attn_group_consts(kh, sk_ref) for kh in range(A_KV_HEADS)]

        def chunk(n, carry):
            st = pl.multiple_of(n * CHUNK, CHUNK)
            q = q_ref[pl.ds(st, CHUNK), :].astype(F32)
            kb = kp_ref[pl.ds(st, 3 * CHUNK), :]
            vb = vp_ref[pl.ds(st, 3 * CHUNK), :]
            valid = _attn_valid(n)
            outs = []
            for kh in range(A_KV_HEADS):
                bias, sink = consts[kh]
                pn, _ = _attn_head(_stack_heads(q, kh), kb[:, kh * 64:(kh + 1) * 64], None, bias, valid, sink)
                o = _mdot(pn, vb[:, kh * 64:(kh + 1) * 64])
                outs += [o[gi * CHUNK:(gi + 1) * CHUNK] for gi in range(A_GROUP)]
            o_ref[pl.ds(st, CHUNK), :] = jnp.concatenate(outs, axis=-1)
            return carry

        lax.fori_loop(0, nc, chunk, 0, unroll=2)

    res = _call(
        body, (proj, proj, proj, sinks), side, (bsz,), name=name,
        in_specs=[pl.BlockSpec((s_len, A_WIDTH), lambda b: (b, 0)), pl.BlockSpec((s_len, 128), lambda b: (b, 4)),
                  pl.BlockSpec((s_len, 128), lambda b: (b, 5)), pl.BlockSpec(memory_space=pltpu.SMEM)],
        out_specs=[pl.BlockSpec((s_len, A_WIDTH), lambda b: (b, 0))],
        out_shape=[jax.ShapeDtypeStruct((t, A_WIDTH), F32)],
        scratch_shapes=[pltpu.VMEM((s_len + pad, A_KV_WIDTH), F32), pltpu.VMEM((s_len + pad, A_KV_WIDTH), F32)],
        compiler_params=_cp("parallel"),
    )
    return res[0] if side is None else (res[0][0], res[1])


def attn_bwd(proj, sinks, dcat, bsz, *, name, side=None):
    t = proj.shape[0]
    s_len = t // bsz
    nc = s_len // CHUNK
    pad = 2 * CHUNK

    def body(q_ref, k_ref, v_ref, do_ref, sk_ref, dq_ref, dk_ref, dv_ref, dsk_ref, kp_ref, vp_ref, dkp_ref, dvp_ref):
        kp_ref[0:pad, :] = jnp.zeros((pad, A_KV_WIDTH), F32)
        vp_ref[0:pad, :] = jnp.zeros((pad, A_KV_WIDTH), F32)
        kp_ref[pad:, :] = k_ref[...].astype(F32)
        vp_ref[pad:, :] = v_ref[...].astype(F32)
        dkp_ref[...] = jnp.zeros_like(dkp_ref)
        dvp_ref[...] = jnp.zeros_like(dvp_ref)
        lane = lax.broadcasted_iota(jnp.int32, (1, 128), 1)

        consts = [_attn_group_consts(kh, sk_ref) for kh in range(A_KV_HEADS)]

        def chunk(n, carry):
            dsk = jnp.zeros((1, 128), F32)
            st = pl.multiple_of(n * CHUNK, CHUNK)
            q = q_ref[pl.ds(st, CHUNK), :].astype(F32)
            do = do_ref[pl.ds(st, CHUNK), :]
            kb = kp_ref[pl.ds(st, 3 * CHUNK), :]
            vb = vp_ref[pl.ds(st, 3 * CHUNK), :]
            valid = _attn_valid(n)
            dqs, dks, dvs = [], [], []
            for kh in range(A_KV_HEADS):
                kk = kb[:, kh * 64:(kh + 1) * 64]
                vv = vb[:, kh * 64:(kh + 1) * 64]
                bias, sink = consts[kh]
                qs = _stack_heads(q, kh)
                dos = _stack_heads(do, kh)
                pn, psink = _attn_head(qs, kk, None, bias, valid, sink)
                dp = _mdot(dos, vv, NT)
                rowdot = jnp.sum(pn * dp, axis=-1, keepdims=True)
                ds = pn * (dp - rowdot)
                sink_part = psink * rowdot
                for gi in range(A_GROUP):
                    part = jnp.sum(sink_part[gi * CHUNK:(gi + 1) * CHUNK], axis=0, keepdims=True)
                    dsk = dsk + jnp.where(lane == kh * A_GROUP + gi, -part, 0.0)
                dq = _mdot(ds, kk) * (A_HEAD_DIM ** -0.5)
                dqs += [dq[gi * CHUNK:(gi + 1) * CHUNK] for gi in range(A_GROUP)]
                dks.append(_mdot(ds, qs, TN) * (A_HEAD_DIM ** -0.5))
                dvs.append(_mdot(pn, dos, TN))
            dq_ref[pl.ds(st, CHUNK), :] = jnp.concatenate(dqs, axis=-1)
            dkp_ref[pl.ds(st, 3 * CHUNK), :] += jnp.concatenate(dks, axis=-1)
            dvp_ref[pl.ds(st, 3 * CHUNK), :] += jnp.concatenate(dvs, axis=-1)
            dsk_ref[0] += dsk
            return carry

        dsk_ref[...] = jnp.zeros_like(dsk_ref)
        lax.fori_loop(0, nc, chunk, 0, unroll=2)
        dk_ref[...] = dkp_ref[pad:, :]
        dv_ref[...] = dvp_ref[pad:, :]

    kv = jax.ShapeDtypeStruct((t, A_KV_WIDTH), F32)
    return _call(
        body, (proj, proj, proj, dcat, sinks), side, (bsz,), name=name,
        in_specs=[pl.BlockSpec((s_len, A_WIDTH), lambda b: (b, 0)), pl.BlockSpec((s_len, 128), lambda b: (b, 4)),
                  pl.BlockSpec((s_len, 128), lambda b: (b, 5)), pl.BlockSpec((s_len, A_WIDTH), lambda b: (b, 0)),
                  pl.BlockSpec(memory_space=pltpu.SMEM)],
        out_specs=[pl.BlockSpec((s_len, A_WIDTH), lambda b: (b, 0)), pl.BlockSpec((s_len, 128), lambda b: (b, 0)),
                   pl.BlockSpec((s_len, 128), lambda b: (b, 0)), pl.BlockSpec((1, 1, 128), lambda b: (b, 0, 0))],
        out_shape=[jax.ShapeDtypeStruct((t, A_WIDTH), F32), kv, kv, jax.ShapeDtypeStruct((bsz, 1, 128), F32)],
        scratch_shapes=[pltpu.VMEM((s_len + pad, A_KV_WIDTH), F32)] * 4,
        compiler_params=_cp("parallel"),
    )


def _lru_gates(x, cw, cb, wa, ba, wx, bx, lam, row):
    xc = _conv_fwd(x, cw, row) + cb
    r = _sigmoid(_mdot(xc, wa) + ba)
    i = _sigmoid(_mdot(xc, wx) + bx)
    sp = _softplus(-lam)
    log_a = -RG_C * r * sp
    a = jnp.exp(log_a)
    mult = jnp.sqrt(_neg_expm1(2.0 * log_a))
    return xc, r, i, sp, a, mult


LRU_BLOCK = 128


def _lru_scan_refs(a_ref, u_ref, h_ref, reverse=False):
    nb = a_ref.shape[0] // LRU_BLOCK
    row = lax.broadcasted_iota(jnp.int32, (LRU_BLOCK, 128), 0)

    def block(i, carry):
        bi = nb - 1 - i if reverse else i
        rs = pl.ds(pl.multiple_of(bi * LRU_BLOCK, LRU_BLOCK), LRU_BLOCK)
        a, u = a_ref[rs, :], u_ref[rs, :]
        d = 1
        while d < LRU_BLOCK:
            keep = row < LRU_BLOCK - d if reverse else row >= d
            sh = LRU_BLOCK - d if reverse else d
            a_sh = jnp.where(keep, pltpu.roll(a, sh, 0), 1.0)
            u_sh = jnp.where(keep, pltpu.roll(u, sh, 0), 0.0)
            u = a * u_sh + u
            a = a * a_sh
            d *= 2
        h = u + a * carry
        h_ref[rs, :] = h
        return h[0:1, :] if reverse else h[LRU_BLOCK - 1:LRU_BLOCK, :]

    lax.fori_loop(0, nb, block, jnp.zeros((1, 128), F32))


def _lru_specs(s_len, order):
    def at(f):
        return lambda *g: f(*order(*g))
    return [pl.BlockSpec((s_len, 128), at(lambda b, cb: (b, 6 + cb))), pl.BlockSpec((s_len, 128), at(lambda b, cb: (b, 10 + cb))),
            pl.BlockSpec((4, 128), at(lambda b, cb: (0, cb))), pl.BlockSpec((1, 128), at(lambda b, cb: (0, cb))),
            pl.BlockSpec((1, 128, 128), at(lambda b, cb: (cb, 0, 0))), pl.BlockSpec((1, 128), at(lambda b, cb: (0, cb))),
            pl.BlockSpec((1, 128, 128), at(lambda b, cb: (cb, 0, 0))), pl.BlockSpec((1, 128), at(lambda b, cb: (0, cb))),
            pl.BlockSpec((1, 128), at(lambda b, cb: (0, cb)))]


def lru_fwd(proj, cw, cb, wa, ba, wx, bxb, lam, bsz, *, name):
    t = proj.shape[0]
    s_len = t // bsz

    def body(x_ref, g_ref, cw_ref, cb_ref, wa_ref, ba_ref, wx_ref, bx_ref, lam_ref, y_ref, a_s, u_s):
        row = lax.broadcasted_iota(jnp.int32, (s_len, 128), 0)
        xc, r, i, sp, a, mult = _lru_gates(x_ref[...].astype(F32), cw_ref[...], cb_ref[...], wa_ref[0], ba_ref[...],
                                           wx_ref[0], bx_ref[...], lam_ref[...], row)
        a_s[...] = a
        u_s[...] = mult * (i * xc)
        _lru_scan_refs(a_s, u_s, y_ref)
        y_ref[...] = y_ref[...] * _gelu(g_ref[...].astype(F32))[0]

    return _pcall(
        body, name=name, grid=(bsz, 4), in_specs=_lru_specs(s_len, lambda b, cb: (b, cb)),
        out_specs=pl.BlockSpec((s_len, 128), lambda b, cb: (b, cb)),
        out_shape=jax.ShapeDtypeStruct((t, B_WIDTH), F32), scratch_shapes=[pltpu.VMEM((s_len, 128), F32)] * 2,
        compiler_params=_cp("parallel", "parallel"),
    )(proj, proj, cw, cb.reshape(1, -1), wa, ba.reshape(1, -1), wx, bxb.reshape(1, -1), lam.reshape(1, -1))


def lru_bwd(proj, cw, cb, wa, ba, wx, bxb, lam, dcat, bsz, *, name, side=None):
    t = proj.shape[0]
    s_len = t // bsz

    def body(x_ref, g_ref, cw_ref, cb_ref, wa_ref, ba_ref, wx_ref, bx_ref, lam_ref, dy_ref,
             dx_ref, dg_ref, dcw_ref, dcb_ref, dwa_ref, dba_ref, dwx_ref, dbx_ref, dlam_ref, a_s, u_s, h_s, g_s):
        b = pl.program_id(1)
        row = lax.broadcasted_iota(jnp.int32, (s_len, 128), 0)
        x = x_ref[...].astype(F32)
        lam = lam_ref[...]
        xc, r, i, sp, a, mult = _lru_gates(x, cw_ref[...], cb_ref[...], wa_ref[0], ba_ref[...], wx_ref[0], bx_ref[...],
                                           lam, row)
        ixc = i * xc
        a_s[...] = a
        u_s[...] = mult * ixc
        _lru_scan_refs(a_s, u_s, h_s)
        h = h_s[...]
        gv = g_ref[...].astype(F32)
        gl, th = _gelu(gv)
        dy = dy_ref[...]
        dg_ref[...] = dy * h * _gelu_grad(gv, th)
        a_s[...] = _shift_up(a, 1, row)
        u_s[...] = dy * gl
        _lru_scan_refs(a_s, u_s, g_s, reverse=True)
        gr = g_s[...]
        da = gr * _shift_dn(h, 1, row)
        dmult = gr * ixc
        di = gr * mult * xc
        dxc = gr * mult * i
        dlog_a = da * a - dmult * (a * a) / mult
        dr = dlog_a * (-RG_C * sp)
        dlam = jnp.sum(dlog_a * r, axis=0, keepdims=True) * (RG_C * _sigmoid(-lam))
        dpa = dr * r * (1.0 - r)
        dpx = di * i * (1.0 - i)
        dxc = dxc + _mdot(dpa, wa_ref[0], NT) + _mdot(dpx, wx_ref[0], NT)
        dx, dcw = _conv_bwd(x, cw_ref[...], dxc, row)
        dx_ref[...] = dx

        @pl.when(b == 0)
        def _():
            for ref in (dcw_ref, dcb_ref, dwa_ref, dba_ref, dwx_ref, dbx_ref, dlam_ref):
                ref[...] = jnp.zeros_like(ref)

        dcw_ref[...] += dcw
        dcb_ref[...] += jnp.sum(dxc, axis=0, keepdims=True)
        dwa_ref[0] += _mdot(xc, dpa, TN)
        dwx_ref[0] += _mdot(xc, dpx, TN)
        dba_ref[...] += jnp.sum(dpa, axis=0, keepdims=True)
        dbx_ref[...] += jnp.sum(dpx, axis=0, keepdims=True)
        dlam_ref[...] += dlam

    order = lambda cb, b: (b, cb)
    act = pl.BlockSpec((s_len, 128), lambda cb, b: (b, cb))
    vec = pl.BlockSpec((1, 128), lambda cb, b: (0, cb))
    mat = pl.BlockSpec((1, 128, 128), lambda cb, b: (cb, 0, 0))
    vshape = jax.ShapeDtypeStruct((1, B_WIDTH), F32)
    mshape = jax.ShapeDtypeStruct((4, 128, 128), F32)
    return _call(
        body, (proj, proj, cw, cb.reshape(1, -1), wa, ba.reshape(1, -1), wx, bxb.reshape(1, -1), lam.reshape(1, -1), dcat),
        side, (4, bsz), name=name,
        in_specs=_lru_specs(s_len, order) + [pl.BlockSpec((s_len, 128), lambda cb, b: (b, 4 + cb))],
        out_specs=[act, act, pl.BlockSpec((4, 128), lambda cb, b: (0, cb)), vec, mat, vec, mat, vec, vec],
        out_shape=[jax.ShapeDtypeStruct((t, B_WIDTH), F32), jax.ShapeDtypeStruct((t, B_WIDTH), F32),
                   jax.ShapeDtypeStruct((4, B_WIDTH), F32), vshape, mshape, vshape, mshape, vshape, vshape],
        scratch_shapes=[pltpu.VMEM((s_len, 128), F32)] * 4,
        compiler_params=_cp("parallel", "arbitrary"),
    )


_BDIMS = {"nn": ((2,), (1,)), "nt": ((2,), (2,)), "tn": ((1,), (1,))}
C_QSCALE = C_HEAD_DIM ** -0.5


def _bmm(a, b, mode, exact=False):
    dims = (_BDIMS[mode], ((0,), (0,)))
    if exact:
        return lax.dot_general(a, b, dims, preferred_element_type=F32, precision=lax.Precision.HIGH)
    return lax.dot_general(a.astype(MM), b.astype(MM), dims, preferred_element_type=F32)


def _col(x, idx, lane):
    return jnp.broadcast_to(jnp.sum(jnp.where(lane == idx, x, 0.0), axis=-1, keepdims=True), x.shape)


def _seg_cumsum(g, row):
    pos = row & (CHUNK - 1)
    d = 1
    while d < CHUNK:
        g = g + jnp.where(pos >= d, pltpu.roll(g, d, 0), 0.0)
        d *= 2
    return g


def _seg_cumsum_rev(g, row):
    pos = row & (CHUNK - 1)
    n = g.shape[0]
    d = 1
    while d < CHUNK:
        g = g + jnp.where(pos < CHUNK - d, pltpu.roll(g, n - d, 0), 0.0)
        d *= 2
    return g


def _neumann_inverse(lmat):
    ii = lax.broadcasted_iota(jnp.int32, lmat.shape, 1)
    jj = lax.broadcasted_iota(jnp.int32, lmat.shape, 2)
    x = -lmat
    tm = jnp.where(ii == jj, 1.0, 0.0) + x
    pw = x
    for _ in range(5):
        pw = _bmm(pw, pw, "nn", exact=True)
        tm = tm + _bmm(tm, pw, "nn", exact=True)
    return tm


def gdc_pre_fwd(proj, cw, bsz, *, name):
    t = proj.shape[0]
    s_len = t // bsz

    def body(x_ref, w_ref, y_ref):
        row = lax.broadcasted_iota(jnp.int32, (s_len, 128), 0)
        c = _conv_fwd(x_ref[...].astype(F32), w_ref[...], row)
        xc = c * _sigmoid(c)
        rn = lax.rsqrt(jnp.sum(xc * xc, axis=-1, keepdims=True) + NORM_EPS)
        y_ref[...] = jnp.where(pl.program_id(1) < 2 * C_HEADS, xc * rn, xc)

    blk = pl.BlockSpec((s_len, 128), lambda b, j: (b, j))
    return _pcall(
        body, name=name, grid=(bsz, 3 * C_HEADS), in_specs=[blk, pl.BlockSpec((4, 128), lambda b, j: (0, j))],
        out_specs=blk, out_shape=jax.ShapeDtypeStruct((t, 3 * C_WIDTH), F32), compiler_params=_cp("parallel", "parallel"),
    )(proj, cw)


def gdc_pre_bwd(proj, cw, dy, dproj, bsz, *, name):
    t = proj.shape[0]
    s_len = t // bsz

    def body(x_ref, w_ref, dq_ref, dk_ref, dv_ref, _, dx_ref, dw_ref):
        row = lax.broadcasted_iota(jnp.int32, (s_len, 128), 0)
        x = x_ref[...].astype(F32)
        c = _conv_fwd(x, w_ref[...], row)
        sg = _sigmoid(c)
        xc = c * sg
        rn = lax.rsqrt(jnp.sum(xc * xc, axis=-1, keepdims=True) + NORM_EPS)
        part = pl.program_id(1) // C_HEADS
        dyv = jnp.where(part == 0, dq_ref[...], jnp.where(part == 1, dk_ref[...], dv_ref[...]))
        xn = xc * rn
        dxc = jnp.where(pl.program_id(1) < 2 * C_HEADS, rn * (dyv - xn * jnp.sum(dyv * xn, axis=-1, keepdims=True)), dyv)
        dc = dxc * (sg * (1.0 + c * (1.0 - sg)))
        dx, dw = _conv_bwd(x, w_ref[...], dc, row)
        dx_ref[...] = dx.astype(MM)
        dw_ref[0] = dw

    blk = pl.BlockSpec((s_len, 128), lambda b, j: (b, j))

    def dy_spec(part):
        return pl.BlockSpec((s_len, 128), lambda b, j: (b, jnp.clip(j - part * C_HEADS, 0, C_HEADS - 1)))

    return _pcall(
        body, name=name, grid=(bsz, 3 * C_HEADS),
        in_specs=[blk, pl.BlockSpec((4, 128), lambda b, j: (0, j)), dy_spec(0), dy_spec(1), dy_spec(2),
                  pl.BlockSpec(memory_space=pl.ANY)],
        out_specs=[blk, pl.BlockSpec((1, 4, 128), lambda b, j: (b, 0, j))],
        out_shape=[jax.ShapeDtypeStruct((t, 4 * C_WIDTH), MM), jax.ShapeDtypeStruct((bsz, 4, 3 * C_WIDTH), F32)],
        input_output_aliases={5: 0}, compiler_params=_cp("parallel", "parallel"),
    )(proj, cw, *dy, dproj)


GDC_GROUP = 16


def _gdc_local(qn, kn, vc, gates, a_log, dtb, h):
    rows = qn.shape[0]
    nc = rows // CHUNK
    row = lax.broadcasted_iota(jnp.int32, (rows, 128), 0)
    lane = lax.broadcasted_iota(jnp.int32, (rows, 128), 1)
    r = {"row": row, "lane": lane}
    r["beta"] = _sigmoid(_col(gates, h, lane))
    r["A"] = jnp.exp(a_log)
    r["pre"] = _col(gates, 8 + h, lane) + dtb
    r["sp"] = _softplus(r["pre"])
    gc = _seg_cumsum(-r["A"] * r["sp"], row)
    sh = (nc, CHUNK, 128)
    q3 = (qn * C_QSCALE).reshape(sh)
    k3 = kn.reshape(sh)
    v3 = vc.reshape(sh)
    beta3 = r["beta"].reshape(sh)
    gc3 = gc.reshape(sh)
    gcl3 = gc3[:, CHUNK - 1:CHUNK, :]
    eg = jnp.exp(gc3)
    ekd = jnp.exp(gcl3 - gc3)
    col64 = gc3[:, :, :CHUNK]
    row64 = jnp.swapaxes(gc3, 1, 2)[:, :CHUNK, :]
    ii = lax.broadcasted_iota(jnp.int32, (nc, CHUNK, CHUNK), 1)
    jj = lax.broadcasted_iota(jnp.int32, (nc, CHUNK, CHUNK), 2)
    tril = ii >= jj
    strict = ii > jj
    dm = jnp.where(tril, jnp.exp(jnp.where(tril, col64 - row64, 0.0)), 0.0)
    kb = k3 * beta3
    lmat = jnp.where(strict, _bmm(kb, k3, "nt") * dm, 0.0)
    attn = _bmm(q3, k3, "nt") * dm
    r.update(q3=q3, k3=k3, v3=v3, beta3=beta3, eg=eg, ekd=ekd, gl=jnp.exp(gcl3), dm=dm, kb=kb, lmat=lmat,
             attn=attn, strict=strict, tril=tril, qg=q3 * eg, kdec=k3 * ekd)
    return r


def _gdc_specs(s_len):
    act = lambda off: pl.BlockSpec((s_len, 128), lambda b, h: (b, off + h))
    smem = pl.BlockSpec(memory_space=pltpu.SMEM)
    return [act(0), act(8), act(16), act(24), pl.BlockSpec((s_len, 128), lambda b, h: (b, 0)), smem, smem,
            pl.BlockSpec((1, 128), lambda b, h: (0, 0))]


def gdc_fwd(qkv, proj, gates, a_log, dtb, ng, bsz, *, name, side=None):
    t = proj.shape[0]
    s_len = t // bsz
    nc = s_len // CHUNK
    grp = min(GDC_GROUP, nc)
    gr = grp * CHUNK

    def body(q_ref, k_ref, v_ref, z_ref, gt_ref, al_ref, dt_ref, ng_ref,
             out_ref, o_ref, tm_ref, st_ref, c_s, b_s, qp_s, op_s, gl_s):
        h = pl.program_id(1)

        def local(gi, carry):
            rs = pl.ds(pl.multiple_of(gi * gr, gr), gr)
            cs = pl.ds(gi * grp, grp)
            r = _gdc_local(q_ref[rs, :], k_ref[rs, :], v_ref[rs, :], gt_ref[rs, :], al_ref[h], dt_ref[h], h)
            tm = _neumann_inverse(r["lmat"])
            tm_ref[0, 0, cs] = tm
            u = _bmm(tm, r["v3"] * r["beta3"], "nn")
            w = _bmm(tm, r["kb"] * r["eg"], "nn")
            c_s[cs] = -_bmm(r["kdec"], w, "tn")
            b_s[cs] = _bmm(r["kdec"], u, "tn")
            qp_s[cs] = r["qg"] - _bmm(r["attn"], w, "nn")
            op_s[cs] = _bmm(r["attn"], u, "nn")
            gl_s[cs] = r["gl"]
            return carry

        lax.fori_loop(0, nc // grp, local, 0)

        def chunk(n, state):
            st = pl.multiple_of(n * CHUNK, CHUNK)
            st_ref[0, 0, n] = state
            o_ref[pl.ds(st, CHUNK), :] = _mdot(qp_s[n], state) + op_s[n]
            return state * gl_s[n] + _mdot(c_s[n], state) + b_s[n]

        lax.fori_loop(0, nc, chunk, jnp.zeros((128, 128), F32))
        o = o_ref[...]
        rms = lax.rsqrt(jnp.mean(o * o, axis=-1, keepdims=True) + NORM_EPS)
        z = z_ref[...].astype(F32)
        out_ref[...] = o * rms * ng_ref[...] * (z * _sigmoid(z))

    blk = pl.BlockSpec((s_len, 128), lambda b, h: (b, h))
    full = jax.ShapeDtypeStruct((t, C_WIDTH), F32)
    return _call(
        body, (qkv, qkv, qkv, proj, gates, a_log, dtb, ng.reshape(1, 128)), side, (bsz, C_HEADS), name=name,
        in_specs=_gdc_specs(s_len),
        out_specs=[blk, blk, pl.BlockSpec((1, 1, nc, CHUNK, CHUNK), lambda b, h: (b, h, 0, 0, 0)),
                   pl.BlockSpec((1, 1, nc, 128, 128), lambda b, h: (b, h, 0, 0, 0))],
        out_shape=[full, full, jax.ShapeDtypeStruct((bsz, C_HEADS, nc, CHUNK, CHUNK), F32),
                   jax.ShapeDtypeStruct((bsz, C_HEADS, nc, 128, 128), F32)],
        scratch_shapes=[pltpu.VMEM((nc, 128, 128), F32)] * 2 + [pltpu.VMEM((nc, CHUNK, 128), F32)] * 2 +
                       [pltpu.VMEM((nc, 1, 128), F32)],
        compiler_params=_cp("parallel", "parallel"),
    )


def gdc_bwd(qkv, proj, gates, a_log, dtb, ng, o_pre, tmat, states, dout, bsz, *, name, side=None):
    t = proj.shape[0]
    s_len = t // bsz
    nc = s_len // CHUNK
    grp = min(GDC_GROUP, nc)
    gr = grp * CHUNK

    def body(q_ref, k_ref, v_ref, z_ref, gt_ref, al_ref, dt_ref, ng_ref, o_ref, tm_ref, st_ref, do_ref,
             dq_ref, dk_ref, dv_ref, dz_ref, dgt_ref, dsm_ref, c_s, e_s, dsp_s, gl_s, dop_s):
        h = pl.program_id(1)
        a_log_h, dtb_h = al_ref[h], dt_ref[h]

        z = z_ref[...].astype(F32)
        sz = _sigmoid(z)
        o = o_ref[...]
        rms = lax.rsqrt(jnp.mean(o * o, axis=-1, keepdims=True) + NORM_EPS)
        on = o * rms
        dout_v = do_ref[...]
        ngv = ng_ref[...]
        dz_ref[...] = (dout_v * on * ngv * (sz * (1.0 + z * (1.0 - sz)))).astype(MM)
        dos = dout_v * (z * sz)
        dng = jnp.sum(dos * on, axis=0, keepdims=True)
        don = dos * ngv
        dop_s[...] = (rms * (don - on * jnp.mean(don * on, axis=-1, keepdims=True))).reshape(nc, CHUNK, 128)

        def local(gi, carry):
            rs = pl.ds(pl.multiple_of(gi * gr, gr), gr)
            cs = pl.ds(gi * grp, grp)
            r = _gdc_local(q_ref[rs, :], k_ref[rs, :], v_ref[rs, :], gt_ref[rs, :], a_log_h, dtb_h, h)
            w = _bmm(tm_ref[0, 0, cs], r["kb"] * r["eg"], "nn")
            c_s[cs] = -_bmm(w, r["kdec"], "tn")
            e_s[cs] = _bmm(r["qg"] - _bmm(r["attn"], w, "nn"), dop_s[cs], "tn")
            gl_s[cs] = r["gl"]
            return carry

        lax.fori_loop(0, nc // grp, local, 0)

        def chunk(i, dstate):
            n = nc - 1 - i
            dsp_s[n] = dstate
            return dstate * gl_s[n] + _mdot(c_s[n], dstate) + e_s[n]

        lax.fori_loop(0, nc, chunk, jnp.zeros((128, 128), F32))

        @pl.when(h == 0)
        def _():
            dgt_ref[...] = jnp.zeros_like(dgt_ref)
            dsm_ref[...] = jnp.zeros_like(dsm_ref)

        def local_bwd(gi, carry):
            d_alog, d_dtb = carry
            rs = pl.ds(pl.multiple_of(gi * gr, gr), gr)
            cs = pl.ds(gi * grp, grp)
            r = _gdc_local(q_ref[rs, :], k_ref[rs, :], v_ref[rs, :], gt_ref[rs, :], a_log_h, dtb_h, h)
            row, lane = r["row"], r["lane"]
            q3, k3, v3, beta3, eg, kb, dm = r["q3"], r["k3"], r["v3"], r["beta3"], r["eg"], r["kb"], r["dm"]
            tm = tm_ref[0, 0, cs]
            u3 = _bmm(tm, v3 * beta3, "nn")
            w3 = _bmm(tm, kb * eg, "nn")
            state, dsp, do3 = st_ref[0, 0, cs], dsp_s[cs], dop_s[cs]
            vn = u3 - _bmm(w3, state, "nn")
            du = _bmm(r["attn"], do3, "tn") + _bmm(r["kdec"], dsp, "nn")
            dat = jnp.where(r["tril"], _bmm(do3, vn, "nt"), 0.0)
            dqg = _bmm(do3, state, "nt")
            dkd = _bmm(vn, dsp, "nt")
            dgl = jnp.sum(jnp.sum(state * dsp, axis=2, keepdims=True), axis=1, keepdims=True)
            dw = -_bmm(du, state, "nt")
            dvb = _bmm(tm, du, "tn")
            dkbg = _bmm(tm, dw, "tn")
            dl = -jnp.where(r["strict"], _bmm(dvb, u3, "nt") + _bmm(dkbg, w3, "nt"), 0.0)
            dml = dl * dm
            dn = dat * dm
            dkb = _bmm(dml, k3, "nn") + dkbg * eg
            dk3 = _bmm(dml, kb, "tn") + _bmm(dn, q3, "tn") + dkd * r["ekd"] + dkb * beta3
            dq3 = dqg * eg + _bmm(dn, k3, "nn")
            e = dl * r["lmat"] + dat * r["attn"]
            ones = jnp.ones((grp, CHUNK, 128), F32)
            colsum = lax.dot_general(e, ones, (_BDIMS["tn"], ((0,), (0,))), preferred_element_type=F32, precision=lax.Precision.HIGH)
            dgc = jnp.sum(e, axis=-1, keepdims=True) - colsum
            dgc = dgc + eg * (jnp.sum(dqg * q3, axis=-1, keepdims=True) + jnp.sum(dkbg * kb, axis=-1, keepdims=True))
            skd = jnp.sum(dkd * r["kdec"], axis=-1, keepdims=True)
            dgcl = jnp.sum(skd, axis=1, keepdims=True) + dgl * r["gl"]
            pos3 = lax.broadcasted_iota(jnp.int32, (grp, CHUNK, 128), 1)
            dgc = dgc - skd + jnp.where(pos3 == CHUNK - 1, dgcl, 0.0)
            dbeta = jnp.sum(dkb * k3, axis=-1, keepdims=True) + jnp.sum(dvb * v3, axis=-1, keepdims=True)
            dg = _seg_cumsum_rev(dgc.reshape(gr, 128), row)
            beta = r["beta"]
            dbl = jnp.broadcast_to(dbeta, (grp, CHUNK, 128)).reshape(gr, 128) * beta * (1.0 - beta)
            dai = dg * (-r["A"]) * _sigmoid(r["pre"])
            dgt_ref[rs, :] += jnp.where(lane == h, dbl, 0.0) + jnp.where(lane == 8 + h, dai, 0.0)
            dq_ref[rs, :] = dq3.reshape(gr, 128) * C_QSCALE
            dk_ref[rs, :] = dk3.reshape(gr, 128)
            dv_ref[rs, :] = (dvb * beta3).reshape(gr, 128)
            return (d_alog + jnp.sum(dg * (-r["sp"]), axis=0, keepdims=True) * r["A"],
                    d_dtb + jnp.sum(dai, axis=0, keepdims=True))

        zero = jnp.zeros((1, 128), F32)
        d_alog, d_dtb = lax.fori_loop(0, nc // grp, local_bwd, (zero, zero))
        r16 = lax.broadcasted_iota(jnp.int32, (16, 128), 0)
        l16 = lax.broadcasted_iota(jnp.int32, (16, 128), 1)
        small = jnp.where((r16 == h) & (l16 == 0), d_alog, 0.0) + jnp.where((r16 == h) & (l16 == 1), d_dtb, 0.0)
        dsm_ref[0] += small + jnp.where(r16 == 8 + h, dng, 0.0)

    blk = pl.BlockSpec((s_len, 128), lambda b, h: (b, h))
    blk3 = lambda off: pl.BlockSpec((s_len, 128), lambda b, h: (b, off + h))
    full = jax.ShapeDtypeStruct((t, C_WIDTH), F32)
    c128 = pltpu.VMEM((nc, CHUNK, 128), F32)
    sq = pltpu.VMEM((nc, 128, 128), F32)
    res = _call(
        body, (qkv, qkv, qkv, proj, gates, a_log, dtb, ng.reshape(1, 128), o_pre, tmat, states, dout), side,
        (bsz, C_HEADS), name=name,
        in_specs=_gdc_specs(s_len) + [blk, pl.BlockSpec((1, 1, nc, CHUNK, CHUNK), lambda b, h: (b, h, 0, 0, 0)),
                                      pl.BlockSpec((1, 1, nc, 128, 128), lambda b, h: (b, h, 0, 0, 0)), blk],
        out_specs=[blk, blk, blk, pl.BlockSpec((s_len, 128), lambda b, h: (b, 3 * C_HEADS + h)),
                   pl.BlockSpec((s_len, 128), lambda b, h: (b, 0)), pl.BlockSpec((1, 16, 128), lambda b, h: (b, 0, 0))],
        out_shape=[full, full, full, jax.ShapeDtypeStruct((t, 4 * C_WIDTH), MM), jax.ShapeDtypeStruct((t, 128), F32),
                   jax.ShapeDtypeStruct((bsz, 16, 128), F32)],
        scratch_shapes=[sq, sq, sq, pltpu.VMEM((nc, 1, 128), F32), c128],
        compiler_params=_cp("parallel", "arbitrary"),
    )
    (dq, dk, dv, dz, dgates, dsm), extra = res if side is not None else (res, None)
    out = ((dq, dk, dv), dz, dgates, dsm)
    return out if side is None else (out, extra)


def join_cols(x, *, name, outs=None, tk=256):
    _, kk, n = x.shape
    tk = _tile8(kk, tk)
    outs = outs or [(0, N_DEV * n, N_DEV * n)]

    def body(x_ref, *o_refs):
        full = jnp.concatenate([x_ref[k] for k in range(N_DEV)], axis=-1)
        for (lo, hi, wd), o_ref in zip(outs, o_refs):
            piece = full[:, lo:hi]
            if wd > hi - lo:
                piece = jnp.concatenate([piece, jnp.zeros((tk, wd - (hi - lo)), piece.dtype)], axis=-1)
            o_ref[...] = piece

    res = _pcall(
        body, name=name, grid=(kk // tk,), in_specs=[pl.BlockSpec((N_DEV, tk, n), lambda i: (0, i, 0))],
        out_specs=[pl.BlockSpec((tk, wd), lambda i: (i, 0)) for _, _, wd in outs],
        out_shape=[jax.ShapeDtypeStruct((kk, wd), x.dtype) for _, _, wd in outs], compiler_params=_cp("parallel"),
    )(x)
    return res if len(outs) > 1 else res[0]


def split_cols(pieces, n, *, name, tk=256):
    kk = pieces[0][0].shape[0]
    tk = _tile8(kk, tk)

    def body(*refs):
        o_ref = refs[-1]
        vals = [r[...][:, :used] for r, (_, used) in zip(refs[:-1], pieces)]
        full = vals[0] if len(vals) == 1 else jnp.concatenate(vals, axis=-1)
        for k in range(N_DEV):
            o_ref[k] = full[:, k * n:(k + 1) * n].astype(MM)

    return _pcall(
        body, name=name, grid=(kk // tk,),
        in_specs=[pl.BlockSpec((tk, arr.shape[1]), lambda i: (i, 0)) for arr, _ in pieces],
        out_specs=pl.BlockSpec((N_DEV, tk, n), lambda i: (0, i, 0)),
        out_shape=jax.ShapeDtypeStruct((N_DEV, kk, n), MM), compiler_params=_cp("parallel"),
    )(*[arr for arr, _ in pieces])


def gather_multi(shards, *, name):
    ng = len(shards)

    def body(*refs):
        x_refs, o_refs = refs[:ng], refs[ng:2 * ng]
        send_sems, recv_sems, local_sems = refs[2 * ng:]
        x, y, c = _me()
        sibling = (x, y, 1 - c)
        chips = [(1 - x, y), (x, 1 - y), (1 - x, 1 - y)]

        def slot(px, py, pc):
            return 4 * px + 2 * py + pc

        def copy(gi, k, block, to, src=None):
            dst = o_refs[gi].at[slot(*block)]
            return pltpu.make_async_remote_copy(
                src_ref=dst if src is None else src, dst_ref=dst, send_sem=send_sems.at[7 * gi + k],
                recv_sem=recv_sems.at[7 * gi + k], device_id=to, device_id_type=MESH_ID)

        own = [pltpu.make_async_copy(x_refs[gi], o_refs[gi].at[slot(x, y, c)], local_sems.at[gi]) for gi in range(ng)]
        for cp in own:
            cp.start()
        first = []
        for gi in range(ng):
            first.append(copy(gi, 0, (x, y, c), sibling, src=x_refs[gi]))
            first += [copy(gi, 1 + j, (x, y, c), (*chip, c), src=x_refs[gi]) for j, chip in enumerate(chips)]
        for cp in first:
            cp.start()
        passed = []
        for j, chip in enumerate(chips):
            for gi in range(ng):
                copy(gi, 1 + j, (*chip, c), (x, y, c)).wait_recv()
                fwd = copy(gi, 4 + j, (*chip, c), sibling)
                fwd.start()
                passed.append(fwd)
        for gi in range(ng):
            copy(gi, 0, sibling, (x, y, c)).wait_recv()
            for j, chip in enumerate(chips):
                copy(gi, 4 + j, (*chip, 1 - c), (x, y, c)).wait_recv()
        for cp in first + passed:
            cp.wait_send()
        for cp in own:
            cp.wait()

    hbm = pl.BlockSpec(memory_space=pl.ANY)
    return _pcall(
        body, name=name, in_specs=[hbm] * ng, out_specs=[hbm] * ng,
        out_shape=[jax.ShapeDtypeStruct((N_DEV,) + s.shape, s.dtype) for s in shards],
        scratch_shapes=[pltpu.SemaphoreType.DMA((7 * ng,)), pltpu.SemaphoreType.DMA((7 * ng,)),
                        pltpu.SemaphoreType.DMA((ng,))],
    )(*shards)


def exchange_multi(parts, *, name):
    ng = len(parts)

    def body(*refs):
        x_refs, o_refs = refs[:ng], refs[ng:2 * ng]
        send_sems, recv_sems, local_sems = refs[2 * ng:]
        x, y, c = _me()
        mine = 4 * x + 2 * y + c
        own = [pltpu.make_async_copy(x_refs[gi].at[mine], o_refs[gi].at[mine], local_sems.at[gi]) for gi in range(ng)]
        for cp in own:
            cp.start()
        copies = []
        for k, (dx, dy, dc) in enumerate(_FLIPS):
            px, py, pc = _flip(x, dx), _flip(y, dy), _flip(c, dc)
            for gi in range(ng):
                cp = pltpu.make_async_remote_copy(
                    src_ref=x_refs[gi].at[4 * px + 2 * py + pc], dst_ref=o_refs[gi].at[mine],
                    send_sem=send_sems.at[7 * gi + k], recv_sem=recv_sems.at[7 * gi + k], device_id=(px, py, pc),
                    device_id_type=MESH_ID)
                cp.start()
                copies.append(cp)
        for cp in copies:
            cp.wait()
        for cp in own:
            cp.wait()

    hbm = pl.BlockSpec(memory_space=pl.ANY)
    return _pcall(
        body, name=name, in_specs=[hbm] * ng, out_specs=[hbm] * ng,
        out_shape=[jax.ShapeDtypeStruct(s.shape, s.dtype) for s in parts],
        scratch_shapes=[pltpu.SemaphoreType.DMA((7 * ng,)), pltpu.SemaphoreType.DMA((7 * ng,)),
                        pltpu.SemaphoreType.DMA((ng,))],
    )(*parts)


def adamw_rows(parts, row0, w, m, v, *, name, tr=256):
    r, cdim = w.shape
    tr = _tile8(math.gcd(r, row0) if row0 else r, tr)
    blk0 = row0 // tr

    def body(p_ref, w_ref, m_ref, v_ref, g_ref, d_ref, mo_ref, vo_ref):
        g = p_ref[0].astype(F32)
        for j in range(1, N_DEV):
            g = g + p_ref[j].astype(F32)
        g_ref[...] = g
        mn = ADAM_B1 * m_ref[...] + (1.0 - ADAM_B1) * g
        vn = ADAM_B2 * v_ref[...] + (1.0 - ADAM_B2) * (g * g)
        mo_ref[...] = mn
        vo_ref[...] = vn
        m_hat = mn / (1.0 - ADAM_B1 ** ADAM_STEP)
        v_hat = vn / (1.0 - ADAM_B2 ** ADAM_STEP)
        d_ref[...] = -ADAM_LR * (m_hat / (jnp.sqrt(v_hat) + ADAM_EPS) + ADAM_WD * w_ref[...])

    blk = pl.BlockSpec((tr, cdim), lambda i: (i, 0))
    shp = jax.ShapeDtypeStruct((r, cdim), F32)
    return _pcall(
        body, name=name, grid=(r // tr,),
        in_specs=[pl.BlockSpec((N_DEV, tr, cdim), lambda i: (0, blk0 + i, 0)), blk, blk, blk],
        out_specs=[blk, blk, blk, blk], out_shape=[shp, shp, shp, shp], compiler_params=_cp("parallel"),
    )(parts, w, m, v)


def _tile8(n, pref):
    for c in range(min(pref, n) - min(pref, n) % 16, 0, -16):
        if n % c == 0:
            return c
    return n


REPL = ["ple_bg", "a_sinks", "b_conv_b", "b_wa", "b_ba", "b_wx", "b_bx", "b_lam", "c_a_log", "c_dt_bias", "c_norm_g"]
WEIGHTS = ["ffn1_wg", "ffn1_wu", "ffn1_wd", "ffn2_wg", "ffn2_wu", "ffn2_wd", "ln_g", "ln_b", "ple_wg", "ple_bg", "ple_wp",
           "ab_w_in", "a_sinks", "b_conv_w", "b_conv_b", "b_wa", "b_ba", "b_wx", "b_bx", "b_lam", "ab_w_out", "c_w_in",
           "c_conv_w", "c_a_log", "c_dt_bias", "c_norm_g", "c_w_out"]
SMALL_NAMES = ["ln_g", "ln_b", "b_conv_w", "c_conv_w"]
SMALL_F32 = 73728
LANES = 128
FFN_TM = 512
FFN_TF = 1408


def _join(blocks, axis):
    moved = jnp.moveaxis(blocks, 0, axis)
    shp = list(moved.shape)
    return moved.reshape(shp[:axis] + [shp[axis] * shp[axis + 1]] + shp[axis + 2:])


def _split(full, axis):
    shp = list(full.shape)
    return jnp.moveaxis(full.reshape(shp[:axis] + [N_DEV, shp[axis] // N_DEV] + shp[axis + 1:]), axis, 0)


def _dense_blocks(w):
    z = jnp.zeros((4, 2, 64, 2, 64), w.dtype)
    w4 = w.reshape(4, 2, 64, 64)
    z = z.at[:, 0, :, 0, :].set(w4[:, 0]).at[:, 1, :, 1, :].set(w4[:, 1])
    return z.reshape(4, 128, 128)


def _diag_blocks(d):
    d5 = d.reshape(4, 2, 64, 2, 64)
    return jnp.stack([d5[:, 0, :, 0, :], d5[:, 1, :, 1, :]], axis=1).reshape(8, 64, 64)


def _flat_pad(arrs, dtype, total):
    flat = jnp.concatenate([z.astype(dtype).reshape(-1) for z in arrs])
    return jnp.pad(flat, (0, total - flat.shape[0]))


def _flat8_pad(arrs, dtype, total):
    flat = jnp.concatenate([z.astype(dtype).reshape(N_DEV, -1) for z in arrs], axis=1)
    return jnp.pad(flat, ((0, 0), (0, total - flat.shape[1])))


def _take(flat, names, shapes):
    out, off = {}, 0
    for n in names:
        sz = int(np.prod(shapes[n]))
        out[n] = flat[..., off:off + sz].reshape(flat.shape[:-1] + tuple(shapes[n]))
        off += sz
    return out


def kernel(x, p, ffn1_wg, ffn1_wu, ffn1_wd, ffn2_wg, ffn2_wu, ffn2_wd, ln_g, ln_b, ple_wg, ple_bg, ple_wp, ab_w_in, a_sinks, b_conv_w, b_conv_b, b_wa, b_ba, b_wx, b_bx, b_lam, ab_w_out, c_w_in, c_conv_w, c_a_log, c_dt_bias, c_norm_g, c_w_out, loss_target, m_ffn1_wg, m_ffn1_wu, m_ffn1_wd, m_ffn2_wg, m_ffn2_wu, m_ffn2_wd, m_ln_g, m_ln_b, m_ple_wg, m_ple_bg, m_ple_wp, m_ab_w_in, m_a_sinks, m_b_conv_w, m_b_conv_b, m_b_wa, m_b_ba, m_b_wx, m_b_bx, m_b_lam, m_ab_w_out, m_c_w_in, m_c_conv_w, m_c_a_log, m_c_dt_bias, m_c_norm_g, m_c_w_out, v_ffn1_wg, v_ffn1_wu, v_ffn1_wd, v_ffn2_wg, v_ffn2_wu, v_ffn2_wd, v_ln_g, v_ln_b, v_ple_wg, v_ple_bg, v_ple_wp, v_ab_w_in, v_a_sinks, v_b_conv_w, v_b_conv_b, v_b_wa, v_b_ba, v_b_wx, v_b_bx, v_b_lam, v_ab_w_out, v_c_w_in, v_c_conv_w, v_c_a_log, v_c_dt_bias, v_c_norm_g, v_c_w_out):
    a = dict(locals())
    return _step3(a)


def _step3(a):
    x, p = a["x"], a["p"]
    bsz, s_len, d = x.shape
    t = bsz * s_len
    x2 = x.reshape(t, d)
    tgt = a["loss_target"].reshape(t, d)
    p2 = p.reshape(DEPTH, t, D_PLE)
    shapes = {n: a[n].shape for n in WEIGHTS}
    n_small = sum(int(np.prod(shapes[n])) for n in SMALL_NAMES)
    small_all = SMALL_NAMES + REPL
    f_ff = shapes["ffn1_wg"][2]
    c_cols = shapes["c_w_in"][2]
    wide = dict(tm=1024, tn=1408, tk=1024)
    tall = dict(tm=1408, tn=1024, tk=1024)

    def cast(z):
        return z.astype(MM)

    def ffn_shards(which, l):
        return [cast(a[which + "_wg"][l]), cast(a[which + "_wu"][l]), cast(a[which + "_wd"][l])]

    def ffn_weights(gat, tag):
        return (join_cols(gat[0], name=f"join_{tag}_wg"), join_cols(gat[1], name=f"join_{tag}_wu"),
                gat[2].reshape(N_DEV * gat[2].shape[1], D_MODEL))

    def rows_full(gat):
        return gat.reshape(N_DEV * gat.shape[1], D_MODEL)

    small_send = _flat_pad([a[n] for n in SMALL_NAMES], F32, 32 * LANES).reshape(32, LANES)
    g0 = gather_multi(ffn_shards("ffn1", 0) + [small_send], name="gather_first")
    ws = _take(g0[3].reshape(N_DEV, -1), SMALL_NAMES, shapes)
    small = {n: _join(ws[n], 2) for n in SMALL_NAMES}
    ln_g, ln_b = small["ln_g"], small["ln_b"]
    wa_d, wx_d = _dense_blocks(a["b_wa"][0]), _dense_blocks(a["b_wx"][0])
    lru_w = (small["b_conv_w"][0], a["b_conv_b"][0], wa_d, a["b_ba"][0], wx_d, a["b_bx"][0], a["b_lam"][0])
    gdc_w = (a["c_a_log"][0], a["c_dt_bias"][0], a["c_norm_g"][0])
    wf = {("ffn1", 0): ffn_weights(g0[:3], "ffn1_0")}

    s0 = {"x0": x2}
    u1 = ffn_shards("ffn2", 0)
    side = ("gather", u1[:2] + [cast(a["ab_w_in"][0]), cast(a["ab_w_out"][0])])
    (s0["y1"], s0["z1"], s0["hg1"], s0["hu1"]), got = ffn_fwd(x2, *wf["ffn1", 0], ln_g[0, 0], ln_b[0, 0],
                                                             name="ffn1_fwd_0", tm=FFN_TM, tf=FFN_TF, side=side)
    ab_w_in, ab_w_out = join_cols(got[2], name="join_ab_in"), rows_full(got[3])
    s0["proj"] = matmul(s0["y1"], ab_w_in, mode="nn", out_dtype=MM, name="ab_in_fwd", tn=896, tk=1024)
    ya, got_ple = attn_fwd(s0["proj"], a["a_sinks"][0], bsz, name="attn_fwd",
                           side=("gather", [u1[2], cast(a["ple_wg"][0]), cast(a["ple_wp"][0])]))
    wf["ffn2", 0] = ffn_weights(got[:2] + got_ple[:1], "ffn2_0")
    got_ple = got_ple[1:]
    yb = lru_fwd(s0["proj"], *lru_w, bsz, name="lru_fwd")
    s0["mix"] = jnp.concatenate([ya, yb], axis=1)
    s0["y2"], s0["z2"] = mm_ln_fwd(s0["mix"], ab_w_out, s0["y1"], ln_g[0, 1], ln_b[0, 1], name="mix_out_fwd_0")
    side = ("gather", ffn_shards("ffn1", 1))
    (s0["y3"], s0["z3"], s0["hg2"], s0["hu2"]), got = ffn_fwd(s0["y2"], *wf["ffn2", 0], ln_g[0, 2], ln_b[0, 2],
                                                             name="ffn2_fwd_0", tm=FFN_TM, tf=FFN_TF, side=side)
    wf["ffn1", 1] = ffn_weights(got, "ffn1_1")
    ple_wg = [rows_full(got_ple[0]), None]
    ple_wp = [_join(got_ple[1], 1), None]
    h1 = ple_fwd(s0["y3"], p2[0], ple_wg[0], a["ple_bg"][0], ple_wp[0], name="ple_fwd_0")

    s1 = {"x0": h1}
    side = ("gather", [cast(a["c_w_in"][0]), cast(a["c_w_out"][0])])
    (s1["y1"], s1["z1"], s1["hg1"], s1["hu1"]), got = ffn_fwd(h1, *wf["ffn1", 1], ln_g[1, 0], ln_b[1, 0],
                                                             name="ffn1_fwd_1", tm=FFN_TM, tf=FFN_TF, side=side)
    c_in_main, c_in_gate = join_cols(got[0], name="join_c_in", outs=[(0, 4 * C_WIDTH, 4 * C_WIDTH),
                                                                      (4 * C_WIDTH, 4 * C_WIDTH + 2 * C_HEADS, LANES)])
    c_w_out = rows_full(got[1])
    s1["proj"] = matmul(s1["y1"], c_in_main, mode="nn", out_dtype=MM, name="c_in_fwd", tm=1024, tn=2048, tk=1024)
    s1["gates"] = matmul(s1["y1"], c_in_gate, mode="nn", name="c_gate_fwd", tk=1024)
    s1["qkv"] = gdc_pre_fwd(s1["proj"], small["c_conv_w"][0], bsz, name="gdc_pre_fwd")
    side = ("gather", ffn_shards("ffn2", 1) + [cast(a["ple_wg"][1]), cast(a["ple_wp"][1])])
    (s1["mix"], s1["o_pre"], s1["tmat"], s1["states"]), got = gdc_fwd(
        s1["qkv"], s1["proj"], s1["gates"], *gdc_w, bsz, name="gdc_fwd", side=side)
    wf["ffn2", 1] = ffn_weights(got[:3], "ffn2_1")
    ple_wg[1], ple_wp[1] = rows_full(got[3]), _join(got[4], 1)
    s1["y2"], s1["z2"] = mm_ln_fwd(s1["mix"], c_w_out, s1["y1"], ln_g[1, 1], ln_b[1, 1], name="mix_out_fwd_1")
    s1["y3"], s1["z3"], s1["hg2"], s1["hu2"] = ffn_fwd(s1["y2"], *wf["ffn2", 1], ln_g[1, 2], ln_b[1, 2], name="ffn2_fwd_1",
                                                           tm=FFN_TM, tf=FFN_TF)
    h2 = ple_fwd(s1["y3"], p2[1], ple_wg[1], a["ple_bg"][1], ple_wp[1], name="ple_fwd_1")
    loss_part, dh = loss_fwd_bwd(h2, tgt, name="loss")

    def ffn_parts(xin, act, dhg, dhu, dz, tag):
        dwg = matmul(xin, dhg, mode="tn", out_dtype=MM, split_n=f_ff, name=f"{tag}_wg_grad", **wide)
        dwu = matmul(xin, dhu, mode="tn", out_dtype=MM, split_n=f_ff, name=f"{tag}_wu_grad", **wide)
        dwd = matmul(act, dz, mode="tn", scale=0.5, out_dtype=MM, name=f"{tag}_wd_grad", **tall)
        return [dwg, dwu, dwd.reshape(N_DEV, f_ff, D_MODEL)]

    def ple_parts(i, s, dt, de):
        gwg = matmul(s["y3"], dt, mode="tn", out_dtype=MM, name=f"ple_wg_grad_{i}", tm=1024, tn=1024)
        gwp = matmul(p2[i], de, mode="tn", out_dtype=MM, name=f"ple_wp_grad_{i}", tn=1024)
        return [gwg.reshape(N_DEV, D_MODEL // N_DEV, D_MODEL), _split(gwp, 1)]

    gln = {"ln_g": [None, None], "ln_b": [None, None]}
    gple_bg = [None, None]

    dz3, dt, de, dbg, dg2, db2 = ple_bwd(dh, s1["y3"], p2[1], ple_wg[1], a["ple_bg"][1], ple_wp[1], s1["z3"], ln_g[1, 2],
                                         name="ple_bwd_1")
    gple_bg[1] = dbg[0]
    parts_ple1 = ple_parts(1, s1, dt, de)
    dy2, act, dhg, dhu = ffn_bwd(dz3, s1["hg2"], s1["hu2"], *wf["ffn2", 1], name="ffn2_bwd_1", tm=FFN_TM, tf=FFN_TF)
    parts_ffn2_1 = ffn_parts(s1["y2"], act, dhg, dhu, dz3, "ffn2_1")
    dz2, dg1, db1 = ln_bwd(dy2, s1["z2"], ln_g[1, 1], name="ln1_bwd_1")
    dmix = matmul(dz2, c_w_out, mode="nt", name="c_out_bwd", tn=1024, tk=1024)
    parts_c_out = matmul(s1["mix"], dz2, mode="tn", out_dtype=MM, name="c_out_grad", tm=1024, tn=1024).reshape(
        N_DEV, D_MODEL // N_DEV, D_MODEL)
    (dqkv, dzc, dgates, dsm), recv1 = gdc_bwd(s1["qkv"], s1["proj"], s1["gates"], *gdc_w, s1["o_pre"],
                                              s1["tmat"], s1["states"], dmix, bsz, name="gdc_bwd",
                                              side=("exchange", parts_ffn2_1 + parts_ple1))
    dproj, dccw = gdc_pre_bwd(s1["proj"], small["c_conv_w"][0], dqkv, dzc, bsz, name="gdc_pre_bwd")
    dgb = dgates.astype(MM)
    dy1 = matmul(dproj, c_in_main, mode="nt", add=dz2, add_scale=DN_ALPHA, name="c_in_bwd", tn=1024, tk=4096)
    dz1, dg0, db0 = matmul(dgb, c_in_gate, mode="nt", add=dy1, name="c_gate_bwd", tn=1024, ln=(s1["z1"], ln_g[1, 0]))
    g_c_main = matmul(s1["y1"], dproj, mode="tn", name="c_in_grad", tm=1024, tn=1024, tk=1024)
    g_c_gate = matmul(s1["y1"], dgb, mode="tn", name="c_gate_grad", tm=1024)
    parts_c_in = split_cols([(g_c_main, 4 * C_WIDTH), (g_c_gate, 2 * C_HEADS)], c_cols, name="split_c_in")
    (dh, act, dhg, dhu), recv_c = ffn_bwd(dz1, s1["hg1"], s1["hu1"], *wf["ffn1", 1], name="ffn1_bwd_1", tm=FFN_TM,
                                          tf=FFN_TF, side=("exchange", [parts_c_in, parts_c_out]))
    parts_ffn1_1 = ffn_parts(s1["x0"], act, dhg, dhu, dz1, "ffn1_1")
    gln["ln_g"][1] = jnp.concatenate([dg0, dg1, dg2], axis=0)
    gln["ln_b"][1] = jnp.concatenate([db0, db1, db2], axis=0)

    dz3, dt, de, dbg, dg2, db2 = ple_bwd(dh, s0["y3"], p2[0], ple_wg[0], a["ple_bg"][0], ple_wp[0], s0["z3"], ln_g[0, 2],
                                         name="ple_bwd_0")
    gple_bg[0] = dbg[0]
    parts_ple0 = ple_parts(0, s0, dt, de)
    (dy2, act, dhg, dhu), recv2 = ffn_bwd(dz3, s0["hg2"], s0["hu2"], *wf["ffn2", 0], name="ffn2_bwd_0", tm=FFN_TM,
                                          tf=FFN_TF, side=("exchange", parts_ffn1_1))
    parts_ffn2_0 = ffn_parts(s0["y2"], act, dhg, dhu, dz3, "ffn2_0")
    dz2, dg1, db1 = ln_bwd(dy2, s0["z2"], ln_g[0, 1], name="ln1_bwd_0")
    dmix = matmul(dz2, ab_w_out, mode="nt", name="ab_out_bwd", tn=1024, tk=1024)
    parts_ab_out = matmul(s0["mix"], dz2, mode="tn", out_dtype=MM, name="ab_out_grad", tm=1024, tn=1024).reshape(
        N_DEV, D_MODEL // N_DEV, D_MODEL)
    (dq, dk, dv, dsk), recv3a = attn_bwd(s0["proj"], a["a_sinks"][0], dmix, bsz, name="attn_bwd",
                                         side=("exchange", parts_ffn2_0[:2]))
    (dbx, dbgate, dcw, dcb, dwa, dba, dwx, dbxb, dlam), recv3b = lru_bwd(
        s0["proj"], *lru_w, dmix, bsz, name="lru_bwd", side=("exchange", [parts_ffn2_0[2]] + parts_ple0 + [parts_ab_out]))
    dproj = jnp.concatenate([dq, dk, dv, dbx, dbgate], axis=1).astype(MM)
    dz1, dg0, db0 = matmul(dproj, ab_w_in, mode="nt", add=dz2, add_scale=DN_ALPHA, name="ab_in_bwd", tn=1024, tk=1792,
                           ln=(s0["z1"], ln_g[0, 0]))
    parts_ab_in = matmul(s0["y1"], dproj, mode="tn", out_dtype=MM, split_n=AB_PROJ // N_DEV, name="ab_in_grad",
                         tm=1024, tn=896)
    gln["ln_g"][0] = jnp.concatenate([dg0, dg1, dg2], axis=0)
    gln["ln_b"][0] = jnp.concatenate([db0, db1, db2], axis=0)

    dsm_sum = jnp.sum(dsm, axis=0)
    full = dict(ln_g=jnp.stack(gln["ln_g"]), ln_b=jnp.stack(gln["ln_b"]), b_conv_w=dcw[None],
                c_conv_w=jnp.sum(dccw, axis=0)[None], ple_bg=jnp.stack(gple_bg),
                a_sinks=jnp.sum(dsk, axis=0)[:, :A_HEADS], b_conv_b=dcb, b_wa=_diag_blocks(dwa)[None], b_ba=dba,
                b_wx=_diag_blocks(dwx)[None], b_bx=dbxb, b_lam=dlam, c_a_log=dsm_sum[None, :C_HEADS, 0],
                c_dt_bias=dsm_sum[None, :C_HEADS, 1], c_norm_g=jnp.sum(dsm_sum[C_HEADS:], axis=0)[None])
    small_rows = SMALL_F32 // LANES
    repl_flat = _flat_pad([full[n] for n in REPL], F32, SMALL_F32 - n_small)
    small8 = jnp.concatenate([_flat8_pad([_split(full[n], 2) for n in SMALL_NAMES], F32, n_small),
                              jnp.broadcast_to(repl_flat, (N_DEV,) + repl_flat.shape)], axis=1)
    (dh, act, dhg, dhu), recv3c = ffn_bwd(dz1, s0["hg1"], s0["hu1"], *wf["ffn1", 0], name="ffn1_bwd_0", tm=FFN_TM,
                                          tf=FFN_TF, side=("exchange", [parts_ab_in, small8.reshape(N_DEV, small_rows, LANES)]))
    grad_x = dh.reshape(bsz, s_len, d)

    parts_wg = matmul(s0["x0"], dhg, mode="tn", out_dtype=MM, split_n=f_ff, name="ffn1_0_wg_grad", **wide)
    parts_wu, recv4a = matmul(s0["x0"], dhu, mode="tn", out_dtype=MM, split_n=f_ff, name="ffn1_0_wu_grad",
                              side=("exchange", [parts_wg]), **wide)
    dwd, recv4b = matmul(act, dz1, mode="tn", scale=0.5, out_dtype=MM, name="ffn1_0_wd_grad",
                         side=("exchange", [parts_wu]), **tall)
    recv4c = exchange_multi([dwd.reshape(N_DEV, f_ff, D_MODEL)], name="exchange_last")

    def upd(parts, n, l, shape2d, **kw):
        wmv = [a[pre + n][l].reshape(shape2d) for pre in ("", "m_", "v_")]
        return adamw_rows(parts, 0, *wmv, name=f"adamw_{n}_{l}", **kw)

    def upd_ffn(parts, which, l):
        return {(which + "_wg", l): upd(parts[0], which + "_wg", l, (D_MODEL, f_ff)),
                (which + "_wu", l): upd(parts[1], which + "_wu", l, (D_MODEL, f_ff)),
                (which + "_wd", l): upd(parts[2], which + "_wd", l, (f_ff, D_MODEL), tr=176)}

    rows8 = D_MODEL // N_DEV
    res = {}
    res.update(upd_ffn(recv1[:3], "ffn2", 1))
    res["ple_wg", 1] = upd(recv1[3], "ple_wg", 1, (rows8, D_MODEL), tr=128)
    res["ple_wp", 1] = upd(recv1[4], "ple_wp", 1, (D_PLE, LANES))
    res.update(upd_ffn(recv2, "ffn1", 1))
    res["c_w_in", 0] = upd(recv_c[0], "c_w_in", 0, (D_MODEL, c_cols))
    res["c_w_out", 0] = upd(recv_c[1], "c_w_out", 0, (rows8, D_MODEL), tr=128)
    res.update(upd_ffn(recv3a + recv3b[:1], "ffn2", 0))
    res["ple_wg", 0] = upd(recv3b[1], "ple_wg", 0, (rows8, D_MODEL), tr=128)
    res["ple_wp", 0] = upd(recv3b[2], "ple_wp", 0, (D_PLE, LANES))
    res["ab_w_out", 0] = upd(recv3b[3], "ab_w_out", 0, (rows8, D_MODEL), tr=128)
    res.update(upd_ffn(recv4a + recv4b + list(recv4c), "ffn1", 0))
    res["ab_w_in", 0] = upd(recv3c[0], "ab_w_in", 0, (D_MODEL, AB_PROJ // N_DEV))
    res_small = adamw_rows(recv3c[1], 0, *[_flat_pad([a[pre + n] for n in small_all], F32, SMALL_F32).reshape(
        small_rows, LANES) for pre in ("", "m_", "v_")], name="adamw_small", tr=small_rows)
    kinds = []
    for k in range(4):
        kd = _take(res_small[k].reshape(-1), small_all, shapes)
        for n in WEIGHTS:
            if n not in kd:
                kd[n] = jnp.stack([res[n, l][k] for l in range(shapes[n][0])]).reshape(shapes[n])
        kinds.append(kd)
    loss = lax.psum(loss_part[0, 0], ("x", "y", "c"))
    return (loss, grad_x, *[kinds[0][n] for n in WEIGHTS], *[kinds[1][n] for n in WEIGHTS],
            *[kinds[2][n] for n in WEIGHTS], *[kinds[3][n] for n in WEIGHTS])
```

```python
import functools
import math

import numpy as np
import jax
import jax.numpy as jnp
from jax import lax
from jax.experimental import pallas as pl
from jax.experimental.pallas import tpu as pltpu

F32 = jnp.float32
MM = jnp.bfloat16
HI = lax.Precision.HIGHEST

D_MODEL = 1024
D_FF = 2816
D_PLE = 256
DEPTH = 2
CHUNK = 64
A_HEADS = 8
A_KV_HEADS = 2
A_GROUP = 4
A_HEAD_DIM = 64
A_WIDTH = 512
A_KV_WIDTH = 128
B_WIDTH = 512
B_BLOCK = 64
RG_C = 8.0
AB_PROJ = 1792
C_HEADS = 8
C_HEAD_DIM = 128
C_WIDTH = 1024
DN_ALPHA = (2.0 * DEPTH) ** 0.25
LN_EPS = 1e-5
NORM_EPS = 1e-6
NEG = -1e30
ADAM_LR = 0.001
ADAM_B1 = 0.9
ADAM_B2 = 0.999
ADAM_EPS = 1e-08
ADAM_WD = 0.01
ADAM_STEP = 10
N_DEV = 8
VMEM_LIMIT = 56 * 1024 * 1024

NN = ((1,), (0,))
NT = ((1,), (1,))
TN = ((0,), (0,))


def _pcall(body, **kw):
    return pl.pallas_call(body, **kw)


def _cp(*sem):
    return pltpu.CompilerParams(dimension_semantics=sem, vmem_limit_bytes=VMEM_LIMIT)


MESH_ID = pl.DeviceIdType.MESH
_FLIPS = [(0, 0, 1), (1, 0, 0), (0, 1, 0), (1, 1, 0), (1, 0, 1), (0, 1, 1), (1, 1, 1)]


def _me():
    return lax.axis_index("x"), lax.axis_index("y"), lax.axis_index("c")


def _flip(coord, d):
    return 1 - coord if d else coord


def _side_copies(kind, x_refs, o_refs, send_sems, recv_sems, local_sems, start):
    x, y, c = _me()
    mine = 4 * x + 2 * y + c
    for gi, (x_ref, o_ref) in enumerate(zip(x_refs, o_refs)):
        src_own = x_ref if kind == "gather" else x_ref.at[mine]
        own = pltpu.make_async_copy(src_own, o_ref.at[mine], local_sems.at[gi])
        own.start() if start else own.wait()
        for k, (dx, dy, dc) in enumerate(_FLIPS):
            px, py, pc = _flip(x, dx), _flip(y, dy), _flip(c, dc)
            src = x_ref if kind == "gather" else x_ref.at[4 * px + 2 * py + pc]
            cp = pltpu.make_async_remote_copy(
                src_ref=src, dst_ref=o_ref.at[mine], send_sem=send_sems.at[7 * gi + k], recv_sem=recv_sems.at[7 * gi + k],
                device_id=(px, py, pc), device_id_type=MESH_ID)
            cp.start() if start else cp.wait()


def _call(body, args, side, grid, **kw):
    if side is None:
        return _pcall(body, grid=grid, **kw)(*args)
    kind, arrs = side
    ns, n_in, n_out = len(arrs), len(args), len(kw["out_specs"])
    scratch = list(kw.get("scratch_shapes", []))
    n_scr = len(scratch)

    def edge(at_end):
        conds = [pl.program_id(ax) == (n - 1 if at_end else 0) for ax, n in enumerate(grid)]
        return functools.reduce(jnp.logical_and, conds)

    def wrapped(*refs):
        ins, sx = refs[:n_in], refs[n_in:n_in + ns]
        outs, so = refs[n_in + ns:n_in + ns + n_out], refs[n_in + ns + n_out:n_in + 2 * ns + n_out]
        rest = refs[n_in + 2 * ns + n_out:]
        scr, sems = rest[:n_scr], rest[n_scr:]

        @pl.when(edge(False))
        def _():
            _side_copies(kind, sx, so, *sems, start=True)

        body(*ins, *outs, *scr)

        @pl.when(edge(True))
        def _():
            _side_copies(kind, sx, so, *sems, start=False)

    hbm = pl.BlockSpec(memory_space=pl.ANY)
    side_shapes = [jax.ShapeDtypeStruct(((N_DEV,) if kind == "gather" else ()) + z.shape, z.dtype) for z in arrs]
    kw = dict(kw)
    kw["in_specs"] = list(kw["in_specs"]) + [hbm] * ns
    kw["out_specs"] = list(kw["out_specs"]) + [hbm] * ns
    kw["out_shape"] = list(kw["out_shape"]) + side_shapes
    kw["scratch_shapes"] = scratch + [pltpu.SemaphoreType.DMA((7 * ns,)), pltpu.SemaphoreType.DMA((7 * ns,)),
                                      pltpu.SemaphoreType.DMA((ns,))]
    kw["compiler_params"] = _cp(*["arbitrary"] * len(grid))
    res = _pcall(wrapped, grid=grid, **kw)(*args, *arrs)
    return list(res[:n_out]), list(res[n_out:])


def _dot(a, b, dims=NN, precision=None):
    return lax.dot_general(a, b, (dims, ((), ())), preferred_element_type=F32, precision=precision)


def _mdot(a, b, dims=NN):
    return _dot(a.astype(MM), b.astype(MM), dims)


def _tile(n, pref):
    if n <= pref:
        return n
    for c in range(pref - pref % 128, 0, -128):
        if n % c == 0:
            return c
    return n


def _sigmoid(x):
    return 1.0 / (1.0 + jnp.exp(-x))


def _softplus(x):
    return jnp.maximum(x, 0.0) + jnp.log(1.0 + jnp.exp(-jnp.abs(x)))


def _ln_stats(z):
    mu = jnp.mean(z, axis=-1, keepdims=True)
    zc = z - mu
    var = jnp.mean(zc * zc, axis=-1, keepdims=True)
    return zc, lax.rsqrt(var + LN_EPS)


def matmul(a, b, *, mode, name, tm=512, tn=512, tk=512, out_dtype=F32, scale=None, add=None, add_scale=1.0, side=None,
           split_n=None, ln=None):
    if mode == "nn":
        (m, kk), (_, n) = a.shape, b.shape
        dims = NN
    elif mode == "nt":
        (m, kk), (n, _) = a.shape, b.shape
        dims = NT
    else:
        (kk, m), (_, n) = a.shape, b.shape
        dims = TN
    tm, tn, tk = _tile(m, tm), _tile(n, tn), _tile(kk, tk)
    if mode == "nn":
        a_spec = pl.BlockSpec((tm, tk), lambda i, j, k: (i, k))
        b_spec = pl.BlockSpec((tk, tn), lambda i, j, k: (k, j))
    elif mode == "nt":
        a_spec = pl.BlockSpec((tm, tk), lambda i, j, k: (i, k))
        b_spec = pl.BlockSpec((tn, tk), lambda i, j, k: (j, k))
    else:
        a_spec = pl.BlockSpec((tk, tm), lambda i, j, k: (k, i))
        b_spec = pl.BlockSpec((tk, tn), lambda i, j, k: (k, j))
    nk = kk // tk
    o_spec = pl.BlockSpec((tm, tn), lambda i, j, k: (i, j))
    has_add = add is not None
    n_in = 2 + has_add + (2 if ln else 0)
    assert not (ln and split_n) and (not ln or tn == n) and (not split_n or tn % split_n == 0)

    def body(*refs):
        a_ref, b_ref = refs[:2]
        o_ref = refs[n_in]
        acc_ref = refs[-1]
        i, j, k = pl.program_id(0), pl.program_id(1), pl.program_id(2)

        @pl.when(k == 0)
        def _():
            acc_ref[...] = jnp.zeros_like(acc_ref)

        acc_ref[...] += _mdot(a_ref[...], b_ref[...], dims)
        if ln:
            z_ref, g_ref = refs[n_in - 2:n_in]
            dg_ref, db_ref = refs[n_in + 1:n_in + 3]

            @pl.when((i == 0) & (k == 0))
            def _():
                dg_ref[...] = jnp.zeros_like(dg_ref)
                db_ref[...] = jnp.zeros_like(db_ref)

        @pl.when(k == nk - 1)
        def _():
            r = acc_ref[...]
            if scale is not None:
                r = r * scale
            if has_add:
                r = r + add_scale * refs[2][...].astype(F32)
            if ln:
                r, dg, db = _ln_bwd_tile(r, z_ref[...], g_ref[...])
                dg_ref[...] += dg
                db_ref[...] += db
            if split_n:
                for q in range(tn // split_n):
                    o_ref[q] = r[:, q * split_n:(q + 1) * split_n].astype(out_dtype)
            else:
                o_ref[...] = r.astype(out_dtype)

    vec = pl.BlockSpec((1, n), lambda i, j, k: (0, 0))
    ins = [a, b] + ([add] if has_add else []) + ([ln[0], ln[1].reshape(1, n)] if ln else [])
    in_specs = [a_spec, b_spec] + ([o_spec] if has_add else []) + ([o_spec, vec] if ln else [])
    out_specs, out_shape = [o_spec], [jax.ShapeDtypeStruct((m, n), out_dtype)]
    if split_n:
        out_specs = [pl.BlockSpec((tn // split_n, tm, split_n), lambda i, j, k: (j, i, 0))]
        out_shape = [jax.ShapeDtypeStruct((n // split_n, m, split_n), out_dtype)]
    if ln:
        out_specs += [vec, vec]
        out_shape += [jax.ShapeDtypeStruct((1, n), F32)] * 2
    res = _call(
        body, ins, side, (m // tm, n // tn, nk), name=name, in_specs=in_specs, out_specs=out_specs, out_shape=out_shape,
        scratch_shapes=[pltpu.VMEM((tm, tn), F32)],
        compiler_params=_cp("arbitrary" if ln else "parallel", "parallel", "arbitrary"),
    )
    outs, extra = (res, None) if side is None else res
    outs = outs[0] if len(outs) == 1 else tuple(outs)
    return outs if side is None else (outs, extra)


def ffn_fwd(x, wg, wu, wd, g, b, *, name, tm=512, tf=256, side=None):
    t, d = x.shape
    f = wg.shape[1]
    tm = min(tm, t)
    nj = f // tf

    def body(x_ref, wg_ref, wu_ref, wd_ref, g_ref, b_ref, y_ref, z_ref, hg_ref, hu_ref, xb_ref, acc_ref):
        j = pl.program_id(1)

        @pl.when(j == 0)
        def _():
            xb_ref[...] = x_ref[...].astype(MM)
            acc_ref[...] = jnp.zeros_like(acc_ref)

        xb = xb_ref[...]
        hg = _dot(xb, wg_ref[...])
        hu = _dot(xb, wu_ref[...])
        hg_ref[...] = hg.astype(MM)
        hu_ref[...] = hu.astype(MM)
        act = (hg * _sigmoid(hg) * hu).astype(MM)
        acc_ref[...] += _dot(act, wd_ref[...])

        @pl.when(j == nj - 1)
        def _():
            z = DN_ALPHA * x_ref[...] + 0.5 * acc_ref[...]
            z_ref[...] = z
            zc, rstd = _ln_stats(z)
            y_ref[...] = zc * rstd * g_ref[...] + b_ref[...]

    row = pl.BlockSpec((tm, d), lambda i, j: (i, 0))
    hid = pl.BlockSpec((tm, tf), lambda i, j: (i, j))
    vec = pl.BlockSpec((1, d), lambda i, j: (0, 0))
    return _call(
        body, (x, wg, wu, wd, g.reshape(1, d), b.reshape(1, d)), side, (t // tm, nj), name=name,
        in_specs=[row, pl.BlockSpec((d, tf), lambda i, j: (0, j)), pl.BlockSpec((d, tf), lambda i, j: (0, j)),
                  pl.BlockSpec((tf, d), lambda i, j: (j, 0)), vec, vec],
        out_specs=[row, row, hid, hid],
        out_shape=[jax.ShapeDtypeStruct((t, d), F32), jax.ShapeDtypeStruct((t, d), F32),
                   jax.ShapeDtypeStruct((t, f), MM), jax.ShapeDtypeStruct((t, f), MM)],
        scratch_shapes=[pltpu.VMEM((tm, d), MM), pltpu.VMEM((tm, d), F32)],
        compiler_params=_cp("parallel", "arbitrary"),
    )


def _ln_bwd_tile(dy, z, g):
    zc, rstd = _ln_stats(z)
    xh = zc * rstd
    dxh = dy * g
    m1 = jnp.mean(dxh, axis=-1, keepdims=True)
    m2 = jnp.mean(dxh * xh, axis=-1, keepdims=True)
    return rstd * (dxh - m1 - xh * m2), jnp.sum(dy * xh, axis=0, keepdims=True), jnp.sum(dy, axis=0, keepdims=True)


def ffn_bwd(dz, hg, hu, wg, wu, wd, *, name, tm=512, tf=256, side=None, ln=None):
    t, d = dz.shape
    f = wg.shape[1]
    tm = min(tm, t)
    nj = f // tf
    n_in = 6 + (2 if ln else 0)

    def body(*refs):
        dz_ref, hg_ref, hu_ref, wg_ref, wu_ref, wd_ref = refs[:6]
        dx_ref, act_ref, dhg_ref, dhu_ref = refs[n_in:n_in + 4]
        dfb_ref, acc_ref = refs[-2:]
        i, j = pl.program_id(0), pl.program_id(1)

        @pl.when(j == 0)
        def _():
            dfb_ref[...] = (0.5 * dz_ref[...]).astype(MM)
            acc_ref[...] = jnp.zeros_like(acc_ref)

        hg = hg_ref[...].astype(F32)
        hu = hu_ref[...].astype(F32)
        s = _sigmoid(hg)
        dact = _dot(dfb_ref[...], wd_ref[...], NT)
        sg = hg * s
        act_ref[...] = (sg * hu).astype(MM)
        dhu = (dact * sg).astype(MM)
        dhg = (dact * hu * (s + sg * (1.0 - s))).astype(MM)
        dhu_ref[...] = dhu
        dhg_ref[...] = dhg
        acc_ref[...] += _dot(dhg, wg_ref[...], NT) + _dot(dhu, wu_ref[...], NT)

        if ln:
            z_ref, g_ref = refs[6:8]
            dg_ref, db_ref = refs[n_in + 4:n_in + 6]

            @pl.when((i == 0) & (j == 0))
            def _():
                dg_ref[...] = jnp.zeros_like(dg_ref)
                db_ref[...] = jnp.zeros_like(db_ref)

        @pl.when(j == nj - 1)
        def _():
            dx = DN_ALPHA * dz_ref[...] + acc_ref[...]
            if ln:
                dx, dg, db = _ln_bwd_tile(dx, z_ref[...], g_ref[...])
                dg_ref[...] += dg
                db_ref[...] += db
            dx_ref[...] = dx

    row = pl.BlockSpec((tm, d), lambda i, j: (i, 0))
    hid = pl.BlockSpec((tm, tf), lambda i, j: (i, j))
    vec = pl.BlockSpec((1, d), lambda i, j: (0, 0))
    vshape = jax.ShapeDtypeStruct((1, d), F32)
    return _call(
        body, (dz, hg, hu, wg, wu, wd) + ((ln[0], ln[1].reshape(1, d)) if ln else ()), side, (t // tm, nj), name=name,
        in_specs=[row, hid, hid, pl.BlockSpec((d, tf), lambda i, j: (0, j)), pl.BlockSpec((d, tf), lambda i, j: (0, j)),
                  pl.BlockSpec((tf, d), lambda i, j: (j, 0))] + ([row, vec] if ln else []),
        out_specs=[row, hid, hid, hid] + ([vec, vec] if ln else []),
        out_shape=[jax.ShapeDtypeStruct((t, d), F32)] + [jax.ShapeDtypeStruct((t, f), MM)] * 3 + ([vshape, vshape] if ln else []),
        scratch_shapes=[pltpu.VMEM((tm, d), MM), pltpu.VMEM((tm, d), F32)],
        compiler_params=_cp("arbitrary" if ln else "parallel", "arbitrary"),
    )


def ln_bwd(dy, z, g, *, name, tm=512):
    t, d = z.shape
    tm = min(tm, t)

    def body(dy_ref, z_ref, g_ref, dz_ref, dg_ref, db_ref):
        i = pl.program_id(0)

        @pl.when(i == 0)
        def _():
            dg_ref[...] = jnp.zeros_like(dg_ref)
            db_ref[...] = jnp.zeros_like(db_ref)

        dy = dy_ref[...]
        zc, rstd = _ln_stats(z_ref[...])
        xh = zc * rstd
        dg_ref[...] += jnp.sum(dy * xh, axis=0, keepdims=True)
        db_ref[...] += jnp.sum(dy, axis=0, keepdims=True)
        dxh = dy * g_ref[...]
        m1 = jnp.mean(dxh, axis=-1, keepdims=True)
        m2 = jnp.mean(dxh * xh, axis=-1, keepdims=True)
        dz_ref[...] = rstd * (dxh - m1 - xh * m2)

    row = pl.BlockSpec((tm, d), lambda i: (i, 0))
    vec = pl.BlockSpec((1, d), lambda i: (0, 0))
    return _pcall(
        body, name=name, grid=(t // tm,), in_specs=[row, row, vec], out_specs=[row, vec, vec],
        out_shape=[jax.ShapeDtypeStruct((t, d), F32), jax.ShapeDtypeStruct((1, d), F32), jax.ShapeDtypeStruct((1, d), F32)],
        compiler_params=_cp("arbitrary"),
    )(dy, z, g.reshape(1, d))


def mm_ln_fwd(a, w, res, g, b, *, name, tm=512):
    t, kk = a.shape
    d = w.shape[1]
    tm = min(tm, t)

    def body(a_ref, w_ref, res_ref, g_ref, b_ref, y_ref, z_ref):
        z = DN_ALPHA * res_ref[...] + _mdot(a_ref[...], w_ref[...])
        z_ref[...] = z
        zc, rstd = _ln_stats(z)
        y_ref[...] = zc * rstd * g_ref[...] + b_ref[...]

    row = pl.BlockSpec((tm, d), lambda i: (i, 0))
    vec = pl.BlockSpec((1, d), lambda i: (0, 0))
    return _pcall(
        body, name=name, grid=(t // tm,),
        in_specs=[pl.BlockSpec((tm, kk), lambda i: (i, 0)), pl.BlockSpec((kk, d), lambda i: (0, 0)), row, vec, vec],
        out_specs=[row, row],
        out_shape=[jax.ShapeDtypeStruct((t, d), F32), jax.ShapeDtypeStruct((t, d), F32)],
        compiler_params=_cp("parallel"),
    )(a, w, res, g.reshape(1, d), b.reshape(1, d))


def ple_fwd(y, p, wg, bg, wp, *, name, tm=512):
    t, d = y.shape
    dp = p.shape[1]
    tm = min(tm, t)

    def body(y_ref, p_ref, wg_ref, bg_ref, wp_ref, o_ref):
        yv = y_ref[...]
        gate = _sigmoid(_mdot(yv, wg_ref[...]) + bg_ref[...])
        o_ref[...] = yv + gate * _mdot(p_ref[...], wp_ref[...])

    row = pl.BlockSpec((tm, d), lambda i: (i, 0))
    return _pcall(
        body, name=name, grid=(t // tm,),
        in_specs=[row, pl.BlockSpec((tm, dp), lambda i: (i, 0)), pl.BlockSpec((d, d), lambda i: (0, 0)),
                  pl.BlockSpec((1, d), lambda i: (0, 0)), pl.BlockSpec((dp, d), lambda i: (0, 0))],
        out_specs=row, out_shape=jax.ShapeDtypeStruct((t, d), F32), compiler_params=_cp("parallel"),
    )(y, p, wg, bg.reshape(1, d), wp)


def ple_bwd(do, y, p, wg, bg, wp, z, g, *, name, tm=512):
    t, d = y.shape
    dp = p.shape[1]
    tm = min(tm, t)

    def body(do_ref, y_ref, p_ref, wg_ref, bg_ref, wp_ref, z_ref, g_ref, dz_ref, dt_ref, de_ref, dbg_ref, dg_ref, db_ref):
        i = pl.program_id(0)

        @pl.when(i == 0)
        def _():
            for ref in (dbg_ref, dg_ref, db_ref):
                ref[...] = jnp.zeros_like(ref)

        dov = do_ref[...]
        gate = _sigmoid(_mdot(y_ref[...], wg_ref[...]) + bg_ref[...])
        emb = _mdot(p_ref[...], wp_ref[...])
        dt = dov * emb * gate * (1.0 - gate)
        dbg_ref[...] += jnp.sum(dt, axis=0, keepdims=True)
        dtb = dt.astype(MM)
        dt_ref[...] = dtb
        de_ref[...] = (dov * gate).astype(MM)
        dz, dg, db = _ln_bwd_tile(dov + _dot(dtb, wg_ref[...], NT), z_ref[...], g_ref[...])
        dz_ref[...] = dz
        dg_ref[...] += dg
        db_ref[...] += db

    row = pl.BlockSpec((tm, d), lambda i: (i, 0))
    vec = pl.BlockSpec((1, d), lambda i: (0, 0))
    vshape = jax.ShapeDtypeStruct((1, d), F32)
    return _pcall(
        body, name=name, grid=(t // tm,),
        in_specs=[row, row, pl.BlockSpec((tm, dp), lambda i: (i, 0)), pl.BlockSpec((d, d), lambda i: (0, 0)),
                  vec, pl.BlockSpec((dp, d), lambda i: (0, 0)), row, vec],
        out_specs=[row, row, row, vec, vec, vec],
        out_shape=[jax.ShapeDtypeStruct((t, d), F32), jax.ShapeDtypeStruct((t, d), MM),
                   jax.ShapeDtypeStruct((t, d), MM), vshape, vshape, vshape],
        compiler_params=_cp("arbitrary"),
    )(do, y, p, wg, bg.reshape(1, d), wp, z, g.reshape(1, d))


def loss_fwd_bwd(y, tgt, *, name, tm=512):
    t, d = y.shape
    tm = min(tm, t)

    def body(y_ref, t_ref, l_ref, dy_ref):
        i = pl.program_id(0)

        @pl.when(i == 0)
        def _():
            l_ref[...] = jnp.zeros_like(l_ref)

        err = y_ref[...] - t_ref[...]
        dy_ref[...] = err * (1.0 / d)
        l_ref[...] += (0.5 / d) * jnp.sum(jnp.sum(err * err, axis=1, keepdims=True), axis=0, keepdims=True)

    row = pl.BlockSpec((tm, d), lambda i: (i, 0))
    return _pcall(
        body, name=name, grid=(t // tm,), in_specs=[row, row],
        out_specs=[pl.BlockSpec((1, 128), lambda i: (0, 0)), row],
        out_shape=[jax.ShapeDtypeStruct((1, 128), F32), jax.ShapeDtypeStruct((t, d), F32)],
        compiler_params=_cp("arbitrary"),
    )(y, tgt)


def _shift_dn(x, s, row):
    return x if s == 0 else jnp.where(row >= s, pltpu.roll(x, s, 0), 0.0)


def _shift_up(x, s, row):
    n = x.shape[0]
    return x if s == 0 else jnp.where(row < n - s, pltpu.roll(x, n - s, 0), 0.0)


def _conv_fwd(x, w, row):
    kk = w.shape[0]
    y = w[kk - 1:kk, :] * x
    for j in range(kk - 1):
        y = y + w[j:j + 1, :] * _shift_dn(x, kk - 1 - j, row)
    return y


def _conv_bwd(x, w, dy, row):
    kk = w.shape[0]
    dx = w[kk - 1:kk, :] * dy
    dws = []
    for j in range(kk - 1):
        dx = dx + w[j:j + 1, :] * _shift_up(dy, kk - 1 - j, row)
        dws.append(jnp.sum(dy * _shift_dn(x, kk - 1 - j, row), axis=0, keepdims=True))
    dws.append(jnp.sum(dy * x, axis=0, keepdims=True))
    return dx, jnp.concatenate(dws, axis=0)


def _gelu(x):
    c = math.sqrt(2.0 / math.pi)
    th = jnp.tanh(c * (x + 0.044715 * x * x * x))
    return 0.5 * x * (1.0 + th), th


def _gelu_grad(x, th):
    c = math.sqrt(2.0 / math.pi)
    return 0.5 * (1.0 + th) + 0.5 * x * (1.0 - th * th) * c * (1.0 + 3.0 * 0.044715 * x * x)


def _neg_expm1(y):
    ser = -(y * (1.0 + y * (0.5 + y * (1.0 / 6.0 + y * (1.0 / 24.0 + y * (1.0 / 120.0))))))
    return jnp.where(y > -0.05, ser, 1.0 - jnp.exp(y))


def _attn_head(qh, kk, vv, bias, valid, sink):
    s = _mdot(qh, kk, NT) * (A_HEAD_DIM ** -0.5) - bias
    s = jnp.where(valid, s, NEG)
    m = jnp.maximum(jnp.max(s, axis=-1, keepdims=True), sink)
    pr = jnp.exp(s - m)
    den = jnp.sum(pr, axis=-1, keepdims=True) + jnp.exp(sink - m)
    return pr / den, jnp.exp(sink - m) / den


def _attn_valid(n):
    ji = lax.broadcasted_iota(jnp.int32, (1, 3 * CHUNK), 1)
    return (n * CHUNK + ji - 2 * CHUNK) >= 0


def _attn_group_consts(kh, sk_ref):
    rows = A_GROUP * CHUNK
    ri = lax.broadcasted_iota(jnp.int32, (rows, 3 * CHUNK), 0)
    ji = lax.broadcasted_iota(jnp.int32, (rows, 3 * CHUNK), 1)
    dist = jnp.abs((ri & (CHUNK - 1)) + 2 * CHUNK - ji).astype(F32)
    rcol = lax.broadcasted_iota(jnp.int32, (rows, 1), 0)
    slope = jnp.zeros((rows, 1), F32)
    sink = jnp.zeros((rows, 1), F32)
    for gi in range(A_GROUP):
        h = kh * A_GROUP + gi
        inblk = (rcol >= gi * CHUNK) & (rcol < (gi + 1) * CHUNK)
        slope = jnp.where(inblk, 2.0 ** -(h + 1), slope)
        sink = jnp.where(inblk, sk_ref[h], sink)
    return slope * dist, sink


def _stack_heads(x, kh):
    return jnp.concatenate([x[:, (kh * A_GROUP + gi) * 64:(kh * A_GROUP + gi + 1) * 64] for gi in range(A_GROUP)], axis=0)


def attn_fwd(proj, sinks, bsz, *, name, side=None):
    t = proj.shape[0]
    s_len = t // bsz
    nc = s_len // CHUNK
    pad = 2 * CHUNK

    def body(q_ref, k_ref, v_ref, sk_ref, o_ref, kp_ref, vp_ref):
        kp_ref[0:pad, :] = jnp.zeros((pad, A_KV_WIDTH), F32)
        vp_ref[0:pad, :] = jnp.zeros((pad, A_KV_WIDTH), F32)
        kp_ref[pad:, :] = k_ref[...].astype(F32)
        vp_ref[pad:, :] = v_ref[...].astype(F32)

        consts = [_attn_group_consts(kh, sk_ref) for kh in range(A_KV_HEADS)]

        def chunk(n, carry):
            st = pl.multiple_of(n * CHUNK, CHUNK)
            q = q_ref[pl.ds(st, CHUNK), :].astype(F32)
            kb = kp_ref[pl.ds(st, 3 * CHUNK), :]
            vb = vp_ref[pl.ds(st, 3 * CHUNK), :]
            valid = _attn_valid(n)
            outs = []
            for kh in range(A_KV_HEADS):
                bias, sink = consts[kh]
                pn, _ = _attn_head(_stack_heads(q, kh), kb[:, kh * 64:(kh + 1) * 64], None, bias, valid, sink)
                o = _mdot(pn, vb[:, kh * 64:(kh + 1) * 64])
                outs += [o[gi * CHUNK:(gi + 1) * CHUNK] for gi in range(A_GROUP)]
            o_ref[pl.ds(st, CHUNK), :] = jnp.concatenate(outs, axis=-1)
            return carry

        lax.fori_loop(0, nc, chunk, 0, unroll=2)

    res = _call(
        body, (proj, proj, proj, sinks), side, (bsz,), name=name,
        in_specs=[pl.BlockSpec((s_len, A_WIDTH), lambda b: (b, 0)), pl.BlockSpec((s_len, 128), lambda b: (b, 4)),
                  pl.BlockSpec((s_len, 128), lambda b: (b, 5)), pl.BlockSpec(memory_space=pltpu.SMEM)],
        out_specs=[pl.BlockSpec((s_len, A_WIDTH), lambda b: (b, 0))],
        out_shape=[jax.ShapeDtypeStruct((t, A_WIDTH), F32)],
        scratch_shapes=[pltpu.VMEM((s_len + pad, A_KV_WIDTH), F32), pltpu.VMEM((s_len + pad, A_KV_WIDTH), F32)],
        compiler_params=_cp("parallel"),
    )
    return res[0] if side is None else (res[0][0], res[1])


def attn_bwd(proj, sinks, dcat, bsz, *, name, side=None):
    t = proj.shape[0]
    s_len = t // bsz
    nc = s_len // CHUNK
    pad = 2 * CHUNK

    def body(q_ref, k_ref, v_ref, do_ref, sk_ref, dq_ref, dk_ref, dv_ref, dsk_ref, kp_ref, vp_ref, dkp_ref, dvp_ref):
        kp_ref[0:pad, :] = jnp.zeros((pad, A_KV_WIDTH), F32)
        vp_ref[0:pad, :] = jnp.zeros((pad, A_KV_WIDTH), F32)
        kp_ref[pad:, :] = k_ref[...].astype(F32)
        vp_ref[pad:, :] = v_ref[...].astype(F32)
        dkp_ref[...] = jnp.zeros_like(dkp_ref)
        dvp_ref[...] = jnp.zeros_like(dvp_ref)
        lane = lax.broadcasted_iota(jnp.int32, (1, 128), 1)

        consts = [_attn_group_consts(kh, sk_ref) for kh in range(A_KV_HEADS)]

        def chunk(n, carry):
            dsk = jnp.zeros((1, 128), F32)
            st = pl.multiple_of(n * CHUNK, CHUNK)
            q = q_ref[pl.ds(st, CHUNK), :].astype(F32)
            do = do_ref[pl.ds(st, CHUNK), :]
            kb = kp_ref[pl.ds(st, 3 * CHUNK), :]
            vb = vp_ref[pl.ds(st, 3 * CHUNK), :]
            valid = _attn_valid(n)
            dqs, dks, dvs = [], [], []
            for kh in range(A_KV_HEADS):
                kk = kb[:, kh * 64:(kh + 1) * 64]
                vv = vb[:, kh * 64:(kh + 1) * 64]
                bias, sink = consts[kh]
                qs = _stack_heads(q, kh)
                dos = _stack_heads(do, kh)
                pn, psink = _attn_head(qs, kk, None, bias, valid, sink)
                dp = _mdot(dos, vv, NT)
                rowdot = jnp.sum(pn * dp, axis=-1, keepdims=True)
                ds = pn * (dp - rowdot)
                sink_part = psink * rowdot
                for gi in range(A_GROUP):
                    part = jnp.sum(sink_part[gi * CHUNK:(gi + 1) * CHUNK], axis=0, keepdims=True)
                    dsk = dsk + jnp.where(lane == kh * A_GROUP + gi, -part, 0.0)
                dq = _mdot(ds, kk) * (A_HEAD_DIM ** -0.5)
                dqs += [dq[gi * CHUNK:(gi + 1) * CHUNK] for gi in range(A_GROUP)]
                dks.append(_mdot(ds, qs, TN) * (A_HEAD_DIM ** -0.5))
                dvs.append(_mdot(pn, dos, TN))
            dq_ref[pl.ds(st, CHUNK), :] = jnp.concatenate(dqs, axis=-1)
            dkp_ref[pl.ds(st, 3 * CHUNK), :] += jnp.concatenate(dks, axis=-1)
            dvp_ref[pl.ds(st, 3 * CHUNK), :] += jnp.concatenate(dvs, axis=-1)
            dsk_ref[0] += dsk
            return carry

        dsk_ref[...] = jnp.zeros_like(dsk_ref)
        lax.fori_loop(0, nc, chunk, 0, unroll=2)
        dk_ref[...] = dkp_ref[pad:, :]
        dv_ref[...] = dvp_ref[pad:, :]

    kv = jax.ShapeDtypeStruct((t, A_KV_WIDTH), F32)
    return _call(
        body, (proj, proj, proj, dcat, sinks), side, (bsz,), name=name,
        in_specs=[pl.BlockSpec((s_len, A_WIDTH), lambda b: (b, 0)), pl.BlockSpec((s_len, 128), lambda b: (b, 4)),
                  pl.BlockSpec((s_len, 128), lambda b: (b, 5)), pl.BlockSpec((s_len, A_WIDTH), lambda b: (b, 0)),
                  pl.BlockSpec(memory_space=pltpu.SMEM)],
        out_specs=[pl.BlockSpec((s_len, A_WIDTH), lambda b: (b, 0)), pl.BlockSpec((s_len, 128), lambda b: (b, 0)),
                   pl.BlockSpec((s_len, 128), lambda b: (b, 0)), pl.BlockSpec((1, 1, 128), lambda b: (b, 0, 0))],
        out_shape=[jax.ShapeDtypeStruct((t, A_WIDTH), F32), kv, kv, jax.ShapeDtypeStruct((bsz, 1, 128), F32)],
        scratch_shapes=[pltpu.VMEM((s_len + pad, A_KV_WIDTH), F32)] * 4,
        compiler_params=_cp("parallel"),
    )


def _lru_gates(x, cw, cb, wa, ba, wx, bx, lam, row):
    xc = _conv_fwd(x, cw, row) + cb
    r = _sigmoid(_mdot(xc, wa) + ba)
    i = _sigmoid(_mdot(xc, wx) + bx)
    sp = _softplus(-lam)
    log_a = -RG_C * r * sp
    a = jnp.exp(log_a)
    mult = jnp.sqrt(_neg_expm1(2.0 * log_a))
    return xc, r, i, sp, a, mult


LRU_BLOCK = 128


def _lru_scan_refs(a_ref, u_ref, h_ref, reverse=False):
    nb = a_ref.shape[0] // LRU_BLOCK
    row = lax.broadcasted_iota(jnp.int32, (LRU_BLOCK, 128), 0)

    def block(i, carry):
        bi = nb - 1 - i if reverse else i
        rs = pl.ds(pl.multiple_of(bi * LRU_BLOCK, LRU_BLOCK), LRU_BLOCK)
        a, u = a_ref[rs, :], u_ref[rs, :]
        d = 1
        while d < LRU_BLOCK:
            keep = row < LRU_BLOCK - d if reverse else row >= d
            sh = LRU_BLOCK - d if reverse else d
            a_sh = jnp.where(keep, pltpu.roll(a, sh, 0), 1.0)
            u_sh = jnp.where(keep, pltpu.roll(u, sh, 0), 0.0)
            u = a * u_sh + u
            a = a * a_sh
            d *= 2
        h = u + a * carry
        h_ref[rs, :] = h
        return h[0:1, :] if reverse else h[LRU_BLOCK - 1:LRU_BLOCK, :]

    lax.fori_loop(0, nb, block, jnp.zeros((1, 128), F32))


def _lru_specs(s_len, order):
    def at(f):
        return lambda *g: f(*order(*g))
    return [pl.BlockSpec((s_len, 128), at(lambda b, cb: (b, 6 + cb))), pl.BlockSpec((s_len, 128), at(lambda b, cb: (b, 10 + cb))),
            pl.BlockSpec((4, 128), at(lambda b, cb: (0, cb))), pl.BlockSpec((1, 128), at(lambda b, cb: (0, cb))),
            pl.BlockSpec((1, 128, 128), at(lambda b, cb: (cb, 0, 0))), pl.BlockSpec((1, 128), at(lambda b, cb: (0, cb))),
            pl.BlockSpec((1, 128, 128), at(lambda b, cb: (cb, 0, 0))), pl.BlockSpec((1, 128), at(lambda b, cb: (0, cb))),
            pl.BlockSpec((1, 128), at(lambda b, cb: (0, cb)))]


def lru_fwd(proj, cw, cb, wa, ba, wx, bxb, lam, bsz, *, name):
    t = proj.shape[0]
    s_len = t // bsz

    def body(x_ref, g_ref, cw_ref, cb_ref, wa_ref, ba_ref, wx_ref, bx_ref, lam_ref, y_ref, a_s, u_s):
        row = lax.broadcasted_iota(jnp.int32, (s_len, 128), 0)
        xc, r, i, sp, a, mult = _lru_gates(x_ref[...].astype(F32), cw_ref[...], cb_ref[...], wa_ref[0], ba_ref[...],
                                           wx_ref[0], bx_ref[...], lam_ref[...], row)
        a_s[...] = a
        u_s[...] = mult * (i * xc)
        _lru_scan_refs(a_s, u_s, y_ref)
        y_ref[...] = y_ref[...] * _gelu(g_ref[...].astype(F32))[0]

    return _pcall(
        body, name=name, grid=(bsz, 4), in_specs=_lru_specs(s_len, lambda b, cb: (b, cb)),
        out_specs=pl.BlockSpec((s_len, 128), lambda b, cb: (b, cb)),
        out_shape=jax.ShapeDtypeStruct((t, B_WIDTH), F32), scratch_shapes=[pltpu.VMEM((s_len, 128), F32)] * 2,
        compiler_params=_cp("parallel", "parallel"),
    )(proj, proj, cw, cb.reshape(1, -1), wa, ba.reshape(1, -1), wx, bxb.reshape(1, -1), lam.reshape(1, -1))


def lru_bwd(proj, cw, cb, wa, ba, wx, bxb, lam, dcat, bsz, *, name, side=None):
    t = proj.shape[0]
    s_len = t // bsz

    def body(x_ref, g_ref, cw_ref, cb_ref, wa_ref, ba_ref, wx_ref, bx_ref, lam_ref, dy_ref,
             dx_ref, dg_ref, dcw_ref, dcb_ref, dwa_ref, dba_ref, dwx_ref, dbx_ref, dlam_ref, a_s, u_s, h_s, g_s):
        b = pl.program_id(1)
        row = lax.broadcasted_iota(jnp.int32, (s_len, 128), 0)
        x = x_ref[...].astype(F32)
        lam = lam_ref[...]
        xc, r, i, sp, a, mult = _lru_gates(x, cw_ref[...], cb_ref[...], wa_ref[0], ba_ref[...], wx_ref[0], bx_ref[...],
                                           lam, row)
        ixc = i * xc
        a_s[...] = a
        u_s[...] = mult * ixc
        _lru_scan_refs(a_s, u_s, h_s)
        h = h_s[...]
        gv = g_ref[...].astype(F32)
        gl, th = _gelu(gv)
        dy = dy_ref[...]
        dg_ref[...] = dy * h * _gelu_grad(gv, th)
        a_s[...] = _shift_up(a, 1, row)
        u_s[...] = dy * gl
        _lru_scan_refs(a_s, u_s, g_s, reverse=True)
        gr = g_s[...]
        da = gr * _shift_dn(h, 1, row)
        dmult = gr * ixc
        di = gr * mult * xc
        dxc = gr * mult * i
        dlog_a = da * a - dmult * (a * a) / mult
        dr = dlog_a * (-RG_C * sp)
        dlam = jnp.sum(dlog_a * r, axis=0, keepdims=True) * (RG_C * _sigmoid(-lam))
        dpa = dr * r * (1.0 - r)
        dpx = di * i * (1.0 - i)
        dxc = dxc + _mdot(dpa, wa_ref[0], NT) + _mdot(dpx, wx_ref[0], NT)
        dx, dcw = _conv_bwd(x, cw_ref[...], dxc, row)
        dx_ref[...] = dx

        @pl.when(b == 0)
        def _():
            for ref in (dcw_ref, dcb_ref, dwa_ref, dba_ref, dwx_ref, dbx_ref, dlam_ref):
                ref[...] = jnp.zeros_like(ref)

        dcw_ref[...] += dcw
        dcb_ref[...] += jnp.sum(dxc, axis=0, keepdims=True)
        dwa_ref[0] += _mdot(xc, dpa, TN)
        dwx_ref[0] += _mdot(xc, dpx, TN)
        dba_ref[...] += jnp.sum(dpa, axis=0, keepdims=True)
        dbx_ref[...] += jnp.sum(dpx, axis=0, keepdims=True)
        dlam_ref[...] += dlam

    order = lambda cb, b: (b, cb)
    act = pl.BlockSpec((s_len, 128), lambda cb, b: (b, cb))
    vec = pl.BlockSpec((1, 128), lambda cb, b: (0, cb))
    mat = pl.BlockSpec((1, 128, 128), lambda cb, b: (cb, 0, 0))
    vshape = jax.ShapeDtypeStruct((1, B_WIDTH), F32)
    mshape = jax.ShapeDtypeStruct((4, 128, 128), F32)
    return _call(
        body, (proj, proj, cw, cb.reshape(1, -1), wa, ba.reshape(1, -1), wx, bxb.reshape(1, -1), lam.reshape(1, -1), dcat),
        side, (4, bsz), name=name,
        in_specs=_lru_specs(s_len, order) + [pl.BlockSpec((s_len, 128), lambda cb, b: (b, 4 + cb))],
        out_specs=[act, act, pl.BlockSpec((4, 128), lambda cb, b: (0, cb)), vec, mat, vec, mat, vec, vec],
        out_shape=[jax.ShapeDtypeStruct((t, B_WIDTH), F32), jax.ShapeDtypeStruct((t, B_WIDTH), F32),
                   jax.ShapeDtypeStruct((4, B_WIDTH), F32), vshape, mshape, vshape, mshape, vshape, vshape],
        scratch_shapes=[pltpu.VMEM((s_len, 128), F32)] * 4,
        compiler_params=_cp("parallel", "arbitrary"),
    )


_BDIMS = {"nn": ((2,), (1,)), "nt": ((2,), (2,)), "tn": ((1,), (1,))}
C_QSCALE = C_HEAD_DIM ** -0.5


def _bmm(a, b, mode, exact=False):
    dims = (_BDIMS[mode], ((0,), (0,)))
    if exact:
        return lax.dot_general(a, b, dims, preferred_element_type=F32, precision=lax.Precision.HIGH)
    return lax.dot_general(a.astype(MM), b.astype(MM), dims, preferred_element_type=F32)


def _col(x, idx, lane):
    return jnp.broadcast_to(jnp.sum(jnp.where(lane == idx, x, 0.0), axis=-1, keepdims=True), x.shape)


def _seg_cumsum(g, row):
    pos = row & (CHUNK - 1)
    d = 1
    while d < CHUNK:
        g = g + jnp.where(pos >= d, pltpu.roll(g, d, 0), 0.0)
        d *= 2
    return g


def _seg_cumsum_rev(g, row):
    pos = row & (CHUNK - 1)
    n = g.shape[0]
    d = 1
    while d < CHUNK:
        g = g + jnp.where(pos < CHUNK - d, pltpu.roll(g, n - d, 0), 0.0)
        d *= 2
    return g


def _neumann_inverse(lmat):
    ii = lax.broadcasted_iota(jnp.int32, lmat.shape, 1)
    jj = lax.broadcasted_iota(jnp.int32, lmat.shape, 2)
    x = -lmat
    tm = jnp.where(ii == jj, 1.0, 0.0) + x
    pw = x
    for it in range(5):
        exact = it < 3
        pw = _bmm(pw, pw, "nn", exact=exact)
        tm = tm + _bmm(tm, pw, "nn", exact=exact)
    return tm


def gdc_pre_fwd(proj, cw, bsz, *, name):
    t = proj.shape[0]
    s_len = t // bsz

    def body(x_ref, w_ref, y_ref):
        row = lax.broadcasted_iota(jnp.int32, (s_len, 128), 0)
        c = _conv_fwd(x_ref[...].astype(F32), w_ref[...], row)
        xc = c * _sigmoid(c)
        rn = lax.rsqrt(jnp.sum(xc * xc, axis=-1, keepdims=True) + NORM_EPS)
        y_ref[...] = jnp.where(pl.program_id(1) < 2 * C_HEADS, xc * rn, xc)

    blk = pl.BlockSpec((s_len, 128), lambda b, j: (b, j))
    return _pcall(
        body, name=name, grid=(bsz, 3 * C_HEADS), in_specs=[blk, pl.BlockSpec((4, 128), lambda b, j: (0, j))],
        out_specs=blk, out_shape=jax.ShapeDtypeStruct((t, 3 * C_WIDTH), F32), compiler_params=_cp("parallel", "parallel"),
    )(proj, cw)


def gdc_pre_bwd(proj, cw, dy, dproj, bsz, *, name):
    t = proj.shape[0]
    s_len = t // bsz

    def body(x_ref, w_ref, dq_ref, dk_ref, dv_ref, _, dx_ref, dw_ref):
        row = lax.broadcasted_iota(jnp.int32, (s_len, 128), 0)
        x = x_ref[...].astype(F32)
        c = _conv_fwd(x, w_ref[...], row)
        sg = _sigmoid(c)
        xc = c * sg
        rn = lax.rsqrt(jnp.sum(xc * xc, axis=-1, keepdims=True) + NORM_EPS)
        part = pl.program_id(1) // C_HEADS
        dyv = jnp.where(part == 0, dq_ref[...], jnp.where(part == 1, dk_ref[...], dv_ref[...]))
        xn = xc * rn
        dxc = jnp.where(pl.program_id(1) < 2 * C_HEADS, rn * (dyv - xn * jnp.sum(dyv * xn, axis=-1, keepdims=True)), dyv)
        dc = dxc * (sg * (1.0 + c * (1.0 - sg)))
        dx, dw = _conv_bwd(x, w_ref[...], dc, row)
        dx_ref[...] = dx.astype(MM)
        dw_ref[0] = dw

    blk = pl.BlockSpec((s_len, 128), lambda b, j: (b, j))

    def dy_spec(part):
        return pl.BlockSpec((s_len, 128), lambda b, j: (b, jnp.clip(j - part * C_HEADS, 0, C_HEADS - 1)))

    return _pcall(
        body, name=name, grid=(bsz, 3 * C_HEADS),
        in_specs=[blk, pl.BlockSpec((4, 128), lambda b, j: (0, j)), dy_spec(0), dy_spec(1), dy_spec(2),
                  pl.BlockSpec(memory_space=pl.ANY)],
        out_specs=[blk, pl.BlockSpec((1, 4, 128), lambda b, j: (b, 0, j))],
        out_shape=[jax.ShapeDtypeStruct((t, 4 * C_WIDTH), MM), jax.ShapeDtypeStruct((bsz, 4, 3 * C_WIDTH), F32)],
        input_output_aliases={5: 0}, compiler_params=_cp("parallel", "parallel"),
    )(proj, cw, *dy, dproj)


GDC_GROUP = 16


def _gdc_local(qn, kn, vc, gates, a_log, dtb, h):
    rows = qn.shape[0]
    nc = rows // CHUNK
    row = lax.broadcasted_iota(jnp.int32, (rows, 128), 0)
    lane = lax.broadcasted_iota(jnp.int32, (rows, 128), 1)
    r = {"row": row, "lane": lane}
    r["beta"] = _sigmoid(_col(gates, h, lane))
    r["A"] = jnp.exp(a_log)
    r["pre"] = _col(gates, 8 + h, lane) + dtb
    r["sp"] = _softplus(r["pre"])
    gc = _seg_cumsum(-r["A"] * r["sp"], row)
    sh = (nc, CHUNK, 128)
    q3 = (qn * C_QSCALE).reshape(sh)
    k3 = kn.reshape(sh)
    v3 = vc.reshape(sh)
    beta3 = r["beta"].reshape(sh)
    gc3 = gc.reshape(sh)
    gcl3 = gc3[:, CHUNK - 1:CHUNK, :]
    eg = jnp.exp(gc3)
    ekd = jnp.exp(gcl3 - gc3)
    col64 = gc3[:, :, :CHUNK]
    row64 = jnp.swapaxes(gc3, 1, 2)[:, :CHUNK, :]
    ii = lax.broadcasted_iota(jnp.int32, (nc, CHUNK, CHUNK), 1)
    jj = lax.broadcasted_iota(jnp.int32, (nc, CHUNK, CHUNK), 2)
    tril = ii >= jj
    strict = ii > jj
    dm = jnp.where(tril, jnp.exp(jnp.where(tril, col64 - row64, 0.0)), 0.0)
    kb = k3 * beta3
    lmat = jnp.where(strict, _bmm(kb, k3, "nt") * dm, 0.0)
    attn = _bmm(q3, k3, "nt") * dm
    r.update(q3=q3, k3=k3, v3=v3, beta3=beta3, eg=eg, ekd=ekd, gl=jnp.exp(gcl3), dm=dm, kb=kb, lmat=lmat,
             attn=attn, strict=strict, tril=tril, qg=q3 * eg, kdec=k3 * ekd)
    return r


def _gdc_specs(s_len):
    act = lambda off: pl.BlockSpec((s_len, 128), lambda b, h: (b, off + h))
    smem = pl.BlockSpec(memory_space=pltpu.SMEM)
    return [act(0), act(8), act(16), act(24), pl.BlockSpec((s_len, 128), lambda b, h: (b, 0)), smem, smem,
            pl.BlockSpec((1, 128), lambda b, h: (0, 0))]


def gdc_fwd(qkv, proj, gates, a_log, dtb, ng, bsz, *, name, side=None):
    t = proj.shape[0]
    s_len = t // bsz
    nc = s_len // CHUNK
    grp = min(GDC_GROUP, nc)
    gr = grp * CHUNK

    def body(q_ref, k_ref, v_ref, z_ref, gt_ref, al_ref, dt_ref, ng_ref,
             out_ref, o_ref, tm_ref, st_ref, c_s, b_s, qp_s, op_s, gl_s):
        h = pl.program_id(1)

        def local(gi, carry):
            rs = pl.ds(pl.multiple_of(gi * gr, gr), gr)
            cs = pl.ds(gi * grp, grp)
            r = _gdc_local(q_ref[rs, :], k_ref[rs, :], v_ref[rs, :], gt_ref[rs, :], al_ref[h], dt_ref[h], h)
            tm = _neumann_inverse(r["lmat"])
            tm_ref[0, 0, cs] = tm
            u = _bmm(tm, r["v3"] * r["beta3"], "nn")
            w = _bmm(tm, r["kb"] * r["eg"], "nn")
            c_s[cs] = -_bmm(r["kdec"], w, "tn")
            b_s[cs] = _bmm(r["kdec"], u, "tn")
            qp_s[cs] = r["qg"] - _bmm(r["attn"], w, "nn")
            op_s[cs] = _bmm(r["attn"], u, "nn")
            gl_s[cs] = r["gl"]
            return carry

        lax.fori_loop(0, nc // grp, local, 0)

        def chunk(n, state):
            st = pl.multiple_of(n * CHUNK, CHUNK)
            st_ref[0, 0, n] = state
            o_ref[pl.ds(st, CHUNK), :] = _mdot(qp_s[n], state) + op_s[n]
            return state * gl_s[n] + _mdot(c_s[n], state) + b_s[n]

        lax.fori_loop(0, nc, chunk, jnp.zeros((128, 128), F32))
        o = o_ref[...]
        rms = lax.rsqrt(jnp.mean(o * o, axis=-1, keepdims=True) + NORM_EPS)
        z = z_ref[...].astype(F32)
        out_ref[...] = o * rms * ng_ref[...] * (z * _sigmoid(z))

    blk = pl.BlockSpec((s_len, 128), lambda b, h: (b, h))
    full = jax.ShapeDtypeStruct((t, C_WIDTH), F32)
    return _call(
        body, (qkv, qkv, qkv, proj, gates, a_log, dtb, ng.reshape(1, 128)), side, (bsz, C_HEADS), name=name,
        in_specs=_gdc_specs(s_len),
        out_specs=[blk, blk, pl.BlockSpec((1, 1, nc, CHUNK, CHUNK), lambda b, h: (b, h, 0, 0, 0)),
                   pl.BlockSpec((1, 1, nc, 128, 128), lambda b, h: (b, h, 0, 0, 0))],
        out_shape=[full, full, jax.ShapeDtypeStruct((bsz, C_HEADS, nc, CHUNK, CHUNK), F32),
                   jax.ShapeDtypeStruct((bsz, C_HEADS, nc, 128, 128), F32)],
        scratch_shapes=[pltpu.VMEM((nc, 128, 128), F32)] * 2 + [pltpu.VMEM((nc, CHUNK, 128), F32)] * 2 +
                       [pltpu.VMEM((nc, 1, 128), F32)],
        compiler_params=_cp("parallel", "parallel"),
    )


def gdc_bwd(qkv, proj, gates, a_log, dtb, ng, o_pre, tmat, states, dout, bsz, *, name, side=None):
    t = proj.shape[0]
    s_len = t // bsz
    nc = s_len // CHUNK
    grp = min(GDC_GROUP, nc)
    gr = grp * CHUNK

    def body(q_ref, k_ref, v_ref, z_ref, gt_ref, al_ref, dt_ref, ng_ref, o_ref, tm_ref, st_ref, do_ref,
             dq_ref, dk_ref, dv_ref, dz_ref, dgt_ref, dsm_ref, c_s, e_s, dsp_s, gl_s, dop_s):
        h = pl.program_id(1)
        a_log_h, dtb_h = al_ref[h], dt_ref[h]

        z = z_ref[...].astype(F32)
        sz = _sigmoid(z)
        o = o_ref[...]
        rms = lax.rsqrt(jnp.mean(o * o, axis=-1, keepdims=True) + NORM_EPS)
        on = o * rms
        dout_v = do_ref[...]
        ngv = ng_ref[...]
        dz_ref[...] = (dout_v * on * ngv * (sz * (1.0 + z * (1.0 - sz)))).astype(MM)
        dos = dout_v * (z * sz)
        dng = jnp.sum(dos * on, axis=0, keepdims=True)
        don = dos * ngv
        dop_s[...] = (rms * (don - on * jnp.mean(don * on, axis=-1, keepdims=True))).reshape(nc, CHUNK, 128)

        def local(gi, carry):
            rs = pl.ds(pl.multiple_of(gi * gr, gr), gr)
            cs = pl.ds(gi * grp, grp)
            r = _gdc_local(q_ref[rs, :], k_ref[rs, :], v_ref[rs, :], gt_ref[rs, :], a_log_h, dtb_h, h)
            w = _bmm(tm_ref[0, 0, cs], r["kb"] * r["eg"], "nn")
            c_s[cs] = -_bmm(w, r["kdec"], "tn")
            e_s[cs] = _bmm(r["qg"] - _bmm(r["attn"], w, "nn"), dop_s[cs], "tn")
            gl_s[cs] = r["gl"]
            return carry

        lax.fori_loop(0, nc // grp, local, 0)

        def chunk(i, dstate):
            n = nc - 1 - i
            dsp_s[n] = dstate
            return dstate * gl_s[n] + _mdot(c_s[n], dstate) + e_s[n]

        lax.fori_loop(0, nc, chunk, jnp.zeros((128, 128), F32))

        @pl.when(h == 0)
        def _():
            dgt_ref[...] = jnp.zeros_like(dgt_ref)
            dsm_ref[...] = jnp.zeros_like(dsm_ref)

        def local_bwd(gi, carry):
            d_alog, d_dtb = carry
            rs = pl.ds(pl.multiple_of(gi * gr, gr), gr)
            cs = pl.ds(gi * grp, grp)
            r = _gdc_local(q_ref[rs, :], k_ref[rs, :], v_ref[rs, :], gt_ref[rs, :], a_log_h, dtb_h, h)
            row, lane = r["row"], r["lane"]
            q3, k3, v3, beta3, eg, kb, dm = r["q3"], r["k3"], r["v3"], r["beta3"], r["eg"], r["kb"], r["dm"]
            tm = tm_ref[0, 0, cs]
            u3 = _bmm(tm, v3 * beta3, "nn")
            w3 = _bmm(tm, kb * eg, "nn")
            state, dsp, do3 = st_ref[0, 0, cs], dsp_s[cs], dop_s[cs]
            vn = u3 - _bmm(w3, state, "nn")
            du = _bmm(r["attn"], do3, "tn") + _bmm(r["kdec"], dsp, "nn")
            dat = jnp.where(r["tril"], _bmm(do3, vn, "nt"), 0.0)
            dqg = _bmm(do3, state, "nt")
            dkd = _bmm(vn, dsp, "nt")
            dgl = jnp.sum(jnp.sum(state * dsp, axis=2, keepdims=True), axis=1, keepdims=True)
            dw = -_bmm(du, state, "nt")
            dvb = _bmm(tm, du, "tn")
            dkbg = _bmm(tm, dw, "tn")
            dl = -jnp.where(r["strict"], _bmm(dvb, u3, "nt") + _bmm(dkbg, w3, "nt"), 0.0)
            dml = dl * dm
            dn = dat * dm
            dkb = _bmm(dml, k3, "nn") + dkbg * eg
            dk3 = _bmm(dml, kb, "tn") + _bmm(dn, q3, "tn") + dkd * r["ekd"] + dkb * beta3
            dq3 = dqg * eg + _bmm(dn, k3, "nn")
            e = dl * r["lmat"] + dat * r["attn"]
            ones = jnp.ones((grp, CHUNK, 128), F32)
            colsum = lax.dot_general(e, ones, (_BDIMS["tn"], ((0,), (0,))), preferred_element_type=F32, precision=lax.Precision.HIGH)
            dgc = jnp.sum(e, axis=-1, keepdims=True) - colsum
            dgc = dgc + eg * (jnp.sum(dqg * q3, axis=-1, keepdims=True) + jnp.sum(dkbg * kb, axis=-1, keepdims=True))
            skd = jnp.sum(dkd * r["kdec"], axis=-1, keepdims=True)
            dgcl = jnp.sum(skd, axis=1, keepdims=True) + dgl * r["gl"]
            pos3 = lax.broadcasted_iota(jnp.int32, (grp, CHUNK, 128), 1)
            dgc = dgc - skd + jnp.where(pos3 == CHUNK - 1, dgcl, 0.0)
            dbeta = jnp.sum(dkb * k3, axis=-1, keepdims=True) + jnp.sum(dvb * v3, axis=-1, keepdims=True)
            dg = _seg_cumsum_rev(dgc.reshape(gr, 128), row)
            beta = r["beta"]
            dbl = jnp.broadcast_to(dbeta, (grp, CHUNK, 128)).reshape(gr, 128) * beta * (1.0 - beta)
            dai = dg * (-r["A"]) * _sigmoid(r["pre"])
            dgt_ref[rs, :] += jnp.where(lane == h, dbl, 0.0) + jnp.where(lane == 8 + h, dai, 0.0)
            dq_ref[rs, :] = dq3.reshape(gr, 128) * C_QSCALE
            dk_ref[rs, :] = dk3.reshape(gr, 128)
            dv_ref[rs, :] = (dvb * beta3).reshape(gr, 128)
            return (d_alog + jnp.sum(dg * (-r["sp"]), axis=0, keepdims=True) * r["A"],
                    d_dtb + jnp.sum(dai, axis=0, keepdims=True))

        zero = jnp.zeros((1, 128), F32)
        d_alog, d_dtb = lax.fori_loop(0, nc // grp, local_bwd, (zero, zero))
        r16 = lax.broadcasted_iota(jnp.int32, (16, 128), 0)
        l16 = lax.broadcasted_iota(jnp.int32, (16, 128), 1)
        small = jnp.where((r16 == h) & (l16 == 0), d_alog, 0.0) + jnp.where((r16 == h) & (l16 == 1), d_dtb, 0.0)
        dsm_ref[0] += small + jnp.where(r16 == 8 + h, dng, 0.0)

    blk = pl.BlockSpec((s_len, 128), lambda b, h: (b, h))
    blk3 = lambda off: pl.BlockSpec((s_len, 128), lambda b, h: (b, off + h))
    full = jax.ShapeDtypeStruct((t, C_WIDTH), F32)
    c128 = pltpu.VMEM((nc, CHUNK, 128), F32)
    sq = pltpu.VMEM((nc, 128, 128), F32)
    res = _call(
        body, (qkv, qkv, qkv, proj, gates, a_log, dtb, ng.reshape(1, 128), o_pre, tmat, states, dout), side,
        (bsz, C_HEADS), name=name,
        in_specs=_gdc_specs(s_len) + [blk, pl.BlockSpec((1, 1, nc, CHUNK, CHUNK), lambda b, h: (b, h, 0, 0, 0)),
                                      pl.BlockSpec((1, 1, nc, 128, 128), lambda b, h: (b, h, 0, 0, 0)), blk],
        out_specs=[blk, blk, blk, pl.BlockSpec((s_len, 128), lambda b, h: (b, 3 * C_HEADS + h)),
                   pl.BlockSpec((s_len, 128), lambda b, h: (b, 0)), pl.BlockSpec((1, 16, 128), lambda b, h: (b, 0, 0))],
        out_shape=[full, full, full, jax.ShapeDtypeStruct((t, 4 * C_WIDTH), MM), jax.ShapeDtypeStruct((t, 128), F32),
                   jax.ShapeDtypeStruct((bsz, 16, 128), F32)],
        scratch_shapes=[sq, sq, sq, pltpu.VMEM((nc, 1, 128), F32), c128],
        compiler_params=_cp("parallel", "arbitrary"),
    )
    (dq, dk, dv, dz, dgates, dsm), extra = res if side is not None else (res, None)
    out = ((dq, dk, dv), dz, dgates, dsm)
    return out if side is None else (out, extra)


def join_cols(x, *, name, outs=None, tk=256):
    _, kk, n = x.shape
    tk = _tile8(kk, tk)
    outs = outs or [(0, N_DEV * n, N_DEV * n)]

    def body(x_ref, *o_refs):
        full = jnp.concatenate([x_ref[k] for k in range(N_DEV)], axis=-1)
        for (lo, hi, wd), o_ref in zip(outs, o_refs):
            piece = full[:, lo:hi]
            if wd > hi - lo:
                piece = jnp.concatenate([piece, jnp.zeros((tk, wd - (hi - lo)), piece.dtype)], axis=-1)
            o_ref[...] = piece

    res = _pcall(
        body, name=name, grid=(kk // tk,), in_specs=[pl.BlockSpec((N_DEV, tk, n), lambda i: (0, i, 0))],
        out_specs=[pl.BlockSpec((tk, wd), lambda i: (i, 0)) for _, _, wd in outs],
        out_shape=[jax.ShapeDtypeStruct((kk, wd), x.dtype) for _, _, wd in outs], compiler_params=_cp("parallel"),
    )(x)
    return res if len(outs) > 1 else res[0]


def split_cols(pieces, n, *, name, tk=256):
    kk = pieces[0][0].shape[0]
    tk = _tile8(kk, tk)

    def body(*refs):
        o_ref = refs[-1]
        vals = [r[...][:, :used] for r, (_, used) in zip(refs[:-1], pieces)]
        full = vals[0] if len(vals) == 1 else jnp.concatenate(vals, axis=-1)
        for k in range(N_DEV):
            o_ref[k] = full[:, k * n:(k + 1) * n].astype(MM)

    return _pcall(
        body, name=name, grid=(kk // tk,),
        in_specs=[pl.BlockSpec((tk, arr.shape[1]), lambda i: (i, 0)) for arr, _ in pieces],
        out_specs=pl.BlockSpec((N_DEV, tk, n), lambda i: (0, i, 0)),
        out_shape=jax.ShapeDtypeStruct((N_DEV, kk, n), MM), compiler_params=_cp("parallel"),
    )(*[arr for arr, _ in pieces])


def gather_multi(shards, *, name):
    ng = len(shards)

    def body(*refs):
        x_refs, o_refs = refs[:ng], refs[ng:2 * ng]
        send_sems, recv_sems, local_sems = refs[2 * ng:]
        x, y, c = _me()
        sibling = (x, y, 1 - c)
        chips = [(1 - x, y), (x, 1 - y), (1 - x, 1 - y)]

        def slot(px, py, pc):
            return 4 * px + 2 * py + pc

        def copy(gi, k, block, to, src=None):
            dst = o_refs[gi].at[slot(*block)]
            return pltpu.make_async_remote_copy(
                src_ref=dst if src is None else src, dst_ref=dst, send_sem=send_sems.at[7 * gi + k],
                recv_sem=recv_sems.at[7 * gi + k], device_id=to, device_id_type=MESH_ID)

        own = [pltpu.make_async_copy(x_refs[gi], o_refs[gi].at[slot(x, y, c)], local_sems.at[gi]) for gi in range(ng)]
        for cp in own:
            cp.start()
        first = []
        for gi in range(ng):
            first.append(copy(gi, 0, (x, y, c), sibling, src=x_refs[gi]))
            first += [copy(gi, 1 + j, (x, y, c), (*chip, c), src=x_refs[gi]) for j, chip in enumerate(chips)]
        for cp in first:
            cp.start()
        passed = []
        for j, chip in enumerate(chips):
            for gi in range(ng):
                copy(gi, 1 + j, (*chip, c), (x, y, c)).wait_recv()
                fwd = copy(gi, 4 + j, (*chip, c), sibling)
                fwd.start()
                passed.append(fwd)
        for gi in range(ng):
            copy(gi, 0, sibling, (x, y, c)).wait_recv()
            for j, chip in enumerate(chips):
                copy(gi, 4 + j, (*chip, 1 - c), (x, y, c)).wait_recv()
        for cp in first + passed:
            cp.wait_send()
        for cp in own:
            cp.wait()

    hbm = pl.BlockSpec(memory_space=pl.ANY)
    return _pcall(
        body, name=name, in_specs=[hbm] * ng, out_specs=[hbm] * ng,
        out_shape=[jax.ShapeDtypeStruct((N_DEV,) + s.shape, s.dtype) for s in shards],
        scratch_shapes=[pltpu.SemaphoreType.DMA((7 * ng,)), pltpu.SemaphoreType.DMA((7 * ng,)),
                        pltpu.SemaphoreType.DMA((ng,))],
    )(*shards)


def exchange_multi(parts, *, name):
    ng = len(parts)

    def body(*refs):
        x_refs, o_refs = refs[:ng], refs[ng:2 * ng]
        send_sems, recv_sems, local_sems = refs[2 * ng:]
        x, y, c = _me()
        mine = 4 * x + 2 * y + c
        own = [pltpu.make_async_copy(x_refs[gi].at[mine], o_refs[gi].at[mine], local_sems.at[gi]) for gi in range(ng)]
        for cp in own:
            cp.start()
        copies = []
        for k, (dx, dy, dc) in enumerate(_FLIPS):
            px, py, pc = _flip(x, dx), _flip(y, dy), _flip(c, dc)
            for gi in range(ng):
                cp = pltpu.make_async_remote_copy(
                    src_ref=x_refs[gi].at[4 * px + 2 * py + pc], dst_ref=o_refs[gi].at[mine],
                    send_sem=send_sems.at[7 * gi + k], recv_sem=recv_sems.at[7 * gi + k], device_id=(px, py, pc),
                    device_id_type=MESH_ID)
                cp.start()
                copies.append(cp)
        for cp in copies:
            cp.wait()
        for cp in own:
            cp.wait()

    hbm = pl.BlockSpec(memory_space=pl.ANY)
    return _pcall(
        body, name=name, in_specs=[hbm] * ng, out_specs=[hbm] * ng,
        out_shape=[jax.ShapeDtypeStruct(s.shape, s.dtype) for s in parts],
        scratch_shapes=[pltpu.SemaphoreType.DMA((7 * ng,)), pltpu.SemaphoreType.DMA((7 * ng,)),
                        pltpu.SemaphoreType.DMA((ng,))],
    )(*parts)


def adamw_rows(parts, row0, w, m, v, *, name, tr=256):
    r, cdim = w.shape
    tr = _tile8(math.gcd(r, row0) if row0 else r, tr)
    blk0 = row0 // tr

    def body(p_ref, w_ref, m_ref, v_ref, g_ref, d_ref, mo_ref, vo_ref):
        g = p_ref[0].astype(F32)
        for j in range(1, N_DEV):
            g = g + p_ref[j].astype(F32)
        g_ref[...] = g
        mn = ADAM_B1 * m_ref[...] + (1.0 - ADAM_B1) * g
        vn = ADAM_B2 * v_ref[...] + (1.0 - ADAM_B2) * (g * g)
        mo_ref[...] = mn
        vo_ref[...] = vn
        m_hat = mn / (1.0 - ADAM_B1 ** ADAM_STEP)
        v_hat = vn / (1.0 - ADAM_B2 ** ADAM_STEP)
        d_ref[...] = -ADAM_LR * (m_hat / (jnp.sqrt(v_hat) + ADAM_EPS) + ADAM_WD * w_ref[...])

    blk = pl.BlockSpec((tr, cdim), lambda i: (i, 0))
    shp = jax.ShapeDtypeStruct((r, cdim), F32)
    return _pcall(
        body, name=name, grid=(r // tr,),
        in_specs=[pl.BlockSpec((N_DEV, tr, cdim), lambda i: (0, blk0 + i, 0)), blk, blk, blk],
        out_specs=[blk, blk, blk, blk], out_shape=[shp, shp, shp, shp], compiler_params=_cp("parallel"),
    )(parts, w, m, v)


def _tile8(n, pref):
    for c in range(min(pref, n) - min(pref, n) % 16, 0, -16):
        if n % c == 0:
            return c
    return n


REPL = ["ple_bg", "a_sinks", "b_conv_b", "b_wa", "b_ba", "b_wx", "b_bx", "b_lam", "c_a_log", "c_dt_bias", "c_norm_g"]
WEIGHTS = ["ffn1_wg", "ffn1_wu", "ffn1_wd", "ffn2_wg", "ffn2_wu", "ffn2_wd", "ln_g", "ln_b", "ple_wg", "ple_bg", "ple_wp",
           "ab_w_in", "a_sinks", "b_conv_w", "b_conv_b", "b_wa", "b_ba", "b_wx", "b_bx", "b_lam", "ab_w_out", "c_w_in",
           "c_conv_w", "c_a_log", "c_dt_bias", "c_norm_g", "c_w_out"]
SMALL_NAMES = ["ln_g", "ln_b", "b_conv_w", "c_conv_w"]
SMALL_F32 = 73728
LANES = 128
FFN_TM = 512
FFN_TF = 1408


def _join(blocks, axis):
    moved = jnp.moveaxis(blocks, 0, axis)
    shp = list(moved.shape)
    return moved.reshape(shp[:axis] + [shp[axis] * shp[axis + 1]] + shp[axis + 2:])


def _split(full, axis):
    shp = list(full.shape)
    return jnp.moveaxis(full.reshape(shp[:axis] + [N_DEV, shp[axis] // N_DEV] + shp[axis + 1:]), axis, 0)


def _dense_blocks(w):
    z = jnp.zeros((4, 2, 64, 2, 64), w.dtype)
    w4 = w.reshape(4, 2, 64, 64)
    z = z.at[:, 0, :, 0, :].set(w4[:, 0]).at[:, 1, :, 1, :].set(w4[:, 1])
    return z.reshape(4, 128, 128)


def _diag_blocks(d):
    d5 = d.reshape(4, 2, 64, 2, 64)
    return jnp.stack([d5[:, 0, :, 0, :], d5[:, 1, :, 1, :]], axis=1).reshape(8, 64, 64)


def _flat_pad(arrs, dtype, total):
    flat = jnp.concatenate([z.astype(dtype).reshape(-1) for z in arrs])
    return jnp.pad(flat, (0, total - flat.shape[0]))


def _flat8_pad(arrs, dtype, total):
    flat = jnp.concatenate([z.astype(dtype).reshape(N_DEV, -1) for z in arrs], axis=1)
    return jnp.pad(flat, ((0, 0), (0, total - flat.shape[1])))


def _take(flat, names, shapes):
    out, off = {}, 0
    for n in names:
        sz = int(np.prod(shapes[n]))
        out[n] = flat[..., off:off + sz].reshape(flat.shape[:-1] + tuple(shapes[n]))
        off += sz
    return out


def kernel(x, p, ffn1_wg, ffn1_wu, ffn1_wd, ffn2_wg, ffn2_wu, ffn2_wd, ln_g, ln_b, ple_wg, ple_bg, ple_wp, ab_w_in, a_sinks, b_conv_w, b_conv_b, b_wa, b_ba, b_wx, b_bx, b_lam, ab_w_out, c_w_in, c_conv_w, c_a_log, c_dt_bias, c_norm_g, c_w_out, loss_target, m_ffn1_wg, m_ffn1_wu, m_ffn1_wd, m_ffn2_wg, m_ffn2_wu, m_ffn2_wd, m_ln_g, m_ln_b, m_ple_wg, m_ple_bg, m_ple_wp, m_ab_w_in, m_a_sinks, m_b_conv_w, m_b_conv_b, m_b_wa, m_b_ba, m_b_wx, m_b_bx, m_b_lam, m_ab_w_out, m_c_w_in, m_c_conv_w, m_c_a_log, m_c_dt_bias, m_c_norm_g, m_c_w_out, v_ffn1_wg, v_ffn1_wu, v_ffn1_wd, v_ffn2_wg, v_ffn2_wu, v_ffn2_wd, v_ln_g, v_ln_b, v_ple_wg, v_ple_bg, v_ple_wp, v_ab_w_in, v_a_sinks, v_b_conv_w, v_b_conv_b, v_b_wa, v_b_ba, v_b_wx, v_b_bx, v_b_lam, v_ab_w_out, v_c_w_in, v_c_conv_w, v_c_a_log, v_c_dt_bias, v_c_norm_g, v_c_w_out):
    a = dict(locals())
    return _step3(a)


def _step3(a):
    x, p = a["x"], a["p"]
    bsz, s_len, d = x.shape
    t = bsz * s_len
    x2 = x.reshape(t, d)
    tgt = a["loss_target"].reshape(t, d)
    p2 = p.reshape(DEPTH, t, D_PLE)
    shapes = {n: a[n].shape for n in WEIGHTS}
    n_small = sum(int(np.prod(shapes[n])) for n in SMALL_NAMES)
    small_all = SMALL_NAMES + REPL
    f_ff = shapes["ffn1_wg"][2]
    c_cols = shapes["c_w_in"][2]
    wide = dict(tm=1024, tn=1408, tk=1024)
    tall = dict(tm=1408, tn=1024, tk=1024)

    def cast(z):
        return z.astype(MM)

    def ffn_shards(which, l):
        return [cast(a[which + "_wg"][l]), cast(a[which + "_wu"][l]), cast(a[which + "_wd"][l])]

    def ffn_weights(gat, tag):
        return (join_cols(gat[0], name=f"join_{tag}_wg"), join_cols(gat[1], name=f"join_{tag}_wu"),
                gat[2].reshape(N_DEV * gat[2].shape[1], D_MODEL))

    def rows_full(gat):
        return gat.reshape(N_DEV * gat.shape[1], D_MODEL)

    small_send = _flat_pad([a[n] for n in SMALL_NAMES], F32, 32 * LANES).reshape(32, LANES)
    g0 = gather_multi(ffn_shards("ffn1", 0) + [small_send], name="gather_first")
    ws = _take(g0[3].reshape(N_DEV, -1), SMALL_NAMES, shapes)
    small = {n: _join(ws[n], 2) for n in SMALL_NAMES}
    ln_g, ln_b = small["ln_g"], small["ln_b"]
    wa_d, wx_d = _dense_blocks(a["b_wa"][0]), _dense_blocks(a["b_wx"][0])
    lru_w = (small["b_conv_w"][0], a["b_conv_b"][0], wa_d, a["b_ba"][0], wx_d, a["b_bx"][0], a["b_lam"][0])
    gdc_w = (a["c_a_log"][0], a["c_dt_bias"][0], a["c_norm_g"][0])
    wf = {("ffn1", 0): ffn_weights(g0[:3], "ffn1_0")}

    s0 = {"x0": x2}
    u1 = ffn_shards("ffn2", 0)
    side = ("gather", u1[:2] + [cast(a["ab_w_in"][0]), cast(a["ab_w_out"][0])])
    (s0["y1"], s0["z1"], s0["hg1"], s0["hu1"]), got = ffn_fwd(x2, *wf["ffn1", 0], ln_g[0, 0], ln_b[0, 0],
                                                             name="ffn1_fwd_0", tm=FFN_TM, tf=FFN_TF, side=side)
    ab_w_in, ab_w_out = join_cols(got[2], name="join_ab_in"), rows_full(got[3])
    s0["proj"] = matmul(s0["y1"], ab_w_in, mode="nn", out_dtype=MM, name="ab_in_fwd", tn=896, tk=1024)
    ya, got_ple = attn_fwd(s0["proj"], a["a_sinks"][0], bsz, name="attn_fwd",
                           side=("gather", [u1[2], cast(a["ple_wg"][0]), cast(a["ple_wp"][0])]))
    wf["ffn2", 0] = ffn_weights(got[:2] + got_ple[:1], "ffn2_0")
    got_ple = got_ple[1:]
    yb = lru_fwd(s0["proj"], *lru_w, bsz, name="lru_fwd")
    s0["mix"] = jnp.concatenate([ya, yb], axis=1)
    s0["y2"], s0["z2"] = mm_ln_fwd(s0["mix"], ab_w_out, s0["y1"], ln_g[0, 1], ln_b[0, 1], name="mix_out_fwd_0")
    side = ("gather", ffn_shards("ffn1", 1))
    (s0["y3"], s0["z3"], s0["hg2"], s0["hu2"]), got = ffn_fwd(s0["y2"], *wf["ffn2", 0], ln_g[0, 2], ln_b[0, 2],
                                                             name="ffn2_fwd_0", tm=FFN_TM, tf=FFN_TF, side=side)
    wf["ffn1", 1] = ffn_weights(got, "ffn1_1")
    ple_wg = [rows_full(got_ple[0]), None]
    ple_wp = [_join(got_ple[1], 1), None]
    h1 = ple_fwd(s0["y3"], p2[0], ple_wg[0], a["ple_bg"][0], ple_wp[0], name="ple_fwd_0")

    s1 = {"x0": h1}
    side = ("gather", [cast(a["c_w_in"][0]), cast(a["c_w_out"][0])])
    (s1["y1"], s1["z1"], s1["hg1"], s1["hu1"]), got = ffn_fwd(h1, *wf["ffn1", 1], ln_g[1, 0], ln_b[1, 0],
                                                             name="ffn1_fwd_1", tm=FFN_TM, tf=FFN_TF, side=side)
    c_in_main, c_in_gate = join_cols(got[0], name="join_c_in", outs=[(0, 4 * C_WIDTH, 4 * C_WIDTH),
                                                                      (4 * C_WIDTH, 4 * C_WIDTH + 2 * C_HEADS, LANES)])
    c_w_out = rows_full(got[1])
    s1["proj"] = matmul(s1["y1"], c_in_main, mode="nn", out_dtype=MM, name="c_in_fwd", tm=1024, tn=2048, tk=1024)
    s1["gates"] = matmul(s1["y1"], c_in_gate, mode="nn", name="c_gate_fwd", tk=1024)
    s1["qkv"] = gdc_pre_fwd(s1["proj"], small["c_conv_w"][0], bsz, name="gdc_pre_fwd")
    side = ("gather", ffn_shards("ffn2", 1) + [cast(a["ple_wg"][1]), cast(a["ple_wp"][1])])
    (s1["mix"], s1["o_pre"], s1["tmat"], s1["states"]), got = gdc_fwd(
        s1["qkv"], s1["proj"], s1["gates"], *gdc_w, bsz, name="gdc_fwd", side=side)
    wf["ffn2", 1] = ffn_weights(got[:3], "ffn2_1")
    ple_wg[1], ple_wp[1] = rows_full(got[3]), _join(got[4], 1)
    s1["y2"], s1["z2"] = mm_ln_fwd(s1["mix"], c_w_out, s1["y1"], ln_g[1, 1], ln_b[1, 1], name="mix_out_fwd_1")
    s1["y3"], s1["z3"], s1["hg2"], s1["hu2"] = ffn_fwd(s1["y2"], *wf["ffn2", 1], ln_g[1, 2], ln_b[1, 2], name="ffn2_fwd_1",
                                                           tm=FFN_TM, tf=FFN_TF)
    h2 = ple_fwd(s1["y3"], p2[1], ple_wg[1], a["ple_bg"][1], ple_wp[1], name="ple_fwd_1")
    loss_part, dh = loss_fwd_bwd(h2, tgt, name="loss")

    def ffn_parts(xin, act, dhg, dhu, dz, tag):
        dwg = matmul(xin, dhg, mode="tn", out_dtype=MM, split_n=f_ff, name=f"{tag}_wg_grad", **wide)
        dwu = matmul(xin, dhu, mode="tn", out_dtype=MM, split_n=f_ff, name=f"{tag}_wu_grad", **wide)
        dwd = matmul(act, dz, mode="tn", scale=0.5, out_dtype=MM, name=f"{tag}_wd_grad", **tall)
        return [dwg, dwu, dwd.reshape(N_DEV, f_ff, D_MODEL)]

    def ple_parts(i, s, dt, de):
        gwg = matmul(s["y3"], dt, mode="tn", out_dtype=MM, name=f"ple_wg_grad_{i}", tm=1024, tn=1024)
        gwp = matmul(p2[i], de, mode="tn", out_dtype=MM, name=f"ple_wp_grad_{i}", tn=1024)
        return [gwg.reshape(N_DEV, D_MODEL // N_DEV, D_MODEL), _split(gwp, 1)]

    gln = {"ln_g": [None, None], "ln_b": [None, None]}
    gple_bg = [None, None]

    dz3, dt, de, dbg, dg2, db2 = ple_bwd(dh, s1["y3"], p2[1], ple_wg[1], a["ple_bg"][1], ple_wp[1], s1["z3"], ln_g[1, 2],
                                         name="ple_bwd_1")
    gple_bg[1] = dbg[0]
    parts_ple1 = ple_parts(1, s1, dt, de)
    dy2, act, dhg, dhu = ffn_bwd(dz3, s1["hg2"], s1["hu2"], *wf["ffn2", 1], name="ffn2_bwd_1", tm=FFN_TM, tf=FFN_TF)
    parts_ffn2_1 = ffn_parts(s1["y2"], act, dhg, dhu, dz3, "ffn2_1")
    dz2, dg1, db1 = ln_bwd(dy2, s1["z2"], ln_g[1, 1], name="ln1_bwd_1")
    dmix = matmul(dz2, c_w_out, mode="nt", name="c_out_bwd", tn=1024, tk=1024)
    parts_c_out = matmul(s1["mix"], dz2, mode="tn", out_dtype=MM, name="c_out_grad", tm=1024, tn=1024).reshape(
        N_DEV, D_MODEL // N_DEV, D_MODEL)
    (dqkv, dzc, dgates, dsm), recv1 = gdc_bwd(s1["qkv"], s1["proj"], s1["gates"], *gdc_w, s1["o_pre"],
                                              s1["tmat"], s1["states"], dmix, bsz, name="gdc_bwd",
                                              side=("exchange", parts_ffn2_1 + parts_ple1))
    dproj, dccw = gdc_pre_bwd(s1["proj"], small["c_conv_w"][0], dqkv, dzc, bsz, name="gdc_pre_bwd")
    dgb = dgates.astype(MM)
    dy1 = matmul(dproj, c_in_main, mode="nt", add=dz2, add_scale=DN_ALPHA, name="c_in_bwd", tn=1024, tk=4096)
    dz1, dg0, db0 = matmul(dgb, c_in_gate, mode="nt", add=dy1, name="c_gate_bwd", tn=1024, ln=(s1["z1"], ln_g[1, 0]))
    g_c_main = matmul(s1["y1"], dproj, mode="tn", name="c_in_grad", tm=1024, tn=1024, tk=1024)
    g_c_gate = matmul(s1["y1"], dgb, mode="tn", name="c_gate_grad", tm=1024)
    parts_c_in = split_cols([(g_c_main, 4 * C_WIDTH), (g_c_gate, 2 * C_HEADS)], c_cols, name="split_c_in")
    (dh, act, dhg, dhu), recv_c = ffn_bwd(dz1, s1["hg1"], s1["hu1"], *wf["ffn1", 1], name="ffn1_bwd_1", tm=FFN_TM,
                                          tf=FFN_TF, side=("exchange", [parts_c_in, parts_c_out]))
    parts_ffn1_1 = ffn_parts(s1["x0"], act, dhg, dhu, dz1, "ffn1_1")
    gln["ln_g"][1] = jnp.concatenate([dg0, dg1, dg2], axis=0)
    gln["ln_b"][1] = jnp.concatenate([db0, db1, db2], axis=0)

    dz3, dt, de, dbg, dg2, db2 = ple_bwd(dh, s0["y3"], p2[0], ple_wg[0], a["ple_bg"][0], ple_wp[0], s0["z3"], ln_g[0, 2],
                                         name="ple_bwd_0")
    gple_bg[0] = dbg[0]
    parts_ple0 = ple_parts(0, s0, dt, de)
    (dy2, act, dhg, dhu), recv2 = ffn_bwd(dz3, s0["hg2"], s0["hu2"], *wf["ffn2", 0], name="ffn2_bwd_0", tm=FFN_TM,
                                          tf=FFN_TF, side=("exchange", parts_ffn1_1))
    parts_ffn2_0 = ffn_parts(s0["y2"], act, dhg, dhu, dz3, "ffn2_0")
    dz2, dg1, db1 = ln_bwd(dy2, s0["z2"], ln_g[0, 1], name="ln1_bwd_0")
    dmix = matmul(dz2, ab_w_out, mode="nt", name="ab_out_bwd", tn=1024, tk=1024)
    parts_ab_out = matmul(s0["mix"], dz2, mode="tn", out_dtype=MM, name="ab_out_grad", tm=1024, tn=1024).reshape(
        N_DEV, D_MODEL // N_DEV, D_MODEL)
    (dq, dk, dv, dsk), recv3a = attn_bwd(s0["proj"], a["a_sinks"][0], dmix, bsz, name="attn_bwd",
                                         side=("exchange", parts_ffn2_0[:2]))
    (dbx, dbgate, dcw, dcb, dwa, dba, dwx, dbxb, dlam), recv3b = lru_bwd(
        s0["proj"], *lru_w, dmix, bsz, name="lru_bwd", side=("exchange", [parts_ffn2_0[2]] + parts_ple0 + [parts_ab_out]))
    dproj = jnp.concatenate([dq, dk, dv, dbx, dbgate], axis=1).astype(MM)
    dz1, dg0, db0 = matmul(dproj, ab_w_in, mode="nt", add=dz2, add_scale=DN_ALPHA, name="ab_in_bwd", tn=1024, tk=1792,
                           ln=(s0["z1"], ln_g[0, 0]))
    parts_ab_in = matmul(s0["y1"], dproj, mode="tn", out_dtype=MM, split_n=AB_PROJ // N_DEV, name="ab_in_grad",
                         tm=1024, tn=896)
    gln["ln_g"][0] = jnp.concatenate([dg0, dg1, dg2], axis=0)
    gln["ln_b"][0] = jnp.concatenate([db0, db1, db2], axis=0)

    dsm_sum = jnp.sum(dsm, axis=0)
    full = dict(ln_g=jnp.stack(gln["ln_g"]), ln_b=jnp.stack(gln["ln_b"]), b_conv_w=dcw[None],
                c_conv_w=jnp.sum(dccw, axis=0)[None], ple_bg=jnp.stack(gple_bg),
                a_sinks=jnp.sum(dsk, axis=0)[:, :A_HEADS], b_conv_b=dcb, b_wa=_diag_blocks(dwa)[None], b_ba=dba,
                b_wx=_diag_blocks(dwx)[None], b_bx=dbxb, b_lam=dlam, c_a_log=dsm_sum[None, :C_HEADS, 0],
                c_dt_bias=dsm_sum[None, :C_HEADS, 1], c_norm_g=jnp.sum(dsm_sum[C_HEADS:], axis=0)[None])
    small_rows = SMALL_F32 // LANES
    repl_flat = _flat_pad([full[n] for n in REPL], F32, SMALL_F32 - n_small)
    small8 = jnp.concatenate([_flat8_pad([_split(full[n], 2) for n in SMALL_NAMES], F32, n_small),
                              jnp.broadcast_to(repl_flat, (N_DEV,) + repl_flat.shape)], axis=1)
    (dh, act, dhg, dhu), recv3c = ffn_bwd(dz1, s0["hg1"], s0["hu1"], *wf["ffn1", 0], name="ffn1_bwd_0", tm=FFN_TM,
                                          tf=FFN_TF, side=("exchange", [parts_ab_in, small8.reshape(N_DEV, small_rows, LANES)]))
    grad_x = dh.reshape(bsz, s_len, d)

    parts_wg = matmul(s0["x0"], dhg, mode="tn", out_dtype=MM, split_n=f_ff, name="ffn1_0_wg_grad", **wide)
    parts_wu, recv4a = matmul(s0["x0"], dhu, mode="tn", out_dtype=MM, split_n=f_ff, name="ffn1_0_wu_grad",
                              side=("exchange", [parts_wg]), **wide)
    dwd, recv4b = matmul(act, dz1, mode="tn", scale=0.5, out_dtype=MM, name="ffn1_0_wd_grad",
                         side=("exchange", [parts_wu]), **tall)
    recv4c = exchange_multi([dwd.reshape(N_DEV, f_ff, D_MODEL)], name="exchange_last")

    def upd(parts, n, l, shape2d, **kw):
        wmv = [a[pre + n][l].reshape(shape2d) for pre in ("", "m_", "v_")]
        return adamw_rows(parts, 0, *wmv, name=f"adamw_{n}_{l}", **kw)

    def upd_ffn(parts, which, l):
        return {(which + "_wg", l): upd(parts[0], which + "_wg", l, (D_MODEL, f_ff)),
                (which + "_wu", l): upd(parts[1], which + "_wu", l, (D_MODEL, f_ff)),
                (which + "_wd", l): upd(parts[2], which + "_wd", l, (f_ff, D_MODEL), tr=176)}

    rows8 = D_MODEL // N_DEV
    res = {}
    res.update(upd_ffn(recv1[:3], "ffn2", 1))
    res["ple_wg", 1] = upd(recv1[3], "ple_wg", 1, (rows8, D_MODEL), tr=128)
    res["ple_wp", 1] = upd(recv1[4], "ple_wp", 1, (D_PLE, LANES))
    res.update(upd_ffn(recv2, "ffn1", 1))
    res["c_w_in", 0] = upd(recv_c[0], "c_w_in", 0, (D_MODEL, c_cols))
    res["c_w_out", 0] = upd(recv_c[1], "c_w_out", 0, (rows8, D_MODEL), tr=128)
    res.update(upd_ffn(recv3a + recv3b[:1], "ffn2", 0))
    res["ple_wg", 0] = upd(recv3b[1], "ple_wg", 0, (rows8, D_MODEL), tr=128)
    res["ple_wp", 0] = upd(recv3b[2], "ple_wp", 0, (D_PLE, LANES))
    res["ab_w_out", 0] = upd(recv3b[3], "ab_w_out", 0, (rows8, D_MODEL), tr=128)
    res.update(upd_ffn(recv4a + recv4b + list(recv4c), "ffn1", 0))
    res["ab_w_in", 0] = upd(recv3c[0], "ab_w_in", 0, (D_MODEL, AB_PROJ // N_DEV))
    res_small = adamw_rows(recv3c[1], 0, *[_flat_pad([a[pre + n] for n in small_all], F32, SMALL_F32).reshape(
        small_rows, LANES) for pre in ("", "m_", "v_")], name="adamw_small", tr=small_rows)
    kinds = []
    for k in range(4):
        kd = _take(res_small[k].reshape(-1), small_all, shapes)
        for n in WEIGHTS:
            if n not in kd:
                kd[n] = jnp.stack([res[n, l][k] for l in range(shapes[n][0])]).reshape(shapes[n])
        kinds.append(kd)
    loss = lax.psum(loss_part[0, 0], ("x", "y", "c"))
    return (loss, grad_x, *[kinds[0][n] for n in WEIGHTS], *[kinds[1][n] for n in WEIGHTS],
            *[kinds[2][n] for n in WEIGHTS], *[kinds[3][n] for n in WEIGHTS])
```

```python
import functools
import math

import numpy as np
import jax
import jax.numpy as jnp
from jax import lax
from jax.experimental import pallas as pl
from jax.experimental.pallas import tpu as pltpu

F32 = jnp.float32
MM = jnp.bfloat16
HI = lax.Precision.HIGHEST

D_MODEL = 1024
D_FF = 2816
D_PLE = 256
DEPTH = 2
CHUNK = 64
A_HEADS = 8
A_KV_HEADS = 2
A_GROUP = 4
A_HEAD_DIM = 64
A_WIDTH = 512
A_KV_WIDTH = 128
B_WIDTH = 512
B_BLOCK = 64
RG_C = 8.0
AB_PROJ = 1792
C_HEADS = 8
C_HEAD_DIM = 128
C_WIDTH = 1024
DN_ALPHA = (2.0 * DEPTH) ** 0.25
LN_EPS = 1e-5
NORM_EPS = 1e-6
NEG = -1e30
ADAM_LR = 0.001
ADAM_B1 = 0.9
ADAM_B2 = 0.999
ADAM_EPS = 1e-08
ADAM_WD = 0.01
ADAM_STEP = 10
N_DEV = 8
VMEM_LIMIT = 56 * 1024 * 1024

NN = ((1,), (0,))
NT = ((1,), (1,))
TN = ((0,), (0,))


def _pcall(body, **kw):
    return pl.pallas_call(body, **kw)


def _cp(*sem):
    return pltpu.CompilerParams(dimension_semantics=sem, vmem_limit_bytes=VMEM_LIMIT)


MESH_ID = pl.DeviceIdType.MESH
_FLIPS = [(0, 0, 1), (1, 0, 0), (0, 1, 0), (1, 1, 0), (1, 0, 1), (0, 1, 1), (1, 1, 1)]


def _me():
    return lax.axis_index("x"), lax.axis_index("y"), lax.axis_index("c")


def _flip(coord, d):
    return 1 - coord if d else coord


def _side_copies(kind, x_refs, o_refs, send_sems, recv_sems, local_sems, start):
    x, y, c = _me()
    mine = 4 * x + 2 * y + c
    for gi, (x_ref, o_ref) in enumerate(zip(x_refs, o_refs)):
        src_own = x_ref if kind == "gather" else x_ref.at[mine]
        own = pltpu.make_async_copy(src_own, o_ref.at[mine], local_sems.at[gi])
        own.start() if start else own.wait()
        for k, (dx, dy, dc) in enumerate(_FLIPS):
            px, py, pc = _flip(x, dx), _flip(y, dy), _flip(c, dc)
            src = x_ref if kind == "gather" else x_ref.at[4 * px + 2 * py + pc]
            cp = pltpu.make_async_remote_copy(
                src_ref=src, dst_ref=o_ref.at[mine], send_sem=send_sems.at[7 * gi + k], recv_sem=recv_sems.at[7 * gi + k],
                device_id=(px, py, pc), device_id_type=MESH_ID)
            cp.start() if start else cp.wait()


def _call(body, args, side, grid, **kw):
    if side is None:
        return _pcall(body, grid=grid, **kw)(*args)
    kind, arrs = side
    ns, n_in, n_out = len(arrs), len(args), len(kw["out_specs"])
    scratch = list(kw.get("scratch_shapes", []))
    n_scr = len(scratch)

    def edge(at_end):
        conds = [pl.program_id(ax) == (n - 1 if at_end else 0) for ax, n in enumerate(grid)]
        return functools.reduce(jnp.logical_and, conds)

    def wrapped(*refs):
        ins, sx = refs[:n_in], refs[n_in:n_in + ns]
        outs, so = refs[n_in + ns:n_in + ns + n_out], refs[n_in + ns + n_out:n_in + 2 * ns + n_out]
        rest = refs[n_in + 2 * ns + n_out:]
        scr, sems = rest[:n_scr], rest[n_scr:]

        @pl.when(edge(False))
        def _():
            _side_copies(kind, sx, so, *sems, start=True)

        body(*ins, *outs, *scr)

        @pl.when(edge(True))
        def _():
            _side_copies(kind, sx, so, *sems, start=False)

    hbm = pl.BlockSpec(memory_space=pl.ANY)
    side_shapes = [jax.ShapeDtypeStruct(((N_DEV,) if kind == "gather" else ()) + z.shape, z.dtype) for z in arrs]
    kw = dict(kw)
    kw["in_specs"] = list(kw["in_specs"]) + [hbm] * ns
    kw["out_specs"] = list(kw["out_specs"]) + [hbm] * ns
    kw["out_shape"] = list(kw["out_shape"]) + side_shapes
    kw["scratch_shapes"] = scratch + [pltpu.SemaphoreType.DMA((7 * ns,)), pltpu.SemaphoreType.DMA((7 * ns,)),
                                      pltpu.SemaphoreType.DMA((ns,))]
    kw["compiler_params"] = _cp(*["arbitrary"] * len(grid))
    res = _pcall(wrapped, grid=grid, **kw)(*args, *arrs)
    return list(res[:n_out]), list(res[n_out:])


def _dot(a, b, dims=NN, precision=None):
    return lax.dot_general(a, b, (dims, ((), ())), preferred_element_type=F32, precision=precision)


def _mdot(a, b, dims=NN):
    return _dot(a.astype(MM), b.astype(MM), dims)


def _tile(n, pref):
    if n <= pref:
        return n
    for c in range(pref - pref % 128, 0, -128):
        if n % c == 0:
            return c
    return n


def _sigmoid(x):
    return 1.0 / (1.0 + jnp.exp(-x))


def _softplus(x):
    return jnp.maximum(x, 0.0) + jnp.log(1.0 + jnp.exp(-jnp.abs(x)))


def _ln_stats(z):
    mu = jnp.mean(z, axis=-1, keepdims=True)
    zc = z - mu
    var = jnp.mean(zc * zc, axis=-1, keepdims=True)
    return zc, lax.rsqrt(var + LN_EPS)


def matmul(a, b, *, mode, name, tm=512, tn=512, tk=512, out_dtype=F32, scale=None, add=None, add_scale=1.0, side=None,
           split_n=None, ln=None):
    if mode == "nn":
        (m, kk), (_, n) = a.shape, b.shape
        dims = NN
    elif mode == "nt":
        (m, kk), (n, _) = a.shape, b.shape
        dims = NT
    else:
        (kk, m), (_, n) = a.shape, b.shape
        dims = TN
    tm, tn, tk = _tile(m, tm), _tile(n, tn), _tile(kk, tk)
    if mode == "nn":
        a_spec = pl.BlockSpec((tm, tk), lambda i, j, k: (i, k))
        b_spec = pl.BlockSpec((tk, tn), lambda i, j, k: (k, j))
    elif mode == "nt":
        a_spec = pl.BlockSpec((tm, tk), lambda i, j, k: (i, k))
        b_spec = pl.BlockSpec((tn, tk), lambda i, j, k: (j, k))
    else:
        a_spec = pl.BlockSpec((tk, tm), lambda i, j, k: (k, i))
        b_spec = pl.BlockSpec((tk, tn), lambda i, j, k: (k, j))
    nk = kk // tk
    o_spec = pl.BlockSpec((tm, tn), lambda i, j, k: (i, j))
    has_add = add is not None
    n_in = 2 + has_add + (2 if ln else 0)
    assert not (ln and split_n) and (not ln or tn == n) and (not split_n or tn % split_n == 0)

    def body(*refs):
        a_ref, b_ref = refs[:2]
        o_ref = refs[n_in]
        acc_ref = refs[-1]
        i, j, k = pl.program_id(0), pl.program_id(1), pl.program_id(2)

        @pl.when(k == 0)
        def _():
            acc_ref[...] = jnp.zeros_like(acc_ref)

        acc_ref[...] += _mdot(a_ref[...], b_ref[...], dims)
        if ln:
            z_ref, g_ref = refs[n_in - 2:n_in]
            dg_ref, db_ref = refs[n_in + 1:n_in + 3]

            @pl.when((i == 0) & (k == 0))
            def _():
                dg_ref[...] = jnp.zeros_like(dg_ref)
                db_ref[...] = jnp.zeros_like(db_ref)

        @pl.when(k == nk - 1)
        def _():
            r = acc_ref[...]
            if scale is not None:
                r = r * scale
            if has_add:
                r = r + add_scale * refs[2][...].astype(F32)
            if ln:
                r, dg, db = _ln_bwd_tile(r, z_ref[...], g_ref[...])
                dg_ref[...] += dg
                db_ref[...] += db
            if split_n:
                for q in range(tn // split_n):
                    o_ref[q] = r[:, q * split_n:(q + 1) * split_n].astype(out_dtype)
            else:
                o_ref[...] = r.astype(out_dtype)

    vec = pl.BlockSpec((1, n), lambda i, j, k: (0, 0))
    ins = [a, b] + ([add] if has_add else []) + ([ln[0], ln[1].reshape(1, n)] if ln else [])
    in_specs = [a_spec, b_spec] + ([o_spec] if has_add else []) + ([o_spec, vec] if ln else [])
    out_specs, out_shape = [o_spec], [jax.ShapeDtypeStruct((m, n), out_dtype)]
    if split_n:
        out_specs = [pl.BlockSpec((tn // split_n, tm, split_n), lambda i, j, k: (j, i, 0))]
        out_shape = [jax.ShapeDtypeStruct((n // split_n, m, split_n), out_dtype)]
    if ln:
        out_specs += [vec, vec]
        out_shape += [jax.ShapeDtypeStruct((1, n), F32)] * 2
    res = _call(
        body, ins, side, (m // tm, n // tn, nk), name=name, in_specs=in_specs, out_specs=out_specs, out_shape=out_shape,
        scratch_shapes=[pltpu.VMEM((tm, tn), F32)],
        compiler_params=_cp("arbitrary" if ln else "parallel", "parallel", "arbitrary"),
    )
    outs, extra = (res, None) if side is None else res
    outs = outs[0] if len(outs) == 1 else tuple(outs)
    return outs if side is None else (outs, extra)


def ffn_fwd(x, wg, wu, wd, g, b, *, name, tm=512, tf=256, side=None):
    t, d = x.shape
    f = wg.shape[1]
    tm = min(tm, t)
    nj = f // tf

    def body(x_ref, wg_ref, wu_ref, wd_ref, g_ref, b_ref, y_ref, z_ref, hg_ref, hu_ref, xb_ref, acc_ref):
        j = pl.program_id(1)

        @pl.when(j == 0)
        def _():
            xb_ref[...] = x_ref[...].astype(MM)
            acc_ref[...] = jnp.zeros_like(acc_ref)

        xb = xb_ref[...]
        hg = _dot(xb, wg_ref[...])
        hu = _dot(xb, wu_ref[...])
        hg_ref[...] = hg.astype(MM)
        hu_ref[...] = hu.astype(MM)
        act = (hg * _sigmoid(hg) * hu).astype(MM)
        acc_ref[...] += _dot(act, wd_ref[...])

        @pl.when(j == nj - 1)
        def _():
            z = DN_ALPHA * x_ref[...] + 0.5 * acc_ref[...]
            z_ref[...] = z
            zc, rstd = _ln_stats(z)
            y_ref[...] = zc * rstd * g_ref[...] + b_ref[...]

    row = pl.BlockSpec((tm, d), lambda i, j: (i, 0))
    hid = pl.BlockSpec((tm, tf), lambda i, j: (i, j))
    vec = pl.BlockSpec((1, d), lambda i, j: (0, 0))
    return _call(
        body, (x, wg, wu, wd, g.reshape(1, d), b.reshape(1, d)), side, (t // tm, nj), name=name,
        in_specs=[row, pl.BlockSpec((d, tf), lambda i, j: (0, j)), pl.BlockSpec((d, tf), lambda i, j: (0, j)),
                  pl.BlockSpec((tf, d), lambda i, j: (j, 0)), vec, vec],
        out_specs=[row, row, hid, hid],
        out_shape=[jax.ShapeDtypeStruct((t, d), F32), jax.ShapeDtypeStruct((t, d), F32),
                   jax.ShapeDtypeStruct((t, f), MM), jax.ShapeDtypeStruct((t, f), MM)],
        scratch_shapes=[pltpu.VMEM((tm, d), MM), pltpu.VMEM((tm, d), F32)],
        compiler_params=_cp("parallel", "arbitrary"),
    )


def _ln_bwd_tile(dy, z, g):
    zc, rstd = _ln_stats(z)
    xh = zc * rstd
    dxh = dy * g
    m1 = jnp.mean(dxh, axis=-1, keepdims=True)
    m2 = jnp.mean(dxh * xh, axis=-1, keepdims=True)
    return rstd * (dxh - m1 - xh * m2), jnp.sum(dy * xh, axis=0, keepdims=True), jnp.sum(dy, axis=0, keepdims=True)


def ffn_bwd(dz, hg, hu, wg, wu, wd, *, name, tm=512, tf=256, side=None, ln=None):
    t, d = dz.shape
    f = wg.shape[1]
    tm = min(tm, t)
    nj = f // tf
    n_in = 6 + (2 if ln else 0)

    def body(*refs):
        dz_ref, hg_ref, hu_ref, wg_ref, wu_ref, wd_ref = refs[:6]
        dx_ref, act_ref, dhg_ref, dhu_ref = refs[n_in:n_in + 4]
        dfb_ref, acc_ref = refs[-2:]
        i, j = pl.program_id(0), pl.program_id(1)

        @pl.when(j == 0)
        def _():
            dfb_ref[...] = (0.5 * dz_ref[...]).astype(MM)
            acc_ref[...] = jnp.zeros_like(acc_ref)

        hg = hg_ref[...].astype(F32)
        hu = hu_ref[...].astype(F32)
        s = _sigmoid(hg)
        dact = _dot(dfb_ref[...], wd_ref[...], NT)
        sg = hg * s
        act_ref[...] = (sg * hu).astype(MM)
        dhu = (dact * sg).astype(MM)
        dhg = (dact * hu * (s + sg * (1.0 - s))).astype(MM)
        dhu_ref[...] = dhu
        dhg_ref[...] = dhg
        acc_ref[...] += _dot(dhg, wg_ref[...], NT) + _dot(dhu, wu_ref[...], NT)

        if ln:
            z_ref, g_ref = refs[6:8]
            dg_ref, db_ref = refs[n_in + 4:n_in + 6]

            @pl.when((i == 0) & (j == 0))
            def _():
                dg_ref[...] = jnp.zeros_like(dg_ref)
                db_ref[...] = jnp.zeros_like(db_ref)

        @pl.when(j == nj - 1)
        def _():
            dx = DN_ALPHA * dz_ref[...] + acc_ref[...]
            if ln:
                dx, dg, db = _ln_bwd_tile(dx, z_ref[...], g_ref[...])
                dg_ref[...] += dg
                db_ref[...] += db
            dx_ref[...] = dx

    row = pl.BlockSpec((tm, d), lambda i, j: (i, 0))
    hid = pl.BlockSpec((tm, tf), lambda i, j: (i, j))
    vec = pl.BlockSpec((1, d), lambda i, j: (0, 0))
    vshape = jax.ShapeDtypeStruct((1, d), F32)
    return _call(
        body, (dz, hg, hu, wg, wu, wd) + ((ln[0], ln[1].reshape(1, d)) if ln else ()), side, (t // tm, nj), name=name,
        in_specs=[row, hid, hid, pl.BlockSpec((d, tf), lambda i, j: (0, j)), pl.BlockSpec((d, tf), lambda i, j: (0, j)),
                  pl.BlockSpec((tf, d), lambda i, j: (j, 0))] + ([row, vec] if ln else []),
        out_specs=[row, hid, hid, hid] + ([vec, vec] if ln else []),
        out_shape=[jax.ShapeDtypeStruct((t, d), F32)] + [jax.ShapeDtypeStruct((t, f), MM)] * 3 + ([vshape, vshape] if ln else []),
        scratch_shapes=[pltpu.VMEM((tm, d), MM), pltpu.VMEM((tm, d), F32)],
        compiler_params=_cp("arbitrary" if ln else "parallel", "arbitrary"),
    )


def ln_bwd(dy, z, g, *, name, tm=512):
    t, d = z.shape
    tm = min(tm, t)

    def body(dy_ref, z_ref, g_ref, dz_ref, dg_ref, db_ref):
        i = pl.program_id(0)

        @pl.when(i == 0)
        def _():
            dg_ref[...] = jnp.zeros_like(dg_ref)
            db_ref[...] = jnp.zeros_like(db_ref)

        dy = dy_ref[...]
        zc, rstd = _ln_stats(z_ref[...])
        xh = zc * rstd
        dg_ref[...] += jnp.sum(dy * xh, axis=0, keepdims=True)
        db_ref[...] += jnp.sum(dy, axis=0, keepdims=True)
        dxh = dy * g_ref[...]
        m1 = jnp.mean(dxh, axis=-1, keepdims=True)
        m2 = jnp.mean(dxh * xh, axis=-1, keepdims=True)
        dz_ref[...] = rstd * (dxh - m1 - xh * m2)

    row = pl.BlockSpec((tm, d), lambda i: (i, 0))
    vec = pl.BlockSpec((1, d), lambda i: (0, 0))
    return _pcall(
        body, name=name, grid=(t // tm,), in_specs=[row, row, vec], out_specs=[row, vec, vec],
        out_shape=[jax.ShapeDtypeStruct((t, d), F32), jax.ShapeDtypeStruct((1, d), F32), jax.ShapeDtypeStruct((1, d), F32)],
        compiler_params=_cp("arbitrary"),
    )(dy, z, g.reshape(1, d))


def mm_ln_fwd(a, w, res, g, b, *, name, tm=512):
    t, kk = a.shape
    d = w.shape[1]
    tm = min(tm, t)

    def body(a_ref, w_ref, res_ref, g_ref, b_ref, y_ref, z_ref):
        z = DN_ALPHA * res_ref[...] + _mdot(a_ref[...], w_ref[...])
        z_ref[...] = z
        zc, rstd = _ln_stats(z)
        y_ref[...] = zc * rstd * g_ref[...] + b_ref[...]

    row = pl.BlockSpec((tm, d), lambda i: (i, 0))
    vec = pl.BlockSpec((1, d), lambda i: (0, 0))
    return _pcall(
        body, name=name, grid=(t // tm,),
        in_specs=[pl.BlockSpec((tm, kk), lambda i: (i, 0)), pl.BlockSpec((kk, d), lambda i: (0, 0)), row, vec, vec],
        out_specs=[row, row],
        out_shape=[jax.ShapeDtypeStruct((t, d), F32), jax.ShapeDtypeStruct((t, d), F32)],
        compiler_params=_cp("parallel"),
    )(a, w, res, g.reshape(1, d), b.reshape(1, d))


def ple_fwd(y, p, wg, bg, wp, *, name, tm=512):
    t, d = y.shape
    dp = p.shape[1]
    tm = min(tm, t)

    def body(y_ref, p_ref, wg_ref, bg_ref, wp_ref, o_ref):
        yv = y_ref[...]
        gate = _sigmoid(_mdot(yv, wg_ref[...]) + bg_ref[...])
        o_ref[...] = yv + gate * _mdot(p_ref[...], wp_ref[...])

    row = pl.BlockSpec((tm, d), lambda i: (i, 0))
    return _pcall(
        body, name=name, grid=(t // tm,),
        in_specs=[row, pl.BlockSpec((tm, dp), lambda i: (i, 0)), pl.BlockSpec((d, d), lambda i: (0, 0)),
                  pl.BlockSpec((1, d), lambda i: (0, 0)), pl.BlockSpec((dp, d), lambda i: (0, 0))],
        out_specs=row, out_shape=jax.ShapeDtypeStruct((t, d), F32), compiler_params=_cp("parallel"),
    )(y, p, wg, bg.reshape(1, d), wp)


def ple_bwd(do, y, p, wg, bg, wp, z, g, *, name, tm=512):
    t, d = y.shape
    dp = p.shape[1]
    tm = min(tm, t)

    def body(do_ref, y_ref, p_ref, wg_ref, bg_ref, wp_ref, z_ref, g_ref, dz_ref, dt_ref, de_ref, dbg_ref, dg_ref, db_ref):
        i = pl.program_id(0)

        @pl.when(i == 0)
        def _():
            for ref in (dbg_ref, dg_ref, db_ref):
                ref[...] = jnp.zeros_like(ref)

        dov = do_ref[...]
        gate = _sigmoid(_mdot(y_ref[...], wg_ref[...]) + bg_ref[...])
        emb = _mdot(p_ref[...], wp_ref[...])
        dt = dov * emb * gate * (1.0 - gate)
        dbg_ref[...] += jnp.sum(dt, axis=0, keepdims=True)
        dtb = dt.astype(MM)
        dt_ref[...] = dtb
        de_ref[...] = (dov * gate).astype(MM)
        dz, dg, db = _ln_bwd_tile(dov + _dot(dtb, wg_ref[...], NT), z_ref[...], g_ref[...])
        dz_ref[...] = dz
        dg_ref[...] += dg
        db_ref[...] += db

    row = pl.BlockSpec((tm, d), lambda i: (i, 0))
    vec = pl.BlockSpec((1, d), lambda i: (0, 0))
    vshape = jax.ShapeDtypeStruct((1, d), F32)
    return _pcall(
        body, name=name, grid=(t // tm,),
        in_specs=[row, row, pl.BlockSpec((tm, dp), lambda i: (i, 0)), pl.BlockSpec((d, d), lambda i: (0, 0)),
                  vec, pl.BlockSpec((dp, d), lambda i: (0, 0)), row, vec],
        out_specs=[row, row, row, vec, vec, vec],
        out_shape=[jax.ShapeDtypeStruct((t, d), F32), jax.ShapeDtypeStruct((t, d), MM),
                   jax.ShapeDtypeStruct((t, d), MM), vshape, vshape, vshape],
        compiler_params=_cp("arbitrary"),
    )(do, y, p, wg, bg.reshape(1, d), wp, z, g.reshape(1, d))


def loss_fwd_bwd(y, tgt, *, name, tm=512):
    t, d = y.shape
    tm = min(tm, t)

    def body(y_ref, t_ref, l_ref, dy_ref):
        i = pl.program_id(0)

        @pl.when(i == 0)
        def _():
            l_ref[...] = jnp.zeros_like(l_ref)

        err = y_ref[...] - t_ref[...]
        dy_ref[...] = err * (1.0 / d)
        l_ref[...] += (0.5 / d) * jnp.sum(jnp.sum(err * err, axis=1, keepdims=True), axis=0, keepdims=True)

    row = pl.BlockSpec((tm, d), lambda i: (i, 0))
    return _pcall(
        body, name=name, grid=(t // tm,), in_specs=[row, row],
        out_specs=[pl.BlockSpec((1, 128), lambda i: (0, 0)), row],
        out_shape=[jax.ShapeDtypeStruct((1, 128), F32), jax.ShapeDtypeStruct((t, d), F32)],
        compiler_params=_cp("arbitrary"),
    )(y, tgt)


def _shift_dn(x, s, row):
    return x if s == 0 else jnp.where(row >= s, pltpu.roll(x, s, 0), 0.0)


def _shift_up(x, s, row):
    n = x.shape[0]
    return x if s == 0 else jnp.where(row < n - s, pltpu.roll(x, n - s, 0), 0.0)


def _conv_fwd(x, w, row):
    kk = w.shape[0]
    y = w[kk - 1:kk, :] * x
    for j in range(kk - 1):
        y = y + w[j:j + 1, :] * _shift_dn(x, kk - 1 - j, row)
    return y


def _conv_bwd(x, w, dy, row):
    kk = w.shape[0]
    dx = w[kk - 1:kk, :] * dy
    dws = []
    for j in range(kk - 1):
        dx = dx + w[j:j + 1, :] * _shift_up(dy, kk - 1 - j, row)
        dws.append(jnp.sum(dy * _shift_dn(x, kk - 1 - j, row), axis=0, keepdims=True))
    dws.append(jnp.sum(dy * x, axis=0, keepdims=True))
    return dx, jnp.concatenate(dws, axis=0)


def _gelu(x):
    c = math.sqrt(2.0 / math.pi)
    th = jnp.tanh(c * (x + 0.044715 * x * x * x))
    return 0.5 * x * (1.0 + th), th


def _gelu_grad(x, th):
    c = math.sqrt(2.0 / math.pi)
    return 0.5 * (1.0 + th) + 0.5 * x * (1.0 - th * th) * c * (1.0 + 3.0 * 0.044715 * x * x)


def _neg_expm1(y):
    ser = -(y * (1.0 + y * (0.5 + y * (1.0 / 6.0 + y * (1.0 / 24.0 + y * (1.0 / 120.0))))))
    return jnp.where(y > -0.05, ser, 1.0 - jnp.exp(y))


def _attn_head(qh, kk, vv, bias, valid, sink):
    s = _mdot(qh, kk, NT) * (A_HEAD_DIM ** -0.5) - bias
    s = jnp.where(valid, s, NEG)
    m = jnp.maximum(jnp.max(s, axis=-1, keepdims=True), sink)
    pr = jnp.exp(s - m)
    den = jnp.sum(pr, axis=-1, keepdims=True) + jnp.exp(sink - m)
    return pr / den, jnp.exp(sink - m) / den


def _attn_valid(n):
    ji = lax.broadcasted_iota(jnp.int32, (1, 3 * CHUNK), 1)
    return (n * CHUNK + ji - 2 * CHUNK) >= 0


def _attn_group_consts(kh, sk_ref):
    rows = A_GROUP * CHUNK
    ri = lax.broadcasted_iota(jnp.int32, (rows, 3 * CHUNK), 0)
    ji = lax.broadcasted_iota(jnp.int32, (rows, 3 * CHUNK), 1)
    dist = jnp.abs((ri & (CHUNK - 1)) + 2 * CHUNK - ji).astype(F32)
    rcol = lax.broadcasted_iota(jnp.int32, (rows, 1), 0)
    slope = jnp.zeros((rows, 1), F32)
    sink = jnp.zeros((rows, 1), F32)
    for gi in range(A_GROUP):
        h = kh * A_GROUP + gi
        inblk = (rcol >= gi * CHUNK) & (rcol < (gi + 1) * CHUNK)
        slope = jnp.where(inblk, 2.0 ** -(h + 1), slope)
        sink = jnp.where(inblk, sk_ref[h], sink)
    return slope * dist, sink


def _stack_heads(x, kh):
    return jnp.concatenate([x[:, (kh * A_GROUP + gi) * 64:(kh * A_GROUP + gi + 1) * 64] for gi in range(A_GROUP)], axis=0)


def attn_fwd(proj, sinks, bsz, *, name, side=None):
    t = proj.shape[0]
    s_len = t // bsz
    nc = s_len // CHUNK
    pad = 2 * CHUNK

    def body(q_ref, k_ref, v_ref, sk_ref, o_ref, kp_ref, vp_ref):
        kp_ref[0:pad, :] = jnp.zeros((pad, A_KV_WIDTH), F32)
        vp_ref[0:pad, :] = jnp.zeros((pad, A_KV_WIDTH), F32)
        kp_ref[pad:, :] = k_ref[...].astype(F32)
        vp_ref[pad:, :] = v_ref[...].astype(F32)

        consts = [_attn_group_consts(kh, sk_ref) for kh in range(A_KV_HEADS)]

        def chunk(n, carry):
            st = pl.multiple_of(n * CHUNK, CHUNK)
            q = q_ref[pl.ds(st, CHUNK), :].astype(F32)
            kb = kp_ref[pl.ds(st, 3 * CHUNK), :]
            vb = vp_ref[pl.ds(st, 3 * CHUNK), :]
            valid = _attn_valid(n)
            outs = []
            for kh in range(A_KV_HEADS):
                bias, sink = consts[kh]
                pn, _ = _attn_head(_stack_heads(q, kh), kb[:, kh * 64:(kh + 1) * 64], None, bias, valid, sink)
                o = _mdot(pn, vb[:, kh * 64:(kh + 1) * 64])
                outs += [o[gi * CHUNK:(gi + 1) * CHUNK] for gi in range(A_GROUP)]
            o_ref[pl.ds(st, CHUNK), :] = jnp.concatenate(outs, axis=-1)
            return carry

        lax.fori_loop(0, nc, chunk, 0, unroll=2)

    res = _call(
        body, (proj, proj, proj, sinks), side, (bsz,), name=name,
        in_specs=[pl.BlockSpec((s_len, A_WIDTH), lambda b: (b, 0)), pl.BlockSpec((s_len, 128), lambda b: (b, 4)),
                  pl.BlockSpec((s_len, 128), lambda b: (b, 5)), pl.BlockSpec(memory_space=pltpu.SMEM)],
        out_specs=[pl.BlockSpec((s_len, A_WIDTH), lambda b: (b, 0))],
        out_shape=[jax.ShapeDtypeStruct((t, A_WIDTH), F32)],
        scratch_shapes=[pltpu.VMEM((s_len + pad, A_KV_WIDTH), F32), pltpu.VMEM((s_len + pad, A_KV_WIDTH), F32)],
        compiler_params=_cp("parallel"),
    )
    return res[0] if side is None else (res[0][0], res[1])


def attn_bwd(proj, sinks, dcat, bsz, *, name, side=None):
    t = proj.shape[0]
    s_len = t // bsz
    nc = s_len // CHUNK
    pad = 2 * CHUNK

    def body(q_ref, k_ref, v_ref, do_ref, sk_ref, dq_ref, dk_ref, dv_ref, dsk_ref, kp_ref, vp_ref, dkp_ref, dvp_ref):
        kp_ref[0:pad, :] = jnp.zeros((pad, A_KV_WIDTH), F32)
        vp_ref[0:pad, :] = jnp.zeros((pad, A_KV_WIDTH), F32)
        kp_ref[pad:, :] = k_ref[...].astype(F32)
        vp_ref[pad:, :] = v_ref[...].astype(F32)
        dkp_ref[...] = jnp.zeros_like(dkp_ref)
        dvp_ref[...] = jnp.zeros_like(dvp_ref)
        lane = lax.broadcasted_iota(jnp.int32, (1, 128), 1)

        consts = [_attn_group_consts(kh, sk_ref) for kh in range(A_KV_HEADS)]

        def chunk(n, carry):
            dsk = jnp.zeros((1, 128), F32)
            st = pl.multiple_of(n * CHUNK, CHUNK)
            q = q_ref[pl.ds(st, CHUNK), :].astype(F32)
            do = do_ref[pl.ds(st, CHUNK), :]
            kb = kp_ref[pl.ds(st, 3 * CHUNK), :]
            vb = vp_ref[pl.ds(st, 3 * CHUNK), :]
            valid = _attn_valid(n)
            dqs, dks, dvs = [], [], []
            for kh in range(A_KV_HEADS):
                kk = kb[:, kh * 64:(kh + 1) * 64]
                vv = vb[:, kh * 64:(kh + 1) * 64]
                bias, sink = consts[kh]
                qs = _stack_heads(q, kh)
                dos = _stack_heads(do, kh)
                pn, psink = _attn_head(qs, kk, None, bias, valid, sink)
                dp = _mdot(dos, vv, NT)
                rowdot = jnp.sum(pn * dp, axis=-1, keepdims=True)
                ds = pn * (dp - rowdot)
                sink_part = psink * rowdot
                for gi in range(A_GROUP):
                    part = jnp.sum(sink_part[gi * CHUNK:(gi + 1) * CHUNK], axis=0, keepdims=True)
                    dsk = dsk + jnp.where(lane == kh * A_GROUP + gi, -part, 0.0)
                dq = _mdot(ds, kk) * (A_HEAD_DIM ** -0.5)
                dqs += [dq[gi * CHUNK:(gi + 1) * CHUNK] for gi in range(A_GROUP)]
                dks.append(_mdot(ds, qs, TN) * (A_HEAD_DIM ** -0.5))
                dvs.append(_mdot(pn, dos, TN))
            dq_ref[pl.ds(st, CHUNK), :] = jnp.concatenate(dqs, axis=-1)
            dkp_ref[pl.ds(st, 3 * CHUNK), :] += jnp.concatenate(dks, axis=-1)
            dvp_ref[pl.ds(st, 3 * CHUNK), :] += jnp.concatenate(dvs, axis=-1)
            dsk_ref[0] += dsk
            return carry

        dsk_ref[...] = jnp.zeros_like(dsk_ref)
        lax.fori_loop(0, nc, chunk, 0, unroll=2)
        dk_ref[...] = dkp_ref[pad:, :]
        dv_ref[...] = dvp_ref[pad:, :]

    kv = jax.ShapeDtypeStruct((t, A_KV_WIDTH), F32)
    return _call(
        body, (proj, proj, proj, dcat, sinks), side, (bsz,), name=name,
        in_specs=[pl.BlockSpec((s_len, A_WIDTH), lambda b: (b, 0)), pl.BlockSpec((s_len, 128), lambda b: (b, 4)),
                  pl.BlockSpec((s_len, 128), lambda b: (b, 5)), pl.BlockSpec((s_len, A_WIDTH), lambda b: (b, 0)),
                  pl.BlockSpec(memory_space=pltpu.SMEM)],
        out_specs=[pl.BlockSpec((s_len, A_WIDTH), lambda b: (b, 0)), pl.BlockSpec((s_len, 128), lambda b: (b, 0)),
                   pl.BlockSpec((s_len, 128), lambda b: (b, 0)), pl.BlockSpec((1, 1, 128), lambda b: (b, 0, 0))],
        out_shape=[jax.ShapeDtypeStruct((t, A_WIDTH), F32), kv, kv, jax.ShapeDtypeStruct((bsz, 1, 128), F32)],
        scratch_shapes=[pltpu.VMEM((s_len + pad, A_KV_WIDTH), F32)] * 4,
        compiler_params=_cp("parallel"),
    )


def _lru_gates(x, cw, cb, wa, ba, wx, bx, lam, row):
    xc = _conv_fwd(x, cw, row) + cb
    r = _sigmoid(_mdot(xc, wa) + ba)
    i = _sigmoid(_mdot(xc, wx) + bx)
    sp = _softplus(-lam)
    log_a = -RG_C * r * sp
    a = jnp.exp(log_a)
    mult = jnp.sqrt(_neg_expm1(2.0 * log_a))
    return xc, r, i, sp, a, mult


LRU_BLOCK = 128


def _lru_scan_refs(a_ref, u_ref, h_ref, reverse=False):
    nb = a_ref.shape[0] // LRU_BLOCK
    row = lax.broadcasted_iota(jnp.int32, (LRU_BLOCK, 128), 0)

    def block(i, carry):
        bi = nb - 1 - i if reverse else i
        rs = pl.ds(pl.multiple_of(bi * LRU_BLOCK, LRU_BLOCK), LRU_BLOCK)
        a, u = a_ref[rs, :], u_ref[rs, :]
        d = 1
        while d < LRU_BLOCK:
            keep = row < LRU_BLOCK - d if reverse else row >= d
            sh = LRU_BLOCK - d if reverse else d
            a_sh = jnp.where(keep, pltpu.roll(a, sh, 0), 1.0)
            u_sh = jnp.where(keep, pltpu.roll(u, sh, 0), 0.0)
            u = a * u_sh + u
            a = a * a_sh
            d *= 2
        h = u + a * carry
        h_ref[rs, :] = h
        return h[0:1, :] if reverse else h[LRU_BLOCK - 1:LRU_BLOCK, :]

    lax.fori_loop(0, nb, block, jnp.zeros((1, 128), F32))


def _lru_specs(s_len, order):
    def at(f):
        return lambda *g: f(*order(*g))
    return [pl.BlockSpec((s_len, 128), at(lambda b, cb: (b, 6 + cb))), pl.BlockSpec((s_len, 128), at(lambda b, cb: (b, 10 + cb))),
            pl.BlockSpec((4, 128), at(lambda b, cb: (0, cb))), pl.BlockSpec((1, 128), at(lambda b, cb: (0, cb))),
            pl.BlockSpec((1, 128, 128), at(lambda b, cb: (cb, 0, 0))), pl.BlockSpec((1, 128), at(lambda b, cb: (0, cb))),
            pl.BlockSpec((1, 128, 128), at(lambda b, cb: (cb, 0, 0))), pl.BlockSpec((1, 128), at(lambda b, cb: (0, cb))),
            pl.BlockSpec((1, 128), at(lambda b, cb: (0, cb)))]


def lru_fwd(proj, cw, cb, wa, ba, wx, bxb, lam, bsz, *, name):
    t = proj.shape[0]
    s_len = t // bsz

    def body(x_ref, g_ref, cw_ref, cb_ref, wa_ref, ba_ref, wx_ref, bx_ref, lam_ref, y_ref, a_s, u_s):
        row = lax.broadcasted_iota(jnp.int32, (s_len, 128), 0)
        xc, r, i, sp, a, mult = _lru_gates(x_ref[...].astype(F32), cw_ref[...], cb_ref[...], wa_ref[0], ba_ref[...],
                                           wx_ref[0], bx_ref[...], lam_ref[...], row)
        a_s[...] = a
        u_s[...] = mult * (i * xc)
        _lru_scan_refs(a_s, u_s, y_ref)
        y_ref[...] = y_ref[...] * _gelu(g_ref[...].astype(F32))[0]

    return _pcall(
        body, name=name, grid=(bsz, 4), in_specs=_lru_specs(s_len, lambda b, cb: (b, cb)),
        out_specs=pl.BlockSpec((s_len, 128), lambda b, cb: (b, cb)),
        out_shape=jax.ShapeDtypeStruct((t, B_WIDTH), F32), scratch_shapes=[pltpu.VMEM((s_len, 128), F32)] * 2,
        compiler_params=_cp("parallel", "parallel"),
    )(proj, proj, cw, cb.reshape(1, -1), wa, ba.reshape(1, -1), wx, bxb.reshape(1, -1), lam.reshape(1, -1))


def lru_bwd(proj, cw, cb, wa, ba, wx, bxb, lam, dcat, bsz, *, name, side=None):
    t = proj.shape[0]
    s_len = t // bsz

    def body(x_ref, g_ref, cw_ref, cb_ref, wa_ref, ba_ref, wx_ref, bx_ref, lam_ref, dy_ref,
             dx_ref, dg_ref, dcw_ref, dcb_ref, dwa_ref, dba_ref, dwx_ref, dbx_ref, dlam_ref, a_s, u_s, h_s, g_s):
        b = pl.program_id(1)
        row = lax.broadcasted_iota(jnp.int32, (s_len, 128), 0)
        x = x_ref[...].astype(F32)
        lam = lam_ref[...]
        xc, r, i, sp, a, mult = _lru_gates(x, cw_ref[...], cb_ref[...], wa_ref[0], ba_ref[...], wx_ref[0], bx_ref[...],
                                           lam, row)
        ixc = i * xc
        a_s[...] = a
        u_s[...] = mult * ixc
        _lru_scan_refs(a_s, u_s, h_s)
        h = h_s[...]
        gv = g_ref[...].astype(F32)
        gl, th = _gelu(gv)
        dy = dy_ref[...]
        dg_ref[...] = dy * h * _gelu_grad(gv, th)
        a_s[...] = _shift_up(a, 1, row)
        u_s[...] = dy * gl
        _lru_scan_refs(a_s, u_s, g_s, reverse=True)
        gr = g_s[...]
        da = gr * _shift_dn(h, 1, row)
        dmult = gr * ixc
        di = gr * mult * xc
        dxc = gr * mult * i
        dlog_a = da * a - dmult * (a * a) / mult
        dr = dlog_a * (-RG_C * sp)
        dlam = jnp.sum(dlog_a * r, axis=0, keepdims=True) * (RG_C * _sigmoid(-lam))
        dpa = dr * r * (1.0 - r)
        dpx = di * i * (1.0 - i)
        dxc = dxc + _mdot(dpa, wa_ref[0], NT) + _mdot(dpx, wx_ref[0], NT)
        dx, dcw = _conv_bwd(x, cw_ref[...], dxc, row)
        dx_ref[...] = dx

        @pl.when(b == 0)
        def _():
            for ref in (dcw_ref, dcb_ref, dwa_ref, dba_ref, dwx_ref, dbx_ref, dlam_ref):
                ref[...] = jnp.zeros_like(ref)

        dcw_ref[...] += dcw
        dcb_ref[...] += jnp.sum(dxc, axis=0, keepdims=True)
        dwa_ref[0] += _mdot(xc, dpa, TN)
        dwx_ref[0] += _mdot(xc, dpx, TN)
        dba_ref[...] += jnp.sum(dpa, axis=0, keepdims=True)
        dbx_ref[...] += jnp.sum(dpx, axis=0, keepdims=True)
        dlam_ref[...] += dlam

    order = lambda cb, b: (b, cb)
    act = pl.BlockSpec((s_len, 128), lambda cb, b: (b, cb))
    vec = pl.BlockSpec((1, 128), lambda cb, b: (0, cb))
    mat = pl.BlockSpec((1, 128, 128), lambda cb, b: (cb, 0, 0))
    vshape = jax.ShapeDtypeStruct((1, B_WIDTH), F32)
    mshape = jax.ShapeDtypeStruct((4, 128, 128), F32)
    return _call(
        body, (proj, proj, cw, cb.reshape(1, -1), wa, ba.reshape(1, -1), wx, bxb.reshape(1, -1), lam.reshape(1, -1), dcat),
        side, (4, bsz), name=name,
        in_specs=_lru_specs(s_len, order) + [pl.BlockSpec((s_len, 128), lambda cb, b: (b, 4 + cb))],
        out_specs=[act, act, pl.BlockSpec((4, 128), lambda cb, b: (0, cb)), vec, mat, vec, mat, vec, vec],
        out_shape=[jax.ShapeDtypeStruct((t, B_WIDTH), F32), jax.ShapeDtypeStruct((t, B_WIDTH), F32),
                   jax.ShapeDtypeStruct((4, B_WIDTH), F32), vshape, mshape, vshape, mshape, vshape, vshape],
        scratch_shapes=[pltpu.VMEM((s_len, 128), F32)] * 4,
        compiler_params=_cp("parallel", "arbitrary"),
    )


_BDIMS = {"nn": ((2,), (1,)), "nt": ((2,), (2,)), "tn": ((1,), (1,))}
C_QSCALE = C_HEAD_DIM ** -0.5


def _bmm(a, b, mode, exact=False):
    dims = (_BDIMS[mode], ((0,), (0,)))
    if exact:
        return lax.dot_general(a, b, dims, preferred_element_type=F32, precision=lax.Precision.HIGH)
    return lax.dot_general(a.astype(MM), b.astype(MM), dims, preferred_element_type=F32)


def _col(x, idx, lane):
    return jnp.broadcast_to(jnp.sum(jnp.where(lane == idx, x, 0.0), axis=-1, keepdims=True), x.shape)


def _seg_cumsum(g, row):
    pos = row & (CHUNK - 1)
    d = 1
    while d < CHUNK:
        g = g + jnp.where(pos >= d, pltpu.roll(g, d, 0), 0.0)
        d *= 2
    return g


def _seg_cumsum_rev(g, row):
    pos = row & (CHUNK - 1)
    n = g.shape[0]
    d = 1
    while d < CHUNK:
        g = g + jnp.where(pos < CHUNK - d, pltpu.roll(g, n - d, 0), 0.0)
        d *= 2
    return g


def _neumann_inverse(lmat):
    ii = lax.broadcasted_iota(jnp.int32, lmat.shape, 1)
    jj = lax.broadcasted_iota(jnp.int32, lmat.shape, 2)
    x = -lmat
    tm = jnp.where(ii == jj, 1.0, 0.0) + x
    pw = x
    for it in range(5):
        exact = it < 2
        pw = _bmm(pw, pw, "nn", exact=exact)
        tm = tm + _bmm(tm, pw, "nn", exact=exact)
    return tm


def gdc_pre_fwd(proj, cw, bsz, *, name):
    t = proj.shape[0]
    s_len = t // bsz

    def body(x_ref, w_ref, y_ref):
        row = lax.broadcasted_iota(jnp.int32, (s_len, 128), 0)
        c = _conv_fwd(x_ref[...].astype(F32), w_ref[...], row)
        xc = c * _sigmoid(c)
        rn = lax.rsqrt(jnp.sum(xc * xc, axis=-1, keepdims=True) + NORM_EPS)
        y_ref[...] = jnp.where(pl.program_id(1) < 2 * C_HEADS, xc * rn, xc)

    blk = pl.BlockSpec((s_len, 128), lambda b, j: (b, j))
    return _pcall(
        body, name=name, grid=(bsz, 3 * C_HEADS), in_specs=[blk, pl.BlockSpec((4, 128), lambda b, j: (0, j))],
        out_specs=blk, out_shape=jax.ShapeDtypeStruct((t, 3 * C_WIDTH), F32), compiler_params=_cp("parallel", "parallel"),
    )(proj, cw)


def gdc_pre_bwd(proj, cw, dy, dproj, bsz, *, name):
    t = proj.shape[0]
    s_len = t // bsz

    def body(x_ref, w_ref, dq_ref, dk_ref, dv_ref, _, dx_ref, dw_ref):
        row = lax.broadcasted_iota(jnp.int32, (s_len, 128), 0)
        x = x_ref[...].astype(F32)
        c = _conv_fwd(x, w_ref[...], row)
        sg = _sigmoid(c)
        xc = c * sg
        rn = lax.rsqrt(jnp.sum(xc * xc, axis=-1, keepdims=True) + NORM_EPS)
        part = pl.program_id(1) // C_HEADS
        dyv = jnp.where(part == 0, dq_ref[...], jnp.where(part == 1, dk_ref[...], dv_ref[...]))
        xn = xc * rn
        dxc = jnp.where(pl.program_id(1) < 2 * C_HEADS, rn * (dyv - xn * jnp.sum(dyv * xn, axis=-1, keepdims=True)), dyv)
        dc = dxc * (sg * (1.0 + c * (1.0 - sg)))
        dx, dw = _conv_bwd(x, w_ref[...], dc, row)
        dx_ref[...] = dx.astype(MM)
        dw_ref[0] = dw

    blk = pl.BlockSpec((s_len, 128), lambda b, j: (b, j))

    def dy_spec(part):
        return pl.BlockSpec((s_len, 128), lambda b, j: (b, jnp.clip(j - part * C_HEADS, 0, C_HEADS - 1)))

    return _pcall(
        body, name=name, grid=(bsz, 3 * C_HEADS),
        in_specs=[blk, pl.BlockSpec((4, 128), lambda b, j: (0, j)), dy_spec(0), dy_spec(1), dy_spec(2),
                  pl.BlockSpec(memory_space=pl.ANY)],
        out_specs=[blk, pl.BlockSpec((1, 4, 128), lambda b, j: (b, 0, j))],
        out_shape=[jax.ShapeDtypeStruct((t, 4 * C_WIDTH), MM), jax.ShapeDtypeStruct((bsz, 4, 3 * C_WIDTH), F32)],
        input_output_aliases={5: 0}, compiler_params=_cp("parallel", "parallel"),
    )(proj, cw, *dy, dproj)


GDC_GROUP = 16


def _gdc_local(qn, kn, vc, gates, a_log, dtb, h):
    rows = qn.shape[0]
    nc = rows // CHUNK
    row = lax.broadcasted_iota(jnp.int32, (rows, 128), 0)
    lane = lax.broadcasted_iota(jnp.int32, (rows, 128), 1)
    r = {"row": row, "lane": lane}
    r["beta"] = _sigmoid(_col(gates, h, lane))
    r["A"] = jnp.exp(a_log)
    r["pre"] = _col(gates, 8 + h, lane) + dtb
    r["sp"] = _softplus(r["pre"])
    gc = _seg_cumsum(-r["A"] * r["sp"], row)
    sh = (nc, CHUNK, 128)
    q3 = (qn * C_QSCALE).reshape(sh)
    k3 = kn.reshape(sh)
    v3 = vc.reshape(sh)
    beta3 = r["beta"].reshape(sh)
    gc3 = gc.reshape(sh)
    gcl3 = gc3[:, CHUNK - 1:CHUNK, :]
    eg = jnp.exp(gc3)
    ekd = jnp.exp(gcl3 - gc3)
    col64 = gc3[:, :, :CHUNK]
    row64 = jnp.swapaxes(gc3, 1, 2)[:, :CHUNK, :]
    ii = lax.broadcasted_iota(jnp.int32, (nc, CHUNK, CHUNK), 1)
    jj = lax.broadcasted_iota(jnp.int32, (nc, CHUNK, CHUNK), 2)
    tril = ii >= jj
    strict = ii > jj
    dm = jnp.where(tril, jnp.exp(jnp.where(tril, col64 - row64, 0.0)), 0.0)
    kb = k3 * beta3
    lmat = jnp.where(strict, _bmm(kb, k3, "nt") * dm, 0.0)
    attn = _bmm(q3, k3, "nt") * dm
    r.update(q3=q3, k3=k3, v3=v3, beta3=beta3, eg=eg, ekd=ekd, gl=jnp.exp(gcl3), dm=dm, kb=kb, lmat=lmat,
             attn=attn, strict=strict, tril=tril, qg=q3 * eg, kdec=k3 * ekd)
    return r


def _gdc_specs(s_len):
    act = lambda off: pl.BlockSpec((s_len, 128), lambda b, h: (b, off + h))
    smem = pl.BlockSpec(memory_space=pltpu.SMEM)
    return [act(0), act(8), act(16), act(24), pl.BlockSpec((s_len, 128), lambda b, h: (b, 0)), smem, smem,
            pl.BlockSpec((1, 128), lambda b, h: (0, 0))]


def gdc_fwd(qkv, proj, gates, a_log, dtb, ng, bsz, *, name, side=None):
    t = proj.shape[0]
    s_len = t // bsz
    nc = s_len // CHUNK
    grp = min(GDC_GROUP, nc)
    gr = grp * CHUNK

    def body(q_ref, k_ref, v_ref, z_ref, gt_ref, al_ref, dt_ref, ng_ref,
             out_ref, o_ref, tm_ref, st_ref, c_s, b_s, qp_s, op_s, gl_s):
        h = pl.program_id(1)

        def local(gi, carry):
            rs = pl.ds(pl.multiple_of(gi * gr, gr), gr)
            cs = pl.ds(gi * grp, grp)
            r = _gdc_local(q_ref[rs, :], k_ref[rs, :], v_ref[rs, :], gt_ref[rs, :], al_ref[h], dt_ref[h], h)
            tm = _neumann_inverse(r["lmat"])
            tm_ref[0, 0, cs] = tm
            u = _bmm(tm, r["v3"] * r["beta3"], "nn")
            w = _bmm(tm, r["kb"] * r["eg"], "nn")
            c_s[cs] = -_bmm(r["kdec"], w, "tn")
            b_s[cs] = _bmm(r["kdec"], u, "tn")
            qp_s[cs] = r["qg"] - _bmm(r["attn"], w, "nn")
            op_s[cs] = _bmm(r["attn"], u, "nn")
            gl_s[cs] = r["gl"]
            return carry

        lax.fori_loop(0, nc // grp, local, 0)

        def chunk(n, state):
            st = pl.multiple_of(n * CHUNK, CHUNK)
            st_ref[0, 0, n] = state
            o_ref[pl.ds(st, CHUNK), :] = _mdot(qp_s[n], state) + op_s[n]
            return state * gl_s[n] + _mdot(c_s[n], state) + b_s[n]

        lax.fori_loop(0, nc, chunk, jnp.zeros((128, 128), F32))
        o = o_ref[...]
        rms = lax.rsqrt(jnp.mean(o * o, axis=-1, keepdims=True) + NORM_EPS)
        z = z_ref[...].astype(F32)
        out_ref[...] = o * rms * ng_ref[...] * (z * _sigmoid(z))

    blk = pl.BlockSpec((s_len, 128), lambda b, h: (b, h))
    full = jax.ShapeDtypeStruct((t, C_WIDTH), F32)
    return _call(
        body, (qkv, qkv, qkv, proj, gates, a_log, dtb, ng.reshape(1, 128)), side, (bsz, C_HEADS), name=name,
        in_specs=_gdc_specs(s_len),
        out_specs=[blk, blk, pl.BlockSpec((1, 1, nc, CHUNK, CHUNK), lambda b, h: (b, h, 0, 0, 0)),
                   pl.BlockSpec((1, 1, nc, 128, 128), lambda b, h: (b, h, 0, 0, 0))],
        out_shape=[full, full, jax.ShapeDtypeStruct((bsz, C_HEADS, nc, CHUNK, CHUNK), F32),
                   jax.ShapeDtypeStruct((bsz, C_HEADS, nc, 128, 128), F32)],
        scratch_shapes=[pltpu.VMEM((nc, 128, 128), F32)] * 2 + [pltpu.VMEM((nc, CHUNK, 128), F32)] * 2 +
                       [pltpu.VMEM((nc, 1, 128), F32)],
        compiler_params=_cp("parallel", "parallel"),
    )


def gdc_bwd(qkv, proj, gates, a_log, dtb, ng, o_pre, tmat, states, dout, bsz, *, name, side=None):
    t = proj.shape[0]
    s_len = t // bsz
    nc = s_len // CHUNK
    grp = min(GDC_GROUP, nc)
    gr = grp * CHUNK

    def body(q_ref, k_ref, v_ref, z_ref, gt_ref, al_ref, dt_ref, ng_ref, o_ref, tm_ref, st_ref, do_ref,
             dq_ref, dk_ref, dv_ref, dz_ref, dgt_ref, dsm_ref, c_s, e_s, dsp_s, gl_s, dop_s):
        h = pl.program_id(1)
        a_log_h, dtb_h = al_ref[h], dt_ref[h]

        z = z_ref[...].astype(F32)
        sz = _sigmoid(z)
        o = o_ref[...]
        rms = lax.rsqrt(jnp.mean(o * o, axis=-1, keepdims=True) + NORM_EPS)
        on = o * rms
        dout_v = do_ref[...]
        ngv = ng_ref[...]
        dz_ref[...] = (dout_v * on * ngv * (sz * (1.0 + z * (1.0 - sz)))).astype(MM)
        dos = dout_v * (z * sz)
        dng = jnp.sum(dos * on, axis=0, keepdims=True)
        don = dos * ngv
        dop_s[...] = (rms * (don - on * jnp.mean(don * on, axis=-1, keepdims=True))).reshape(nc, CHUNK, 128)

        def local(gi, carry):
            rs = pl.ds(pl.multiple_of(gi * gr, gr), gr)
            cs = pl.ds(gi * grp, grp)
            r = _gdc_local(q_ref[rs, :], k_ref[rs, :], v_ref[rs, :], gt_ref[rs, :], a_log_h, dtb_h, h)
            w = _bmm(tm_ref[0, 0, cs], r["kb"] * r["eg"], "nn")
            c_s[cs] = -_bmm(w, r["kdec"], "tn")
            e_s[cs] = _bmm(r["qg"] - _bmm(r["attn"], w, "nn"), dop_s[cs], "tn")
            gl_s[cs] = r["gl"]
            return carry

        lax.fori_loop(0, nc // grp, local, 0)

        def chunk(i, dstate):
            n = nc - 1 - i
            dsp_s[n] = dstate
            return dstate * gl_s[n] + _mdot(c_s[n], dstate) + e_s[n]

        lax.fori_loop(0, nc, chunk, jnp.zeros((128, 128), F32))

        @pl.when(h == 0)
        def _():
            dgt_ref[...] = jnp.zeros_like(dgt_ref)
            dsm_ref[...] = jnp.zeros_like(dsm_ref)

        def local_bwd(gi, carry):
            d_alog, d_dtb = carry
            rs = pl.ds(pl.multiple_of(gi * gr, gr), gr)
            cs = pl.ds(gi * grp, grp)
            r = _gdc_local(q_ref[rs, :], k_ref[rs, :], v_ref[rs, :], gt_ref[rs, :], a_log_h, dtb_h, h)
            row, lane = r["row"], r["lane"]
            q3, k3, v3, beta3, eg, kb, dm = r["q3"], r["k3"], r["v3"], r["beta3"], r["eg"], r["kb"], r["dm"]
            tm = tm_ref[0, 0, cs]
            u3 = _bmm(tm, v3 * beta3, "nn")
            w3 = _bmm(tm, kb * eg, "nn")
            state, dsp, do3 = st_ref[0, 0, cs], dsp_s[cs], dop_s[cs]
            vn = u3 - _bmm(w3, state, "nn")
            du = _bmm(r["attn"], do3, "tn") + _bmm(r["kdec"], dsp, "nn")
            dat = jnp.where(r["tril"], _bmm(do3, vn, "nt"), 0.0)
            dqg = _bmm(do3, state, "nt")
            dkd = _bmm(vn, dsp, "nt")
            dgl = jnp.sum(jnp.sum(state * dsp, axis=2, keepdims=True), axis=1, keepdims=True)
            dw = -_bmm(du, state, "nt")
            dvb = _bmm(tm, du, "tn")
            dkbg = _bmm(tm, dw, "tn")
            dl = -jnp.where(r["strict"], _bmm(dvb, u3, "nt") + _bmm(dkbg, w3, "nt"), 0.0)
            dml = dl * dm
            dn = dat * dm
            dkb = _bmm(dml, k3, "nn") + dkbg * eg
            dk3 = _bmm(dml, kb, "tn") + _bmm(dn, q3, "tn") + dkd * r["ekd"] + dkb * beta3
            dq3 = dqg * eg + _bmm(dn, k3, "nn")
            e = dl * r["lmat"] + dat * r["attn"]
            ones = jnp.ones((grp, CHUNK, 128), F32)
            colsum = lax.dot_general(e, ones, (_BDIMS["tn"], ((0,), (0,))), preferred_element_type=F32, precision=lax.Precision.HIGH)
            dgc = jnp.sum(e, axis=-1, keepdims=True) - colsum
            dgc = dgc + eg * (jnp.sum(dqg * q3, axis=-1, keepdims=True) + jnp.sum(dkbg * kb, axis=-1, keepdims=True))
            skd = jnp.sum(dkd * r["kdec"], axis=-1, keepdims=True)
            dgcl = jnp.sum(skd, axis=1, keepdims=True) + dgl * r["gl"]
            pos3 = lax.broadcasted_iota(jnp.int32, (grp, CHUNK, 128), 1)
            dgc = dgc - skd + jnp.where(pos3 == CHUNK - 1, dgcl, 0.0)
            dbeta = jnp.sum(dkb * k3, axis=-1, keepdims=True) + jnp.sum(dvb * v3, axis=-1, keepdims=True)
            dg = _seg_cumsum_rev(dgc.reshape(gr, 128), row)
            beta = r["beta"]
            dbl = jnp.broadcast_to(dbeta, (grp, CHUNK, 128)).reshape(gr, 128) * beta * (1.0 - beta)
            dai = dg * (-r["A"]) * _sigmoid(r["pre"])
            dgt_ref[rs, :] += jnp.where(lane == h, dbl, 0.0) + jnp.where(lane == 8 + h, dai, 0.0)
            dq_ref[rs, :] = dq3.reshape(gr, 128) * C_QSCALE
            dk_ref[rs, :] = dk3.reshape(gr, 128)
            dv_ref[rs, :] = (dvb * beta3).reshape(gr, 128)
            return (d_alog + jnp.sum(dg * (-r["sp"]), axis=0, keepdims=True) * r["A"],
                    d_dtb + jnp.sum(dai, axis=0, keepdims=True))

        zero = jnp.zeros((1, 128), F32)
        d_alog, d_dtb = lax.fori_loop(0, nc // grp, local_bwd, (zero, zero))
        r16 = lax.broadcasted_iota(jnp.int32, (16, 128), 0)
        l16 = lax.broadcasted_iota(jnp.int32, (16, 128), 1)
        small = jnp.where((r16 == h) & (l16 == 0), d_alog, 0.0) + jnp.where((r16 == h) & (l16 == 1), d_dtb, 0.0)
        dsm_ref[0] += small + jnp.where(r16 == 8 + h, dng, 0.0)

    blk = pl.BlockSpec((s_len, 128), lambda b, h: (b, h))
    blk3 = lambda off: pl.BlockSpec((s_len, 128), lambda b, h: (b, off + h))
    full = jax.ShapeDtypeStruct((t, C_WIDTH), F32)
    c128 = pltpu.VMEM((nc, CHUNK, 128), F32)
    sq = pltpu.VMEM((nc, 128, 128), F32)
    res = _call(
        body, (qkv, qkv, qkv, proj, gates, a_log, dtb, ng.reshape(1, 128), o_pre, tmat, states, dout), side,
        (bsz, C_HEADS), name=name,
        in_specs=_gdc_specs(s_len) + [blk, pl.BlockSpec((1, 1, nc, CHUNK, CHUNK), lambda b, h: (b, h, 0, 0, 0)),
                                      pl.BlockSpec((1, 1, nc, 128, 128), lambda b, h: (b, h, 0, 0, 0)), blk],
        out_specs=[blk, blk, blk, pl.BlockSpec((s_len, 128), lambda b, h: (b, 3 * C_HEADS + h)),
                   pl.BlockSpec((s_len, 128), lambda b, h: (b, 0)), pl.BlockSpec((1, 16, 128), lambda b, h: (b, 0, 0))],
        out_shape=[full, full, full, jax.ShapeDtypeStruct((t, 4 * C_WIDTH), MM), jax.ShapeDtypeStruct((t, 128), F32),
                   jax.ShapeDtypeStruct((bsz, 16, 128), F32)],
        scratch_shapes=[sq, sq, sq, pltpu.VMEM((nc, 1, 128), F32), c128],
        compiler_params=_cp("parallel", "arbitrary"),
    )
    (dq, dk, dv, dz, dgates, dsm), extra = res if side is not None else (res, None)
    out = ((dq, dk, dv), dz, dgates, dsm)
    return out if side is None else (out, extra)


def join_cols(x, *, name, outs=None, tk=256):
    _, kk, n = x.shape
    tk = _tile8(kk, tk)
    outs = outs or [(0, N_DEV * n, N_DEV * n)]

    def body(x_ref, *o_refs):
        full = jnp.concatenate([x_ref[k] for k in range(N_DEV)], axis=-1)
        for (lo, hi, wd), o_ref in zip(outs, o_refs):
            piece = full[:, lo:hi]
            if wd > hi - lo:
                piece = jnp.concatenate([piece, jnp.zeros((tk, wd - (hi - lo)), piece.dtype)], axis=-1)
            o_ref[...] = piece

    res = _pcall(
        body, name=name, grid=(kk // tk,), in_specs=[pl.BlockSpec((N_DEV, tk, n), lambda i: (0, i, 0))],
        out_specs=[pl.BlockSpec((tk, wd), lambda i: (i, 0)) for _, _, wd in outs],
        out_shape=[jax.ShapeDtypeStruct((kk, wd), x.dtype) for _, _, wd in outs], compiler_params=_cp("parallel"),
    )(x)
    return res if len(outs) > 1 else res[0]


def split_cols(pieces, n, *, name, tk=256):
    kk = pieces[0][0].shape[0]
    tk = _tile8(kk, tk)

    def body(*refs):
        o_ref = refs[-1]
        vals = [r[...][:, :used] for r, (_, used) in zip(refs[:-1], pieces)]
        full = vals[0] if len(vals) == 1 else jnp.concatenate(vals, axis=-1)
        for k in range(N_DEV):
            o_ref[k] = full[:, k * n:(k + 1) * n].astype(MM)

    return _pcall(
        body, name=name, grid=(kk // tk,),
        in_specs=[pl.BlockSpec((tk, arr.shape[1]), lambda i: (i, 0)) for arr, _ in pieces],
        out_specs=pl.BlockSpec((N_DEV, tk, n), lambda i: (0, i, 0)),
        out_shape=jax.ShapeDtypeStruct((N_DEV, kk, n), MM), compiler_params=_cp("parallel"),
    )(*[arr for arr, _ in pieces])


def gather_multi(shards, *, name):
    ng = len(shards)

    def body(*refs):
        x_refs, o_refs = refs[:ng], refs[ng:2 * ng]
        send_sems, recv_sems, local_sems = refs[2 * ng:]
        x, y, c = _me()
        sibling = (x, y, 1 - c)
        chips = [(1 - x, y), (x, 1 - y), (1 - x, 1 - y)]

        def slot(px, py, pc):
            return 4 * px + 2 * py + pc

        def copy(gi, k, block, to, src=None):
            dst = o_refs[gi].at[slot(*block)]
            return pltpu.make_async_remote_copy(
                src_ref=dst if src is None else src, dst_ref=dst, send_sem=send_sems.at[7 * gi + k],
                recv_sem=recv_sems.at[7 * gi + k], device_id=to, device_id_type=MESH_ID)

        own = [pltpu.make_async_copy(x_refs[gi], o_refs[gi].at[slot(x, y, c)], local_sems.at[gi]) for gi in range(ng)]
        for cp in own:
            cp.start()
        first = []
        for gi in range(ng):
            first.append(copy(gi, 0, (x, y, c), sibling, src=x_refs[gi]))
            first += [copy(gi, 1 + j, (x, y, c), (*chip, c), src=x_refs[gi]) for j, chip in enumerate(chips)]
        for cp in first:
            cp.start()
        passed = []
        for j, chip in enumerate(chips):
            for gi in range(ng):
                copy(gi, 1 + j, (*chip, c), (x, y, c)).wait_recv()
                fwd = copy(gi, 4 + j, (*chip, c), sibling)
                fwd.start()
                passed.append(fwd)
        for gi in range(ng):
            copy(gi, 0, sibling, (x, y, c)).wait_recv()
            for j, chip in enumerate(chips):
                copy(gi, 4 + j, (*chip, 1 - c), (x, y, c)).wait_recv()
        for cp in first + passed:
            cp.wait_send()
        for cp in own:
            cp.wait()

    hbm = pl.BlockSpec(memory_space=pl.ANY)
    return _pcall(
        body, name=name, in_specs=[hbm] * ng, out_specs=[hbm] * ng,
        out_shape=[jax.ShapeDtypeStruct((N_DEV,) + s.shape, s.dtype) for s in shards],
        scratch_shapes=[pltpu.SemaphoreType.DMA((7 * ng,)), pltpu.SemaphoreType.DMA((7 * ng,)),
                        pltpu.SemaphoreType.DMA((ng,))],
    )(*shards)


def exchange_multi(parts, *, name):
    ng = len(parts)

    def body(*refs):
        x_refs, o_refs = refs[:ng], refs[ng:2 * ng]
        send_sems, recv_sems, local_sems = refs[2 * ng:]
        x, y, c = _me()
        mine = 4 * x + 2 * y + c
        own = [pltpu.make_async_copy(x_refs[gi].at[mine], o_refs[gi].at[mine], local_sems.at[gi]) for gi in range(ng)]
        for cp in own:
            cp.start()
        copies = []
        for k, (dx, dy, dc) in enumerate(_FLIPS):
            px, py, pc = _flip(x, dx), _flip(y, dy), _flip(c, dc)
            for gi in range(ng):
                cp = pltpu.make_async_remote_copy(
                    src_ref=x_refs[gi].at[4 * px + 2 * py + pc], dst_ref=o_refs[gi].at[mine],
                    send_sem=send_sems.at[7 * gi + k], recv_sem=recv_sems.at[7 * gi + k], device_id=(px, py, pc),
                    device_id_type=MESH_ID)
                cp.start()
                copies.append(cp)
        for cp in copies:
            cp.wait()
        for cp in own:
            cp.wait()

    hbm = pl.BlockSpec(memory_space=pl.ANY)
    return _pcall(
        body, name=name, in_specs=[hbm] * ng, out_specs=[hbm] * ng,
        out_shape=[jax.ShapeDtypeStruct(s.shape, s.dtype) for s in parts],
        scratch_shapes=[pltpu.SemaphoreType.DMA((7 * ng,)), pltpu.SemaphoreType.DMA((7 * ng,)),
                        pltpu.SemaphoreType.DMA((ng,))],
    )(*parts)


def adamw_rows(parts, row0, w, m, v, *, name, tr=256):
    r, cdim = w.shape
    tr = _tile8(math.gcd(r, row0) if row0 else r, tr)
    blk0 = row0 // tr

    def body(p_ref, w_ref, m_ref, v_ref, g_ref, d_ref, mo_ref, vo_ref):
        g = p_ref[0].astype(F32)
        for j in range(1, N_DEV):
            g = g + p_ref[j].astype(F32)
        g_ref[...] = g
        mn = ADAM_B1 * m_ref[...] + (1.0 - ADAM_B1) * g
        vn = ADAM_B2 * v_ref[...] + (1.0 - ADAM_B2) * (g * g)
        mo_ref[...] = mn
        vo_ref[...] = vn
        m_hat = mn / (1.0 - ADAM_B1 ** ADAM_STEP)
        v_hat = vn / (1.0 - ADAM_B2 ** ADAM_STEP)
        d_ref[...] = -ADAM_LR * (m_hat / (jnp.sqrt(v_hat) + ADAM_EPS) + ADAM_WD * w_ref[...])

    blk = pl.BlockSpec((tr, cdim), lambda i: (i, 0))
    shp = jax.ShapeDtypeStruct((r, cdim), F32)
    return _pcall(
        body, name=name, grid=(r // tr,),
        in_specs=[pl.BlockSpec((N_DEV, tr, cdim), lambda i: (0, blk0 + i, 0)), blk, blk, blk],
        out_specs=[blk, blk, blk, blk], out_shape=[shp, shp, shp, shp], compiler_params=_cp("parallel"),
    )(parts, w, m, v)


def _tile8(n, pref):
    for c in range(min(pref, n) - min(pref, n) % 16, 0, -16):
        if n % c == 0:
            return c
    return n


REPL = ["ple_bg", "a_sinks", "b_conv_b", "b_wa", "b_ba", "b_wx", "b_bx", "b_lam", "c_a_log", "c_dt_bias", "c_norm_g"]
WEIGHTS = ["ffn1_wg", "ffn1_wu", "ffn1_wd", "ffn2_wg", "ffn2_wu", "ffn2_wd", "ln_g", "ln_b", "ple_wg", "ple_bg", "ple_wp",
           "ab_w_in", "a_sinks", "b_conv_w", "b_conv_b", "b_wa", "b_ba", "b_wx", "b_bx", "b_lam", "ab_w_out", "c_w_in",
           "c_conv_w", "c_a_log", "c_dt_bias", "c_norm_g", "c_w_out"]
SMALL_NAMES = ["ln_g", "ln_b", "b_conv_w", "c_conv_w"]
SMALL_F32 = 73728
LANES = 128
FFN_TM = 512
FFN_TF = 1408


def _join(blocks, axis):
    moved = jnp.moveaxis(blocks, 0, axis)
    shp = list(moved.shape)
    return moved.reshape(shp[:axis] + [shp[axis] * shp[axis + 1]] + shp[axis + 2:])


def _split(full, axis):
    shp = list(full.shape)
    return jnp.moveaxis(full.reshape(shp[:axis] + [N_DEV, shp[axis] // N_DEV] + shp[axis + 1:]), axis, 0)


def _dense_blocks(w):
    z = jnp.zeros((4, 2, 64, 2, 64), w.dtype)
    w4 = w.reshape(4, 2, 64, 64)
    z = z.at[:, 0, :, 0, :].set(w4[:, 0]).at[:, 1, :, 1, :].set(w4[:, 1])
    return z.reshape(4, 128, 128)


def _diag_blocks(d):
    d5 = d.reshape(4, 2, 64, 2, 64)
    return jnp.stack([d5[:, 0, :, 0, :], d5[:, 1, :, 1, :]], axis=1).reshape(8, 64, 64)


def _flat_pad(arrs, dtype, total):
    flat = jnp.concatenate([z.astype(dtype).reshape(-1) for z in arrs])
    return jnp.pad(flat, (0, total - flat.shape[0]))


def _flat8_pad(arrs, dtype, total):
    flat = jnp.concatenate([z.astype(dtype).reshape(N_DEV, -1) for z in arrs], axis=1)
    return jnp.pad(flat, ((0, 0), (0, total - flat.shape[1])))


def _take(flat, names, shapes):
    out, off = {}, 0
    for n in names:
        sz = int(np.prod(shapes[n]))
        out[n] = flat[..., off:off + sz].reshape(flat.shape[:-1] + tuple(shapes[n]))
        off += sz
    return out


def kernel(x, p, ffn1_wg, ffn1_wu, ffn1_wd, ffn2_wg, ffn2_wu, ffn2_wd, ln_g, ln_b, ple_wg, ple_bg, ple_wp, ab_w_in, a_sinks, b_conv_w, b_conv_b, b_wa, b_ba, b_wx, b_bx, b_lam, ab_w_out, c_w_in, c_conv_w, c_a_log, c_dt_bias, c_norm_g, c_w_out, loss_target, m_ffn1_wg, m_ffn1_wu, m_ffn1_wd, m_ffn2_wg, m_ffn2_wu, m_ffn2_wd, m_ln_g, m_ln_b, m_ple_wg, m_ple_bg, m_ple_wp, m_ab_w_in, m_a_sinks, m_b_conv_w, m_b_conv_b, m_b_wa, m_b_ba, m_b_wx, m_b_bx, m_b_lam, m_ab_w_out, m_c_w_in, m_c_conv_w, m_c_a_log, m_c_dt_bias, m_c_norm_g, m_c_w_out, v_ffn1_wg, v_ffn1_wu, v_ffn1_wd, v_ffn2_wg, v_ffn2_wu, v_ffn2_wd, v_ln_g, v_ln_b, v_ple_wg, v_ple_bg, v_ple_wp, v_ab_w_in, v_a_sinks, v_b_conv_w, v_b_conv_b, v_b_wa, v_b_ba, v_b_wx, v_b_bx, v_b_lam, v_ab_w_out, v_c_w_in, v_c_conv_w, v_c_a_log, v_c_dt_bias, v_c_norm_g, v_c_w_out):
    a = dict(locals())
    return _step3(a)


def _step3(a):
    x, p = a["x"], a["p"]
    bsz, s_len, d = x.shape
    t = bsz * s_len
    x2 = x.reshape(t, d)
    tgt = a["loss_target"].reshape(t, d)
    p2 = p.reshape(DEPTH, t, D_PLE)
    shapes = {n: a[n].shape for n in WEIGHTS}
    n_small = sum(int(np.prod(shapes[n])) for n in SMALL_NAMES)
    small_all = SMALL_NAMES + REPL
    f_ff = shapes["ffn1_wg"][2]
    c_cols = shapes["c_w_in"][2]
    wide = dict(tm=1024, tn=1408, tk=1024)
    tall = dict(tm=1408, tn=1024, tk=1024)

    def cast(z):
        return z.astype(MM)

    def ffn_shards(which, l):
        return [cast(a[which + "_wg"][l]), cast(a[which + "_wu"][l]), cast(a[which + "_wd"][l])]

    def ffn_weights(gat, tag):
        return (join_cols(gat[0], name=f"join_{tag}_wg"), join_cols(gat[1], name=f"join_{tag}_wu"),
                gat[2].reshape(N_DEV * gat[2].shape[1], D_MODEL))

    def rows_full(gat):
        return gat.reshape(N_DEV * gat.shape[1], D_MODEL)

    small_send = _flat_pad([a[n] for n in SMALL_NAMES], F32, 32 * LANES).reshape(32, LANES)
    g0 = gather_multi(ffn_shards("ffn1", 0) + [small_send], name="gather_first")
    ws = _take(g0[3].reshape(N_DEV, -1), SMALL_NAMES, shapes)
    small = {n: _join(ws[n], 2) for n in SMALL_NAMES}
    ln_g, ln_b = small["ln_g"], small["ln_b"]
    wa_d, wx_d = _dense_blocks(a["b_wa"][0]), _dense_blocks(a["b_wx"][0])
    lru_w = (small["b_conv_w"][0], a["b_conv_b"][0], wa_d, a["b_ba"][0], wx_d, a["b_bx"][0], a["b_lam"][0])
    gdc_w = (a["c_a_log"][0], a["c_dt_bias"][0], a["c_norm_g"][0])
    wf = {("ffn1", 0): ffn_weights(g0[:3], "ffn1_0")}

    s0 = {"x0": x2}
    u1 = ffn_shards("ffn2", 0)
    side = ("gather", u1[:2] + [cast(a["ab_w_in"][0]), cast(a["ab_w_out"][0])])
    (s0["y1"], s0["z1"], s0["hg1"], s0["hu1"]), got = ffn_fwd(x2, *wf["ffn1", 0], ln_g[0, 0], ln_b[0, 0],
                                                             name="ffn1_fwd_0", tm=FFN_TM, tf=FFN_TF, side=side)
    ab_w_in, ab_w_out = join_cols(got[2], name="join_ab_in"), rows_full(got[3])
    s0["proj"] = matmul(s0["y1"], ab_w_in, mode="nn", out_dtype=MM, name="ab_in_fwd", tn=896, tk=1024)
    ya, got_ple = attn_fwd(s0["proj"], a["a_sinks"][0], bsz, name="attn_fwd",
                           side=("gather", [u1[2], cast(a["ple_wg"][0]), cast(a["ple_wp"][0])]))
    wf["ffn2", 0] = ffn_weights(got[:2] + got_ple[:1], "ffn2_0")
    got_ple = got_ple[1:]
    yb = lru_fwd(s0["proj"], *lru_w, bsz, name="lru_fwd")
    s0["mix"] = jnp.concatenate([ya, yb], axis=1)
    s0["y2"], s0["z2"] = mm_ln_fwd(s0["mix"], ab_w_out, s0["y1"], ln_g[0, 1], ln_b[0, 1], name="mix_out_fwd_0")
    side = ("gather", ffn_shards("ffn1", 1))
    (s0["y3"], s0["z3"], s0["hg2"], s0["hu2"]), got = ffn_fwd(s0["y2"], *wf["ffn2", 0], ln_g[0, 2], ln_b[0, 2],
                                                             name="ffn2_fwd_0", tm=FFN_TM, tf=FFN_TF, side=side)
    wf["ffn1", 1] = ffn_weights(got, "ffn1_1")
    ple_wg = [rows_full(got_ple[0]), None]
    ple_wp = [_join(got_ple[1], 1), None]
    h1 = ple_fwd(s0["y3"], p2[0], ple_wg[0], a["ple_bg"][0], ple_wp[0], name="ple_fwd_0")

    s1 = {"x0": h1}
    side = ("gather", [cast(a["c_w_in"][0]), cast(a["c_w_out"][0])])
    (s1["y1"], s1["z1"], s1["hg1"], s1["hu1"]), got = ffn_fwd(h1, *wf["ffn1", 1], ln_g[1, 0], ln_b[1, 0],
                                                             name="ffn1_fwd_1", tm=FFN_TM, tf=FFN_TF, side=side)
    c_in_main, c_in_gate = join_cols(got[0], name="join_c_in", outs=[(0, 4 * C_WIDTH, 4 * C_WIDTH),
                                                                      (4 * C_WIDTH, 4 * C_WIDTH + 2 * C_HEADS, LANES)])
    c_w_out = rows_full(got[1])
    s1["proj"] = matmul(s1["y1"], c_in_main, mode="nn", out_dtype=MM, name="c_in_fwd", tm=1024, tn=2048, tk=1024)
    s1["gates"] = matmul(s1["y1"], c_in_gate, mode="nn", name="c_gate_fwd", tk=1024)
    s1["qkv"] = gdc_pre_fwd(s1["proj"], small["c_conv_w"][0], bsz, name="gdc_pre_fwd")
    side = ("gather", ffn_shards("ffn2", 1) + [cast(a["ple_wg"][1]), cast(a["ple_wp"][1])])
    (s1["mix"], s1["o_pre"], s1["tmat"], s1["states"]), got = gdc_fwd(
        s1["qkv"], s1["proj"], s1["gates"], *gdc_w, bsz, name="gdc_fwd", side=side)
    wf["ffn2", 1] = ffn_weights(got[:3], "ffn2_1")
    ple_wg[1], ple_wp[1] = rows_full(got[3]), _join(got[4], 1)
    s1["y2"], s1["z2"] = mm_ln_fwd(s1["mix"], c_w_out, s1["y1"], ln_g[1, 1], ln_b[1, 1], name="mix_out_fwd_1")
    s1["y3"], s1["z3"], s1["hg2"], s1["hu2"] = ffn_fwd(s1["y2"], *wf["ffn2", 1], ln_g[1, 2], ln_b[1, 2], name="ffn2_fwd_1",
                                                           tm=FFN_TM, tf=FFN_TF)
    h2 = ple_fwd(s1["y3"], p2[1], ple_wg[1], a["ple_bg"][1], ple_wp[1], name="ple_fwd_1")
    loss_part, dh = loss_fwd_bwd(h2, tgt, name="loss")

    def ffn_parts(xin, act, dhg, dhu, dz, tag):
        dwg = matmul(xin, dhg, mode="tn", out_dtype=MM, split_n=f_ff, name=f"{tag}_wg_grad", **wide)
        dwu = matmul(xin, dhu, mode="tn", out_dtype=MM, split_n=f_ff, name=f"{tag}_wu_grad", **wide)
        dwd = matmul(act, dz, mode="tn", scale=0.5, out_dtype=MM, name=f"{tag}_wd_grad", **tall)
        return [dwg, dwu, dwd.reshape(N_DEV, f_ff, D_MODEL)]

    def ple_parts(i, s, dt, de):
        gwg = matmul(s["y3"], dt, mode="tn", out_dtype=MM, name=f"ple_wg_grad_{i}", tm=1024, tn=1024)
        gwp = matmul(p2[i], de, mode="tn", out_dtype=MM, name=f"ple_wp_grad_{i}", tn=1024)
        return [gwg.reshape(N_DEV, D_MODEL // N_DEV, D_MODEL), _split(gwp, 1)]

    gln = {"ln_g": [None, None], "ln_b": [None, None]}
    gple_bg = [None, None]

    dz3, dt, de, dbg, dg2, db2 = ple_bwd(dh, s1["y3"], p2[1], ple_wg[1], a["ple_bg"][1], ple_wp[1], s1["z3"], ln_g[1, 2],
                                         name="ple_bwd_1")
    gple_bg[1] = dbg[0]
    parts_ple1 = ple_parts(1, s1, dt, de)
    dy2, act, dhg, dhu = ffn_bwd(dz3, s1["hg2"], s1["hu2"], *wf["ffn2", 1], name="ffn2_bwd_1", tm=FFN_TM, tf=FFN_TF)
    parts_ffn2_1 = ffn_parts(s1["y2"], act, dhg, dhu, dz3, "ffn2_1")
    dz2, dg1, db1 = ln_bwd(dy2, s1["z2"], ln_g[1, 1], name="ln1_bwd_1")
    dmix = matmul(dz2, c_w_out, mode="nt", name="c_out_bwd", tn=1024, tk=1024)
    parts_c_out = matmul(s1["mix"], dz2, mode="tn", out_dtype=MM, name="c_out_grad", tm=1024, tn=1024).reshape(
        N_DEV, D_MODEL // N_DEV, D_MODEL)
    (dqkv, dzc, dgates, dsm), recv1 = gdc_bwd(s1["qkv"], s1["proj"], s1["gates"], *gdc_w, s1["o_pre"],
                                              s1["tmat"], s1["states"], dmix, bsz, name="gdc_bwd",
                                              side=("exchange", parts_ffn2_1 + parts_ple1))
    dproj, dccw = gdc_pre_bwd(s1["proj"], small["c_conv_w"][0], dqkv, dzc, bsz, name="gdc_pre_bwd")
    dgb = dgates.astype(MM)
    dy1 = matmul(dproj, c_in_main, mode="nt", add=dz2, add_scale=DN_ALPHA, name="c_in_bwd", tn=1024, tk=4096)
    dz1, dg0, db0 = matmul(dgb, c_in_gate, mode="nt", add=dy1, name="c_gate_bwd", tn=1024, ln=(s1["z1"], ln_g[1, 0]))
    g_c_main = matmul(s1["y1"], dproj, mode="tn", name="c_in_grad", tm=1024, tn=1024, tk=1024)
    g_c_gate = matmul(s1["y1"], dgb, mode="tn", name="c_gate_grad", tm=1024)
    parts_c_in = split_cols([(g_c_main, 4 * C_WIDTH), (g_c_gate, 2 * C_HEADS)], c_cols, name="split_c_in")
    (dh, act, dhg, dhu), recv_c = ffn_bwd(dz1, s1["hg1"], s1["hu1"], *wf["ffn1", 1], name="ffn1_bwd_1", tm=FFN_TM,
                                          tf=FFN_TF, side=("exchange", [parts_c_in, parts_c_out]))
    parts_ffn1_1 = ffn_parts(s1["x0"], act, dhg, dhu, dz1, "ffn1_1")
    gln["ln_g"][1] = jnp.concatenate([dg0, dg1, dg2], axis=0)
    gln["ln_b"][1] = jnp.concatenate([db0, db1, db2], axis=0)

    dz3, dt, de, dbg, dg2, db2 = ple_bwd(dh, s0["y3"], p2[0], ple_wg[0], a["ple_bg"][0], ple_wp[0], s0["z3"], ln_g[0, 2],
                                         name="ple_bwd_0")
    gple_bg[0] = dbg[0]
    parts_ple0 = ple_parts(0, s0, dt, de)
    (dy2, act, dhg, dhu), recv2 = ffn_bwd(dz3, s0["hg2"], s0["hu2"], *wf["ffn2", 0], name="ffn2_bwd_0", tm=FFN_TM,
                                          tf=FFN_TF, side=("exchange", parts_ffn1_1))
    parts_ffn2_0 = ffn_parts(s0["y2"], act, dhg, dhu, dz3, "ffn2_0")
    dz2, dg1, db1 = ln_bwd(dy2, s0["z2"], ln_g[0, 1], name="ln1_bwd_0")
    dmix = matmul(dz2, ab_w_out, mode="nt", name="ab_out_bwd", tn=1024, tk=1024)
    parts_ab_out = matmul(s0["mix"], dz2, mode="tn", out_dtype=MM, name="ab_out_grad", tm=1024, tn=1024).reshape(
        N_DEV, D_MODEL // N_DEV, D_MODEL)
    (dq, dk, dv, dsk), recv3a = attn_bwd(s0["proj"], a["a_sinks"][0], dmix, bsz, name="attn_bwd",
                                         side=("exchange", parts_ffn2_0[:2]))
    (dbx, dbgate, dcw, dcb, dwa, dba, dwx, dbxb, dlam), recv3b = lru_bwd(
        s0["proj"], *lru_w, dmix, bsz, name="lru_bwd", side=("exchange", [parts_ffn2_0[2]] + parts_ple0 + [parts_ab_out]))
    dproj = jnp.concatenate([dq, dk, dv, dbx, dbgate], axis=1).astype(MM)
    dz1, dg0, db0 = matmul(dproj, ab_w_in, mode="nt", add=dz2, add_scale=DN_ALPHA, name="ab_in_bwd", tn=1024, tk=1792,
                           ln=(s0["z1"], ln_g[0, 0]))
    parts_ab_in = matmul(s0["y1"], dproj, mode="tn", out_dtype=MM, split_n=AB_PROJ // N_DEV, name="ab_in_grad",
                         tm=1024, tn=896)
    gln["ln_g"][0] = jnp.concatenate([dg0, dg1, dg2], axis=0)
    gln["ln_b"][0] = jnp.concatenate([db0, db1, db2], axis=0)

    dsm_sum = jnp.sum(dsm, axis=0)
    full = dict(ln_g=jnp.stack(gln["ln_g"]), ln_b=jnp.stack(gln["ln_b"]), b_conv_w=dcw[None],
                c_conv_w=jnp.sum(dccw, axis=0)[None], ple_bg=jnp.stack(gple_bg),
                a_sinks=jnp.sum(dsk, axis=0)[:, :A_HEADS], b_conv_b=dcb, b_wa=_diag_blocks(dwa)[None], b_ba=dba,
                b_wx=_diag_blocks(dwx)[None], b_bx=dbxb, b_lam=dlam, c_a_log=dsm_sum[None, :C_HEADS, 0],
                c_dt_bias=dsm_sum[None, :C_HEADS, 1], c_norm_g=jnp.sum(dsm_sum[C_HEADS:], axis=0)[None])
    small_rows = SMALL_F32 // LANES
    repl_flat = _flat_pad([full[n] for n in REPL], F32, SMALL_F32 - n_small)
    small8 = jnp.concatenate([_flat8_pad([_split(full[n], 2) for n in SMALL_NAMES], F32, n_small),
                              jnp.broadcast_to(repl_flat, (N_DEV,) + repl_flat.shape)], axis=1)
    (dh, act, dhg, dhu), recv3c = ffn_bwd(dz1, s0["hg1"], s0["hu1"], *wf["ffn1", 0], name="ffn1_bwd_0", tm=FFN_TM,
                                          tf=FFN_TF, side=("exchange", [parts_ab_in, small8.reshape(N_DEV, small_rows, LANES)]))
    grad_x = dh.reshape(bsz, s_len, d)

    parts_wg = matmul(s0["x0"], dhg, mode="tn", out_dtype=MM, split_n=f_ff, name="ffn1_0_wg_grad", **wide)
    parts_wu, recv4a = matmul(s0["x0"], dhu, mode="tn", out_dtype=MM, split_n=f_ff, name="ffn1_0_wu_grad",
                              side=("exchange", [parts_wg]), **wide)
    dwd, recv4b = matmul(act, dz1, mode="tn", scale=0.5, out_dtype=MM, name="ffn1_0_wd_grad",
                         side=("exchange", [parts_wu]), **tall)
    recv4c = exchange_multi([dwd.reshape(N_DEV, f_ff, D_MODEL)], name="exchange_last")

    def upd(parts, n, l, shape2d, **kw):
        wmv = [a[pre + n][l].reshape(shape2d) for pre in ("", "m_", "v_")]
        return adamw_rows(parts, 0, *wmv, name=f"adamw_{n}_{l}", **kw)

    def upd_ffn(parts, which, l):
        return {(which + "_wg", l): upd(parts[0], which + "_wg", l, (D_MODEL, f_ff)),
                (which + "_wu", l): upd(parts[1], which + "_wu", l, (D_MODEL, f_ff)),
                (which + "_wd", l): upd(parts[2], which + "_wd", l, (f_ff, D_MODEL), tr=176)}

    rows8 = D_MODEL // N_DEV
    res = {}
    res.update(upd_ffn(recv1[:3], "ffn2", 1))
    res["ple_wg", 1] = upd(recv1[3], "ple_wg", 1, (rows8, D_MODEL), tr=128)
    res["ple_wp", 1] = upd(recv1[4], "ple_wp", 1, (D_PLE, LANES))
    res.update(upd_ffn(recv2, "ffn1", 1))
    res["c_w_in", 0] = upd(recv_c[0], "c_w_in", 0, (D_MODEL, c_cols))
    res["c_w_out", 0] = upd(recv_c[1], "c_w_out", 0, (rows8, D_MODEL), tr=128)
    res.update(upd_ffn(recv3a + recv3b[:1], "ffn2", 0))
    res["ple_wg", 0] = upd(recv3b[1], "ple_wg", 0, (rows8, D_MODEL), tr=128)
    res["ple_wp", 0] = upd(recv3b[2], "ple_wp", 0, (D_PLE, LANES))
    res["ab_w_out", 0] = upd(recv3b[3], "ab_w_out", 0, (rows8, D_MODEL), tr=128)
    res.update(upd_ffn(recv4a + recv4b + list(recv4c), "ffn1", 0))
    res["ab_w_in", 0] = upd(recv3c[0], "ab_w_in", 0, (D_MODEL, AB_PROJ // N_DEV))
    res_small = adamw_rows(recv3c[1], 0, *[_flat_pad([a[pre + n] for n in small_all], F32, SMALL_F32).reshape(
        small_rows, LANES) for pre in ("", "m_", "v_")], name="adamw_small", tr=small_rows)
    kinds = []
    for k in range(4):
        kd = _take(res_small[k].reshape(-1), small_all, shapes)
        for n in WEIGHTS:
            if n not in kd:
                kd[n] = jnp.stack([res[n, l][k] for l in range(shapes[n][0])]).reshape(shapes[n])
        kinds.append(kd)
    loss = lax.psum(loss_part[0, 0], ("x", "y", "c"))
    return (loss, grad_x, *[kinds[0][n] for n in WEIGHTS], *[kinds[1][n] for n in WEIGHTS],
            *[kinds[2][n] for n in WEIGHTS], *[kinds[3][n] for n in WEIGHTS])
```
